```python
import jax, jax.numpy as jnp
from jax import lax
import numpy as np

D_MODEL = 1024
BATCH = 1
SEQ = 16384
DEPTH = 1
DEC_BATCH = 2
DEC_SEQ = 16384
PAST_LEN = 128

HEAD_DIM = 64
N_HEADS = 12
N_KV_HEADS = 4
GQA_GROUP = N_HEADS // N_KV_HEADS
ATTN_WIDTH = N_HEADS * HEAD_DIM
KV_WIDTH = N_KV_HEADS * HEAD_DIM
N_FOURIER_GROUPS = 4
FOURIER_GROUP_DIM = 64
FOURIER_WIDTH = N_FOURIER_GROUPS * FOURIER_GROUP_DIM
MIX_WIDTH = FOURIER_WIDTH + ATTN_WIDTH
IN_WIDTH = FOURIER_WIDTH + ATTN_WIDTH + 2 * KV_WIDTH
WINDOW = 128
BLOCK = 128
ROPE_THETA = 10000.0
N_EXPERTS = 16
EC_CAPACITY_FACTOR = 2
EXPERT_D_FF = 2048
EPS = 1e-6
NEG = -1e30

kernel_name = "hybrid_fnet_swa_ec_encoder"


def rms_norm(x, g):
    xf = x.astype(jnp.float32)
    y = xf * lax.rsqrt(jnp.mean(xf * xf, axis=-1, keepdims=True) + EPS)
    return (y * g.astype(jnp.float32)).astype(x.dtype)


def apply_rope(t, s):
    half = HEAD_DIM // 2
    inv_freq = 1.0 / (ROPE_THETA ** (jnp.arange(half, dtype=jnp.float32) / half))
    ang = jnp.arange(s, dtype=jnp.float32)[:, None] * inv_freq[None, :]
    cos = jnp.cos(ang)[None, :, None, :]
    sin = jnp.sin(ang)[None, :, None, :]
    tf = t.astype(jnp.float32)
    t1, t2 = tf[..., :half], tf[..., half:]
    out = jnp.concatenate([t1 * cos - t2 * sin, t2 * cos + t1 * sin], axis=-1)
    return out.astype(t.dtype)


def fourier_mixer(u, w_fourier):
    b, s, _ = u.shape
    uf = u.astype(jnp.float32).reshape(b, s, N_FOURIER_GROUPS, FOURIER_GROUP_DIM)
    z = jnp.fft.fft2(uf, axes=(1, 3), norm="ortho").real
    z = z.reshape(b, s, FOURIER_WIDTH).astype(u.dtype)
    return jnp.matmul(z, w_fourier)


def banded_window_attention(q, k, v, sink):
    b, s, _, _ = q.shape
    nb = s // BLOCK
    scale = HEAD_DIM ** -0.5
    qb = q.reshape(b, nb, BLOCK, N_KV_HEADS, GQA_GROUP, HEAD_DIM)

    def band(t):
        tp = jnp.pad(t, ((0, 0), (BLOCK, BLOCK), (0, 0), (0, 0)))
        tp = tp.reshape(b, nb + 2, BLOCK, N_KV_HEADS, HEAD_DIM)
        return jnp.concatenate([tp[:, :-2], tp[:, 1:-1], tp[:, 2:]], axis=2)

    kb, vb = band(k), band(v)
    scores = jnp.einsum('bnqhgd,bnjhd->bnhgqj', qb, kb,
                        preferred_element_type=jnp.float32) * scale
    a = jnp.arange(BLOCK)[:, None]
    j = jnp.arange(3 * BLOCK)[None, :]
    rel = j - BLOCK - a
    kpos = (jnp.arange(nb)[:, None, None] - 1) * BLOCK + j[None]
    valid = (jnp.abs(rel) <= WINDOW)[None] & (kpos >= 0) & (kpos < s)
    scores = jnp.where(valid[None, :, None, None], scores, NEG)
    sink_f = sink.astype(jnp.float32).reshape(N_KV_HEADS, GQA_GROUP)[None, None, :, :, None, None]
    m = jnp.maximum(jnp.max(scores, axis=-1, keepdims=True), sink_f)
    p = jnp.exp(scores - m)
    p = p / (jnp.sum(p, axis=-1, keepdims=True) + jnp.exp(sink_f - m))
    out = jnp.einsum('bnhgqj,bnjhd->bnqhgd', p, vb.astype(jnp.float32))
    return out.reshape(b, s, ATTN_WIDTH).astype(q.dtype)


def token_mixer(hn, w_in, w_fourier, sink, w_out):
    b, s, _ = hn.shape
    proj = jnp.matmul(hn, w_in)
    o1 = FOURIER_WIDTH
    o2 = o1 + ATTN_WIDTH
    o3 = o2 + KV_WIDTH
    u_f = proj[..., :o1]
    q = proj[..., o1:o2].reshape(b, s, N_HEADS, HEAD_DIM)
    k = proj[..., o2:o3].reshape(b, s, N_KV_HEADS, HEAD_DIM)
    v = proj[..., o3:].reshape(b, s, N_KV_HEADS, HEAD_DIM)
    q = apply_rope(q, s)
    k = apply_rope(k, s)
    y_f = fourier_mixer(u_f, w_fourier)
    y_a = banded_window_attention(q, k, v, sink)
    return jnp.matmul(jnp.concatenate([y_f, y_a], axis=-1), w_out)


def expert_choice_ffn(x, w_router, w_gate, w_up, w_down):
    b, s, d = x.shape
    n = b * s
    t = x.reshape(n, d)
    cap = EC_CAPACITY_FACTOR * n // N_EXPERTS
    logits = jnp.matmul(t, w_router, preferred_element_type=jnp.float32)
    aff = jax.nn.softmax(logits, axis=-1)
    gate_vals, idx = lax.top_k(aff.T, cap)
    xe = t[idx]
    h = jax.nn.silu(jnp.einsum('ecd,edf->ecf', xe, w_gate)) * jnp.einsum('ecd,edf->ecf', xe, w_up)
    ye = jnp.einsum('ecf,efd->ecd', h, w_down) * gate_vals[..., None].astype(x.dtype)
    out = jnp.zeros_like(t).at[idx.reshape(-1)].add(ye.reshape(-1, d))
    return out.reshape(b, s, d)


def run_trunk(x, norm_mix_g, w_in, w_fourier, sink, w_out, norm_ffn_g,
              w_router, w_gate, w_up, w_down, norm_final_g):
    for l in range(DEPTH):
        x = x + token_mixer(rms_norm(x, norm_mix_g[l]), w_in[l], w_fourier[l], sink[l], w_out[l])
        x = x + expert_choice_ffn(rms_norm(x, norm_ffn_g[l]), w_router[l], w_gate[l], w_up[l], w_down[l])
    return rms_norm(x, norm_final_g)


def setup_inputs(seed: int = 0) -> dict:
    key = jax.random.key(seed)
    ks = jax.random.split(key, 14)
    f32 = jnp.float32
    nrm = lambda k, shape, scale: jax.random.normal(k, shape, f32) * scale
    return {
        "x_prompt": nrm(ks[0], (BATCH, SEQ, D_MODEL), 1.0),
        "x_sample": nrm(ks[1], (DEC_BATCH, DEC_SEQ, D_MODEL), 1.0),
        "norm_mix_g": 1.0 + nrm(ks[2], (DEPTH, D_MODEL), 0.02),
        "w_in": nrm(ks[3], (DEPTH, D_MODEL, IN_WIDTH), D_MODEL ** -0.5),
        "w_fourier": nrm(ks[4], (DEPTH, FOURIER_WIDTH, FOURIER_WIDTH), FOURIER_WIDTH ** -0.5),
        "sink": nrm(ks[5], (DEPTH, N_HEADS), 0.5),
        "w_out": nrm(ks[6], (DEPTH, MIX_WIDTH, D_MODEL), MIX_WIDTH ** -0.5),
        "norm_ffn_g": 1.0 + nrm(ks[7], (DEPTH, D_MODEL), 0.02),
        "w_router": nrm(ks[8], (DEPTH, D_MODEL, N_EXPERTS), D_MODEL ** -0.5),
        "w_gate": nrm(ks[9], (DEPTH, N_EXPERTS, D_MODEL, EXPERT_D_FF), D_MODEL ** -0.5),
        "w_up": nrm(ks[10], (DEPTH, N_EXPERTS, D_MODEL, EXPERT_D_FF), D_MODEL ** -0.5),
        "w_down": nrm(ks[11], (DEPTH, N_EXPERTS, EXPERT_D_FF, D_MODEL), EXPERT_D_FF ** -0.5),
        "norm_final_g": 1.0 + nrm(ks[12], (D_MODEL,), 0.02),
    }


def reference(x_prompt, x_sample, norm_mix_g, w_in, w_fourier, sink, w_out, norm_ffn_g,
              w_router, w_gate, w_up, w_down, norm_final_g):
    y_prompt = run_trunk(x_prompt, norm_mix_g, w_in, w_fourier, sink, w_out, norm_ffn_g,
                         w_router, w_gate, w_up, w_down, norm_final_g)
    y_sample = run_trunk(x_sample, norm_mix_g, w_in, w_fourier, sink, w_out, norm_ffn_g,
                         w_router, w_gate, w_up, w_down, norm_final_g)
    return (y_prompt, y_sample)
```

```python
import functools

import numpy as np
import jax
import jax.numpy as jnp
from jax import lax
from jax.experimental import pallas as pl
from jax.experimental.pallas import tpu as pltpu

F32 = jnp.float32
BF16 = jnp.bfloat16

D_MODEL = 1024
SEQ = 16384
N_SEQ = 3
N_TOK = N_SEQ * SEQ
HEAD_DIM = 64
N_HEADS = 12
N_KV_HEADS = 4
GQA_GROUP = N_HEADS // N_KV_HEADS
ATTN_WIDTH = N_HEADS * HEAD_DIM
KV_WIDTH = N_KV_HEADS * HEAD_DIM
FOURIER_WIDTH = 256
FOURIER_GROUP_DIM = 64
IN_WIDTH = FOURIER_WIDTH + ATTN_WIDTH + 2 * KV_WIDTH
WINDOW = 128
ROPE_THETA = 10000.0
N_EXPERTS = 16
EC_CAPACITY_FACTOR = 2
EXPERT_D_FF = 2048
EPS = 1e-6
NEG = -1e30

LANES = 128
FFT_R = 128
VMEM_LIMIT = 48 * 1024 * 1024

TM = 512
TQ = 512
QB = 128
FF_TM = 512
FF_FC = 512


def _cparams(*sem):
    return pltpu.CompilerParams(dimension_semantics=sem, vmem_limit_bytes=VMEM_LIMIT)


def _inproj_kernel(x_ref, g_ref, w_ref, cos_ref, sin_ref, u_ref, q_ref, k_ref, v_ref):
    x = x_ref[...]
    ms = jnp.mean(x * x, axis=-1, keepdims=True)
    hn = (x * lax.rsqrt(ms + EPS) * g_ref[...]).astype(BF16)
    proj = jnp.dot(hn, w_ref[...], preferred_element_type=F32)
    u_ref[...] = proj[:, :FOURIER_WIDTH]
    cos = cos_ref[...]
    sin = sin_ref[...]
    lane = lax.broadcasted_iota(jnp.int32, cos.shape, 1)
    first_half = (lane % HEAD_DIM) < (HEAD_DIM // 2)

    def rope(t):
        partner = jnp.where(first_half,
                            pltpu.roll(t, LANES - HEAD_DIM // 2, 1),
                            pltpu.roll(t, HEAD_DIM // 2, 1))
        return t * cos + partner * sin

    scale = HEAD_DIM ** -0.5
    for c in range(ATTN_WIDTH // LANES):
        lo = FOURIER_WIDTH + c * LANES
        q_ref[:, c * LANES:(c + 1) * LANES] = (rope(proj[:, lo:lo + LANES]) * scale).astype(BF16)
    for c in range(KV_WIDTH // LANES):
        lo = FOURIER_WIDTH + ATTN_WIDTH + c * LANES
        k_ref[:, c * LANES:(c + 1) * LANES] = rope(proj[:, lo:lo + LANES]).astype(BF16)
    v_ref[...] = proj[:, FOURIER_WIDTH + ATTN_WIDTH + KV_WIDTH:].astype(BF16)


def _inproj(x, g, w_in, cos_t, sin_t):
    n_pos_blocks = SEQ // TM
    return pl.pallas_call(
        _inproj_kernel,
        grid=(N_TOK // TM,),
        in_specs=[
            pl.BlockSpec((TM, D_MODEL), lambda i: (i, 0)),
            pl.BlockSpec((1, D_MODEL), lambda i: (0, 0)),
            pl.BlockSpec((D_MODEL, IN_WIDTH), lambda i: (0, 0)),
            pl.BlockSpec((TM, LANES), lambda i: (i % n_pos_blocks, 0)),
            pl.BlockSpec((TM, LANES), lambda i: (i % n_pos_blocks, 0)),
        ],
        out_specs=[
            pl.BlockSpec((TM, FOURIER_WIDTH), lambda i: (i, 0)),
            pl.BlockSpec((TM, ATTN_WIDTH), lambda i: (i, 0)),
            pl.BlockSpec((TM, KV_WIDTH), lambda i: (i, 0)),
            pl.BlockSpec((TM, KV_WIDTH), lambda i: (i, 0)),
        ],
        out_shape=[
            jax.ShapeDtypeStruct((N_TOK, FOURIER_WIDTH), F32),
            jax.ShapeDtypeStruct((N_TOK, ATTN_WIDTH), BF16),
            jax.ShapeDtypeStruct((N_TOK, KV_WIDTH), BF16),
            jax.ShapeDtypeStruct((N_TOK, KV_WIDTH), BF16),
        ],
        compiler_params=_cparams("parallel"),
    )(x, g, w_in, cos_t, sin_t)


def _rope_tables():
    half = HEAD_DIM // 2
    inv_freq = 1.0 / (ROPE_THETA ** (jnp.arange(half, dtype=F32) / half))
    ang = jnp.arange(SEQ, dtype=F32)[:, None] * inv_freq[None, :]
    cos = jnp.cos(ang)
    sin = jnp.sin(ang)
    reps = LANES // HEAD_DIM
    cos_t = jnp.tile(jnp.concatenate([cos, cos], axis=-1), (1, reps))
    sin_t = jnp.tile(jnp.concatenate([-sin, sin], axis=-1), (1, reps))
    return cos_t, sin_t


def _dft_tables():
    r = FFT_R
    c = np.arange(FOURIER_GROUP_DIM)
    ang_c = 2.0 * np.pi * np.outer(c, c) / FOURIER_GROUP_DIM
    n_groups = FOURIER_WIDTH // FOURIER_GROUP_DIM
    eye = np.eye(n_groups)
    ch_scale = FOURIER_GROUP_DIM ** -0.5
    c_blk = np.kron(eye, np.cos(ang_c)) * ch_scale
    s_blk = np.kron(eye, np.sin(ang_c)) * ch_scale
    cs = np.concatenate([c_blk, -s_blk], axis=1)
    k = np.arange(r)
    ang_r = 2.0 * np.pi * np.outer(k, k) / r
    st_scale = r ** -0.5
    wr = np.cos(ang_r) * st_scale
    wi = -np.sin(ang_r) * st_scale
    wbig = np.block([[wr, -wi], [wi, wr]])
    ang_t = 2.0 * np.pi * np.outer(k, k) / (r * r)
    tr = np.cos(ang_t)
    ti = -np.sin(ang_t)
    f = lambda a: jnp.asarray(a, dtype=F32)
    return f(cs).astype(BF16), f(wbig).astype(BF16), f(wr), f(wi), f(tr), f(ti)


FS_SB = 8
FS_KB = 8


def _fft_stage1_kernel(u_ref, cs_ref, wbig_ref, y_ref):
    cs = cs_ref[...]
    wbig = wbig_ref[...]
    fw = FOURIER_WIDTH
    for j in range(FS_SB):
        u = u_ref[:, j * fw:(j + 1) * fw].astype(BF16)
        g = jnp.dot(u, cs, preferred_element_type=F32)
        gst = jnp.concatenate([g[:, :fw], g[:, fw:]], axis=0).astype(BF16)
        y = jnp.dot(wbig, gst, preferred_element_type=F32)
        y_ref[:, j * 2 * fw:j * 2 * fw + fw] = y[:FFT_R]
        y_ref[:, j * 2 * fw + fw:(j + 1) * 2 * fw] = y[FFT_R:]


def _fft_stage2_kernel(y_ref, wr_ref, wi_ref, tr_ref, ti_ref, wf_ref, o_ref):
    kb = pl.program_id(1)
    wr = wr_ref[...]
    wi = wi_ref[...]
    wf = wf_ref[...]
    fw = FOURIER_WIDTH
    for j in range(FS_KB):
        k1 = kb * FS_KB + j
        tr = tr_ref[pl.ds(k1, 1), :]
        ti = ti_ref[pl.ds(k1, 1), :]
        mr = wr * tr - wi * ti
        mi = wr * ti + wi * tr
        m2 = jnp.concatenate([mr, -mi], axis=1).astype(BF16)
        yb = y_ref[j * FFT_R:(j + 1) * FFT_R, :]
        yst = jnp.concatenate([yb[:, :fw], yb[:, fw:]], axis=0).astype(BF16)
        z = jnp.dot(m2, yst, preferred_element_type=F32)
        yf = jnp.dot(z.astype(BF16), wf, preferred_element_type=F32)
        o_ref[:, j * fw:(j + 1) * fw] = yf.astype(BF16)


def _fourier(u, w_fourier):
    cs, wbig, wr, wi, tr, ti = _dft_tables()
    r, fw = FFT_R, FOURIER_WIDTH
    u2 = u.reshape(N_SEQ * r, r * fw)
    y = pl.pallas_call(
        _fft_stage1_kernel,
        grid=(N_SEQ, r // FS_SB),
        in_specs=[
            pl.BlockSpec((r, FS_SB * fw), lambda s, j: (s, j)),
            pl.BlockSpec((fw, 2 * fw), lambda s, j: (0, 0)),
            pl.BlockSpec((2 * r, 2 * r), lambda s, j: (0, 0)),
        ],
        out_specs=pl.BlockSpec((r, FS_SB * 2 * fw), lambda s, j: (s, j)),
        out_shape=jax.ShapeDtypeStruct((N_SEQ * r, r * 2 * fw), F32),
        compiler_params=_cparams("parallel", "parallel"),
    )(u2, cs, wbig)
    y2 = y.reshape(N_SEQ * r * r, 2 * fw)
    full = lambda s, j: (0, 0)
    yf = pl.pallas_call(
        _fft_stage2_kernel,
        grid=(N_SEQ, r // FS_KB),
        in_specs=[
            pl.BlockSpec((FS_KB * r, 2 * fw), lambda s, j: (s * (r // FS_KB) + j, 0)),
            pl.BlockSpec((r, r), full),
            pl.BlockSpec((r, r), full),
            pl.BlockSpec((r, r), full),
            pl.BlockSpec((r, r), full),
            pl.BlockSpec((fw, fw), full),
        ],
        out_specs=pl.BlockSpec((r, FS_KB * fw), lambda s, j: (s, j)),
        out_shape=jax.ShapeDtypeStruct((N_SEQ * r, r * fw), BF16),
        compiler_params=_cparams("parallel", "parallel"),
    )(y2, wr, wi, tr, ti, w_fourier)
    return yf.reshape(N_TOK, fw)


def _attn_kernel(sink_ref, q_ref, kp_ref, kc_ref, kn_ref, vp_ref, vc_ref, vn_ref, o_ref,
                 kext, vext):
    i = pl.program_id(1)
    kext[0:QB, :] = kp_ref[...]
    kext[QB:QB + TQ, :] = kc_ref[...]
    kext[QB + TQ:, :] = kn_ref[...]
    vext[0:QB, :] = vp_ref[...]
    vext[QB:QB + TQ, :] = vc_ref[...]
    vext[QB + TQ:, :] = vn_ref[...]

    rows = GQA_GROUP * QB
    nkeys = 3 * QB
    a = lax.broadcasted_iota(jnp.int32, (rows, nkeys), 0) % QB
    j = lax.broadcasted_iota(jnp.int32, (rows, nkeys), 1)
    rel = j - QB - a
    band = (rel <= WINDOW) & (rel >= -WINDOW)
    row_id = lax.broadcasted_iota(jnp.int32, (rows, 1), 0)

    for sb in range(TQ // QB):
        kpos = (i * (TQ // QB) + sb - 1) * QB + j
        valid = band & (kpos >= 0) & (kpos < SEQ)
        for h in range(N_KV_HEADS):
            heads = [GQA_GROUP * h + g for g in range(GQA_GROUP)]
            qs = jnp.concatenate(
                [q_ref[sb * QB:(sb + 1) * QB, hd * HEAD_DIM:(hd + 1) * HEAD_DIM] for hd in heads],
                axis=0)
            kh = kext[sb * QB:sb * QB + nkeys, h * HEAD_DIM:(h + 1) * HEAD_DIM]
            vh = vext[sb * QB:sb * QB + nkeys, h * HEAD_DIM:(h + 1) * HEAD_DIM]
            s = lax.dot_general(qs, kh, (((1,), (1,)), ((), ())), preferred_element_type=F32)
            s = jnp.where(valid, s, NEG)
            sink = jnp.where(row_id < QB, sink_ref[heads[0]],
                             jnp.where(row_id < 2 * QB, sink_ref[heads[1]], sink_ref[heads[2]]))
            m = jnp.maximum(jnp.max(s, axis=-1, keepdims=True), sink)
            p = jnp.exp(s - m)
            denom = jnp.sum(p, axis=-1, keepdims=True) + jnp.exp(sink - m)
            o = jnp.dot(p.astype(BF16), vh, preferred_element_type=F32) * (1.0 / denom)
            for g, hd in enumerate(heads):
                o_ref[sb * QB:(sb + 1) * QB, hd * HEAD_DIM:(hd + 1) * HEAD_DIM] = (
                    o[g * QB:(g + 1) * QB].astype(BF16))


def _attention(sink, q, k, v):
    nqb = SEQ // QB
    per = TQ // QB
    cur = lambda s, i: (s * (SEQ // TQ) + i, 0)
    prev = lambda s, i: (s * nqb + jnp.maximum(i * per - 1, 0), 0)
    nxt = lambda s, i: (s * nqb + jnp.minimum(i * per + per, nqb - 1), 0)
    return pl.pallas_call(
        _attn_kernel,
        grid=(N_SEQ, SEQ // TQ),
        in_specs=[
            pl.BlockSpec(memory_space=pltpu.SMEM),
            pl.BlockSpec((TQ, ATTN_WIDTH), cur),
            pl.BlockSpec((QB, KV_WIDTH), prev),
            pl.BlockSpec((TQ, KV_WIDTH), cur),
            pl.BlockSpec((QB, KV_WIDTH), nxt),
            pl.BlockSpec((QB, KV_WIDTH), prev),
            pl.BlockSpec((TQ, KV_WIDTH), cur),
            pl.BlockSpec((QB, KV_WIDTH), nxt),
        ],
        out_specs=pl.BlockSpec((TQ, ATTN_WIDTH), cur),
        out_shape=jax.ShapeDtypeStruct((N_TOK, ATTN_WIDTH), BF16),
        scratch_shapes=[pltpu.VMEM((TQ + 2 * QB, KV_WIDTH), BF16),
                        pltpu.VMEM((TQ + 2 * QB, KV_WIDTH), BF16)],
        compiler_params=_cparams("parallel", "parallel"),
    )(sink, q, k, k, k, v, v, v)


def _outproj_kernel(x_ref, yf_ref, ya_ref, wo_ref, g_ref, wr_ref, x1_ref, xn_ref, aff_ref):
    mix = jnp.dot(yf_ref[...], wo_ref[:FOURIER_WIDTH, :], preferred_element_type=F32)
    mix = mix + jnp.dot(ya_ref[...], wo_ref[FOURIER_WIDTH:, :], preferred_element_type=F32)
    x1 = x_ref[...] + mix
    x1_ref[...] = x1
    ms = jnp.mean(x1 * x1, axis=-1, keepdims=True)
    xn = (x1 * lax.rsqrt(ms + EPS) * g_ref[...]).astype(BF16)
    xn_ref[...] = xn
    logits = lax.dot_general(wr_ref[...], xn, (((1,), (1,)), ((), ())), preferred_element_type=F32)
    mx = jnp.max(logits, axis=0, keepdims=True)
    ex = jnp.exp(logits - mx)
    aff = ex / jnp.sum(ex, axis=0, keepdims=True)
    for c in range(TM // LANES):
        aff_ref[c] = aff[:, c * LANES:(c + 1) * LANES]


def _outproj(x, yf, ya, w_out, g, w_router_t):
    return pl.pallas_call(
        _outproj_kernel,
        grid=(N_TOK // TM,),
        in_specs=[
            pl.BlockSpec((TM, D_MODEL), lambda i: (i, 0)),
            pl.BlockSpec((TM, FOURIER_WIDTH), lambda i: (i, 0)),
            pl.BlockSpec((TM, ATTN_WIDTH), lambda i: (i, 0)),
            pl.BlockSpec((D_MODEL, D_MODEL), lambda i: (0, 0)),
            pl.BlockSpec((1, D_MODEL), lambda i: (0, 0)),
            pl.BlockSpec((N_EXPERTS, D_MODEL), lambda i: (0, 0)),
        ],
        out_specs=[
            pl.BlockSpec((TM, D_MODEL), lambda i: (i, 0)),
            pl.BlockSpec((TM, D_MODEL), lambda i: (i, 0)),
            pl.BlockSpec((TM // LANES, N_EXPERTS, LANES), lambda i: (i, 0, 0)),
        ],
        out_shape=[
            jax.ShapeDtypeStruct((N_TOK, D_MODEL), F32),
            jax.ShapeDtypeStruct((N_TOK, D_MODEL), BF16),
            jax.ShapeDtypeStruct((N_TOK // LANES, N_EXPERTS, LANES), F32),
        ],
        compiler_params=_cparams("parallel"),
    )(x, yf, ya, w_out, g, w_router_t)


def _ffn_kernel(x_ref, wg_ref, wu_ref, wd_ref, o_ref):
    x = x_ref[0]
    acc = jnp.zeros((FF_TM, D_MODEL), F32)
    for c in range(EXPERT_D_FF // FF_FC):
        gt = jnp.dot(x, wg_ref[0, :, c * FF_FC:(c + 1) * FF_FC], preferred_element_type=F32)
        up = jnp.dot(x, wu_ref[0, :, c * FF_FC:(c + 1) * FF_FC], preferred_element_type=F32)
        h = (gt * (1.0 / (1.0 + jnp.exp(-gt))) * up).astype(BF16)
        acc = acc + jnp.dot(h, wd_ref[0, c * FF_FC:(c + 1) * FF_FC, :], preferred_element_type=F32)
    o_ref[0] = acc


def _ffn(xe, w_gate, w_up, w_down):
    n_slots = xe.shape[1]
    return pl.pallas_call(
        _ffn_kernel,
        grid=(N_EXPERTS, n_slots // FF_TM),
        in_specs=[
            pl.BlockSpec((1, FF_TM, D_MODEL), lambda e, i: (e, i, 0)),
            pl.BlockSpec((1, D_MODEL, EXPERT_D_FF), lambda e, i: (e, 0, 0)),
            pl.BlockSpec((1, D_MODEL, EXPERT_D_FF), lambda e, i: (e, 0, 0)),
            pl.BlockSpec((1, EXPERT_D_FF, D_MODEL), lambda e, i: (e, 0, 0)),
        ],
        out_specs=pl.BlockSpec((1, FF_TM, D_MODEL), lambda e, i: (e, i, 0)),
        out_shape=jax.ShapeDtypeStruct((N_EXPERTS, n_slots, D_MODEL), F32),
        compiler_params=_cparams("parallel", "parallel"),
    )(xe, w_gate, w_up, w_down)


def _final_kernel(x1_ref, f_ref, g_ref, o_ref):
    x = x1_ref[...] + f_ref[...]
    ms = jnp.mean(x * x, axis=-1, keepdims=True)
    o_ref[...] = x * lax.rsqrt(ms + EPS) * g_ref[...]


def _final(x1, ffn_out, g):
    return pl.pallas_call(
        _final_kernel,
        grid=(N_TOK // TM,),
        in_specs=[
            pl.BlockSpec((TM, D_MODEL), lambda i: (i, 0)),
            pl.BlockSpec((TM, D_MODEL), lambda i: (i, 0)),
            pl.BlockSpec((1, D_MODEL), lambda i: (0, 0)),
        ],
        out_specs=pl.BlockSpec((TM, D_MODEL), lambda i: (i, 0)),
        out_shape=jax.ShapeDtypeStruct((N_TOK, D_MODEL), F32),
        compiler_params=_cparams("parallel"),
    )(x1, ffn_out, g)


def kernel(x_prompt, x_sample, norm_mix_g, w_in, w_fourier, sink, w_out, norm_ffn_g, w_router,
           w_gate, w_up, w_down, norm_final_g):
    x = jnp.concatenate([x_prompt.reshape(-1, D_MODEL), x_sample.reshape(-1, D_MODEL)], axis=0)
    cos_t, sin_t = _rope_tables()

    u, q, k, v = _inproj(x, norm_mix_g[0][None, :], w_in[0].astype(BF16), cos_t, sin_t)
    yf = _fourier(u, w_fourier[0].astype(BF16))
    ya = _attention(sink[0], q, k, v)
    x1, xn, aff3 = _outproj(x, yf, ya, w_out[0].astype(BF16), norm_ffn_g[0][None, :],
                            w_router[0].T.astype(BF16))

    aff_t = aff3.transpose(1, 0, 2).reshape(N_EXPERTS, N_TOK)
    groups = [(0, SEQ), (SEQ, 2 * SEQ)]
    idxs, gates = [], []
    for off, n in groups:
        cap = EC_CAPACITY_FACTOR * n // N_EXPERTS
        gv, ix = lax.top_k(aff_t[:, off:off + n], cap)
        idxs.append(ix + off)
        gates.append(gv)
    idx = jnp.concatenate(idxs, axis=1)
    gate = jnp.concatenate(gates, axis=1)
    xe = xn[idx]
    ye = _ffn(xe, w_gate[0].astype(BF16), w_up[0].astype(BF16), w_down[0].astype(BF16))
    contrib = ye * gate[..., None]
    ffn_out = jnp.zeros((N_TOK, D_MODEL), F32).at[idx.reshape(-1)].add(contrib.reshape(-1, D_MODEL))

    y = _final(x1, ffn_out, norm_final_g[None, :])
    y_prompt = y[:SEQ].reshape(x_prompt.shape)
    y_sample = y[SEQ:].reshape(x_sample.shape)
    return (y_prompt, y_sample)
```

```python
import functools

import numpy as np
import jax
import jax.numpy as jnp
from jax import lax
from jax.experimental import pallas as pl
from jax.experimental.pallas import tpu as pltpu

F32 = jnp.float32
BF16 = jnp.bfloat16

D_MODEL = 1024
SEQ = 16384
N_SEQ = 3
N_TOK = N_SEQ * SEQ
HEAD_DIM = 64
N_HEADS = 12
N_KV_HEADS = 4
GQA_GROUP = N_HEADS // N_KV_HEADS
ATTN_WIDTH = N_HEADS * HEAD_DIM
KV_WIDTH = N_KV_HEADS * HEAD_DIM
FOURIER_WIDTH = 256
FOURIER_GROUP_DIM = 64
IN_WIDTH = FOURIER_WIDTH + ATTN_WIDTH + 2 * KV_WIDTH
WINDOW = 128
ROPE_THETA = 10000.0
N_EXPERTS = 16
EC_CAPACITY_FACTOR = 2
EXPERT_D_FF = 2048
EPS = 1e-6
NEG = -1e30

LANES = 128
FFT_R = 128
VMEM_LIMIT = 48 * 1024 * 1024

TM = 512
TQ = 512
QB = 128
FF_TM = 512
FF_FC = 512


def _cparams(*sem):
    return pltpu.CompilerParams(dimension_semantics=sem, vmem_limit_bytes=VMEM_LIMIT)


def _inproj_kernel(x_ref, g_ref, w_ref, cos_ref, sin_ref, u_ref, q_ref, k_ref, v_ref):
    x = x_ref[...]
    ms = jnp.mean(x * x, axis=-1, keepdims=True)
    hn = (x * lax.rsqrt(ms + EPS) * g_ref[...]).astype(BF16)
    proj = jnp.dot(hn, w_ref[...], preferred_element_type=F32)
    u_ref[...] = proj[:, :FOURIER_WIDTH]
    cos = cos_ref[...]
    sin = sin_ref[...]
    lane = lax.broadcasted_iota(jnp.int32, cos.shape, 1)
    first_half = (lane % HEAD_DIM) < (HEAD_DIM // 2)

    def rope(t):
        partner = jnp.where(first_half,
                            pltpu.roll(t, LANES - HEAD_DIM // 2, 1),
                            pltpu.roll(t, HEAD_DIM // 2, 1))
        return t * cos + partner * sin

    scale = HEAD_DIM ** -0.5
    for c in range(ATTN_WIDTH // LANES):
        lo = FOURIER_WIDTH + c * LANES
        q_ref[:, c * LANES:(c + 1) * LANES] = (rope(proj[:, lo:lo + LANES]) * scale).astype(BF16)
    for c in range(KV_WIDTH // LANES):
        lo = FOURIER_WIDTH + ATTN_WIDTH + c * LANES
        k_ref[:, c * LANES:(c + 1) * LANES] = rope(proj[:, lo:lo + LANES]).astype(BF16)
    v_ref[...] = proj[:, FOURIER_WIDTH + ATTN_WIDTH + KV_WIDTH:].astype(BF16)


def _inproj(x, g, w_in, cos_t, sin_t):
    n_pos_blocks = SEQ // TM
    return pl.pallas_call(
        _inproj_kernel,
        grid=(N_TOK // TM,),
        in_specs=[
            pl.BlockSpec((TM, D_MODEL), lambda i: (i, 0)),
            pl.BlockSpec((1, D_MODEL), lambda i: (0, 0)),
            pl.BlockSpec((D_MODEL, IN_WIDTH), lambda i: (0, 0)),
            pl.BlockSpec((TM, LANES), lambda i: (i % n_pos_blocks, 0)),
            pl.BlockSpec((TM, LANES), lambda i: (i % n_pos_blocks, 0)),
        ],
        out_specs=[
            pl.BlockSpec((TM, FOURIER_WIDTH), lambda i: (i, 0)),
            pl.BlockSpec((TM, ATTN_WIDTH), lambda i: (i, 0)),
            pl.BlockSpec((TM, KV_WIDTH), lambda i: (i, 0)),
            pl.BlockSpec((TM, KV_WIDTH), lambda i: (i, 0)),
        ],
        out_shape=[
            jax.ShapeDtypeStruct((N_TOK, FOURIER_WIDTH), F32),
            jax.ShapeDtypeStruct((N_TOK, ATTN_WIDTH), BF16),
            jax.ShapeDtypeStruct((N_TOK, KV_WIDTH), BF16),
            jax.ShapeDtypeStruct((N_TOK, KV_WIDTH), BF16),
        ],
        compiler_params=_cparams("parallel"),
    )(x, g, w_in, cos_t, sin_t)


def _rope_tables():
    half = HEAD_DIM // 2
    inv_freq = 1.0 / (ROPE_THETA ** (jnp.arange(half, dtype=F32) / half))
    ang = jnp.arange(SEQ, dtype=F32)[:, None] * inv_freq[None, :]
    cos = jnp.cos(ang)
    sin = jnp.sin(ang)
    reps = LANES // HEAD_DIM
    cos_t = jnp.tile(jnp.concatenate([cos, cos], axis=-1), (1, reps))
    sin_t = jnp.tile(jnp.concatenate([-sin, sin], axis=-1), (1, reps))
    return cos_t, sin_t


def _dft_tables():
    r = FFT_R
    c = np.arange(FOURIER_GROUP_DIM)
    ang_c = 2.0 * np.pi * np.outer(c, c) / FOURIER_GROUP_DIM
    n_groups = FOURIER_WIDTH // FOURIER_GROUP_DIM
    eye = np.eye(n_groups)
    ch_scale = FOURIER_GROUP_DIM ** -0.5
    c_blk = np.kron(eye, np.cos(ang_c)) * ch_scale
    s_blk = np.kron(eye, np.sin(ang_c)) * ch_scale
    cs = np.concatenate([c_blk, -s_blk], axis=1)
    k = np.arange(r)
    ang_r = 2.0 * np.pi * np.outer(k, k) / r
    st_scale = r ** -0.5
    wr = np.cos(ang_r) * st_scale
    wi = -np.sin(ang_r) * st_scale
    wbig = np.block([[wr, -wi], [wi, wr]])
    ang_t = 2.0 * np.pi * np.outer(k, k) / (r * r)
    tr = np.cos(ang_t)
    ti = -np.sin(ang_t)
    f = lambda a: jnp.asarray(a, dtype=F32)
    return f(cs).astype(BF16), f(wbig).astype(BF16), f(wr), f(wi), f(tr), f(ti)


FS_SB = 8
FS_KB = 8


def _fft_stage1_kernel(u_ref, cs_ref, wbig_ref, y_ref):
    cs = cs_ref[...]
    wbig = wbig_ref[...]
    fw = FOURIER_WIDTH
    for j in range(FS_SB):
        u = u_ref[:, j * fw:(j + 1) * fw].astype(BF16)
        g = jnp.dot(u, cs, preferred_element_type=F32)
        gst = jnp.concatenate([g[:, :fw], g[:, fw:]], axis=0).astype(BF16)
        y = jnp.dot(wbig, gst, preferred_element_type=F32)
        y_ref[:, j * 2 * fw:j * 2 * fw + fw] = y[:FFT_R]
        y_ref[:, j * 2 * fw + fw:(j + 1) * 2 * fw] = y[FFT_R:]


def _fft_stage2_kernel(y_ref, wr_ref, wi_ref, tr_ref, ti_ref, wf_ref, o_ref):
    kb = pl.program_id(1)
    wr = wr_ref[...]
    wi = wi_ref[...]
    wf = wf_ref[...]
    fw = FOURIER_WIDTH
    for j in range(FS_KB):
        k1 = kb * FS_KB + j
        tr = tr_ref[pl.ds(k1, 1), :]
        ti = ti_ref[pl.ds(k1, 1), :]
        mr = wr * tr - wi * ti
        mi = wr * ti + wi * tr
        m2 = jnp.concatenate([mr, -mi], axis=1).astype(BF16)
        yb = y_ref[j * FFT_R:(j + 1) * FFT_R, :]
        yst = jnp.concatenate([yb[:, :fw], yb[:, fw:]], axis=0).astype(BF16)
        z = jnp.dot(m2, yst, preferred_element_type=F32)
        yf = jnp.dot(z.astype(BF16), wf, preferred_element_type=F32)
        o_ref[:, j * fw:(j + 1) * fw] = yf.astype(BF16)


def _fourier(u, w_fourier):
    cs, wbig, wr, wi, tr, ti = _dft_tables()
    r, fw = FFT_R, FOURIER_WIDTH
    u2 = u.reshape(N_SEQ * r, r * fw)
    y = pl.pallas_call(
        _fft_stage1_kernel,
        grid=(N_SEQ, r // FS_SB),
        in_specs=[
            pl.BlockSpec((r, FS_SB * fw), lambda s, j: (s, j)),
            pl.BlockSpec((fw, 2 * fw), lambda s, j: (0, 0)),
            pl.BlockSpec((2 * r, 2 * r), lambda s, j: (0, 0)),
        ],
        out_specs=pl.BlockSpec((r, FS_SB * 2 * fw), lambda s, j: (s, j)),
        out_shape=jax.ShapeDtypeStruct((N_SEQ * r, r * 2 * fw), F32),
        compiler_params=_cparams("parallel", "parallel"),
    )(u2, cs, wbig)
    y2 = y.reshape(N_SEQ * r * r, 2 * fw)
    full = lambda s, j: (0, 0)
    yf = pl.pallas_call(
        _fft_stage2_kernel,
        grid=(N_SEQ, r // FS_KB),
        in_specs=[
            pl.BlockSpec((FS_KB * r, 2 * fw), lambda s, j: (s * (r // FS_KB) + j, 0)),
            pl.BlockSpec((r, r), full),
            pl.BlockSpec((r, r), full),
            pl.BlockSpec((r, r), full),
            pl.BlockSpec((r, r), full),
            pl.BlockSpec((fw, fw), full),
        ],
        out_specs=pl.BlockSpec((r, FS_KB * fw), lambda s, j: (s, j)),
        out_shape=jax.ShapeDtypeStruct((N_SEQ * r, r * fw), BF16),
        compiler_params=_cparams("parallel", "parallel"),
    )(y2, wr, wi, tr, ti, w_fourier)
    return yf.reshape(N_TOK, fw)


def _attn_kernel(sink_ref, q_ref, kp_ref, kc_ref, kn_ref, vp_ref, vc_ref, vn_ref, o_ref,
                 kext, vext):
    i = pl.program_id(1)
    kext[0:QB, :] = kp_ref[...]
    kext[QB:QB + TQ, :] = kc_ref[...]
    kext[QB + TQ:, :] = kn_ref[...]
    vext[0:QB, :] = vp_ref[...]
    vext[QB:QB + TQ, :] = vc_ref[...]
    vext[QB + TQ:, :] = vn_ref[...]

    rows = GQA_GROUP * QB
    nkeys = 3 * QB
    a = lax.broadcasted_iota(jnp.int32, (rows, nkeys), 0) % QB
    j = lax.broadcasted_iota(jnp.int32, (rows, nkeys), 1)
    rel = j - QB - a
    band = (rel <= WINDOW) & (rel >= -WINDOW)
    row_id = lax.broadcasted_iota(jnp.int32, (rows, 1), 0)

    for sb in range(TQ // QB):
        kpos = (i * (TQ // QB) + sb - 1) * QB + j
        valid = band & (kpos >= 0) & (kpos < SEQ)
        for h in range(N_KV_HEADS):
            heads = [GQA_GROUP * h + g for g in range(GQA_GROUP)]
            qs = jnp.concatenate(
                [q_ref[sb * QB:(sb + 1) * QB, hd * HEAD_DIM:(hd + 1) * HEAD_DIM] for hd in heads],
                axis=0)
            kh = kext[sb * QB:sb * QB + nkeys, h * HEAD_DIM:(h + 1) * HEAD_DIM]
            vh = vext[sb * QB:sb * QB + nkeys, h * HEAD_DIM:(h + 1) * HEAD_DIM]
            s = lax.dot_general(qs, kh, (((1,), (1,)), ((), ())), preferred_element_type=F32)
            s = jnp.where(valid, s, NEG)
            sink = jnp.where(row_id < QB, sink_ref[heads[0]],
                             jnp.where(row_id < 2 * QB, sink_ref[heads[1]], sink_ref[heads[2]]))
            m = jnp.maximum(jnp.max(s, axis=-1, keepdims=True), sink)
            p = jnp.exp(s - m)
            denom = jnp.sum(p, axis=-1, keepdims=True) + jnp.exp(sink - m)
            o = jnp.dot(p.astype(BF16), vh, preferred_element_type=F32) * (1.0 / denom)
            for g, hd in enumerate(heads):
                o_ref[sb * QB:(sb + 1) * QB, hd * HEAD_DIM:(hd + 1) * HEAD_DIM] = (
                    o[g * QB:(g + 1) * QB].astype(BF16))


def _attention(sink, q, k, v):
    nqb = SEQ // QB
    per = TQ // QB
    cur = lambda s, i: (s * (SEQ // TQ) + i, 0)
    prev = lambda s, i: (s * nqb + jnp.maximum(i * per - 1, 0), 0)
    nxt = lambda s, i: (s * nqb + jnp.minimum(i * per + per, nqb - 1), 0)
    return pl.pallas_call(
        _attn_kernel,
        grid=(N_SEQ, SEQ // TQ),
        in_specs=[
            pl.BlockSpec(memory_space=pltpu.SMEM),
            pl.BlockSpec((TQ, ATTN_WIDTH), cur),
            pl.BlockSpec((QB, KV_WIDTH), prev),
            pl.BlockSpec((TQ, KV_WIDTH), cur),
            pl.BlockSpec((QB, KV_WIDTH), nxt),
            pl.BlockSpec((QB, KV_WIDTH), prev),
            pl.BlockSpec((TQ, KV_WIDTH), cur),
            pl.BlockSpec((QB, KV_WIDTH), nxt),
        ],
        out_specs=pl.BlockSpec((TQ, ATTN_WIDTH), cur),
        out_shape=jax.ShapeDtypeStruct((N_TOK, ATTN_WIDTH), BF16),
        scratch_shapes=[pltpu.VMEM((TQ + 2 * QB, KV_WIDTH), BF16),
                        pltpu.VMEM((TQ + 2 * QB, KV_WIDTH), BF16)],
        compiler_params=_cparams("parallel", "parallel"),
    )(sink, q, k, k, k, v, v, v)


def _outproj_kernel(x_ref, yf_ref, ya_ref, wo_ref, g_ref, wr_ref, x1_ref, xn_ref, aff_ref):
    mix = jnp.dot(yf_ref[...], wo_ref[:FOURIER_WIDTH, :], preferred_element_type=F32)
    mix = mix + jnp.dot(ya_ref[...], wo_ref[FOURIER_WIDTH:, :], preferred_element_type=F32)
    x1 = x_ref[...] + mix
    x1_ref[...] = x1
    ms = jnp.mean(x1 * x1, axis=-1, keepdims=True)
    xn = (x1 * lax.rsqrt(ms + EPS) * g_ref[...]).astype(BF16)
    xn_ref[...] = xn
    logits = lax.dot_general(wr_ref[...], xn, (((1,), (1,)), ((), ())), preferred_element_type=F32)
    mx = jnp.max(logits, axis=0, keepdims=True)
    ex = jnp.exp(logits - mx)
    aff = ex / jnp.sum(ex, axis=0, keepdims=True)
    for c in range(TM // LANES):
        aff_ref[c] = aff[:, c * LANES:(c + 1) * LANES]


def _outproj(x, yf, ya, w_out, g, w_router_t):
    return pl.pallas_call(
        _outproj_kernel,
        grid=(N_TOK // TM,),
        in_specs=[
            pl.BlockSpec((TM, D_MODEL), lambda i: (i, 0)),
            pl.BlockSpec((TM, FOURIER_WIDTH), lambda i: (i, 0)),
            pl.BlockSpec((TM, ATTN_WIDTH), lambda i: (i, 0)),
            pl.BlockSpec((D_MODEL, D_MODEL), lambda i: (0, 0)),
            pl.BlockSpec((1, D_MODEL), lambda i: (0, 0)),
            pl.BlockSpec((N_EXPERTS, D_MODEL), lambda i: (0, 0)),
        ],
        out_specs=[
            pl.BlockSpec((TM, D_MODEL), lambda i: (i, 0)),
            pl.BlockSpec((TM, D_MODEL), lambda i: (i, 0)),
            pl.BlockSpec((TM // LANES, N_EXPERTS, LANES), lambda i: (i, 0, 0)),
        ],
        out_shape=[
            jax.ShapeDtypeStruct((N_TOK, D_MODEL), F32),
            jax.ShapeDtypeStruct((N_TOK, D_MODEL), BF16),
            jax.ShapeDtypeStruct((N_TOK // LANES, N_EXPERTS, LANES), F32),
        ],
        compiler_params=_cparams("parallel"),
    )(x, yf, ya, w_out, g, w_router_t)


RB = 512
N_RB = N_TOK // RB
GROUP_TOKENS = (SEQ, 2 * SEQ)
GROUP0_BLOCKS = GROUP_TOKENS[0] // RB
CHUNK = 8
TILE_ROWS = 256
CHUNKS_PER_TILE = TILE_ROWS // CHUNK
SLOT_TILES = 14
SLOT_ROWS = SLOT_TILES * FF_TM
assert SLOT_ROWS >= EC_CAPACITY_FACTOR * N_TOK // N_EXPERTS + N_RB * (CHUNK - 1)
TRASH_ROW = N_EXPERTS * SLOT_ROWS
MAX_CHUNKS = (N_EXPERTS * (RB + CHUNK - 1)) // CHUNK
LIST_LEN = ((MAX_CHUNKS + CHUNKS_PER_TILE - 1) // CHUNKS_PER_TILE) * CHUNKS_PER_TILE


def _thresh_kernel(aff_ref, thr_ref, need_ref, *, cap):
    aff = aff_ref[...]

    def body(i, cur):
        cand = cur | jnp.left_shift(jnp.int32(1), 30 - i)
        cnt = jnp.sum(jnp.where(aff >= pltpu.bitcast(cand, F32), 1.0, 0.0), axis=1, keepdims=True)
        return jnp.where(cnt >= cap, cand, cur)

    thr = pltpu.bitcast(lax.fori_loop(0, 31, body, jnp.zeros((N_EXPERTS, 1), jnp.int32)), F32)
    n_gt = jnp.sum(jnp.where(aff > thr, 1.0, 0.0), axis=1, keepdims=True)
    thr_ref[...] = jnp.broadcast_to(thr, thr_ref.shape)
    need_ref[...] = jnp.broadcast_to(cap - n_gt, need_ref.shape)


def _thresholds(aff_group):
    n = aff_group.shape[1]
    cap = EC_CAPACITY_FACTOR * n // N_EXPERTS
    return pl.pallas_call(
        functools.partial(_thresh_kernel, cap=float(cap)),
        out_shape=[jax.ShapeDtypeStruct((N_EXPERTS, LANES), F32),
                   jax.ShapeDtypeStruct((N_EXPERTS, LANES), F32)],
        compiler_params=pltpu.CompilerParams(vmem_limit_bytes=VMEM_LIMIT),
    )(aff_group)


def _select_kernel(aff_ref, thr_ref, need_ref, ltri_ref, selw_ref, rank_ref, m_ref, eq_seen):
    rb = pl.program_id(0)

    @pl.when((rb == 0) | (rb == GROUP0_BLOCKS))
    def _():
        eq_seen[...] = jnp.zeros_like(eq_seen)

    aff = aff_ref[...]
    thr = thr_ref[0]
    need = need_ref[0]
    ltri = ltri_ref[...]
    eq = jnp.where(aff == thr, 1.0, 0.0)
    eq_before = jnp.dot(ltri, eq.astype(BF16), preferred_element_type=F32) + eq_seen[...]
    sel = (aff > thr) | ((aff == thr) & (eq_before < need))
    eq_seen[...] += jnp.sum(eq, axis=0, keepdims=True)
    self = jnp.where(sel, 1.0, 0.0)
    rank_ref[...] = jnp.dot(ltri, self.astype(BF16), preferred_element_type=F32)
    selw_ref[...] = jnp.where(sel, aff, 0.0)
    m_ref[0] = jnp.sum(self, axis=0, keepdims=True)


def _select(aff_tok, thr, need):
    ltri = jnp.asarray(np.tril(np.ones((RB, RB), np.float32), -1), dtype=BF16)
    grp = lambda rb: (jnp.where(rb >= GROUP0_BLOCKS, 1, 0), 0, 0)
    return pl.pallas_call(
        _select_kernel,
        grid=(N_RB,),
        in_specs=[
            pl.BlockSpec((RB, N_EXPERTS), lambda rb: (rb, 0)),
            pl.BlockSpec((1, 1, N_EXPERTS), grp),
            pl.BlockSpec((1, 1, N_EXPERTS), grp),
            pl.BlockSpec((RB, RB), lambda rb: (0, 0)),
        ],
        out_specs=[
            pl.BlockSpec((RB, N_EXPERTS), lambda rb: (rb, 0)),
            pl.BlockSpec((RB, N_EXPERTS), lambda rb: (rb, 0)),
            pl.BlockSpec((1, 1, N_EXPERTS), lambda rb: (rb, 0, 0)),
        ],
        out_shape=[
            jax.ShapeDtypeStruct((N_TOK, N_EXPERTS), F32),
            jax.ShapeDtypeStruct((N_TOK, N_EXPERTS), F32),
            jax.ShapeDtypeStruct((N_RB, 1, N_EXPERTS), F32),
        ],
        scratch_shapes=[pltpu.VMEM((1, N_EXPERTS), F32)],
        compiler_params=_cparams("arbitrary"),
    )(aff_tok, thr, need, ltri)


def _fill_chunk_list(list_ref, rb, seg_start_ref, seg_rows_ref):
    cnt = jnp.int32(0)
    for e in range(N_EXPERTS):
        base = e * SLOT_ROWS + seg_start_ref[rb, e]

        def body(k, c, base=base):
            list_ref[c] = base + CHUNK * k
            return c + 1

        cnt = lax.fori_loop(0, seg_rows_ref[rb, e] // CHUNK, body, cnt)
    return cnt


def _gather_kernel(seg_start_ref, seg_rows_ref, seg_off_ref, tot_ref,
                   x_ref, selw_ref, rank_ref, xe_hbm,
                   p_scr, obuf, zbuf, dst_list, pending, sem, zsem):
    rb = pl.program_id(0)
    n_rows = tot_ref[rb]
    n_chunks = n_rows // CHUNK
    n_tiles = (n_rows + TILE_ROWS - 1) // TILE_ROWS

    @pl.when(rb == 0)
    def _():
        pending[0] = 0
        pending[1] = 0

    _fill_chunk_list(dst_list, rb, seg_start_ref, seg_rows_ref)

    def tile_copy(slot):
        return pltpu.make_async_copy(obuf.at[slot], xe_hbm.at[pl.ds(0, TILE_ROWS)], sem.at[slot])

    e_iota = lax.broadcasted_iota(jnp.int32, (N_EXPERTS, 1), 0)
    off = jnp.zeros((N_EXPERTS, 1), F32)
    for e in range(N_EXPERTS):
        off = jnp.where(e_iota == e, seg_off_ref[rb, e].astype(F32), off)
    tgt = jnp.where(selw_ref[...] > 0.0, rank_ref[...] + off, -1.0)
    x = x_ref[...]

    def tile_body(t, carry):
        slot = t % 2
        r0 = t * TILE_ROWS

        @pl.when(pending[slot] == 1)
        def _():
            tile_copy(slot).wait()

        p_scr[...] = jnp.zeros_like(p_scr)
        row = (lax.broadcasted_iota(jnp.int32, (TILE_ROWS, RB), 0) + r0).astype(F32)
        for e in range(N_EXPERTS):
            lo = seg_off_ref[rb, e]
            hi = lo + seg_rows_ref[rb, e]

            @pl.when((lo < r0 + TILE_ROWS) & (hi > r0))
            def _(e=e):
                p_scr[...] = jnp.where(tgt[e:e + 1, :] == row, 1.0, p_scr[...])

        obuf[slot] = jnp.dot(p_scr[...].astype(BF16), x, preferred_element_type=F32)

        def issue(i, c):
            g = t * CHUNKS_PER_TILE + i
            dst = jnp.where(g < n_chunks, dst_list[jnp.minimum(g, n_chunks - 1)],
                            TRASH_ROW + slot * TILE_ROWS + i * CHUNK)
            pltpu.make_async_copy(obuf.at[slot, pl.ds(pl.multiple_of(i * CHUNK, CHUNK), CHUNK)],
                                  xe_hbm.at[pl.ds(pl.multiple_of(dst, CHUNK), CHUNK)],
                                  sem.at[slot]).start()
            return c

        lax.fori_loop(0, CHUNKS_PER_TILE, issue, 0)
        pending[slot] = 1
        return carry

    lax.fori_loop(0, n_tiles, tile_body, 0)

    @pl.when(rb == N_RB - 1)
    def _():
        for slot in range(2):
            @pl.when(pending[slot] == 1)
            def _(slot=slot):
                tile_copy(slot).wait()
        zbuf[...] = jnp.zeros_like(zbuf)
        for e in range(N_EXPERTS):
            used = seg_start_ref[rb, e] + seg_rows_ref[rb, e]
            end = ((used + FF_TM - 1) // FF_TM) * FF_TM

            def zcopy(k, used=used, e=e):
                return pltpu.make_async_copy(
                    zbuf, xe_hbm.at[pl.ds(pl.multiple_of(e * SLOT_ROWS + used + k * CHUNK, CHUNK), CHUNK)],
                    zsem)

            def zstart(k, c, zcopy=zcopy):
                zcopy(k).start()
                return c

            def zwait(k, c, zcopy=zcopy):
                zcopy(k).wait()
                return c

            lax.fori_loop(0, (end - used) // CHUNK, zstart, 0)
            lax.fori_loop(0, (end - used) // CHUNK, zwait, 0)


def _route_gather(seg_start, seg_rows, seg_off, tot, xn, selw_t, rank_t):
    gs = pltpu.PrefetchScalarGridSpec(
        num_scalar_prefetch=4,
        grid=(N_RB,),
        in_specs=[
            pl.BlockSpec((RB, D_MODEL), lambda rb, *_: (rb, 0)),
            pl.BlockSpec((N_EXPERTS, RB), lambda rb, *_: (0, rb)),
            pl.BlockSpec((N_EXPERTS, RB), lambda rb, *_: (0, rb)),
        ],
        out_specs=pl.BlockSpec(memory_space=pl.ANY),
        scratch_shapes=[
            pltpu.VMEM((TILE_ROWS, RB), F32),
            pltpu.VMEM((2, TILE_ROWS, D_MODEL), F32),
            pltpu.VMEM((CHUNK, D_MODEL), F32),
            pltpu.SMEM((LIST_LEN,), jnp.int32),
            pltpu.SMEM((2,), jnp.int32),
            pltpu.SemaphoreType.DMA((2,)),
            pltpu.SemaphoreType.DMA,
        ],
    )
    return pl.pallas_call(
        _gather_kernel,
        grid_spec=gs,
        out_shape=jax.ShapeDtypeStruct((TRASH_ROW + 2 * TILE_ROWS, D_MODEL), F32),
        compiler_params=_cparams("arbitrary"),
    )(seg_start, seg_rows, seg_off, tot, xn, selw_t, rank_t)


def _ffn_kernel(used_ref, x_ref, wg_ref, wu_ref, wd_ref, o_ref):
    e = pl.program_id(0)
    i = pl.program_id(1)

    @pl.when(i * FF_TM < used_ref[e])
    def _():
        x = x_ref[...].astype(BF16)
        acc = jnp.zeros((FF_TM, D_MODEL), F32)
        for c in range(EXPERT_D_FF // FF_FC):
            gt = jnp.dot(x, wg_ref[0, :, c * FF_FC:(c + 1) * FF_FC], preferred_element_type=F32)
            up = jnp.dot(x, wu_ref[0, :, c * FF_FC:(c + 1) * FF_FC], preferred_element_type=F32)
            h = (gt * (1.0 / (1.0 + jnp.exp(-gt))) * up).astype(BF16)
            acc = acc + jnp.dot(h, wd_ref[0, c * FF_FC:(c + 1) * FF_FC, :], preferred_element_type=F32)
        o_ref[...] = acc


def _ffn(used, xe, w_gate, w_up, w_down):
    def rows(e, i, used_ref):
        last = jnp.maximum((used_ref[e] + FF_TM - 1) // FF_TM - 1, 0)
        return (e * SLOT_TILES + jnp.minimum(i, last), 0)

    gs = pltpu.PrefetchScalarGridSpec(
        num_scalar_prefetch=1,
        grid=(N_EXPERTS, SLOT_TILES),
        in_specs=[
            pl.BlockSpec((FF_TM, D_MODEL), rows),
            pl.BlockSpec((1, D_MODEL, EXPERT_D_FF), lambda e, i, u: (e, 0, 0)),
            pl.BlockSpec((1, D_MODEL, EXPERT_D_FF), lambda e, i, u: (e, 0, 0)),
            pl.BlockSpec((1, EXPERT_D_FF, D_MODEL), lambda e, i, u: (e, 0, 0)),
        ],
        out_specs=pl.BlockSpec((FF_TM, D_MODEL), rows),
    )
    return pl.pallas_call(
        _ffn_kernel,
        grid_spec=gs,
        out_shape=jax.ShapeDtypeStruct((N_EXPERTS * SLOT_ROWS, D_MODEL), F32),
        compiler_params=_cparams("arbitrary", "arbitrary"),
    )(used, xe, w_gate, w_up, w_down)


def _combine_kernel(seg_start_ref, seg_rows_ref, seg_off_ref, tot_ref,
                    x1_ref, selw_ref, rank_ref, g_ref, ye_hbm, o_ref,
                    w_scr, ybuf, acc, src_list, sem):
    rb = pl.program_id(0)
    n_rows = tot_ref[rb]
    n_chunks = n_rows // CHUNK
    n_tiles = (n_rows + TILE_ROWS - 1) // TILE_ROWS

    _fill_chunk_list(src_list, rb, seg_start_ref, seg_rows_ref)

    def fetch(t, slot):
        def issue(i, c):
            g = t * CHUNKS_PER_TILE + i
            src = jnp.where(g < n_chunks, src_list[jnp.minimum(g, n_chunks - 1)], 0)
            pltpu.make_async_copy(ye_hbm.at[pl.ds(pl.multiple_of(src, CHUNK), CHUNK)],
                                  ybuf.at[slot, pl.ds(pl.multiple_of(i * CHUNK, CHUNK), CHUNK)],
                                  sem.at[slot]).start()
            return c

        lax.fori_loop(0, CHUNKS_PER_TILE, issue, 0)

    def tile_wait(slot):
        pltpu.make_async_copy(ye_hbm.at[pl.ds(0, TILE_ROWS)], ybuf.at[slot], sem.at[slot]).wait()

    @pl.when(n_tiles > 0)
    def _():
        fetch(0, 0)

    e_iota = lax.broadcasted_iota(jnp.int32, (1, N_EXPERTS), 1)
    off = jnp.zeros((1, N_EXPERTS), F32)
    for e in range(N_EXPERTS):
        off = jnp.where(e_iota == e, seg_off_ref[rb, e].astype(F32), off)
    selw = selw_ref[...]
    tgt = jnp.where(selw > 0.0, rank_ref[...] + off, -1.0)
    acc[...] = jnp.zeros_like(acc)

    def tile_body(t, carry):
        slot = t % 2
        c0 = t * TILE_ROWS

        @pl.when(t + 1 < n_tiles)
        def _():
            fetch(t + 1, 1 - slot)

        tile_wait(slot)
        w_scr[...] = jnp.zeros_like(w_scr)
        col = (lax.broadcasted_iota(jnp.int32, (RB, TILE_ROWS), 1) + c0).astype(F32)
        for e in range(N_EXPERTS):
            lo = seg_off_ref[rb, e]
            hi = lo + seg_rows_ref[rb, e]

            @pl.when((lo < c0 + TILE_ROWS) & (hi > c0))
            def _(e=e):
                w_scr[...] = jnp.where(tgt[:, e:e + 1] == col, selw[:, e:e + 1], w_scr[...])

        acc[...] += jnp.dot(w_scr[...].astype(BF16), ybuf[slot].astype(BF16),
                            preferred_element_type=F32)
        return carry

    lax.fori_loop(0, n_tiles, tile_body, 0)
    x2 = x1_ref[...] + acc[...]
    ms = jnp.mean(x2 * x2, axis=-1, keepdims=True)
    o_ref[...] = x2 * lax.rsqrt(ms + EPS) * g_ref[...]


def _combine(seg_start, seg_rows, seg_off, tot, x1, selw, rank, g, ye):
    gs = pltpu.PrefetchScalarGridSpec(
        num_scalar_prefetch=4,
        grid=(N_RB,),
        in_specs=[
            pl.BlockSpec((RB, D_MODEL), lambda rb, *_: (rb, 0)),
            pl.BlockSpec((RB, N_EXPERTS), lambda rb, *_: (rb, 0)),
            pl.BlockSpec((RB, N_EXPERTS), lambda rb, *_: (rb, 0)),
            pl.BlockSpec((1, D_MODEL), lambda rb, *_: (0, 0)),
            pl.BlockSpec(memory_space=pl.ANY),
        ],
        out_specs=pl.BlockSpec((RB, D_MODEL), lambda rb, *_: (rb, 0)),
        scratch_shapes=[
            pltpu.VMEM((RB, TILE_ROWS), F32),
            pltpu.VMEM((2, TILE_ROWS, D_MODEL), F32),
            pltpu.VMEM((RB, D_MODEL), F32),
            pltpu.SMEM((LIST_LEN,), jnp.int32),
            pltpu.SemaphoreType.DMA((2,)),
        ],
    )
    return pl.pallas_call(
        _combine_kernel,
        grid_spec=gs,
        out_shape=jax.ShapeDtypeStruct((N_TOK, D_MODEL), F32),
        compiler_params=_cparams("arbitrary"),
    )(seg_start, seg_rows, seg_off, tot, x1, selw, rank, g, ye)


def kernel(x_prompt, x_sample, norm_mix_g, w_in, w_fourier, sink, w_out, norm_ffn_g, w_router,
           w_gate, w_up, w_down, norm_final_g):
    x = jnp.concatenate([x_prompt.reshape(-1, D_MODEL), x_sample.reshape(-1, D_MODEL)], axis=0)
    cos_t, sin_t = _rope_tables()

    u, q, k, v = _inproj(x, norm_mix_g[0][None, :], w_in[0].astype(BF16), cos_t, sin_t)
    yf = _fourier(u, w_fourier[0].astype(BF16))
    ya = _attention(sink[0], q, k, v)
    x1, xn, aff3 = _outproj(x, yf, ya, w_out[0].astype(BF16), norm_ffn_g[0][None, :],
                            w_router[0].T.astype(BF16))

    aff_t = aff3.transpose(1, 0, 2).reshape(N_EXPERTS, N_TOK)
    thr0, need0 = _thresholds(aff_t[:, :GROUP_TOKENS[0]])
    thr1, need1 = _thresholds(aff_t[:, GROUP_TOKENS[0]:])
    thr = jnp.stack([thr0[:, 0], thr1[:, 0]])[:, None, :]
    need = jnp.stack([need0[:, 0], need1[:, 0]])[:, None, :]
    selw, rank, m = _select(aff_t.T, thr, need)

    seg_rows = ((m[:, 0, :].astype(jnp.int32) + CHUNK - 1) // CHUNK) * CHUNK
    seg_start = jnp.cumsum(seg_rows, axis=0) - seg_rows
    seg_off = jnp.cumsum(seg_rows, axis=1) - seg_rows
    tot = jnp.sum(seg_rows, axis=1)
    used = jnp.sum(seg_rows, axis=0)

    xe = _route_gather(seg_start, seg_rows, seg_off, tot, xn, selw.T, rank.T)
    ye = _ffn(used, xe, w_gate[0].astype(BF16), w_up[0].astype(BF16), w_down[0].astype(BF16))
    y = _combine(seg_start, seg_rows, seg_off, tot, x1, selw, rank, norm_final_g[None, :], ye)

    y_prompt = y[:SEQ].reshape(x_prompt.shape)
    y_sample = y[SEQ:].reshape(x_sample.shape)
    return (y_prompt, y_sample)
```

```python
import functools

import numpy as np
import jax
import jax.numpy as jnp
from jax import lax
from jax.experimental import pallas as pl
from jax.experimental.pallas import tpu as pltpu

F32 = jnp.float32
BF16 = jnp.bfloat16

D_MODEL = 1024
SEQ = 16384
N_SEQ = 3
N_TOK = N_SEQ * SEQ
HEAD_DIM = 64
N_HEADS = 12
N_KV_HEADS = 4
GQA_GROUP = N_HEADS // N_KV_HEADS
ATTN_WIDTH = N_HEADS * HEAD_DIM
KV_WIDTH = N_KV_HEADS * HEAD_DIM
FOURIER_WIDTH = 256
FOURIER_GROUP_DIM = 64
IN_WIDTH = FOURIER_WIDTH + ATTN_WIDTH + 2 * KV_WIDTH
WINDOW = 128
ROPE_THETA = 10000.0
N_EXPERTS = 16
EC_CAPACITY_FACTOR = 2
EXPERT_D_FF = 2048
EPS = 1e-6
NEG = -1e30

LANES = 128
FFT_R = 128
VMEM_LIMIT = 48 * 1024 * 1024

TM = 512
TQ = 512
QB = 128
FF_TM = 512
FF_FC = 512


def _cparams(*sem):
    return pltpu.CompilerParams(dimension_semantics=sem, vmem_limit_bytes=VMEM_LIMIT)


def _x_specs():
    n_p = SEQ // TM
    return [pl.BlockSpec((TM, D_MODEL), lambda i, *_: (jnp.minimum(i, n_p - 1), 0)),
            pl.BlockSpec((TM, D_MODEL), lambda i, *_: (jnp.maximum(i - n_p, 0), 0))]


def _x_tile(xp_ref, xs_ref):
    return jnp.where(pl.program_id(0) < SEQ // TM, xp_ref[...], xs_ref[...])


def _inproj_kernel(xp_ref, xs_ref, g_ref, w_ref, cos_ref, sin_ref, u_ref, q_ref, k_ref, v_ref):
    x = _x_tile(xp_ref, xs_ref)
    ms = jnp.mean(x * x, axis=-1, keepdims=True)
    hn = (x * lax.rsqrt(ms + EPS) * g_ref[...]).astype(BF16)
    proj = jnp.dot(hn, w_ref[...], preferred_element_type=F32)
    u_ref[...] = proj[:, :FOURIER_WIDTH]
    cos = cos_ref[...]
    sin = sin_ref[...]
    lane = lax.broadcasted_iota(jnp.int32, cos.shape, 1)
    first_half = (lane % HEAD_DIM) < (HEAD_DIM // 2)

    def rope(t):
        partner = jnp.where(first_half,
                            pltpu.roll(t, LANES - HEAD_DIM // 2, 1),
                            pltpu.roll(t, HEAD_DIM // 2, 1))
        return t * cos + partner * sin

    scale = HEAD_DIM ** -0.5
    for c in range(ATTN_WIDTH // LANES):
        lo = FOURIER_WIDTH + c * LANES
        q_ref[:, c * LANES:(c + 1) * LANES] = (rope(proj[:, lo:lo + LANES]) * scale).astype(BF16)
    for c in range(KV_WIDTH // LANES):
        lo = FOURIER_WIDTH + ATTN_WIDTH + c * LANES
        k_ref[:, c * LANES:(c + 1) * LANES] = rope(proj[:, lo:lo + LANES]).astype(BF16)
    v_ref[...] = proj[:, FOURIER_WIDTH + ATTN_WIDTH + KV_WIDTH:].astype(BF16)


def _inproj(xp, xs, g, w_in, cos_t, sin_t):
    n_pos_blocks = SEQ // TM
    return pl.pallas_call(
        _inproj_kernel,
        grid=(N_TOK // TM,),
        in_specs=_x_specs() + [
            pl.BlockSpec((1, D_MODEL), lambda i: (0, 0)),
            pl.BlockSpec((D_MODEL, IN_WIDTH), lambda i: (0, 0)),
            pl.BlockSpec((TM, LANES), lambda i: (i % n_pos_blocks, 0)),
            pl.BlockSpec((TM, LANES), lambda i: (i % n_pos_blocks, 0)),
        ],
        out_specs=[
            pl.BlockSpec((TM, FOURIER_WIDTH), lambda i: (i, 0)),
            pl.BlockSpec((TM, ATTN_WIDTH), lambda i: (i, 0)),
            pl.BlockSpec((TM, KV_WIDTH), lambda i: (i, 0)),
            pl.BlockSpec((TM, KV_WIDTH), lambda i: (i, 0)),
        ],
        out_shape=[
            jax.ShapeDtypeStruct((N_TOK, FOURIER_WIDTH), F32),
            jax.ShapeDtypeStruct((N_TOK, ATTN_WIDTH), BF16),
            jax.ShapeDtypeStruct((N_TOK, KV_WIDTH), BF16),
            jax.ShapeDtypeStruct((N_TOK, KV_WIDTH), BF16),
        ],
        compiler_params=_cparams("parallel"),
    )(xp, xs, g, w_in, cos_t, sin_t)


def _rope_tables():
    half = HEAD_DIM // 2
    inv_freq = 1.0 / (ROPE_THETA ** (jnp.arange(half, dtype=F32) / half))
    ang = jnp.arange(SEQ, dtype=F32)[:, None] * inv_freq[None, :]
    cos = jnp.cos(ang)
    sin = jnp.sin(ang)
    reps = LANES // HEAD_DIM
    cos_t = jnp.tile(jnp.concatenate([cos, cos], axis=-1), (1, reps))
    sin_t = jnp.tile(jnp.concatenate([-sin, sin], axis=-1), (1, reps))
    return cos_t, sin_t


def _dft_tables():
    r = FFT_R
    c = np.arange(FOURIER_GROUP_DIM)
    ang_c = 2.0 * np.pi * np.outer(c, c) / FOURIER_GROUP_DIM
    n_groups = FOURIER_WIDTH // FOURIER_GROUP_DIM
    eye = np.eye(n_groups)
    ch_scale = FOURIER_GROUP_DIM ** -0.5
    c_blk = np.kron(eye, np.cos(ang_c)) * ch_scale
    s_blk = np.kron(eye, np.sin(ang_c)) * ch_scale
    cs = np.concatenate([c_blk, -s_blk], axis=1)
    k = np.arange(r)
    ang_r = 2.0 * np.pi * np.outer(k, k) / r
    st_scale = r ** -0.5
    wr = np.cos(ang_r) * st_scale
    wi = -np.sin(ang_r) * st_scale
    wbig = np.block([[wr, -wi], [wi, wr]])
    ang_t = 2.0 * np.pi * np.outer(k, k) / (r * r)
    tr = np.cos(ang_t)
    ti = -np.sin(ang_t)
    f = lambda a: jnp.asarray(a, dtype=F32)
    return f(cs).astype(BF16), f(wbig).astype(BF16), f(wr), f(wi), f(tr), f(ti)


FS_SB = 8
FS_KB = 8


def _fft_stage1_kernel(u_ref, cs_ref, wbig_ref, y_ref):
    cs = cs_ref[...]
    wbig = wbig_ref[...]
    fw = FOURIER_WIDTH
    for j in range(FS_SB):
        u = u_ref[:, j * fw:(j + 1) * fw].astype(BF16)
        g = jnp.dot(u, cs, preferred_element_type=F32)
        gst = jnp.concatenate([g[:, :fw], g[:, fw:]], axis=0).astype(BF16)
        y = jnp.dot(wbig, gst, preferred_element_type=F32)
        y_ref[:, j * 2 * fw:j * 2 * fw + fw] = y[:FFT_R]
        y_ref[:, j * 2 * fw + fw:(j + 1) * 2 * fw] = y[FFT_R:]


def _fft_stage2_kernel(y_ref, wr_ref, wi_ref, tr_ref, ti_ref, wf_ref, o_ref):
    kb = pl.program_id(1)
    wr = wr_ref[...]
    wi = wi_ref[...]
    wf = wf_ref[...]
    fw = FOURIER_WIDTH
    for j in range(FS_KB):
        k1 = kb * FS_KB + j
        tr = tr_ref[pl.ds(k1, 1), :]
        ti = ti_ref[pl.ds(k1, 1), :]
        mr = wr * tr - wi * ti
        mi = wr * ti + wi * tr
        m2 = jnp.concatenate([mr, -mi], axis=1).astype(BF16)
        yb = y_ref[j * FFT_R:(j + 1) * FFT_R, :]
        yst = jnp.concatenate([yb[:, :fw], yb[:, fw:]], axis=0).astype(BF16)
        z = jnp.dot(m2, yst, preferred_element_type=F32)
        yf = jnp.dot(z.astype(BF16), wf, preferred_element_type=F32)
        o_ref[:, j * fw:(j + 1) * fw] = yf.astype(BF16)


def _fourier(u, w_fourier):
    cs, wbig, wr, wi, tr, ti = _dft_tables()
    r, fw = FFT_R, FOURIER_WIDTH
    u2 = u.reshape(N_SEQ * r, r * fw)
    y = pl.pallas_call(
        _fft_stage1_kernel,
        grid=(N_SEQ, r // FS_SB),
        in_specs=[
            pl.BlockSpec((r, FS_SB * fw), lambda s, j: (s, j)),
            pl.BlockSpec((fw, 2 * fw), lambda s, j: (0, 0)),
            pl.BlockSpec((2 * r, 2 * r), lambda s, j: (0, 0)),
        ],
        out_specs=pl.BlockSpec((r, FS_SB * 2 * fw), lambda s, j: (s, j)),
        out_shape=jax.ShapeDtypeStruct((N_SEQ * r, r * 2 * fw), F32),
        compiler_params=_cparams("parallel", "parallel"),
    )(u2, cs, wbig)
    y2 = y.reshape(N_SEQ * r * r, 2 * fw)
    full = lambda s, j: (0, 0)
    yf = pl.pallas_call(
        _fft_stage2_kernel,
        grid=(N_SEQ, r // FS_KB),
        in_specs=[
            pl.BlockSpec((FS_KB * r, 2 * fw), lambda s, j: (s * (r // FS_KB) + j, 0)),
            pl.BlockSpec((r, r), full),
            pl.BlockSpec((r, r), full),
            pl.BlockSpec((r, r), full),
            pl.BlockSpec((r, r), full),
            pl.BlockSpec((fw, fw), full),
        ],
        out_specs=pl.BlockSpec((r, FS_KB * fw), lambda s, j: (s, j)),
        out_shape=jax.ShapeDtypeStruct((N_SEQ * r, r * fw), BF16),
        compiler_params=_cparams("parallel", "parallel"),
    )(y2, wr, wi, tr, ti, w_fourier)
    return yf.reshape(N_TOK, fw)


def _attn_kernel(sink_ref, q_ref, kp_ref, kc_ref, kn_ref, vp_ref, vc_ref, vn_ref, o_ref,
                 kext, vext):
    i = pl.program_id(1)
    kext[0:QB, :] = kp_ref[...]
    kext[QB:QB + TQ, :] = kc_ref[...]
    kext[QB + TQ:, :] = kn_ref[...]
    vext[0:QB, :] = vp_ref[...]
    vext[QB:QB + TQ, :] = vc_ref[...]
    vext[QB + TQ:, :] = vn_ref[...]

    rows = GQA_GROUP * QB
    nkeys = 3 * QB
    a = lax.broadcasted_iota(jnp.int32, (rows, nkeys), 0) % QB
    j = lax.broadcasted_iota(jnp.int32, (rows, nkeys), 1)
    rel = j - QB - a
    band = (rel <= WINDOW) & (rel >= -WINDOW)
    row_id = lax.broadcasted_iota(jnp.int32, (rows, 1), 0)

    for sb in range(TQ // QB):
        kpos = (i * (TQ // QB) + sb - 1) * QB + j
        valid = band & (kpos >= 0) & (kpos < SEQ)
        for h in range(N_KV_HEADS):
            heads = [GQA_GROUP * h + g for g in range(GQA_GROUP)]
            qs = jnp.concatenate(
                [q_ref[sb * QB:(sb + 1) * QB, hd * HEAD_DIM:(hd + 1) * HEAD_DIM] for hd in heads],
                axis=0)
            kh = kext[sb * QB:sb * QB + nkeys, h * HEAD_DIM:(h + 1) * HEAD_DIM]
            vh = vext[sb * QB:sb * QB + nkeys, h * HEAD_DIM:(h + 1) * HEAD_DIM]
            s = lax.dot_general(qs, kh, (((1,), (1,)), ((), ())), preferred_element_type=F32)
            s = jnp.where(valid, s, NEG)
            sink = jnp.where(row_id < QB, sink_ref[heads[0]],
                             jnp.where(row_id < 2 * QB, sink_ref[heads[1]], sink_ref[heads[2]]))
            m = jnp.maximum(jnp.max(s, axis=-1, keepdims=True), sink)
            p = jnp.exp(s - m)
            denom = jnp.sum(p, axis=-1, keepdims=True) + jnp.exp(sink - m)
            o = jnp.dot(p.astype(BF16), vh, preferred_element_type=F32) * (1.0 / denom)
            for g, hd in enumerate(heads):
                o_ref[sb * QB:(sb + 1) * QB, hd * HEAD_DIM:(hd + 1) * HEAD_DIM] = (
                    o[g * QB:(g + 1) * QB].astype(BF16))


def _attention(sink, q, k, v):
    nqb = SEQ // QB
    per = TQ // QB
    cur = lambda s, i: (s * (SEQ // TQ) + i, 0)
    prev = lambda s, i: (s * nqb + jnp.maximum(i * per - 1, 0), 0)
    nxt = lambda s, i: (s * nqb + jnp.minimum(i * per + per, nqb - 1), 0)
    return pl.pallas_call(
        _attn_kernel,
        grid=(N_SEQ, SEQ // TQ),
        in_specs=[
            pl.BlockSpec(memory_space=pltpu.SMEM),
            pl.BlockSpec((TQ, ATTN_WIDTH), cur),
            pl.BlockSpec((QB, KV_WIDTH), prev),
            pl.BlockSpec((TQ, KV_WIDTH), cur),
            pl.BlockSpec((QB, KV_WIDTH), nxt),
            pl.BlockSpec((QB, KV_WIDTH), prev),
            pl.BlockSpec((TQ, KV_WIDTH), cur),
            pl.BlockSpec((QB, KV_WIDTH), nxt),
        ],
        out_specs=pl.BlockSpec((TQ, ATTN_WIDTH), cur),
        out_shape=jax.ShapeDtypeStruct((N_TOK, ATTN_WIDTH), BF16),
        scratch_shapes=[pltpu.VMEM((TQ + 2 * QB, KV_WIDTH), BF16),
                        pltpu.VMEM((TQ + 2 * QB, KV_WIDTH), BF16)],
        compiler_params=_cparams("parallel", "parallel"),
    )(sink, q, k, k, k, v, v, v)


def _outproj_kernel(xp_ref, xs_ref, yf_ref, ya_ref, wo_ref, g_ref, wr_ref, x1_ref, xn_ref, aff_ref):
    mix = jnp.dot(yf_ref[...], wo_ref[:FOURIER_WIDTH, :], preferred_element_type=F32)
    mix = mix + jnp.dot(ya_ref[...], wo_ref[FOURIER_WIDTH:, :], preferred_element_type=F32)
    x1 = _x_tile(xp_ref, xs_ref) + mix
    x1_ref[...] = x1
    ms = jnp.mean(x1 * x1, axis=-1, keepdims=True)
    xn = (x1 * lax.rsqrt(ms + EPS) * g_ref[...]).astype(BF16)
    xn_ref[...] = xn
    logits = lax.dot_general(wr_ref[...], xn, (((1,), (1,)), ((), ())), preferred_element_type=F32)
    mx = jnp.max(logits, axis=0, keepdims=True)
    ex = jnp.exp(logits - mx)
    aff_ref[...] = ex / jnp.sum(ex, axis=0, keepdims=True)


def _outproj(xp, xs, yf, ya, w_out, g, w_router_t):
    return pl.pallas_call(
        _outproj_kernel,
        grid=(N_TOK // TM,),
        in_specs=_x_specs() + [
            pl.BlockSpec((TM, FOURIER_WIDTH), lambda i: (i, 0)),
            pl.BlockSpec((TM, ATTN_WIDTH), lambda i: (i, 0)),
            pl.BlockSpec((D_MODEL, D_MODEL), lambda i: (0, 0)),
            pl.BlockSpec((1, D_MODEL), lambda i: (0, 0)),
            pl.BlockSpec((N_EXPERTS, D_MODEL), lambda i: (0, 0)),
        ],
        out_specs=[
            pl.BlockSpec((TM, D_MODEL), lambda i: (i, 0)),
            pl.BlockSpec((TM, D_MODEL), lambda i: (i, 0)),
            pl.BlockSpec((N_EXPERTS, TM), lambda i: (0, i)),
        ],
        out_shape=[
            jax.ShapeDtypeStruct((N_TOK, D_MODEL), F32),
            jax.ShapeDtypeStruct((N_TOK, D_MODEL), BF16),
            jax.ShapeDtypeStruct((N_EXPERTS, N_TOK), F32),
        ],
        compiler_params=_cparams("parallel"),
    )(xp, xs, yf, ya, w_out, g, w_router_t)


RB = 512
N_RB = N_TOK // RB
GROUP_TOKENS = (SEQ, 2 * SEQ)
GROUP0_BLOCKS = GROUP_TOKENS[0] // RB
CHUNK = 8
TILE_ROWS = 256
CHUNKS_PER_TILE = TILE_ROWS // CHUNK
SLOT_TILES = 14
SLOT_ROWS = SLOT_TILES * FF_TM
assert SLOT_ROWS >= EC_CAPACITY_FACTOR * N_TOK // N_EXPERTS + N_RB * (CHUNK - 1)
TRASH_ROW = N_EXPERTS * SLOT_ROWS
MAX_CHUNKS = (N_EXPERTS * (RB + CHUNK - 1)) // CHUNK
LIST_LEN = ((MAX_CHUNKS + CHUNKS_PER_TILE - 1) // CHUNKS_PER_TILE) * CHUNKS_PER_TILE


def _thresh_kernel(aff_ref, thr_ref, need_ref, *, cap):
    aff = aff_ref[...]

    def body(i, cur):
        cand = cur | jnp.left_shift(jnp.int32(1), 30 - i)
        cnt = jnp.sum(jnp.where(aff >= pltpu.bitcast(cand, F32), 1.0, 0.0), axis=1, keepdims=True)
        return jnp.where(cnt >= cap, cand, cur)

    thr = pltpu.bitcast(lax.fori_loop(0, 31, body, jnp.zeros((N_EXPERTS, 1), jnp.int32)), F32)
    n_gt = jnp.sum(jnp.where(aff > thr, 1.0, 0.0), axis=1, keepdims=True)
    thr_ref[...] = jnp.broadcast_to(thr, thr_ref.shape)
    need_ref[...] = jnp.broadcast_to(cap - n_gt, need_ref.shape)


def _thresholds(aff_group):
    n = aff_group.shape[1]
    cap = EC_CAPACITY_FACTOR * n // N_EXPERTS
    return pl.pallas_call(
        functools.partial(_thresh_kernel, cap=float(cap)),
        out_shape=[jax.ShapeDtypeStruct((N_EXPERTS, LANES), F32),
                   jax.ShapeDtypeStruct((N_EXPERTS, LANES), F32)],
        compiler_params=pltpu.CompilerParams(vmem_limit_bytes=VMEM_LIMIT),
    )(aff_group)


def _select_kernel(aff_ref, thr_ref, need_ref, utri_ref, selw_ref, rank_ref, m_ref, eq_seen):
    rb = pl.program_id(0)

    @pl.when((rb == 0) | (rb == GROUP0_BLOCKS))
    def _():
        eq_seen[...] = jnp.zeros_like(eq_seen)

    aff = aff_ref[...]
    thr = thr_ref[:, 0:1]
    need = need_ref[:, 0:1]
    utri = utri_ref[...]
    eq = jnp.where(aff == thr, 1.0, 0.0)
    eq_before = jnp.dot(eq.astype(BF16), utri, preferred_element_type=F32) + eq_seen[...]
    sel = (aff > thr) | ((aff == thr) & (eq_before < need))
    eq_seen[...] += jnp.sum(eq, axis=1, keepdims=True)
    self = jnp.where(sel, 1.0, 0.0)
    rank_ref[...] = jnp.dot(self.astype(BF16), utri, preferred_element_type=F32)
    selw_ref[...] = jnp.where(sel, aff, 0.0)
    m_ref[0] = jnp.broadcast_to(jnp.sum(self, axis=1, keepdims=True), (N_EXPERTS, LANES))


def _select(aff_t, thr, need):
    utri = jnp.asarray(np.triu(np.ones((RB, RB), np.float32), 1), dtype=BF16)
    grp = lambda rb: (jnp.where(rb >= GROUP0_BLOCKS, 1, 0), 0)
    return pl.pallas_call(
        _select_kernel,
        grid=(N_RB,),
        in_specs=[
            pl.BlockSpec((N_EXPERTS, RB), lambda rb: (0, rb)),
            pl.BlockSpec((N_EXPERTS, LANES), grp),
            pl.BlockSpec((N_EXPERTS, LANES), grp),
            pl.BlockSpec((RB, RB), lambda rb: (0, 0)),
        ],
        out_specs=[
            pl.BlockSpec((N_EXPERTS, RB), lambda rb: (0, rb)),
            pl.BlockSpec((N_EXPERTS, RB), lambda rb: (0, rb)),
            pl.BlockSpec((1, N_EXPERTS, LANES), lambda rb: (rb, 0, 0)),
        ],
        out_shape=[
            jax.ShapeDtypeStruct((N_EXPERTS, N_TOK), F32),
            jax.ShapeDtypeStruct((N_EXPERTS, N_TOK), F32),
            jax.ShapeDtypeStruct((N_RB, N_EXPERTS, LANES), F32),
        ],
        scratch_shapes=[pltpu.VMEM((N_EXPERTS, 1), F32)],
        compiler_params=_cparams("arbitrary"),
    )(aff_t, thr, need, utri)


def _fill_chunk_list(list_ref, rb, seg_start_ref, seg_rows_ref):
    cnt = jnp.int32(0)
    for e in range(N_EXPERTS):
        base = e * SLOT_ROWS + seg_start_ref[rb, e]

        def body(k, c, base=base):
            list_ref[c] = base + CHUNK * k
            return c + 1

        cnt = lax.fori_loop(0, seg_rows_ref[rb, e] // CHUNK, body, cnt)
    return cnt


def _gather_kernel(seg_start_ref, seg_rows_ref, seg_off_ref, tot_ref,
                   x_ref, selw_ref, rank_ref, xe_hbm,
                   p_scr, obuf, zbuf, dst_list, pending, sem, zsem):
    rb = pl.program_id(0)
    n_rows = tot_ref[rb]
    n_chunks = n_rows // CHUNK
    n_tiles = (n_rows + TILE_ROWS - 1) // TILE_ROWS

    @pl.when(rb == 0)
    def _():
        pending[0] = 0
        pending[1] = 0

    _fill_chunk_list(dst_list, rb, seg_start_ref, seg_rows_ref)

    def tile_copy(slot):
        return pltpu.make_async_copy(obuf.at[slot], xe_hbm.at[pl.ds(0, TILE_ROWS)], sem.at[slot])

    e_iota = lax.broadcasted_iota(jnp.int32, (N_EXPERTS, 1), 0)
    off = jnp.zeros((N_EXPERTS, 1), F32)
    for e in range(N_EXPERTS):
        off = jnp.where(e_iota == e, seg_off_ref[rb, e].astype(F32), off)
    tgt = jnp.where(selw_ref[...] > 0.0, rank_ref[...] + off, -1.0)
    x = x_ref[...]

    def tile_body(t, carry):
        slot = t % 2
        r0 = t * TILE_ROWS

        @pl.when(pending[slot] == 1)
        def _():
            tile_copy(slot).wait()

        p_scr[...] = jnp.zeros_like(p_scr)
        row = (lax.broadcasted_iota(jnp.int32, (TILE_ROWS, RB), 0) + r0).astype(F32)
        for e in range(N_EXPERTS):
            lo = seg_off_ref[rb, e]
            hi = lo + seg_rows_ref[rb, e]

            @pl.when((lo < r0 + TILE_ROWS) & (hi > r0))
            def _(e=e):
                p_scr[...] = jnp.where(tgt[e:e + 1, :] == row, 1.0, p_scr[...])

        obuf[slot] = jnp.dot(p_scr[...].astype(BF16), x, preferred_element_type=F32)

        def issue(i, c):
            g = t * CHUNKS_PER_TILE + i
            dst = jnp.where(g < n_chunks, dst_list[jnp.minimum(g, n_chunks - 1)],
                            TRASH_ROW + slot * TILE_ROWS + i * CHUNK)
            pltpu.make_async_copy(obuf.at[slot, pl.ds(pl.multiple_of(i * CHUNK, CHUNK), CHUNK)],
                                  xe_hbm.at[pl.ds(pl.multiple_of(dst, CHUNK), CHUNK)],
                                  sem.at[slot]).start()
            return c

        lax.fori_loop(0, CHUNKS_PER_TILE, issue, 0)
        pending[slot] = 1
        return carry

    lax.fori_loop(0, n_tiles, tile_body, 0)

    @pl.when(rb == N_RB - 1)
    def _():
        for slot in range(2):
            @pl.when(pending[slot] == 1)
            def _(slot=slot):
                tile_copy(slot).wait()
        zbuf[...] = jnp.zeros_like(zbuf)
        for e in range(N_EXPERTS):
            used = seg_start_ref[rb, e] + seg_rows_ref[rb, e]
            end = ((used + FF_TM - 1) // FF_TM) * FF_TM

            def zcopy(k, used=used, e=e):
                return pltpu.make_async_copy(
                    zbuf, xe_hbm.at[pl.ds(pl.multiple_of(e * SLOT_ROWS + used + k * CHUNK, CHUNK), CHUNK)],
                    zsem)

            def zstart(k, c, zcopy=zcopy):
                zcopy(k).start()
                return c

            def zwait(k, c, zcopy=zcopy):
                zcopy(k).wait()
                return c

            lax.fori_loop(0, (end - used) // CHUNK, zstart, 0)
            lax.fori_loop(0, (end - used) // CHUNK, zwait, 0)


def _route_gather(seg_start, seg_rows, seg_off, tot, xn, selw_t, rank_t):
    gs = pltpu.PrefetchScalarGridSpec(
        num_scalar_prefetch=4,
        grid=(N_RB,),
        in_specs=[
            pl.BlockSpec((RB, D_MODEL), lambda rb, *_: (rb, 0)),
            pl.BlockSpec((N_EXPERTS, RB), lambda rb, *_: (0, rb)),
            pl.BlockSpec((N_EXPERTS, RB), lambda rb, *_: (0, rb)),
        ],
        out_specs=pl.BlockSpec(memory_space=pl.ANY),
        scratch_shapes=[
            pltpu.VMEM((TILE_ROWS, RB), F32),
            pltpu.VMEM((2, TILE_ROWS, D_MODEL), F32),
            pltpu.VMEM((CHUNK, D_MODEL), F32),
            pltpu.SMEM((LIST_LEN,), jnp.int32),
            pltpu.SMEM((2,), jnp.int32),
            pltpu.SemaphoreType.DMA((2,)),
            pltpu.SemaphoreType.DMA,
        ],
    )
    return pl.pallas_call(
        _gather_kernel,
        grid_spec=gs,
        out_shape=jax.ShapeDtypeStruct((TRASH_ROW + 2 * TILE_ROWS, D_MODEL), F32),
        compiler_params=_cparams("arbitrary"),
    )(seg_start, seg_rows, seg_off, tot, xn, selw_t, rank_t)


def _ffn_kernel(used_ref, x_ref, wg_ref, wu_ref, wd_ref, o_ref):
    e = pl.program_id(0)
    i = pl.program_id(1)

    @pl.when(i * FF_TM < used_ref[e])
    def _():
        x = x_ref[...].astype(BF16)
        acc = jnp.zeros((FF_TM, D_MODEL), F32)
        for c in range(EXPERT_D_FF // FF_FC):
            gt = jnp.dot(x, wg_ref[0, :, c * FF_FC:(c + 1) * FF_FC], preferred_element_type=F32)
            up = jnp.dot(x, wu_ref[0, :, c * FF_FC:(c + 1) * FF_FC], preferred_element_type=F32)
            h = (gt * (1.0 / (1.0 + jnp.exp(-gt))) * up).astype(BF16)
            acc = acc + jnp.dot(h, wd_ref[0, c * FF_FC:(c + 1) * FF_FC, :], preferred_element_type=F32)
        o_ref[...] = acc


def _ffn(used, xe, w_gate, w_up, w_down):
    def rows(e, i, used_ref):
        last = jnp.maximum((used_ref[e] + FF_TM - 1) // FF_TM - 1, 0)
        return (e * SLOT_TILES + jnp.minimum(i, last), 0)

    gs = pltpu.PrefetchScalarGridSpec(
        num_scalar_prefetch=1,
        grid=(N_EXPERTS, SLOT_TILES),
        in_specs=[
            pl.BlockSpec((FF_TM, D_MODEL), rows),
            pl.BlockSpec((1, D_MODEL, EXPERT_D_FF), lambda e, i, u: (e, 0, 0)),
            pl.BlockSpec((1, D_MODEL, EXPERT_D_FF), lambda e, i, u: (e, 0, 0)),
            pl.BlockSpec((1, EXPERT_D_FF, D_MODEL), lambda e, i, u: (e, 0, 0)),
        ],
        out_specs=pl.BlockSpec((FF_TM, D_MODEL), rows),
    )
    return pl.pallas_call(
        _ffn_kernel,
        grid_spec=gs,
        out_shape=jax.ShapeDtypeStruct((N_EXPERTS * SLOT_ROWS, D_MODEL), F32),
        compiler_params=_cparams("arbitrary", "arbitrary"),
    )(used, xe, w_gate, w_up, w_down)


def _combine_kernel(seg_start_ref, seg_rows_ref, seg_off_ref, tot_ref,
                    x1_ref, selw_ref, rank_ref, g_ref, ye_hbm, op_ref, os_ref,
                    w_scr, ybuf, acc, src_list, sem):
    rb = pl.program_id(0)
    n_rows = tot_ref[rb]
    n_chunks = n_rows // CHUNK
    n_tiles = (n_rows + TILE_ROWS - 1) // TILE_ROWS

    _fill_chunk_list(src_list, rb, seg_start_ref, seg_rows_ref)

    def fetch(t, slot):
        def issue(i, c):
            g = t * CHUNKS_PER_TILE + i
            src = jnp.where(g < n_chunks, src_list[jnp.minimum(g, n_chunks - 1)], 0)
            pltpu.make_async_copy(ye_hbm.at[pl.ds(pl.multiple_of(src, CHUNK), CHUNK)],
                                  ybuf.at[slot, pl.ds(pl.multiple_of(i * CHUNK, CHUNK), CHUNK)],
                                  sem.at[slot]).start()
            return c

        lax.fori_loop(0, CHUNKS_PER_TILE, issue, 0)

    def tile_wait(slot):
        pltpu.make_async_copy(ye_hbm.at[pl.ds(0, TILE_ROWS)], ybuf.at[slot], sem.at[slot]).wait()

    @pl.when(n_tiles > 0)
    def _():
        fetch(0, 0)

    e_iota = lax.broadcasted_iota(jnp.int32, (N_EXPERTS, 1), 0)
    off = jnp.zeros((N_EXPERTS, 1), F32)
    for e in range(N_EXPERTS):
        off = jnp.where(e_iota == e, seg_off_ref[rb, e].astype(F32), off)
    selw = selw_ref[...]
    tgt = jnp.where(selw > 0.0, rank_ref[...] + off, -1.0)
    acc[...] = jnp.zeros_like(acc)

    def tile_body(t, carry):
        slot = t % 2
        r0 = t * TILE_ROWS

        @pl.when(t + 1 < n_tiles)
        def _():
            fetch(t + 1, 1 - slot)

        tile_wait(slot)
        w_scr[...] = jnp.zeros_like(w_scr)
        row = (lax.broadcasted_iota(jnp.int32, (TILE_ROWS, RB), 0) + r0).astype(F32)
        for e in range(N_EXPERTS):
            lo = seg_off_ref[rb, e]
            hi = lo + seg_rows_ref[rb, e]

            @pl.when((lo < r0 + TILE_ROWS) & (hi > r0))
            def _(e=e):
                w_scr[...] = jnp.where(tgt[e:e + 1, :] == row, selw[e:e + 1, :], w_scr[...])

        acc[...] += lax.dot_general(w_scr[...].astype(BF16), ybuf[slot].astype(BF16),
                                    (((0,), (0,)), ((), ())), preferred_element_type=F32)
        return carry

    lax.fori_loop(0, n_tiles, tile_body, 0)
    x2 = x1_ref[...] + acc[...]
    ms = jnp.mean(x2 * x2, axis=-1, keepdims=True)
    y = x2 * lax.rsqrt(ms + EPS) * g_ref[...]

    @pl.when(rb < GROUP0_BLOCKS)
    def _():
        op_ref[...] = y

    @pl.when(rb >= GROUP0_BLOCKS)
    def _():
        os_ref[...] = y


def _combine(seg_start, seg_rows, seg_off, tot, x1, selw_t, rank_t, g, ye):
    gs = pltpu.PrefetchScalarGridSpec(
        num_scalar_prefetch=4,
        grid=(N_RB,),
        in_specs=[
            pl.BlockSpec((RB, D_MODEL), lambda rb, *_: (rb, 0)),
            pl.BlockSpec((N_EXPERTS, RB), lambda rb, *_: (0, rb)),
            pl.BlockSpec((N_EXPERTS, RB), lambda rb, *_: (0, rb)),
            pl.BlockSpec((1, D_MODEL), lambda rb, *_: (0, 0)),
            pl.BlockSpec(memory_space=pl.ANY),
        ],
        out_specs=[
            pl.BlockSpec((RB, D_MODEL), lambda rb, *_: (jnp.minimum(rb, GROUP0_BLOCKS - 1), 0)),
            pl.BlockSpec((RB, D_MODEL), lambda rb, *_: (jnp.maximum(rb - GROUP0_BLOCKS, 0), 0)),
        ],
        scratch_shapes=[
            pltpu.VMEM((TILE_ROWS, RB), F32),
            pltpu.VMEM((2, TILE_ROWS, D_MODEL), F32),
            pltpu.VMEM((RB, D_MODEL), F32),
            pltpu.SMEM((LIST_LEN,), jnp.int32),
            pltpu.SemaphoreType.DMA((2,)),
        ],
    )
    return pl.pallas_call(
        _combine_kernel,
        grid_spec=gs,
        out_shape=[jax.ShapeDtypeStruct((GROUP_TOKENS[0], D_MODEL), F32),
                   jax.ShapeDtypeStruct((GROUP_TOKENS[1], D_MODEL), F32)],
        compiler_params=_cparams("arbitrary"),
    )(seg_start, seg_rows, seg_off, tot, x1, selw_t, rank_t, g, ye)


def kernel(x_prompt, x_sample, norm_mix_g, w_in, w_fourier, sink, w_out, norm_ffn_g, w_router,
           w_gate, w_up, w_down, norm_final_g):
    xp = x_prompt.reshape(-1, D_MODEL)
    xs = x_sample.reshape(-1, D_MODEL)
    cos_t, sin_t = _rope_tables()

    u, q, k, v = _inproj(xp, xs, norm_mix_g[0][None, :], w_in[0].astype(BF16), cos_t, sin_t)
    yf = _fourier(u, w_fourier[0].astype(BF16))
    ya = _attention(sink[0], q, k, v)
    x1, xn, aff_t = _outproj(xp, xs, yf, ya, w_out[0].astype(BF16), norm_ffn_g[0][None, :],
                             w_router[0].T.astype(BF16))

    thr0, need0 = _thresholds(aff_t[:, :GROUP_TOKENS[0]])
    thr1, need1 = _thresholds(aff_t[:, GROUP_TOKENS[0]:])
    selw_t, rank_t, m = _select(aff_t, jnp.concatenate([thr0, thr1]), jnp.concatenate([need0, need1]))

    seg_rows = ((m[:, :, 0].astype(jnp.int32) + CHUNK - 1) // CHUNK) * CHUNK
    seg_start = jnp.cumsum(seg_rows, axis=0) - seg_rows
    seg_off = jnp.cumsum(seg_rows, axis=1) - seg_rows
    tot = jnp.sum(seg_rows, axis=1)
    used = jnp.sum(seg_rows, axis=0)

    xe = _route_gather(seg_start, seg_rows, seg_off, tot, xn, selw_t, rank_t)
    ye = _ffn(used, xe, w_gate[0].astype(BF16), w_up[0].astype(BF16), w_down[0].astype(BF16))
    yp, ys = _combine(seg_start, seg_rows, seg_off, tot, x1, selw_t, rank_t, norm_final_g[None, :], ye)
    return (yp.reshape(x_prompt.shape), ys.reshape(x_sample.shape))
```

```python
import functools

import numpy as np
import jax
import jax.numpy as jnp
from jax import lax
from jax.experimental import pallas as pl
from jax.experimental.pallas import tpu as pltpu

F32 = jnp.float32
BF16 = jnp.bfloat16

D_MODEL = 1024
SEQ = 16384
N_SEQ = 3
N_TOK = N_SEQ * SEQ
HEAD_DIM = 64
N_HEADS = 12
N_KV_HEADS = 4
GQA_GROUP = N_HEADS // N_KV_HEADS
ATTN_WIDTH = N_HEADS * HEAD_DIM
KV_WIDTH = N_KV_HEADS * HEAD_DIM
FOURIER_WIDTH = 256
FOURIER_GROUP_DIM = 64
IN_WIDTH = FOURIER_WIDTH + ATTN_WIDTH + 2 * KV_WIDTH
WINDOW = 128
ROPE_THETA = 10000.0
N_EXPERTS = 16
EC_CAPACITY_FACTOR = 2
EXPERT_D_FF = 2048
EPS = 1e-6
NEG = -1e30

LANES = 128
FFT_R = 128
VMEM_LIMIT = 48 * 1024 * 1024

TM = 512
TQ = 512
QB = 128
FF_TM = 512
FF_FC = 512


def _cparams(*sem):
    return pltpu.CompilerParams(dimension_semantics=sem, vmem_limit_bytes=VMEM_LIMIT)


def _x_specs():
    n_p = SEQ // TM
    return [pl.BlockSpec((TM, D_MODEL), lambda i, *_: (jnp.minimum(i, n_p - 1), 0)),
            pl.BlockSpec((TM, D_MODEL), lambda i, *_: (jnp.maximum(i - n_p, 0), 0))]


def _x_tile(xp_ref, xs_ref):
    return jnp.where(pl.program_id(0) < SEQ // TM, xp_ref[...], xs_ref[...])


def _inproj_kernel(xp_ref, xs_ref, g_ref, w_ref, cos_ref, sin_ref, u_ref, q_ref, k_ref, v_ref):
    x = _x_tile(xp_ref, xs_ref)
    ms = jnp.mean(x * x, axis=-1, keepdims=True)
    hn = (x * lax.rsqrt(ms + EPS) * g_ref[...]).astype(BF16)
    proj = jnp.dot(hn, w_ref[...], preferred_element_type=F32)
    u_ref[...] = proj[:, :FOURIER_WIDTH].astype(BF16)
    cos = cos_ref[...]
    sin = sin_ref[...]
    lane = lax.broadcasted_iota(jnp.int32, cos.shape, 1)
    first_half = (lane % HEAD_DIM) < (HEAD_DIM // 2)

    def rope(t):
        partner = jnp.where(first_half,
                            pltpu.roll(t, LANES - HEAD_DIM // 2, 1),
                            pltpu.roll(t, HEAD_DIM // 2, 1))
        return t * cos + partner * sin

    scale = HEAD_DIM ** -0.5
    for c in range(ATTN_WIDTH // LANES):
        lo = FOURIER_WIDTH + c * LANES
        q_ref[:, c * LANES:(c + 1) * LANES] = (rope(proj[:, lo:lo + LANES]) * scale).astype(BF16)
    for c in range(KV_WIDTH // LANES):
        lo = FOURIER_WIDTH + ATTN_WIDTH + c * LANES
        k_ref[:, c * LANES:(c + 1) * LANES] = rope(proj[:, lo:lo + LANES]).astype(BF16)
    v_ref[...] = proj[:, FOURIER_WIDTH + ATTN_WIDTH + KV_WIDTH:].astype(BF16)


def _inproj(xp, xs, g, w_in, cos_t, sin_t):
    n_pos_blocks = SEQ // TM
    return pl.pallas_call(
        _inproj_kernel,
        grid=(N_TOK // TM,),
        in_specs=_x_specs() + [
            pl.BlockSpec((1, D_MODEL), lambda i: (0, 0)),
            pl.BlockSpec((D_MODEL, IN_WIDTH), lambda i: (0, 0)),
            pl.BlockSpec((TM, LANES), lambda i: (i % n_pos_blocks, 0)),
            pl.BlockSpec((TM, LANES), lambda i: (i % n_pos_blocks, 0)),
        ],
        out_specs=[
            pl.BlockSpec((TM, FOURIER_WIDTH), lambda i: (i, 0)),
            pl.BlockSpec((TM, ATTN_WIDTH), lambda i: (i, 0)),
            pl.BlockSpec((TM, KV_WIDTH), lambda i: (i, 0)),
            pl.BlockSpec((TM, KV_WIDTH), lambda i: (i, 0)),
        ],
        out_shape=[
            jax.ShapeDtypeStruct((N_TOK, FOURIER_WIDTH), BF16),
            jax.ShapeDtypeStruct((N_TOK, ATTN_WIDTH), BF16),
            jax.ShapeDtypeStruct((N_TOK, KV_WIDTH), BF16),
            jax.ShapeDtypeStruct((N_TOK, KV_WIDTH), BF16),
        ],
        compiler_params=_cparams("parallel"),
    )(xp, xs, g, w_in, cos_t, sin_t)


def _rope_tables():
    half = HEAD_DIM // 2
    inv_freq = 1.0 / (ROPE_THETA ** (jnp.arange(half, dtype=F32) / half))
    ang = jnp.arange(SEQ, dtype=F32)[:, None] * inv_freq[None, :]
    cos = jnp.cos(ang)
    sin = jnp.sin(ang)
    reps = LANES // HEAD_DIM
    cos_t = jnp.tile(jnp.concatenate([cos, cos], axis=-1), (1, reps))
    sin_t = jnp.tile(jnp.concatenate([-sin, sin], axis=-1), (1, reps))
    return cos_t, sin_t


def _dft_tables():
    r = FFT_R
    c = np.arange(FOURIER_GROUP_DIM)
    ang_c = 2.0 * np.pi * np.outer(c, c) / FOURIER_GROUP_DIM
    n_groups = FOURIER_WIDTH // FOURIER_GROUP_DIM
    eye = np.eye(n_groups)
    ch_scale = FOURIER_GROUP_DIM ** -0.5
    c_blk = np.kron(eye, np.cos(ang_c)) * ch_scale
    s_blk = np.kron(eye, np.sin(ang_c)) * ch_scale
    cs = np.concatenate([c_blk, -s_blk], axis=1)
    k = np.arange(r)
    ang_r = 2.0 * np.pi * np.outer(k, k) / r
    st_scale = r ** -0.5
    wr = np.cos(ang_r) * st_scale
    wi = -np.sin(ang_r) * st_scale
    wbig = np.block([[wr, -wi], [wi, wr]])
    ang_t = 2.0 * np.pi * np.outer(k, k) / (r * r)
    tr = np.cos(ang_t)
    ti = -np.sin(ang_t)
    f = lambda a: jnp.asarray(a, dtype=F32)
    return f(cs).astype(BF16), f(wbig).astype(BF16), f(wr), f(wi), f(tr), f(ti)


FS_SB = 8
FS_KB = 8


def _fft_stage1_kernel(u_ref, cs_ref, wbig_ref, y_ref):
    cs = cs_ref[...]
    wbig = wbig_ref[...]
    fw = FOURIER_WIDTH
    for j in range(FS_SB):
        u = u_ref[:, j * fw:(j + 1) * fw]
        g = jnp.dot(u, cs, preferred_element_type=F32)
        gst = jnp.concatenate([g[:, :fw], g[:, fw:]], axis=0).astype(BF16)
        y = jnp.dot(wbig, gst, preferred_element_type=F32)
        y_ref[:, j * 2 * fw:j * 2 * fw + fw] = y[:FFT_R].astype(BF16)
        y_ref[:, j * 2 * fw + fw:(j + 1) * 2 * fw] = y[FFT_R:].astype(BF16)


def _fft_stage2_kernel(y_ref, wr_ref, wi_ref, tr_ref, ti_ref, wf_ref, o_ref):
    kb = pl.program_id(1)
    wr = wr_ref[...]
    wi = wi_ref[...]
    wf = wf_ref[...]
    fw = FOURIER_WIDTH
    for j in range(FS_KB):
        k1 = kb * FS_KB + j
        tr = tr_ref[pl.ds(k1, 1), :]
        ti = ti_ref[pl.ds(k1, 1), :]
        mr = wr * tr - wi * ti
        mi = wr * ti + wi * tr
        m2 = jnp.concatenate([mr, -mi], axis=1).astype(BF16)
        yb = y_ref[j * FFT_R:(j + 1) * FFT_R, :]
        yst = jnp.concatenate([yb[:, :fw], yb[:, fw:]], axis=0)
        z = jnp.dot(m2, yst, preferred_element_type=F32)
        yf = jnp.dot(z.astype(BF16), wf, preferred_element_type=F32)
        o_ref[:, j * fw:(j + 1) * fw] = yf.astype(BF16)


def _fourier(u, w_fourier):
    cs, wbig, wr, wi, tr, ti = _dft_tables()
    r, fw = FFT_R, FOURIER_WIDTH
    u2 = u.reshape(N_SEQ * r, r * fw)
    y = pl.pallas_call(
        _fft_stage1_kernel,
        grid=(N_SEQ, r // FS_SB),
        in_specs=[
            pl.BlockSpec((r, FS_SB * fw), lambda s, j: (s, j)),
            pl.BlockSpec((fw, 2 * fw), lambda s, j: (0, 0)),
            pl.BlockSpec((2 * r, 2 * r), lambda s, j: (0, 0)),
        ],
        out_specs=pl.BlockSpec((r, FS_SB * 2 * fw), lambda s, j: (s, j)),
        out_shape=jax.ShapeDtypeStruct((N_SEQ * r, r * 2 * fw), BF16),
        compiler_params=_cparams("parallel", "parallel"),
    )(u2, cs, wbig)
    y2 = y.reshape(N_SEQ * r * r, 2 * fw)
    full = lambda s, j: (0, 0)
    yf = pl.pallas_call(
        _fft_stage2_kernel,
        grid=(N_SEQ, r // FS_KB),
        in_specs=[
            pl.BlockSpec((FS_KB * r, 2 * fw), lambda s, j: (s * (r // FS_KB) + j, 0)),
            pl.BlockSpec((r, r), full),
            pl.BlockSpec((r, r), full),
            pl.BlockSpec((r, r), full),
            pl.BlockSpec((r, r), full),
            pl.BlockSpec((fw, fw), full),
        ],
        out_specs=pl.BlockSpec((r, FS_KB * fw), lambda s, j: (s, j)),
        out_shape=jax.ShapeDtypeStruct((N_SEQ * r, r * fw), BF16),
        compiler_params=_cparams("parallel", "parallel"),
    )(y2, wr, wi, tr, ti, w_fourier)
    return yf.reshape(N_TOK, fw)


def _attn_kernel(sink_ref, q_ref, kp_ref, kc_ref, kn_ref, vp_ref, vc_ref, vn_ref, o_ref,
                 kext, vext):
    i = pl.program_id(1)
    kext[0:QB, :] = kp_ref[...]
    kext[QB:QB + TQ, :] = kc_ref[...]
    kext[QB + TQ:, :] = kn_ref[...]
    vext[0:QB, :] = vp_ref[...]
    vext[QB:QB + TQ, :] = vc_ref[...]
    vext[QB + TQ:, :] = vn_ref[...]

    rows = GQA_GROUP * QB
    nkeys = 3 * QB
    a = lax.broadcasted_iota(jnp.int32, (rows, nkeys), 0) % QB
    j = lax.broadcasted_iota(jnp.int32, (rows, nkeys), 1)
    rel = j - QB - a
    band = (rel <= WINDOW) & (rel >= -WINDOW)
    row_id = lax.broadcasted_iota(jnp.int32, (rows, 1), 0)

    for sb in range(TQ // QB):
        kpos = (i * (TQ // QB) + sb - 1) * QB + j
        valid = band & (kpos >= 0) & (kpos < SEQ)
        for h in range(N_KV_HEADS):
            heads = [GQA_GROUP * h + g for g in range(GQA_GROUP)]
            qs = jnp.concatenate(
                [q_ref[sb * QB:(sb + 1) * QB, hd * HEAD_DIM:(hd + 1) * HEAD_DIM] for hd in heads],
                axis=0)
            kh = kext[sb * QB:sb * QB + nkeys, h * HEAD_DIM:(h + 1) * HEAD_DIM]
            vh = vext[sb * QB:sb * QB + nkeys, h * HEAD_DIM:(h + 1) * HEAD_DIM]
            s = lax.dot_general(qs, kh, (((1,), (1,)), ((), ())), preferred_element_type=F32)
            s = jnp.where(valid, s, NEG)
            sink = jnp.where(row_id < QB, sink_ref[heads[0]],
                             jnp.where(row_id < 2 * QB, sink_ref[heads[1]], sink_ref[heads[2]]))
            m = jnp.maximum(jnp.max(s, axis=-1, keepdims=True), sink)
            p = jnp.exp(s - m)
            denom = jnp.sum(p, axis=-1, keepdims=True) + jnp.exp(sink - m)
            o = jnp.dot(p.astype(BF16), vh, preferred_element_type=F32) * (1.0 / denom)
            for g, hd in enumerate(heads):
                o_ref[sb * QB:(sb + 1) * QB, hd * HEAD_DIM:(hd + 1) * HEAD_DIM] = (
                    o[g * QB:(g + 1) * QB].astype(BF16))


def _attention(sink, q, k, v):
    nqb = SEQ // QB
    per = TQ // QB
    cur = lambda s, i: (s * (SEQ // TQ) + i, 0)
    prev = lambda s, i: (s * nqb + jnp.maximum(i * per - 1, 0), 0)
    nxt = lambda s, i: (s * nqb + jnp.minimum(i * per + per, nqb - 1), 0)
    return pl.pallas_call(
        _attn_kernel,
        grid=(N_SEQ, SEQ // TQ),
        in_specs=[
            pl.BlockSpec(memory_space=pltpu.SMEM),
            pl.BlockSpec((TQ, ATTN_WIDTH), cur),
            pl.BlockSpec((QB, KV_WIDTH), prev),
            pl.BlockSpec((TQ, KV_WIDTH), cur),
            pl.BlockSpec((QB, KV_WIDTH), nxt),
            pl.BlockSpec((QB, KV_WIDTH), prev),
            pl.BlockSpec((TQ, KV_WIDTH), cur),
            pl.BlockSpec((QB, KV_WIDTH), nxt),
        ],
        out_specs=pl.BlockSpec((TQ, ATTN_WIDTH), cur),
        out_shape=jax.ShapeDtypeStruct((N_TOK, ATTN_WIDTH), BF16),
        scratch_shapes=[pltpu.VMEM((TQ + 2 * QB, KV_WIDTH), BF16),
                        pltpu.VMEM((TQ + 2 * QB, KV_WIDTH), BF16)],
        compiler_params=_cparams("parallel", "parallel"),
    )(sink, q, k, k, k, v, v, v)


def _outproj_kernel(xp_ref, xs_ref, yf_ref, ya_ref, wo_ref, g_ref, wr_ref, x1_ref, xn_ref, aff_ref):
    mix = jnp.dot(yf_ref[...], wo_ref[:FOURIER_WIDTH, :], preferred_element_type=F32)
    mix = mix + jnp.dot(ya_ref[...], wo_ref[FOURIER_WIDTH:, :], preferred_element_type=F32)
    x1 = _x_tile(xp_ref, xs_ref) + mix
    x1_ref[...] = x1
    ms = jnp.mean(x1 * x1, axis=-1, keepdims=True)
    xn = (x1 * lax.rsqrt(ms + EPS) * g_ref[...]).astype(BF16)
    xn_ref[...] = xn
    logits = lax.dot_general(wr_ref[...], xn, (((1,), (1,)), ((), ())), preferred_element_type=F32)
    mx = jnp.max(logits, axis=0, keepdims=True)
    ex = jnp.exp(logits - mx)
    aff_ref[...] = ex / jnp.sum(ex, axis=0, keepdims=True)


def _outproj(xp, xs, yf, ya, w_out, g, w_router_t):
    return pl.pallas_call(
        _outproj_kernel,
        grid=(N_TOK // TM,),
        in_specs=_x_specs() + [
            pl.BlockSpec((TM, FOURIER_WIDTH), lambda i: (i, 0)),
            pl.BlockSpec((TM, ATTN_WIDTH), lambda i: (i, 0)),
            pl.BlockSpec((D_MODEL, D_MODEL), lambda i: (0, 0)),
            pl.BlockSpec((1, D_MODEL), lambda i: (0, 0)),
            pl.BlockSpec((N_EXPERTS, D_MODEL), lambda i: (0, 0)),
        ],
        out_specs=[
            pl.BlockSpec((TM, D_MODEL), lambda i: (i, 0)),
            pl.BlockSpec((TM, D_MODEL), lambda i: (i, 0)),
            pl.BlockSpec((N_EXPERTS, TM), lambda i: (0, i)),
        ],
        out_shape=[
            jax.ShapeDtypeStruct((N_TOK, D_MODEL), F32),
            jax.ShapeDtypeStruct((N_TOK, D_MODEL), BF16),
            jax.ShapeDtypeStruct((N_EXPERTS, N_TOK), F32),
        ],
        compiler_params=_cparams("parallel"),
    )(xp, xs, yf, ya, w_out, g, w_router_t)


RB = 512
N_RB = N_TOK // RB
GROUP_TOKENS = (SEQ, 2 * SEQ)
GROUP0_BLOCKS = GROUP_TOKENS[0] // RB
CHUNK = 8
TILE_ROWS = 256
CHUNKS_PER_TILE = TILE_ROWS // CHUNK
SLOT_TILES = 14
SLOT_ROWS = SLOT_TILES * FF_TM
assert SLOT_ROWS >= EC_CAPACITY_FACTOR * N_TOK // N_EXPERTS + N_RB * (CHUNK - 1)
TRASH_ROW = N_EXPERTS * SLOT_ROWS
MAX_CHUNKS = (N_EXPERTS * (RB + CHUNK - 1)) // CHUNK
LIST_LEN = ((MAX_CHUNKS + CHUNKS_PER_TILE - 1) // CHUNKS_PER_TILE) * CHUNKS_PER_TILE


def _thresh_kernel(aff_ref, thr_ref, need_ref, *, cap):
    aff = aff_ref[...]

    def body(i, cur):
        cand = cur | jnp.left_shift(jnp.int32(1), 30 - i)
        cnt = jnp.sum(jnp.where(aff >= pltpu.bitcast(cand, F32), 1.0, 0.0), axis=1, keepdims=True)
        return jnp.where(cnt >= cap, cand, cur)

    thr = pltpu.bitcast(lax.fori_loop(0, 31, body, jnp.zeros((N_EXPERTS, 1), jnp.int32)), F32)
    n_gt = jnp.sum(jnp.where(aff > thr, 1.0, 0.0), axis=1, keepdims=True)
    thr_ref[...] = jnp.broadcast_to(thr, thr_ref.shape)
    need_ref[...] = jnp.broadcast_to(cap - n_gt, need_ref.shape)


def _thresholds(aff_group):
    n = aff_group.shape[1]
    cap = EC_CAPACITY_FACTOR * n // N_EXPERTS
    return pl.pallas_call(
        functools.partial(_thresh_kernel, cap=float(cap)),
        out_shape=[jax.ShapeDtypeStruct((N_EXPERTS, LANES), F32),
                   jax.ShapeDtypeStruct((N_EXPERTS, LANES), F32)],
        compiler_params=pltpu.CompilerParams(vmem_limit_bytes=VMEM_LIMIT),
    )(aff_group)


def _select_kernel(aff_ref, thr_ref, need_ref, utri_ref, selw_ref, rank_ref, m_ref, eq_seen):
    rb = pl.program_id(0)

    @pl.when((rb == 0) | (rb == GROUP0_BLOCKS))
    def _():
        eq_seen[...] = jnp.zeros_like(eq_seen)

    aff = aff_ref[...]
    thr = thr_ref[:, 0:1]
    need = need_ref[:, 0:1]
    utri = utri_ref[...]
    eq = jnp.where(aff == thr, 1.0, 0.0)
    eq_before = jnp.dot(eq.astype(BF16), utri, preferred_element_type=F32) + eq_seen[...]
    sel = (aff > thr) | ((aff == thr) & (eq_before < need))
    eq_seen[...] += jnp.sum(eq, axis=1, keepdims=True)
    self = jnp.where(sel, 1.0, 0.0)
    rank_ref[...] = jnp.dot(self.astype(BF16), utri, preferred_element_type=F32)
    selw_ref[...] = jnp.where(sel, aff, 0.0)
    m_ref[0] = jnp.broadcast_to(jnp.sum(self, axis=1, keepdims=True), (N_EXPERTS, LANES))


def _select(aff_t, thr, need):
    utri = jnp.asarray(np.triu(np.ones((RB, RB), np.float32), 1), dtype=BF16)
    grp = lambda rb: (jnp.where(rb >= GROUP0_BLOCKS, 1, 0), 0)
    return pl.pallas_call(
        _select_kernel,
        grid=(N_RB,),
        in_specs=[
            pl.BlockSpec((N_EXPERTS, RB), lambda rb: (0, rb)),
            pl.BlockSpec((N_EXPERTS, LANES), grp),
            pl.BlockSpec((N_EXPERTS, LANES), grp),
            pl.BlockSpec((RB, RB), lambda rb: (0, 0)),
        ],
        out_specs=[
            pl.BlockSpec((N_EXPERTS, RB), lambda rb: (0, rb)),
            pl.BlockSpec((N_EXPERTS, RB), lambda rb: (0, rb)),
            pl.BlockSpec((1, N_EXPERTS, LANES), lambda rb: (rb, 0, 0)),
        ],
        out_shape=[
            jax.ShapeDtypeStruct((N_EXPERTS, N_TOK), F32),
            jax.ShapeDtypeStruct((N_EXPERTS, N_TOK), F32),
            jax.ShapeDtypeStruct((N_RB, N_EXPERTS, LANES), F32),
        ],
        scratch_shapes=[pltpu.VMEM((N_EXPERTS, 1), F32)],
        compiler_params=_cparams("arbitrary"),
    )(aff_t, thr, need, utri)


def _fill_chunk_list(list_ref, which, rb, seg_start_ref, seg_rows_ref):
    cnt = jnp.int32(0)
    for e in range(N_EXPERTS):
        base = e * SLOT_ROWS + seg_start_ref[rb, e]

        def body(k, c, base=base):
            list_ref[which, c] = base + CHUNK * k
            return c + 1

        cnt = lax.fori_loop(0, seg_rows_ref[rb, e] // CHUNK, body, cnt)
    return cnt


def _gather_kernel(seg_start_ref, seg_rows_ref, seg_off_ref, tot_ref,
                   x_ref, selw_ref, rank_ref, xe_hbm,
                   p_scr, obuf, zbuf, dst_list, pending, sem, zsem):
    rb = pl.program_id(0)
    n_rows = tot_ref[rb]
    n_chunks = n_rows // CHUNK
    n_tiles = (n_rows + TILE_ROWS - 1) // TILE_ROWS

    @pl.when(rb == 0)
    def _():
        pending[0] = 0
        pending[1] = 0

    _fill_chunk_list(dst_list, 0, rb, seg_start_ref, seg_rows_ref)

    def tile_copy(slot):
        return pltpu.make_async_copy(obuf.at[slot], xe_hbm.at[pl.ds(0, TILE_ROWS)], sem.at[slot])

    e_iota = lax.broadcasted_iota(jnp.int32, (N_EXPERTS, 1), 0)
    off = jnp.zeros((N_EXPERTS, 1), F32)
    for e in range(N_EXPERTS):
        off = jnp.where(e_iota == e, seg_off_ref[rb, e].astype(F32), off)
    tgt = jnp.where(selw_ref[...] > 0.0, rank_ref[...] + off, -1.0)
    x = x_ref[...]

    def tile_body(t, carry):
        slot = t % 2
        r0 = t * TILE_ROWS

        @pl.when(pending[slot] == 1)
        def _():
            tile_copy(slot).wait()

        p_scr[...] = jnp.zeros_like(p_scr)
        row = (lax.broadcasted_iota(jnp.int32, (TILE_ROWS, RB), 0) + r0).astype(F32)
        for e in range(N_EXPERTS):
            lo = seg_off_ref[rb, e]
            hi = lo + seg_rows_ref[rb, e]

            @pl.when((lo < r0 + TILE_ROWS) & (hi > r0))
            def _(e=e):
                p_scr[...] = jnp.where(tgt[e:e + 1, :] == row, 1.0, p_scr[...])

        obuf[slot] = jnp.dot(p_scr[...].astype(BF16), x, preferred_element_type=F32)

        for i in range(CHUNKS_PER_TILE):
            g = t * CHUNKS_PER_TILE + i
            dst = jnp.where(g < n_chunks, dst_list[0, jnp.minimum(g, n_chunks - 1)],
                            TRASH_ROW + slot * TILE_ROWS + i * CHUNK)
            pltpu.make_async_copy(obuf.at[slot, pl.ds(i * CHUNK, CHUNK)],
                                  xe_hbm.at[pl.ds(pl.multiple_of(dst, CHUNK), CHUNK)],
                                  sem.at[slot]).start()
        pending[slot] = 1
        return carry

    lax.fori_loop(0, n_tiles, tile_body, 0)

    @pl.when(rb == N_RB - 1)
    def _():
        for slot in range(2):
            @pl.when(pending[slot] == 1)
            def _(slot=slot):
                tile_copy(slot).wait()
        zbuf[...] = jnp.zeros_like(zbuf)
        for e in range(N_EXPERTS):
            used = seg_start_ref[rb, e] + seg_rows_ref[rb, e]
            end = ((used + FF_TM - 1) // FF_TM) * FF_TM

            def zcopy(k, used=used, e=e):
                return pltpu.make_async_copy(
                    zbuf, xe_hbm.at[pl.ds(pl.multiple_of(e * SLOT_ROWS + used + k * CHUNK, CHUNK), CHUNK)],
                    zsem)

            def zstart(k, c, zcopy=zcopy):
                zcopy(k).start()
                return c

            def zwait(k, c, zcopy=zcopy):
                zcopy(k).wait()
                return c

            lax.fori_loop(0, (end - used) // CHUNK, zstart, 0)
            lax.fori_loop(0, (end - used) // CHUNK, zwait, 0)


def _route_gather(seg_start, seg_rows, seg_off, tot, xn, selw_t, rank_t):
    gs = pltpu.PrefetchScalarGridSpec(
        num_scalar_prefetch=4,
        grid=(N_RB,),
        in_specs=[
            pl.BlockSpec((RB, D_MODEL), lambda rb, *_: (rb, 0)),
            pl.BlockSpec((N_EXPERTS, RB), lambda rb, *_: (0, rb)),
            pl.BlockSpec((N_EXPERTS, RB), lambda rb, *_: (0, rb)),
        ],
        out_specs=pl.BlockSpec(memory_space=pl.ANY),
        scratch_shapes=[
            pltpu.VMEM((TILE_ROWS, RB), F32),
            pltpu.VMEM((2, TILE_ROWS, D_MODEL), F32),
            pltpu.VMEM((CHUNK, D_MODEL), F32),
            pltpu.SMEM((1, LIST_LEN), jnp.int32),
            pltpu.SMEM((2,), jnp.int32),
            pltpu.SemaphoreType.DMA((2,)),
            pltpu.SemaphoreType.DMA,
        ],
    )
    return pl.pallas_call(
        _gather_kernel,
        grid_spec=gs,
        out_shape=jax.ShapeDtypeStruct((TRASH_ROW + 2 * TILE_ROWS, D_MODEL), F32),
        compiler_params=_cparams("arbitrary"),
    )(seg_start, seg_rows, seg_off, tot, xn, selw_t, rank_t)


def _ffn_kernel(used_ref, x_ref, wg_ref, wu_ref, wd_ref, o_ref):
    e = pl.program_id(0)
    i = pl.program_id(1)

    @pl.when(i * FF_TM < used_ref[e])
    def _():
        x = x_ref[...].astype(BF16)
        acc = jnp.zeros((FF_TM, D_MODEL), F32)
        for c in range(EXPERT_D_FF // FF_FC):
            gt = jnp.dot(x, wg_ref[0, :, c * FF_FC:(c + 1) * FF_FC], preferred_element_type=F32)
            up = jnp.dot(x, wu_ref[0, :, c * FF_FC:(c + 1) * FF_FC], preferred_element_type=F32)
            h = (gt * (1.0 / (1.0 + jnp.exp(-gt))) * up).astype(BF16)
            acc = acc + jnp.dot(h, wd_ref[0, c * FF_FC:(c + 1) * FF_FC, :], preferred_element_type=F32)
        o_ref[...] = acc


def _ffn(used, xe, w_gate, w_up, w_down):
    def rows(e, i, used_ref):
        last = jnp.maximum((used_ref[e] + FF_TM - 1) // FF_TM - 1, 0)
        return (e * SLOT_TILES + jnp.minimum(i, last), 0)

    gs = pltpu.PrefetchScalarGridSpec(
        num_scalar_prefetch=1,
        grid=(N_EXPERTS, SLOT_TILES),
        in_specs=[
            pl.BlockSpec((FF_TM, D_MODEL), rows),
            pl.BlockSpec((1, D_MODEL, EXPERT_D_FF), lambda e, i, u: (e, 0, 0)),
            pl.BlockSpec((1, D_MODEL, EXPERT_D_FF), lambda e, i, u: (e, 0, 0)),
            pl.BlockSpec((1, EXPERT_D_FF, D_MODEL), lambda e, i, u: (e, 0, 0)),
        ],
        out_specs=pl.BlockSpec((FF_TM, D_MODEL), rows),
    )
    return pl.pallas_call(
        _ffn_kernel,
        grid_spec=gs,
        out_shape=jax.ShapeDtypeStruct((N_EXPERTS * SLOT_ROWS, D_MODEL), F32),
        compiler_params=_cparams("arbitrary", "arbitrary"),
    )(used, xe, w_gate, w_up, w_down)


def _combine_kernel(seg_start_ref, seg_rows_ref, seg_off_ref, tot_ref,
                    x1_ref, selw_ref, rank_ref, g_ref, ye_hbm, op_ref, os_ref,
                    w_scr, ybuf, acc, src_list, first_slot, sem):
    rb = pl.program_id(0)
    cur = rb % 2
    nxt_rb = jnp.minimum(rb + 1, N_RB - 1)

    def tiles_of(b):
        return (tot_ref[b] + TILE_ROWS - 1) // TILE_ROWS

    n_tiles = tiles_of(rb)
    n_tiles_next = jnp.where(rb + 1 < N_RB, tiles_of(nxt_rb), 0)

    def fetch(b, which, t, slot):
        n_chunks = tot_ref[b] // CHUNK
        for i in range(CHUNKS_PER_TILE):
            g = t * CHUNKS_PER_TILE + i
            src = jnp.where(g < n_chunks, src_list[which, jnp.minimum(g, n_chunks - 1)], 0)
            pltpu.make_async_copy(ye_hbm.at[pl.ds(pl.multiple_of(src, CHUNK), CHUNK)],
                                  ybuf.at[slot, pl.ds(i * CHUNK, CHUNK)],
                                  sem.at[slot]).start()

    def tile_wait(slot):
        pltpu.make_async_copy(ye_hbm.at[pl.ds(0, TILE_ROWS)], ybuf.at[slot], sem.at[slot]).wait()

    @pl.when(rb == 0)
    def _():
        _fill_chunk_list(src_list, 0, 0, seg_start_ref, seg_rows_ref)
        first_slot[0] = 0

        @pl.when(n_tiles > 0)
        def _():
            fetch(0, 0, 0, 0)

    @pl.when(rb + 1 < N_RB)
    def _():
        _fill_chunk_list(src_list, 1 - cur, nxt_rb, seg_start_ref, seg_rows_ref)

    slot0 = first_slot[0]
    first_slot[0] = (slot0 + n_tiles) % 2

    @pl.when((n_tiles == 0) & (n_tiles_next > 0))
    def _():
        fetch(nxt_rb, 1 - cur, 0, slot0)

    e_iota = lax.broadcasted_iota(jnp.int32, (N_EXPERTS, 1), 0)
    off = jnp.zeros((N_EXPERTS, 1), F32)
    for e in range(N_EXPERTS):
        off = jnp.where(e_iota == e, seg_off_ref[rb, e].astype(F32), off)
    selw = selw_ref[...]
    tgt = jnp.where(selw > 0.0, rank_ref[...] + off, -1.0)
    acc[...] = jnp.zeros_like(acc)

    def tile_body(t, carry):
        slot = (slot0 + t) % 2
        r0 = t * TILE_ROWS

        @pl.when(t + 1 < n_tiles)
        def _():
            fetch(rb, cur, t + 1, 1 - slot)

        @pl.when((t + 1 == n_tiles) & (n_tiles_next > 0))
        def _():
            fetch(nxt_rb, 1 - cur, 0, 1 - slot)

        tile_wait(slot)
        w_scr[...] = jnp.zeros_like(w_scr)
        row = (lax.broadcasted_iota(jnp.int32, (TILE_ROWS, RB), 0) + r0).astype(F32)
        for e in range(N_EXPERTS):
            lo = seg_off_ref[rb, e]
            hi = lo + seg_rows_ref[rb, e]

            @pl.when((lo < r0 + TILE_ROWS) & (hi > r0))
            def _(e=e):
                w_scr[...] = jnp.where(tgt[e:e + 1, :] == row, selw[e:e + 1, :], w_scr[...])

        acc[...] += lax.dot_general(w_scr[...].astype(BF16), ybuf[slot].astype(BF16),
                                    (((0,), (0,)), ((), ())), preferred_element_type=F32)
        return carry

    lax.fori_loop(0, n_tiles, tile_body, 0)
    x2 = x1_ref[...] + acc[...]
    ms = jnp.mean(x2 * x2, axis=-1, keepdims=True)
    y = x2 * lax.rsqrt(ms + EPS) * g_ref[...]

    @pl.when(rb < GROUP0_BLOCKS)
    def _():
        op_ref[...] = y

    @pl.when(rb >= GROUP0_BLOCKS)
    def _():
        os_ref[...] = y


def _combine(seg_start, seg_rows, seg_off, tot, x1, selw_t, rank_t, g, ye):
    gs = pltpu.PrefetchScalarGridSpec(
        num_scalar_prefetch=4,
        grid=(N_RB,),
        in_specs=[
            pl.BlockSpec((RB, D_MODEL), lambda rb, *_: (rb, 0)),
            pl.BlockSpec((N_EXPERTS, RB), lambda rb, *_: (0, rb)),
            pl.BlockSpec((N_EXPERTS, RB), lambda rb, *_: (0, rb)),
            pl.BlockSpec((1, D_MODEL), lambda rb, *_: (0, 0)),
            pl.BlockSpec(memory_space=pl.ANY),
        ],
        out_specs=[
            pl.BlockSpec((RB, D_MODEL), lambda rb, *_: (jnp.minimum(rb, GROUP0_BLOCKS - 1), 0)),
            pl.BlockSpec((RB, D_MODEL), lambda rb, *_: (jnp.maximum(rb - GROUP0_BLOCKS, 0), 0)),
        ],
        scratch_shapes=[
            pltpu.VMEM((TILE_ROWS, RB), F32),
            pltpu.VMEM((2, TILE_ROWS, D_MODEL), F32),
            pltpu.VMEM((RB, D_MODEL), F32),
            pltpu.SMEM((2, LIST_LEN), jnp.int32),
            pltpu.SMEM((1,), jnp.int32),
            pltpu.SemaphoreType.DMA((2,)),
        ],
    )
    return pl.pallas_call(
        _combine_kernel,
        grid_spec=gs,
        out_shape=[jax.ShapeDtypeStruct((GROUP_TOKENS[0], D_MODEL), F32),
                   jax.ShapeDtypeStruct((GROUP_TOKENS[1], D_MODEL), F32)],
        compiler_params=_cparams("arbitrary"),
    )(seg_start, seg_rows, seg_off, tot, x1, selw_t, rank_t, g, ye)


def kernel(x_prompt, x_sample, norm_mix_g, w_in, w_fourier, sink, w_out, norm_ffn_g, w_router,
           w_gate, w_up, w_down, norm_final_g):
    xp = x_prompt.reshape(-1, D_MODEL)
    xs = x_sample.reshape(-1, D_MODEL)
    cos_t, sin_t = _rope_tables()

    u, q, k, v = _inproj(xp, xs, norm_mix_g[0][None, :], w_in[0].astype(BF16), cos_t, sin_t)
    yf = _fourier(u, w_fourier[0].astype(BF16))
    ya = _attention(sink[0], q, k, v)
    x1, xn, aff_t = _outproj(xp, xs, yf, ya, w_out[0].astype(BF16), norm_ffn_g[0][None, :],
                             w_router[0].T.astype(BF16))

    thr0, need0 = _thresholds(aff_t[:, :GROUP_TOKENS[0]])
    thr1, need1 = _thresholds(aff_t[:, GROUP_TOKENS[0]:])
    selw_t, rank_t, m = _select(aff_t, jnp.concatenate([thr0, thr1]), jnp.concatenate([need0, need1]))

    seg_rows = ((m[:, :, 0].astype(jnp.int32) + CHUNK - 1) // CHUNK) * CHUNK
    seg_start = jnp.cumsum(seg_rows, axis=0) - seg_rows
    seg_off = jnp.cumsum(seg_rows, axis=1) - seg_rows
    tot = jnp.sum(seg_rows, axis=1)
    used = jnp.sum(seg_rows, axis=0)

    xe = _route_gather(seg_start, seg_rows, seg_off, tot, xn, selw_t, rank_t)
    ye = _ffn(used, xe, w_gate[0].astype(BF16), w_up[0].astype(BF16), w_down[0].astype(BF16))
    yp, ys = _combine(seg_start, seg_rows, seg_off, tot, x1, selw_t, rank_t, norm_final_g[None, :], ye)
    return (yp.reshape(x_prompt.shape), ys.reshape(x_sample.shape))
```

```python
import functools

import numpy as np
import jax
import jax.numpy as jnp
from jax import lax
from jax.experimental import pallas as pl
from jax.experimental.pallas import tpu as pltpu

F32 = jnp.float32
BF16 = jnp.bfloat16

D_MODEL = 1024
SEQ = 16384
N_SEQ = 3
N_TOK = N_SEQ * SEQ
HEAD_DIM = 64
N_HEADS = 12
N_KV_HEADS = 4
GQA_GROUP = N_HEADS // N_KV_HEADS
ATTN_WIDTH = N_HEADS * HEAD_DIM
KV_WIDTH = N_KV_HEADS * HEAD_DIM
FOURIER_WIDTH = 256
FOURIER_GROUP_DIM = 64
IN_WIDTH = FOURIER_WIDTH + ATTN_WIDTH + 2 * KV_WIDTH
WINDOW = 128
ROPE_THETA = 10000.0
N_EXPERTS = 16
EC_CAPACITY_FACTOR = 2
EXPERT_D_FF = 2048
EPS = 1e-6
NEG = -1e30

LANES = 128
FFT_R = 128
VMEM_LIMIT = 48 * 1024 * 1024

TM = 512
TQ = 512
QB = 128
FF_TM = 512
FF_FC = 512


def _cparams(*sem):
    return pltpu.CompilerParams(dimension_semantics=sem, vmem_limit_bytes=VMEM_LIMIT)


def _x_specs():
    n_p = SEQ // TM
    return [pl.BlockSpec((TM, D_MODEL), lambda i, *_: (jnp.minimum(i, n_p - 1), 0)),
            pl.BlockSpec((TM, D_MODEL), lambda i, *_: (jnp.maximum(i - n_p, 0), 0))]


def _x_tile(xp_ref, xs_ref):
    return jnp.where(pl.program_id(0) < SEQ // TM, xp_ref[...], xs_ref[...])


def _inproj_kernel(xp_ref, xs_ref, g_ref, w_ref, cos_ref, sin_ref, u_ref, q_ref, k_ref, v_ref):
    x = _x_tile(xp_ref, xs_ref)
    ms = jnp.mean(x * x, axis=-1, keepdims=True)
    hn = (x * lax.rsqrt(ms + EPS) * g_ref[...]).astype(BF16)
    proj = jnp.dot(hn, w_ref[...], preferred_element_type=F32)
    u_ref[...] = proj[:, :FOURIER_WIDTH].astype(BF16)
    cos = cos_ref[...]
    sin = sin_ref[...]
    lane = lax.broadcasted_iota(jnp.int32, cos.shape, 1)
    first_half = (lane % HEAD_DIM) < (HEAD_DIM // 2)

    def rope(t):
        partner = jnp.where(first_half,
                            pltpu.roll(t, LANES - HEAD_DIM // 2, 1),
                            pltpu.roll(t, HEAD_DIM // 2, 1))
        return t * cos + partner * sin

    scale = HEAD_DIM ** -0.5
    for c in range(ATTN_WIDTH // LANES):
        lo = FOURIER_WIDTH + c * LANES
        q_ref[:, c * LANES:(c + 1) * LANES] = (rope(proj[:, lo:lo + LANES]) * scale).astype(BF16)
    for c in range(KV_WIDTH // LANES):
        lo = FOURIER_WIDTH + ATTN_WIDTH + c * LANES
        k_ref[:, c * LANES:(c + 1) * LANES] = rope(proj[:, lo:lo + LANES]).astype(BF16)
    v_ref[...] = proj[:, FOURIER_WIDTH + ATTN_WIDTH + KV_WIDTH:].astype(BF16)


def _inproj(xp, xs, g, w_in, cos_t, sin_t):
    n_pos_blocks = SEQ // TM
    return pl.pallas_call(
        _inproj_kernel,
        grid=(N_TOK // TM,),
        in_specs=_x_specs() + [
            pl.BlockSpec((1, D_MODEL), lambda i: (0, 0)),
            pl.BlockSpec((D_MODEL, IN_WIDTH), lambda i: (0, 0)),
            pl.BlockSpec((TM, LANES), lambda i: (i % n_pos_blocks, 0)),
            pl.BlockSpec((TM, LANES), lambda i: (i % n_pos_blocks, 0)),
        ],
        out_specs=[
            pl.BlockSpec((TM, FOURIER_WIDTH), lambda i: (i, 0)),
            pl.BlockSpec((TM, ATTN_WIDTH), lambda i: (i, 0)),
            pl.BlockSpec((TM, KV_WIDTH), lambda i: (i, 0)),
            pl.BlockSpec((TM, KV_WIDTH), lambda i: (i, 0)),
        ],
        out_shape=[
            jax.ShapeDtypeStruct((N_TOK, FOURIER_WIDTH), BF16),
            jax.ShapeDtypeStruct((N_TOK, ATTN_WIDTH), BF16),
            jax.ShapeDtypeStruct((N_TOK, KV_WIDTH), BF16),
            jax.ShapeDtypeStruct((N_TOK, KV_WIDTH), BF16),
        ],
        compiler_params=_cparams("parallel"),
    )(xp, xs, g, w_in, cos_t, sin_t)


def _rope_tables():
    half = HEAD_DIM // 2
    inv_freq = 1.0 / (ROPE_THETA ** (jnp.arange(half, dtype=F32) / half))
    ang = jnp.arange(SEQ, dtype=F32)[:, None] * inv_freq[None, :]
    cos = jnp.cos(ang)
    sin = jnp.sin(ang)
    reps = LANES // HEAD_DIM
    cos_t = jnp.tile(jnp.concatenate([cos, cos], axis=-1), (1, reps))
    sin_t = jnp.tile(jnp.concatenate([-sin, sin], axis=-1), (1, reps))
    return cos_t, sin_t


def _dft_tables():
    r = FFT_R
    c = np.arange(FOURIER_GROUP_DIM)
    ang_c = 2.0 * np.pi * np.outer(c, c) / FOURIER_GROUP_DIM
    n_groups = FOURIER_WIDTH // FOURIER_GROUP_DIM
    eye = np.eye(n_groups)
    ch_scale = FOURIER_GROUP_DIM ** -0.5
    c_blk = np.kron(eye, np.cos(ang_c)) * ch_scale
    s_blk = np.kron(eye, np.sin(ang_c)) * ch_scale
    cs = np.concatenate([c_blk, -s_blk], axis=1)
    k = np.arange(r)
    ang_r = 2.0 * np.pi * np.outer(k, k) / r
    st_scale = r ** -0.5
    wr = np.cos(ang_r) * st_scale
    wi = -np.sin(ang_r) * st_scale
    wbig = np.block([[wr, -wi], [wi, wr]])
    ang_t = 2.0 * np.pi * np.outer(k, k) / (r * r)
    tr = np.cos(ang_t)
    ti = -np.sin(ang_t)
    f = lambda a: jnp.asarray(a, dtype=F32)
    return f(cs).astype(BF16), f(wbig).astype(BF16), f(wr), f(wi), f(tr), f(ti)


FS_SB = 8
FS_KB = 8


def _fft_stage1_kernel(u_ref, cs_ref, wbig_ref, y_ref):
    cs = cs_ref[...]
    wbig = wbig_ref[...]
    fw = FOURIER_WIDTH
    for j in range(FS_SB):
        u = u_ref[:, j * fw:(j + 1) * fw]
        g = jnp.dot(u, cs, preferred_element_type=F32)
        gst = jnp.concatenate([g[:, :fw], g[:, fw:]], axis=0).astype(BF16)
        y = jnp.dot(wbig, gst, preferred_element_type=F32)
        y_ref[:, j * 2 * fw:j * 2 * fw + fw] = y[:FFT_R].astype(BF16)
        y_ref[:, j * 2 * fw + fw:(j + 1) * 2 * fw] = y[FFT_R:].astype(BF16)


def _fft_stage2_kernel(y_ref, wr_ref, wi_ref, tr_ref, ti_ref, wf_ref, o_ref):
    kb = pl.program_id(1)
    wr = wr_ref[...]
    wi = wi_ref[...]
    wf = wf_ref[...]
    fw = FOURIER_WIDTH
    for j in range(FS_KB):
        k1 = kb * FS_KB + j
        tr = tr_ref[pl.ds(k1, 1), :]
        ti = ti_ref[pl.ds(k1, 1), :]
        mr = wr * tr - wi * ti
        mi = wr * ti + wi * tr
        m2 = jnp.concatenate([mr, -mi], axis=1).astype(BF16)
        yb = y_ref[j * FFT_R:(j + 1) * FFT_R, :]
        yst = jnp.concatenate([yb[:, :fw], yb[:, fw:]], axis=0)
        z = jnp.dot(m2, yst, preferred_element_type=F32)
        yf = jnp.dot(z.astype(BF16), wf, preferred_element_type=F32)
        o_ref[:, j * fw:(j + 1) * fw] = yf.astype(BF16)


def _fourier(u, w_fourier):
    cs, wbig, wr, wi, tr, ti = _dft_tables()
    r, fw = FFT_R, FOURIER_WIDTH
    u2 = u.reshape(N_SEQ * r, r * fw)
    y = pl.pallas_call(
        _fft_stage1_kernel,
        grid=(N_SEQ, r // FS_SB),
        in_specs=[
            pl.BlockSpec((r, FS_SB * fw), lambda s, j: (s, j)),
            pl.BlockSpec((fw, 2 * fw), lambda s, j: (0, 0)),
            pl.BlockSpec((2 * r, 2 * r), lambda s, j: (0, 0)),
        ],
        out_specs=pl.BlockSpec((r, FS_SB * 2 * fw), lambda s, j: (s, j)),
        out_shape=jax.ShapeDtypeStruct((N_SEQ * r, r * 2 * fw), BF16),
        compiler_params=_cparams("parallel", "parallel"),
    )(u2, cs, wbig)
    y2 = y.reshape(N_SEQ * r * r, 2 * fw)
    full = lambda s, j: (0, 0)
    yf = pl.pallas_call(
        _fft_stage2_kernel,
        grid=(N_SEQ, r // FS_KB),
        in_specs=[
            pl.BlockSpec((FS_KB * r, 2 * fw), lambda s, j: (s * (r // FS_KB) + j, 0)),
            pl.BlockSpec((r, r), full),
            pl.BlockSpec((r, r), full),
            pl.BlockSpec((r, r), full),
            pl.BlockSpec((r, r), full),
            pl.BlockSpec((fw, fw), full),
        ],
        out_specs=pl.BlockSpec((r, FS_KB * fw), lambda s, j: (s, j)),
        out_shape=jax.ShapeDtypeStruct((N_SEQ * r, r * fw), BF16),
        compiler_params=_cparams("parallel", "parallel"),
    )(y2, wr, wi, tr, ti, w_fourier)
    return yf.reshape(N_TOK, fw)


def _attn_kernel(sink_ref, q_ref, kp_ref, kc_ref, kn_ref, vp_ref, vc_ref, vn_ref, o_ref,
                 kext, vext):
    i = pl.program_id(1)
    kext[0:QB, :] = kp_ref[...]
    kext[QB:QB + TQ, :] = kc_ref[...]
    kext[QB + TQ:, :] = kn_ref[...]
    vext[0:QB, :] = vp_ref[...]
    vext[QB:QB + TQ, :] = vc_ref[...]
    vext[QB + TQ:, :] = vn_ref[...]

    rows = GQA_GROUP * QB
    nkeys = 3 * QB
    a = lax.broadcasted_iota(jnp.int32, (rows, nkeys), 0) % QB
    j = lax.broadcasted_iota(jnp.int32, (rows, nkeys), 1)
    rel = j - QB - a
    band = (rel <= WINDOW) & (rel >= -WINDOW)
    row_id = lax.broadcasted_iota(jnp.int32, (rows, 1), 0)

    for sb in range(TQ // QB):
        kpos = (i * (TQ // QB) + sb - 1) * QB + j
        valid = band & (kpos >= 0) & (kpos < SEQ)
        for h in range(N_KV_HEADS):
            heads = [GQA_GROUP * h + g for g in range(GQA_GROUP)]
            qs = jnp.concatenate(
                [q_ref[sb * QB:(sb + 1) * QB, hd * HEAD_DIM:(hd + 1) * HEAD_DIM] for hd in heads],
                axis=0)
            kh = kext[sb * QB:sb * QB + nkeys, h * HEAD_DIM:(h + 1) * HEAD_DIM]
            vh = vext[sb * QB:sb * QB + nkeys, h * HEAD_DIM:(h + 1) * HEAD_DIM]
            s = lax.dot_general(qs, kh, (((1,), (1,)), ((), ())), preferred_element_type=F32)
            s = jnp.where(valid, s, NEG)
            sink = jnp.where(row_id < QB, sink_ref[heads[0]],
                             jnp.where(row_id < 2 * QB, sink_ref[heads[1]], sink_ref[heads[2]]))
            m = jnp.maximum(jnp.max(s, axis=-1, keepdims=True), sink)
            p = jnp.exp(s - m)
            denom = jnp.sum(p, axis=-1, keepdims=True) + jnp.exp(sink - m)
            o = jnp.dot(p.astype(BF16), vh, preferred_element_type=F32) * (1.0 / denom)
            for g, hd in enumerate(heads):
                o_ref[sb * QB:(sb + 1) * QB, hd * HEAD_DIM:(hd + 1) * HEAD_DIM] = (
                    o[g * QB:(g + 1) * QB].astype(BF16))


def _attention(sink, q, k, v):
    nqb = SEQ // QB
    per = TQ // QB
    cur = lambda s, i: (s * (SEQ // TQ) + i, 0)
    prev = lambda s, i: (s * nqb + jnp.maximum(i * per - 1, 0), 0)
    nxt = lambda s, i: (s * nqb + jnp.minimum(i * per + per, nqb - 1), 0)
    return pl.pallas_call(
        _attn_kernel,
        grid=(N_SEQ, SEQ // TQ),
        in_specs=[
            pl.BlockSpec(memory_space=pltpu.SMEM),
            pl.BlockSpec((TQ, ATTN_WIDTH), cur),
            pl.BlockSpec((QB, KV_WIDTH), prev),
            pl.BlockSpec((TQ, KV_WIDTH), cur),
            pl.BlockSpec((QB, KV_WIDTH), nxt),
            pl.BlockSpec((QB, KV_WIDTH), prev),
            pl.BlockSpec((TQ, KV_WIDTH), cur),
            pl.BlockSpec((QB, KV_WIDTH), nxt),
        ],
        out_specs=pl.BlockSpec((TQ, ATTN_WIDTH), cur),
        out_shape=jax.ShapeDtypeStruct((N_TOK, ATTN_WIDTH), BF16),
        scratch_shapes=[pltpu.VMEM((TQ + 2 * QB, KV_WIDTH), BF16),
                        pltpu.VMEM((TQ + 2 * QB, KV_WIDTH), BF16)],
        compiler_params=_cparams("parallel", "parallel"),
    )(sink, q, k, k, k, v, v, v)


def _outproj_kernel(xp_ref, xs_ref, yf_ref, ya_ref, wo_ref, g_ref, wr_ref, x1_ref, xn_ref, aff_ref):
    mix = jnp.dot(yf_ref[...], wo_ref[:FOURIER_WIDTH, :], preferred_element_type=F32)
    mix = mix + jnp.dot(ya_ref[...], wo_ref[FOURIER_WIDTH:, :], preferred_element_type=F32)
    x1 = _x_tile(xp_ref, xs_ref) + mix
    x1_ref[...] = x1
    ms = jnp.mean(x1 * x1, axis=-1, keepdims=True)
    xn = (x1 * lax.rsqrt(ms + EPS) * g_ref[...]).astype(BF16)
    xn_ref[...] = xn
    logits = lax.dot_general(wr_ref[...], xn, (((1,), (1,)), ((), ())), preferred_element_type=F32)
    mx = jnp.max(logits, axis=0, keepdims=True)
    ex = jnp.exp(logits - mx)
    aff_ref[...] = ex / jnp.sum(ex, axis=0, keepdims=True)


def _outproj(xp, xs, yf, ya, w_out, g, w_router_t):
    return pl.pallas_call(
        _outproj_kernel,
        grid=(N_TOK // TM,),
        in_specs=_x_specs() + [
            pl.BlockSpec((TM, FOURIER_WIDTH), lambda i: (i, 0)),
            pl.BlockSpec((TM, ATTN_WIDTH), lambda i: (i, 0)),
            pl.BlockSpec((D_MODEL, D_MODEL), lambda i: (0, 0)),
            pl.BlockSpec((1, D_MODEL), lambda i: (0, 0)),
            pl.BlockSpec((N_EXPERTS, D_MODEL), lambda i: (0, 0)),
        ],
        out_specs=[
            pl.BlockSpec((TM, D_MODEL), lambda i: (i, 0)),
            pl.BlockSpec((TM, D_MODEL), lambda i: (i, 0)),
            pl.BlockSpec((N_EXPERTS, TM), lambda i: (0, i)),
        ],
        out_shape=[
            jax.ShapeDtypeStruct((N_TOK, D_MODEL), F32),
            jax.ShapeDtypeStruct((N_TOK, D_MODEL), BF16),
            jax.ShapeDtypeStruct((N_EXPERTS, N_TOK), F32),
        ],
        compiler_params=_cparams("parallel"),
    )(xp, xs, yf, ya, w_out, g, w_router_t)


RB = 512
N_RB = N_TOK // RB
GROUP_TOKENS = (SEQ, 2 * SEQ)
GROUP0_BLOCKS = GROUP_TOKENS[0] // RB
CHUNK = 8
TILE_ROWS = 256
CHUNKS_PER_TILE = TILE_ROWS // CHUNK
SLOT_TILES = 14
SLOT_ROWS = SLOT_TILES * FF_TM
assert SLOT_ROWS >= EC_CAPACITY_FACTOR * N_TOK // N_EXPERTS + N_RB * (CHUNK - 1)
SLOT_CHUNKS = SLOT_ROWS // CHUNK
TRASH_CHUNK = N_EXPERTS * SLOT_CHUNKS
MAX_CHUNKS = (N_EXPERTS * (RB + CHUNK - 1)) // CHUNK
CHUNK_SHAPE = (2 * CHUNK, D_MODEL // 2)


def _pack_chunks(rows_f32):
    n = rows_f32.shape[0] // CHUNK
    r3 = rows_f32.reshape(n, CHUNK, D_MODEL)
    half = D_MODEL // 2
    return jnp.concatenate([r3[:, :, :half], r3[:, :, half:]], axis=1).astype(BF16)


def _unpack_chunks(chunks_bf16):
    n = chunks_bf16.shape[0]
    c3 = chunks_bf16.astype(F32)
    left = c3[:, :CHUNK, :].reshape(n * CHUNK, D_MODEL // 2)
    right = c3[:, CHUNK:, :].reshape(n * CHUNK, D_MODEL // 2)
    return jnp.concatenate([left, right], axis=1).astype(BF16)
LIST_LEN = ((MAX_CHUNKS + CHUNKS_PER_TILE - 1) // CHUNKS_PER_TILE) * CHUNKS_PER_TILE


def _thresh_kernel(aff_ref, thr_ref, need_ref, *, cap):
    aff = aff_ref[...]

    def body(i, cur):
        cand = cur | jnp.left_shift(jnp.int32(1), 30 - i)
        cnt = jnp.sum(jnp.where(aff >= pltpu.bitcast(cand, F32), 1.0, 0.0), axis=1, keepdims=True)
        return jnp.where(cnt >= cap, cand, cur)

    thr = pltpu.bitcast(lax.fori_loop(0, 31, body, jnp.zeros((N_EXPERTS, 1), jnp.int32)), F32)
    n_gt = jnp.sum(jnp.where(aff > thr, 1.0, 0.0), axis=1, keepdims=True)
    thr_ref[...] = jnp.broadcast_to(thr, thr_ref.shape)
    need_ref[...] = jnp.broadcast_to(cap - n_gt, need_ref.shape)


def _thresholds(aff_group):
    n = aff_group.shape[1]
    cap = EC_CAPACITY_FACTOR * n // N_EXPERTS
    return pl.pallas_call(
        functools.partial(_thresh_kernel, cap=float(cap)),
        out_shape=[jax.ShapeDtypeStruct((N_EXPERTS, LANES), F32),
                   jax.ShapeDtypeStruct((N_EXPERTS, LANES), F32)],
        compiler_params=pltpu.CompilerParams(vmem_limit_bytes=VMEM_LIMIT),
    )(aff_group)


def _select_kernel(aff_ref, thr_ref, need_ref, utri_ref, selw_ref, rank_ref, m_ref, eq_seen):
    rb = pl.program_id(0)

    @pl.when((rb == 0) | (rb == GROUP0_BLOCKS))
    def _():
        eq_seen[...] = jnp.zeros_like(eq_seen)

    aff = aff_ref[...]
    thr = thr_ref[:, 0:1]
    need = need_ref[:, 0:1]
    utri = utri_ref[...]
    eq = jnp.where(aff == thr, 1.0, 0.0)
    eq_before = jnp.dot(eq.astype(BF16), utri, preferred_element_type=F32) + eq_seen[...]
    sel = (aff > thr) | ((aff == thr) & (eq_before < need))
    eq_seen[...] += jnp.sum(eq, axis=1, keepdims=True)
    self = jnp.where(sel, 1.0, 0.0)
    rank_ref[...] = jnp.dot(self.astype(BF16), utri, preferred_element_type=F32)
    selw_ref[...] = jnp.where(sel, aff, 0.0)
    m_ref[0] = jnp.broadcast_to(jnp.sum(self, axis=1, keepdims=True), (N_EXPERTS, LANES))


def _select(aff_t, thr, need):
    utri = jnp.asarray(np.triu(np.ones((RB, RB), np.float32), 1), dtype=BF16)
    grp = lambda rb: (jnp.where(rb >= GROUP0_BLOCKS, 1, 0), 0)
    return pl.pallas_call(
        _select_kernel,
        grid=(N_RB,),
        in_specs=[
            pl.BlockSpec((N_EXPERTS, RB), lambda rb: (0, rb)),
            pl.BlockSpec((N_EXPERTS, LANES), grp),
            pl.BlockSpec((N_EXPERTS, LANES), grp),
            pl.BlockSpec((RB, RB), lambda rb: (0, 0)),
        ],
        out_specs=[
            pl.BlockSpec((N_EXPERTS, RB), lambda rb: (0, rb)),
            pl.BlockSpec((N_EXPERTS, RB), lambda rb: (0, rb)),
            pl.BlockSpec((1, N_EXPERTS, LANES), lambda rb: (rb, 0, 0)),
        ],
        out_shape=[
            jax.ShapeDtypeStruct((N_EXPERTS, N_TOK), F32),
            jax.ShapeDtypeStruct((N_EXPERTS, N_TOK), F32),
            jax.ShapeDtypeStruct((N_RB, N_EXPERTS, LANES), F32),
        ],
        scratch_shapes=[pltpu.VMEM((N_EXPERTS, 1), F32)],
        compiler_params=_cparams("arbitrary"),
    )(aff_t, thr, need, utri)


def _fill_chunk_list(list_ref, which, rb, seg_start_ref, seg_rows_ref):
    cnt = jnp.int32(0)
    for e in range(N_EXPERTS):
        base = e * SLOT_CHUNKS + seg_start_ref[rb, e] // CHUNK

        def body(k, c, base=base):
            list_ref[which, c] = base + k
            return c + 1

        cnt = lax.fori_loop(0, seg_rows_ref[rb, e] // CHUNK, body, cnt)
    return cnt


def _gather_kernel(seg_start_ref, seg_rows_ref, seg_off_ref, tot_ref,
                   x_ref, selw_ref, rank_ref, xe_hbm,
                   p_scr, obuf, zbuf, dst_list, pending, sem, zsem):
    rb = pl.program_id(0)
    n_rows = tot_ref[rb]
    n_chunks = n_rows // CHUNK
    n_tiles = (n_rows + TILE_ROWS - 1) // TILE_ROWS

    @pl.when(rb == 0)
    def _():
        pending[0] = 0
        pending[1] = 0

    _fill_chunk_list(dst_list, 0, rb, seg_start_ref, seg_rows_ref)

    def tile_copy(slot):
        return pltpu.make_async_copy(obuf.at[slot], xe_hbm.at[pl.ds(0, CHUNKS_PER_TILE)], sem.at[slot])

    e_iota = lax.broadcasted_iota(jnp.int32, (N_EXPERTS, 1), 0)
    off = jnp.zeros((N_EXPERTS, 1), F32)
    for e in range(N_EXPERTS):
        off = jnp.where(e_iota == e, seg_off_ref[rb, e].astype(F32), off)
    tgt = jnp.where(selw_ref[...] > 0.0, rank_ref[...] + off, -1.0)
    x = x_ref[...]

    def tile_body(t, carry):
        slot = t % 2
        r0 = t * TILE_ROWS

        @pl.when(pending[slot] == 1)
        def _():
            tile_copy(slot).wait()

        p_scr[...] = jnp.zeros_like(p_scr)
        row = (lax.broadcasted_iota(jnp.int32, (TILE_ROWS, RB), 0) + r0).astype(F32)
        for e in range(N_EXPERTS):
            lo = seg_off_ref[rb, e]
            hi = lo + seg_rows_ref[rb, e]

            @pl.when((lo < r0 + TILE_ROWS) & (hi > r0))
            def _(e=e):
                p_scr[...] = jnp.where(tgt[e:e + 1, :] == row, 1.0, p_scr[...])

        obuf[slot] = _pack_chunks(jnp.dot(p_scr[...].astype(BF16), x, preferred_element_type=F32))

        for i in range(CHUNKS_PER_TILE):
            g = t * CHUNKS_PER_TILE + i
            dst = jnp.where(g < n_chunks, dst_list[0, jnp.minimum(g, n_chunks - 1)],
                            TRASH_CHUNK + slot * CHUNKS_PER_TILE + i)
            pltpu.make_async_copy(obuf.at[slot, i], xe_hbm.at[dst], sem.at[slot]).start()
        pending[slot] = 1
        return carry

    lax.fori_loop(0, n_tiles, tile_body, 0)

    @pl.when(rb == N_RB - 1)
    def _():
        for slot in range(2):
            @pl.when(pending[slot] == 1)
            def _(slot=slot):
                tile_copy(slot).wait()
        zbuf[...] = jnp.zeros_like(zbuf)
        for e in range(N_EXPERTS):
            used = seg_start_ref[rb, e] + seg_rows_ref[rb, e]
            end = ((used + FF_TM - 1) // FF_TM) * FF_TM

            def zcopy(k, used=used, e=e):
                return pltpu.make_async_copy(zbuf, xe_hbm.at[e * SLOT_CHUNKS + used // CHUNK + k], zsem)

            def zstart(k, c, zcopy=zcopy):
                zcopy(k).start()
                return c

            def zwait(k, c, zcopy=zcopy):
                zcopy(k).wait()
                return c

            lax.fori_loop(0, (end - used) // CHUNK, zstart, 0)
            lax.fori_loop(0, (end - used) // CHUNK, zwait, 0)


def _route_gather(seg_start, seg_rows, seg_off, tot, xn, selw_t, rank_t):
    gs = pltpu.PrefetchScalarGridSpec(
        num_scalar_prefetch=4,
        grid=(N_RB,),
        in_specs=[
            pl.BlockSpec((RB, D_MODEL), lambda rb, *_: (rb, 0)),
            pl.BlockSpec((N_EXPERTS, RB), lambda rb, *_: (0, rb)),
            pl.BlockSpec((N_EXPERTS, RB), lambda rb, *_: (0, rb)),
        ],
        out_specs=pl.BlockSpec(memory_space=pl.ANY),
        scratch_shapes=[
            pltpu.VMEM((TILE_ROWS, RB), F32),
            pltpu.VMEM((2, CHUNKS_PER_TILE) + CHUNK_SHAPE, BF16),
            pltpu.VMEM(CHUNK_SHAPE, BF16),
            pltpu.SMEM((1, LIST_LEN), jnp.int32),
            pltpu.SMEM((2,), jnp.int32),
            pltpu.SemaphoreType.DMA((2,)),
            pltpu.SemaphoreType.DMA,
        ],
    )
    return pl.pallas_call(
        _gather_kernel,
        grid_spec=gs,
        out_shape=jax.ShapeDtypeStruct((TRASH_CHUNK + 2 * CHUNKS_PER_TILE,) + CHUNK_SHAPE, BF16),
        compiler_params=_cparams("arbitrary"),
    )(seg_start, seg_rows, seg_off, tot, xn, selw_t, rank_t)


def _ffn_kernel(used_ref, x_ref, wg_ref, wu_ref, wd_ref, o_ref):
    e = pl.program_id(0)
    i = pl.program_id(1)

    @pl.when(i * FF_TM < used_ref[e])
    def _():
        x = _unpack_chunks(x_ref[...])
        acc = jnp.zeros((FF_TM, D_MODEL), F32)
        for c in range(EXPERT_D_FF // FF_FC):
            gt = jnp.dot(x, wg_ref[0, :, c * FF_FC:(c + 1) * FF_FC], preferred_element_type=F32)
            up = jnp.dot(x, wu_ref[0, :, c * FF_FC:(c + 1) * FF_FC], preferred_element_type=F32)
            h = (gt * (1.0 / (1.0 + jnp.exp(-gt))) * up).astype(BF16)
            acc = acc + jnp.dot(h, wd_ref[0, c * FF_FC:(c + 1) * FF_FC, :], preferred_element_type=F32)
        o_ref[...] = _pack_chunks(acc)


def _ffn(used, xe, w_gate, w_up, w_down):
    def rows(e, i, used_ref):
        last = jnp.maximum((used_ref[e] + FF_TM - 1) // FF_TM - 1, 0)
        return (e * SLOT_TILES + jnp.minimum(i, last), 0, 0)

    tile_chunks = (FF_TM // CHUNK,) + CHUNK_SHAPE
    gs = pltpu.PrefetchScalarGridSpec(
        num_scalar_prefetch=1,
        grid=(N_EXPERTS, SLOT_TILES),
        in_specs=[
            pl.BlockSpec(tile_chunks, rows),
            pl.BlockSpec((1, D_MODEL, EXPERT_D_FF), lambda e, i, u: (e, 0, 0)),
            pl.BlockSpec((1, D_MODEL, EXPERT_D_FF), lambda e, i, u: (e, 0, 0)),
            pl.BlockSpec((1, EXPERT_D_FF, D_MODEL), lambda e, i, u: (e, 0, 0)),
        ],
        out_specs=pl.BlockSpec(tile_chunks, rows),
    )
    return pl.pallas_call(
        _ffn_kernel,
        grid_spec=gs,
        out_shape=jax.ShapeDtypeStruct((N_EXPERTS * SLOT_CHUNKS,) + CHUNK_SHAPE, BF16),
        compiler_params=_cparams("arbitrary", "arbitrary"),
    )(used, xe, w_gate, w_up, w_down)


def _combine_kernel(seg_start_ref, seg_rows_ref, seg_off_ref, tot_ref,
                    x1_ref, selw_ref, rank_ref, g_ref, ye_hbm, op_ref, os_ref,
                    w_scr, ybuf, acc, src_list, first_slot, sem):
    rb = pl.program_id(0)
    cur = rb % 2
    nxt_rb = jnp.minimum(rb + 1, N_RB - 1)

    def tiles_of(b):
        return (tot_ref[b] + TILE_ROWS - 1) // TILE_ROWS

    n_tiles = tiles_of(rb)
    n_tiles_next = jnp.where(rb + 1 < N_RB, tiles_of(nxt_rb), 0)

    def fetch(b, which, t, slot):
        n_chunks = tot_ref[b] // CHUNK
        for i in range(CHUNKS_PER_TILE):
            g = t * CHUNKS_PER_TILE + i
            src = jnp.where(g < n_chunks, src_list[which, jnp.minimum(g, n_chunks - 1)], 0)
            pltpu.make_async_copy(ye_hbm.at[src], ybuf.at[slot, i], sem.at[slot]).start()

    def tile_wait(slot):
        pltpu.make_async_copy(ye_hbm.at[pl.ds(0, CHUNKS_PER_TILE)], ybuf.at[slot], sem.at[slot]).wait()

    @pl.when(rb == 0)
    def _():
        _fill_chunk_list(src_list, 0, 0, seg_start_ref, seg_rows_ref)
        first_slot[0] = 0

        @pl.when(n_tiles > 0)
        def _():
            fetch(0, 0, 0, 0)

    @pl.when(rb + 1 < N_RB)
    def _():
        _fill_chunk_list(src_list, 1 - cur, nxt_rb, seg_start_ref, seg_rows_ref)

    slot0 = first_slot[0]
    first_slot[0] = (slot0 + n_tiles) % 2

    @pl.when((n_tiles == 0) & (n_tiles_next > 0))
    def _():
        fetch(nxt_rb, 1 - cur, 0, slot0)

    e_iota = lax.broadcasted_iota(jnp.int32, (N_EXPERTS, 1), 0)
    off = jnp.zeros((N_EXPERTS, 1), F32)
    for e in range(N_EXPERTS):
        off = jnp.where(e_iota == e, seg_off_ref[rb, e].astype(F32), off)
    selw = selw_ref[...]
    tgt = jnp.where(selw > 0.0, rank_ref[...] + off, -1.0)
    acc[...] = jnp.zeros_like(acc)

    def tile_body(t, carry):
        slot = (slot0 + t) % 2
        r0 = t * TILE_ROWS

        @pl.when(t + 1 < n_tiles)
        def _():
            fetch(rb, cur, t + 1, 1 - slot)

        @pl.when((t + 1 == n_tiles) & (n_tiles_next > 0))
        def _():
            fetch(nxt_rb, 1 - cur, 0, 1 - slot)

        tile_wait(slot)
        w_scr[...] = jnp.zeros_like(w_scr)
        row = (lax.broadcasted_iota(jnp.int32, (TILE_ROWS, RB), 0) + r0).astype(F32)
        for e in range(N_EXPERTS):
            lo = seg_off_ref[rb, e]
            hi = lo + seg_rows_ref[rb, e]

            @pl.when((lo < r0 + TILE_ROWS) & (hi > r0))
            def _(e=e):
                w_scr[...] = jnp.where(tgt[e:e + 1, :] == row, selw[e:e + 1, :], w_scr[...])

        acc[...] += lax.dot_general(w_scr[...].astype(BF16), _unpack_chunks(ybuf[slot]),
                                    (((0,), (0,)), ((), ())), preferred_element_type=F32)
        return carry

    lax.fori_loop(0, n_tiles, tile_body, 0)
    x2 = x1_ref[...] + acc[...]
    ms = jnp.mean(x2 * x2, axis=-1, keepdims=True)
    y = x2 * lax.rsqrt(ms + EPS) * g_ref[...]

    @pl.when(rb < GROUP0_BLOCKS)
    def _():
        op_ref[...] = y

    @pl.when(rb >= GROUP0_BLOCKS)
    def _():
        os_ref[...] = y


def _combine(seg_start, seg_rows, seg_off, tot, x1, selw_t, rank_t, g, ye):
    gs = pltpu.PrefetchScalarGridSpec(
        num_scalar_prefetch=4,
        grid=(N_RB,),
        in_specs=[
            pl.BlockSpec((RB, D_MODEL), lambda rb, *_: (rb, 0)),
            pl.BlockSpec((N_EXPERTS, RB), lambda rb, *_: (0, rb)),
            pl.BlockSpec((N_EXPERTS, RB), lambda rb, *_: (0, rb)),
            pl.BlockSpec((1, D_MODEL), lambda rb, *_: (0, 0)),
            pl.BlockSpec(memory_space=pl.ANY),
        ],
        out_specs=[
            pl.BlockSpec((RB, D_MODEL), lambda rb, *_: (jnp.minimum(rb, GROUP0_BLOCKS - 1), 0)),
            pl.BlockSpec((RB, D_MODEL), lambda rb, *_: (jnp.maximum(rb - GROUP0_BLOCKS, 0), 0)),
        ],
        scratch_shapes=[
            pltpu.VMEM((TILE_ROWS, RB), F32),
            pltpu.VMEM((2, CHUNKS_PER_TILE) + CHUNK_SHAPE, BF16),
            pltpu.VMEM((RB, D_MODEL), F32),
            pltpu.SMEM((2, LIST_LEN), jnp.int32),
            pltpu.SMEM((1,), jnp.int32),
            pltpu.SemaphoreType.DMA((2,)),
        ],
    )
    return pl.pallas_call(
        _combine_kernel,
        grid_spec=gs,
        out_shape=[jax.ShapeDtypeStruct((GROUP_TOKENS[0], D_MODEL), F32),
                   jax.ShapeDtypeStruct((GROUP_TOKENS[1], D_MODEL), F32)],
        compiler_params=_cparams("arbitrary"),
    )(seg_start, seg_rows, seg_off, tot, x1, selw_t, rank_t, g, ye)


def kernel(x_prompt, x_sample, norm_mix_g, w_in, w_fourier, sink, w_out, norm_ffn_g, w_router,
           w_gate, w_up, w_down, norm_final_g):
    xp = x_prompt.reshape(-1, D_MODEL)
    xs = x_sample.reshape(-1, D_MODEL)
    cos_t, sin_t = _rope_tables()

    u, q, k, v = _inproj(xp, xs, norm_mix_g[0][None, :], w_in[0].astype(BF16), cos_t, sin_t)
    yf = _fourier(u, w_fourier[0].astype(BF16))
    ya = _attention(sink[0], q, k, v)
    x1, xn, aff_t = _outproj(xp, xs, yf, ya, w_out[0].astype(BF16), norm_ffn_g[0][None, :],
                             w_router[0].T.astype(BF16))

    thr0, need0 = _thresholds(aff_t[:, :GROUP_TOKENS[0]])
    thr1, need1 = _thresholds(aff_t[:, GROUP_TOKENS[0]:])
    selw_t, rank_t, m = _select(aff_t, jnp.concatenate([thr0, thr1]), jnp.concatenate([need0, need1]))

    seg_rows = ((m[:, :, 0].astype(jnp.int32) + CHUNK - 1) // CHUNK) * CHUNK
    seg_start = jnp.cumsum(seg_rows, axis=0) - seg_rows
    seg_off = jnp.cumsum(seg_rows, axis=1) - seg_rows
    tot = jnp.sum(seg_rows, axis=1)
    used = jnp.sum(seg_rows, axis=0)

    xe = _route_gather(seg_start, seg_rows, seg_off, tot, xn, selw_t, rank_t)
    ye = _ffn(used, xe, w_gate[0].astype(BF16), w_up[0].astype(BF16), w_down[0].astype(BF16))
    yp, ys = _combine(seg_start, seg_rows, seg_off, tot, x1, selw_t, rank_t, norm_final_g[None, :], ye)
    return (yp.reshape(x_prompt.shape), ys.reshape(x_sample.shape))
```

```python
import functools

import numpy as np
import jax
import jax.numpy as jnp
from jax import lax
from jax.experimental import pallas as pl
from jax.experimental.pallas import tpu as pltpu

F32 = jnp.float32
BF16 = jnp.bfloat16

D_MODEL = 1024
SEQ = 16384
N_SEQ = 3
N_TOK = N_SEQ * SEQ
HEAD_DIM = 64
N_HEADS = 12
N_KV_HEADS = 4
GQA_GROUP = N_HEADS // N_KV_HEADS
ATTN_WIDTH = N_HEADS * HEAD_DIM
KV_WIDTH = N_KV_HEADS * HEAD_DIM
FOURIER_WIDTH = 256
FOURIER_GROUP_DIM = 64
IN_WIDTH = FOURIER_WIDTH + ATTN_WIDTH + 2 * KV_WIDTH
WINDOW = 128
ROPE_THETA = 10000.0
N_EXPERTS = 16
EC_CAPACITY_FACTOR = 2
EXPERT_D_FF = 2048
EPS = 1e-6
NEG = -1e30

LANES = 128
FFT_R = 128
VMEM_LIMIT = 48 * 1024 * 1024

TM = 512
TQ = 512
QB = 128
FF_TM = 512
FF_FC = 512


def _cparams(*sem):
    return pltpu.CompilerParams(dimension_semantics=sem, vmem_limit_bytes=VMEM_LIMIT)


def _x_specs():
    n_p = SEQ // TM
    return [pl.BlockSpec((TM, D_MODEL), lambda i, *_: (jnp.minimum(i, n_p - 1), 0)),
            pl.BlockSpec((TM, D_MODEL), lambda i, *_: (jnp.maximum(i - n_p, 0), 0))]


def _x_tile(xp_ref, xs_ref):
    return jnp.where(pl.program_id(0) < SEQ // TM, xp_ref[...], xs_ref[...])


def _inproj_kernel(xp_ref, xs_ref, g_ref, w_ref, cos_ref, sin_ref, u_ref, q_ref, k_ref, v_ref):
    x = _x_tile(xp_ref, xs_ref)
    ms = jnp.mean(x * x, axis=-1, keepdims=True)
    hn = (x * lax.rsqrt(ms + EPS) * g_ref[...]).astype(BF16)
    proj = jnp.dot(hn, w_ref[...], preferred_element_type=F32)
    u_ref[...] = proj[:, :FOURIER_WIDTH].astype(BF16)
    cos = cos_ref[...]
    sin = sin_ref[...]
    lane = lax.broadcasted_iota(jnp.int32, cos.shape, 1)
    first_half = (lane % HEAD_DIM) < (HEAD_DIM // 2)

    def rope(t):
        partner = jnp.where(first_half,
                            pltpu.roll(t, LANES - HEAD_DIM // 2, 1),
                            pltpu.roll(t, HEAD_DIM // 2, 1))
        return t * cos + partner * sin

    scale = HEAD_DIM ** -0.5
    for c in range(ATTN_WIDTH // LANES):
        lo = FOURIER_WIDTH + c * LANES
        q_ref[:, c * LANES:(c + 1) * LANES] = (rope(proj[:, lo:lo + LANES]) * scale).astype(BF16)
    for c in range(KV_WIDTH // LANES):
        lo = FOURIER_WIDTH + ATTN_WIDTH + c * LANES
        k_ref[:, c * LANES:(c + 1) * LANES] = rope(proj[:, lo:lo + LANES]).astype(BF16)
    v_ref[...] = proj[:, FOURIER_WIDTH + ATTN_WIDTH + KV_WIDTH:].astype(BF16)


def _inproj(xp, xs, g, w_in, cos_t, sin_t):
    n_pos_blocks = SEQ // TM
    return pl.pallas_call(
        _inproj_kernel,
        grid=(N_TOK // TM,),
        in_specs=_x_specs() + [
            pl.BlockSpec((1, D_MODEL), lambda i: (0, 0)),
            pl.BlockSpec((D_MODEL, IN_WIDTH), lambda i: (0, 0)),
            pl.BlockSpec((TM, LANES), lambda i: (i % n_pos_blocks, 0)),
            pl.BlockSpec((TM, LANES), lambda i: (i % n_pos_blocks, 0)),
        ],
        out_specs=[
            pl.BlockSpec((TM, FOURIER_WIDTH), lambda i: (i, 0)),
            pl.BlockSpec((TM, ATTN_WIDTH), lambda i: (i, 0)),
            pl.BlockSpec((TM, KV_WIDTH), lambda i: (i, 0)),
            pl.BlockSpec((TM, KV_WIDTH), lambda i: (i, 0)),
        ],
        out_shape=[
            jax.ShapeDtypeStruct((N_TOK, FOURIER_WIDTH), BF16),
            jax.ShapeDtypeStruct((N_TOK, ATTN_WIDTH), BF16),
            jax.ShapeDtypeStruct((N_TOK, KV_WIDTH), BF16),
            jax.ShapeDtypeStruct((N_TOK, KV_WIDTH), BF16),
        ],
        compiler_params=_cparams("parallel"),
    )(xp, xs, g, w_in, cos_t, sin_t)


def _rope_tables():
    half = HEAD_DIM // 2
    inv_freq = 1.0 / (ROPE_THETA ** (jnp.arange(half, dtype=F32) / half))
    ang = jnp.arange(SEQ, dtype=F32)[:, None] * inv_freq[None, :]
    cos = jnp.cos(ang)
    sin = jnp.sin(ang)
    reps = LANES // HEAD_DIM
    cos_t = jnp.tile(jnp.concatenate([cos, cos], axis=-1), (1, reps))
    sin_t = jnp.tile(jnp.concatenate([-sin, sin], axis=-1), (1, reps))
    return cos_t, sin_t


def _dft_tables():
    r = FFT_R
    c = np.arange(FOURIER_GROUP_DIM)
    ang_c = 2.0 * np.pi * np.outer(c, c) / FOURIER_GROUP_DIM
    n_groups = FOURIER_WIDTH // FOURIER_GROUP_DIM
    eye = np.eye(n_groups)
    ch_scale = FOURIER_GROUP_DIM ** -0.5
    c_blk = np.kron(eye, np.cos(ang_c)) * ch_scale
    s_blk = np.kron(eye, np.sin(ang_c)) * ch_scale
    cs = np.concatenate([c_blk, -s_blk], axis=1)
    k = np.arange(r)
    ang_r = 2.0 * np.pi * np.outer(k, k) / r
    st_scale = r ** -0.5
    wr = np.cos(ang_r) * st_scale
    wi = -np.sin(ang_r) * st_scale
    wbig = np.block([[wr, -wi], [wi, wr]])
    ang_t = 2.0 * np.pi * np.outer(k, k) / (r * r)
    tr = np.cos(ang_t)
    ti = -np.sin(ang_t)
    f = lambda a: jnp.asarray(a, dtype=F32)
    return f(cs).astype(BF16), f(wbig).astype(BF16), f(wr), f(wi), f(tr), f(ti)


FS_SB = 8
FS_KB = 8


def _fft_stage1_kernel(u_ref, cs_ref, wbig_ref, y_ref):
    cs = cs_ref[...]
    wbig = wbig_ref[...]
    fw = FOURIER_WIDTH
    for j in range(FS_SB):
        u = u_ref[:, j * fw:(j + 1) * fw]
        g = jnp.dot(u, cs, preferred_element_type=F32)
        gst = jnp.concatenate([g[:, :fw], g[:, fw:]], axis=0).astype(BF16)
        y = jnp.dot(wbig, gst, preferred_element_type=F32)
        y_ref[:, j * 2 * fw:j * 2 * fw + fw] = y[:FFT_R].astype(BF16)
        y_ref[:, j * 2 * fw + fw:(j + 1) * 2 * fw] = y[FFT_R:].astype(BF16)


def _fft_stage2_kernel(y_ref, wr_ref, wi_ref, tr_ref, ti_ref, wf_ref, o_ref):
    kb = pl.program_id(1)
    wr = wr_ref[...]
    wi = wi_ref[...]
    wf = wf_ref[...]
    fw = FOURIER_WIDTH
    for j in range(FS_KB):
        k1 = kb * FS_KB + j
        tr = tr_ref[pl.ds(k1, 1), :]
        ti = ti_ref[pl.ds(k1, 1), :]
        mr = wr * tr - wi * ti
        mi = wr * ti + wi * tr
        m2 = jnp.concatenate([mr, -mi], axis=1).astype(BF16)
        yb = y_ref[j * FFT_R:(j + 1) * FFT_R, :]
        yst = jnp.concatenate([yb[:, :fw], yb[:, fw:]], axis=0)
        z = jnp.dot(m2, yst, preferred_element_type=F32)
        yf = jnp.dot(z.astype(BF16), wf, preferred_element_type=F32)
        o_ref[:, j * fw:(j + 1) * fw] = yf.astype(BF16)


def _fourier(u, w_fourier):
    cs, wbig, wr, wi, tr, ti = _dft_tables()
    r, fw = FFT_R, FOURIER_WIDTH
    u2 = u.reshape(N_SEQ * r, r * fw)
    y = pl.pallas_call(
        _fft_stage1_kernel,
        grid=(N_SEQ, r // FS_SB),
        in_specs=[
            pl.BlockSpec((r, FS_SB * fw), lambda s, j: (s, j)),
            pl.BlockSpec((fw, 2 * fw), lambda s, j: (0, 0)),
            pl.BlockSpec((2 * r, 2 * r), lambda s, j: (0, 0)),
        ],
        out_specs=pl.BlockSpec((r, FS_SB * 2 * fw), lambda s, j: (s, j)),
        out_shape=jax.ShapeDtypeStruct((N_SEQ * r, r * 2 * fw), BF16),
        compiler_params=_cparams("parallel", "parallel"),
    )(u2, cs, wbig)
    y2 = y.reshape(N_SEQ * r * r, 2 * fw)
    full = lambda s, j: (0, 0)
    yf = pl.pallas_call(
        _fft_stage2_kernel,
        grid=(N_SEQ, r // FS_KB),
        in_specs=[
            pl.BlockSpec((FS_KB * r, 2 * fw), lambda s, j: (s * (r // FS_KB) + j, 0)),
            pl.BlockSpec((r, r), full),
            pl.BlockSpec((r, r), full),
            pl.BlockSpec((r, r), full),
            pl.BlockSpec((r, r), full),
            pl.BlockSpec((fw, fw), full),
        ],
        out_specs=pl.BlockSpec((r, FS_KB * fw), lambda s, j: (s, j)),
        out_shape=jax.ShapeDtypeStruct((N_SEQ * r, r * fw), BF16),
        compiler_params=_cparams("parallel", "parallel"),
    )(y2, wr, wi, tr, ti, w_fourier)
    return yf.reshape(N_TOK, fw)


def _attn_kernel(sink_ref, q_ref, kp_ref, kc_ref, kn_ref, vp_ref, vc_ref, vn_ref, o_ref,
                 kext, vext):
    i = pl.program_id(1)
    kext[0:QB, :] = kp_ref[...]
    kext[QB:QB + TQ, :] = kc_ref[...]
    kext[QB + TQ:, :] = kn_ref[...]
    vext[0:QB, :] = vp_ref[...]
    vext[QB:QB + TQ, :] = vc_ref[...]
    vext[QB + TQ:, :] = vn_ref[...]

    rows = GQA_GROUP * QB
    nkeys = 3 * QB
    a = lax.broadcasted_iota(jnp.int32, (rows, nkeys), 0) % QB
    j = lax.broadcasted_iota(jnp.int32, (rows, nkeys), 1)
    rel = j - QB - a
    band = (rel <= WINDOW) & (rel >= -WINDOW)
    row_id = lax.broadcasted_iota(jnp.int32, (rows, 1), 0)

    for sb in range(TQ // QB):
        kpos = (i * (TQ // QB) + sb - 1) * QB + j
        valid = band & (kpos >= 0) & (kpos < SEQ)
        for h in range(N_KV_HEADS):
            heads = [GQA_GROUP * h + g for g in range(GQA_GROUP)]
            qs = jnp.concatenate(
                [q_ref[sb * QB:(sb + 1) * QB, hd * HEAD_DIM:(hd + 1) * HEAD_DIM] for hd in heads],
                axis=0)
            kh = kext[sb * QB:sb * QB + nkeys, h * HEAD_DIM:(h + 1) * HEAD_DIM]
            vh = vext[sb * QB:sb * QB + nkeys, h * HEAD_DIM:(h + 1) * HEAD_DIM]
            s = lax.dot_general(qs, kh, (((1,), (1,)), ((), ())), preferred_element_type=F32)
            s = jnp.where(valid, s, NEG)
            sink = jnp.where(row_id < QB, sink_ref[heads[0]],
                             jnp.where(row_id < 2 * QB, sink_ref[heads[1]], sink_ref[heads[2]]))
            m = jnp.maximum(jnp.max(s, axis=-1, keepdims=True), sink)
            p = jnp.exp(s - m)
            denom = jnp.sum(p, axis=-1, keepdims=True) + jnp.exp(sink - m)
            o = jnp.dot(p.astype(BF16), vh, preferred_element_type=F32) * (1.0 / denom)
            for g, hd in enumerate(heads):
                o_ref[sb * QB:(sb + 1) * QB, hd * HEAD_DIM:(hd + 1) * HEAD_DIM] = (
                    o[g * QB:(g + 1) * QB].astype(BF16))


def _attention(sink, q, k, v):
    nqb = SEQ // QB
    per = TQ // QB
    cur = lambda s, i: (s * (SEQ // TQ) + i, 0)
    prev = lambda s, i: (s * nqb + jnp.maximum(i * per - 1, 0), 0)
    nxt = lambda s, i: (s * nqb + jnp.minimum(i * per + per, nqb - 1), 0)
    return pl.pallas_call(
        _attn_kernel,
        grid=(N_SEQ, SEQ // TQ),
        in_specs=[
            pl.BlockSpec(memory_space=pltpu.SMEM),
            pl.BlockSpec((TQ, ATTN_WIDTH), cur),
            pl.BlockSpec((QB, KV_WIDTH), prev),
            pl.BlockSpec((TQ, KV_WIDTH), cur),
            pl.BlockSpec((QB, KV_WIDTH), nxt),
            pl.BlockSpec((QB, KV_WIDTH), prev),
            pl.BlockSpec((TQ, KV_WIDTH), cur),
            pl.BlockSpec((QB, KV_WIDTH), nxt),
        ],
        out_specs=pl.BlockSpec((TQ, ATTN_WIDTH), cur),
        out_shape=jax.ShapeDtypeStruct((N_TOK, ATTN_WIDTH), BF16),
        scratch_shapes=[pltpu.VMEM((TQ + 2 * QB, KV_WIDTH), BF16),
                        pltpu.VMEM((TQ + 2 * QB, KV_WIDTH), BF16)],
        compiler_params=_cparams("parallel", "parallel"),
    )(sink, q, k, k, k, v, v, v)


def _outproj_kernel(xp_ref, xs_ref, yf_ref, ya_ref, wo_ref, g_ref, wr_ref, x1_ref, xn_ref, aff_ref):
    mix = jnp.dot(yf_ref[...], wo_ref[:FOURIER_WIDTH, :], preferred_element_type=F32)
    mix = mix + jnp.dot(ya_ref[...], wo_ref[FOURIER_WIDTH:, :], preferred_element_type=F32)
    x1 = _x_tile(xp_ref, xs_ref) + mix
    x1_ref[...] = x1
    ms = jnp.mean(x1 * x1, axis=-1, keepdims=True)
    xn = (x1 * lax.rsqrt(ms + EPS) * g_ref[...]).astype(BF16)
    xn_ref[...] = xn
    logits = lax.dot_general(wr_ref[...], xn, (((1,), (1,)), ((), ())), preferred_element_type=F32)
    mx = jnp.max(logits, axis=0, keepdims=True)
    ex = jnp.exp(logits - mx)
    aff_ref[...] = ex / jnp.sum(ex, axis=0, keepdims=True)


def _outproj(xp, xs, yf, ya, w_out, g, w_router_t):
    return pl.pallas_call(
        _outproj_kernel,
        grid=(N_TOK // TM,),
        in_specs=_x_specs() + [
            pl.BlockSpec((TM, FOURIER_WIDTH), lambda i: (i, 0)),
            pl.BlockSpec((TM, ATTN_WIDTH), lambda i: (i, 0)),
            pl.BlockSpec((D_MODEL, D_MODEL), lambda i: (0, 0)),
            pl.BlockSpec((1, D_MODEL), lambda i: (0, 0)),
            pl.BlockSpec((N_EXPERTS, D_MODEL), lambda i: (0, 0)),
        ],
        out_specs=[
            pl.BlockSpec((TM, D_MODEL), lambda i: (i, 0)),
            pl.BlockSpec((TM, D_MODEL), lambda i: (i, 0)),
            pl.BlockSpec((N_EXPERTS, TM), lambda i: (0, i)),
        ],
        out_shape=[
            jax.ShapeDtypeStruct((N_TOK, D_MODEL), F32),
            jax.ShapeDtypeStruct((N_TOK, D_MODEL), BF16),
            jax.ShapeDtypeStruct((N_EXPERTS, N_TOK), F32),
        ],
        compiler_params=_cparams("parallel"),
    )(xp, xs, yf, ya, w_out, g, w_router_t)


RB = 512
N_RB = N_TOK // RB
GROUP_TOKENS = (SEQ, 2 * SEQ)
GROUP0_BLOCKS = GROUP_TOKENS[0] // RB
CHUNK = 8
TILE_ROWS = 256
CHUNKS_PER_TILE = TILE_ROWS // CHUNK
SLOT_TILES = 14
SLOT_ROWS = SLOT_TILES * FF_TM
assert SLOT_ROWS >= EC_CAPACITY_FACTOR * N_TOK // N_EXPERTS + N_RB * (CHUNK - 1)
SLOT_CHUNKS = SLOT_ROWS // CHUNK
TRASH_CHUNK = N_EXPERTS * SLOT_CHUNKS
MAX_CHUNKS = (N_EXPERTS * (RB + CHUNK - 1)) // CHUNK
CHUNK_SHAPE = (2 * CHUNK, D_MODEL // 2)


def _pack_chunks(rows_f32):
    n = rows_f32.shape[0] // CHUNK
    r3 = rows_f32.reshape(n, CHUNK, D_MODEL)
    half = D_MODEL // 2
    return jnp.concatenate([r3[:, :, :half], r3[:, :, half:]], axis=1).astype(BF16)


def _unpack_chunks(chunks_bf16):
    n = chunks_bf16.shape[0]
    c3 = chunks_bf16.astype(F32)
    left = c3[:, :CHUNK, :].reshape(n * CHUNK, D_MODEL // 2)
    right = c3[:, CHUNK:, :].reshape(n * CHUNK, D_MODEL // 2)
    return jnp.concatenate([left, right], axis=1).astype(BF16)
LIST_LEN = ((MAX_CHUNKS + CHUNKS_PER_TILE - 1) // CHUNKS_PER_TILE) * CHUNKS_PER_TILE


def _thresh_kernel(aff_ref, thr_ref, need_ref, *, cap):
    aff = aff_ref[...]

    def body(i, cur):
        cand = cur | jnp.left_shift(jnp.int32(1), 30 - i)
        cnt = jnp.sum(jnp.where(aff >= pltpu.bitcast(cand, F32), 1.0, 0.0), axis=1, keepdims=True)
        return jnp.where(cnt >= cap, cand, cur)

    thr = pltpu.bitcast(lax.fori_loop(0, 31, body, jnp.zeros((N_EXPERTS, 1), jnp.int32)), F32)
    n_gt = jnp.sum(jnp.where(aff > thr, 1.0, 0.0), axis=1, keepdims=True)
    thr_ref[...] = jnp.broadcast_to(thr, thr_ref.shape)
    need_ref[...] = jnp.broadcast_to(cap - n_gt, need_ref.shape)


def _thresholds(aff_group):
    n = aff_group.shape[1]
    cap = EC_CAPACITY_FACTOR * n // N_EXPERTS
    return pl.pallas_call(
        functools.partial(_thresh_kernel, cap=float(cap)),
        out_shape=[jax.ShapeDtypeStruct((N_EXPERTS, LANES), F32),
                   jax.ShapeDtypeStruct((N_EXPERTS, LANES), F32)],
        compiler_params=pltpu.CompilerParams(vmem_limit_bytes=VMEM_LIMIT),
    )(aff_group)


def _select_kernel(aff_ref, thr_ref, need_ref, utri_ref, selw_ref, rank_ref, m_ref, eq_seen):
    rb = pl.program_id(0)

    @pl.when((rb == 0) | (rb == GROUP0_BLOCKS))
    def _():
        eq_seen[...] = jnp.zeros_like(eq_seen)

    aff = aff_ref[...]
    thr = thr_ref[:, 0:1]
    need = need_ref[:, 0:1]
    utri = utri_ref[...]
    eq = jnp.where(aff == thr, 1.0, 0.0)
    eq_before = jnp.dot(eq.astype(BF16), utri, preferred_element_type=F32) + eq_seen[...]
    sel = (aff > thr) | ((aff == thr) & (eq_before < need))
    eq_seen[...] += jnp.sum(eq, axis=1, keepdims=True)
    self = jnp.where(sel, 1.0, 0.0)
    rank_ref[...] = jnp.dot(self.astype(BF16), utri, preferred_element_type=F32)
    selw_ref[...] = jnp.where(sel, aff, 0.0)
    m_ref[0] = jnp.broadcast_to(jnp.sum(self, axis=1, keepdims=True), (N_EXPERTS, LANES))


def _select(aff_t, thr, need):
    utri = jnp.asarray(np.triu(np.ones((RB, RB), np.float32), 1), dtype=BF16)
    grp = lambda rb: (jnp.where(rb >= GROUP0_BLOCKS, 1, 0), 0)
    return pl.pallas_call(
        _select_kernel,
        grid=(N_RB,),
        in_specs=[
            pl.BlockSpec((N_EXPERTS, RB), lambda rb: (0, rb)),
            pl.BlockSpec((N_EXPERTS, LANES), grp),
            pl.BlockSpec((N_EXPERTS, LANES), grp),
            pl.BlockSpec((RB, RB), lambda rb: (0, 0)),
        ],
        out_specs=[
            pl.BlockSpec((N_EXPERTS, RB), lambda rb: (0, rb)),
            pl.BlockSpec((N_EXPERTS, RB), lambda rb: (0, rb)),
            pl.BlockSpec((1, N_EXPERTS, LANES), lambda rb: (rb, 0, 0)),
        ],
        out_shape=[
            jax.ShapeDtypeStruct((N_EXPERTS, N_TOK), F32),
            jax.ShapeDtypeStruct((N_EXPERTS, N_TOK), F32),
            jax.ShapeDtypeStruct((N_RB, N_EXPERTS, LANES), F32),
        ],
        scratch_shapes=[pltpu.VMEM((N_EXPERTS, 1), F32)],
        compiler_params=_cparams("arbitrary"),
    )(aff_t, thr, need, utri)


def _fill_chunk_list(list_ref, which, rb, seg_start_ref, seg_rows_ref):
    cnt = jnp.int32(0)
    for e in range(N_EXPERTS):
        base = e * SLOT_CHUNKS + seg_start_ref[rb, e] // CHUNK

        def body(k, c, base=base):
            list_ref[which, c] = base + k
            return c + 1

        cnt = lax.fori_loop(0, seg_rows_ref[rb, e] // CHUNK, body, cnt)
    return cnt


def _gather_kernel(seg_start_ref, seg_rows_ref, seg_off_ref, tot_ref,
                   x_ref, selw_ref, rank_ref, xe_hbm,
                   p_scr, obuf, zbuf, dst_list, pending, sem, zsem):
    rb = pl.program_id(0)
    n_rows = tot_ref[rb]
    n_chunks = n_rows // CHUNK
    n_tiles = (n_rows + TILE_ROWS - 1) // TILE_ROWS

    @pl.when(rb == 0)
    def _():
        pending[0] = 0
        pending[1] = 0

    _fill_chunk_list(dst_list, 0, rb, seg_start_ref, seg_rows_ref)

    def tile_copy(slot):
        return pltpu.make_async_copy(obuf.at[slot], xe_hbm.at[pl.ds(0, CHUNKS_PER_TILE)], sem.at[slot])

    e_iota = lax.broadcasted_iota(jnp.int32, (N_EXPERTS, 1), 0)
    off = jnp.zeros((N_EXPERTS, 1), F32)
    for e in range(N_EXPERTS):
        off = jnp.where(e_iota == e, seg_off_ref[rb, e].astype(F32), off)
    tgt = jnp.where(selw_ref[...] > 0.0, rank_ref[...] + off, -1.0)
    x = x_ref[...]

    def tile_body(t, carry):
        slot = t % 2
        r0 = t * TILE_ROWS

        @pl.when(pending[slot] == 1)
        def _():
            tile_copy(slot).wait()

        p_scr[...] = jnp.zeros_like(p_scr)
        row = (lax.broadcasted_iota(jnp.int32, (TILE_ROWS, RB), 0) + r0).astype(F32)
        for e in range(N_EXPERTS):
            lo = seg_off_ref[rb, e]
            hi = lo + seg_rows_ref[rb, e]

            @pl.when((lo < r0 + TILE_ROWS) & (hi > r0))
            def _(e=e):
                p_scr[...] = jnp.where(tgt[e:e + 1, :] == row, 1.0, p_scr[...])

        obuf[slot] = _pack_chunks(jnp.dot(p_scr[...].astype(BF16), x, preferred_element_type=F32))

        for i in range(CHUNKS_PER_TILE):
            g = t * CHUNKS_PER_TILE + i
            dst = jnp.where(g < n_chunks, dst_list[0, jnp.minimum(g, n_chunks - 1)],
                            TRASH_CHUNK + slot * CHUNKS_PER_TILE + i)
            pltpu.make_async_copy(obuf.at[slot, i], xe_hbm.at[dst], sem.at[slot]).start()
        pending[slot] = 1
        return carry

    lax.fori_loop(0, n_tiles, tile_body, 0)

    @pl.when(rb == N_RB - 1)
    def _():
        for slot in range(2):
            @pl.when(pending[slot] == 1)
            def _(slot=slot):
                tile_copy(slot).wait()
        zbuf[...] = jnp.zeros_like(zbuf)
        for e in range(N_EXPERTS):
            used = seg_start_ref[rb, e] + seg_rows_ref[rb, e]
            end = ((used + FF_TM - 1) // FF_TM) * FF_TM

            def zcopy(k, used=used, e=e):
                return pltpu.make_async_copy(zbuf, xe_hbm.at[e * SLOT_CHUNKS + used // CHUNK + k], zsem)

            def zstart(k, c, zcopy=zcopy):
                zcopy(k).start()
                return c

            def zwait(k, c, zcopy=zcopy):
                zcopy(k).wait()
                return c

            lax.fori_loop(0, (end - used) // CHUNK, zstart, 0)
            lax.fori_loop(0, (end - used) // CHUNK, zwait, 0)


def _route_gather(seg_start, seg_rows, seg_off, tot, xn, selw_t, rank_t):
    gs = pltpu.PrefetchScalarGridSpec(
        num_scalar_prefetch=4,
        grid=(N_RB,),
        in_specs=[
            pl.BlockSpec((RB, D_MODEL), lambda rb, *_: (rb, 0)),
            pl.BlockSpec((N_EXPERTS, RB), lambda rb, *_: (0, rb)),
            pl.BlockSpec((N_EXPERTS, RB), lambda rb, *_: (0, rb)),
        ],
        out_specs=pl.BlockSpec(memory_space=pl.ANY),
        scratch_shapes=[
            pltpu.VMEM((TILE_ROWS, RB), F32),
            pltpu.VMEM((2, CHUNKS_PER_TILE) + CHUNK_SHAPE, BF16),
            pltpu.VMEM(CHUNK_SHAPE, BF16),
            pltpu.SMEM((1, LIST_LEN), jnp.int32),
            pltpu.SMEM((2,), jnp.int32),
            pltpu.SemaphoreType.DMA((2,)),
            pltpu.SemaphoreType.DMA,
        ],
    )
    return pl.pallas_call(
        _gather_kernel,
        grid_spec=gs,
        out_shape=jax.ShapeDtypeStruct((TRASH_CHUNK + 2 * CHUNKS_PER_TILE,) + CHUNK_SHAPE, BF16),
        compiler_params=_cparams("arbitrary"),
    )(seg_start, seg_rows, seg_off, tot, xn, selw_t, rank_t)


W_PIECES_PER_MATRIX = 4
W_PIECES = 3 * W_PIECES_PER_MATRIX
assert SLOT_TILES > W_PIECES
WG_PIECE_ROWS = D_MODEL // W_PIECES_PER_MATRIX
WD_PIECE_ROWS = EXPERT_D_FF // W_PIECES_PER_MATRIX
FFN_VMEM_LIMIT = 56 * 1024 * 1024


def _ffn_kernel(used_ref, x_ref, wg_hbm, wu_hbm, wd_hbm, o_ref,
                wg_bf, wu_bf, wd_bf, stage_up, stage_dn, sem):
    e = pl.program_id(0)
    i = pl.program_id(1)
    slot = e % 2

    matrices = ((wg_hbm, stage_up, wg_bf, WG_PIECE_ROWS),
                (wu_hbm, stage_up, wu_bf, WG_PIECE_ROWS),
                (wd_hbm, stage_dn, wd_bf, WD_PIECE_ROWS))

    def piece(expert, k, buf, finish):
        def of_kind(kind):
            w_hbm, stage, w_bf, rows = matrices[kind]
            r0 = pl.multiple_of((k - kind * W_PIECES_PER_MATRIX) * rows, rows)
            cp = pltpu.make_async_copy(w_hbm.at[expert, pl.ds(r0, rows), :], stage.at[buf], sem.at[buf])
            if finish:
                cp.wait()
                w_bf[expert % 2, pl.ds(r0, rows), :] = stage[buf].astype(BF16)
            else:
                cp.start()

        if isinstance(k, int):
            of_kind(k // W_PIECES_PER_MATRIX)
        else:
            for kind in range(len(matrices)):
                pl.when(k // W_PIECES_PER_MATRIX == kind)(functools.partial(of_kind, kind))

    @pl.when((e == 0) & (i == 0))
    def _():
        for k in range(W_PIECES):
            piece(0, k, k % 2, finish=False)
            piece(0, k, k % 2, finish=True)

    @pl.when(e + 1 < N_EXPERTS)
    def _():
        @pl.when((i >= 1) & (i <= W_PIECES))
        def _():
            piece(e + 1, i - 1, (i - 1) % 2, finish=True)

        @pl.when(i < W_PIECES)
        def _():
            piece(e + 1, i, i % 2, finish=False)

    @pl.when(i * FF_TM < used_ref[e])
    def _():
        x = _unpack_chunks(x_ref[...])
        acc = jnp.zeros((FF_TM, D_MODEL), F32)
        for c in range(EXPERT_D_FF // FF_FC):
            gt = jnp.dot(x, wg_bf[slot, :, c * FF_FC:(c + 1) * FF_FC], preferred_element_type=F32)
            up = jnp.dot(x, wu_bf[slot, :, c * FF_FC:(c + 1) * FF_FC], preferred_element_type=F32)
            h = (gt * (1.0 / (1.0 + jnp.exp(-gt))) * up).astype(BF16)
            acc = acc + jnp.dot(h, wd_bf[slot, c * FF_FC:(c + 1) * FF_FC, :], preferred_element_type=F32)
        o_ref[...] = _pack_chunks(acc)


def _ffn(used, xe, w_gate, w_up, w_down):
    def rows(e, i, used_ref):
        last = jnp.maximum((used_ref[e] + FF_TM - 1) // FF_TM - 1, 0)
        return (e * SLOT_TILES + jnp.minimum(i, last), 0, 0)

    tile_chunks = (FF_TM // CHUNK,) + CHUNK_SHAPE
    gs = pltpu.PrefetchScalarGridSpec(
        num_scalar_prefetch=1,
        grid=(N_EXPERTS, SLOT_TILES),
        in_specs=[
            pl.BlockSpec(tile_chunks, rows),
            pl.BlockSpec(memory_space=pl.ANY),
            pl.BlockSpec(memory_space=pl.ANY),
            pl.BlockSpec(memory_space=pl.ANY),
        ],
        out_specs=pl.BlockSpec(tile_chunks, rows),
        scratch_shapes=[
            pltpu.VMEM((2, D_MODEL, EXPERT_D_FF), BF16),
            pltpu.VMEM((2, D_MODEL, EXPERT_D_FF), BF16),
            pltpu.VMEM((2, EXPERT_D_FF, D_MODEL), BF16),
            pltpu.VMEM((2, WG_PIECE_ROWS, EXPERT_D_FF), F32),
            pltpu.VMEM((2, WD_PIECE_ROWS, D_MODEL), F32),
            pltpu.SemaphoreType.DMA((2,)),
        ],
    )
    return pl.pallas_call(
        _ffn_kernel,
        grid_spec=gs,
        out_shape=jax.ShapeDtypeStruct((N_EXPERTS * SLOT_CHUNKS,) + CHUNK_SHAPE, BF16),
        compiler_params=pltpu.CompilerParams(dimension_semantics=("arbitrary", "arbitrary"),
                                             vmem_limit_bytes=FFN_VMEM_LIMIT),
    )(used, xe, w_gate, w_up, w_down)


def _combine_kernel(seg_start_ref, seg_rows_ref, seg_off_ref, tot_ref,
                    x1_ref, selw_ref, rank_ref, g_ref, ye_hbm, op_ref, os_ref,
                    w_scr, ybuf, acc, src_list, first_slot, sem):
    rb = pl.program_id(0)
    cur = rb % 2
    nxt_rb = jnp.minimum(rb + 1, N_RB - 1)

    def tiles_of(b):
        return (tot_ref[b] + TILE_ROWS - 1) // TILE_ROWS

    n_tiles = tiles_of(rb)
    n_tiles_next = jnp.where(rb + 1 < N_RB, tiles_of(nxt_rb), 0)

    def fetch(b, which, t, slot):
        n_chunks = tot_ref[b] // CHUNK
        for i in range(CHUNKS_PER_TILE):
            g = t * CHUNKS_PER_TILE + i
            src = jnp.where(g < n_chunks, src_list[which, jnp.minimum(g, n_chunks - 1)], 0)
            pltpu.make_async_copy(ye_hbm.at[src], ybuf.at[slot, i], sem.at[slot]).start()

    def tile_wait(slot):
        pltpu.make_async_copy(ye_hbm.at[pl.ds(0, CHUNKS_PER_TILE)], ybuf.at[slot], sem.at[slot]).wait()

    @pl.when(rb == 0)
    def _():
        _fill_chunk_list(src_list, 0, 0, seg_start_ref, seg_rows_ref)
        first_slot[0] = 0

        @pl.when(n_tiles > 0)
        def _():
            fetch(0, 0, 0, 0)

    @pl.when(rb + 1 < N_RB)
    def _():
        _fill_chunk_list(src_list, 1 - cur, nxt_rb, seg_start_ref, seg_rows_ref)

    slot0 = first_slot[0]
    first_slot[0] = (slot0 + n_tiles) % 2

    @pl.when((n_tiles == 0) & (n_tiles_next > 0))
    def _():
        fetch(nxt_rb, 1 - cur, 0, slot0)

    e_iota = lax.broadcasted_iota(jnp.int32, (N_EXPERTS, 1), 0)
    off = jnp.zeros((N_EXPERTS, 1), F32)
    for e in range(N_EXPERTS):
        off = jnp.where(e_iota == e, seg_off_ref[rb, e].astype(F32), off)
    selw = selw_ref[...]
    tgt = jnp.where(selw > 0.0, rank_ref[...] + off, -1.0)
    acc[...] = jnp.zeros_like(acc)

    def tile_body(t, carry):
        slot = (slot0 + t) % 2
        r0 = t * TILE_ROWS

        @pl.when(t + 1 < n_tiles)
        def _():
            fetch(rb, cur, t + 1, 1 - slot)

        @pl.when((t + 1 == n_tiles) & (n_tiles_next > 0))
        def _():
            fetch(nxt_rb, 1 - cur, 0, 1 - slot)

        tile_wait(slot)
        w_scr[...] = jnp.zeros_like(w_scr)
        row = (lax.broadcasted_iota(jnp.int32, (TILE_ROWS, RB), 0) + r0).astype(F32)
        for e in range(N_EXPERTS):
            lo = seg_off_ref[rb, e]
            hi = lo + seg_rows_ref[rb, e]

            @pl.when((lo < r0 + TILE_ROWS) & (hi > r0))
            def _(e=e):
                w_scr[...] = jnp.where(tgt[e:e + 1, :] == row, selw[e:e + 1, :], w_scr[...])

        acc[...] += lax.dot_general(w_scr[...].astype(BF16), _unpack_chunks(ybuf[slot]),
                                    (((0,), (0,)), ((), ())), preferred_element_type=F32)
        return carry

    lax.fori_loop(0, n_tiles, tile_body, 0)
    x2 = x1_ref[...] + acc[...]
    ms = jnp.mean(x2 * x2, axis=-1, keepdims=True)
    y = x2 * lax.rsqrt(ms + EPS) * g_ref[...]

    @pl.when(rb < GROUP0_BLOCKS)
    def _():
        op_ref[...] = y

    @pl.when(rb >= GROUP0_BLOCKS)
    def _():
        os_ref[...] = y


def _combine(seg_start, seg_rows, seg_off, tot, x1, selw_t, rank_t, g, ye):
    gs = pltpu.PrefetchScalarGridSpec(
        num_scalar_prefetch=4,
        grid=(N_RB,),
        in_specs=[
            pl.BlockSpec((RB, D_MODEL), lambda rb, *_: (rb, 0)),
            pl.BlockSpec((N_EXPERTS, RB), lambda rb, *_: (0, rb)),
            pl.BlockSpec((N_EXPERTS, RB), lambda rb, *_: (0, rb)),
            pl.BlockSpec((1, D_MODEL), lambda rb, *_: (0, 0)),
            pl.BlockSpec(memory_space=pl.ANY),
        ],
        out_specs=[
            pl.BlockSpec((RB, D_MODEL), lambda rb, *_: (jnp.minimum(rb, GROUP0_BLOCKS - 1), 0)),
            pl.BlockSpec((RB, D_MODEL), lambda rb, *_: (jnp.maximum(rb - GROUP0_BLOCKS, 0), 0)),
        ],
        scratch_shapes=[
            pltpu.VMEM((TILE_ROWS, RB), F32),
            pltpu.VMEM((2, CHUNKS_PER_TILE) + CHUNK_SHAPE, BF16),
            pltpu.VMEM((RB, D_MODEL), F32),
            pltpu.SMEM((2, LIST_LEN), jnp.int32),
            pltpu.SMEM((1,), jnp.int32),
            pltpu.SemaphoreType.DMA((2,)),
        ],
    )
    return pl.pallas_call(
        _combine_kernel,
        grid_spec=gs,
        out_shape=[jax.ShapeDtypeStruct((GROUP_TOKENS[0], D_MODEL), F32),
                   jax.ShapeDtypeStruct((GROUP_TOKENS[1], D_MODEL), F32)],
        compiler_params=_cparams("arbitrary"),
    )(seg_start, seg_rows, seg_off, tot, x1, selw_t, rank_t, g, ye)


def kernel(x_prompt, x_sample, norm_mix_g, w_in, w_fourier, sink, w_out, norm_ffn_g, w_router,
           w_gate, w_up, w_down, norm_final_g):
    xp = x_prompt.reshape(-1, D_MODEL)
    xs = x_sample.reshape(-1, D_MODEL)
    cos_t, sin_t = _rope_tables()

    u, q, k, v = _inproj(xp, xs, norm_mix_g[0][None, :], w_in[0].astype(BF16), cos_t, sin_t)
    yf = _fourier(u, w_fourier[0].astype(BF16))
    ya = _attention(sink[0], q, k, v)
    x1, xn, aff_t = _outproj(xp, xs, yf, ya, w_out[0].astype(BF16), norm_ffn_g[0][None, :],
                             w_router[0].T.astype(BF16))

    thr0, need0 = _thresholds(aff_t[:, :GROUP_TOKENS[0]])
    thr1, need1 = _thresholds(aff_t[:, GROUP_TOKENS[0]:])
    selw_t, rank_t, m = _select(aff_t, jnp.concatenate([thr0, thr1]), jnp.concatenate([need0, need1]))

    seg_rows = ((m[:, :, 0].astype(jnp.int32) + CHUNK - 1) // CHUNK) * CHUNK
    seg_start = jnp.cumsum(seg_rows, axis=0) - seg_rows
    seg_off = jnp.cumsum(seg_rows, axis=1) - seg_rows
    tot = jnp.sum(seg_rows, axis=1)
    used = jnp.sum(seg_rows, axis=0)

    xe = _route_gather(seg_start, seg_rows, seg_off, tot, xn, selw_t, rank_t)
    ye = _ffn(used, xe, w_gate[0], w_up[0], w_down[0])
    yp, ys = _combine(seg_start, seg_rows, seg_off, tot, x1, selw_t, rank_t, norm_final_g[None, :], ye)
    return (yp.reshape(x_prompt.shape), ys.reshape(x_sample.shape))
```

```python
import functools

import numpy as np
import jax
import jax.numpy as jnp
from jax import lax
from jax.experimental import pallas as pl
from jax.experimental.pallas import tpu as pltpu

F32 = jnp.float32
BF16 = jnp.bfloat16

D_MODEL = 1024
SEQ = 16384
N_SEQ = 3
N_TOK = N_SEQ * SEQ
HEAD_DIM = 64
N_HEADS = 12
N_KV_HEADS = 4
GQA_GROUP = N_HEADS // N_KV_HEADS
ATTN_WIDTH = N_HEADS * HEAD_DIM
KV_WIDTH = N_KV_HEADS * HEAD_DIM
FOURIER_WIDTH = 256
FOURIER_GROUP_DIM = 64
MIX_WIDTH = FOURIER_WIDTH + ATTN_WIDTH
IN_WIDTH = MIX_WIDTH + 2 * KV_WIDTH
WINDOW = 128
ROPE_THETA = 10000.0
N_EXPERTS = 16
EC_CAPACITY_FACTOR = 2
EXPERT_D_FF = 2048
EPS = 1e-6
NEG = -1e30

LANES = 128
FFT_R = 128
VMEM_LIMIT = 48 * 1024 * 1024

TM = 512
TQ = 512
QB = 128
FF_TM = 512
FF_FC = 512


def _cparams(*sem):
    return pltpu.CompilerParams(dimension_semantics=sem, vmem_limit_bytes=VMEM_LIMIT)


def _x_specs():
    n_p = SEQ // TM
    return [pl.BlockSpec((TM, D_MODEL), lambda i, *_: (jnp.minimum(i, n_p - 1), 0)),
            pl.BlockSpec((TM, D_MODEL), lambda i, *_: (jnp.maximum(i - n_p, 0), 0))]


def _x_tile(xp_ref, xs_ref):
    return jnp.where(pl.program_id(0) < SEQ // TM, xp_ref[...], xs_ref[...])


def _inproj_kernel(xp_ref, xs_ref, g_ref, w_ref, cos_ref, sin_ref, u_ref, q_ref, k_ref, v_ref):
    x = _x_tile(xp_ref, xs_ref)
    ms = jnp.mean(x * x, axis=-1, keepdims=True)
    hn = (x * lax.rsqrt(ms + EPS) * g_ref[...]).astype(BF16)
    proj = jnp.dot(hn, w_ref[...], preferred_element_type=F32)
    u_ref[...] = proj[:, :FOURIER_WIDTH].astype(BF16)
    cos = cos_ref[...]
    sin = sin_ref[...]
    lane = lax.broadcasted_iota(jnp.int32, cos.shape, 1)
    first_half = (lane % HEAD_DIM) < (HEAD_DIM // 2)

    def rope(t):
        partner = jnp.where(first_half,
                            pltpu.roll(t, LANES - HEAD_DIM // 2, 1),
                            pltpu.roll(t, HEAD_DIM // 2, 1))
        return t * cos + partner * sin

    scale = HEAD_DIM ** -0.5 * LOG2E
    for c in range(ATTN_WIDTH // LANES):
        lo = FOURIER_WIDTH + c * LANES
        q_ref[:, c * LANES:(c + 1) * LANES] = (rope(proj[:, lo:lo + LANES]) * scale).astype(BF16)
    low_half = lane < HEAD_DIM
    for c in range(KV_WIDTH // LANES):
        lo = FOURIER_WIDTH + ATTN_WIDTH + c * LANES
        kc = rope(proj[:, lo:lo + LANES])
        vc = proj[:, lo + KV_WIDTH:lo + KV_WIDTH + LANES]
        for half, keep in enumerate((low_half, ~low_half)):
            g = 2 * c + half
            k_ref[:, g * LANES:(g + 1) * LANES] = jnp.where(keep, kc, 0.0).astype(BF16)
            v_ref[:, g * LANES:(g + 1) * LANES] = jnp.where(keep, vc, 0.0).astype(BF16)


def _inproj(xp, xs, g, w_in, cos_t, sin_t):
    n_pos_blocks = SEQ // TM
    return pl.pallas_call(
        _inproj_kernel,
        grid=(N_TOK // TM,),
        in_specs=_x_specs() + [
            pl.BlockSpec((1, D_MODEL), lambda i: (0, 0)),
            pl.BlockSpec((D_MODEL, IN_WIDTH), lambda i: (0, 0)),
            pl.BlockSpec((TM, LANES), lambda i: (i % n_pos_blocks, 0)),
            pl.BlockSpec((TM, LANES), lambda i: (i % n_pos_blocks, 0)),
        ],
        out_specs=[
            pl.BlockSpec((TM, FOURIER_WIDTH), lambda i: (i, 0)),
            pl.BlockSpec((TM, ATTN_WIDTH), lambda i: (i, 0)),
            pl.BlockSpec((TM, KV4), lambda i: (i, 0)),
            pl.BlockSpec((TM, KV4), lambda i: (i, 0)),
        ],
        out_shape=[
            jax.ShapeDtypeStruct((N_TOK, FOURIER_WIDTH), BF16),
            jax.ShapeDtypeStruct((N_TOK, ATTN_WIDTH), BF16),
            jax.ShapeDtypeStruct((N_TOK, KV4), BF16),
            jax.ShapeDtypeStruct((N_TOK, KV4), BF16),
        ],
        compiler_params=_cparams("parallel"),
    )(xp, xs, g, w_in, cos_t, sin_t)


def _rope_tables():
    half = HEAD_DIM // 2
    inv_freq = 1.0 / (ROPE_THETA ** (jnp.arange(half, dtype=F32) / half))
    ang = jnp.arange(SEQ, dtype=F32)[:, None] * inv_freq[None, :]
    cos = jnp.cos(ang)
    sin = jnp.sin(ang)
    reps = LANES // HEAD_DIM
    cos_t = jnp.tile(jnp.concatenate([cos, cos], axis=-1), (1, reps))
    sin_t = jnp.tile(jnp.concatenate([-sin, sin], axis=-1), (1, reps))
    return cos_t, sin_t


def _dft_tables():
    r = FFT_R
    c = np.arange(FOURIER_GROUP_DIM)
    ang_c = 2.0 * np.pi * np.outer(c, c) / FOURIER_GROUP_DIM
    n_groups = FOURIER_WIDTH // FOURIER_GROUP_DIM
    eye = np.eye(n_groups)
    ch_scale = FOURIER_GROUP_DIM ** -0.5
    c_blk = np.kron(eye, np.cos(ang_c)) * ch_scale
    s_blk = np.kron(eye, np.sin(ang_c)) * ch_scale
    cs = np.concatenate([c_blk, -s_blk], axis=1)
    k = np.arange(r)
    ang_r = 2.0 * np.pi * np.outer(k, k) / r
    st_scale = r ** -0.5
    wr = np.cos(ang_r) * st_scale
    wi = -np.sin(ang_r) * st_scale
    wbig = np.block([[wr, -wi], [wi, wr]])
    ang_t = 2.0 * np.pi * np.outer(k, k) / (r * r)
    tr = np.cos(ang_t)
    ti = -np.sin(ang_t)
    f = lambda a: jnp.asarray(a, dtype=F32)
    return f(cs).astype(BF16), f(wbig).astype(BF16), f(wr), f(wi), f(tr), f(ti)


FS_SB = 8
FS_KB = 8


def _fft_stage1_kernel(u_ref, cs_ref, wbig_ref, y_ref):
    cs = cs_ref[...]
    wbig = wbig_ref[...]
    fw = FOURIER_WIDTH
    for j in range(FS_SB):
        u = u_ref[:, j * fw:(j + 1) * fw]
        g = jnp.dot(u, cs, preferred_element_type=F32)
        gst = jnp.concatenate([g[:, :fw], g[:, fw:]], axis=0).astype(BF16)
        y = jnp.dot(wbig, gst, preferred_element_type=F32)
        y_ref[:, j * 2 * fw:j * 2 * fw + fw] = y[:FFT_R].astype(BF16)
        y_ref[:, j * 2 * fw + fw:(j + 1) * 2 * fw] = y[FFT_R:].astype(BF16)


def _fft_stage2_kernel(y_ref, wr_ref, wi_ref, tr_ref, ti_ref, wf_ref, o_ref):
    kb = pl.program_id(1)
    wr = wr_ref[...]
    wi = wi_ref[...]
    wf = wf_ref[...]
    fw = FOURIER_WIDTH
    for j in range(FS_KB):
        k1 = kb * FS_KB + j
        tr = tr_ref[pl.ds(k1, 1), :]
        ti = ti_ref[pl.ds(k1, 1), :]
        mr = wr * tr - wi * ti
        mi = wr * ti + wi * tr
        m2 = jnp.concatenate([mr, -mi], axis=1).astype(BF16)
        yb = y_ref[j * FFT_R:(j + 1) * FFT_R, :]
        yst = jnp.concatenate([yb[:, :fw], yb[:, fw:]], axis=0)
        z = jnp.dot(m2, yst, preferred_element_type=F32)
        yf = jnp.dot(z.astype(BF16), wf, preferred_element_type=F32)
        o_ref[:, j * fw:(j + 1) * fw] = yf.astype(BF16)


def _fourier(u, w_fourier):
    cs, wbig, wr, wi, tr, ti = _dft_tables()
    r, fw = FFT_R, FOURIER_WIDTH
    u2 = u.reshape(N_SEQ * r, r * fw)
    y = pl.pallas_call(
        _fft_stage1_kernel,
        grid=(N_SEQ, r // FS_SB),
        in_specs=[
            pl.BlockSpec((r, FS_SB * fw), lambda s, j: (s, j)),
            pl.BlockSpec((fw, 2 * fw), lambda s, j: (0, 0)),
            pl.BlockSpec((2 * r, 2 * r), lambda s, j: (0, 0)),
        ],
        out_specs=pl.BlockSpec((r, FS_SB * 2 * fw), lambda s, j: (s, j)),
        out_shape=jax.ShapeDtypeStruct((N_SEQ * r, r * 2 * fw), BF16),
        compiler_params=_cparams("parallel", "parallel"),
    )(u2, cs, wbig)
    y2 = y.reshape(N_SEQ * r * r, 2 * fw)
    full = lambda s, j: (0, 0)
    yf = pl.pallas_call(
        _fft_stage2_kernel,
        grid=(N_SEQ, r // FS_KB),
        in_specs=[
            pl.BlockSpec((FS_KB * r, 2 * fw), lambda s, j: (s * (r // FS_KB) + j, 0)),
            pl.BlockSpec((r, r), full),
            pl.BlockSpec((r, r), full),
            pl.BlockSpec((r, r), full),
            pl.BlockSpec((r, r), full),
            pl.BlockSpec((fw, fw), full),
        ],
        out_specs=pl.BlockSpec((r, FS_KB * fw), lambda s, j: (s, j)),
        out_shape=jax.ShapeDtypeStruct((N_SEQ * r, r * fw), BF16),
        compiler_params=_cparams("parallel", "parallel"),
    )(y2, wr, wi, tr, ti, w_fourier)
    return yf.reshape(N_TOK, fw)


LO_HEADS = (0, 1, 2, 6, 7, 8)
HI_HEADS = (3, 4, 5, 9, 10, 11)
Q_ORDER = tuple(h for pair in zip(LO_HEADS, HI_HEADS) for h in pair)
assert all(h // GQA_GROUP % 2 == 0 for h in LO_HEADS) and all(h // GQA_GROUP % 2 == 1 for h in HI_HEADS)
KV4 = 2 * KV_WIDTH
LOG2E = 1.4426950408889634


def _attn_kernel(sink_ref, q_ref, kp_ref, kc_ref, kn_ref, vp_ref, vc_ref, vn_ref, o_ref,
                 kext, vext):
    i = pl.program_id(1)
    n_i = pl.num_programs(1)
    kext[0:QB, :] = kp_ref[...]
    kext[QB:QB + TQ, :] = kc_ref[...]
    kext[QB + TQ:, :] = kn_ref[...]
    vext[0:QB, :] = vp_ref[...]
    vext[QB:QB + TQ, :] = vc_ref[...]
    vext[QB + TQ:, :] = vn_ref[...]

    rows = GQA_GROUP * QB
    a = lax.broadcasted_iota(jnp.int32, (rows, QB), 0) % QB
    j = lax.broadcasted_iota(jnp.int32, (rows, QB), 1)
    band_prev = jnp.where(j >= a, 0.0, NEG)
    band_next = jnp.where(j <= a, 0.0, NEG)
    row_id = lax.broadcasted_iota(jnp.int32, (rows, 1), 0)
    low_half = lax.broadcasted_iota(jnp.int32, (rows, LANES), 1) < HEAD_DIM

    def softmax_half(s, sinks, bias_prev, bias_next):
        s_p = s[:, :QB] + bias_prev
        s_c = s[:, QB:2 * QB]
        s_n = s[:, 2 * QB:] + bias_next
        sink = jnp.where(row_id < QB, sinks[0], jnp.where(row_id < 2 * QB, sinks[1], sinks[2])) * LOG2E
        m = jnp.max(jnp.maximum(jnp.maximum(s_p, s_c), s_n), axis=-1, keepdims=True)
        m = jnp.maximum(m, sink)
        p_p = jnp.exp2(s_p - m)
        p_c = jnp.exp2(s_c - m)
        p_n = jnp.exp2(s_n - m)
        denom = jnp.sum(p_p + p_c + p_n, axis=-1, keepdims=True) + jnp.exp2(sink - m)
        return [p_p.astype(BF16), p_c.astype(BF16), p_n.astype(BF16)], 1.0 / denom

    for sb in range(TQ // QB):
        blk = i * (TQ // QB) + sb
        bias_prev = band_prev + jnp.where(blk == 0, NEG, 0.0)
        bias_next = band_next + jnp.where(blk == n_i * (TQ // QB) - 1, NEG, 0.0)
        for pr in range(N_KV_HEADS // 2):
            groups = [GQA_GROUP * pr + m for m in range(GQA_GROUP)]
            q3 = jnp.concatenate([q_ref[sb * QB:(sb + 1) * QB, g * LANES:(g + 1) * LANES] for g in groups],
                                 axis=0)
            kp = kext[sb * QB:sb * QB + 3 * QB, pr * 2 * LANES:(pr + 1) * 2 * LANES]
            vp = vext[sb * QB:sb * QB + 3 * QB, pr * 2 * LANES:(pr + 1) * 2 * LANES]
            kcat = jnp.concatenate([kp[:, :LANES], kp[:, LANES:]], axis=0)
            vcat = jnp.concatenate([vp[:, :LANES], vp[:, LANES:]], axis=0)
            s = lax.dot_general(q3, kcat, (((1,), (1,)), ((), ())), preferred_element_type=F32)
            p_lo, r_lo = softmax_half(s[:, :3 * QB], [sink_ref[LO_HEADS[g]] for g in groups],
                                      bias_prev, bias_next)
            p_hi, r_hi = softmax_half(s[:, 3 * QB:], [sink_ref[HI_HEADS[g]] for g in groups],
                                      bias_prev, bias_next)
            p = jnp.concatenate(p_lo + p_hi, axis=1)
            o = jnp.dot(p, vcat, preferred_element_type=F32) * jnp.where(low_half, r_lo, r_hi)
            for m, g in enumerate(groups):
                o_ref[sb * QB:(sb + 1) * QB, g * LANES:(g + 1) * LANES] = o[m * QB:(m + 1) * QB].astype(BF16)


def _attention(sink, q, k, v):
    nqb = SEQ // QB
    per = TQ // QB
    cur = lambda s, i: (s * (SEQ // TQ) + i, 0)
    prev = lambda s, i: (s * nqb + jnp.maximum(i * per - 1, 0), 0)
    nxt = lambda s, i: (s * nqb + jnp.minimum(i * per + per, nqb - 1), 0)
    return pl.pallas_call(
        _attn_kernel,
        grid=(N_SEQ, SEQ // TQ),
        in_specs=[
            pl.BlockSpec(memory_space=pltpu.SMEM),
            pl.BlockSpec((TQ, ATTN_WIDTH), cur),
            pl.BlockSpec((QB, KV4), prev),
            pl.BlockSpec((TQ, KV4), cur),
            pl.BlockSpec((QB, KV4), nxt),
            pl.BlockSpec((QB, KV4), prev),
            pl.BlockSpec((TQ, KV4), cur),
            pl.BlockSpec((QB, KV4), nxt),
        ],
        out_specs=pl.BlockSpec((TQ, ATTN_WIDTH), cur),
        out_shape=jax.ShapeDtypeStruct((N_TOK, ATTN_WIDTH), BF16),
        scratch_shapes=[pltpu.VMEM((TQ + 2 * QB, KV4), BF16),
                        pltpu.VMEM((TQ + 2 * QB, KV4), BF16)],
        compiler_params=_cparams("parallel", "parallel"),
    )(sink, q, k, k, k, v, v, v)


def _outproj_kernel(xp_ref, xs_ref, yf_ref, ya_ref, wo_ref, g_ref, wr_ref, x1_ref, xn_ref, aff_ref):
    mix = jnp.dot(yf_ref[...], wo_ref[:FOURIER_WIDTH, :], preferred_element_type=F32)
    mix = mix + jnp.dot(ya_ref[...], wo_ref[FOURIER_WIDTH:, :], preferred_element_type=F32)
    x1 = _x_tile(xp_ref, xs_ref) + mix
    x1_ref[...] = x1
    ms = jnp.mean(x1 * x1, axis=-1, keepdims=True)
    xn = (x1 * lax.rsqrt(ms + EPS) * g_ref[...]).astype(BF16)
    xn_ref[...] = xn
    logits = lax.dot_general(wr_ref[...], xn, (((1,), (1,)), ((), ())), preferred_element_type=F32)
    mx = jnp.max(logits, axis=0, keepdims=True)
    ex = jnp.exp(logits - mx)
    aff_ref[...] = ex / jnp.sum(ex, axis=0, keepdims=True)


def _outproj(xp, xs, yf, ya, w_out, g, w_router_t):
    return pl.pallas_call(
        _outproj_kernel,
        grid=(N_TOK // TM,),
        in_specs=_x_specs() + [
            pl.BlockSpec((TM, FOURIER_WIDTH), lambda i: (i, 0)),
            pl.BlockSpec((TM, ATTN_WIDTH), lambda i: (i, 0)),
            pl.BlockSpec((D_MODEL, D_MODEL), lambda i: (0, 0)),
            pl.BlockSpec((1, D_MODEL), lambda i: (0, 0)),
            pl.BlockSpec((N_EXPERTS, D_MODEL), lambda i: (0, 0)),
        ],
        out_specs=[
            pl.BlockSpec((TM, D_MODEL), lambda i: (i, 0)),
            pl.BlockSpec((TM, D_MODEL), lambda i: (i, 0)),
            pl.BlockSpec((N_EXPERTS, TM), lambda i: (0, i)),
        ],
        out_shape=[
            jax.ShapeDtypeStruct((N_TOK, D_MODEL), F32),
            jax.ShapeDtypeStruct((N_TOK, D_MODEL), BF16),
            jax.ShapeDtypeStruct((N_EXPERTS, N_TOK), F32),
        ],
        compiler_params=_cparams("parallel"),
    )(xp, xs, yf, ya, w_out, g, w_router_t)


RB = 512
N_RB = N_TOK // RB
GROUP_TOKENS = (SEQ, 2 * SEQ)
GROUP0_BLOCKS = GROUP_TOKENS[0] // RB
CHUNK = 8
TILE_ROWS = 256
CHUNKS_PER_TILE = TILE_ROWS // CHUNK
SLOT_TILES = 14
SLOT_ROWS = SLOT_TILES * FF_TM
assert SLOT_ROWS >= EC_CAPACITY_FACTOR * N_TOK // N_EXPERTS + N_RB * (CHUNK - 1)
SLOT_CHUNKS = SLOT_ROWS // CHUNK
TRASH_CHUNK = N_EXPERTS * SLOT_CHUNKS
MAX_CHUNKS = (N_EXPERTS * (RB + CHUNK - 1)) // CHUNK
CHUNK_SHAPE = (2 * CHUNK, D_MODEL // 2)


def _pack_chunks(rows_f32):
    n = rows_f32.shape[0] // CHUNK
    r3 = rows_f32.reshape(n, CHUNK, D_MODEL)
    half = D_MODEL // 2
    return jnp.concatenate([r3[:, :, :half], r3[:, :, half:]], axis=1).astype(BF16)


def _unpack_chunks(chunks_bf16):
    n = chunks_bf16.shape[0]
    c3 = chunks_bf16.astype(F32)
    left = c3[:, :CHUNK, :].reshape(n * CHUNK, D_MODEL // 2)
    right = c3[:, CHUNK:, :].reshape(n * CHUNK, D_MODEL // 2)
    return jnp.concatenate([left, right], axis=1).astype(BF16)
LIST_LEN = ((MAX_CHUNKS + CHUNKS_PER_TILE - 1) // CHUNKS_PER_TILE) * CHUNKS_PER_TILE


def _thresh_kernel(aff_ref, thr_ref, need_ref, *, cap):
    aff = aff_ref[...]

    def body(i, cur):
        cand = cur | jnp.left_shift(jnp.int32(1), 30 - i)
        cnt = jnp.sum(jnp.where(aff >= pltpu.bitcast(cand, F32), 1.0, 0.0), axis=1, keepdims=True)
        return jnp.where(cnt >= cap, cand, cur)

    thr = pltpu.bitcast(lax.fori_loop(0, 31, body, jnp.zeros((N_EXPERTS, 1), jnp.int32)), F32)
    n_gt = jnp.sum(jnp.where(aff > thr, 1.0, 0.0), axis=1, keepdims=True)
    thr_ref[...] = jnp.broadcast_to(thr, thr_ref.shape)
    need_ref[...] = jnp.broadcast_to(cap - n_gt, need_ref.shape)


def _thresholds(aff_group):
    n = aff_group.shape[1]
    cap = EC_CAPACITY_FACTOR * n // N_EXPERTS
    return pl.pallas_call(
        functools.partial(_thresh_kernel, cap=float(cap)),
        out_shape=[jax.ShapeDtypeStruct((N_EXPERTS, LANES), F32),
                   jax.ShapeDtypeStruct((N_EXPERTS, LANES), F32)],
        compiler_params=pltpu.CompilerParams(vmem_limit_bytes=VMEM_LIMIT),
    )(aff_group)


def _select_kernel(aff_ref, thr_ref, need_ref, utri_ref, selw_ref, rank_ref, m_ref, eq_seen):
    rb = pl.program_id(0)

    @pl.when((rb == 0) | (rb == GROUP0_BLOCKS))
    def _():
        eq_seen[...] = jnp.zeros_like(eq_seen)

    aff = aff_ref[...]
    thr = thr_ref[:, 0:1]
    need = need_ref[:, 0:1]
    utri = utri_ref[...]
    eq = jnp.where(aff == thr, 1.0, 0.0)
    eq_before = jnp.dot(eq.astype(BF16), utri, preferred_element_type=F32) + eq_seen[...]
    sel = (aff > thr) | ((aff == thr) & (eq_before < need))
    eq_seen[...] += jnp.sum(eq, axis=1, keepdims=True)
    self = jnp.where(sel, 1.0, 0.0)
    rank_ref[...] = jnp.dot(self.astype(BF16), utri, preferred_element_type=F32)
    selw_ref[...] = jnp.where(sel, aff, 0.0)
    m_ref[0] = jnp.broadcast_to(jnp.sum(self, axis=1, keepdims=True), (N_EXPERTS, LANES))


def _select(aff_t, thr, need):
    utri = jnp.asarray(np.triu(np.ones((RB, RB), np.float32), 1), dtype=BF16)
    grp = lambda rb: (jnp.where(rb >= GROUP0_BLOCKS, 1, 0), 0)
    return pl.pallas_call(
        _select_kernel,
        grid=(N_RB,),
        in_specs=[
            pl.BlockSpec((N_EXPERTS, RB), lambda rb: (0, rb)),
            pl.BlockSpec((N_EXPERTS, LANES), grp),
            pl.BlockSpec((N_EXPERTS, LANES), grp),
            pl.BlockSpec((RB, RB), lambda rb: (0, 0)),
        ],
        out_specs=[
            pl.BlockSpec((N_EXPERTS, RB), lambda rb: (0, rb)),
            pl.BlockSpec((N_EXPERTS, RB), lambda rb: (0, rb)),
            pl.BlockSpec((1, N_EXPERTS, LANES), lambda rb: (rb, 0, 0)),
        ],
        out_shape=[
            jax.ShapeDtypeStruct((N_EXPERTS, N_TOK), F32),
            jax.ShapeDtypeStruct((N_EXPERTS, N_TOK), F32),
            jax.ShapeDtypeStruct((N_RB, N_EXPERTS, LANES), F32),
        ],
        scratch_shapes=[pltpu.VMEM((N_EXPERTS, 1), F32)],
        compiler_params=_cparams("arbitrary"),
    )(aff_t, thr, need, utri)


def _fill_chunk_list(list_ref, which, rb, seg_start_ref, seg_rows_ref):
    cnt = jnp.int32(0)
    for e in range(N_EXPERTS):
        base = e * SLOT_CHUNKS + seg_start_ref[rb, e] // CHUNK

        def body(k, c, base=base):
            list_ref[which, c] = base + k
            return c + 1

        cnt = lax.fori_loop(0, seg_rows_ref[rb, e] // CHUNK, body, cnt)
    return cnt


WIN_ROWS = 128


def _mark_windows(scr, tgt_row, value, r0, lo, n):
    first = jnp.maximum(lo, r0)
    last = jnp.minimum(lo + n, r0 + TILE_ROWS)
    for w in range(TILE_ROWS // WIN_ROWS):
        start = first + w * WIN_ROWS

        @pl.when(start < last)
        def _(start=start):
            win = pl.ds(pl.multiple_of(start - r0, CHUNK), WIN_ROWS)
            row = (lax.broadcasted_iota(jnp.int32, (WIN_ROWS, RB), 0) + start).astype(F32)
            scr[win, :] = jnp.where(tgt_row == row, value, scr[win, :])


def _gather_kernel(seg_start_ref, seg_rows_ref, seg_off_ref, tot_ref,
                   x_ref, selw_ref, rank_ref, xe_hbm,
                   p_scr, obuf, zbuf, dst_list, pending, sem, zsem):
    rb = pl.program_id(0)
    n_rows = tot_ref[rb]
    n_chunks = n_rows // CHUNK
    n_tiles = (n_rows + TILE_ROWS - 1) // TILE_ROWS

    @pl.when(rb == 0)
    def _():
        pending[0] = 0
        pending[1] = 0
        p_scr[...] = jnp.zeros_like(p_scr)

    _fill_chunk_list(dst_list, 0, rb, seg_start_ref, seg_rows_ref)

    def tile_copy(slot):
        return pltpu.make_async_copy(obuf.at[slot], xe_hbm.at[pl.ds(0, CHUNKS_PER_TILE)], sem.at[slot])

    e_iota = lax.broadcasted_iota(jnp.int32, (N_EXPERTS, 1), 0)
    off = jnp.zeros((N_EXPERTS, 1), F32)
    for e in range(N_EXPERTS):
        off = jnp.where(e_iota == e, seg_off_ref[rb, e].astype(F32), off)
    tgt = jnp.where(selw_ref[...] > 0.0, rank_ref[...] + off, -1.0)
    x = x_ref[...]

    def tile_body(t, carry):
        slot = t % 2
        r0 = t * TILE_ROWS

        @pl.when(pending[slot] == 1)
        def _():
            tile_copy(slot).wait()

        p_scr[0:TILE_ROWS, :] = jnp.zeros((TILE_ROWS, RB), F32)
        for e in range(N_EXPERTS):
            _mark_windows(p_scr, tgt[e:e + 1, :], 1.0, r0, seg_off_ref[rb, e], seg_rows_ref[rb, e])

        obuf[slot] = _pack_chunks(jnp.dot(p_scr[0:TILE_ROWS, :].astype(BF16), x,
                                          preferred_element_type=F32))

        for i in range(CHUNKS_PER_TILE):
            g = t * CHUNKS_PER_TILE + i
            dst = jnp.where(g < n_chunks, dst_list[0, jnp.minimum(g, n_chunks - 1)],
                            TRASH_CHUNK + slot * CHUNKS_PER_TILE + i)
            pltpu.make_async_copy(obuf.at[slot, i], xe_hbm.at[dst], sem.at[slot]).start()
        pending[slot] = 1
        return carry

    lax.fori_loop(0, n_tiles, tile_body, 0)

    @pl.when(rb == N_RB - 1)
    def _():
        for slot in range(2):
            @pl.when(pending[slot] == 1)
            def _(slot=slot):
                tile_copy(slot).wait()
        zbuf[...] = jnp.zeros_like(zbuf)
        for e in range(N_EXPERTS):
            used = seg_start_ref[rb, e] + seg_rows_ref[rb, e]
            end = ((used + FF_TM - 1) // FF_TM) * FF_TM

            def zcopy(k, used=used, e=e):
                return pltpu.make_async_copy(zbuf, xe_hbm.at[e * SLOT_CHUNKS + used // CHUNK + k], zsem)

            def zstart(k, c, zcopy=zcopy):
                zcopy(k).start()
                return c

            def zwait(k, c, zcopy=zcopy):
                zcopy(k).wait()
                return c

            lax.fori_loop(0, (end - used) // CHUNK, zstart, 0)
            lax.fori_loop(0, (end - used) // CHUNK, zwait, 0)


def _route_gather(seg_start, seg_rows, seg_off, tot, xn, selw_t, rank_t):
    gs = pltpu.PrefetchScalarGridSpec(
        num_scalar_prefetch=4,
        grid=(N_RB,),
        in_specs=[
            pl.BlockSpec((RB, D_MODEL), lambda rb, *_: (rb, 0)),
            pl.BlockSpec((N_EXPERTS, RB), lambda rb, *_: (0, rb)),
            pl.BlockSpec((N_EXPERTS, RB), lambda rb, *_: (0, rb)),
        ],
        out_specs=pl.BlockSpec(memory_space=pl.ANY),
        scratch_shapes=[
            pltpu.VMEM((TILE_ROWS + WIN_ROWS, RB), F32),
            pltpu.VMEM((2, CHUNKS_PER_TILE) + CHUNK_SHAPE, BF16),
            pltpu.VMEM(CHUNK_SHAPE, BF16),
            pltpu.SMEM((1, LIST_LEN), jnp.int32),
            pltpu.SMEM((2,), jnp.int32),
            pltpu.SemaphoreType.DMA((2,)),
            pltpu.SemaphoreType.DMA,
        ],
    )
    return pl.pallas_call(
        _gather_kernel,
        grid_spec=gs,
        out_shape=jax.ShapeDtypeStruct((TRASH_CHUNK + 2 * CHUNKS_PER_TILE,) + CHUNK_SHAPE, BF16),
        compiler_params=_cparams("arbitrary"),
    )(seg_start, seg_rows, seg_off, tot, xn, selw_t, rank_t)


W_PIECES_PER_MATRIX = 4
W_PIECES = 3 * W_PIECES_PER_MATRIX
assert SLOT_TILES > W_PIECES
WG_PIECE_ROWS = D_MODEL // W_PIECES_PER_MATRIX
WD_PIECE_ROWS = EXPERT_D_FF // W_PIECES_PER_MATRIX
FFN_VMEM_LIMIT = 56 * 1024 * 1024


def _ffn_kernel(used_ref, x_ref, wg_hbm, wu_hbm, wd_hbm, o_ref,
                wg_bf, wu_bf, wd_bf, stage_up, stage_dn, sem):
    e = pl.program_id(0)
    i = pl.program_id(1)
    slot = e % 2

    matrices = ((wg_hbm, stage_up, wg_bf, WG_PIECE_ROWS),
                (wu_hbm, stage_up, wu_bf, WG_PIECE_ROWS),
                (wd_hbm, stage_dn, wd_bf, WD_PIECE_ROWS))

    def piece(expert, k, buf, finish):
        def of_kind(kind):
            w_hbm, stage, w_bf, rows = matrices[kind]
            r0 = pl.multiple_of((k - kind * W_PIECES_PER_MATRIX) * rows, rows)
            cp = pltpu.make_async_copy(w_hbm.at[expert, pl.ds(r0, rows), :], stage.at[buf], sem.at[buf])
            if finish:
                cp.wait()
                w_bf[expert % 2, pl.ds(r0, rows), :] = stage[buf].astype(BF16)
            else:
                cp.start()

        if isinstance(k, int):
            of_kind(k // W_PIECES_PER_MATRIX)
        else:
            for kind in range(len(matrices)):
                pl.when(k // W_PIECES_PER_MATRIX == kind)(functools.partial(of_kind, kind))

    @pl.when((e == 0) & (i == 0))
    def _():
        for k in range(W_PIECES):
            piece(0, k, k % 2, finish=False)
            piece(0, k, k % 2, finish=True)

    @pl.when(e + 1 < N_EXPERTS)
    def _():
        @pl.when((i >= 1) & (i <= W_PIECES))
        def _():
            piece(e + 1, i - 1, (i - 1) % 2, finish=True)

        @pl.when(i < W_PIECES)
        def _():
            piece(e + 1, i, i % 2, finish=False)

    @pl.when(i * FF_TM < used_ref[e])
    def _():
        x = _unpack_chunks(x_ref[...])
        acc = jnp.zeros((FF_TM, D_MODEL), F32)
        for c in range(EXPERT_D_FF // FF_FC):
            gt = jnp.dot(x, wg_bf[slot, :, c * FF_FC:(c + 1) * FF_FC], preferred_element_type=F32)
            up = jnp.dot(x, wu_bf[slot, :, c * FF_FC:(c + 1) * FF_FC], preferred_element_type=F32)
            h = (gt * (1.0 / (1.0 + jnp.exp(-gt))) * up).astype(BF16)
            acc = acc + jnp.dot(h, wd_bf[slot, c * FF_FC:(c + 1) * FF_FC, :], preferred_element_type=F32)
        o_ref[...] = _pack_chunks(acc)


def _ffn(used, xe, w_gate, w_up, w_down):
    def rows(e, i, used_ref):
        last = jnp.maximum((used_ref[e] + FF_TM - 1) // FF_TM - 1, 0)
        return (e * SLOT_TILES + jnp.minimum(i, last), 0, 0)

    tile_chunks = (FF_TM // CHUNK,) + CHUNK_SHAPE
    gs = pltpu.PrefetchScalarGridSpec(
        num_scalar_prefetch=1,
        grid=(N_EXPERTS, SLOT_TILES),
        in_specs=[
            pl.BlockSpec(tile_chunks, rows),
            pl.BlockSpec(memory_space=pl.ANY),
            pl.BlockSpec(memory_space=pl.ANY),
            pl.BlockSpec(memory_space=pl.ANY),
        ],
        out_specs=pl.BlockSpec(tile_chunks, rows),
        scratch_shapes=[
            pltpu.VMEM((2, D_MODEL, EXPERT_D_FF), BF16),
            pltpu.VMEM((2, D_MODEL, EXPERT_D_FF), BF16),
            pltpu.VMEM((2, EXPERT_D_FF, D_MODEL), BF16),
            pltpu.VMEM((2, WG_PIECE_ROWS, EXPERT_D_FF), F32),
            pltpu.VMEM((2, WD_PIECE_ROWS, D_MODEL), F32),
            pltpu.SemaphoreType.DMA((2,)),
        ],
    )
    return pl.pallas_call(
        _ffn_kernel,
        grid_spec=gs,
        out_shape=jax.ShapeDtypeStruct((N_EXPERTS * SLOT_CHUNKS,) + CHUNK_SHAPE, BF16),
        compiler_params=pltpu.CompilerParams(dimension_semantics=("arbitrary", "arbitrary"),
                                             vmem_limit_bytes=FFN_VMEM_LIMIT),
    )(used, xe, w_gate, w_up, w_down)


def _combine_kernel(seg_start_ref, seg_rows_ref, seg_off_ref, tot_ref,
                    x1_ref, selw_ref, rank_ref, g_ref, ye_hbm, op_ref, os_ref,
                    w_scr, ybuf, acc, src_list, first_slot, sem):
    rb = pl.program_id(0)
    cur = rb % 2
    nxt_rb = jnp.minimum(rb + 1, N_RB - 1)

    def tiles_of(b):
        return (tot_ref[b] + TILE_ROWS - 1) // TILE_ROWS

    n_tiles = tiles_of(rb)
    n_tiles_next = jnp.where(rb + 1 < N_RB, tiles_of(nxt_rb), 0)

    def fetch(b, which, t, slot):
        n_chunks = tot_ref[b] // CHUNK
        for i in range(CHUNKS_PER_TILE):
            g = t * CHUNKS_PER_TILE + i
            src = jnp.where(g < n_chunks, src_list[which, jnp.minimum(g, n_chunks - 1)], 0)
            pltpu.make_async_copy(ye_hbm.at[src], ybuf.at[slot, i], sem.at[slot]).start()

    def tile_wait(slot):
        pltpu.make_async_copy(ye_hbm.at[pl.ds(0, CHUNKS_PER_TILE)], ybuf.at[slot], sem.at[slot]).wait()

    @pl.when(rb == 0)
    def _():
        w_scr[...] = jnp.zeros_like(w_scr)
        _fill_chunk_list(src_list, 0, 0, seg_start_ref, seg_rows_ref)
        first_slot[0] = 0

        @pl.when(n_tiles > 0)
        def _():
            fetch(0, 0, 0, 0)

    @pl.when(rb + 1 < N_RB)
    def _():
        _fill_chunk_list(src_list, 1 - cur, nxt_rb, seg_start_ref, seg_rows_ref)

    slot0 = first_slot[0]
    first_slot[0] = (slot0 + n_tiles) % 2

    @pl.when((n_tiles == 0) & (n_tiles_next > 0))
    def _():
        fetch(nxt_rb, 1 - cur, 0, slot0)

    e_iota = lax.broadcasted_iota(jnp.int32, (N_EXPERTS, 1), 0)
    off = jnp.zeros((N_EXPERTS, 1), F32)
    for e in range(N_EXPERTS):
        off = jnp.where(e_iota == e, seg_off_ref[rb, e].astype(F32), off)
    selw = selw_ref[...]
    tgt = jnp.where(selw > 0.0, rank_ref[...] + off, -1.0)
    acc[...] = x1_ref[...]

    def tile_body(t, carry):
        slot = (slot0 + t) % 2
        r0 = t * TILE_ROWS

        @pl.when(t + 1 < n_tiles)
        def _():
            fetch(rb, cur, t + 1, 1 - slot)

        @pl.when((t + 1 == n_tiles) & (n_tiles_next > 0))
        def _():
            fetch(nxt_rb, 1 - cur, 0, 1 - slot)

        tile_wait(slot)
        w_scr[0:TILE_ROWS, :] = jnp.zeros((TILE_ROWS, RB), F32)
        for e in range(N_EXPERTS):
            _mark_windows(w_scr, tgt[e:e + 1, :], selw[e:e + 1, :], r0,
                          seg_off_ref[rb, e], seg_rows_ref[rb, e])

        acc[...] += lax.dot_general(w_scr[0:TILE_ROWS, :].astype(BF16), _unpack_chunks(ybuf[slot]),
                                    (((0,), (0,)), ((), ())), preferred_element_type=F32)
        return carry

    lax.fori_loop(0, n_tiles, tile_body, 0)
    x2 = acc[...]
    ms = jnp.mean(x2 * x2, axis=-1, keepdims=True)
    y = x2 * lax.rsqrt(ms + EPS) * g_ref[...]

    @pl.when(rb < GROUP0_BLOCKS)
    def _():
        op_ref[...] = y

    @pl.when(rb >= GROUP0_BLOCKS)
    def _():
        os_ref[...] = y


def _combine(seg_start, seg_rows, seg_off, tot, x1, selw_t, rank_t, g, ye):
    gs = pltpu.PrefetchScalarGridSpec(
        num_scalar_prefetch=4,
        grid=(N_RB,),
        in_specs=[
            pl.BlockSpec((RB, D_MODEL), lambda rb, *_: (rb, 0)),
            pl.BlockSpec((N_EXPERTS, RB), lambda rb, *_: (0, rb)),
            pl.BlockSpec((N_EXPERTS, RB), lambda rb, *_: (0, rb)),
            pl.BlockSpec((1, D_MODEL), lambda rb, *_: (0, 0)),
            pl.BlockSpec(memory_space=pl.ANY),
        ],
        out_specs=[
            pl.BlockSpec((RB, D_MODEL), lambda rb, *_: (jnp.minimum(rb, GROUP0_BLOCKS - 1), 0)),
            pl.BlockSpec((RB, D_MODEL), lambda rb, *_: (jnp.maximum(rb - GROUP0_BLOCKS, 0), 0)),
        ],
        scratch_shapes=[
            pltpu.VMEM((TILE_ROWS + WIN_ROWS, RB), F32),
            pltpu.VMEM((2, CHUNKS_PER_TILE) + CHUNK_SHAPE, BF16),
            pltpu.VMEM((RB, D_MODEL), F32),
            pltpu.SMEM((2, LIST_LEN), jnp.int32),
            pltpu.SMEM((1,), jnp.int32),
            pltpu.SemaphoreType.DMA((2,)),
        ],
    )
    return pl.pallas_call(
        _combine_kernel,
        grid_spec=gs,
        out_shape=[jax.ShapeDtypeStruct((GROUP_TOKENS[0], D_MODEL), F32),
                   jax.ShapeDtypeStruct((GROUP_TOKENS[1], D_MODEL), F32)],
        compiler_params=_cparams("arbitrary"),
    )(seg_start, seg_rows, seg_off, tot, x1, selw_t, rank_t, g, ye)


def kernel(x_prompt, x_sample, norm_mix_g, w_in, w_fourier, sink, w_out, norm_ffn_g, w_router,
           w_gate, w_up, w_down, norm_final_g):
    xp = x_prompt.reshape(-1, D_MODEL)
    xs = x_sample.reshape(-1, D_MODEL)
    cos_t, sin_t = _rope_tables()

    mix_cols = np.concatenate([
        np.arange(FOURIER_WIDTH),
        FOURIER_WIDTH + (np.asarray(Q_ORDER)[:, None] * HEAD_DIM + np.arange(HEAD_DIM)[None, :]).reshape(-1)])
    in_cols = np.concatenate([mix_cols, np.arange(MIX_WIDTH, IN_WIDTH)])
    w_in_r = w_in[0][:, in_cols].astype(BF16)
    w_out_r = w_out[0][mix_cols, :].astype(BF16)

    u, q, k, v = _inproj(xp, xs, norm_mix_g[0][None, :], w_in_r, cos_t, sin_t)
    yf = _fourier(u, w_fourier[0].astype(BF16))
    ya = _attention(sink[0], q, k, v)
    x1, xn, aff_t = _outproj(xp, xs, yf, ya, w_out_r, norm_ffn_g[0][None, :],
                             w_router[0].T.astype(BF16))

    thr0, need0 = _thresholds(aff_t[:, :GROUP_TOKENS[0]])
    thr1, need1 = _thresholds(aff_t[:, GROUP_TOKENS[0]:])
    selw_t, rank_t, m = _select(aff_t, jnp.concatenate([thr0, thr1]), jnp.concatenate([need0, need1]))

    seg_rows = ((m[:, :, 0].astype(jnp.int32) + CHUNK - 1) // CHUNK) * CHUNK
    seg_start = jnp.cumsum(seg_rows, axis=0) - seg_rows
    seg_off = jnp.cumsum(seg_rows, axis=1) - seg_rows
    tot = jnp.sum(seg_rows, axis=1)
    used = jnp.sum(seg_rows, axis=0)

    xe = _route_gather(seg_start, seg_rows, seg_off, tot, xn, selw_t, rank_t)
    ye = _ffn(used, xe, w_gate[0], w_up[0], w_down[0])
    yp, ys = _combine(seg_start, seg_rows, seg_off, tot, x1, selw_t, rank_t, norm_final_g[None, :], ye)
    return (yp.reshape(x_prompt.shape), ys.reshape(x_sample.shape))
```

```python
import functools

import numpy as np
import jax
import jax.numpy as jnp
from jax import lax
from jax.experimental import pallas as pl
from jax.experimental.pallas import tpu as pltpu

F32 = jnp.float32
BF16 = jnp.bfloat16

D_MODEL = 1024
SEQ = 16384
N_SEQ = 3
N_TOK = N_SEQ * SEQ
HEAD_DIM = 64
N_HEADS = 12
N_KV_HEADS = 4
GQA_GROUP = N_HEADS // N_KV_HEADS
ATTN_WIDTH = N_HEADS * HEAD_DIM
KV_WIDTH = N_KV_HEADS * HEAD_DIM
FOURIER_WIDTH = 256
FOURIER_GROUP_DIM = 64
MIX_WIDTH = FOURIER_WIDTH + ATTN_WIDTH
IN_WIDTH = MIX_WIDTH + 2 * KV_WIDTH
WINDOW = 128
ROPE_THETA = 10000.0
N_EXPERTS = 16
EC_CAPACITY_FACTOR = 2
EXPERT_D_FF = 2048
EPS = 1e-6
NEG = -1e30

LANES = 128
FFT_R = 128
VMEM_LIMIT = 48 * 1024 * 1024

TM = 512
TQ = 512
QB = 128
FF_TM = 512
FF_FC = 512


def _cparams(*sem):
    return pltpu.CompilerParams(dimension_semantics=sem, vmem_limit_bytes=VMEM_LIMIT)


def _x_specs():
    n_p = SEQ // TM
    return [pl.BlockSpec((TM, D_MODEL), lambda i, *_: (jnp.minimum(i, n_p - 1), 0)),
            pl.BlockSpec((TM, D_MODEL), lambda i, *_: (jnp.maximum(i - n_p, 0), 0))]


def _x_tile(xp_ref, xs_ref):
    return jnp.where(pl.program_id(0) < SEQ // TM, xp_ref[...], xs_ref[...])


def _inproj_kernel(xp_ref, xs_ref, g_ref, w_ref, cos_ref, sin_ref, u_ref, q_ref, k_ref, v_ref):
    x = _x_tile(xp_ref, xs_ref)
    ms = jnp.mean(x * x, axis=-1, keepdims=True)
    hn = (x * lax.rsqrt(ms + EPS) * g_ref[...]).astype(BF16)
    proj = jnp.dot(hn, w_ref[...], preferred_element_type=F32)
    u_ref[...] = proj[:, :FOURIER_WIDTH].astype(BF16)
    cos = cos_ref[...]
    sin = sin_ref[...]
    lane = lax.broadcasted_iota(jnp.int32, cos.shape, 1)
    first_half = (lane % HEAD_DIM) < (HEAD_DIM // 2)

    def rope(t):
        partner = jnp.where(first_half,
                            pltpu.roll(t, LANES - HEAD_DIM // 2, 1),
                            pltpu.roll(t, HEAD_DIM // 2, 1))
        return t * cos + partner * sin

    scale = HEAD_DIM ** -0.5 * LOG2E
    for c in range(ATTN_WIDTH // LANES):
        lo = FOURIER_WIDTH + c * LANES
        q_ref[:, c * LANES:(c + 1) * LANES] = (rope(proj[:, lo:lo + LANES]) * scale).astype(BF16)
    low_half = lane < HEAD_DIM
    for c in range(KV_WIDTH // LANES):
        lo = FOURIER_WIDTH + ATTN_WIDTH + c * LANES
        kc = rope(proj[:, lo:lo + LANES])
        vc = proj[:, lo + KV_WIDTH:lo + KV_WIDTH + LANES]
        for half, keep in enumerate((low_half, ~low_half)):
            g = 2 * c + half
            k_ref[:, g * LANES:(g + 1) * LANES] = jnp.where(keep, kc, 0.0).astype(BF16)
            v_ref[:, g * LANES:(g + 1) * LANES] = jnp.where(keep, vc, 0.0).astype(BF16)


def _inproj(xp, xs, g, w_in, cos_t, sin_t):
    n_pos_blocks = SEQ // TM
    return pl.pallas_call(
        _inproj_kernel,
        grid=(N_TOK // TM,),
        in_specs=_x_specs() + [
            pl.BlockSpec((1, D_MODEL), lambda i: (0, 0)),
            pl.BlockSpec((D_MODEL, IN_WIDTH), lambda i: (0, 0)),
            pl.BlockSpec((TM, LANES), lambda i: (i % n_pos_blocks, 0)),
            pl.BlockSpec((TM, LANES), lambda i: (i % n_pos_blocks, 0)),
        ],
        out_specs=[
            pl.BlockSpec((TM, FOURIER_WIDTH), lambda i: (i, 0)),
            pl.BlockSpec((TM, ATTN_WIDTH), lambda i: (i, 0)),
            pl.BlockSpec((TM, KV4), lambda i: (i, 0)),
            pl.BlockSpec((TM, KV4), lambda i: (i, 0)),
        ],
        out_shape=[
            jax.ShapeDtypeStruct((N_TOK, FOURIER_WIDTH), BF16),
            jax.ShapeDtypeStruct((N_TOK, ATTN_WIDTH), BF16),
            jax.ShapeDtypeStruct((N_TOK, KV4), BF16),
            jax.ShapeDtypeStruct((N_TOK, KV4), BF16),
        ],
        compiler_params=_cparams("parallel"),
    )(xp, xs, g, w_in, cos_t, sin_t)


def _rope_tables():
    half = HEAD_DIM // 2
    inv_freq = 1.0 / (ROPE_THETA ** (jnp.arange(half, dtype=F32) / half))
    ang = jnp.arange(SEQ, dtype=F32)[:, None] * inv_freq[None, :]
    cos = jnp.cos(ang)
    sin = jnp.sin(ang)
    reps = LANES // HEAD_DIM
    cos_t = jnp.tile(jnp.concatenate([cos, cos], axis=-1), (1, reps))
    sin_t = jnp.tile(jnp.concatenate([-sin, sin], axis=-1), (1, reps))
    return cos_t, sin_t


def _dft_tables():
    r = FFT_R
    c = np.arange(FOURIER_GROUP_DIM)
    ang_c = 2.0 * np.pi * np.outer(c, c) / FOURIER_GROUP_DIM
    n_groups = FOURIER_WIDTH // FOURIER_GROUP_DIM
    eye = np.eye(n_groups)
    ch_scale = FOURIER_GROUP_DIM ** -0.5
    c_blk = np.kron(eye, np.cos(ang_c)) * ch_scale
    s_blk = np.kron(eye, np.sin(ang_c)) * ch_scale
    cs = np.concatenate([c_blk, -s_blk], axis=1)
    k = np.arange(r)
    ang_r = 2.0 * np.pi * np.outer(k, k) / r
    st_scale = r ** -0.5
    wr = np.cos(ang_r) * st_scale
    wi = -np.sin(ang_r) * st_scale
    wbig = np.block([[wr, -wi], [wi, wr]])
    ang_t = 2.0 * np.pi * np.outer(k, k) / (r * r)
    tr = np.cos(ang_t)
    ti = -np.sin(ang_t)
    f = lambda a: jnp.asarray(a, dtype=F32)
    return f(cs).astype(BF16), f(wbig).astype(BF16), f(wr), f(wi), f(tr), f(ti)


FS_SB = 8
FS_KB = 8


def _fft_stage1_kernel(u_ref, cs_ref, wbig_ref, y_ref):
    cs = cs_ref[...]
    wbig = wbig_ref[...]
    fw = FOURIER_WIDTH
    for j in range(FS_SB):
        u = u_ref[:, j * fw:(j + 1) * fw]
        g = jnp.dot(u, cs, preferred_element_type=F32)
        gst = jnp.concatenate([g[:, :fw], g[:, fw:]], axis=0).astype(BF16)
        y = jnp.dot(wbig, gst, preferred_element_type=F32)
        y_ref[:, j * 2 * fw:j * 2 * fw + fw] = y[:FFT_R].astype(BF16)
        y_ref[:, j * 2 * fw + fw:(j + 1) * 2 * fw] = y[FFT_R:].astype(BF16)


def _fft_stage2_kernel(y_ref, wr_ref, wi_ref, tr_ref, ti_ref, wf_ref, o_ref):
    kb = pl.program_id(1)
    wr = wr_ref[...]
    wi = wi_ref[...]
    wf = wf_ref[...]
    fw = FOURIER_WIDTH
    for j in range(FS_KB):
        k1 = kb * FS_KB + j
        tr = tr_ref[pl.ds(k1, 1), :]
        ti = ti_ref[pl.ds(k1, 1), :]
        mr = wr * tr - wi * ti
        mi = wr * ti + wi * tr
        m2 = jnp.concatenate([mr, -mi], axis=1).astype(BF16)
        yb = y_ref[j * FFT_R:(j + 1) * FFT_R, :]
        yst = jnp.concatenate([yb[:, :fw], yb[:, fw:]], axis=0)
        z = jnp.dot(m2, yst, preferred_element_type=F32)
        yf = jnp.dot(z.astype(BF16), wf, preferred_element_type=F32)
        o_ref[:, j * fw:(j + 1) * fw] = yf.astype(BF16)


def _fourier(u, w_fourier):
    cs, wbig, wr, wi, tr, ti = _dft_tables()
    r, fw = FFT_R, FOURIER_WIDTH
    u2 = u.reshape(N_SEQ * r, r * fw)
    y = pl.pallas_call(
        _fft_stage1_kernel,
        grid=(N_SEQ, r // FS_SB),
        in_specs=[
            pl.BlockSpec((r, FS_SB * fw), lambda s, j: (s, j)),
            pl.BlockSpec((fw, 2 * fw), lambda s, j: (0, 0)),
            pl.BlockSpec((2 * r, 2 * r), lambda s, j: (0, 0)),
        ],
        out_specs=pl.BlockSpec((r, FS_SB * 2 * fw), lambda s, j: (s, j)),
        out_shape=jax.ShapeDtypeStruct((N_SEQ * r, r * 2 * fw), BF16),
        compiler_params=_cparams("parallel", "parallel"),
    )(u2, cs, wbig)
    y2 = y.reshape(N_SEQ * r * r, 2 * fw)
    full = lambda s, j: (0, 0)
    yf = pl.pallas_call(
        _fft_stage2_kernel,
        grid=(N_SEQ, r // FS_KB),
        in_specs=[
            pl.BlockSpec((FS_KB * r, 2 * fw), lambda s, j: (s * (r // FS_KB) + j, 0)),
            pl.BlockSpec((r, r), full),
            pl.BlockSpec((r, r), full),
            pl.BlockSpec((r, r), full),
            pl.BlockSpec((r, r), full),
            pl.BlockSpec((fw, fw), full),
        ],
        out_specs=pl.BlockSpec((r, FS_KB * fw), lambda s, j: (s, j)),
        out_shape=jax.ShapeDtypeStruct((N_SEQ * r, r * fw), BF16),
        compiler_params=_cparams("parallel", "parallel"),
    )(y2, wr, wi, tr, ti, w_fourier)
    return yf.reshape(N_TOK, fw)


LO_HEADS = (0, 1, 2, 6, 7, 8)
HI_HEADS = (3, 4, 5, 9, 10, 11)
Q_ORDER = tuple(h for pair in zip(LO_HEADS, HI_HEADS) for h in pair)
assert all(h // GQA_GROUP % 2 == 0 for h in LO_HEADS) and all(h // GQA_GROUP % 2 == 1 for h in HI_HEADS)
KV4 = 2 * KV_WIDTH
LOG2E = 1.4426950408889634


def _attn_kernel(sink_ref, q_ref, kp_ref, kc_ref, kn_ref, vp_ref, vc_ref, vn_ref, o_ref,
                 kext, vext):
    i = pl.program_id(1)
    n_i = pl.num_programs(1)
    kext[0:QB, :] = kp_ref[...]
    kext[QB:QB + TQ, :] = kc_ref[...]
    kext[QB + TQ:, :] = kn_ref[...]
    vext[0:QB, :] = vp_ref[...]
    vext[QB:QB + TQ, :] = vc_ref[...]
    vext[QB + TQ:, :] = vn_ref[...]

    rows = GQA_GROUP * QB
    a = lax.broadcasted_iota(jnp.int32, (rows, QB), 0) % QB
    j = lax.broadcasted_iota(jnp.int32, (rows, QB), 1)
    band_prev = jnp.where(j >= a, 0.0, NEG)
    band_next = jnp.where(j <= a, 0.0, NEG)
    row_id = lax.broadcasted_iota(jnp.int32, (rows, 1), 0)
    low_half = lax.broadcasted_iota(jnp.int32, (rows, LANES), 1) < HEAD_DIM

    def softmax_half(s, sinks, bias_prev, bias_next):
        s_p = s[:, :QB] + bias_prev
        s_c = s[:, QB:2 * QB]
        s_n = s[:, 2 * QB:] + bias_next
        sink = jnp.where(row_id < QB, sinks[0], jnp.where(row_id < 2 * QB, sinks[1], sinks[2])) * LOG2E
        m = jnp.max(jnp.maximum(jnp.maximum(s_p, s_c), s_n), axis=-1, keepdims=True)
        m = jnp.maximum(m, sink)
        p_p = jnp.exp2(s_p - m)
        p_c = jnp.exp2(s_c - m)
        p_n = jnp.exp2(s_n - m)
        denom = jnp.sum(p_p + p_c + p_n, axis=-1, keepdims=True) + jnp.exp2(sink - m)
        return [p_p.astype(BF16), p_c.astype(BF16), p_n.astype(BF16)], 1.0 / denom

    for sb in range(TQ // QB):
        blk = i * (TQ // QB) + sb
        bias_prev = band_prev + jnp.where(blk == 0, NEG, 0.0)
        bias_next = band_next + jnp.where(blk == n_i * (TQ // QB) - 1, NEG, 0.0)
        for pr in range(N_KV_HEADS // 2):
            groups = [GQA_GROUP * pr + m for m in range(GQA_GROUP)]
            q3 = jnp.concatenate([q_ref[sb * QB:(sb + 1) * QB, g * LANES:(g + 1) * LANES] for g in groups],
                                 axis=0)
            kp = kext[sb * QB:sb * QB + 3 * QB, pr * 2 * LANES:(pr + 1) * 2 * LANES]
            vp = vext[sb * QB:sb * QB + 3 * QB, pr * 2 * LANES:(pr + 1) * 2 * LANES]
            kcat = jnp.concatenate([kp[:, :LANES], kp[:, LANES:]], axis=0)
            vcat = jnp.concatenate([vp[:, :LANES], vp[:, LANES:]], axis=0)
            s = lax.dot_general(q3, kcat, (((1,), (1,)), ((), ())), preferred_element_type=F32)
            p_lo, r_lo = softmax_half(s[:, :3 * QB], [sink_ref[LO_HEADS[g]] for g in groups],
                                      bias_prev, bias_next)
            p_hi, r_hi = softmax_half(s[:, 3 * QB:], [sink_ref[HI_HEADS[g]] for g in groups],
                                      bias_prev, bias_next)
            p = jnp.concatenate(p_lo + p_hi, axis=1)
            o = jnp.dot(p, vcat, preferred_element_type=F32) * jnp.where(low_half, r_lo, r_hi)
            for m, g in enumerate(groups):
                o_ref[sb * QB:(sb + 1) * QB, g * LANES:(g + 1) * LANES] = o[m * QB:(m + 1) * QB].astype(BF16)


def _attention(sink, q, k, v):
    nqb = SEQ // QB
    per = TQ // QB
    cur = lambda s, i: (s * (SEQ // TQ) + i, 0)
    prev = lambda s, i: (s * nqb + jnp.maximum(i * per - 1, 0), 0)
    nxt = lambda s, i: (s * nqb + jnp.minimum(i * per + per, nqb - 1), 0)
    return pl.pallas_call(
        _attn_kernel,
        grid=(N_SEQ, SEQ // TQ),
        in_specs=[
            pl.BlockSpec(memory_space=pltpu.SMEM),
            pl.BlockSpec((TQ, ATTN_WIDTH), cur),
            pl.BlockSpec((QB, KV4), prev),
            pl.BlockSpec((TQ, KV4), cur),
            pl.BlockSpec((QB, KV4), nxt),
            pl.BlockSpec((QB, KV4), prev),
            pl.BlockSpec((TQ, KV4), cur),
            pl.BlockSpec((QB, KV4), nxt),
        ],
        out_specs=pl.BlockSpec((TQ, ATTN_WIDTH), cur),
        out_shape=jax.ShapeDtypeStruct((N_TOK, ATTN_WIDTH), BF16),
        scratch_shapes=[pltpu.VMEM((TQ + 2 * QB, KV4), BF16),
                        pltpu.VMEM((TQ + 2 * QB, KV4), BF16)],
        compiler_params=_cparams("parallel", "parallel"),
    )(sink, q, k, k, k, v, v, v)


def _outproj_kernel(xp_ref, xs_ref, yf_ref, ya_ref, wo_ref, g_ref, wr_ref, x1_ref, xn_ref, aff_ref):
    mix = jnp.dot(yf_ref[...], wo_ref[:FOURIER_WIDTH, :], preferred_element_type=F32)
    mix = mix + jnp.dot(ya_ref[...], wo_ref[FOURIER_WIDTH:, :], preferred_element_type=F32)
    x1 = _x_tile(xp_ref, xs_ref) + mix
    x1_ref[...] = x1
    ms = jnp.mean(x1 * x1, axis=-1, keepdims=True)
    xn = (x1 * lax.rsqrt(ms + EPS) * g_ref[...]).astype(BF16)
    xn_ref[...] = xn
    logits = lax.dot_general(wr_ref[...], xn, (((1,), (1,)), ((), ())), preferred_element_type=F32)
    mx = jnp.max(logits, axis=0, keepdims=True)
    ex = jnp.exp(logits - mx)
    aff_ref[...] = ex / jnp.sum(ex, axis=0, keepdims=True)


def _outproj(xp, xs, yf, ya, w_out, g, w_router_t):
    return pl.pallas_call(
        _outproj_kernel,
        grid=(N_TOK // TM,),
        in_specs=_x_specs() + [
            pl.BlockSpec((TM, FOURIER_WIDTH), lambda i: (i, 0)),
            pl.BlockSpec((TM, ATTN_WIDTH), lambda i: (i, 0)),
            pl.BlockSpec((D_MODEL, D_MODEL), lambda i: (0, 0)),
            pl.BlockSpec((1, D_MODEL), lambda i: (0, 0)),
            pl.BlockSpec((N_EXPERTS, D_MODEL), lambda i: (0, 0)),
        ],
        out_specs=[
            pl.BlockSpec((TM, D_MODEL), lambda i: (i, 0)),
            pl.BlockSpec((TM, D_MODEL), lambda i: (i, 0)),
            pl.BlockSpec((N_EXPERTS, TM), lambda i: (0, i)),
        ],
        out_shape=[
            jax.ShapeDtypeStruct((N_TOK, D_MODEL), F32),
            jax.ShapeDtypeStruct((N_TOK, D_MODEL), BF16),
            jax.ShapeDtypeStruct((N_EXPERTS, N_TOK), F32),
        ],
        compiler_params=_cparams("parallel"),
    )(xp, xs, yf, ya, w_out, g, w_router_t)


RB = 512
N_RB = N_TOK // RB
GROUP_TOKENS = (SEQ, 2 * SEQ)
GROUP0_BLOCKS = GROUP_TOKENS[0] // RB
CHUNK = 8
TILE_ROWS = 256
CHUNKS_PER_TILE = TILE_ROWS // CHUNK
SLOT_TILES = 14
SLOT_ROWS = SLOT_TILES * FF_TM
assert SLOT_ROWS >= EC_CAPACITY_FACTOR * N_TOK // N_EXPERTS + N_RB * (CHUNK - 1)
SLOT_CHUNKS = SLOT_ROWS // CHUNK
TRASH_CHUNK = N_EXPERTS * SLOT_CHUNKS
MAX_CHUNKS = (N_EXPERTS * (RB + CHUNK - 1)) // CHUNK
CHUNK_SHAPE = (2 * CHUNK, D_MODEL // 2)


def _pack_chunks(rows_f32):
    n = rows_f32.shape[0] // CHUNK
    r3 = rows_f32.reshape(n, CHUNK, D_MODEL)
    half = D_MODEL // 2
    return jnp.concatenate([r3[:, :, :half], r3[:, :, half:]], axis=1).astype(BF16)


def _unpack_chunks(chunks_bf16):
    n = chunks_bf16.shape[0]
    c3 = chunks_bf16.astype(F32)
    left = c3[:, :CHUNK, :].reshape(n * CHUNK, D_MODEL // 2)
    right = c3[:, CHUNK:, :].reshape(n * CHUNK, D_MODEL // 2)
    return jnp.concatenate([left, right], axis=1).astype(BF16)
LIST_UNROLL = 4
LIST_LEN = ((MAX_CHUNKS + LIST_UNROLL + CHUNKS_PER_TILE - 1) // CHUNKS_PER_TILE) * CHUNKS_PER_TILE


def _thresh_kernel(aff_ref, thr_ref, need_ref, *, cap):
    aff = aff_ref[...]

    def body(i, cur):
        cand = cur | jnp.left_shift(jnp.int32(1), 30 - i)
        cnt = jnp.sum(jnp.where(aff >= pltpu.bitcast(cand, F32), 1.0, 0.0), axis=1, keepdims=True)
        return jnp.where(cnt >= cap, cand, cur)

    thr = pltpu.bitcast(lax.fori_loop(0, 31, body, jnp.zeros((N_EXPERTS, 1), jnp.int32)), F32)
    n_gt = jnp.sum(jnp.where(aff > thr, 1.0, 0.0), axis=1, keepdims=True)
    thr_ref[...] = jnp.broadcast_to(thr, thr_ref.shape)
    need_ref[...] = jnp.broadcast_to(cap - n_gt, need_ref.shape)


def _thresholds(aff_group):
    n = aff_group.shape[1]
    cap = EC_CAPACITY_FACTOR * n // N_EXPERTS
    return pl.pallas_call(
        functools.partial(_thresh_kernel, cap=float(cap)),
        out_shape=[jax.ShapeDtypeStruct((N_EXPERTS, LANES), F32),
                   jax.ShapeDtypeStruct((N_EXPERTS, LANES), F32)],
        compiler_params=pltpu.CompilerParams(vmem_limit_bytes=VMEM_LIMIT),
    )(aff_group)


def _select_kernel(aff_ref, thr_ref, need_ref, utri_ref, selw_ref, rank_ref, m_ref, eq_seen):
    rb = pl.program_id(0)

    @pl.when((rb == 0) | (rb == GROUP0_BLOCKS))
    def _():
        eq_seen[...] = jnp.zeros_like(eq_seen)

    aff = aff_ref[...]
    thr = thr_ref[:, 0:1]
    need = need_ref[:, 0:1]
    utri = utri_ref[...]
    eq = jnp.where(aff == thr, 1.0, 0.0)
    eq_before = jnp.dot(eq.astype(BF16), utri, preferred_element_type=F32) + eq_seen[...]
    sel = (aff > thr) | ((aff == thr) & (eq_before < need))
    eq_seen[...] += jnp.sum(eq, axis=1, keepdims=True)
    self = jnp.where(sel, 1.0, 0.0)
    rank_ref[...] = jnp.dot(self.astype(BF16), utri, preferred_element_type=F32)
    selw_ref[...] = jnp.where(sel, aff, 0.0)
    m_ref[0] = jnp.broadcast_to(jnp.sum(self, axis=1, keepdims=True), (N_EXPERTS, LANES))


def _select(aff_t, thr, need):
    utri = jnp.asarray(np.triu(np.ones((RB, RB), np.float32), 1), dtype=BF16)
    grp = lambda rb: (jnp.where(rb >= GROUP0_BLOCKS, 1, 0), 0)
    return pl.pallas_call(
        _select_kernel,
        grid=(N_RB,),
        in_specs=[
            pl.BlockSpec((N_EXPERTS, RB), lambda rb: (0, rb)),
            pl.BlockSpec((N_EXPERTS, LANES), grp),
            pl.BlockSpec((N_EXPERTS, LANES), grp),
            pl.BlockSpec((RB, RB), lambda rb: (0, 0)),
        ],
        out_specs=[
            pl.BlockSpec((N_EXPERTS, RB), lambda rb: (0, rb)),
            pl.BlockSpec((N_EXPERTS, RB), lambda rb: (0, rb)),
            pl.BlockSpec((1, N_EXPERTS, LANES), lambda rb: (rb, 0, 0)),
        ],
        out_shape=[
            jax.ShapeDtypeStruct((N_EXPERTS, N_TOK), F32),
            jax.ShapeDtypeStruct((N_EXPERTS, N_TOK), F32),
            jax.ShapeDtypeStruct((N_RB, N_EXPERTS, LANES), F32),
        ],
        scratch_shapes=[pltpu.VMEM((N_EXPERTS, 1), F32)],
        compiler_params=_cparams("arbitrary"),
    )(aff_t, thr, need, utri)


def _fill_chunk_list(list_ref, which, rb, seg_start_ref, seg_rows_ref):
    cnt = jnp.int32(0)
    for e in range(N_EXPERTS):
        base = e * SLOT_CHUNKS + seg_start_ref[rb, e] // CHUNK
        n = seg_rows_ref[rb, e] // CHUNK

        def body(k, carry, base=base, cnt=cnt):
            for u in range(LIST_UNROLL):
                list_ref[which, cnt + LIST_UNROLL * k + u] = base + LIST_UNROLL * k + u
            return carry

        lax.fori_loop(0, (n + LIST_UNROLL - 1) // LIST_UNROLL, body, 0)
        cnt = cnt + n
    return cnt


WIN_ROWS = 128


def _window_rows():
    return lax.broadcasted_iota(jnp.int32, (WIN_ROWS, RB), 0).astype(F32)


def _mark_windows(scr, win_rows, tgt_row, value, r0, lo, n):
    first = jnp.maximum(lo, r0)
    last = jnp.minimum(lo + n, r0 + TILE_ROWS)
    for w in range(TILE_ROWS // WIN_ROWS):
        start = first + w * WIN_ROWS

        @pl.when(start < last)
        def _(start=start):
            win = pl.ds(pl.multiple_of(start - r0, CHUNK), WIN_ROWS)
            scr[win, :] = jnp.where(tgt_row == win_rows + start.astype(F32), value, scr[win, :])


def _gather_kernel(seg_start_ref, seg_rows_ref, seg_off_ref, tot_ref,
                   x_ref, selw_ref, rank_ref, xe_hbm,
                   p_scr, obuf, zbuf, dst_list, pending, sem, zsem):
    rb = pl.program_id(0)
    n_rows = tot_ref[rb]
    n_chunks = n_rows // CHUNK
    n_tiles = (n_rows + TILE_ROWS - 1) // TILE_ROWS

    @pl.when(rb == 0)
    def _():
        pending[0] = 0
        pending[1] = 0
        p_scr[...] = jnp.zeros_like(p_scr)

    _fill_chunk_list(dst_list, 0, rb, seg_start_ref, seg_rows_ref)

    def tile_copy(slot):
        return pltpu.make_async_copy(obuf.at[slot], xe_hbm.at[pl.ds(0, CHUNKS_PER_TILE)], sem.at[slot])

    e_iota = lax.broadcasted_iota(jnp.int32, (N_EXPERTS, 1), 0)
    off = jnp.zeros((N_EXPERTS, 1), F32)
    for e in range(N_EXPERTS):
        off = jnp.where(e_iota == e, seg_off_ref[rb, e].astype(F32), off)
    tgt = jnp.where(selw_ref[...] > 0.0, rank_ref[...] + off, -1.0)
    x = x_ref[...]
    win_rows = _window_rows()

    def tile_body(t, carry):
        slot = t % 2
        r0 = t * TILE_ROWS

        @pl.when(pending[slot] == 1)
        def _():
            tile_copy(slot).wait()

        p_scr[0:TILE_ROWS, :] = jnp.zeros((TILE_ROWS, RB), F32)
        for e in range(N_EXPERTS):
            _mark_windows(p_scr, win_rows, tgt[e:e + 1, :], 1.0, r0,
                          seg_off_ref[rb, e], seg_rows_ref[rb, e])

        obuf[slot] = _pack_chunks(jnp.dot(p_scr[0:TILE_ROWS, :].astype(BF16), x,
                                          preferred_element_type=F32))

        for i in range(CHUNKS_PER_TILE):
            g = t * CHUNKS_PER_TILE + i
            dst = jnp.where(g < n_chunks, dst_list[0, jnp.minimum(g, n_chunks - 1)],
                            TRASH_CHUNK + slot * CHUNKS_PER_TILE + i)
            pltpu.make_async_copy(obuf.at[slot, i], xe_hbm.at[dst], sem.at[slot]).start()
        pending[slot] = 1
        return carry

    lax.fori_loop(0, n_tiles, tile_body, 0)

    @pl.when(rb == N_RB - 1)
    def _():
        for slot in range(2):
            @pl.when(pending[slot] == 1)
            def _(slot=slot):
                tile_copy(slot).wait()
        zbuf[...] = jnp.zeros_like(zbuf)

        def zero_fill(first_chunk, count):
            def zcopy(k):
                return pltpu.make_async_copy(zbuf, xe_hbm.at[first_chunk + k], zsem)

            def zstart(k, c):
                zcopy(k).start()
                return c

            def zwait(k, c):
                zcopy(k).wait()
                return c

            lax.fori_loop(0, count, zstart, 0)
            lax.fori_loop(0, count, zwait, 0)

        for e in range(N_EXPERTS):
            used_chunks = (seg_start_ref[rb, e] + seg_rows_ref[rb, e]) // CHUNK
            zero_fill(e * SLOT_CHUNKS + used_chunks, SLOT_CHUNKS - used_chunks)
        zero_fill(TRASH_CHUNK, 2 * CHUNKS_PER_TILE)


def _route_gather(seg_start, seg_rows, seg_off, tot, xn, selw_t, rank_t):
    gs = pltpu.PrefetchScalarGridSpec(
        num_scalar_prefetch=4,
        grid=(N_RB,),
        in_specs=[
            pl.BlockSpec((RB, D_MODEL), lambda rb, *_: (rb, 0)),
            pl.BlockSpec((N_EXPERTS, RB), lambda rb, *_: (0, rb)),
            pl.BlockSpec((N_EXPERTS, RB), lambda rb, *_: (0, rb)),
        ],
        out_specs=pl.BlockSpec(memory_space=pl.ANY),
        scratch_shapes=[
            pltpu.VMEM((TILE_ROWS + WIN_ROWS, RB), F32),
            pltpu.VMEM((2, CHUNKS_PER_TILE) + CHUNK_SHAPE, BF16),
            pltpu.VMEM(CHUNK_SHAPE, BF16),
            pltpu.SMEM((1, LIST_LEN), jnp.int32),
            pltpu.SMEM((2,), jnp.int32),
            pltpu.SemaphoreType.DMA((2,)),
            pltpu.SemaphoreType.DMA,
        ],
    )
    return pl.pallas_call(
        _gather_kernel,
        grid_spec=gs,
        out_shape=jax.ShapeDtypeStruct((TRASH_CHUNK + 2 * CHUNKS_PER_TILE,) + CHUNK_SHAPE, BF16),
        compiler_params=_cparams("arbitrary"),
    )(seg_start, seg_rows, seg_off, tot, xn, selw_t, rank_t)


W_PIECES_PER_MATRIX = 4
W_PIECES = 3 * W_PIECES_PER_MATRIX
assert SLOT_TILES > W_PIECES
WG_PIECE_ROWS = D_MODEL // W_PIECES_PER_MATRIX
WD_PIECE_ROWS = EXPERT_D_FF // W_PIECES_PER_MATRIX
FFN_VMEM_LIMIT = 56 * 1024 * 1024


def _ffn_kernel(used_ref, x_ref, wg_hbm, wu_hbm, wd_hbm, o_ref,
                wg_bf, wu_bf, wd_bf, stage_up, stage_dn, sem):
    e = pl.program_id(0)
    i = pl.program_id(1)
    slot = e % 2

    matrices = ((wg_hbm, stage_up, wg_bf, WG_PIECE_ROWS),
                (wu_hbm, stage_up, wu_bf, WG_PIECE_ROWS),
                (wd_hbm, stage_dn, wd_bf, WD_PIECE_ROWS))

    def piece(expert, k, buf, finish):
        def of_kind(kind):
            w_hbm, stage, w_bf, rows = matrices[kind]
            r0 = pl.multiple_of((k - kind * W_PIECES_PER_MATRIX) * rows, rows)
            cp = pltpu.make_async_copy(w_hbm.at[expert, pl.ds(r0, rows), :], stage.at[buf], sem.at[buf])
            if finish:
                cp.wait()
                w_bf[expert % 2, pl.ds(r0, rows), :] = stage[buf].astype(BF16)
            else:
                cp.start()

        if isinstance(k, int):
            of_kind(k // W_PIECES_PER_MATRIX)
        else:
            for kind in range(len(matrices)):
                pl.when(k // W_PIECES_PER_MATRIX == kind)(functools.partial(of_kind, kind))

    @pl.when((e == 0) & (i == 0))
    def _():
        for k in range(W_PIECES):
            piece(0, k, k % 2, finish=False)
            piece(0, k, k % 2, finish=True)

    @pl.when(e + 1 < N_EXPERTS)
    def _():
        @pl.when((i >= 1) & (i <= W_PIECES))
        def _():
            piece(e + 1, i - 1, (i - 1) % 2, finish=True)

        @pl.when(i < W_PIECES)
        def _():
            piece(e + 1, i, i % 2, finish=False)

    @pl.when(i * FF_TM < used_ref[e])
    def _():
        x = _unpack_chunks(x_ref[...])
        acc = jnp.zeros((FF_TM, D_MODEL), F32)
        for c in range(EXPERT_D_FF // FF_FC):
            gt = jnp.dot(x, wg_bf[slot, :, c * FF_FC:(c + 1) * FF_FC], preferred_element_type=F32)
            up = jnp.dot(x, wu_bf[slot, :, c * FF_FC:(c + 1) * FF_FC], preferred_element_type=F32)
            h = (gt * (1.0 / (1.0 + jnp.exp(-gt))) * up).astype(BF16)
            acc = acc + jnp.dot(h, wd_bf[slot, c * FF_FC:(c + 1) * FF_FC, :], preferred_element_type=F32)
        o_ref[...] = _pack_chunks(acc)

    @pl.when(i * FF_TM >= used_ref[e])
    def _():
        o_ref[...] = jnp.zeros_like(o_ref)


def _ffn(used, xe, w_gate, w_up, w_down):
    def in_rows(e, i, used_ref):
        last = jnp.maximum((used_ref[e] + FF_TM - 1) // FF_TM - 1, 0)
        return (e * SLOT_TILES + jnp.minimum(i, last), 0, 0)

    def out_rows(e, i, used_ref):
        return (e * SLOT_TILES + i, 0, 0)

    tile_chunks = (FF_TM // CHUNK,) + CHUNK_SHAPE
    gs = pltpu.PrefetchScalarGridSpec(
        num_scalar_prefetch=1,
        grid=(N_EXPERTS, SLOT_TILES),
        in_specs=[
            pl.BlockSpec(tile_chunks, in_rows),
            pl.BlockSpec(memory_space=pl.ANY),
            pl.BlockSpec(memory_space=pl.ANY),
            pl.BlockSpec(memory_space=pl.ANY),
        ],
        out_specs=pl.BlockSpec(tile_chunks, out_rows),
        scratch_shapes=[
            pltpu.VMEM((2, D_MODEL, EXPERT_D_FF), BF16),
            pltpu.VMEM((2, D_MODEL, EXPERT_D_FF), BF16),
            pltpu.VMEM((2, EXPERT_D_FF, D_MODEL), BF16),
            pltpu.VMEM((2, WG_PIECE_ROWS, EXPERT_D_FF), F32),
            pltpu.VMEM((2, WD_PIECE_ROWS, D_MODEL), F32),
            pltpu.SemaphoreType.DMA((2,)),
        ],
    )
    return pl.pallas_call(
        _ffn_kernel,
        grid_spec=gs,
        out_shape=jax.ShapeDtypeStruct((N_EXPERTS * SLOT_CHUNKS,) + CHUNK_SHAPE, BF16),
        compiler_params=pltpu.CompilerParams(dimension_semantics=("arbitrary", "arbitrary"),
                                             vmem_limit_bytes=FFN_VMEM_LIMIT),
    )(used, xe, w_gate, w_up, w_down)


def _combine_kernel(seg_start_ref, seg_rows_ref, seg_off_ref, tot_ref,
                    x1_ref, selw_ref, rank_ref, g_ref, ye_hbm, op_ref, os_ref,
                    w_scr, ybuf, acc, src_list, first_slot, sem):
    rb = pl.program_id(0)
    cur = rb % 2
    nxt_rb = jnp.minimum(rb + 1, N_RB - 1)

    def tiles_of(b):
        return (tot_ref[b] + TILE_ROWS - 1) // TILE_ROWS

    n_tiles = tiles_of(rb)
    n_tiles_next = jnp.where(rb + 1 < N_RB, tiles_of(nxt_rb), 0)

    def fetch(b, which, t, slot):
        n_chunks = tot_ref[b] // CHUNK
        for i in range(CHUNKS_PER_TILE):
            g = t * CHUNKS_PER_TILE + i
            src = jnp.where(g < n_chunks, src_list[which, jnp.minimum(g, n_chunks - 1)], 0)
            pltpu.make_async_copy(ye_hbm.at[src], ybuf.at[slot, i], sem.at[slot]).start()

    def tile_wait(slot):
        pltpu.make_async_copy(ye_hbm.at[pl.ds(0, CHUNKS_PER_TILE)], ybuf.at[slot], sem.at[slot]).wait()

    @pl.when(rb == 0)
    def _():
        w_scr[...] = jnp.zeros_like(w_scr)
        _fill_chunk_list(src_list, 0, 0, seg_start_ref, seg_rows_ref)
        first_slot[0] = 0

        @pl.when(n_tiles > 0)
        def _():
            fetch(0, 0, 0, 0)

    @pl.when(rb + 1 < N_RB)
    def _():
        _fill_chunk_list(src_list, 1 - cur, nxt_rb, seg_start_ref, seg_rows_ref)

    slot0 = first_slot[0]
    first_slot[0] = (slot0 + n_tiles) % 2

    @pl.when((n_tiles == 0) & (n_tiles_next > 0))
    def _():
        fetch(nxt_rb, 1 - cur, 0, slot0)

    e_iota = lax.broadcasted_iota(jnp.int32, (N_EXPERTS, 1), 0)
    off = jnp.zeros((N_EXPERTS, 1), F32)
    for e in range(N_EXPERTS):
        off = jnp.where(e_iota == e, seg_off_ref[rb, e].astype(F32), off)
    selw = selw_ref[...]
    win_rows = _window_rows()
    tgt = jnp.where(selw > 0.0, rank_ref[...] + off, -1.0)
    acc[...] = x1_ref[...]

    def tile_body(t, carry):
        slot = (slot0 + t) % 2
        r0 = t * TILE_ROWS

        @pl.when(t + 1 < n_tiles)
        def _():
            fetch(rb, cur, t + 1, 1 - slot)

        @pl.when((t + 1 == n_tiles) & (n_tiles_next > 0))
        def _():
            fetch(nxt_rb, 1 - cur, 0, 1 - slot)

        tile_wait(slot)
        w_scr[0:TILE_ROWS, :] = jnp.zeros((TILE_ROWS, RB), F32)
        for e in range(N_EXPERTS):
            _mark_windows(w_scr, win_rows, tgt[e:e + 1, :], selw[e:e + 1, :], r0,
                          seg_off_ref[rb, e], seg_rows_ref[rb, e])

        acc[...] += lax.dot_general(w_scr[0:TILE_ROWS, :].astype(BF16), _unpack_chunks(ybuf[slot]),
                                    (((0,), (0,)), ((), ())), preferred_element_type=F32)
        return carry

    lax.fori_loop(0, n_tiles, tile_body, 0)
    x2 = acc[...]
    ms = jnp.mean(x2 * x2, axis=-1, keepdims=True)
    y = x2 * lax.rsqrt(ms + EPS) * g_ref[...]

    @pl.when(rb < GROUP0_BLOCKS)
    def _():
        op_ref[...] = y

    @pl.when(rb >= GROUP0_BLOCKS)
    def _():
        os_ref[...] = y


def _combine(seg_start, seg_rows, seg_off, tot, x1, selw_t, rank_t, g, ye):
    gs = pltpu.PrefetchScalarGridSpec(
        num_scalar_prefetch=4,
        grid=(N_RB,),
        in_specs=[
            pl.BlockSpec((RB, D_MODEL), lambda rb, *_: (rb, 0)),
            pl.BlockSpec((N_EXPERTS, RB), lambda rb, *_: (0, rb)),
            pl.BlockSpec((N_EXPERTS, RB), lambda rb, *_: (0, rb)),
            pl.BlockSpec((1, D_MODEL), lambda rb, *_: (0, 0)),
            pl.BlockSpec(memory_space=pl.ANY),
        ],
        out_specs=[
            pl.BlockSpec((RB, D_MODEL), lambda rb, *_: (jnp.minimum(rb, GROUP0_BLOCKS - 1), 0)),
            pl.BlockSpec((RB, D_MODEL), lambda rb, *_: (jnp.maximum(rb - GROUP0_BLOCKS, 0), 0)),
        ],
        scratch_shapes=[
            pltpu.VMEM((TILE_ROWS + WIN_ROWS, RB), F32),
            pltpu.VMEM((2, CHUNKS_PER_TILE) + CHUNK_SHAPE, BF16),
            pltpu.VMEM((RB, D_MODEL), F32),
            pltpu.SMEM((2, LIST_LEN), jnp.int32),
            pltpu.SMEM((1,), jnp.int32),
            pltpu.SemaphoreType.DMA((2,)),
        ],
    )
    return pl.pallas_call(
        _combine_kernel,
        grid_spec=gs,
        out_shape=[jax.ShapeDtypeStruct((GROUP_TOKENS[0], D_MODEL), F32),
                   jax.ShapeDtypeStruct((GROUP_TOKENS[1], D_MODEL), F32)],
        compiler_params=_cparams("arbitrary"),
    )(seg_start, seg_rows, seg_off, tot, x1, selw_t, rank_t, g, ye)


def kernel(x_prompt, x_sample, norm_mix_g, w_in, w_fourier, sink, w_out, norm_ffn_g, w_router,
           w_gate, w_up, w_down, norm_final_g):
    xp = x_prompt.reshape(-1, D_MODEL)
    xs = x_sample.reshape(-1, D_MODEL)
    cos_t, sin_t = _rope_tables()

    mix_cols = np.concatenate([
        np.arange(FOURIER_WIDTH),
        FOURIER_WIDTH + (np.asarray(Q_ORDER)[:, None] * HEAD_DIM + np.arange(HEAD_DIM)[None, :]).reshape(-1)])
    in_cols = np.concatenate([mix_cols, np.arange(MIX_WIDTH, IN_WIDTH)])
    w_in_r = w_in[0][:, in_cols].astype(BF16)
    w_out_r = w_out[0][mix_cols, :].astype(BF16)

    u, q, k, v = _inproj(xp, xs, norm_mix_g[0][None, :], w_in_r, cos_t, sin_t)
    yf = _fourier(u, w_fourier[0].astype(BF16))
    ya = _attention(sink[0], q, k, v)
    x1, xn, aff_t = _outproj(xp, xs, yf, ya, w_out_r, norm_ffn_g[0][None, :],
                             w_router[0].T.astype(BF16))

    thr0, need0 = _thresholds(aff_t[:, :GROUP_TOKENS[0]])
    thr1, need1 = _thresholds(aff_t[:, GROUP_TOKENS[0]:])
    selw_t, rank_t, m = _select(aff_t, jnp.concatenate([thr0, thr1]), jnp.concatenate([need0, need1]))

    seg_rows = ((m[:, :, 0].astype(jnp.int32) + CHUNK - 1) // CHUNK) * CHUNK
    seg_start = jnp.cumsum(seg_rows, axis=0) - seg_rows
    seg_off = jnp.cumsum(seg_rows, axis=1) - seg_rows
    tot = jnp.sum(seg_rows, axis=1)
    used = jnp.sum(seg_rows, axis=0)

    xe = _route_gather(seg_start, seg_rows, seg_off, tot, xn, selw_t, rank_t)
    ye = _ffn(used, xe, w_gate[0], w_up[0], w_down[0])
    yp, ys = _combine(seg_start, seg_rows, seg_off, tot, x1, selw_t, rank_t, norm_final_g[None, :], ye)
    return (yp.reshape(x_prompt.shape), ys.reshape(x_sample.shape))
```

```python
import functools

import numpy as np
import jax
import jax.numpy as jnp
from jax import lax
from jax.experimental import pallas as pl
from jax.experimental.pallas import tpu as pltpu

F32 = jnp.float32
BF16 = jnp.bfloat16

D_MODEL = 1024
SEQ = 16384
N_SEQ = 3
N_TOK = N_SEQ * SEQ
HEAD_DIM = 64
N_HEADS = 12
N_KV_HEADS = 4
GQA_GROUP = N_HEADS // N_KV_HEADS
ATTN_WIDTH = N_HEADS * HEAD_DIM
KV_WIDTH = N_KV_HEADS * HEAD_DIM
FOURIER_WIDTH = 256
FOURIER_GROUP_DIM = 64
MIX_WIDTH = FOURIER_WIDTH + ATTN_WIDTH
IN_WIDTH = MIX_WIDTH + 2 * KV_WIDTH
WINDOW = 128
ROPE_THETA = 10000.0
N_EXPERTS = 16
EC_CAPACITY_FACTOR = 2
EXPERT_D_FF = 2048
EPS = 1e-6
NEG = -1e30

LANES = 128
FFT_R = 128
VMEM_LIMIT = 48 * 1024 * 1024

TM = 512
TQ = 512
QB = 128
FF_TM = 512
FF_FC = 512


def _cparams(*sem):
    return pltpu.CompilerParams(dimension_semantics=sem, vmem_limit_bytes=VMEM_LIMIT)


def _x_specs():
    n_p = SEQ // TM
    return [pl.BlockSpec((TM, D_MODEL), lambda i, *_: (jnp.minimum(i, n_p - 1), 0)),
            pl.BlockSpec((TM, D_MODEL), lambda i, *_: (jnp.maximum(i - n_p, 0), 0))]


def _x_tile(xp_ref, xs_ref):
    return jnp.where(pl.program_id(0) < SEQ // TM, xp_ref[...], xs_ref[...])


def _inproj_kernel(xp_ref, xs_ref, g_ref, w_ref, cos_ref, sin_ref, u_ref, q_ref, k_ref, v_ref):
    x = _x_tile(xp_ref, xs_ref)
    ms = jnp.mean(x * x, axis=-1, keepdims=True)
    hn = (x * lax.rsqrt(ms + EPS) * g_ref[...]).astype(BF16)
    proj = jnp.dot(hn, w_ref[...], preferred_element_type=F32)
    u_ref[...] = proj[:, :FOURIER_WIDTH].astype(BF16)
    cos = cos_ref[...]
    sin = sin_ref[...]
    lane = lax.broadcasted_iota(jnp.int32, cos.shape, 1)
    first_half = (lane % HEAD_DIM) < (HEAD_DIM // 2)

    def rope(t):
        partner = jnp.where(first_half,
                            pltpu.roll(t, LANES - HEAD_DIM // 2, 1),
                            pltpu.roll(t, HEAD_DIM // 2, 1))
        return t * cos + partner * sin

    scale = HEAD_DIM ** -0.5 * LOG2E
    for c in range(ATTN_WIDTH // LANES):
        lo = FOURIER_WIDTH + c * LANES
        q_ref[:, c * LANES:(c + 1) * LANES] = (rope(proj[:, lo:lo + LANES]) * scale).astype(BF16)
    low_half = lane < HEAD_DIM
    for c in range(KV_WIDTH // LANES):
        lo = FOURIER_WIDTH + ATTN_WIDTH + c * LANES
        kc = rope(proj[:, lo:lo + LANES])
        vc = proj[:, lo + KV_WIDTH:lo + KV_WIDTH + LANES]
        for half, keep in enumerate((low_half, ~low_half)):
            g = 2 * c + half
            k_ref[:, g * LANES:(g + 1) * LANES] = jnp.where(keep, kc, 0.0).astype(BF16)
            v_ref[:, g * LANES:(g + 1) * LANES] = jnp.where(keep, vc, 0.0).astype(BF16)


def _inproj(xp, xs, g, w_in, cos_t, sin_t):
    n_pos_blocks = SEQ // TM
    return pl.pallas_call(
        _inproj_kernel,
        grid=(N_TOK // TM,),
        in_specs=_x_specs() + [
            pl.BlockSpec((1, D_MODEL), lambda i: (0, 0)),
            pl.BlockSpec((D_MODEL, IN_WIDTH), lambda i: (0, 0)),
            pl.BlockSpec((TM, LANES), lambda i: (i % n_pos_blocks, 0)),
            pl.BlockSpec((TM, LANES), lambda i: (i % n_pos_blocks, 0)),
        ],
        out_specs=[
            pl.BlockSpec((TM, FOURIER_WIDTH), lambda i: (i, 0)),
            pl.BlockSpec((TM, ATTN_WIDTH), lambda i: (i, 0)),
            pl.BlockSpec((TM, KV4), lambda i: (i, 0)),
            pl.BlockSpec((TM, KV4), lambda i: (i, 0)),
        ],
        out_shape=[
            jax.ShapeDtypeStruct((N_TOK, FOURIER_WIDTH), BF16),
            jax.ShapeDtypeStruct((N_TOK, ATTN_WIDTH), BF16),
            jax.ShapeDtypeStruct((N_TOK, KV4), BF16),
            jax.ShapeDtypeStruct((N_TOK, KV4), BF16),
        ],
        compiler_params=_cparams("parallel"),
    )(xp, xs, g, w_in, cos_t, sin_t)


def _rope_tables():
    half = HEAD_DIM // 2
    inv_freq = 1.0 / (ROPE_THETA ** (jnp.arange(half, dtype=F32) / half))
    ang = jnp.arange(SEQ, dtype=F32)[:, None] * inv_freq[None, :]
    cos = jnp.cos(ang)
    sin = jnp.sin(ang)
    reps = LANES // HEAD_DIM
    cos_t = jnp.tile(jnp.concatenate([cos, cos], axis=-1), (1, reps))
    sin_t = jnp.tile(jnp.concatenate([-sin, sin], axis=-1), (1, reps))
    return cos_t, sin_t


def _dft_tables():
    r = FFT_R
    c = np.arange(FOURIER_GROUP_DIM)
    ang_c = 2.0 * np.pi * np.outer(c, c) / FOURIER_GROUP_DIM
    n_groups = FOURIER_WIDTH // FOURIER_GROUP_DIM
    eye = np.eye(n_groups)
    ch_scale = FOURIER_GROUP_DIM ** -0.5
    c_blk = np.kron(eye, np.cos(ang_c)) * ch_scale
    s_blk = np.kron(eye, np.sin(ang_c)) * ch_scale
    cs = np.concatenate([c_blk, -s_blk], axis=1)
    k = np.arange(r)
    ang_r = 2.0 * np.pi * np.outer(k, k) / r
    st_scale = r ** -0.5
    wr = np.cos(ang_r) * st_scale
    wi = -np.sin(ang_r) * st_scale
    wbig = np.block([[wr, -wi], [wi, wr]])
    ang_t = 2.0 * np.pi * np.outer(k, k) / (r * r)
    tr = np.cos(ang_t)
    ti = -np.sin(ang_t)
    f = lambda a: jnp.asarray(a, dtype=F32)
    return f(cs).astype(BF16), f(wbig).astype(BF16), f(wr), f(wi), f(tr), f(ti)


FS_SB = 8
FS_KB = 8


def _fft_stage1_kernel(u_ref, cs_ref, wbig_ref, y_ref):
    cs = cs_ref[...]
    wbig = wbig_ref[...]
    fw = FOURIER_WIDTH
    for j in range(FS_SB):
        u = u_ref[:, j * fw:(j + 1) * fw]
        g = jnp.dot(u, cs, preferred_element_type=F32)
        gst = jnp.concatenate([g[:, :fw], g[:, fw:]], axis=0).astype(BF16)
        y = jnp.dot(wbig, gst, preferred_element_type=F32)
        y_ref[:, j * 2 * fw:j * 2 * fw + fw] = y[:FFT_R].astype(BF16)
        y_ref[:, j * 2 * fw + fw:(j + 1) * 2 * fw] = y[FFT_R:].astype(BF16)


def _fft_stage2_kernel(y_ref, wr_ref, wi_ref, tr_ref, ti_ref, wf_ref, o_ref):
    kb = pl.program_id(1)
    wr = wr_ref[...]
    wi = wi_ref[...]
    wf = wf_ref[...]
    fw = FOURIER_WIDTH
    for j in range(FS_KB):
        k1 = kb * FS_KB + j
        tr = tr_ref[pl.ds(k1, 1), :]
        ti = ti_ref[pl.ds(k1, 1), :]
        mr = wr * tr - wi * ti
        mi = wr * ti + wi * tr
        m2 = jnp.concatenate([mr, -mi], axis=1).astype(BF16)
        yb = y_ref[j * FFT_R:(j + 1) * FFT_R, :]
        yst = jnp.concatenate([yb[:, :fw], yb[:, fw:]], axis=0)
        z = jnp.dot(m2, yst, preferred_element_type=F32)
        yf = jnp.dot(z.astype(BF16), wf, preferred_element_type=F32)
        o_ref[:, j * fw:(j + 1) * fw] = yf.astype(BF16)


def _fourier(u, w_fourier):
    cs, wbig, wr, wi, tr, ti = _dft_tables()
    r, fw = FFT_R, FOURIER_WIDTH
    u2 = u.reshape(N_SEQ * r, r * fw)
    y = pl.pallas_call(
        _fft_stage1_kernel,
        grid=(N_SEQ, r // FS_SB),
        in_specs=[
            pl.BlockSpec((r, FS_SB * fw), lambda s, j: (s, j)),
            pl.BlockSpec((fw, 2 * fw), lambda s, j: (0, 0)),
            pl.BlockSpec((2 * r, 2 * r), lambda s, j: (0, 0)),
        ],
        out_specs=pl.BlockSpec((r, FS_SB * 2 * fw), lambda s, j: (s, j)),
        out_shape=jax.ShapeDtypeStruct((N_SEQ * r, r * 2 * fw), BF16),
        compiler_params=_cparams("parallel", "parallel"),
    )(u2, cs, wbig)
    y2 = y.reshape(N_SEQ * r * r, 2 * fw)
    full = lambda s, j: (0, 0)
    yf = pl.pallas_call(
        _fft_stage2_kernel,
        grid=(N_SEQ, r // FS_KB),
        in_specs=[
            pl.BlockSpec((FS_KB * r, 2 * fw), lambda s, j: (s * (r // FS_KB) + j, 0)),
            pl.BlockSpec((r, r), full),
            pl.BlockSpec((r, r), full),
            pl.BlockSpec((r, r), full),
            pl.BlockSpec((r, r), full),
            pl.BlockSpec((fw, fw), full),
        ],
        out_specs=pl.BlockSpec((r, FS_KB * fw), lambda s, j: (s, j)),
        out_shape=jax.ShapeDtypeStruct((N_SEQ * r, r * fw), BF16),
        compiler_params=_cparams("parallel", "parallel"),
    )(y2, wr, wi, tr, ti, w_fourier)
    return yf.reshape(N_TOK, fw)


LO_HEADS = (0, 1, 2, 6, 7, 8)
HI_HEADS = (3, 4, 5, 9, 10, 11)
Q_ORDER = tuple(h for pair in zip(LO_HEADS, HI_HEADS) for h in pair)
assert all(h // GQA_GROUP % 2 == 0 for h in LO_HEADS) and all(h // GQA_GROUP % 2 == 1 for h in HI_HEADS)
KV4 = 2 * KV_WIDTH
LOG2E = 1.4426950408889634


def _attn_kernel(sink_ref, q_ref, kp_ref, kc_ref, kn_ref, vp_ref, vc_ref, vn_ref, o_ref,
                 kext, vext):
    i = pl.program_id(1)
    n_i = pl.num_programs(1)
    kext[0:QB, :] = kp_ref[...]
    kext[QB:QB + TQ, :] = kc_ref[...]
    kext[QB + TQ:, :] = kn_ref[...]
    vext[0:QB, :] = vp_ref[...]
    vext[QB:QB + TQ, :] = vc_ref[...]
    vext[QB + TQ:, :] = vn_ref[...]

    rows = GQA_GROUP * QB
    a = lax.broadcasted_iota(jnp.int32, (rows, QB), 0) % QB
    j = lax.broadcasted_iota(jnp.int32, (rows, QB), 1)
    band_prev = jnp.where(j >= a, 0.0, NEG)
    band_next = jnp.where(j <= a, 0.0, NEG)
    row_id = lax.broadcasted_iota(jnp.int32, (rows, 1), 0)
    low_half = lax.broadcasted_iota(jnp.int32, (rows, LANES), 1) < HEAD_DIM

    def softmax_half(s, sinks, bias_prev, bias_next):
        s_p = s[:, :QB] + bias_prev
        s_c = s[:, QB:2 * QB]
        s_n = s[:, 2 * QB:] + bias_next
        sink = jnp.where(row_id < QB, sinks[0], jnp.where(row_id < 2 * QB, sinks[1], sinks[2])) * LOG2E
        m = jnp.max(jnp.maximum(jnp.maximum(s_p, s_c), s_n), axis=-1, keepdims=True)
        m = jnp.maximum(m, sink)
        p_p = jnp.exp2(s_p - m)
        p_c = jnp.exp2(s_c - m)
        p_n = jnp.exp2(s_n - m)
        denom = jnp.sum(p_p + p_c + p_n, axis=-1, keepdims=True) + jnp.exp2(sink - m)
        return [p_p.astype(BF16), p_c.astype(BF16), p_n.astype(BF16)], 1.0 / denom

    for sb in range(TQ // QB):
        blk = i * (TQ // QB) + sb
        bias_prev = band_prev + jnp.where(blk == 0, NEG, 0.0)
        bias_next = band_next + jnp.where(blk == n_i * (TQ // QB) - 1, NEG, 0.0)
        for pr in range(N_KV_HEADS // 2):
            groups = [GQA_GROUP * pr + m for m in range(GQA_GROUP)]
            q3 = jnp.concatenate([q_ref[sb * QB:(sb + 1) * QB, g * LANES:(g + 1) * LANES] for g in groups],
                                 axis=0)
            kp = kext[sb * QB:sb * QB + 3 * QB, pr * 2 * LANES:(pr + 1) * 2 * LANES]
            vp = vext[sb * QB:sb * QB + 3 * QB, pr * 2 * LANES:(pr + 1) * 2 * LANES]
            kcat = jnp.concatenate([kp[:, :LANES], kp[:, LANES:]], axis=0)
            vcat = jnp.concatenate([vp[:, :LANES], vp[:, LANES:]], axis=0)
            s = lax.dot_general(q3, kcat, (((1,), (1,)), ((), ())), preferred_element_type=F32)
            p_lo, r_lo = softmax_half(s[:, :3 * QB], [sink_ref[LO_HEADS[g]] for g in groups],
                                      bias_prev, bias_next)
            p_hi, r_hi = softmax_half(s[:, 3 * QB:], [sink_ref[HI_HEADS[g]] for g in groups],
                                      bias_prev, bias_next)
            p = jnp.concatenate(p_lo + p_hi, axis=1)
            o = jnp.dot(p, vcat, preferred_element_type=F32) * jnp.where(low_half, r_lo, r_hi)
            for m, g in enumerate(groups):
                o_ref[sb * QB:(sb + 1) * QB, g * LANES:(g + 1) * LANES] = o[m * QB:(m + 1) * QB].astype(BF16)


def _attention(sink, q, k, v):
    nqb = SEQ // QB
    per = TQ // QB
    cur = lambda s, i: (s * (SEQ // TQ) + i, 0)
    prev = lambda s, i: (s * nqb + jnp.maximum(i * per - 1, 0), 0)
    nxt = lambda s, i: (s * nqb + jnp.minimum(i * per + per, nqb - 1), 0)
    return pl.pallas_call(
        _attn_kernel,
        grid=(N_SEQ, SEQ // TQ),
        in_specs=[
            pl.BlockSpec(memory_space=pltpu.SMEM),
            pl.BlockSpec((TQ, ATTN_WIDTH), cur),
            pl.BlockSpec((QB, KV4), prev),
            pl.BlockSpec((TQ, KV4), cur),
            pl.BlockSpec((QB, KV4), nxt),
            pl.BlockSpec((QB, KV4), prev),
            pl.BlockSpec((TQ, KV4), cur),
            pl.BlockSpec((QB, KV4), nxt),
        ],
        out_specs=pl.BlockSpec((TQ, ATTN_WIDTH), cur),
        out_shape=jax.ShapeDtypeStruct((N_TOK, ATTN_WIDTH), BF16),
        scratch_shapes=[pltpu.VMEM((TQ + 2 * QB, KV4), BF16),
                        pltpu.VMEM((TQ + 2 * QB, KV4), BF16)],
        compiler_params=_cparams("parallel", "parallel"),
    )(sink, q, k, k, k, v, v, v)


def _outproj_kernel(xp_ref, xs_ref, yf_ref, ya_ref, wo_ref, g_ref, wr_ref, x1_ref, xn_ref, aff_ref):
    mix = jnp.dot(yf_ref[...], wo_ref[:FOURIER_WIDTH, :], preferred_element_type=F32)
    mix = mix + jnp.dot(ya_ref[...], wo_ref[FOURIER_WIDTH:, :], preferred_element_type=F32)
    x1 = _x_tile(xp_ref, xs_ref) + mix
    x1_ref[...] = x1
    ms = jnp.mean(x1 * x1, axis=-1, keepdims=True)
    xn = (x1 * lax.rsqrt(ms + EPS) * g_ref[...]).astype(BF16)
    xn_ref[...] = xn
    logits = lax.dot_general(wr_ref[...], xn, (((1,), (1,)), ((), ())), preferred_element_type=F32)
    mx = jnp.max(logits, axis=0, keepdims=True)
    ex = jnp.exp(logits - mx)
    aff_ref[...] = ex / jnp.sum(ex, axis=0, keepdims=True)


def _outproj(xp, xs, yf, ya, w_out, g, w_router_t):
    return pl.pallas_call(
        _outproj_kernel,
        grid=(N_TOK // TM,),
        in_specs=_x_specs() + [
            pl.BlockSpec((TM, FOURIER_WIDTH), lambda i: (i, 0)),
            pl.BlockSpec((TM, ATTN_WIDTH), lambda i: (i, 0)),
            pl.BlockSpec((D_MODEL, D_MODEL), lambda i: (0, 0)),
            pl.BlockSpec((1, D_MODEL), lambda i: (0, 0)),
            pl.BlockSpec((N_EXPERTS, D_MODEL), lambda i: (0, 0)),
        ],
        out_specs=[
            pl.BlockSpec((TM, D_MODEL), lambda i: (i, 0)),
            pl.BlockSpec((TM, D_MODEL), lambda i: (i, 0)),
            pl.BlockSpec((N_EXPERTS, TM), lambda i: (0, i)),
        ],
        out_shape=[
            jax.ShapeDtypeStruct((N_TOK, D_MODEL), F32),
            jax.ShapeDtypeStruct((N_TOK, D_MODEL), BF16),
            jax.ShapeDtypeStruct((N_EXPERTS, N_TOK), F32),
        ],
        compiler_params=_cparams("parallel"),
    )(xp, xs, yf, ya, w_out, g, w_router_t)


RB = 512
N_RB = N_TOK // RB
GROUP_TOKENS = (SEQ, 2 * SEQ)
GROUP0_BLOCKS = GROUP_TOKENS[0] // RB
CHUNK = 8
TILE_ROWS = 256
CHUNKS_PER_TILE = TILE_ROWS // CHUNK
SLOT_TILES = 14
SLOT_ROWS = SLOT_TILES * FF_TM
assert SLOT_ROWS >= EC_CAPACITY_FACTOR * N_TOK // N_EXPERTS + N_RB * (CHUNK - 1)
SLOT_CHUNKS = SLOT_ROWS // CHUNK
TILE_SLOTS = 4
LOOKAHEAD = TILE_SLOTS - 1
TRASH_CHUNK = N_EXPERTS * SLOT_CHUNKS
TRASH_CHUNKS = TILE_SLOTS * CHUNKS_PER_TILE
MAX_CHUNKS = (N_EXPERTS * (RB + CHUNK - 1)) // CHUNK
CHUNK_SHAPE = (2 * CHUNK, D_MODEL // 2)


def _pack_chunks(rows_f32):
    n = rows_f32.shape[0] // CHUNK
    r3 = rows_f32.reshape(n, CHUNK, D_MODEL)
    half = D_MODEL // 2
    return jnp.concatenate([r3[:, :, :half], r3[:, :, half:]], axis=1).astype(BF16)


def _unpack_chunks(chunks_bf16):
    n = chunks_bf16.shape[0]
    c3 = chunks_bf16.astype(F32)
    left = c3[:, :CHUNK, :].reshape(n * CHUNK, D_MODEL // 2)
    right = c3[:, CHUNK:, :].reshape(n * CHUNK, D_MODEL // 2)
    return jnp.concatenate([left, right], axis=1).astype(BF16)
LIST_UNROLL = 4
LIST_LEN = ((MAX_CHUNKS + LIST_UNROLL + CHUNKS_PER_TILE - 1) // CHUNKS_PER_TILE) * CHUNKS_PER_TILE


def _thresh_kernel(aff_ref, thr_ref, need_ref, *, cap):
    aff = aff_ref[...]

    def body(i, cur):
        cand = cur | jnp.left_shift(jnp.int32(1), 30 - i)
        cnt = jnp.sum(jnp.where(aff >= pltpu.bitcast(cand, F32), 1.0, 0.0), axis=1, keepdims=True)
        return jnp.where(cnt >= cap, cand, cur)

    thr = pltpu.bitcast(lax.fori_loop(0, 31, body, jnp.zeros((N_EXPERTS, 1), jnp.int32)), F32)
    n_gt = jnp.sum(jnp.where(aff > thr, 1.0, 0.0), axis=1, keepdims=True)
    thr_ref[...] = jnp.broadcast_to(thr, thr_ref.shape)
    need_ref[...] = jnp.broadcast_to(cap - n_gt, need_ref.shape)


def _thresholds(aff_group):
    n = aff_group.shape[1]
    cap = EC_CAPACITY_FACTOR * n // N_EXPERTS
    return pl.pallas_call(
        functools.partial(_thresh_kernel, cap=float(cap)),
        out_shape=[jax.ShapeDtypeStruct((N_EXPERTS, LANES), F32),
                   jax.ShapeDtypeStruct((N_EXPERTS, LANES), F32)],
        compiler_params=pltpu.CompilerParams(vmem_limit_bytes=VMEM_LIMIT),
    )(aff_group)


def _select_kernel(aff_ref, thr_ref, need_ref, utri_ref, selw_ref, rank_ref, m_ref, eq_seen):
    rb = pl.program_id(0)

    @pl.when((rb == 0) | (rb == GROUP0_BLOCKS))
    def _():
        eq_seen[...] = jnp.zeros_like(eq_seen)

    aff = aff_ref[...]
    thr = thr_ref[:, 0:1]
    need = need_ref[:, 0:1]
    utri = utri_ref[...]
    eq = jnp.where(aff == thr, 1.0, 0.0)
    eq_before = jnp.dot(eq.astype(BF16), utri, preferred_element_type=F32) + eq_seen[...]
    sel = (aff > thr) | ((aff == thr) & (eq_before < need))
    eq_seen[...] += jnp.sum(eq, axis=1, keepdims=True)
    self = jnp.where(sel, 1.0, 0.0)
    rank_ref[...] = jnp.dot(self.astype(BF16), utri, preferred_element_type=F32)
    selw_ref[...] = jnp.where(sel, aff, 0.0)
    m_ref[0] = jnp.broadcast_to(jnp.sum(self, axis=1, keepdims=True), (N_EXPERTS, LANES))


def _select(aff_t, thr, need):
    utri = jnp.asarray(np.triu(np.ones((RB, RB), np.float32), 1), dtype=BF16)
    grp = lambda rb: (jnp.where(rb >= GROUP0_BLOCKS, 1, 0), 0)
    return pl.pallas_call(
        _select_kernel,
        grid=(N_RB,),
        in_specs=[
            pl.BlockSpec((N_EXPERTS, RB), lambda rb: (0, rb)),
            pl.BlockSpec((N_EXPERTS, LANES), grp),
            pl.BlockSpec((N_EXPERTS, LANES), grp),
            pl.BlockSpec((RB, RB), lambda rb: (0, 0)),
        ],
        out_specs=[
            pl.BlockSpec((N_EXPERTS, RB), lambda rb: (0, rb)),
            pl.BlockSpec((N_EXPERTS, RB), lambda rb: (0, rb)),
            pl.BlockSpec((1, N_EXPERTS, LANES), lambda rb: (rb, 0, 0)),
        ],
        out_shape=[
            jax.ShapeDtypeStruct((N_EXPERTS, N_TOK), F32),
            jax.ShapeDtypeStruct((N_EXPERTS, N_TOK), F32),
            jax.ShapeDtypeStruct((N_RB, N_EXPERTS, LANES), F32),
        ],
        scratch_shapes=[pltpu.VMEM((N_EXPERTS, 1), F32)],
        compiler_params=_cparams("arbitrary"),
    )(aff_t, thr, need, utri)


def _fill_chunk_list(list_ref, which, rb, seg_start_ref, seg_rows_ref):
    cnt = jnp.int32(0)
    for e in range(N_EXPERTS):
        base = e * SLOT_CHUNKS + seg_start_ref[rb, e] // CHUNK
        n = seg_rows_ref[rb, e] // CHUNK

        def body(k, carry, base=base, cnt=cnt):
            for u in range(LIST_UNROLL):
                list_ref[which, cnt + LIST_UNROLL * k + u] = base + LIST_UNROLL * k + u
            return carry

        lax.fori_loop(0, (n + LIST_UNROLL - 1) // LIST_UNROLL, body, 0)
        cnt = cnt + n
    return cnt


WIN_ROWS = 128


def _window_rows():
    return lax.broadcasted_iota(jnp.int32, (WIN_ROWS, RB), 0).astype(F32)


def _mark_windows(scr, win_rows, tgt_row, value, r0, lo, n):
    first = jnp.maximum(lo, r0)
    last = jnp.minimum(lo + n, r0 + TILE_ROWS)
    for w in range(TILE_ROWS // WIN_ROWS):
        start = first + w * WIN_ROWS

        @pl.when(start < last)
        def _(start=start):
            win = pl.ds(pl.multiple_of(start - r0, CHUNK), WIN_ROWS)
            scr[win, :] = jnp.where(tgt_row == win_rows + start.astype(F32), value, scr[win, :])


def _gather_kernel(seg_start_ref, seg_rows_ref, seg_off_ref, tot_ref,
                   x_ref, selw_ref, rank_ref, xe_hbm,
                   p_scr, obuf, zbuf, dst_list, pending, sem, zsem):
    rb = pl.program_id(0)
    n_rows = tot_ref[rb]
    n_chunks = n_rows // CHUNK
    n_tiles = (n_rows + TILE_ROWS - 1) // TILE_ROWS

    @pl.when(rb == 0)
    def _():
        for slot in range(TILE_SLOTS):
            pending[slot] = 0
        p_scr[...] = jnp.zeros_like(p_scr)

    _fill_chunk_list(dst_list, 0, rb, seg_start_ref, seg_rows_ref)

    def tile_copy(slot):
        return pltpu.make_async_copy(obuf.at[slot], xe_hbm.at[pl.ds(0, CHUNKS_PER_TILE)], sem.at[slot])

    e_iota = lax.broadcasted_iota(jnp.int32, (N_EXPERTS, 1), 0)
    off = jnp.zeros((N_EXPERTS, 1), F32)
    for e in range(N_EXPERTS):
        off = jnp.where(e_iota == e, seg_off_ref[rb, e].astype(F32), off)
    tgt = jnp.where(selw_ref[...] > 0.0, rank_ref[...] + off, -1.0)
    x = x_ref[...]
    win_rows = _window_rows()

    def tile_body(t, carry):
        slot = t % TILE_SLOTS
        r0 = t * TILE_ROWS

        @pl.when(pending[slot] == 1)
        def _():
            tile_copy(slot).wait()

        p_scr[0:TILE_ROWS, :] = jnp.zeros((TILE_ROWS, RB), F32)
        for e in range(N_EXPERTS):
            _mark_windows(p_scr, win_rows, tgt[e:e + 1, :], 1.0, r0,
                          seg_off_ref[rb, e], seg_rows_ref[rb, e])

        obuf[slot] = _pack_chunks(jnp.dot(p_scr[0:TILE_ROWS, :].astype(BF16), x,
                                          preferred_element_type=F32))

        for i in range(CHUNKS_PER_TILE):
            g = t * CHUNKS_PER_TILE + i
            dst = jnp.where(g < n_chunks, dst_list[0, jnp.minimum(g, n_chunks - 1)],
                            TRASH_CHUNK + slot * CHUNKS_PER_TILE + i)
            pltpu.make_async_copy(obuf.at[slot, i], xe_hbm.at[dst], sem.at[slot]).start()
        pending[slot] = 1
        return carry

    lax.fori_loop(0, n_tiles, tile_body, 0)

    @pl.when(rb == N_RB - 1)
    def _():
        for slot in range(TILE_SLOTS):
            @pl.when(pending[slot] == 1)
            def _(slot=slot):
                tile_copy(slot).wait()
        zbuf[...] = jnp.zeros_like(zbuf)

        def zero_fill(first_chunk, count):
            def zcopy(k):
                return pltpu.make_async_copy(zbuf, xe_hbm.at[first_chunk + k], zsem)

            def zstart(k, c):
                zcopy(k).start()
                return c

            def zwait(k, c):
                zcopy(k).wait()
                return c

            lax.fori_loop(0, count, zstart, 0)
            lax.fori_loop(0, count, zwait, 0)

        for e in range(N_EXPERTS):
            used_chunks = (seg_start_ref[rb, e] + seg_rows_ref[rb, e]) // CHUNK
            zero_fill(e * SLOT_CHUNKS + used_chunks, SLOT_CHUNKS - used_chunks)
        zero_fill(TRASH_CHUNK, TRASH_CHUNKS)


def _route_gather(seg_start, seg_rows, seg_off, tot, xn, selw_t, rank_t):
    gs = pltpu.PrefetchScalarGridSpec(
        num_scalar_prefetch=4,
        grid=(N_RB,),
        in_specs=[
            pl.BlockSpec((RB, D_MODEL), lambda rb, *_: (rb, 0)),
            pl.BlockSpec((N_EXPERTS, RB), lambda rb, *_: (0, rb)),
            pl.BlockSpec((N_EXPERTS, RB), lambda rb, *_: (0, rb)),
        ],
        out_specs=pl.BlockSpec(memory_space=pl.ANY),
        scratch_shapes=[
            pltpu.VMEM((TILE_ROWS + WIN_ROWS, RB), F32),
            pltpu.VMEM((TILE_SLOTS, CHUNKS_PER_TILE) + CHUNK_SHAPE, BF16),
            pltpu.VMEM(CHUNK_SHAPE, BF16),
            pltpu.SMEM((1, LIST_LEN), jnp.int32),
            pltpu.SMEM((TILE_SLOTS,), jnp.int32),
            pltpu.SemaphoreType.DMA((TILE_SLOTS,)),
            pltpu.SemaphoreType.DMA,
        ],
    )
    return pl.pallas_call(
        _gather_kernel,
        grid_spec=gs,
        out_shape=jax.ShapeDtypeStruct((TRASH_CHUNK + TRASH_CHUNKS,) + CHUNK_SHAPE, BF16),
        compiler_params=_cparams("arbitrary"),
    )(seg_start, seg_rows, seg_off, tot, xn, selw_t, rank_t)


W_PIECES_PER_MATRIX = 4
W_PIECES = 3 * W_PIECES_PER_MATRIX
assert SLOT_TILES > W_PIECES
WG_PIECE_ROWS = D_MODEL // W_PIECES_PER_MATRIX
WD_PIECE_ROWS = EXPERT_D_FF // W_PIECES_PER_MATRIX
FFN_VMEM_LIMIT = 56 * 1024 * 1024


def _ffn_kernel(used_ref, x_ref, wg_hbm, wu_hbm, wd_hbm, o_ref,
                wg_bf, wu_bf, wd_bf, stage_up, stage_dn, sem):
    e = pl.program_id(0)
    i = pl.program_id(1)
    slot = e % 2

    matrices = ((wg_hbm, stage_up, wg_bf, WG_PIECE_ROWS),
                (wu_hbm, stage_up, wu_bf, WG_PIECE_ROWS),
                (wd_hbm, stage_dn, wd_bf, WD_PIECE_ROWS))

    def piece(expert, k, buf, finish):
        def of_kind(kind):
            w_hbm, stage, w_bf, rows = matrices[kind]
            r0 = pl.multiple_of((k - kind * W_PIECES_PER_MATRIX) * rows, rows)
            cp = pltpu.make_async_copy(w_hbm.at[expert, pl.ds(r0, rows), :], stage.at[buf], sem.at[buf])
            if finish:
                cp.wait()
                w_bf[expert % 2, pl.ds(r0, rows), :] = stage[buf].astype(BF16)
            else:
                cp.start()

        if isinstance(k, int):
            of_kind(k // W_PIECES_PER_MATRIX)
        else:
            for kind in range(len(matrices)):
                pl.when(k // W_PIECES_PER_MATRIX == kind)(functools.partial(of_kind, kind))

    @pl.when((e == 0) & (i == 0))
    def _():
        for k in range(W_PIECES):
            piece(0, k, k % 2, finish=False)
            piece(0, k, k % 2, finish=True)

    @pl.when(e + 1 < N_EXPERTS)
    def _():
        @pl.when((i >= 1) & (i <= W_PIECES))
        def _():
            piece(e + 1, i - 1, (i - 1) % 2, finish=True)

        @pl.when(i < W_PIECES)
        def _():
            piece(e + 1, i, i % 2, finish=False)

    @pl.when(i * FF_TM < used_ref[e])
    def _():
        x = _unpack_chunks(x_ref[...])
        acc = jnp.zeros((FF_TM, D_MODEL), F32)
        for c in range(EXPERT_D_FF // FF_FC):
            gt = jnp.dot(x, wg_bf[slot, :, c * FF_FC:(c + 1) * FF_FC], preferred_element_type=F32)
            up = jnp.dot(x, wu_bf[slot, :, c * FF_FC:(c + 1) * FF_FC], preferred_element_type=F32)
            h = (gt * (1.0 / (1.0 + jnp.exp(-gt))) * up).astype(BF16)
            acc = acc + jnp.dot(h, wd_bf[slot, c * FF_FC:(c + 1) * FF_FC, :], preferred_element_type=F32)
        o_ref[...] = _pack_chunks(acc)

    @pl.when(i * FF_TM >= used_ref[e])
    def _():
        o_ref[...] = jnp.zeros_like(o_ref)


def _ffn(used, xe, w_gate, w_up, w_down):
    def in_rows(e, i, used_ref):
        last = jnp.maximum((used_ref[e] + FF_TM - 1) // FF_TM - 1, 0)
        return (e * SLOT_TILES + jnp.minimum(i, last), 0, 0)

    def out_rows(e, i, used_ref):
        return (e * SLOT_TILES + i, 0, 0)

    tile_chunks = (FF_TM // CHUNK,) + CHUNK_SHAPE
    gs = pltpu.PrefetchScalarGridSpec(
        num_scalar_prefetch=1,
        grid=(N_EXPERTS, SLOT_TILES),
        in_specs=[
            pl.BlockSpec(tile_chunks, in_rows),
            pl.BlockSpec(memory_space=pl.ANY),
            pl.BlockSpec(memory_space=pl.ANY),
            pl.BlockSpec(memory_space=pl.ANY),
        ],
        out_specs=pl.BlockSpec(tile_chunks, out_rows),
        scratch_shapes=[
            pltpu.VMEM((2, D_MODEL, EXPERT_D_FF), BF16),
            pltpu.VMEM((2, D_MODEL, EXPERT_D_FF), BF16),
            pltpu.VMEM((2, EXPERT_D_FF, D_MODEL), BF16),
            pltpu.VMEM((2, WG_PIECE_ROWS, EXPERT_D_FF), F32),
            pltpu.VMEM((2, WD_PIECE_ROWS, D_MODEL), F32),
            pltpu.SemaphoreType.DMA((2,)),
        ],
    )
    return pl.pallas_call(
        _ffn_kernel,
        grid_spec=gs,
        out_shape=jax.ShapeDtypeStruct((N_EXPERTS * SLOT_CHUNKS,) + CHUNK_SHAPE, BF16),
        compiler_params=pltpu.CompilerParams(dimension_semantics=("arbitrary", "arbitrary"),
                                             vmem_limit_bytes=FFN_VMEM_LIMIT),
    )(used, xe, w_gate, w_up, w_down)


def _combine_kernel(seg_start_ref, seg_rows_ref, seg_off_ref, tot_ref,
                    x1_ref, selw_ref, rank_ref, g_ref, ye_hbm, op_ref, os_ref,
                    w_scr, ybuf, acc, src_list, state, sem):
    rb = pl.program_id(0)
    cur = rb % 2
    nxt_rb = jnp.minimum(rb + 1, N_RB - 1)

    def tiles_of(b):
        return (tot_ref[b] + TILE_ROWS - 1) // TILE_ROWS

    n_tiles = tiles_of(rb)
    n_tiles_next = jnp.where(rb + 1 < N_RB, tiles_of(nxt_rb), 0)

    def fetch(b, which, t, slot):
        n_chunks = tot_ref[b] // CHUNK
        for i in range(CHUNKS_PER_TILE):
            g = t * CHUNKS_PER_TILE + i
            src = jnp.where(g < n_chunks, src_list[which, jnp.minimum(g, n_chunks - 1)], 0)
            pltpu.make_async_copy(ye_hbm.at[src], ybuf.at[slot, i], sem.at[slot]).start()

    def tile_wait(slot):
        pltpu.make_async_copy(ye_hbm.at[pl.ds(0, CHUNKS_PER_TILE)], ybuf.at[slot], sem.at[slot]).wait()

    @pl.when(rb == 0)
    def _():
        w_scr[...] = jnp.zeros_like(w_scr)
        _fill_chunk_list(src_list, 0, 0, seg_start_ref, seg_rows_ref)
        state[0] = 0
        state[1] = 0

    @pl.when(rb + 1 < N_RB)
    def _():
        _fill_chunk_list(src_list, 1 - cur, nxt_rb, seg_start_ref, seg_rows_ref)

    slot0 = state[0]
    started = state[1]
    for j in range(LOOKAHEAD):
        @pl.when((j >= started) & (j < n_tiles))
        def _(j=j):
            fetch(rb, cur, j, (slot0 + j) % TILE_SLOTS)

    e_iota = lax.broadcasted_iota(jnp.int32, (N_EXPERTS, 1), 0)
    off = jnp.zeros((N_EXPERTS, 1), F32)
    for e in range(N_EXPERTS):
        off = jnp.where(e_iota == e, seg_off_ref[rb, e].astype(F32), off)
    selw = selw_ref[...]
    win_rows = _window_rows()
    tgt = jnp.where(selw > 0.0, rank_ref[...] + off, -1.0)
    acc[...] = x1_ref[...]

    def tile_body(t, next_started):
        slot = (slot0 + t) % TILE_SLOTS
        r0 = t * TILE_ROWS
        ahead = t + LOOKAHEAD

        @pl.when(ahead < n_tiles)
        def _():
            fetch(rb, cur, ahead, (slot0 + ahead) % TILE_SLOTS)

        start_next = (ahead >= n_tiles) & (next_started < jnp.minimum(LOOKAHEAD, n_tiles_next))

        @pl.when(start_next)
        def _():
            fetch(nxt_rb, 1 - cur, next_started, (slot0 + n_tiles + next_started) % TILE_SLOTS)

        tile_wait(slot)
        w_scr[0:TILE_ROWS, :] = jnp.zeros((TILE_ROWS, RB), F32)
        for e in range(N_EXPERTS):
            _mark_windows(w_scr, win_rows, tgt[e:e + 1, :], selw[e:e + 1, :], r0,
                          seg_off_ref[rb, e], seg_rows_ref[rb, e])

        acc[...] += lax.dot_general(w_scr[0:TILE_ROWS, :].astype(BF16), _unpack_chunks(ybuf[slot]),
                                    (((0,), (0,)), ((), ())), preferred_element_type=F32)
        return next_started + start_next.astype(jnp.int32)

    state[1] = lax.fori_loop(0, n_tiles, tile_body, jnp.int32(0))
    state[0] = (slot0 + n_tiles) % TILE_SLOTS
    x2 = acc[...]
    ms = jnp.mean(x2 * x2, axis=-1, keepdims=True)
    y = x2 * lax.rsqrt(ms + EPS) * g_ref[...]

    @pl.when(rb < GROUP0_BLOCKS)
    def _():
        op_ref[...] = y

    @pl.when(rb >= GROUP0_BLOCKS)
    def _():
        os_ref[...] = y


def _combine(seg_start, seg_rows, seg_off, tot, x1, selw_t, rank_t, g, ye):
    gs = pltpu.PrefetchScalarGridSpec(
        num_scalar_prefetch=4,
        grid=(N_RB,),
        in_specs=[
            pl.BlockSpec((RB, D_MODEL), lambda rb, *_: (rb, 0)),
            pl.BlockSpec((N_EXPERTS, RB), lambda rb, *_: (0, rb)),
            pl.BlockSpec((N_EXPERTS, RB), lambda rb, *_: (0, rb)),
            pl.BlockSpec((1, D_MODEL), lambda rb, *_: (0, 0)),
            pl.BlockSpec(memory_space=pl.ANY),
        ],
        out_specs=[
            pl.BlockSpec((RB, D_MODEL), lambda rb, *_: (jnp.minimum(rb, GROUP0_BLOCKS - 1), 0)),
            pl.BlockSpec((RB, D_MODEL), lambda rb, *_: (jnp.maximum(rb - GROUP0_BLOCKS, 0), 0)),
        ],
        scratch_shapes=[
            pltpu.VMEM((TILE_ROWS + WIN_ROWS, RB), F32),
            pltpu.VMEM((TILE_SLOTS, CHUNKS_PER_TILE) + CHUNK_SHAPE, BF16),
            pltpu.VMEM((RB, D_MODEL), F32),
            pltpu.SMEM((2, LIST_LEN), jnp.int32),
            pltpu.SMEM((2,), jnp.int32),
            pltpu.SemaphoreType.DMA((TILE_SLOTS,)),
        ],
    )
    return pl.pallas_call(
        _combine_kernel,
        grid_spec=gs,
        out_shape=[jax.ShapeDtypeStruct((GROUP_TOKENS[0], D_MODEL), F32),
                   jax.ShapeDtypeStruct((GROUP_TOKENS[1], D_MODEL), F32)],
        compiler_params=_cparams("arbitrary"),
    )(seg_start, seg_rows, seg_off, tot, x1, selw_t, rank_t, g, ye)


def kernel(x_prompt, x_sample, norm_mix_g, w_in, w_fourier, sink, w_out, norm_ffn_g, w_router,
           w_gate, w_up, w_down, norm_final_g):
    xp = x_prompt.reshape(-1, D_MODEL)
    xs = x_sample.reshape(-1, D_MODEL)
    cos_t, sin_t = _rope_tables()

    mix_cols = np.concatenate([
        np.arange(FOURIER_WIDTH),
        FOURIER_WIDTH + (np.asarray(Q_ORDER)[:, None] * HEAD_DIM + np.arange(HEAD_DIM)[None, :]).reshape(-1)])
    in_cols = np.concatenate([mix_cols, np.arange(MIX_WIDTH, IN_WIDTH)])
    w_in_r = w_in[0][:, in_cols].astype(BF16)
    w_out_r = w_out[0][mix_cols, :].astype(BF16)

    u, q, k, v = _inproj(xp, xs, norm_mix_g[0][None, :], w_in_r, cos_t, sin_t)
    yf = _fourier(u, w_fourier[0].astype(BF16))
    ya = _attention(sink[0], q, k, v)
    x1, xn, aff_t = _outproj(xp, xs, yf, ya, w_out_r, norm_ffn_g[0][None, :],
                             w_router[0].T.astype(BF16))

    thr0, need0 = _thresholds(aff_t[:, :GROUP_TOKENS[0]])
    thr1, need1 = _thresholds(aff_t[:, GROUP_TOKENS[0]:])
    selw_t, rank_t, m = _select(aff_t, jnp.concatenate([thr0, thr1]), jnp.concatenate([need0, need1]))

    seg_rows = ((m[:, :, 0].astype(jnp.int32) + CHUNK - 1) // CHUNK) * CHUNK
    seg_start = jnp.cumsum(seg_rows, axis=0) - seg_rows
    seg_off = jnp.cumsum(seg_rows, axis=1) - seg_rows
    tot = jnp.sum(seg_rows, axis=1)
    used = jnp.sum(seg_rows, axis=0)

    xe = _route_gather(seg_start, seg_rows, seg_off, tot, xn, selw_t, rank_t)
    ye = _ffn(used, xe, w_gate[0], w_up[0], w_down[0])
    yp, ys = _combine(seg_start, seg_rows, seg_off, tot, x1, selw_t, rank_t, norm_final_g[None, :], ye)
    return (yp.reshape(x_prompt.shape), ys.reshape(x_sample.shape))
```

```python
import functools

import numpy as np
import jax
import jax.numpy as jnp
from jax import lax
from jax.experimental import pallas as pl
from jax.experimental.pallas import tpu as pltpu

F32 = jnp.float32
BF16 = jnp.bfloat16

D_MODEL = 1024
SEQ = 16384
N_SEQ = 3
N_TOK = N_SEQ * SEQ
HEAD_DIM = 64
N_HEADS = 12
N_KV_HEADS = 4
GQA_GROUP = N_HEADS // N_KV_HEADS
ATTN_WIDTH = N_HEADS * HEAD_DIM
KV_WIDTH = N_KV_HEADS * HEAD_DIM
FOURIER_WIDTH = 256
FOURIER_GROUP_DIM = 64
MIX_WIDTH = FOURIER_WIDTH + ATTN_WIDTH
IN_WIDTH = MIX_WIDTH + 2 * KV_WIDTH
WINDOW = 128
ROPE_THETA = 10000.0
N_EXPERTS = 16
EC_CAPACITY_FACTOR = 2
EXPERT_D_FF = 2048
EPS = 1e-6
NEG = -1e30

LANES = 128
FFT_R = 128
VMEM_LIMIT = 48 * 1024 * 1024

TM = 1024
TQ = 512
QB = 128
FF_TM = 512
FF_FC = 512


def _cparams(*sem):
    return pltpu.CompilerParams(dimension_semantics=sem, vmem_limit_bytes=VMEM_LIMIT)


def _x_specs():
    n_p = SEQ // TM
    return [pl.BlockSpec((TM, D_MODEL), lambda i, *_: (jnp.minimum(i, n_p - 1), 0)),
            pl.BlockSpec((TM, D_MODEL), lambda i, *_: (jnp.maximum(i - n_p, 0), 0))]


def _x_tile(xp_ref, xs_ref):
    return jnp.where(pl.program_id(0) < SEQ // TM, xp_ref[...], xs_ref[...])


def _inproj_kernel(xp_ref, xs_ref, g_ref, w_ref, cos_ref, sin_ref, u_ref, q_ref, k_ref, v_ref):
    x = _x_tile(xp_ref, xs_ref)
    ms = jnp.mean(x * x, axis=-1, keepdims=True)
    hn = (x * lax.rsqrt(ms + EPS) * g_ref[...]).astype(BF16)
    proj = jnp.dot(hn, w_ref[...], preferred_element_type=F32)
    u_ref[...] = proj[:, :FOURIER_WIDTH].astype(BF16)
    cos = cos_ref[...]
    sin = sin_ref[...]
    lane = lax.broadcasted_iota(jnp.int32, cos.shape, 1)
    first_half = (lane % HEAD_DIM) < (HEAD_DIM // 2)

    def rope(t):
        partner = jnp.where(first_half,
                            pltpu.roll(t, LANES - HEAD_DIM // 2, 1),
                            pltpu.roll(t, HEAD_DIM // 2, 1))
        return t * cos + partner * sin

    scale = HEAD_DIM ** -0.5 * LOG2E
    for c in range(ATTN_WIDTH // LANES):
        lo = FOURIER_WIDTH + c * LANES
        q_ref[:, c * LANES:(c + 1) * LANES] = (rope(proj[:, lo:lo + LANES]) * scale).astype(BF16)
    low_half = lane < HEAD_DIM
    for c in range(KV_WIDTH // LANES):
        lo = FOURIER_WIDTH + ATTN_WIDTH + c * LANES
        kc = rope(proj[:, lo:lo + LANES])
        vc = proj[:, lo + KV_WIDTH:lo + KV_WIDTH + LANES]
        for half, keep in enumerate((low_half, ~low_half)):
            g = 2 * c + half
            k_ref[:, g * LANES:(g + 1) * LANES] = jnp.where(keep, kc, 0.0).astype(BF16)
            v_ref[:, g * LANES:(g + 1) * LANES] = jnp.where(keep, vc, 0.0).astype(BF16)


def _inproj(xp, xs, g, w_in, cos_t, sin_t):
    n_pos_blocks = SEQ // TM
    return pl.pallas_call(
        _inproj_kernel,
        grid=(N_TOK // TM,),
        in_specs=_x_specs() + [
            pl.BlockSpec((1, D_MODEL), lambda i: (0, 0)),
            pl.BlockSpec((D_MODEL, IN_WIDTH), lambda i: (0, 0)),
            pl.BlockSpec((TM, LANES), lambda i: (i % n_pos_blocks, 0)),
            pl.BlockSpec((TM, LANES), lambda i: (i % n_pos_blocks, 0)),
        ],
        out_specs=[
            pl.BlockSpec((TM, FOURIER_WIDTH), lambda i: (i, 0)),
            pl.BlockSpec((TM, ATTN_WIDTH), lambda i: (i, 0)),
            pl.BlockSpec((TM, KV4), lambda i: (i, 0)),
            pl.BlockSpec((TM, KV4), lambda i: (i, 0)),
        ],
        out_shape=[
            jax.ShapeDtypeStruct((N_TOK, FOURIER_WIDTH), BF16),
            jax.ShapeDtypeStruct((N_TOK, ATTN_WIDTH), BF16),
            jax.ShapeDtypeStruct((N_TOK, KV4), BF16),
            jax.ShapeDtypeStruct((N_TOK, KV4), BF16),
        ],
        compiler_params=_cparams("parallel"),
    )(xp, xs, g, w_in, cos_t, sin_t)


def _rope_tables():
    half = HEAD_DIM // 2
    inv_freq = 1.0 / (ROPE_THETA ** (jnp.arange(half, dtype=F32) / half))
    ang = jnp.arange(SEQ, dtype=F32)[:, None] * inv_freq[None, :]
    cos = jnp.cos(ang)
    sin = jnp.sin(ang)
    reps = LANES // HEAD_DIM
    cos_t = jnp.tile(jnp.concatenate([cos, cos], axis=-1), (1, reps))
    sin_t = jnp.tile(jnp.concatenate([-sin, sin], axis=-1), (1, reps))
    return cos_t, sin_t


def _dft_tables():
    r = FFT_R
    c = np.arange(FOURIER_GROUP_DIM)
    ang_c = 2.0 * np.pi * np.outer(c, c) / FOURIER_GROUP_DIM
    n_groups = FOURIER_WIDTH // FOURIER_GROUP_DIM
    eye = np.eye(n_groups)
    ch_scale = FOURIER_GROUP_DIM ** -0.5
    c_blk = np.kron(eye, np.cos(ang_c)) * ch_scale
    s_blk = np.kron(eye, np.sin(ang_c)) * ch_scale
    cs = np.concatenate([c_blk, -s_blk], axis=1)
    k = np.arange(r)
    ang_r = 2.0 * np.pi * np.outer(k, k) / r
    st_scale = r ** -0.5
    wr = np.cos(ang_r) * st_scale
    wi = -np.sin(ang_r) * st_scale
    wbig = np.block([[wr, -wi], [wi, wr]])
    ang_t = 2.0 * np.pi * np.outer(k, k) / (r * r)
    tr = np.cos(ang_t)
    ti = -np.sin(ang_t)
    f = lambda a: jnp.asarray(a, dtype=F32)
    return f(cs).astype(BF16), f(wbig).astype(BF16), f(wr), f(wi), f(tr), f(ti)


FS_SB = 8
FS_KB = 8


def _fft_stage1_kernel(u_ref, cs_ref, wbig_ref, y_ref):
    cs = cs_ref[...]
    wbig = wbig_ref[...]
    fw = FOURIER_WIDTH
    for j in range(FS_SB):
        u = u_ref[:, j * fw:(j + 1) * fw]
        g = jnp.dot(u, cs, preferred_element_type=F32)
        gst = jnp.concatenate([g[:, :fw], g[:, fw:]], axis=0).astype(BF16)
        y = jnp.dot(wbig, gst, preferred_element_type=F32)
        y_ref[:, j * 2 * fw:j * 2 * fw + fw] = y[:FFT_R].astype(BF16)
        y_ref[:, j * 2 * fw + fw:(j + 1) * 2 * fw] = y[FFT_R:].astype(BF16)


def _fft_stage2_kernel(y_ref, wr_ref, wi_ref, tr_ref, ti_ref, wf_ref, o_ref):
    kb = pl.program_id(1)
    wr = wr_ref[...]
    wi = wi_ref[...]
    wf = wf_ref[...]
    fw = FOURIER_WIDTH
    for j in range(FS_KB):
        k1 = kb * FS_KB + j
        tr = tr_ref[pl.ds(k1, 1), :]
        ti = ti_ref[pl.ds(k1, 1), :]
        mr = wr * tr - wi * ti
        mi = wr * ti + wi * tr
        m2 = jnp.concatenate([mr, -mi], axis=1).astype(BF16)
        yb = y_ref[j * FFT_R:(j + 1) * FFT_R, :]
        yst = jnp.concatenate([yb[:, :fw], yb[:, fw:]], axis=0)
        z = jnp.dot(m2, yst, preferred_element_type=F32)
        yf = jnp.dot(z.astype(BF16), wf, preferred_element_type=F32)
        o_ref[:, j * fw:(j + 1) * fw] = yf.astype(BF16)


def _fourier(u, w_fourier):
    cs, wbig, wr, wi, tr, ti = _dft_tables()
    r, fw = FFT_R, FOURIER_WIDTH
    u2 = u.reshape(N_SEQ * r, r * fw)
    y = pl.pallas_call(
        _fft_stage1_kernel,
        grid=(N_SEQ, r // FS_SB),
        in_specs=[
            pl.BlockSpec((r, FS_SB * fw), lambda s, j: (s, j)),
            pl.BlockSpec((fw, 2 * fw), lambda s, j: (0, 0)),
            pl.BlockSpec((2 * r, 2 * r), lambda s, j: (0, 0)),
        ],
        out_specs=pl.BlockSpec((r, FS_SB * 2 * fw), lambda s, j: (s, j)),
        out_shape=jax.ShapeDtypeStruct((N_SEQ * r, r * 2 * fw), BF16),
        compiler_params=_cparams("parallel", "parallel"),
    )(u2, cs, wbig)
    y2 = y.reshape(N_SEQ * r * r, 2 * fw)
    full = lambda s, j: (0, 0)
    yf = pl.pallas_call(
        _fft_stage2_kernel,
        grid=(N_SEQ, r // FS_KB),
        in_specs=[
            pl.BlockSpec((FS_KB * r, 2 * fw), lambda s, j: (s * (r // FS_KB) + j, 0)),
            pl.BlockSpec((r, r), full),
            pl.BlockSpec((r, r), full),
            pl.BlockSpec((r, r), full),
            pl.BlockSpec((r, r), full),
            pl.BlockSpec((fw, fw), full),
        ],
        out_specs=pl.BlockSpec((r, FS_KB * fw), lambda s, j: (s, j)),
        out_shape=jax.ShapeDtypeStruct((N_SEQ * r, r * fw), BF16),
        compiler_params=_cparams("parallel", "parallel"),
    )(y2, wr, wi, tr, ti, w_fourier)
    return yf.reshape(N_TOK, fw)


LO_HEADS = (0, 1, 2, 6, 7, 8)
HI_HEADS = (3, 4, 5, 9, 10, 11)
Q_ORDER = tuple(h for pair in zip(LO_HEADS, HI_HEADS) for h in pair)
assert all(h // GQA_GROUP % 2 == 0 for h in LO_HEADS) and all(h // GQA_GROUP % 2 == 1 for h in HI_HEADS)
KV4 = 2 * KV_WIDTH
LOG2E = 1.4426950408889634


def _attn_kernel(sink_ref, q_ref, kp_ref, kc_ref, kn_ref, vp_ref, vc_ref, vn_ref, o_ref,
                 kext, vext):
    i = pl.program_id(1)
    n_i = pl.num_programs(1)
    kext[0:QB, :] = kp_ref[...]
    kext[QB:QB + TQ, :] = kc_ref[...]
    kext[QB + TQ:, :] = kn_ref[...]
    vext[0:QB, :] = vp_ref[...]
    vext[QB:QB + TQ, :] = vc_ref[...]
    vext[QB + TQ:, :] = vn_ref[...]

    rows = GQA_GROUP * QB
    a = lax.broadcasted_iota(jnp.int32, (rows, QB), 0) % QB
    j = lax.broadcasted_iota(jnp.int32, (rows, QB), 1)
    band_prev = jnp.where(j >= a, 0.0, NEG)
    band_next = jnp.where(j <= a, 0.0, NEG)
    row_id = lax.broadcasted_iota(jnp.int32, (rows, 1), 0)
    low_half = lax.broadcasted_iota(jnp.int32, (rows, LANES), 1) < HEAD_DIM

    def softmax_half(s, sinks, bias_prev, bias_next):
        s_p = s[:, :QB] + bias_prev
        s_c = s[:, QB:2 * QB]
        s_n = s[:, 2 * QB:] + bias_next
        sink = jnp.where(row_id < QB, sinks[0], jnp.where(row_id < 2 * QB, sinks[1], sinks[2])) * LOG2E
        m = jnp.max(jnp.maximum(jnp.maximum(s_p, s_c), s_n), axis=-1, keepdims=True)
        m = jnp.maximum(m, sink)
        p_p = jnp.exp2(s_p - m)
        p_c = jnp.exp2(s_c - m)
        p_n = jnp.exp2(s_n - m)
        denom = jnp.sum(p_p + p_c + p_n, axis=-1, keepdims=True) + jnp.exp2(sink - m)
        return [p_p.astype(BF16), p_c.astype(BF16), p_n.astype(BF16)], 1.0 / denom

    for sb in range(TQ // QB):
        blk = i * (TQ // QB) + sb
        bias_prev = band_prev + jnp.where(blk == 0, NEG, 0.0)
        bias_next = band_next + jnp.where(blk == n_i * (TQ // QB) - 1, NEG, 0.0)
        for pr in range(N_KV_HEADS // 2):
            groups = [GQA_GROUP * pr + m for m in range(GQA_GROUP)]
            q3 = jnp.concatenate([q_ref[sb * QB:(sb + 1) * QB, g * LANES:(g + 1) * LANES] for g in groups],
                                 axis=0)
            kp = kext[sb * QB:sb * QB + 3 * QB, pr * 2 * LANES:(pr + 1) * 2 * LANES]
            vp = vext[sb * QB:sb * QB + 3 * QB, pr * 2 * LANES:(pr + 1) * 2 * LANES]
            kcat = jnp.concatenate([kp[:, :LANES], kp[:, LANES:]], axis=0)
            vcat = jnp.concatenate([vp[:, :LANES], vp[:, LANES:]], axis=0)
            s = lax.dot_general(q3, kcat, (((1,), (1,)), ((), ())), preferred_element_type=F32)
            p_lo, r_lo = softmax_half(s[:, :3 * QB], [sink_ref[LO_HEADS[g]] for g in groups],
                                      bias_prev, bias_next)
            p_hi, r_hi = softmax_half(s[:, 3 * QB:], [sink_ref[HI_HEADS[g]] for g in groups],
                                      bias_prev, bias_next)
            p = jnp.concatenate(p_lo + p_hi, axis=1)
            o = jnp.dot(p, vcat, preferred_element_type=F32) * jnp.where(low_half, r_lo, r_hi)
            for m, g in enumerate(groups):
                o_ref[sb * QB:(sb + 1) * QB, g * LANES:(g + 1) * LANES] = o[m * QB:(m + 1) * QB].astype(BF16)


def _attention(sink, q, k, v):
    nqb = SEQ // QB
    per = TQ // QB
    cur = lambda s, i: (s * (SEQ // TQ) + i, 0)
    prev = lambda s, i: (s * nqb + jnp.maximum(i * per - 1, 0), 0)
    nxt = lambda s, i: (s * nqb + jnp.minimum(i * per + per, nqb - 1), 0)
    return pl.pallas_call(
        _attn_kernel,
        grid=(N_SEQ, SEQ // TQ),
        in_specs=[
            pl.BlockSpec(memory_space=pltpu.SMEM),
            pl.BlockSpec((TQ, ATTN_WIDTH), cur),
            pl.BlockSpec((QB, KV4), prev),
            pl.BlockSpec((TQ, KV4), cur),
            pl.BlockSpec((QB, KV4), nxt),
            pl.BlockSpec((QB, KV4), prev),
            pl.BlockSpec((TQ, KV4), cur),
            pl.BlockSpec((QB, KV4), nxt),
        ],
        out_specs=pl.BlockSpec((TQ, ATTN_WIDTH), cur),
        out_shape=jax.ShapeDtypeStruct((N_TOK, ATTN_WIDTH), BF16),
        scratch_shapes=[pltpu.VMEM((TQ + 2 * QB, KV4), BF16),
                        pltpu.VMEM((TQ + 2 * QB, KV4), BF16)],
        compiler_params=_cparams("parallel", "parallel"),
    )(sink, q, k, k, k, v, v, v)


def _outproj_kernel(xp_ref, xs_ref, yf_ref, ya_ref, wo_ref, g_ref, wr_ref, x1_ref, xn_ref, aff_ref):
    mix = jnp.dot(yf_ref[...], wo_ref[:FOURIER_WIDTH, :], preferred_element_type=F32)
    mix = mix + jnp.dot(ya_ref[...], wo_ref[FOURIER_WIDTH:, :], preferred_element_type=F32)
    x1 = _x_tile(xp_ref, xs_ref) + mix
    x1_ref[...] = x1
    ms = jnp.mean(x1 * x1, axis=-1, keepdims=True)
    xn = (x1 * lax.rsqrt(ms + EPS) * g_ref[...]).astype(BF16)
    xn_ref[...] = xn
    logits = lax.dot_general(wr_ref[...], xn, (((1,), (1,)), ((), ())), preferred_element_type=F32)
    mx = jnp.max(logits, axis=0, keepdims=True)
    ex = jnp.exp(logits - mx)
    aff_ref[...] = ex / jnp.sum(ex, axis=0, keepdims=True)


def _outproj(xp, xs, yf, ya, w_out, g, w_router_t):
    return pl.pallas_call(
        _outproj_kernel,
        grid=(N_TOK // TM,),
        in_specs=_x_specs() + [
            pl.BlockSpec((TM, FOURIER_WIDTH), lambda i: (i, 0)),
            pl.BlockSpec((TM, ATTN_WIDTH), lambda i: (i, 0)),
            pl.BlockSpec((D_MODEL, D_MODEL), lambda i: (0, 0)),
            pl.BlockSpec((1, D_MODEL), lambda i: (0, 0)),
            pl.BlockSpec((N_EXPERTS, D_MODEL), lambda i: (0, 0)),
        ],
        out_specs=[
            pl.BlockSpec((TM, D_MODEL), lambda i: (i, 0)),
            pl.BlockSpec((TM, D_MODEL), lambda i: (i, 0)),
            pl.BlockSpec((N_EXPERTS, TM), lambda i: (0, i)),
        ],
        out_shape=[
            jax.ShapeDtypeStruct((N_TOK, D_MODEL), F32),
            jax.ShapeDtypeStruct((N_TOK, D_MODEL), BF16),
            jax.ShapeDtypeStruct((N_EXPERTS, N_TOK), F32),
        ],
        compiler_params=_cparams("parallel"),
    )(xp, xs, yf, ya, w_out, g, w_router_t)


RB = 512
N_RB = N_TOK // RB
GROUP_TOKENS = (SEQ, 2 * SEQ)
GROUP0_BLOCKS = GROUP_TOKENS[0] // RB
CHUNK = 8
TILE_ROWS = 256
CHUNKS_PER_TILE = TILE_ROWS // CHUNK
SLOT_TILES = 14
SLOT_ROWS = SLOT_TILES * FF_TM
assert SLOT_ROWS >= EC_CAPACITY_FACTOR * N_TOK // N_EXPERTS + N_RB * (CHUNK - 1)
SLOT_CHUNKS = SLOT_ROWS // CHUNK
TILE_SLOTS = 4
LOOKAHEAD = TILE_SLOTS - 1
TRASH_CHUNK = N_EXPERTS * SLOT_CHUNKS
TRASH_CHUNKS = TILE_SLOTS * CHUNKS_PER_TILE
MAX_CHUNKS = (N_EXPERTS * (RB + CHUNK - 1)) // CHUNK
CHUNK_SHAPE = (2 * CHUNK, D_MODEL // 2)


def _pack_chunks(rows_f32):
    n = rows_f32.shape[0] // CHUNK
    r3 = rows_f32.reshape(n, CHUNK, D_MODEL)
    half = D_MODEL // 2
    return jnp.concatenate([r3[:, :, :half], r3[:, :, half:]], axis=1).astype(BF16)


def _unpack_chunks(chunks_bf16):
    n = chunks_bf16.shape[0]
    c3 = chunks_bf16.astype(F32)
    left = c3[:, :CHUNK, :].reshape(n * CHUNK, D_MODEL // 2)
    right = c3[:, CHUNK:, :].reshape(n * CHUNK, D_MODEL // 2)
    return jnp.concatenate([left, right], axis=1).astype(BF16)
LIST_UNROLL = 4
LIST_LEN = ((MAX_CHUNKS + LIST_UNROLL + CHUNKS_PER_TILE - 1) // CHUNKS_PER_TILE) * CHUNKS_PER_TILE


def _thresh_kernel(aff_ref, thr_ref, need_ref, *, cap):
    aff = aff_ref[...]

    def body(i, cur):
        cand = cur | jnp.left_shift(jnp.int32(1), 30 - i)
        cnt = jnp.sum(jnp.where(aff >= pltpu.bitcast(cand, F32), 1.0, 0.0), axis=1, keepdims=True)
        return jnp.where(cnt >= cap, cand, cur)

    thr = pltpu.bitcast(lax.fori_loop(0, 31, body, jnp.zeros((N_EXPERTS, 1), jnp.int32)), F32)
    n_gt = jnp.sum(jnp.where(aff > thr, 1.0, 0.0), axis=1, keepdims=True)
    thr_ref[...] = jnp.broadcast_to(thr, thr_ref.shape)
    need_ref[...] = jnp.broadcast_to(cap - n_gt, need_ref.shape)


def _thresholds(aff_group):
    n = aff_group.shape[1]
    cap = EC_CAPACITY_FACTOR * n // N_EXPERTS
    return pl.pallas_call(
        functools.partial(_thresh_kernel, cap=float(cap)),
        out_shape=[jax.ShapeDtypeStruct((N_EXPERTS, LANES), F32),
                   jax.ShapeDtypeStruct((N_EXPERTS, LANES), F32)],
        compiler_params=pltpu.CompilerParams(vmem_limit_bytes=VMEM_LIMIT),
    )(aff_group)


def _select_kernel(aff_ref, thr_ref, need_ref, utri_ref, selw_ref, rank_ref, m_ref, eq_seen):
    rb = pl.program_id(0)

    @pl.when((rb == 0) | (rb == GROUP0_BLOCKS))
    def _():
        eq_seen[...] = jnp.zeros_like(eq_seen)

    aff = aff_ref[...]
    thr = thr_ref[:, 0:1]
    need = need_ref[:, 0:1]
    utri = utri_ref[...]
    eq = jnp.where(aff == thr, 1.0, 0.0)
    eq_before = jnp.dot(eq.astype(BF16), utri, preferred_element_type=F32) + eq_seen[...]
    sel = (aff > thr) | ((aff == thr) & (eq_before < need))
    eq_seen[...] += jnp.sum(eq, axis=1, keepdims=True)
    self = jnp.where(sel, 1.0, 0.0)
    rank_ref[...] = jnp.dot(self.astype(BF16), utri, preferred_element_type=F32)
    selw_ref[...] = jnp.where(sel, aff, 0.0)
    m_ref[0] = jnp.broadcast_to(jnp.sum(self, axis=1, keepdims=True), (N_EXPERTS, LANES))


def _select(aff_t, thr, need):
    utri = jnp.asarray(np.triu(np.ones((RB, RB), np.float32), 1), dtype=BF16)
    grp = lambda rb: (jnp.where(rb >= GROUP0_BLOCKS, 1, 0), 0)
    return pl.pallas_call(
        _select_kernel,
        grid=(N_RB,),
        in_specs=[
            pl.BlockSpec((N_EXPERTS, RB), lambda rb: (0, rb)),
            pl.BlockSpec((N_EXPERTS, LANES), grp),
            pl.BlockSpec((N_EXPERTS, LANES), grp),
            pl.BlockSpec((RB, RB), lambda rb: (0, 0)),
        ],
        out_specs=[
            pl.BlockSpec((N_EXPERTS, RB), lambda rb: (0, rb)),
            pl.BlockSpec((N_EXPERTS, RB), lambda rb: (0, rb)),
            pl.BlockSpec((1, N_EXPERTS, LANES), lambda rb: (rb, 0, 0)),
        ],
        out_shape=[
            jax.ShapeDtypeStruct((N_EXPERTS, N_TOK), F32),
            jax.ShapeDtypeStruct((N_EXPERTS, N_TOK), F32),
            jax.ShapeDtypeStruct((N_RB, N_EXPERTS, LANES), F32),
        ],
        scratch_shapes=[pltpu.VMEM((N_EXPERTS, 1), F32)],
        compiler_params=_cparams("arbitrary"),
    )(aff_t, thr, need, utri)


def _fill_chunk_list(list_ref, which, rb, seg_start_ref, seg_rows_ref):
    cnt = jnp.int32(0)
    for e in range(N_EXPERTS):
        base = e * SLOT_CHUNKS + seg_start_ref[rb, e] // CHUNK
        n = seg_rows_ref[rb, e] // CHUNK

        def body(k, carry, base=base, cnt=cnt):
            for u in range(LIST_UNROLL):
                list_ref[which, cnt + LIST_UNROLL * k + u] = base + LIST_UNROLL * k + u
            return carry

        lax.fori_loop(0, (n + LIST_UNROLL - 1) // LIST_UNROLL, body, 0)
        cnt = cnt + n
    return cnt


WIN_ROWS = 128
SPAN_TILES = 8
SPAN_ROWS = SPAN_TILES * TILE_ROWS


def _mark_span(scr, rb, s0, n_rows, tgt, values, seg_off_ref, seg_rows_ref):
    win_rows = lax.broadcasted_iota(jnp.int32, (WIN_ROWS, RB), 0).astype(F32)
    n_here = jnp.minimum(n_rows - s0, SPAN_ROWS)

    def zero(i, carry):
        scr[pl.ds(pl.multiple_of(i * TILE_ROWS, TILE_ROWS), TILE_ROWS), :] = jnp.zeros((TILE_ROWS, RB), F32)
        return carry

    lax.fori_loop(0, (n_here + TILE_ROWS - 1) // TILE_ROWS, zero, 0)
    for e in range(N_EXPERTS):
        lo = seg_off_ref[rb, e]
        first = jnp.maximum(lo, s0)
        last = jnp.minimum(lo + seg_rows_ref[rb, e], s0 + SPAN_ROWS)

        def mark(w, carry, e=e, first=first):
            start = first + w * WIN_ROWS
            win = pl.ds(pl.multiple_of(start - s0, CHUNK), WIN_ROWS)
            scr[win, :] = jnp.where(tgt[e:e + 1, :] == win_rows + start.astype(F32), values[e], scr[win, :])
            return carry

        lax.fori_loop(0, jnp.maximum(last - first + WIN_ROWS - 1, 0) // WIN_ROWS, mark, 0)


def _gather_kernel(seg_start_ref, seg_rows_ref, seg_off_ref, tot_ref,
                   x_ref, selw_ref, rank_ref, xe_hbm,
                   p_scr, obuf, zbuf, dst_list, pending, sem, zsem):
    rb = pl.program_id(0)
    n_rows = tot_ref[rb]
    n_chunks = n_rows // CHUNK
    n_tiles = (n_rows + TILE_ROWS - 1) // TILE_ROWS

    @pl.when(rb == 0)
    def _():
        for slot in range(TILE_SLOTS):
            pending[slot] = 0
        p_scr[...] = jnp.zeros_like(p_scr)

    _fill_chunk_list(dst_list, 0, rb, seg_start_ref, seg_rows_ref)

    def tile_copy(slot):
        return pltpu.make_async_copy(obuf.at[slot], xe_hbm.at[pl.ds(0, CHUNKS_PER_TILE)], sem.at[slot])

    e_iota = lax.broadcasted_iota(jnp.int32, (N_EXPERTS, 1), 0)
    off = jnp.zeros((N_EXPERTS, 1), F32)
    for e in range(N_EXPERTS):
        off = jnp.where(e_iota == e, seg_off_ref[rb, e].astype(F32), off)
    tgt = jnp.where(selw_ref[...] > 0.0, rank_ref[...] + off, -1.0)
    x = x_ref[...]

    def tile_body(t, carry):
        slot = t % TILE_SLOTS

        @pl.when(pending[slot] == 1)
        def _():
            tile_copy(slot).wait()

        @pl.when(t % SPAN_TILES == 0)
        def _():
            _mark_span(p_scr, rb, t * TILE_ROWS, n_rows, tgt, [1.0] * N_EXPERTS, seg_off_ref, seg_rows_ref)

        p = p_scr[pl.ds(pl.multiple_of((t % SPAN_TILES) * TILE_ROWS, TILE_ROWS), TILE_ROWS), :]
        obuf[slot] = _pack_chunks(jnp.dot(p.astype(BF16), x, preferred_element_type=F32))

        for i in range(CHUNKS_PER_TILE):
            g = t * CHUNKS_PER_TILE + i
            dst = jnp.where(g < n_chunks, dst_list[0, jnp.minimum(g, n_chunks - 1)],
                            TRASH_CHUNK + slot * CHUNKS_PER_TILE + i)
            pltpu.make_async_copy(obuf.at[slot, i], xe_hbm.at[dst], sem.at[slot]).start()
        pending[slot] = 1
        return carry

    lax.fori_loop(0, n_tiles, tile_body, 0)

    @pl.when(rb == N_RB - 1)
    def _():
        for slot in range(TILE_SLOTS):
            @pl.when(pending[slot] == 1)
            def _(slot=slot):
                tile_copy(slot).wait()
        zbuf[...] = jnp.zeros_like(zbuf)

        def zero_fill(first_chunk, count):
            def zcopy(k):
                return pltpu.make_async_copy(zbuf, xe_hbm.at[first_chunk + k], zsem)

            def zstart(k, c):
                zcopy(k).start()
                return c

            def zwait(k, c):
                zcopy(k).wait()
                return c

            lax.fori_loop(0, count, zstart, 0)
            lax.fori_loop(0, count, zwait, 0)

        for e in range(N_EXPERTS):
            used_chunks = (seg_start_ref[rb, e] + seg_rows_ref[rb, e]) // CHUNK
            zero_fill(e * SLOT_CHUNKS + used_chunks, SLOT_CHUNKS - used_chunks)
        zero_fill(TRASH_CHUNK, TRASH_CHUNKS)


def _route_gather(seg_start, seg_rows, seg_off, tot, xn, selw_t, rank_t):
    gs = pltpu.PrefetchScalarGridSpec(
        num_scalar_prefetch=4,
        grid=(N_RB,),
        in_specs=[
            pl.BlockSpec((RB, D_MODEL), lambda rb, *_: (rb, 0)),
            pl.BlockSpec((N_EXPERTS, RB), lambda rb, *_: (0, rb)),
            pl.BlockSpec((N_EXPERTS, RB), lambda rb, *_: (0, rb)),
        ],
        out_specs=pl.BlockSpec(memory_space=pl.ANY),
        scratch_shapes=[
            pltpu.VMEM((SPAN_ROWS + WIN_ROWS, RB), F32),
            pltpu.VMEM((TILE_SLOTS, CHUNKS_PER_TILE) + CHUNK_SHAPE, BF16),
            pltpu.VMEM(CHUNK_SHAPE, BF16),
            pltpu.SMEM((1, LIST_LEN), jnp.int32),
            pltpu.SMEM((TILE_SLOTS,), jnp.int32),
            pltpu.SemaphoreType.DMA((TILE_SLOTS,)),
            pltpu.SemaphoreType.DMA,
        ],
    )
    return pl.pallas_call(
        _gather_kernel,
        grid_spec=gs,
        out_shape=jax.ShapeDtypeStruct((TRASH_CHUNK + TRASH_CHUNKS,) + CHUNK_SHAPE, BF16),
        compiler_params=_cparams("arbitrary"),
    )(seg_start, seg_rows, seg_off, tot, xn, selw_t, rank_t)


W_PIECES_PER_MATRIX = 4
W_PIECES = 3 * W_PIECES_PER_MATRIX
assert SLOT_TILES > W_PIECES
WG_PIECE_ROWS = D_MODEL // W_PIECES_PER_MATRIX
WD_PIECE_ROWS = EXPERT_D_FF // W_PIECES_PER_MATRIX
FFN_VMEM_LIMIT = 56 * 1024 * 1024


def _ffn_kernel(used_ref, x_ref, wg_hbm, wu_hbm, wd_hbm, o_ref,
                wg_bf, wu_bf, wd_bf, stage_up, stage_dn, sem):
    e = pl.program_id(0)
    i = pl.program_id(1)
    slot = e % 2

    matrices = ((wg_hbm, stage_up, wg_bf, WG_PIECE_ROWS),
                (wu_hbm, stage_up, wu_bf, WG_PIECE_ROWS),
                (wd_hbm, stage_dn, wd_bf, WD_PIECE_ROWS))

    def piece(expert, k, buf, finish):
        def of_kind(kind):
            w_hbm, stage, w_bf, rows = matrices[kind]
            r0 = pl.multiple_of((k - kind * W_PIECES_PER_MATRIX) * rows, rows)
            cp = pltpu.make_async_copy(w_hbm.at[expert, pl.ds(r0, rows), :], stage.at[buf], sem.at[buf])
            if finish:
                cp.wait()
                w_bf[expert % 2, pl.ds(r0, rows), :] = stage[buf].astype(BF16)
            else:
                cp.start()

        if isinstance(k, int):
            of_kind(k // W_PIECES_PER_MATRIX)
        else:
            for kind in range(len(matrices)):
                pl.when(k // W_PIECES_PER_MATRIX == kind)(functools.partial(of_kind, kind))

    @pl.when((e == 0) & (i == 0))
    def _():
        for k in range(W_PIECES):
            piece(0, k, k % 2, finish=False)
            piece(0, k, k % 2, finish=True)

    @pl.when(e + 1 < N_EXPERTS)
    def _():
        @pl.when((i >= 1) & (i <= W_PIECES))
        def _():
            piece(e + 1, i - 1, (i - 1) % 2, finish=True)

        @pl.when(i < W_PIECES)
        def _():
            piece(e + 1, i, i % 2, finish=False)

    @pl.when(i * FF_TM < used_ref[e])
    def _():
        x = _unpack_chunks(x_ref[...])
        acc = jnp.zeros((FF_TM, D_MODEL), F32)
        for c in range(EXPERT_D_FF // FF_FC):
            gt = jnp.dot(x, wg_bf[slot, :, c * FF_FC:(c + 1) * FF_FC], preferred_element_type=F32)
            up = jnp.dot(x, wu_bf[slot, :, c * FF_FC:(c + 1) * FF_FC], preferred_element_type=F32)
            h = (gt * (1.0 / (1.0 + jnp.exp(-gt))) * up).astype(BF16)
            acc = acc + jnp.dot(h, wd_bf[slot, c * FF_FC:(c + 1) * FF_FC, :], preferred_element_type=F32)
        o_ref[...] = _pack_chunks(acc)

    @pl.when(i * FF_TM >= used_ref[e])
    def _():
        o_ref[...] = jnp.zeros_like(o_ref)


def _ffn(used, xe, w_gate, w_up, w_down):
    def in_rows(e, i, used_ref):
        last = jnp.maximum((used_ref[e] + FF_TM - 1) // FF_TM - 1, 0)
        return (e * SLOT_TILES + jnp.minimum(i, last), 0, 0)

    def out_rows(e, i, used_ref):
        return (e * SLOT_TILES + i, 0, 0)

    tile_chunks = (FF_TM // CHUNK,) + CHUNK_SHAPE
    gs = pltpu.PrefetchScalarGridSpec(
        num_scalar_prefetch=1,
        grid=(N_EXPERTS, SLOT_TILES),
        in_specs=[
            pl.BlockSpec(tile_chunks, in_rows),
            pl.BlockSpec(memory_space=pl.ANY),
            pl.BlockSpec(memory_space=pl.ANY),
            pl.BlockSpec(memory_space=pl.ANY),
        ],
        out_specs=pl.BlockSpec(tile_chunks, out_rows),
        scratch_shapes=[
            pltpu.VMEM((2, D_MODEL, EXPERT_D_FF), BF16),
            pltpu.VMEM((2, D_MODEL, EXPERT_D_FF), BF16),
            pltpu.VMEM((2, EXPERT_D_FF, D_MODEL), BF16),
            pltpu.VMEM((2, WG_PIECE_ROWS, EXPERT_D_FF), F32),
            pltpu.VMEM((2, WD_PIECE_ROWS, D_MODEL), F32),
            pltpu.SemaphoreType.DMA((2,)),
        ],
    )
    return pl.pallas_call(
        _ffn_kernel,
        grid_spec=gs,
        out_shape=jax.ShapeDtypeStruct((N_EXPERTS * SLOT_CHUNKS,) + CHUNK_SHAPE, BF16),
        compiler_params=pltpu.CompilerParams(dimension_semantics=("arbitrary", "arbitrary"),
                                             vmem_limit_bytes=FFN_VMEM_LIMIT),
    )(used, xe, w_gate, w_up, w_down)


def _combine_kernel(seg_start_ref, seg_rows_ref, seg_off_ref, tot_ref,
                    x1_ref, selw_ref, rank_ref, g_ref, ye_hbm, op_ref, os_ref,
                    w_scr, ybuf, acc, src_list, state, sem):
    rb = pl.program_id(0)
    cur = rb % 2
    nxt_rb = jnp.minimum(rb + 1, N_RB - 1)

    def tiles_of(b):
        return (tot_ref[b] + TILE_ROWS - 1) // TILE_ROWS

    n_tiles = tiles_of(rb)
    n_tiles_next = jnp.where(rb + 1 < N_RB, tiles_of(nxt_rb), 0)

    def fetch(b, which, t, slot):
        n_chunks = tot_ref[b] // CHUNK
        for i in range(CHUNKS_PER_TILE):
            g = t * CHUNKS_PER_TILE + i
            src = jnp.where(g < n_chunks, src_list[which, jnp.minimum(g, n_chunks - 1)], 0)
            pltpu.make_async_copy(ye_hbm.at[src], ybuf.at[slot, i], sem.at[slot]).start()

    def tile_wait(slot):
        pltpu.make_async_copy(ye_hbm.at[pl.ds(0, CHUNKS_PER_TILE)], ybuf.at[slot], sem.at[slot]).wait()

    @pl.when(rb == 0)
    def _():
        w_scr[...] = jnp.zeros_like(w_scr)
        _fill_chunk_list(src_list, 0, 0, seg_start_ref, seg_rows_ref)
        state[0] = 0
        state[1] = 0

    @pl.when(rb + 1 < N_RB)
    def _():
        _fill_chunk_list(src_list, 1 - cur, nxt_rb, seg_start_ref, seg_rows_ref)

    slot0 = state[0]
    started = state[1]
    for j in range(LOOKAHEAD):
        @pl.when((j >= started) & (j < n_tiles))
        def _(j=j):
            fetch(rb, cur, j, (slot0 + j) % TILE_SLOTS)

    e_iota = lax.broadcasted_iota(jnp.int32, (N_EXPERTS, 1), 0)
    off = jnp.zeros((N_EXPERTS, 1), F32)
    for e in range(N_EXPERTS):
        off = jnp.where(e_iota == e, seg_off_ref[rb, e].astype(F32), off)
    selw = selw_ref[...]
    gates = [selw[e:e + 1, :] for e in range(N_EXPERTS)]
    tgt = jnp.where(selw > 0.0, rank_ref[...] + off, -1.0)
    acc[...] = x1_ref[...]

    def tile_body(t, next_started):
        slot = (slot0 + t) % TILE_SLOTS
        ahead = t + LOOKAHEAD

        @pl.when(ahead < n_tiles)
        def _():
            fetch(rb, cur, ahead, (slot0 + ahead) % TILE_SLOTS)

        start_next = (ahead >= n_tiles) & (next_started < jnp.minimum(LOOKAHEAD, n_tiles_next))

        @pl.when(start_next)
        def _():
            fetch(nxt_rb, 1 - cur, next_started, (slot0 + n_tiles + next_started) % TILE_SLOTS)

        @pl.when(t % SPAN_TILES == 0)
        def _():
            _mark_span(w_scr, rb, t * TILE_ROWS, tot_ref[rb], tgt, gates, seg_off_ref, seg_rows_ref)

        tile_wait(slot)
        w = w_scr[pl.ds(pl.multiple_of((t % SPAN_TILES) * TILE_ROWS, TILE_ROWS), TILE_ROWS), :]
        acc[...] += lax.dot_general(w.astype(BF16), _unpack_chunks(ybuf[slot]),
                                    (((0,), (0,)), ((), ())), preferred_element_type=F32)
        return next_started + start_next.astype(jnp.int32)

    state[1] = lax.fori_loop(0, n_tiles, tile_body, jnp.int32(0))
    state[0] = (slot0 + n_tiles) % TILE_SLOTS
    x2 = acc[...]
    ms = jnp.mean(x2 * x2, axis=-1, keepdims=True)
    y = x2 * lax.rsqrt(ms + EPS) * g_ref[...]

    @pl.when(rb < GROUP0_BLOCKS)
    def _():
        op_ref[...] = y

    @pl.when(rb >= GROUP0_BLOCKS)
    def _():
        os_ref[...] = y


def _combine(seg_start, seg_rows, seg_off, tot, x1, selw_t, rank_t, g, ye):
    gs = pltpu.PrefetchScalarGridSpec(
        num_scalar_prefetch=4,
        grid=(N_RB,),
        in_specs=[
            pl.BlockSpec((RB, D_MODEL), lambda rb, *_: (rb, 0)),
            pl.BlockSpec((N_EXPERTS, RB), lambda rb, *_: (0, rb)),
            pl.BlockSpec((N_EXPERTS, RB), lambda rb, *_: (0, rb)),
            pl.BlockSpec((1, D_MODEL), lambda rb, *_: (0, 0)),
            pl.BlockSpec(memory_space=pl.ANY),
        ],
        out_specs=[
            pl.BlockSpec((RB, D_MODEL), lambda rb, *_: (jnp.minimum(rb, GROUP0_BLOCKS - 1), 0)),
            pl.BlockSpec((RB, D_MODEL), lambda rb, *_: (jnp.maximum(rb - GROUP0_BLOCKS, 0), 0)),
        ],
        scratch_shapes=[
            pltpu.VMEM((SPAN_ROWS + WIN_ROWS, RB), F32),
            pltpu.VMEM((TILE_SLOTS, CHUNKS_PER_TILE) + CHUNK_SHAPE, BF16),
            pltpu.VMEM((RB, D_MODEL), F32),
            pltpu.SMEM((2, LIST_LEN), jnp.int32),
            pltpu.SMEM((2,), jnp.int32),
            pltpu.SemaphoreType.DMA((TILE_SLOTS,)),
        ],
    )
    return pl.pallas_call(
        _combine_kernel,
        grid_spec=gs,
        out_shape=[jax.ShapeDtypeStruct((GROUP_TOKENS[0], D_MODEL), F32),
                   jax.ShapeDtypeStruct((GROUP_TOKENS[1], D_MODEL), F32)],
        compiler_params=_cparams("arbitrary"),
    )(seg_start, seg_rows, seg_off, tot, x1, selw_t, rank_t, g, ye)


def kernel(x_prompt, x_sample, norm_mix_g, w_in, w_fourier, sink, w_out, norm_ffn_g, w_router,
           w_gate, w_up, w_down, norm_final_g):
    assert x_prompt.shape == (1, SEQ, D_MODEL) and x_sample.shape == (N_SEQ - 1, SEQ, D_MODEL)
    assert w_in.shape == (1, D_MODEL, IN_WIDTH) and w_out.shape == (1, MIX_WIDTH, D_MODEL)
    assert w_gate.shape == (1, N_EXPERTS, D_MODEL, EXPERT_D_FF) and w_down.shape == (1, N_EXPERTS, EXPERT_D_FF, D_MODEL)
    xp = x_prompt.reshape(-1, D_MODEL)
    xs = x_sample.reshape(-1, D_MODEL)
    cos_t, sin_t = _rope_tables()

    mix_cols = np.concatenate([
        np.arange(FOURIER_WIDTH),
        FOURIER_WIDTH + (np.asarray(Q_ORDER)[:, None] * HEAD_DIM + np.arange(HEAD_DIM)[None, :]).reshape(-1)])
    in_cols = np.concatenate([mix_cols, np.arange(MIX_WIDTH, IN_WIDTH)])
    w_in_r = w_in[0][:, in_cols].astype(BF16)
    w_out_r = w_out[0][mix_cols, :].astype(BF16)

    u, q, k, v = _inproj(xp, xs, norm_mix_g[0][None, :], w_in_r, cos_t, sin_t)
    yf = _fourier(u, w_fourier[0].astype(BF16))
    ya = _attention(sink[0], q, k, v)
    x1, xn, aff_t = _outproj(xp, xs, yf, ya, w_out_r, norm_ffn_g[0][None, :],
                             w_router[0].T.astype(BF16))

    thr0, need0 = _thresholds(aff_t[:, :GROUP_TOKENS[0]])
    thr1, need1 = _thresholds(aff_t[:, GROUP_TOKENS[0]:])
    selw_t, rank_t, m = _select(aff_t, jnp.concatenate([thr0, thr1]), jnp.concatenate([need0, need1]))

    seg_rows = ((m[:, :, 0].astype(jnp.int32) + CHUNK - 1) // CHUNK) * CHUNK
    seg_start = jnp.cumsum(seg_rows, axis=0) - seg_rows
    seg_off = jnp.cumsum(seg_rows, axis=1) - seg_rows
    tot = jnp.sum(seg_rows, axis=1)
    used = jnp.sum(seg_rows, axis=0)

    xe = _route_gather(seg_start, seg_rows, seg_off, tot, xn, selw_t, rank_t)
    ye = _ffn(used, xe, w_gate[0], w_up[0], w_down[0])
    yp, ys = _combine(seg_start, seg_rows, seg_off, tot, x1, selw_t, rank_t, norm_final_g[None, :], ye)
    return (yp.reshape(x_prompt.shape), ys.reshape(x_sample.shape))
```

```python
import functools

import numpy as np
import jax
import jax.numpy as jnp
from jax import lax
from jax.experimental import pallas as pl
from jax.experimental.pallas import tpu as pltpu

F32 = jnp.float32
BF16 = jnp.bfloat16

D_MODEL = 1024
SEQ = 16384
N_SEQ = 3
N_TOK = N_SEQ * SEQ
HEAD_DIM = 64
N_HEADS = 12
N_KV_HEADS = 4
GQA_GROUP = N_HEADS // N_KV_HEADS
ATTN_WIDTH = N_HEADS * HEAD_DIM
KV_WIDTH = N_KV_HEADS * HEAD_DIM
FOURIER_WIDTH = 256
FOURIER_GROUP_DIM = 64
MIX_WIDTH = FOURIER_WIDTH + ATTN_WIDTH
IN_WIDTH = MIX_WIDTH + 2 * KV_WIDTH
WINDOW = 128
ROPE_THETA = 10000.0
N_EXPERTS = 16
EC_CAPACITY_FACTOR = 2
EXPERT_D_FF = 2048
EPS = 1e-6
NEG = -1e30

LANES = 128
FFT_R = 128
VMEM_LIMIT = 48 * 1024 * 1024

TM = 1024
TQ = 512
QB = 128
FF_TM = 512
FF_FC = 512


def _cparams(*sem):
    return pltpu.CompilerParams(dimension_semantics=sem, vmem_limit_bytes=VMEM_LIMIT)


def _x_specs():
    n_p = SEQ // TM
    return [pl.BlockSpec((TM, D_MODEL), lambda i, *_: (jnp.minimum(i, n_p - 1), 0)),
            pl.BlockSpec((TM, D_MODEL), lambda i, *_: (jnp.maximum(i - n_p, 0), 0))]


def _x_tile(xp_ref, xs_ref):
    return jnp.where(pl.program_id(0) < SEQ // TM, xp_ref[...], xs_ref[...])


def _inproj_kernel(xp_ref, xs_ref, g_ref, w_ref, cos_ref, sin_ref, u_ref, q_ref, k_ref, v_ref):
    x = _x_tile(xp_ref, xs_ref)
    ms = jnp.mean(x * x, axis=-1, keepdims=True)
    hn = (x * lax.rsqrt(ms + EPS) * g_ref[...]).astype(BF16)
    proj = jnp.dot(hn, w_ref[...], preferred_element_type=F32)
    u_ref[...] = proj[:, :FOURIER_WIDTH].astype(BF16)
    cos = cos_ref[...]
    sin = sin_ref[...]
    lane = lax.broadcasted_iota(jnp.int32, cos.shape, 1)
    first_half = (lane % HEAD_DIM) < (HEAD_DIM // 2)

    def rope(t):
        partner = jnp.where(first_half,
                            pltpu.roll(t, LANES - HEAD_DIM // 2, 1),
                            pltpu.roll(t, HEAD_DIM // 2, 1))
        return t * cos + partner * sin

    scale = HEAD_DIM ** -0.5 * LOG2E
    for c in range(ATTN_WIDTH // LANES):
        lo = FOURIER_WIDTH + c * LANES
        q_ref[:, c * LANES:(c + 1) * LANES] = (rope(proj[:, lo:lo + LANES]) * scale).astype(BF16)
    low_half = lane < HEAD_DIM
    for c in range(KV_WIDTH // LANES):
        lo = FOURIER_WIDTH + ATTN_WIDTH + c * LANES
        kc = rope(proj[:, lo:lo + LANES])
        vc = proj[:, lo + KV_WIDTH:lo + KV_WIDTH + LANES]
        for half, keep in enumerate((low_half, ~low_half)):
            g = 2 * c + half
            k_ref[:, g * LANES:(g + 1) * LANES] = jnp.where(keep, kc, 0.0).astype(BF16)
            v_ref[:, g * LANES:(g + 1) * LANES] = jnp.where(keep, vc, 0.0).astype(BF16)


def _inproj(xp, xs, g, w_in, cos_t, sin_t):
    n_pos_blocks = SEQ // TM
    return pl.pallas_call(
        _inproj_kernel,
        grid=(N_TOK // TM,),
        in_specs=_x_specs() + [
            pl.BlockSpec((1, D_MODEL), lambda i: (0, 0)),
            pl.BlockSpec((D_MODEL, IN_WIDTH), lambda i: (0, 0)),
            pl.BlockSpec((TM, LANES), lambda i: (i % n_pos_blocks, 0)),
            pl.BlockSpec((TM, LANES), lambda i: (i % n_pos_blocks, 0)),
        ],
        out_specs=[
            pl.BlockSpec((TM, FOURIER_WIDTH), lambda i: (i, 0)),
            pl.BlockSpec((TM, ATTN_WIDTH), lambda i: (i, 0)),
            pl.BlockSpec((TM, KV4), lambda i: (i, 0)),
            pl.BlockSpec((TM, KV4), lambda i: (i, 0)),
        ],
        out_shape=[
            jax.ShapeDtypeStruct((N_TOK, FOURIER_WIDTH), BF16),
            jax.ShapeDtypeStruct((N_TOK, ATTN_WIDTH), BF16),
            jax.ShapeDtypeStruct((N_TOK, KV4), BF16),
            jax.ShapeDtypeStruct((N_TOK, KV4), BF16),
        ],
        compiler_params=_cparams("parallel"),
    )(xp, xs, g, w_in, cos_t, sin_t)


def _rope_tables():
    half = HEAD_DIM // 2
    inv_freq = 1.0 / (ROPE_THETA ** (jnp.arange(half, dtype=F32) / half))
    ang = jnp.arange(SEQ, dtype=F32)[:, None] * inv_freq[None, :]
    cos = jnp.cos(ang)
    sin = jnp.sin(ang)
    reps = LANES // HEAD_DIM
    cos_t = jnp.tile(jnp.concatenate([cos, cos], axis=-1), (1, reps))
    sin_t = jnp.tile(jnp.concatenate([-sin, sin], axis=-1), (1, reps))
    return cos_t, sin_t


def _dft_tables():
    r = FFT_R
    c = np.arange(FOURIER_GROUP_DIM)
    ang_c = 2.0 * np.pi * np.outer(c, c) / FOURIER_GROUP_DIM
    n_groups = FOURIER_WIDTH // FOURIER_GROUP_DIM
    eye = np.eye(n_groups)
    ch_scale = FOURIER_GROUP_DIM ** -0.5
    c_blk = np.kron(eye, np.cos(ang_c)) * ch_scale
    s_blk = np.kron(eye, np.sin(ang_c)) * ch_scale
    cs = np.concatenate([c_blk, -s_blk], axis=1)
    k = np.arange(r)
    ang_r = 2.0 * np.pi * np.outer(k, k) / r
    st_scale = r ** -0.5
    wr = np.cos(ang_r) * st_scale
    wi = -np.sin(ang_r) * st_scale
    wbig = np.block([[wr, -wi], [wi, wr]])
    ang_t = 2.0 * np.pi * np.outer(k, k) / (r * r)
    tr = np.cos(ang_t)
    ti = -np.sin(ang_t)
    f = lambda a: jnp.asarray(a, dtype=F32)
    return f(cs).astype(BF16), f(wbig).astype(BF16), f(wr), f(wi), f(tr), f(ti)


FS_SB = 8
FS_KB = 8


def _fft_stage1_kernel(u_ref, cs_ref, wbig_ref, y_ref):
    cs = cs_ref[...]
    wbig = wbig_ref[...]
    fw = FOURIER_WIDTH
    for j in range(FS_SB):
        u = u_ref[:, j * fw:(j + 1) * fw]
        g = jnp.dot(u, cs, preferred_element_type=F32)
        gst = jnp.concatenate([g[:, :fw], g[:, fw:]], axis=0).astype(BF16)
        y = jnp.dot(wbig, gst, preferred_element_type=F32)
        y_ref[:, j * 2 * fw:j * 2 * fw + fw] = y[:FFT_R].astype(BF16)
        y_ref[:, j * 2 * fw + fw:(j + 1) * 2 * fw] = y[FFT_R:].astype(BF16)


def _fft_stage2_kernel(y_ref, wr_ref, wi_ref, tr_ref, ti_ref, wf_ref, o_ref):
    kb = pl.program_id(1)
    wr = wr_ref[...]
    wi = wi_ref[...]
    wf = wf_ref[...]
    fw = FOURIER_WIDTH
    for j in range(FS_KB):
        k1 = kb * FS_KB + j
        tr = tr_ref[pl.ds(k1, 1), :]
        ti = ti_ref[pl.ds(k1, 1), :]
        mr = wr * tr - wi * ti
        mi = wr * ti + wi * tr
        m2 = jnp.concatenate([mr, -mi], axis=1).astype(BF16)
        yb = y_ref[j * FFT_R:(j + 1) * FFT_R, :]
        yst = jnp.concatenate([yb[:, :fw], yb[:, fw:]], axis=0)
        z = jnp.dot(m2, yst, preferred_element_type=F32)
        yf = jnp.dot(z.astype(BF16), wf, preferred_element_type=F32)
        o_ref[:, j * fw:(j + 1) * fw] = yf.astype(BF16)


def _fourier(u, w_fourier):
    cs, wbig, wr, wi, tr, ti = _dft_tables()
    r, fw = FFT_R, FOURIER_WIDTH
    u2 = u.reshape(N_SEQ * r, r * fw)
    y = pl.pallas_call(
        _fft_stage1_kernel,
        grid=(N_SEQ, r // FS_SB),
        in_specs=[
            pl.BlockSpec((r, FS_SB * fw), lambda s, j: (s, j)),
            pl.BlockSpec((fw, 2 * fw), lambda s, j: (0, 0)),
            pl.BlockSpec((2 * r, 2 * r), lambda s, j: (0, 0)),
        ],
        out_specs=pl.BlockSpec((r, FS_SB * 2 * fw), lambda s, j: (s, j)),
        out_shape=jax.ShapeDtypeStruct((N_SEQ * r, r * 2 * fw), BF16),
        compiler_params=_cparams("parallel", "parallel"),
    )(u2, cs, wbig)
    y2 = y.reshape(N_SEQ * r * r, 2 * fw)
    full = lambda s, j: (0, 0)
    yf = pl.pallas_call(
        _fft_stage2_kernel,
        grid=(N_SEQ, r // FS_KB),
        in_specs=[
            pl.BlockSpec((FS_KB * r, 2 * fw), lambda s, j: (s * (r // FS_KB) + j, 0)),
            pl.BlockSpec((r, r), full),
            pl.BlockSpec((r, r), full),
            pl.BlockSpec((r, r), full),
            pl.BlockSpec((r, r), full),
            pl.BlockSpec((fw, fw), full),
        ],
        out_specs=pl.BlockSpec((r, FS_KB * fw), lambda s, j: (s, j)),
        out_shape=jax.ShapeDtypeStruct((N_SEQ * r, r * fw), BF16),
        compiler_params=_cparams("parallel", "parallel"),
    )(y2, wr, wi, tr, ti, w_fourier)
    return yf.reshape(N_TOK, fw)


LO_HEADS = (0, 1, 2, 6, 7, 8)
HI_HEADS = (3, 4, 5, 9, 10, 11)
Q_ORDER = tuple(h for pair in zip(LO_HEADS, HI_HEADS) for h in pair)
assert all(h // GQA_GROUP % 2 == 0 for h in LO_HEADS) and all(h // GQA_GROUP % 2 == 1 for h in HI_HEADS)
KV4 = 2 * KV_WIDTH
LOG2E = 1.4426950408889634


def _attn_kernel(sink_ref, q_ref, kp_ref, kc_ref, kn_ref, vp_ref, vc_ref, vn_ref, o_ref,
                 kext, vext):
    i = pl.program_id(1)
    n_i = pl.num_programs(1)
    kext[0:QB, :] = kp_ref[...]
    kext[QB:QB + TQ, :] = kc_ref[...]
    kext[QB + TQ:, :] = kn_ref[...]
    vext[0:QB, :] = vp_ref[...]
    vext[QB:QB + TQ, :] = vc_ref[...]
    vext[QB + TQ:, :] = vn_ref[...]

    rows = GQA_GROUP * QB
    a = lax.broadcasted_iota(jnp.int32, (rows, QB), 0) % QB
    j = lax.broadcasted_iota(jnp.int32, (rows, QB), 1)
    band_prev = jnp.where(j >= a, 0.0, NEG)
    band_next = jnp.where(j <= a, 0.0, NEG)
    row_id = lax.broadcasted_iota(jnp.int32, (rows, 1), 0)
    low_half = lax.broadcasted_iota(jnp.int32, (rows, LANES), 1) < HEAD_DIM

    def softmax_half(s, sinks, bias_prev, bias_next):
        s_p = s[:, :QB] + bias_prev
        s_c = s[:, QB:2 * QB]
        s_n = s[:, 2 * QB:] + bias_next
        sink = jnp.where(row_id < QB, sinks[0], jnp.where(row_id < 2 * QB, sinks[1], sinks[2])) * LOG2E
        m = jnp.max(jnp.maximum(jnp.maximum(s_p, s_c), s_n), axis=-1, keepdims=True)
        m = jnp.maximum(m, sink)
        p_p = jnp.exp2(s_p - m)
        p_c = jnp.exp2(s_c - m)
        p_n = jnp.exp2(s_n - m)
        denom = jnp.sum(p_p + p_c + p_n, axis=-1, keepdims=True) + jnp.exp2(sink - m)
        return [p_p.astype(BF16), p_c.astype(BF16), p_n.astype(BF16)], 1.0 / denom

    for sb in range(TQ // QB):
        blk = i * (TQ // QB) + sb
        bias_prev = band_prev + jnp.where(blk == 0, NEG, 0.0)
        bias_next = band_next + jnp.where(blk == n_i * (TQ // QB) - 1, NEG, 0.0)
        for pr in range(N_KV_HEADS // 2):
            groups = [GQA_GROUP * pr + m for m in range(GQA_GROUP)]
            q3 = jnp.concatenate([q_ref[sb * QB:(sb + 1) * QB, g * LANES:(g + 1) * LANES] for g in groups],
                                 axis=0)
            kp = kext[sb * QB:sb * QB + 3 * QB, pr * 2 * LANES:(pr + 1) * 2 * LANES]
            vp = vext[sb * QB:sb * QB + 3 * QB, pr * 2 * LANES:(pr + 1) * 2 * LANES]
            kcat = jnp.concatenate([kp[:, :LANES], kp[:, LANES:]], axis=0)
            vcat = jnp.concatenate([vp[:, :LANES], vp[:, LANES:]], axis=0)
            s = lax.dot_general(q3, kcat, (((1,), (1,)), ((), ())), preferred_element_type=F32)
            p_lo, r_lo = softmax_half(s[:, :3 * QB], [sink_ref[LO_HEADS[g]] for g in groups],
                                      bias_prev, bias_next)
            p_hi, r_hi = softmax_half(s[:, 3 * QB:], [sink_ref[HI_HEADS[g]] for g in groups],
                                      bias_prev, bias_next)
            p = jnp.concatenate(p_lo + p_hi, axis=1)
            o = jnp.dot(p, vcat, preferred_element_type=F32) * jnp.where(low_half, r_lo, r_hi)
            for m, g in enumerate(groups):
                o_ref[sb * QB:(sb + 1) * QB, g * LANES:(g + 1) * LANES] = o[m * QB:(m + 1) * QB].astype(BF16)


def _attention(sink, q, k, v):
    nqb = SEQ // QB
    per = TQ // QB
    cur = lambda s, i: (s * (SEQ // TQ) + i, 0)
    prev = lambda s, i: (s * nqb + jnp.maximum(i * per - 1, 0), 0)
    nxt = lambda s, i: (s * nqb + jnp.minimum(i * per + per, nqb - 1), 0)
    return pl.pallas_call(
        _attn_kernel,
        grid=(N_SEQ, SEQ // TQ),
        in_specs=[
            pl.BlockSpec(memory_space=pltpu.SMEM),
            pl.BlockSpec((TQ, ATTN_WIDTH), cur),
            pl.BlockSpec((QB, KV4), prev),
            pl.BlockSpec((TQ, KV4), cur),
            pl.BlockSpec((QB, KV4), nxt),
            pl.BlockSpec((QB, KV4), prev),
            pl.BlockSpec((TQ, KV4), cur),
            pl.BlockSpec((QB, KV4), nxt),
        ],
        out_specs=pl.BlockSpec((TQ, ATTN_WIDTH), cur),
        out_shape=jax.ShapeDtypeStruct((N_TOK, ATTN_WIDTH), BF16),
        scratch_shapes=[pltpu.VMEM((TQ + 2 * QB, KV4), BF16),
                        pltpu.VMEM((TQ + 2 * QB, KV4), BF16)],
        compiler_params=_cparams("parallel", "parallel"),
    )(sink, q, k, k, k, v, v, v)


def _outproj_kernel(xp_ref, xs_ref, yf_ref, ya_ref, wo_ref, g_ref, wr_ref, x1_ref, xn_ref, aff_ref):
    mix = jnp.dot(yf_ref[...], wo_ref[:FOURIER_WIDTH, :], preferred_element_type=F32)
    mix = mix + jnp.dot(ya_ref[...], wo_ref[FOURIER_WIDTH:, :], preferred_element_type=F32)
    x1 = _x_tile(xp_ref, xs_ref) + mix
    x1_ref[...] = x1
    ms = jnp.mean(x1 * x1, axis=-1, keepdims=True)
    xn = (x1 * lax.rsqrt(ms + EPS) * g_ref[...]).astype(BF16)
    xn_ref[...] = xn
    logits = lax.dot_general(wr_ref[...], xn, (((1,), (1,)), ((), ())), preferred_element_type=F32)
    mx = jnp.max(logits, axis=0, keepdims=True)
    ex = jnp.exp(logits - mx)
    aff_ref[...] = ex / jnp.sum(ex, axis=0, keepdims=True)


def _outproj(xp, xs, yf, ya, w_out, g, w_router_t):
    return pl.pallas_call(
        _outproj_kernel,
        grid=(N_TOK // TM,),
        in_specs=_x_specs() + [
            pl.BlockSpec((TM, FOURIER_WIDTH), lambda i: (i, 0)),
            pl.BlockSpec((TM, ATTN_WIDTH), lambda i: (i, 0)),
            pl.BlockSpec((D_MODEL, D_MODEL), lambda i: (0, 0)),
            pl.BlockSpec((1, D_MODEL), lambda i: (0, 0)),
            pl.BlockSpec((N_EXPERTS, D_MODEL), lambda i: (0, 0)),
        ],
        out_specs=[
            pl.BlockSpec((TM, D_MODEL), lambda i: (i, 0)),
            pl.BlockSpec((TM, D_MODEL), lambda i: (i, 0)),
            pl.BlockSpec((N_EXPERTS, TM), lambda i: (0, i)),
        ],
        out_shape=[
            jax.ShapeDtypeStruct((N_TOK, D_MODEL), F32),
            jax.ShapeDtypeStruct((N_TOK, D_MODEL), BF16),
            jax.ShapeDtypeStruct((N_EXPERTS, N_TOK), F32),
        ],
        compiler_params=_cparams("parallel"),
    )(xp, xs, yf, ya, w_out, g, w_router_t)


RB = 512
N_RB = N_TOK // RB
GROUP_TOKENS = (SEQ, 2 * SEQ)
GROUP0_BLOCKS = GROUP_TOKENS[0] // RB
CHUNK = 8
TILE_ROWS = 256
CHUNKS_PER_TILE = TILE_ROWS // CHUNK
SLOT_ROWS = EC_CAPACITY_FACTOR * N_TOK // N_EXPERTS
assert all((EC_CAPACITY_FACTOR * n // N_EXPERTS) % CHUNK == 0 for n in GROUP_TOKENS)
SLOT_TILES = SLOT_ROWS // FF_TM
assert SLOT_TILES * FF_TM == SLOT_ROWS
SLOT_CHUNKS = SLOT_ROWS // CHUNK
TILE_SLOTS = 4
LOOKAHEAD = TILE_SLOTS - 1
TRASH_CHUNK = N_EXPERTS * SLOT_CHUNKS
HEAD_TRASH_CHUNK = TRASH_CHUNK + TILE_SLOTS * CHUNKS_PER_TILE
TRASH_CHUNKS = TILE_SLOTS * CHUNKS_PER_TILE + FF_TM // CHUNK
assert N_EXPERTS <= FF_TM // CHUNK and TRASH_CHUNKS % (FF_TM // CHUNK) == 0
MAX_CHUNKS = (N_EXPERTS * (RB + 2 * CHUNK)) // CHUNK
CHUNK_SHAPE = (2 * CHUNK, D_MODEL // 2)


def _pack_chunks(rows_f32):
    n = rows_f32.shape[0] // CHUNK
    r3 = rows_f32.reshape(n, CHUNK, D_MODEL)
    half = D_MODEL // 2
    return jnp.concatenate([r3[:, :, :half], r3[:, :, half:]], axis=1).astype(BF16)


def _unpack_chunks(chunks_bf16):
    n = chunks_bf16.shape[0]
    c3 = chunks_bf16.astype(F32)
    left = c3[:, :CHUNK, :].reshape(n * CHUNK, D_MODEL // 2)
    right = c3[:, CHUNK:, :].reshape(n * CHUNK, D_MODEL // 2)
    return jnp.concatenate([left, right], axis=1).astype(BF16)
LIST_UNROLL = 4
LIST_LEN = ((MAX_CHUNKS + LIST_UNROLL + CHUNKS_PER_TILE - 1) // CHUNKS_PER_TILE) * CHUNKS_PER_TILE


def _thresh_kernel(aff_ref, thr_ref, need_ref, *, cap):
    aff = aff_ref[...]

    def body(i, cur):
        cand = cur | jnp.left_shift(jnp.int32(1), 30 - i)
        cnt = jnp.sum(jnp.where(aff >= pltpu.bitcast(cand, F32), 1.0, 0.0), axis=1, keepdims=True)
        return jnp.where(cnt >= cap, cand, cur)

    thr = pltpu.bitcast(lax.fori_loop(0, 31, body, jnp.zeros((N_EXPERTS, 1), jnp.int32)), F32)
    n_gt = jnp.sum(jnp.where(aff > thr, 1.0, 0.0), axis=1, keepdims=True)
    thr_ref[...] = jnp.broadcast_to(thr, thr_ref.shape)
    need_ref[...] = jnp.broadcast_to(cap - n_gt, need_ref.shape)


def _thresholds(aff_group):
    n = aff_group.shape[1]
    cap = EC_CAPACITY_FACTOR * n // N_EXPERTS
    return pl.pallas_call(
        functools.partial(_thresh_kernel, cap=float(cap)),
        out_shape=[jax.ShapeDtypeStruct((N_EXPERTS, LANES), F32),
                   jax.ShapeDtypeStruct((N_EXPERTS, LANES), F32)],
        compiler_params=pltpu.CompilerParams(vmem_limit_bytes=VMEM_LIMIT),
    )(aff_group)


def _select_kernel(aff_ref, thr_ref, need_ref, utri_ref, selw_ref, rank_ref, m_ref, eq_seen):
    rb = pl.program_id(0)

    @pl.when((rb == 0) | (rb == GROUP0_BLOCKS))
    def _():
        eq_seen[...] = jnp.zeros_like(eq_seen)

    aff = aff_ref[...]
    thr = thr_ref[:, 0:1]
    need = need_ref[:, 0:1]
    utri = utri_ref[...]
    eq = jnp.where(aff == thr, 1.0, 0.0)
    eq_before = jnp.dot(eq.astype(BF16), utri, preferred_element_type=F32) + eq_seen[...]
    sel = (aff > thr) | ((aff == thr) & (eq_before < need))
    eq_seen[...] += jnp.sum(eq, axis=1, keepdims=True)
    self = jnp.where(sel, 1.0, 0.0)
    rank_ref[...] = jnp.dot(self.astype(BF16), utri, preferred_element_type=F32)
    selw_ref[...] = jnp.where(sel, aff, 0.0)
    m_ref[0] = jnp.broadcast_to(jnp.sum(self, axis=1, keepdims=True), (N_EXPERTS, LANES))


def _select(aff_t, thr, need):
    utri = jnp.asarray(np.triu(np.ones((RB, RB), np.float32), 1), dtype=BF16)
    grp = lambda rb: (jnp.where(rb >= GROUP0_BLOCKS, 1, 0), 0)
    return pl.pallas_call(
        _select_kernel,
        grid=(N_RB,),
        in_specs=[
            pl.BlockSpec((N_EXPERTS, RB), lambda rb: (0, rb)),
            pl.BlockSpec((N_EXPERTS, LANES), grp),
            pl.BlockSpec((N_EXPERTS, LANES), grp),
            pl.BlockSpec((RB, RB), lambda rb: (0, 0)),
        ],
        out_specs=[
            pl.BlockSpec((N_EXPERTS, RB), lambda rb: (0, rb)),
            pl.BlockSpec((N_EXPERTS, RB), lambda rb: (0, rb)),
            pl.BlockSpec((1, N_EXPERTS, LANES), lambda rb: (rb, 0, 0)),
        ],
        out_shape=[
            jax.ShapeDtypeStruct((N_EXPERTS, N_TOK), F32),
            jax.ShapeDtypeStruct((N_EXPERTS, N_TOK), F32),
            jax.ShapeDtypeStruct((N_RB, N_EXPERTS, LANES), F32),
        ],
        scratch_shapes=[pltpu.VMEM((N_EXPERTS, 1), F32)],
        compiler_params=_cparams("arbitrary"),
    )(aff_t, thr, need, utri)


def _fill_chunk_list(list_ref, which, rb, seg_start_ref, seg_rows_ref):
    cnt = jnp.int32(0)
    for e in range(N_EXPERTS):
        base = e * SLOT_CHUNKS + seg_start_ref[rb, e] // CHUNK
        n = seg_rows_ref[rb, e] // CHUNK

        def body(k, carry, base=base, cnt=cnt):
            for u in range(LIST_UNROLL):
                list_ref[which, cnt + LIST_UNROLL * k + u] = base + LIST_UNROLL * k + u
            return carry

        lax.fori_loop(0, (n + LIST_UNROLL - 1) // LIST_UNROLL, body, 0)
        cnt = cnt + n
    return cnt


WIN_ROWS = 128
SPAN_TILES = 8
SPAN_ROWS = SPAN_TILES * TILE_ROWS


def _mark_span(scr, rb, s0, n_rows, tgt, values, seg_off_ref, seg_rows_ref):
    win_rows = lax.broadcasted_iota(jnp.int32, (WIN_ROWS, RB), 0).astype(F32)
    n_here = jnp.minimum(n_rows - s0, SPAN_ROWS)

    def zero(i, carry):
        scr[pl.ds(pl.multiple_of(i * TILE_ROWS, TILE_ROWS), TILE_ROWS), :] = jnp.zeros((TILE_ROWS, RB), F32)
        return carry

    lax.fori_loop(0, (n_here + TILE_ROWS - 1) // TILE_ROWS, zero, 0)
    for e in range(N_EXPERTS):
        lo = seg_off_ref[rb, e]
        first = jnp.maximum(lo, s0)
        last = jnp.minimum(lo + seg_rows_ref[rb, e], s0 + SPAN_ROWS)

        def mark(w, carry, e=e, first=first):
            start = first + w * WIN_ROWS
            win = pl.ds(pl.multiple_of(start - s0, CHUNK), WIN_ROWS)
            scr[win, :] = jnp.where(tgt[e:e + 1, :] == win_rows + start.astype(F32), values[e], scr[win, :])
            return carry

        lax.fori_loop(0, jnp.maximum(last - first + WIN_ROWS - 1, 0) // WIN_ROWS, mark, 0)


def _expert_column(ref, rb):
    e_iota = lax.broadcasted_iota(jnp.int32, (N_EXPERTS, 1), 0)
    col = jnp.zeros((N_EXPERTS, 1), F32)
    for e in range(N_EXPERTS):
        col = jnp.where(e_iota == e, ref[rb, e].astype(F32), col)
    return col


def _gather_kernel(in_start_ref, in_rows_ref, in_off_ref, tot_ref, in_rank_ref, seg_start_ref, seg_len_ref,
                   x_ref, selw_ref, rank_ref, xe_hbm,
                   p_scr, obuf, hbuf, carry, zbuf, dst_list, pending, sem, hsem, zsem):
    rb = pl.program_id(0)
    n_rows = tot_ref[rb]
    n_chunks = n_rows // CHUNK
    n_tiles = (n_rows + TILE_ROWS - 1) // TILE_ROWS
    head_slot = TILE_SLOTS

    @pl.when(rb == 0)
    def _():
        for slot in range(TILE_SLOTS + 1):
            pending[slot] = 0
        p_scr[...] = jnp.zeros_like(p_scr)
        carry[...] = jnp.zeros_like(carry)

    _fill_chunk_list(dst_list, 0, rb, in_start_ref, in_rows_ref)

    def tile_copy(slot):
        return pltpu.make_async_copy(obuf.at[slot], xe_hbm.at[pl.ds(0, CHUNKS_PER_TILE)], sem.at[slot])

    def head_copy():
        return pltpu.make_async_copy(hbuf, xe_hbm.at[pl.ds(0, N_EXPERTS)], hsem)

    x = x_ref[...]
    rank = rank_ref[...]
    sel = selw_ref[...] > 0.0

    j8 = lax.broadcasted_iota(jnp.int32, (CHUNK, RB), 0).astype(F32)
    heads, tails, flags = [], [], []
    for e in range(N_EXPERTS):
        start = seg_start_ref[rb, e]
        end = start + seg_len_ref[rb, e]
        nonempty = seg_len_ref[rb, e] > 0
        has_head = nonempty & (start % CHUNK != 0)
        has_tail = nonempty & (end % CHUNK != 0)
        single = has_head & has_tail & (start // CHUNK == end // CHUNK)
        head_shift = (start % CHUNK).astype(F32)
        tail_rank0 = ((end // CHUNK) * CHUNK - start).astype(F32)
        r_e = rank[e:e + 1, :]
        s_e = sel[e:e + 1, :]
        heads.append(jnp.where(s_e & (r_e + head_shift == j8), has_head.astype(F32), 0.0))
        tails.append(jnp.where(s_e & (r_e - tail_rank0 == j8), (has_tail & ~single).astype(F32), 0.0))
        flags.append((has_head, has_tail, single, start // CHUNK))
    p_edge = jnp.concatenate(heads + tails, axis=0).astype(BF16)
    edge = jnp.dot(p_edge, x, preferred_element_type=F32)

    @pl.when(pending[head_slot] == 1)
    def _():
        head_copy().wait()

    head_rows = []
    for e, (has_head, has_tail, single, head_chunk) in enumerate(flags):
        rows = slice(e * CHUNK, (e + 1) * CHUNK)
        old = carry[rows, :]
        head = edge[rows, :] + old
        tail = edge[N_EXPERTS * CHUNK + e * CHUNK:N_EXPERTS * CHUNK + (e + 1) * CHUNK, :]
        head_rows.append(head)
        keep = ~single & ~has_tail & ~has_head
        carry[rows, :] = (single.astype(F32) * head + (has_tail & ~single).astype(F32) * tail
                          + keep.astype(F32) * old)
    hbuf[...] = _pack_chunks(jnp.concatenate(head_rows, axis=0))
    for e, (has_head, has_tail, single, head_chunk) in enumerate(flags):
        dst = jnp.where(has_head & ~single, e * SLOT_CHUNKS + head_chunk, HEAD_TRASH_CHUNK + e)
        pltpu.make_async_copy(hbuf.at[e], xe_hbm.at[dst], hsem).start()
    pending[head_slot] = 1

    off = _expert_column(in_off_ref, rb)
    rank0 = _expert_column(in_rank_ref, rb)
    n_in = _expert_column(in_rows_ref, rb)
    tgt = jnp.where(sel & (rank >= rank0) & (rank < rank0 + n_in), rank - rank0 + off, -1.0)

    def tile_body(t, carry_):
        slot = t % TILE_SLOTS

        @pl.when(pending[slot] == 1)
        def _():
            tile_copy(slot).wait()

        @pl.when(t % SPAN_TILES == 0)
        def _():
            _mark_span(p_scr, rb, t * TILE_ROWS, n_rows, tgt, [1.0] * N_EXPERTS, in_off_ref, in_rows_ref)

        p = p_scr[pl.ds(pl.multiple_of((t % SPAN_TILES) * TILE_ROWS, TILE_ROWS), TILE_ROWS), :]
        obuf[slot] = _pack_chunks(jnp.dot(p.astype(BF16), x, preferred_element_type=F32))

        for i in range(CHUNKS_PER_TILE):
            g = t * CHUNKS_PER_TILE + i
            dst = jnp.where(g < n_chunks, dst_list[0, jnp.minimum(g, n_chunks - 1)],
                            TRASH_CHUNK + slot * CHUNKS_PER_TILE + i)
            pltpu.make_async_copy(obuf.at[slot, i], xe_hbm.at[dst], sem.at[slot]).start()
        pending[slot] = 1
        return carry_

    lax.fori_loop(0, n_tiles, tile_body, 0)

    @pl.when(rb == N_RB - 1)
    def _():
        for slot in range(TILE_SLOTS):
            @pl.when(pending[slot] == 1)
            def _(slot=slot):
                tile_copy(slot).wait()
        head_copy().wait()
        zbuf[...] = jnp.zeros_like(zbuf)

        def zcopy(k):
            return pltpu.make_async_copy(zbuf, xe_hbm.at[TRASH_CHUNK + k], zsem)

        def zstart(k, c):
            zcopy(k).start()
            return c

        def zwait(k, c):
            zcopy(k).wait()
            return c

        lax.fori_loop(0, TRASH_CHUNKS, zstart, 0)
        lax.fori_loop(0, TRASH_CHUNKS, zwait, 0)


def _route_gather(tables, xn, selw_t, rank_t):
    gs = pltpu.PrefetchScalarGridSpec(
        num_scalar_prefetch=len(tables),
        grid=(N_RB,),
        in_specs=[
            pl.BlockSpec((RB, D_MODEL), lambda rb, *_: (rb, 0)),
            pl.BlockSpec((N_EXPERTS, RB), lambda rb, *_: (0, rb)),
            pl.BlockSpec((N_EXPERTS, RB), lambda rb, *_: (0, rb)),
        ],
        out_specs=pl.BlockSpec(memory_space=pl.ANY),
        scratch_shapes=[
            pltpu.VMEM((SPAN_ROWS + WIN_ROWS, RB), F32),
            pltpu.VMEM((TILE_SLOTS, CHUNKS_PER_TILE) + CHUNK_SHAPE, BF16),
            pltpu.VMEM((N_EXPERTS,) + CHUNK_SHAPE, BF16),
            pltpu.VMEM((N_EXPERTS * CHUNK, D_MODEL), F32),
            pltpu.VMEM(CHUNK_SHAPE, BF16),
            pltpu.SMEM((1, LIST_LEN), jnp.int32),
            pltpu.SMEM((TILE_SLOTS + 1,), jnp.int32),
            pltpu.SemaphoreType.DMA((TILE_SLOTS,)),
            pltpu.SemaphoreType.DMA,
            pltpu.SemaphoreType.DMA,
        ],
    )
    return pl.pallas_call(
        _gather_kernel,
        grid_spec=gs,
        out_shape=jax.ShapeDtypeStruct((TRASH_CHUNK + TRASH_CHUNKS,) + CHUNK_SHAPE, BF16),
        compiler_params=_cparams("arbitrary"),
    )(*tables, xn, selw_t, rank_t)


W_PIECES_PER_MATRIX = 4
W_PIECES = 3 * W_PIECES_PER_MATRIX
FFN_STEPS = max(SLOT_TILES, W_PIECES + 1)
WG_PIECE_ROWS = D_MODEL // W_PIECES_PER_MATRIX
WD_PIECE_ROWS = EXPERT_D_FF // W_PIECES_PER_MATRIX
FFN_VMEM_LIMIT = 56 * 1024 * 1024


def _ffn_kernel(x_ref, wg_hbm, wu_hbm, wd_hbm, o_ref,
                wg_bf, wu_bf, wd_bf, stage_up, stage_dn, sem):
    e = pl.program_id(0)
    i = pl.program_id(1)
    slot = e % 2

    matrices = ((wg_hbm, stage_up, wg_bf, WG_PIECE_ROWS),
                (wu_hbm, stage_up, wu_bf, WG_PIECE_ROWS),
                (wd_hbm, stage_dn, wd_bf, WD_PIECE_ROWS))

    def piece(expert, k, buf, finish):
        def of_kind(kind):
            w_hbm, stage, w_bf, rows = matrices[kind]
            r0 = pl.multiple_of((k - kind * W_PIECES_PER_MATRIX) * rows, rows)
            cp = pltpu.make_async_copy(w_hbm.at[expert, pl.ds(r0, rows), :], stage.at[buf], sem.at[buf])
            if finish:
                cp.wait()
                w_bf[expert % 2, pl.ds(r0, rows), :] = stage[buf].astype(BF16)
            else:
                cp.start()

        if isinstance(k, int):
            of_kind(k // W_PIECES_PER_MATRIX)
        else:
            for kind in range(len(matrices)):
                pl.when(k // W_PIECES_PER_MATRIX == kind)(functools.partial(of_kind, kind))

    @pl.when((e == 0) & (i == 0))
    def _():
        for k in range(W_PIECES):
            piece(0, k, k % 2, finish=False)
            piece(0, k, k % 2, finish=True)

    @pl.when(e + 1 < N_EXPERTS)
    def _():
        @pl.when((i >= 1) & (i <= W_PIECES))
        def _():
            piece(e + 1, i - 1, (i - 1) % 2, finish=True)

        @pl.when(i < W_PIECES)
        def _():
            piece(e + 1, i, i % 2, finish=False)

    @pl.when(i < SLOT_TILES)
    def _():
        x = _unpack_chunks(x_ref[...])
        acc = jnp.zeros((FF_TM, D_MODEL), F32)
        for c in range(EXPERT_D_FF // FF_FC):
            gt = jnp.dot(x, wg_bf[slot, :, c * FF_FC:(c + 1) * FF_FC], preferred_element_type=F32)
            up = jnp.dot(x, wu_bf[slot, :, c * FF_FC:(c + 1) * FF_FC], preferred_element_type=F32)
            h = (gt * (1.0 / (1.0 + jnp.exp(-gt))) * up).astype(BF16)
            acc = acc + jnp.dot(h, wd_bf[slot, c * FF_FC:(c + 1) * FF_FC, :], preferred_element_type=F32)
        o_ref[...] = _pack_chunks(acc)


def _ffn(xe, w_gate, w_up, w_down):
    def rows(e, i):
        return (e * SLOT_TILES + jnp.minimum(i, SLOT_TILES - 1), 0, 0)

    tile_chunks = (FF_TM // CHUNK,) + CHUNK_SHAPE
    gs = pltpu.PrefetchScalarGridSpec(
        num_scalar_prefetch=0,
        grid=(N_EXPERTS, FFN_STEPS),
        in_specs=[
            pl.BlockSpec(tile_chunks, rows),
            pl.BlockSpec(memory_space=pl.ANY),
            pl.BlockSpec(memory_space=pl.ANY),
            pl.BlockSpec(memory_space=pl.ANY),
        ],
        out_specs=pl.BlockSpec(tile_chunks, rows),
        scratch_shapes=[
            pltpu.VMEM((2, D_MODEL, EXPERT_D_FF), BF16),
            pltpu.VMEM((2, D_MODEL, EXPERT_D_FF), BF16),
            pltpu.VMEM((2, EXPERT_D_FF, D_MODEL), BF16),
            pltpu.VMEM((2, WG_PIECE_ROWS, EXPERT_D_FF), F32),
            pltpu.VMEM((2, WD_PIECE_ROWS, D_MODEL), F32),
            pltpu.SemaphoreType.DMA((2,)),
        ],
    )
    return pl.pallas_call(
        _ffn_kernel,
        grid_spec=gs,
        out_shape=jax.ShapeDtypeStruct((N_EXPERTS * SLOT_CHUNKS,) + CHUNK_SHAPE, BF16),
        compiler_params=pltpu.CompilerParams(dimension_semantics=("arbitrary", "arbitrary"),
                                             vmem_limit_bytes=FFN_VMEM_LIMIT),
    )(xe, w_gate, w_up, w_down)


def _combine_kernel(seg_start_ref, seg_rows_ref, seg_off_ref, tot_ref, seg_shift_ref,
                    x1_ref, selw_ref, rank_ref, g_ref, ye_hbm, op_ref, os_ref,
                    w_scr, ybuf, acc, src_list, state, sem):
    rb = pl.program_id(0)
    cur = rb % 2
    nxt_rb = jnp.minimum(rb + 1, N_RB - 1)

    def tiles_of(b):
        return (tot_ref[b] + TILE_ROWS - 1) // TILE_ROWS

    n_tiles = tiles_of(rb)
    n_tiles_next = jnp.where(rb + 1 < N_RB, tiles_of(nxt_rb), 0)

    def fetch(b, which, t, slot):
        n_chunks = tot_ref[b] // CHUNK
        for i in range(CHUNKS_PER_TILE):
            g = t * CHUNKS_PER_TILE + i
            src = jnp.where(g < n_chunks, src_list[which, jnp.minimum(g, n_chunks - 1)], 0)
            pltpu.make_async_copy(ye_hbm.at[src], ybuf.at[slot, i], sem.at[slot]).start()

    def tile_wait(slot):
        pltpu.make_async_copy(ye_hbm.at[pl.ds(0, CHUNKS_PER_TILE)], ybuf.at[slot], sem.at[slot]).wait()

    @pl.when(rb == 0)
    def _():
        w_scr[...] = jnp.zeros_like(w_scr)
        _fill_chunk_list(src_list, 0, 0, seg_start_ref, seg_rows_ref)
        state[0] = 0
        state[1] = 0

    @pl.when(rb + 1 < N_RB)
    def _():
        _fill_chunk_list(src_list, 1 - cur, nxt_rb, seg_start_ref, seg_rows_ref)

    slot0 = state[0]
    started = state[1]
    for j in range(LOOKAHEAD):
        @pl.when((j >= started) & (j < n_tiles))
        def _(j=j):
            fetch(rb, cur, j, (slot0 + j) % TILE_SLOTS)

    off = _expert_column(seg_off_ref, rb) + _expert_column(seg_shift_ref, rb)
    selw = selw_ref[...]
    gates = [selw[e:e + 1, :] for e in range(N_EXPERTS)]
    tgt = jnp.where(selw > 0.0, rank_ref[...] + off, -1.0)
    acc[...] = x1_ref[...]

    def tile_body(t, next_started):
        slot = (slot0 + t) % TILE_SLOTS
        ahead = t + LOOKAHEAD

        @pl.when(ahead < n_tiles)
        def _():
            fetch(rb, cur, ahead, (slot0 + ahead) % TILE_SLOTS)

        start_next = (ahead >= n_tiles) & (next_started < jnp.minimum(LOOKAHEAD, n_tiles_next))

        @pl.when(start_next)
        def _():
            fetch(nxt_rb, 1 - cur, next_started, (slot0 + n_tiles + next_started) % TILE_SLOTS)

        @pl.when(t % SPAN_TILES == 0)
        def _():
            _mark_span(w_scr, rb, t * TILE_ROWS, tot_ref[rb], tgt, gates, seg_off_ref, seg_rows_ref)

        tile_wait(slot)
        w = w_scr[pl.ds(pl.multiple_of((t % SPAN_TILES) * TILE_ROWS, TILE_ROWS), TILE_ROWS), :]
        acc[...] += lax.dot_general(w.astype(BF16), _unpack_chunks(ybuf[slot]),
                                    (((0,), (0,)), ((), ())), preferred_element_type=F32)
        return next_started + start_next.astype(jnp.int32)

    state[1] = lax.fori_loop(0, n_tiles, tile_body, jnp.int32(0))
    state[0] = (slot0 + n_tiles) % TILE_SLOTS
    x2 = acc[...]
    ms = jnp.mean(x2 * x2, axis=-1, keepdims=True)
    y = x2 * lax.rsqrt(ms + EPS) * g_ref[...]

    @pl.when(rb < GROUP0_BLOCKS)
    def _():
        op_ref[...] = y

    @pl.when(rb >= GROUP0_BLOCKS)
    def _():
        os_ref[...] = y


def _combine(tables, x1, selw_t, rank_t, g, ye):
    gs = pltpu.PrefetchScalarGridSpec(
        num_scalar_prefetch=len(tables),
        grid=(N_RB,),
        in_specs=[
            pl.BlockSpec((RB, D_MODEL), lambda rb, *_: (rb, 0)),
            pl.BlockSpec((N_EXPERTS, RB), lambda rb, *_: (0, rb)),
            pl.BlockSpec((N_EXPERTS, RB), lambda rb, *_: (0, rb)),
            pl.BlockSpec((1, D_MODEL), lambda rb, *_: (0, 0)),
            pl.BlockSpec(memory_space=pl.ANY),
        ],
        out_specs=[
            pl.BlockSpec((RB, D_MODEL), lambda rb, *_: (jnp.minimum(rb, GROUP0_BLOCKS - 1), 0)),
            pl.BlockSpec((RB, D_MODEL), lambda rb, *_: (jnp.maximum(rb - GROUP0_BLOCKS, 0), 0)),
        ],
        scratch_shapes=[
            pltpu.VMEM((SPAN_ROWS + WIN_ROWS, RB), F32),
            pltpu.VMEM((TILE_SLOTS, CHUNKS_PER_TILE) + CHUNK_SHAPE, BF16),
            pltpu.VMEM((RB, D_MODEL), F32),
            pltpu.SMEM((2, LIST_LEN), jnp.int32),
            pltpu.SMEM((2,), jnp.int32),
            pltpu.SemaphoreType.DMA((TILE_SLOTS,)),
        ],
    )
    return pl.pallas_call(
        _combine_kernel,
        grid_spec=gs,
        out_shape=[jax.ShapeDtypeStruct((GROUP_TOKENS[0], D_MODEL), F32),
                   jax.ShapeDtypeStruct((GROUP_TOKENS[1], D_MODEL), F32)],
        compiler_params=_cparams("arbitrary"),
    )(*tables, x1, selw_t, rank_t, g, ye)


def kernel(x_prompt, x_sample, norm_mix_g, w_in, w_fourier, sink, w_out, norm_ffn_g, w_router,
           w_gate, w_up, w_down, norm_final_g):
    assert x_prompt.shape == (1, SEQ, D_MODEL) and x_sample.shape == (N_SEQ - 1, SEQ, D_MODEL)
    assert w_in.shape == (1, D_MODEL, IN_WIDTH) and w_out.shape == (1, MIX_WIDTH, D_MODEL)
    assert w_gate.shape == (1, N_EXPERTS, D_MODEL, EXPERT_D_FF) and w_down.shape == (1, N_EXPERTS, EXPERT_D_FF, D_MODEL)
    xp = x_prompt.reshape(-1, D_MODEL)
    xs = x_sample.reshape(-1, D_MODEL)
    cos_t, sin_t = _rope_tables()

    mix_cols = np.concatenate([
        np.arange(FOURIER_WIDTH),
        FOURIER_WIDTH + (np.asarray(Q_ORDER)[:, None] * HEAD_DIM + np.arange(HEAD_DIM)[None, :]).reshape(-1)])
    in_cols = np.concatenate([mix_cols, np.arange(MIX_WIDTH, IN_WIDTH)])
    w_in_r = w_in[0][:, in_cols].astype(BF16)
    w_out_r = w_out[0][mix_cols, :].astype(BF16)

    u, q, k, v = _inproj(xp, xs, norm_mix_g[0][None, :], w_in_r, cos_t, sin_t)
    yf = _fourier(u, w_fourier[0].astype(BF16))
    ya = _attention(sink[0], q, k, v)
    x1, xn, aff_t = _outproj(xp, xs, yf, ya, w_out_r, norm_ffn_g[0][None, :],
                             w_router[0].T.astype(BF16))

    thr0, need0 = _thresholds(aff_t[:, :GROUP_TOKENS[0]])
    thr1, need1 = _thresholds(aff_t[:, GROUP_TOKENS[0]:])
    selw_t, rank_t, m = _select(aff_t, jnp.concatenate([thr0, thr1]), jnp.concatenate([need0, need1]))

    seg_len = m[:, :, 0].astype(jnp.int32)
    seg_start = jnp.cumsum(seg_len, axis=0) - seg_len
    seg_end = seg_start + seg_len

    def offsets(rows):
        return jnp.cumsum(rows, axis=1) - rows, jnp.sum(rows, axis=1)

    in_start = -(-seg_start // CHUNK) * CHUNK
    in_rows = jnp.maximum(seg_end // CHUNK * CHUNK - in_start, 0)
    in_off, in_tot = offsets(in_rows)
    gather_tables = (in_start, in_rows, in_off, in_tot, in_start - seg_start, seg_start, seg_len)
    out_start = seg_start // CHUNK * CHUNK
    out_rows = jnp.where(seg_len > 0, -(-seg_end // CHUNK) * CHUNK - out_start, 0)
    out_off, out_tot = offsets(out_rows)
    combine_tables = (out_start, out_rows, out_off, out_tot, seg_start - out_start)

    xe = _route_gather(gather_tables, xn, selw_t, rank_t)
    ye = _ffn(xe, w_gate[0], w_up[0], w_down[0])
    yp, ys = _combine(combine_tables, x1, selw_t, rank_t, norm_final_g[None, :], ye)
    return (yp.reshape(x_prompt.shape), ys.reshape(x_sample.shape))
```

```python
import functools

import numpy as np
import jax
import jax.numpy as jnp
from jax import lax
from jax.experimental import pallas as pl
from jax.experimental.pallas import tpu as pltpu

F32 = jnp.float32
BF16 = jnp.bfloat16

D_MODEL = 1024
SEQ = 16384
N_SEQ = 3
N_TOK = N_SEQ * SEQ
HEAD_DIM = 64
N_HEADS = 12
N_KV_HEADS = 4
GQA_GROUP = N_HEADS // N_KV_HEADS
ATTN_WIDTH = N_HEADS * HEAD_DIM
KV_WIDTH = N_KV_HEADS * HEAD_DIM
FOURIER_WIDTH = 256
FOURIER_GROUP_DIM = 64
MIX_WIDTH = FOURIER_WIDTH + ATTN_WIDTH
IN_WIDTH = MIX_WIDTH + 2 * KV_WIDTH
WINDOW = 128
ROPE_THETA = 10000.0
N_EXPERTS = 16
EC_CAPACITY_FACTOR = 2
EXPERT_D_FF = 2048
EPS = 1e-6
NEG = -1e30

LANES = 128
FFT_R = 128
VMEM_LIMIT = 48 * 1024 * 1024

TM = 1024
TQ = 1024
QB = 128
FF_TM = 512
FF_FC = 512


def _cparams(*sem):
    return pltpu.CompilerParams(dimension_semantics=sem, vmem_limit_bytes=VMEM_LIMIT)


def _x_specs():
    n_p = SEQ // TM
    return [pl.BlockSpec((TM, D_MODEL), lambda i, *_: (jnp.minimum(i, n_p - 1), 0)),
            pl.BlockSpec((TM, D_MODEL), lambda i, *_: (jnp.maximum(i - n_p, 0), 0))]


def _x_tile(xp_ref, xs_ref):
    return jnp.where(pl.program_id(0) < SEQ // TM, xp_ref[...], xs_ref[...])


def _inproj_kernel(xp_ref, xs_ref, g_ref, w_ref, cos_ref, sin_ref, u_ref, q_ref, k_ref, v_ref):
    x = _x_tile(xp_ref, xs_ref)
    ms = jnp.mean(x * x, axis=-1, keepdims=True)
    hn = (x * lax.rsqrt(ms + EPS) * g_ref[...]).astype(BF16)
    proj = jnp.dot(hn, w_ref[...], preferred_element_type=F32)
    u_ref[...] = proj[:, :FOURIER_WIDTH].astype(BF16)
    cos = cos_ref[...]
    sin = sin_ref[...]
    lane = lax.broadcasted_iota(jnp.int32, cos.shape, 1)
    first_half = (lane % HEAD_DIM) < (HEAD_DIM // 2)

    def rope(t):
        partner = jnp.where(first_half,
                            pltpu.roll(t, LANES - HEAD_DIM // 2, 1),
                            pltpu.roll(t, HEAD_DIM // 2, 1))
        return t * cos + partner * sin

    scale = HEAD_DIM ** -0.5 * LOG2E
    for c in range(ATTN_WIDTH // LANES):
        lo = FOURIER_WIDTH + c * LANES
        q_ref[:, c * LANES:(c + 1) * LANES] = (rope(proj[:, lo:lo + LANES]) * scale).astype(BF16)
    low_half = lane < HEAD_DIM
    for c in range(KV_WIDTH // LANES):
        lo = FOURIER_WIDTH + ATTN_WIDTH + c * LANES
        kc = rope(proj[:, lo:lo + LANES])
        vc = proj[:, lo + KV_WIDTH:lo + KV_WIDTH + LANES]
        for half, keep in enumerate((low_half, ~low_half)):
            g = 2 * c + half
            k_ref[:, g * LANES:(g + 1) * LANES] = jnp.where(keep, kc, 0.0).astype(BF16)
            v_ref[:, g * LANES:(g + 1) * LANES] = jnp.where(keep, vc, 0.0).astype(BF16)


def _inproj(xp, xs, g, w_in, cos_t, sin_t):
    n_pos_blocks = SEQ // TM
    return pl.pallas_call(
        _inproj_kernel,
        grid=(N_TOK // TM,),
        in_specs=_x_specs() + [
            pl.BlockSpec((1, D_MODEL), lambda i: (0, 0)),
            pl.BlockSpec((D_MODEL, IN_WIDTH), lambda i: (0, 0)),
            pl.BlockSpec((TM, LANES), lambda i: (i % n_pos_blocks, 0)),
            pl.BlockSpec((TM, LANES), lambda i: (i % n_pos_blocks, 0)),
        ],
        out_specs=[
            pl.BlockSpec((TM, FOURIER_WIDTH), lambda i: (i, 0)),
            pl.BlockSpec((TM, ATTN_WIDTH), lambda i: (i, 0)),
            pl.BlockSpec((TM, KV4), lambda i: (i, 0)),
            pl.BlockSpec((TM, KV4), lambda i: (i, 0)),
        ],
        out_shape=[
            jax.ShapeDtypeStruct((N_TOK, FOURIER_WIDTH), BF16),
            jax.ShapeDtypeStruct((N_TOK, ATTN_WIDTH), BF16),
            jax.ShapeDtypeStruct((N_TOK, KV4), BF16),
            jax.ShapeDtypeStruct((N_TOK, KV4), BF16),
        ],
        compiler_params=_cparams("parallel"),
    )(xp, xs, g, w_in, cos_t, sin_t)


def _rope_tables():
    half = HEAD_DIM // 2
    inv_freq = 1.0 / (ROPE_THETA ** (jnp.arange(half, dtype=F32) / half))
    ang = jnp.arange(SEQ, dtype=F32)[:, None] * inv_freq[None, :]
    cos = jnp.cos(ang)
    sin = jnp.sin(ang)
    reps = LANES // HEAD_DIM
    cos_t = jnp.tile(jnp.concatenate([cos, cos], axis=-1), (1, reps))
    sin_t = jnp.tile(jnp.concatenate([-sin, sin], axis=-1), (1, reps))
    return cos_t, sin_t


def _dft_tables():
    r = FFT_R
    c = np.arange(FOURIER_GROUP_DIM)
    ang_c = 2.0 * np.pi * np.outer(c, c) / FOURIER_GROUP_DIM
    n_groups = FOURIER_WIDTH // FOURIER_GROUP_DIM
    eye = np.eye(n_groups)
    ch_scale = FOURIER_GROUP_DIM ** -0.5
    c_blk = np.kron(eye, np.cos(ang_c)) * ch_scale
    s_blk = np.kron(eye, np.sin(ang_c)) * ch_scale
    cs = np.concatenate([c_blk, -s_blk], axis=1)
    k = np.arange(r)
    ang_r = 2.0 * np.pi * np.outer(k, k) / r
    st_scale = r ** -0.5
    wr = np.cos(ang_r) * st_scale
    wi = -np.sin(ang_r) * st_scale
    wbig = np.block([[wr, -wi], [wi, wr]])
    ang_t = 2.0 * np.pi * np.outer(k, k) / (r * r)
    tr = np.cos(ang_t)
    ti = -np.sin(ang_t)
    f = lambda a: jnp.asarray(a, dtype=F32)
    return f(cs).astype(BF16), f(wbig).astype(BF16), f(wr), f(wi), f(tr), f(ti)


FS_SB = 8
FS_KB = 8


def _fft_stage1_kernel(u_ref, cs_ref, wbig_ref, y_ref):
    cs = cs_ref[...]
    wbig = wbig_ref[...]
    fw = FOURIER_WIDTH
    for j in range(FS_SB):
        u = u_ref[:, j * fw:(j + 1) * fw]
        g = jnp.dot(u, cs, preferred_element_type=F32)
        gst = jnp.concatenate([g[:, :fw], g[:, fw:]], axis=0).astype(BF16)
        y = jnp.dot(wbig, gst, preferred_element_type=F32)
        y_ref[:, j * 2 * fw:j * 2 * fw + fw] = y[:FFT_R].astype(BF16)
        y_ref[:, j * 2 * fw + fw:(j + 1) * 2 * fw] = y[FFT_R:].astype(BF16)


def _fft_stage2_kernel(y_ref, wr_ref, wi_ref, tr_ref, ti_ref, wf_ref, o_ref):
    kb = pl.program_id(1)
    wr = wr_ref[...]
    wi = wi_ref[...]
    wf = wf_ref[...]
    fw = FOURIER_WIDTH
    for j in range(FS_KB):
        k1 = kb * FS_KB + j
        tr = tr_ref[pl.ds(k1, 1), :]
        ti = ti_ref[pl.ds(k1, 1), :]
        mr = wr * tr - wi * ti
        mi = wr * ti + wi * tr
        m2 = jnp.concatenate([mr, -mi], axis=1).astype(BF16)
        yb = y_ref[j * FFT_R:(j + 1) * FFT_R, :]
        yst = jnp.concatenate([yb[:, :fw], yb[:, fw:]], axis=0)
        z = jnp.dot(m2, yst, preferred_element_type=F32)
        yf = jnp.dot(z.astype(BF16), wf, preferred_element_type=F32)
        o_ref[:, j * fw:(j + 1) * fw] = yf.astype(BF16)


def _fourier(u, w_fourier):
    cs, wbig, wr, wi, tr, ti = _dft_tables()
    r, fw = FFT_R, FOURIER_WIDTH
    u2 = u.reshape(N_SEQ * r, r * fw)
    y = pl.pallas_call(
        _fft_stage1_kernel,
        grid=(N_SEQ, r // FS_SB),
        in_specs=[
            pl.BlockSpec((r, FS_SB * fw), lambda s, j: (s, j)),
            pl.BlockSpec((fw, 2 * fw), lambda s, j: (0, 0)),
            pl.BlockSpec((2 * r, 2 * r), lambda s, j: (0, 0)),
        ],
        out_specs=pl.BlockSpec((r, FS_SB * 2 * fw), lambda s, j: (s, j)),
        out_shape=jax.ShapeDtypeStruct((N_SEQ * r, r * 2 * fw), BF16),
        compiler_params=_cparams("parallel", "parallel"),
    )(u2, cs, wbig)
    y2 = y.reshape(N_SEQ * r * r, 2 * fw)
    full = lambda s, j: (0, 0)
    yf = pl.pallas_call(
        _fft_stage2_kernel,
        grid=(N_SEQ, r // FS_KB),
        in_specs=[
            pl.BlockSpec((FS_KB * r, 2 * fw), lambda s, j: (s * (r // FS_KB) + j, 0)),
            pl.BlockSpec((r, r), full),
            pl.BlockSpec((r, r), full),
            pl.BlockSpec((r, r), full),
            pl.BlockSpec((r, r), full),
            pl.BlockSpec((fw, fw), full),
        ],
        out_specs=pl.BlockSpec((r, FS_KB * fw), lambda s, j: (s, j)),
        out_shape=jax.ShapeDtypeStruct((N_SEQ * r, r * fw), BF16),
        compiler_params=_cparams("parallel", "parallel"),
    )(y2, wr, wi, tr, ti, w_fourier)
    return yf.reshape(N_TOK, fw)


LO_HEADS = (0, 1, 2, 6, 7, 8)
HI_HEADS = (3, 4, 5, 9, 10, 11)
Q_ORDER = tuple(h for pair in zip(LO_HEADS, HI_HEADS) for h in pair)
assert all(h // GQA_GROUP % 2 == 0 for h in LO_HEADS) and all(h // GQA_GROUP % 2 == 1 for h in HI_HEADS)
KV4 = 2 * KV_WIDTH
LOG2E = 1.4426950408889634


def _attn_kernel(sink_ref, q_ref, kp_ref, kc_ref, kn_ref, vp_ref, vc_ref, vn_ref, o_ref,
                 kext, vext):
    i = pl.program_id(1)
    n_i = pl.num_programs(1)
    kext[0:QB, :] = kp_ref[...]
    kext[QB:QB + TQ, :] = kc_ref[...]
    kext[QB + TQ:, :] = kn_ref[...]
    vext[0:QB, :] = vp_ref[...]
    vext[QB:QB + TQ, :] = vc_ref[...]
    vext[QB + TQ:, :] = vn_ref[...]

    rows = GQA_GROUP * QB
    a = lax.broadcasted_iota(jnp.int32, (rows, QB), 0) % QB
    j = lax.broadcasted_iota(jnp.int32, (rows, QB), 1)
    band_prev = jnp.where(j >= a, 0.0, NEG)
    band_next = jnp.where(j <= a, 0.0, NEG)
    row_id = lax.broadcasted_iota(jnp.int32, (rows, 1), 0)
    low_half = lax.broadcasted_iota(jnp.int32, (rows, LANES), 1) < HEAD_DIM

    def softmax_half(s, sinks, bias_prev, bias_next):
        s_p = s[:, :QB] + bias_prev
        s_c = s[:, QB:2 * QB]
        s_n = s[:, 2 * QB:] + bias_next
        sink = jnp.where(row_id < QB, sinks[0], jnp.where(row_id < 2 * QB, sinks[1], sinks[2])) * LOG2E
        m = jnp.max(jnp.maximum(jnp.maximum(s_p, s_c), s_n), axis=-1, keepdims=True)
        m = jnp.maximum(m, sink)
        p_p = jnp.exp2(s_p - m)
        p_c = jnp.exp2(s_c - m)
        p_n = jnp.exp2(s_n - m)
        denom = jnp.sum(p_p + p_c + p_n, axis=-1, keepdims=True) + jnp.exp2(sink - m)
        return [p_p.astype(BF16), p_c.astype(BF16), p_n.astype(BF16)], 1.0 / denom

    for sb in range(TQ // QB):
        blk = i * (TQ // QB) + sb
        bias_prev = band_prev + jnp.where(blk == 0, NEG, 0.0)
        bias_next = band_next + jnp.where(blk == n_i * (TQ // QB) - 1, NEG, 0.0)
        for pr in range(N_KV_HEADS // 2):
            groups = [GQA_GROUP * pr + m for m in range(GQA_GROUP)]
            q3 = jnp.concatenate([q_ref[sb * QB:(sb + 1) * QB, g * LANES:(g + 1) * LANES] for g in groups],
                                 axis=0)
            kp = kext[sb * QB:sb * QB + 3 * QB, pr * 2 * LANES:(pr + 1) * 2 * LANES]
            vp = vext[sb * QB:sb * QB + 3 * QB, pr * 2 * LANES:(pr + 1) * 2 * LANES]
            kcat = jnp.concatenate([kp[:, :LANES], kp[:, LANES:]], axis=0)
            vcat = jnp.concatenate([vp[:, :LANES], vp[:, LANES:]], axis=0)
            s = lax.dot_general(q3, kcat, (((1,), (1,)), ((), ())), preferred_element_type=F32)
            p_lo, r_lo = softmax_half(s[:, :3 * QB], [sink_ref[LO_HEADS[g]] for g in groups],
                                      bias_prev, bias_next)
            p_hi, r_hi = softmax_half(s[:, 3 * QB:], [sink_ref[HI_HEADS[g]] for g in groups],
                                      bias_prev, bias_next)
            p = jnp.concatenate(p_lo + p_hi, axis=1)
            o = jnp.dot(p, vcat, preferred_element_type=F32) * jnp.where(low_half, r_lo, r_hi)
            for m, g in enumerate(groups):
                o_ref[sb * QB:(sb + 1) * QB, g * LANES:(g + 1) * LANES] = o[m * QB:(m + 1) * QB].astype(BF16)


def _attention(sink, q, k, v):
    nqb = SEQ // QB
    per = TQ // QB
    cur = lambda s, i: (s * (SEQ // TQ) + i, 0)
    prev = lambda s, i: (s * nqb + jnp.maximum(i * per - 1, 0), 0)
    nxt = lambda s, i: (s * nqb + jnp.minimum(i * per + per, nqb - 1), 0)
    return pl.pallas_call(
        _attn_kernel,
        grid=(N_SEQ, SEQ // TQ),
        in_specs=[
            pl.BlockSpec(memory_space=pltpu.SMEM),
            pl.BlockSpec((TQ, ATTN_WIDTH), cur),
            pl.BlockSpec((QB, KV4), prev),
            pl.BlockSpec((TQ, KV4), cur),
            pl.BlockSpec((QB, KV4), nxt),
            pl.BlockSpec((QB, KV4), prev),
            pl.BlockSpec((TQ, KV4), cur),
            pl.BlockSpec((QB, KV4), nxt),
        ],
        out_specs=pl.BlockSpec((TQ, ATTN_WIDTH), cur),
        out_shape=jax.ShapeDtypeStruct((N_TOK, ATTN_WIDTH), BF16),
        scratch_shapes=[pltpu.VMEM((TQ + 2 * QB, KV4), BF16),
                        pltpu.VMEM((TQ + 2 * QB, KV4), BF16)],
        compiler_params=_cparams("parallel", "parallel"),
    )(sink, q, k, k, k, v, v, v)


def _outproj_kernel(xp_ref, xs_ref, yf_ref, ya_ref, wo_ref, g_ref, wr_ref, x1_ref, xn_ref, aff_ref):
    mix = jnp.dot(yf_ref[...], wo_ref[:FOURIER_WIDTH, :], preferred_element_type=F32)
    mix = mix + jnp.dot(ya_ref[...], wo_ref[FOURIER_WIDTH:, :], preferred_element_type=F32)
    x1 = _x_tile(xp_ref, xs_ref) + mix
    x1_ref[...] = x1
    ms = jnp.mean(x1 * x1, axis=-1, keepdims=True)
    xn = (x1 * lax.rsqrt(ms + EPS) * g_ref[...]).astype(BF16)
    xn_ref[...] = xn
    logits = lax.dot_general(wr_ref[...], xn, (((1,), (1,)), ((), ())), preferred_element_type=F32)
    mx = jnp.max(logits, axis=0, keepdims=True)
    ex = jnp.exp(logits - mx)
    aff_ref[...] = ex / jnp.sum(ex, axis=0, keepdims=True)


def _outproj(xp, xs, yf, ya, w_out, g, w_router_t):
    return pl.pallas_call(
        _outproj_kernel,
        grid=(N_TOK // TM,),
        in_specs=_x_specs() + [
            pl.BlockSpec((TM, FOURIER_WIDTH), lambda i: (i, 0)),
            pl.BlockSpec((TM, ATTN_WIDTH), lambda i: (i, 0)),
            pl.BlockSpec((D_MODEL, D_MODEL), lambda i: (0, 0)),
            pl.BlockSpec((1, D_MODEL), lambda i: (0, 0)),
            pl.BlockSpec((N_EXPERTS, D_MODEL), lambda i: (0, 0)),
        ],
        out_specs=[
            pl.BlockSpec((TM, D_MODEL), lambda i: (i, 0)),
            pl.BlockSpec((TM, D_MODEL), lambda i: (i, 0)),
            pl.BlockSpec((N_EXPERTS, TM), lambda i: (0, i)),
        ],
        out_shape=[
            jax.ShapeDtypeStruct((N_TOK, D_MODEL), F32),
            jax.ShapeDtypeStruct((N_TOK, D_MODEL), BF16),
            jax.ShapeDtypeStruct((N_EXPERTS, N_TOK), F32),
        ],
        compiler_params=_cparams("parallel"),
    )(xp, xs, yf, ya, w_out, g, w_router_t)


RB = 512
N_RB = N_TOK // RB
GROUP_TOKENS = (SEQ, 2 * SEQ)
GROUP0_BLOCKS = GROUP_TOKENS[0] // RB
CHUNK = 8
TILE_ROWS = 256
CHUNKS_PER_TILE = TILE_ROWS // CHUNK
SLOT_ROWS = EC_CAPACITY_FACTOR * N_TOK // N_EXPERTS
assert all((EC_CAPACITY_FACTOR * n // N_EXPERTS) % CHUNK == 0 for n in GROUP_TOKENS)
SLOT_TILES = SLOT_ROWS // FF_TM
assert SLOT_TILES * FF_TM == SLOT_ROWS
SLOT_CHUNKS = SLOT_ROWS // CHUNK
TILE_SLOTS = 4
LOOKAHEAD = TILE_SLOTS - 1
TRASH_CHUNK = N_EXPERTS * SLOT_CHUNKS
HEAD_TRASH_CHUNK = TRASH_CHUNK + TILE_SLOTS * CHUNKS_PER_TILE
TRASH_CHUNKS = TILE_SLOTS * CHUNKS_PER_TILE + FF_TM // CHUNK
assert N_EXPERTS <= FF_TM // CHUNK and TRASH_CHUNKS % (FF_TM // CHUNK) == 0
MAX_CHUNKS = (N_EXPERTS * (RB + 2 * CHUNK)) // CHUNK
CHUNK_SHAPE = (2 * CHUNK, D_MODEL // 2)


def _pack_chunks(rows_f32):
    n = rows_f32.shape[0] // CHUNK
    r3 = rows_f32.reshape(n, CHUNK, D_MODEL)
    half = D_MODEL // 2
    return jnp.concatenate([r3[:, :, :half], r3[:, :, half:]], axis=1).astype(BF16)


def _unpack_chunks(chunks_bf16):
    n = chunks_bf16.shape[0]
    c3 = chunks_bf16.astype(F32)
    left = c3[:, :CHUNK, :].reshape(n * CHUNK, D_MODEL // 2)
    right = c3[:, CHUNK:, :].reshape(n * CHUNK, D_MODEL // 2)
    return jnp.concatenate([left, right], axis=1).astype(BF16)
LIST_UNROLL = 4
LIST_LEN = (MAX_CHUNKS // CHUNKS_PER_TILE + 2) * CHUNKS_PER_TILE


def _thresh_kernel(aff_ref, thr_ref, need_ref, *, cap):
    aff = aff_ref[...]

    def body(i, cur):
        cand = cur | jnp.left_shift(jnp.int32(1), 30 - i)
        cnt = jnp.sum(jnp.where(aff >= pltpu.bitcast(cand, F32), 1.0, 0.0), axis=1, keepdims=True)
        return jnp.where(cnt >= cap, cand, cur)

    thr = pltpu.bitcast(lax.fori_loop(0, 31, body, jnp.zeros((N_EXPERTS, 1), jnp.int32)), F32)
    n_gt = jnp.sum(jnp.where(aff > thr, 1.0, 0.0), axis=1, keepdims=True)
    thr_ref[...] = jnp.broadcast_to(thr, thr_ref.shape)
    need_ref[...] = jnp.broadcast_to(cap - n_gt, need_ref.shape)


def _thresholds(aff_group):
    n = aff_group.shape[1]
    cap = EC_CAPACITY_FACTOR * n // N_EXPERTS
    return pl.pallas_call(
        functools.partial(_thresh_kernel, cap=float(cap)),
        out_shape=[jax.ShapeDtypeStruct((N_EXPERTS, LANES), F32),
                   jax.ShapeDtypeStruct((N_EXPERTS, LANES), F32)],
        compiler_params=pltpu.CompilerParams(vmem_limit_bytes=VMEM_LIMIT),
    )(aff_group)


def _select_kernel(aff_ref, thr_ref, need_ref, utri_ref, selw_ref, rank_ref, m_ref, eq_seen):
    rb = pl.program_id(0)

    @pl.when((rb == 0) | (rb == GROUP0_BLOCKS))
    def _():
        eq_seen[...] = jnp.zeros_like(eq_seen)

    aff = aff_ref[...]
    thr = thr_ref[:, 0:1]
    need = need_ref[:, 0:1]
    utri = utri_ref[...]
    eq = jnp.where(aff == thr, 1.0, 0.0)
    eq_before = jnp.dot(eq.astype(BF16), utri, preferred_element_type=F32) + eq_seen[...]
    sel = (aff > thr) | ((aff == thr) & (eq_before < need))
    eq_seen[...] += jnp.sum(eq, axis=1, keepdims=True)
    self = jnp.where(sel, 1.0, 0.0)
    rank_ref[...] = jnp.dot(self.astype(BF16), utri, preferred_element_type=F32)
    selw_ref[...] = jnp.where(sel, aff, 0.0)
    m_ref[0] = jnp.broadcast_to(jnp.sum(self, axis=1, keepdims=True), (N_EXPERTS, LANES))


def _select(aff_t, thr, need):
    utri = jnp.asarray(np.triu(np.ones((RB, RB), np.float32), 1), dtype=BF16)
    grp = lambda rb: (jnp.where(rb >= GROUP0_BLOCKS, 1, 0), 0)
    return pl.pallas_call(
        _select_kernel,
        grid=(N_RB,),
        in_specs=[
            pl.BlockSpec((N_EXPERTS, RB), lambda rb: (0, rb)),
            pl.BlockSpec((N_EXPERTS, LANES), grp),
            pl.BlockSpec((N_EXPERTS, LANES), grp),
            pl.BlockSpec((RB, RB), lambda rb: (0, 0)),
        ],
        out_specs=[
            pl.BlockSpec((N_EXPERTS, RB), lambda rb: (0, rb)),
            pl.BlockSpec((N_EXPERTS, RB), lambda rb: (0, rb)),
            pl.BlockSpec((1, N_EXPERTS, LANES), lambda rb: (rb, 0, 0)),
        ],
        out_shape=[
            jax.ShapeDtypeStruct((N_EXPERTS, N_TOK), F32),
            jax.ShapeDtypeStruct((N_EXPERTS, N_TOK), F32),
            jax.ShapeDtypeStruct((N_RB, N_EXPERTS, LANES), F32),
        ],
        scratch_shapes=[pltpu.VMEM((N_EXPERTS, 1), F32)],
        compiler_params=_cparams("arbitrary"),
    )(aff_t, thr, need, utri)


def _fill_chunk_list(list_ref, which, rb, seg_start_ref, seg_rows_ref, filler):
    cnt = jnp.int32(0)
    for e in range(N_EXPERTS):
        base = e * SLOT_CHUNKS + seg_start_ref[rb, e] // CHUNK
        n = seg_rows_ref[rb, e] // CHUNK

        def body(k, carry, base=base, cnt=cnt):
            for u in range(LIST_UNROLL):
                list_ref[which, cnt + LIST_UNROLL * k + u] = base + LIST_UNROLL * k + u
            return carry

        lax.fori_loop(0, (n + LIST_UNROLL - 1) // LIST_UNROLL, body, 0)
        cnt = cnt + n
    for j in range(CHUNKS_PER_TILE):
        list_ref[which, cnt + j] = filler(cnt + j)
    return cnt


WIN_ROWS = 128
SPAN_TILES = 8
SPAN_ROWS = SPAN_TILES * TILE_ROWS


def _mark_span(scr, rb, s0, n_rows, tgt, values, seg_off_ref, seg_rows_ref):
    win_rows = lax.broadcasted_iota(jnp.int32, (WIN_ROWS, RB), 0).astype(F32)
    n_here = jnp.minimum(n_rows - s0, SPAN_ROWS)

    def zero(i, carry):
        scr[pl.ds(pl.multiple_of(i * TILE_ROWS, TILE_ROWS), TILE_ROWS), :] = jnp.zeros((TILE_ROWS, RB), F32)
        return carry

    lax.fori_loop(0, (n_here + TILE_ROWS - 1) // TILE_ROWS, zero, 0)
    for e in range(N_EXPERTS):
        lo = seg_off_ref[rb, e]
        first = jnp.maximum(lo, s0)
        last = jnp.minimum(lo + seg_rows_ref[rb, e], s0 + SPAN_ROWS)

        def mark(w, carry, e=e, first=first):
            start = first + w * WIN_ROWS
            win = pl.ds(pl.multiple_of(start - s0, CHUNK), WIN_ROWS)
            scr[win, :] = jnp.where(tgt[e:e + 1, :] == win_rows + start.astype(F32), values[e], scr[win, :])
            return carry

        lax.fori_loop(0, jnp.maximum(last - first + WIN_ROWS - 1, 0) // WIN_ROWS, mark, 0)


def _expert_column(ref, rb):
    e_iota = lax.broadcasted_iota(jnp.int32, (N_EXPERTS, 1), 0)
    col = jnp.zeros((N_EXPERTS, 1), F32)
    for e in range(N_EXPERTS):
        col = jnp.where(e_iota == e, ref[rb, e].astype(F32), col)
    return col


def _gather_kernel(in_start_ref, in_rows_ref, in_off_ref, tot_ref, in_rank_ref, seg_start_ref, seg_len_ref,
                   x_ref, selw_ref, rank_ref, xe_hbm,
                   p_scr, obuf, hbuf, carry, zbuf, dst_list, pending, sem, hsem, zsem):
    rb = pl.program_id(0)
    n_rows = tot_ref[rb]
    n_tiles = (n_rows + TILE_ROWS - 1) // TILE_ROWS
    head_slot = TILE_SLOTS

    @pl.when(rb == 0)
    def _():
        for slot in range(TILE_SLOTS + 1):
            pending[slot] = 0
        p_scr[...] = jnp.zeros_like(p_scr)
        carry[...] = jnp.zeros_like(carry)

    _fill_chunk_list(dst_list, 0, rb, in_start_ref, in_rows_ref,
                     lambda c: TRASH_CHUNK + c % (TILE_SLOTS * CHUNKS_PER_TILE))

    def tile_copy(slot):
        return pltpu.make_async_copy(obuf.at[slot], xe_hbm.at[pl.ds(0, CHUNKS_PER_TILE)], sem.at[slot])

    def head_copy():
        return pltpu.make_async_copy(hbuf, xe_hbm.at[pl.ds(0, N_EXPERTS)], hsem)

    x = x_ref[...]
    rank = rank_ref[...]
    sel = selw_ref[...] > 0.0

    j8 = lax.broadcasted_iota(jnp.int32, (CHUNK, RB), 0).astype(F32)
    heads, tails, flags = [], [], []
    for e in range(N_EXPERTS):
        start = seg_start_ref[rb, e]
        end = start + seg_len_ref[rb, e]
        nonempty = seg_len_ref[rb, e] > 0
        has_head = nonempty & (start % CHUNK != 0)
        has_tail = nonempty & (end % CHUNK != 0)
        single = has_head & has_tail & (start // CHUNK == end // CHUNK)
        head_shift = (start % CHUNK).astype(F32)
        tail_rank0 = ((end // CHUNK) * CHUNK - start).astype(F32)
        r_e = rank[e:e + 1, :]
        s_e = sel[e:e + 1, :]
        heads.append(jnp.where(s_e & (r_e + head_shift == j8), has_head.astype(F32), 0.0))
        tails.append(jnp.where(s_e & (r_e - tail_rank0 == j8), (has_tail & ~single).astype(F32), 0.0))
        flags.append((has_head, has_tail, single, start // CHUNK))
    p_edge = jnp.concatenate(heads + tails, axis=0).astype(BF16)
    edge = jnp.dot(p_edge, x, preferred_element_type=F32)

    @pl.when(pending[head_slot] == 1)
    def _():
        head_copy().wait()

    head_rows = []
    for e, (has_head, has_tail, single, head_chunk) in enumerate(flags):
        rows = slice(e * CHUNK, (e + 1) * CHUNK)
        old = carry[rows, :]
        head = edge[rows, :] + old
        tail = edge[N_EXPERTS * CHUNK + e * CHUNK:N_EXPERTS * CHUNK + (e + 1) * CHUNK, :]
        head_rows.append(head)
        keep = ~single & ~has_tail & ~has_head
        carry[rows, :] = (single.astype(F32) * head + (has_tail & ~single).astype(F32) * tail
                          + keep.astype(F32) * old)
    hbuf[...] = _pack_chunks(jnp.concatenate(head_rows, axis=0))
    for e, (has_head, has_tail, single, head_chunk) in enumerate(flags):
        dst = jnp.where(has_head & ~single, e * SLOT_CHUNKS + head_chunk, HEAD_TRASH_CHUNK + e)
        pltpu.make_async_copy(hbuf.at[e], xe_hbm.at[dst], hsem).start()
    pending[head_slot] = 1

    off = _expert_column(in_off_ref, rb)
    rank0 = _expert_column(in_rank_ref, rb)
    n_in = _expert_column(in_rows_ref, rb)
    tgt = jnp.where(sel & (rank >= rank0) & (rank < rank0 + n_in), rank - rank0 + off, -1.0)

    def tile_body(t, carry_):
        slot = t % TILE_SLOTS

        @pl.when(pending[slot] == 1)
        def _():
            tile_copy(slot).wait()

        @pl.when(t % SPAN_TILES == 0)
        def _():
            _mark_span(p_scr, rb, t * TILE_ROWS, n_rows, tgt, [1.0] * N_EXPERTS, in_off_ref, in_rows_ref)

        p = p_scr[pl.ds(pl.multiple_of((t % SPAN_TILES) * TILE_ROWS, TILE_ROWS), TILE_ROWS), :]
        obuf[slot] = _pack_chunks(jnp.dot(p.astype(BF16), x, preferred_element_type=F32))

        for i in range(CHUNKS_PER_TILE):
            pltpu.make_async_copy(obuf.at[slot, i], xe_hbm.at[dst_list[0, t * CHUNKS_PER_TILE + i]],
                                  sem.at[slot]).start()
        pending[slot] = 1
        return carry_

    lax.fori_loop(0, n_tiles, tile_body, 0)

    @pl.when(rb == N_RB - 1)
    def _():
        for slot in range(TILE_SLOTS):
            @pl.when(pending[slot] == 1)
            def _(slot=slot):
                tile_copy(slot).wait()
        head_copy().wait()
        zbuf[...] = jnp.zeros_like(zbuf)

        def zcopy(k):
            return pltpu.make_async_copy(zbuf, xe_hbm.at[TRASH_CHUNK + k], zsem)

        def zstart(k, c):
            zcopy(k).start()
            return c

        def zwait(k, c):
            zcopy(k).wait()
            return c

        lax.fori_loop(0, TRASH_CHUNKS, zstart, 0)
        lax.fori_loop(0, TRASH_CHUNKS, zwait, 0)


def _route_gather(tables, xn, selw_t, rank_t):
    gs = pltpu.PrefetchScalarGridSpec(
        num_scalar_prefetch=len(tables),
        grid=(N_RB,),
        in_specs=[
            pl.BlockSpec((RB, D_MODEL), lambda rb, *_: (rb, 0)),
            pl.BlockSpec((N_EXPERTS, RB), lambda rb, *_: (0, rb)),
            pl.BlockSpec((N_EXPERTS, RB), lambda rb, *_: (0, rb)),
        ],
        out_specs=pl.BlockSpec(memory_space=pl.ANY),
        scratch_shapes=[
            pltpu.VMEM((SPAN_ROWS + WIN_ROWS, RB), F32),
            pltpu.VMEM((TILE_SLOTS, CHUNKS_PER_TILE) + CHUNK_SHAPE, BF16),
            pltpu.VMEM((N_EXPERTS,) + CHUNK_SHAPE, BF16),
            pltpu.VMEM((N_EXPERTS * CHUNK, D_MODEL), F32),
            pltpu.VMEM(CHUNK_SHAPE, BF16),
            pltpu.SMEM((1, LIST_LEN), jnp.int32),
            pltpu.SMEM((TILE_SLOTS + 1,), jnp.int32),
            pltpu.SemaphoreType.DMA((TILE_SLOTS,)),
            pltpu.SemaphoreType.DMA,
            pltpu.SemaphoreType.DMA,
        ],
    )
    return pl.pallas_call(
        _gather_kernel,
        grid_spec=gs,
        out_shape=jax.ShapeDtypeStruct((TRASH_CHUNK + TRASH_CHUNKS,) + CHUNK_SHAPE, BF16),
        compiler_params=_cparams("arbitrary"),
    )(*tables, xn, selw_t, rank_t)


W_PIECES_PER_MATRIX = 4
W_PIECES = 3 * W_PIECES_PER_MATRIX
FFN_STEPS = max(SLOT_TILES, W_PIECES + 1)
WG_PIECE_ROWS = D_MODEL // W_PIECES_PER_MATRIX
WD_PIECE_ROWS = EXPERT_D_FF // W_PIECES_PER_MATRIX
FFN_VMEM_LIMIT = 56 * 1024 * 1024


def _ffn_kernel(x_ref, wg_hbm, wu_hbm, wd_hbm, o_ref,
                wg_bf, wu_bf, wd_bf, stage_up, stage_dn, sem):
    e = pl.program_id(0)
    i = pl.program_id(1)
    slot = e % 2

    matrices = ((wg_hbm, stage_up, wg_bf, WG_PIECE_ROWS),
                (wu_hbm, stage_up, wu_bf, WG_PIECE_ROWS),
                (wd_hbm, stage_dn, wd_bf, WD_PIECE_ROWS))

    def piece(expert, k, buf, finish):
        def of_kind(kind):
            w_hbm, stage, w_bf, rows = matrices[kind]
            r0 = pl.multiple_of((k - kind * W_PIECES_PER_MATRIX) * rows, rows)
            cp = pltpu.make_async_copy(w_hbm.at[expert, pl.ds(r0, rows), :], stage.at[buf], sem.at[buf])
            if finish:
                cp.wait()
                w_bf[expert % 2, pl.ds(r0, rows), :] = stage[buf].astype(BF16)
            else:
                cp.start()

        if isinstance(k, int):
            of_kind(k // W_PIECES_PER_MATRIX)
        else:
            for kind in range(len(matrices)):
                pl.when(k // W_PIECES_PER_MATRIX == kind)(functools.partial(of_kind, kind))

    @pl.when((e == 0) & (i == 0))
    def _():
        for k in range(W_PIECES):
            piece(0, k, k % 2, finish=False)
            piece(0, k, k % 2, finish=True)

    @pl.when(e + 1 < N_EXPERTS)
    def _():
        @pl.when((i >= 1) & (i <= W_PIECES))
        def _():
            piece(e + 1, i - 1, (i - 1) % 2, finish=True)

        @pl.when(i < W_PIECES)
        def _():
            piece(e + 1, i, i % 2, finish=False)

    @pl.when(i < SLOT_TILES)
    def _():
        x = _unpack_chunks(x_ref[...])
        acc = jnp.zeros((FF_TM, D_MODEL), F32)
        for c in range(EXPERT_D_FF // FF_FC):
            gt = jnp.dot(x, wg_bf[slot, :, c * FF_FC:(c + 1) * FF_FC], preferred_element_type=F32)
            up = jnp.dot(x, wu_bf[slot, :, c * FF_FC:(c + 1) * FF_FC], preferred_element_type=F32)
            h = (gt * (1.0 / (1.0 + jnp.exp(-gt))) * up).astype(BF16)
            acc = acc + jnp.dot(h, wd_bf[slot, c * FF_FC:(c + 1) * FF_FC, :], preferred_element_type=F32)
        o_ref[...] = _pack_chunks(acc)


def _ffn(xe, w_gate, w_up, w_down):
    def rows(e, i):
        return (e * SLOT_TILES + jnp.minimum(i, SLOT_TILES - 1), 0, 0)

    tile_chunks = (FF_TM // CHUNK,) + CHUNK_SHAPE
    gs = pltpu.PrefetchScalarGridSpec(
        num_scalar_prefetch=0,
        grid=(N_EXPERTS, FFN_STEPS),
        in_specs=[
            pl.BlockSpec(tile_chunks, rows),
            pl.BlockSpec(memory_space=pl.ANY),
            pl.BlockSpec(memory_space=pl.ANY),
            pl.BlockSpec(memory_space=pl.ANY),
        ],
        out_specs=pl.BlockSpec(tile_chunks, rows),
        scratch_shapes=[
            pltpu.VMEM((2, D_MODEL, EXPERT_D_FF), BF16),
            pltpu.VMEM((2, D_MODEL, EXPERT_D_FF), BF16),
            pltpu.VMEM((2, EXPERT_D_FF, D_MODEL), BF16),
            pltpu.VMEM((2, WG_PIECE_ROWS, EXPERT_D_FF), F32),
            pltpu.VMEM((2, WD_PIECE_ROWS, D_MODEL), F32),
            pltpu.SemaphoreType.DMA((2,)),
        ],
    )
    return pl.pallas_call(
        _ffn_kernel,
        grid_spec=gs,
        out_shape=jax.ShapeDtypeStruct((N_EXPERTS * SLOT_CHUNKS,) + CHUNK_SHAPE, BF16),
        compiler_params=pltpu.CompilerParams(dimension_semantics=("arbitrary", "arbitrary"),
                                             vmem_limit_bytes=FFN_VMEM_LIMIT),
    )(xe, w_gate, w_up, w_down)


def _combine_kernel(seg_start_ref, seg_rows_ref, seg_off_ref, tot_ref, seg_shift_ref,
                    x1_ref, selw_ref, rank_ref, g_ref, ye_hbm, op_ref, os_ref,
                    w_scr, ybuf, acc, src_list, state, sem):
    rb = pl.program_id(0)
    cur = rb % 2
    nxt_rb = jnp.minimum(rb + 1, N_RB - 1)

    def tiles_of(b):
        return (tot_ref[b] + TILE_ROWS - 1) // TILE_ROWS

    n_tiles = tiles_of(rb)
    n_tiles_next = jnp.where(rb + 1 < N_RB, tiles_of(nxt_rb), 0)

    def fetch(which, t, slot):
        for i in range(CHUNKS_PER_TILE):
            pltpu.make_async_copy(ye_hbm.at[src_list[which, t * CHUNKS_PER_TILE + i]], ybuf.at[slot, i],
                                  sem.at[slot]).start()

    def tile_wait(slot):
        pltpu.make_async_copy(ye_hbm.at[pl.ds(0, CHUNKS_PER_TILE)], ybuf.at[slot], sem.at[slot]).wait()

    @pl.when(rb == 0)
    def _():
        w_scr[...] = jnp.zeros_like(w_scr)
        _fill_chunk_list(src_list, 0, 0, seg_start_ref, seg_rows_ref, lambda c: 0)
        state[0] = 0
        state[1] = 0

    @pl.when(rb + 1 < N_RB)
    def _():
        _fill_chunk_list(src_list, 1 - cur, nxt_rb, seg_start_ref, seg_rows_ref, lambda c: 0)

    slot0 = state[0]
    started = state[1]
    for j in range(LOOKAHEAD):
        @pl.when((j >= started) & (j < n_tiles))
        def _(j=j):
            fetch(cur, j, (slot0 + j) % TILE_SLOTS)

    off = _expert_column(seg_off_ref, rb) + _expert_column(seg_shift_ref, rb)
    selw = selw_ref[...]
    gates = [selw[e:e + 1, :] for e in range(N_EXPERTS)]
    tgt = jnp.where(selw > 0.0, rank_ref[...] + off, -1.0)
    acc[...] = x1_ref[...]

    def tile_body(t, next_started):
        slot = (slot0 + t) % TILE_SLOTS
        ahead = t + LOOKAHEAD

        @pl.when(ahead < n_tiles)
        def _():
            fetch(cur, ahead, (slot0 + ahead) % TILE_SLOTS)

        start_next = (ahead >= n_tiles) & (next_started < jnp.minimum(LOOKAHEAD, n_tiles_next))

        @pl.when(start_next)
        def _():
            fetch(1 - cur, next_started, (slot0 + n_tiles + next_started) % TILE_SLOTS)

        @pl.when(t % SPAN_TILES == 0)
        def _():
            _mark_span(w_scr, rb, t * TILE_ROWS, tot_ref[rb], tgt, gates, seg_off_ref, seg_rows_ref)

        tile_wait(slot)
        w = w_scr[pl.ds(pl.multiple_of((t % SPAN_TILES) * TILE_ROWS, TILE_ROWS), TILE_ROWS), :]
        acc[...] += lax.dot_general(w.astype(BF16), _unpack_chunks(ybuf[slot]),
                                    (((0,), (0,)), ((), ())), preferred_element_type=F32)
        return next_started + start_next.astype(jnp.int32)

    state[1] = lax.fori_loop(0, n_tiles, tile_body, jnp.int32(0))
    state[0] = (slot0 + n_tiles) % TILE_SLOTS
    x2 = acc[...]
    ms = jnp.mean(x2 * x2, axis=-1, keepdims=True)
    y = x2 * lax.rsqrt(ms + EPS) * g_ref[...]

    @pl.when(rb < GROUP0_BLOCKS)
    def _():
        op_ref[...] = y

    @pl.when(rb >= GROUP0_BLOCKS)
    def _():
        os_ref[...] = y


def _combine(tables, x1, selw_t, rank_t, g, ye):
    gs = pltpu.PrefetchScalarGridSpec(
        num_scalar_prefetch=len(tables),
        grid=(N_RB,),
        in_specs=[
            pl.BlockSpec((RB, D_MODEL), lambda rb, *_: (rb, 0)),
            pl.BlockSpec((N_EXPERTS, RB), lambda rb, *_: (0, rb)),
            pl.BlockSpec((N_EXPERTS, RB), lambda rb, *_: (0, rb)),
            pl.BlockSpec((1, D_MODEL), lambda rb, *_: (0, 0)),
            pl.BlockSpec(memory_space=pl.ANY),
        ],
        out_specs=[
            pl.BlockSpec((RB, D_MODEL), lambda rb, *_: (jnp.minimum(rb, GROUP0_BLOCKS - 1), 0)),
            pl.BlockSpec((RB, D_MODEL), lambda rb, *_: (jnp.maximum(rb - GROUP0_BLOCKS, 0), 0)),
        ],
        scratch_shapes=[
            pltpu.VMEM((SPAN_ROWS + WIN_ROWS, RB), F32),
            pltpu.VMEM((TILE_SLOTS, CHUNKS_PER_TILE) + CHUNK_SHAPE, BF16),
            pltpu.VMEM((RB, D_MODEL), F32),
            pltpu.SMEM((2, LIST_LEN), jnp.int32),
            pltpu.SMEM((2,), jnp.int32),
            pltpu.SemaphoreType.DMA((TILE_SLOTS,)),
        ],
    )
    return pl.pallas_call(
        _combine_kernel,
        grid_spec=gs,
        out_shape=[jax.ShapeDtypeStruct((GROUP_TOKENS[0], D_MODEL), F32),
                   jax.ShapeDtypeStruct((GROUP_TOKENS[1], D_MODEL), F32)],
        compiler_params=_cparams("arbitrary"),
    )(*tables, x1, selw_t, rank_t, g, ye)


def kernel(x_prompt, x_sample, norm_mix_g, w_in, w_fourier, sink, w_out, norm_ffn_g, w_router,
           w_gate, w_up, w_down, norm_final_g):
    assert x_prompt.shape == (1, SEQ, D_MODEL) and x_sample.shape == (N_SEQ - 1, SEQ, D_MODEL)
    assert w_in.shape == (1, D_MODEL, IN_WIDTH) and w_out.shape == (1, MIX_WIDTH, D_MODEL)
    assert w_gate.shape == (1, N_EXPERTS, D_MODEL, EXPERT_D_FF) and w_down.shape == (1, N_EXPERTS, EXPERT_D_FF, D_MODEL)
    xp = x_prompt.reshape(-1, D_MODEL)
    xs = x_sample.reshape(-1, D_MODEL)
    cos_t, sin_t = _rope_tables()

    mix_cols = np.concatenate([
        np.arange(FOURIER_WIDTH),
        FOURIER_WIDTH + (np.asarray(Q_ORDER)[:, None] * HEAD_DIM + np.arange(HEAD_DIM)[None, :]).reshape(-1)])
    in_cols = np.concatenate([mix_cols, np.arange(MIX_WIDTH, IN_WIDTH)])
    w_in_r = w_in[0][:, in_cols].astype(BF16)
    w_out_r = w_out[0][mix_cols, :].astype(BF16)

    u, q, k, v = _inproj(xp, xs, norm_mix_g[0][None, :], w_in_r, cos_t, sin_t)
    yf = _fourier(u, w_fourier[0].astype(BF16))
    ya = _attention(sink[0], q, k, v)
    x1, xn, aff_t = _outproj(xp, xs, yf, ya, w_out_r, norm_ffn_g[0][None, :],
                             w_router[0].T.astype(BF16))

    thr0, need0 = _thresholds(aff_t[:, :GROUP_TOKENS[0]])
    thr1, need1 = _thresholds(aff_t[:, GROUP_TOKENS[0]:])
    selw_t, rank_t, m = _select(aff_t, jnp.concatenate([thr0, thr1]), jnp.concatenate([need0, need1]))

    seg_len = m[:, :, 0].astype(jnp.int32)
    seg_start = jnp.cumsum(seg_len, axis=0) - seg_len
    seg_end = seg_start + seg_len

    def offsets(rows):
        return jnp.cumsum(rows, axis=1) - rows, jnp.sum(rows, axis=1)

    in_start = -(-seg_start // CHUNK) * CHUNK
    in_rows = jnp.maximum(seg_end // CHUNK * CHUNK - in_start, 0)
    in_off, in_tot = offsets(in_rows)
    gather_tables = (in_start, in_rows, in_off, in_tot, in_start - seg_start, seg_start, seg_len)
    out_start = seg_start // CHUNK * CHUNK
    out_rows = jnp.where(seg_len > 0, -(-seg_end // CHUNK) * CHUNK - out_start, 0)
    out_off, out_tot = offsets(out_rows)
    combine_tables = (out_start, out_rows, out_off, out_tot, seg_start - out_start)

    xe = _route_gather(gather_tables, xn, selw_t, rank_t)
    ye = _ffn(xe, w_gate[0], w_up[0], w_down[0])
    yp, ys = _combine(combine_tables, x1, selw_t, rank_t, norm_final_g[None, :], ye)
    return (yp.reshape(x_prompt.shape), ys.reshape(x_sample.shape))
```

```python
import functools

import numpy as np
import jax
import jax.numpy as jnp
from jax import lax
from jax.experimental import pallas as pl
from jax.experimental.pallas import tpu as pltpu

F32 = jnp.float32
BF16 = jnp.bfloat16

D_MODEL = 1024
SEQ = 16384
N_SEQ = 3
N_TOK = N_SEQ * SEQ
HEAD_DIM = 64
N_HEADS = 12
N_KV_HEADS = 4
GQA_GROUP = N_HEADS // N_KV_HEADS
ATTN_WIDTH = N_HEADS * HEAD_DIM
KV_WIDTH = N_KV_HEADS * HEAD_DIM
FOURIER_WIDTH = 256
FOURIER_GROUP_DIM = 64
MIX_WIDTH = FOURIER_WIDTH + ATTN_WIDTH
IN_WIDTH = MIX_WIDTH + 2 * KV_WIDTH
WINDOW = 128
ROPE_THETA = 10000.0
N_EXPERTS = 16
EC_CAPACITY_FACTOR = 2
EXPERT_D_FF = 2048
EPS = 1e-6
NEG = -1e30

LANES = 128
FFT_R = 128
VMEM_LIMIT = 48 * 1024 * 1024

TM = 1024
TQ = 1024
QB = 128
FF_TM = 512
FF_FC = 512


def _cparams(*sem):
    return pltpu.CompilerParams(dimension_semantics=sem, vmem_limit_bytes=VMEM_LIMIT)


def _x_specs():
    n_p = SEQ // TM
    return [pl.BlockSpec((TM, D_MODEL), lambda i, *_: (jnp.minimum(i, n_p - 1), 0)),
            pl.BlockSpec((TM, D_MODEL), lambda i, *_: (jnp.maximum(i - n_p, 0), 0))]


def _x_tile(xp_ref, xs_ref):
    return jnp.where(pl.program_id(0) < SEQ // TM, xp_ref[...], xs_ref[...])


def _inproj_kernel(xp_ref, xs_ref, g_ref, w_ref, cos_ref, sin_ref, u_ref, q_ref, k_ref, v_ref):
    x = _x_tile(xp_ref, xs_ref)
    ms = jnp.mean(x * x, axis=-1, keepdims=True)
    hn = (x * lax.rsqrt(ms + EPS) * g_ref[...]).astype(BF16)
    proj = jnp.dot(hn, w_ref[...], preferred_element_type=F32)
    u_ref[...] = proj[:, :FOURIER_WIDTH].astype(BF16)
    cos = cos_ref[...]
    sin = sin_ref[...]
    lane = lax.broadcasted_iota(jnp.int32, cos.shape, 1)
    first_half = (lane % HEAD_DIM) < (HEAD_DIM // 2)

    def rope(t):
        partner = jnp.where(first_half,
                            pltpu.roll(t, LANES - HEAD_DIM // 2, 1),
                            pltpu.roll(t, HEAD_DIM // 2, 1))
        return t * cos + partner * sin

    scale = HEAD_DIM ** -0.5 * LOG2E
    for c in range(ATTN_WIDTH // LANES):
        lo = FOURIER_WIDTH + c * LANES
        q_ref[:, c * LANES:(c + 1) * LANES] = (rope(proj[:, lo:lo + LANES]) * scale).astype(BF16)
    low_half = lane < HEAD_DIM
    for c in range(KV_WIDTH // LANES):
        lo = FOURIER_WIDTH + ATTN_WIDTH + c * LANES
        kc = rope(proj[:, lo:lo + LANES])
        vc = proj[:, lo + KV_WIDTH:lo + KV_WIDTH + LANES]
        for half, keep in enumerate((low_half, ~low_half)):
            g = 2 * c + half
            k_ref[:, g * LANES:(g + 1) * LANES] = jnp.where(keep, kc, 0.0).astype(BF16)
            v_ref[:, g * LANES:(g + 1) * LANES] = jnp.where(keep, vc, 0.0).astype(BF16)


def _inproj(xp, xs, g, w_in, cos_t, sin_t):
    n_pos_blocks = SEQ // TM
    return pl.pallas_call(
        _inproj_kernel,
        grid=(N_TOK // TM,),
        in_specs=_x_specs() + [
            pl.BlockSpec((1, D_MODEL), lambda i: (0, 0)),
            pl.BlockSpec((D_MODEL, IN_WIDTH), lambda i: (0, 0)),
            pl.BlockSpec((TM, LANES), lambda i: (i % n_pos_blocks, 0)),
            pl.BlockSpec((TM, LANES), lambda i: (i % n_pos_blocks, 0)),
        ],
        out_specs=[
            pl.BlockSpec((TM, FOURIER_WIDTH), lambda i: (i, 0)),
            pl.BlockSpec((TM, ATTN_WIDTH), lambda i: (i, 0)),
            pl.BlockSpec((TM, KV4), lambda i: (i, 0)),
            pl.BlockSpec((TM, KV4), lambda i: (i, 0)),
        ],
        out_shape=[
            jax.ShapeDtypeStruct((N_TOK, FOURIER_WIDTH), BF16),
            jax.ShapeDtypeStruct((N_TOK, ATTN_WIDTH), BF16),
            jax.ShapeDtypeStruct((N_TOK, KV4), BF16),
            jax.ShapeDtypeStruct((N_TOK, KV4), BF16),
        ],
        compiler_params=_cparams("parallel"),
    )(xp, xs, g, w_in, cos_t, sin_t)


def _rope_tables():
    half = HEAD_DIM // 2
    inv_freq = 1.0 / (ROPE_THETA ** (jnp.arange(half, dtype=F32) / half))
    ang = jnp.arange(SEQ, dtype=F32)[:, None] * inv_freq[None, :]
    cos = jnp.cos(ang)
    sin = jnp.sin(ang)
    reps = LANES // HEAD_DIM
    cos_t = jnp.tile(jnp.concatenate([cos, cos], axis=-1), (1, reps))
    sin_t = jnp.tile(jnp.concatenate([-sin, sin], axis=-1), (1, reps))
    return cos_t, sin_t


def _dft_tables():
    r = FFT_R
    c = np.arange(FOURIER_GROUP_DIM)
    ang_c = 2.0 * np.pi * np.outer(c, c) / FOURIER_GROUP_DIM
    n_groups = FOURIER_WIDTH // FOURIER_GROUP_DIM
    eye = np.eye(n_groups)
    ch_scale = FOURIER_GROUP_DIM ** -0.5
    c_blk = np.kron(eye, np.cos(ang_c)) * ch_scale
    s_blk = np.kron(eye, np.sin(ang_c)) * ch_scale
    cs = np.concatenate([c_blk, -s_blk], axis=1)
    k = np.arange(r)
    ang_r = 2.0 * np.pi * np.outer(k, k) / r
    st_scale = r ** -0.5
    wr = np.cos(ang_r) * st_scale
    wi = -np.sin(ang_r) * st_scale
    wbig = np.block([[wr, -wi], [wi, wr]])
    ang_t = 2.0 * np.pi * np.outer(k, k) / (r * r)
    tr = np.cos(ang_t)
    ti = -np.sin(ang_t)
    f = lambda a: jnp.asarray(a, dtype=F32)
    return f(cs).astype(BF16), f(wbig).astype(BF16), f(wr), f(wi), f(tr), f(ti)


FS_SB = 16
FS_KB = 16


def _fft_stage1_kernel(u_ref, cs_ref, wbig_ref, y_ref):
    cs = cs_ref[...]
    wbig = wbig_ref[...]
    fw = FOURIER_WIDTH
    for j in range(FS_SB):
        u = u_ref[:, j * fw:(j + 1) * fw]
        g = jnp.dot(u, cs, preferred_element_type=F32)
        gst = jnp.concatenate([g[:, :fw], g[:, fw:]], axis=0).astype(BF16)
        y = jnp.dot(wbig, gst, preferred_element_type=F32)
        y_ref[:, j * 2 * fw:j * 2 * fw + fw] = y[:FFT_R].astype(BF16)
        y_ref[:, j * 2 * fw + fw:(j + 1) * 2 * fw] = y[FFT_R:].astype(BF16)


def _fft_stage2_kernel(y_ref, wr_ref, wi_ref, tr_ref, ti_ref, wf_ref, o_ref):
    kb = pl.program_id(1)
    wr = wr_ref[...]
    wi = wi_ref[...]
    wf = wf_ref[...]
    fw = FOURIER_WIDTH
    for j in range(FS_KB):
        k1 = kb * FS_KB + j
        tr = tr_ref[pl.ds(k1, 1), :]
        ti = ti_ref[pl.ds(k1, 1), :]
        mr = wr * tr - wi * ti
        mi = wr * ti + wi * tr
        m2 = jnp.concatenate([mr, -mi], axis=1).astype(BF16)
        yb = y_ref[j * FFT_R:(j + 1) * FFT_R, :]
        yst = jnp.concatenate([yb[:, :fw], yb[:, fw:]], axis=0)
        z = jnp.dot(m2, yst, preferred_element_type=F32)
        yf = jnp.dot(z.astype(BF16), wf, preferred_element_type=F32)
        o_ref[:, j * fw:(j + 1) * fw] = yf.astype(BF16)


def _fourier(u, w_fourier):
    cs, wbig, wr, wi, tr, ti = _dft_tables()
    r, fw = FFT_R, FOURIER_WIDTH
    u2 = u.reshape(N_SEQ * r, r * fw)
    y = pl.pallas_call(
        _fft_stage1_kernel,
        grid=(N_SEQ, r // FS_SB),
        in_specs=[
            pl.BlockSpec((r, FS_SB * fw), lambda s, j: (s, j)),
            pl.BlockSpec((fw, 2 * fw), lambda s, j: (0, 0)),
            pl.BlockSpec((2 * r, 2 * r), lambda s, j: (0, 0)),
        ],
        out_specs=pl.BlockSpec((r, FS_SB * 2 * fw), lambda s, j: (s, j)),
        out_shape=jax.ShapeDtypeStruct((N_SEQ * r, r * 2 * fw), BF16),
        compiler_params=_cparams("parallel", "parallel"),
    )(u2, cs, wbig)
    y2 = y.reshape(N_SEQ * r * r, 2 * fw)
    full = lambda s, j: (0, 0)
    yf = pl.pallas_call(
        _fft_stage2_kernel,
        grid=(N_SEQ, r // FS_KB),
        in_specs=[
            pl.BlockSpec((FS_KB * r, 2 * fw), lambda s, j: (s * (r // FS_KB) + j, 0)),
            pl.BlockSpec((r, r), full),
            pl.BlockSpec((r, r), full),
            pl.BlockSpec((r, r), full),
            pl.BlockSpec((r, r), full),
            pl.BlockSpec((fw, fw), full),
        ],
        out_specs=pl.BlockSpec((r, FS_KB * fw), lambda s, j: (s, j)),
        out_shape=jax.ShapeDtypeStruct((N_SEQ * r, r * fw), BF16),
        compiler_params=_cparams("parallel", "parallel"),
    )(y2, wr, wi, tr, ti, w_fourier)
    return yf.reshape(N_TOK, fw)


LO_HEADS = (0, 1, 2, 6, 7, 8)
HI_HEADS = (3, 4, 5, 9, 10, 11)
Q_ORDER = tuple(h for pair in zip(LO_HEADS, HI_HEADS) for h in pair)
assert all(h // GQA_GROUP % 2 == 0 for h in LO_HEADS) and all(h // GQA_GROUP % 2 == 1 for h in HI_HEADS)
KV4 = 2 * KV_WIDTH
LOG2E = 1.4426950408889634


def _attn_kernel(sink_ref, q_ref, kp_ref, kc_ref, kn_ref, vp_ref, vc_ref, vn_ref, o_ref,
                 kext, vext):
    i = pl.program_id(1)
    n_i = pl.num_programs(1)
    kext[0:QB, :] = kp_ref[...]
    kext[QB:QB + TQ, :] = kc_ref[...]
    kext[QB + TQ:, :] = kn_ref[...]
    vext[0:QB, :] = vp_ref[...]
    vext[QB:QB + TQ, :] = vc_ref[...]
    vext[QB + TQ:, :] = vn_ref[...]

    rows = GQA_GROUP * QB
    a = lax.broadcasted_iota(jnp.int32, (rows, QB), 0) % QB
    j = lax.broadcasted_iota(jnp.int32, (rows, QB), 1)
    band_prev = jnp.where(j >= a, 0.0, NEG)
    band_next = jnp.where(j <= a, 0.0, NEG)
    row_id = lax.broadcasted_iota(jnp.int32, (rows, 1), 0)
    low_half = lax.broadcasted_iota(jnp.int32, (rows, LANES), 1) < HEAD_DIM

    def softmax_half(s, sinks, bias_prev, bias_next):
        s_p = s[:, :QB] + bias_prev
        s_c = s[:, QB:2 * QB]
        s_n = s[:, 2 * QB:] + bias_next
        sink = jnp.where(row_id < QB, sinks[0], jnp.where(row_id < 2 * QB, sinks[1], sinks[2])) * LOG2E
        m = jnp.max(jnp.maximum(jnp.maximum(s_p, s_c), s_n), axis=-1, keepdims=True)
        m = jnp.maximum(m, sink)
        p_p = jnp.exp2(s_p - m)
        p_c = jnp.exp2(s_c - m)
        p_n = jnp.exp2(s_n - m)
        denom = jnp.sum(p_p + p_c + p_n, axis=-1, keepdims=True) + jnp.exp2(sink - m)
        return [p_p.astype(BF16), p_c.astype(BF16), p_n.astype(BF16)], 1.0 / denom

    for sb in range(TQ // QB):
        blk = i * (TQ // QB) + sb
        bias_prev = band_prev + jnp.where(blk == 0, NEG, 0.0)
        bias_next = band_next + jnp.where(blk == n_i * (TQ // QB) - 1, NEG, 0.0)
        for pr in range(N_KV_HEADS // 2):
            groups = [GQA_GROUP * pr + m for m in range(GQA_GROUP)]
            q3 = jnp.concatenate([q_ref[sb * QB:(sb + 1) * QB, g * LANES:(g + 1) * LANES] for g in groups],
                                 axis=0)
            kp = kext[sb * QB:sb * QB + 3 * QB, pr * 2 * LANES:(pr + 1) * 2 * LANES]
            vp = vext[sb * QB:sb * QB + 3 * QB, pr * 2 * LANES:(pr + 1) * 2 * LANES]
            kcat = jnp.concatenate([kp[:, :LANES], kp[:, LANES:]], axis=0)
            vcat = jnp.concatenate([vp[:, :LANES], vp[:, LANES:]], axis=0)
            s = lax.dot_general(q3, kcat, (((1,), (1,)), ((), ())), preferred_element_type=F32)
            p_lo, r_lo = softmax_half(s[:, :3 * QB], [sink_ref[LO_HEADS[g]] for g in groups],
                                      bias_prev, bias_next)
            p_hi, r_hi = softmax_half(s[:, 3 * QB:], [sink_ref[HI_HEADS[g]] for g in groups],
                                      bias_prev, bias_next)
            p = jnp.concatenate(p_lo + p_hi, axis=1)
            o = jnp.dot(p, vcat, preferred_element_type=F32) * jnp.where(low_half, r_lo, r_hi)
            for m, g in enumerate(groups):
                o_ref[sb * QB:(sb + 1) * QB, g * LANES:(g + 1) * LANES] = o[m * QB:(m + 1) * QB].astype(BF16)


def _attention(sink, q, k, v):
    nqb = SEQ // QB
    per = TQ // QB
    cur = lambda s, i: (s * (SEQ // TQ) + i, 0)
    prev = lambda s, i: (s * nqb + jnp.maximum(i * per - 1, 0), 0)
    nxt = lambda s, i: (s * nqb + jnp.minimum(i * per + per, nqb - 1), 0)
    return pl.pallas_call(
        _attn_kernel,
        grid=(N_SEQ, SEQ // TQ),
        in_specs=[
            pl.BlockSpec(memory_space=pltpu.SMEM),
            pl.BlockSpec((TQ, ATTN_WIDTH), cur),
            pl.BlockSpec((QB, KV4), prev),
            pl.BlockSpec((TQ, KV4), cur),
            pl.BlockSpec((QB, KV4), nxt),
            pl.BlockSpec((QB, KV4), prev),
            pl.BlockSpec((TQ, KV4), cur),
            pl.BlockSpec((QB, KV4), nxt),
        ],
        out_specs=pl.BlockSpec((TQ, ATTN_WIDTH), cur),
        out_shape=jax.ShapeDtypeStruct((N_TOK, ATTN_WIDTH), BF16),
        scratch_shapes=[pltpu.VMEM((TQ + 2 * QB, KV4), BF16),
                        pltpu.VMEM((TQ + 2 * QB, KV4), BF16)],
        compiler_params=_cparams("parallel", "parallel"),
    )(sink, q, k, k, k, v, v, v)


def _outproj_kernel(xp_ref, xs_ref, yf_ref, ya_ref, wo_ref, g_ref, wr_ref, x1_ref, xn_ref, aff_ref):
    mix = jnp.dot(yf_ref[...], wo_ref[:FOURIER_WIDTH, :], preferred_element_type=F32)
    mix = mix + jnp.dot(ya_ref[...], wo_ref[FOURIER_WIDTH:, :], preferred_element_type=F32)
    x1 = _x_tile(xp_ref, xs_ref) + mix
    x1_ref[...] = x1
    ms = jnp.mean(x1 * x1, axis=-1, keepdims=True)
    xn = (x1 * lax.rsqrt(ms + EPS) * g_ref[...]).astype(BF16)
    xn_ref[...] = xn
    logits = lax.dot_general(wr_ref[...], xn, (((1,), (1,)), ((), ())), preferred_element_type=F32)
    mx = jnp.max(logits, axis=0, keepdims=True)
    ex = jnp.exp(logits - mx)
    aff_ref[...] = ex / jnp.sum(ex, axis=0, keepdims=True)


def _outproj(xp, xs, yf, ya, w_out, g, w_router_t):
    return pl.pallas_call(
        _outproj_kernel,
        grid=(N_TOK // TM,),
        in_specs=_x_specs() + [
            pl.BlockSpec((TM, FOURIER_WIDTH), lambda i: (i, 0)),
            pl.BlockSpec((TM, ATTN_WIDTH), lambda i: (i, 0)),
            pl.BlockSpec((D_MODEL, D_MODEL), lambda i: (0, 0)),
            pl.BlockSpec((1, D_MODEL), lambda i: (0, 0)),
            pl.BlockSpec((N_EXPERTS, D_MODEL), lambda i: (0, 0)),
        ],
        out_specs=[
            pl.BlockSpec((TM, D_MODEL), lambda i: (i, 0)),
            pl.BlockSpec((TM, D_MODEL), lambda i: (i, 0)),
            pl.BlockSpec((N_EXPERTS, TM), lambda i: (0, i)),
        ],
        out_shape=[
            jax.ShapeDtypeStruct((N_TOK, D_MODEL), F32),
            jax.ShapeDtypeStruct((N_TOK, D_MODEL), BF16),
            jax.ShapeDtypeStruct((N_EXPERTS, N_TOK), F32),
        ],
        compiler_params=_cparams("parallel"),
    )(xp, xs, yf, ya, w_out, g, w_router_t)


RB = 512
N_RB = N_TOK // RB
GROUP_TOKENS = (SEQ, 2 * SEQ)
GROUP0_BLOCKS = GROUP_TOKENS[0] // RB
SEL_BLOCKS = 4
assert GROUP0_BLOCKS % SEL_BLOCKS == 0 and N_RB % SEL_BLOCKS == 0
CHUNK = 8
TILE_ROWS = 256
CHUNKS_PER_TILE = TILE_ROWS // CHUNK
SLOT_ROWS = EC_CAPACITY_FACTOR * N_TOK // N_EXPERTS
assert all((EC_CAPACITY_FACTOR * n // N_EXPERTS) % CHUNK == 0 for n in GROUP_TOKENS)
SLOT_TILES = SLOT_ROWS // FF_TM
assert SLOT_TILES * FF_TM == SLOT_ROWS
SLOT_CHUNKS = SLOT_ROWS // CHUNK
TILE_SLOTS = 4
LOOKAHEAD = TILE_SLOTS - 1
TRASH_CHUNK = N_EXPERTS * SLOT_CHUNKS
HEAD_TRASH_CHUNK = TRASH_CHUNK + TILE_SLOTS * CHUNKS_PER_TILE
TRASH_CHUNKS = TILE_SLOTS * CHUNKS_PER_TILE + FF_TM // CHUNK
assert N_EXPERTS <= FF_TM // CHUNK and TRASH_CHUNKS % (FF_TM // CHUNK) == 0
MAX_CHUNKS = (N_EXPERTS * (RB + 2 * CHUNK)) // CHUNK
CHUNK_SHAPE = (2 * CHUNK, D_MODEL // 2)


def _pack_chunks(rows_f32):
    n = rows_f32.shape[0] // CHUNK
    r3 = rows_f32.reshape(n, CHUNK, D_MODEL)
    half = D_MODEL // 2
    return jnp.concatenate([r3[:, :, :half], r3[:, :, half:]], axis=1).astype(BF16)


def _unpack_chunks(chunks_bf16):
    n = chunks_bf16.shape[0]
    c3 = chunks_bf16.astype(F32)
    left = c3[:, :CHUNK, :].reshape(n * CHUNK, D_MODEL // 2)
    right = c3[:, CHUNK:, :].reshape(n * CHUNK, D_MODEL // 2)
    return jnp.concatenate([left, right], axis=1).astype(BF16)
LIST_UNROLL = 4
LIST_LEN = (MAX_CHUNKS // CHUNKS_PER_TILE + 2) * CHUNKS_PER_TILE


def _thresh_kernel(aff_ref, thr_ref, need_ref, *, cap):
    aff = aff_ref[...]

    def body(i, cur):
        cand = cur | jnp.left_shift(jnp.int32(1), 30 - i)
        cnt = jnp.sum(jnp.where(aff >= pltpu.bitcast(cand, F32), 1.0, 0.0), axis=1, keepdims=True)
        return jnp.where(cnt >= cap, cand, cur)

    thr = pltpu.bitcast(lax.fori_loop(0, 31, body, jnp.zeros((N_EXPERTS, 1), jnp.int32)), F32)
    n_gt = jnp.sum(jnp.where(aff > thr, 1.0, 0.0), axis=1, keepdims=True)
    thr_ref[...] = jnp.broadcast_to(thr, thr_ref.shape)
    need_ref[...] = jnp.broadcast_to(cap - n_gt, need_ref.shape)


def _thresholds(aff_group):
    n = aff_group.shape[1]
    cap = EC_CAPACITY_FACTOR * n // N_EXPERTS
    return pl.pallas_call(
        functools.partial(_thresh_kernel, cap=float(cap)),
        out_shape=[jax.ShapeDtypeStruct((N_EXPERTS, LANES), F32),
                   jax.ShapeDtypeStruct((N_EXPERTS, LANES), F32)],
        compiler_params=pltpu.CompilerParams(vmem_limit_bytes=VMEM_LIMIT),
    )(aff_group)


def _select_kernel(aff_ref, thr_ref, need_ref, utri_ref, selw_ref, rank_ref, m_ref, eq_seen):
    step = pl.program_id(0)

    @pl.when((step == 0) | (step == GROUP0_BLOCKS // SEL_BLOCKS))
    def _():
        eq_seen[...] = jnp.zeros_like(eq_seen)

    thr = thr_ref[:, 0:1]
    need = need_ref[:, 0:1]
    utri = utri_ref[...]
    for b in range(SEL_BLOCKS):
        cols = slice(b * RB, (b + 1) * RB)
        aff = aff_ref[:, cols]
        eq = jnp.where(aff == thr, 1.0, 0.0)
        eq_before = jnp.dot(eq.astype(BF16), utri, preferred_element_type=F32) + eq_seen[...]
        sel = (aff > thr) | ((aff == thr) & (eq_before < need))
        eq_seen[...] += jnp.sum(eq, axis=1, keepdims=True)
        self = jnp.where(sel, 1.0, 0.0)
        rank_ref[:, cols] = jnp.dot(self.astype(BF16), utri, preferred_element_type=F32)
        selw_ref[:, cols] = jnp.where(sel, aff, 0.0)
        m_ref[b] = jnp.broadcast_to(jnp.sum(self, axis=1, keepdims=True), (N_EXPERTS, LANES))


def _select(aff_t, thr, need):
    utri = jnp.asarray(np.triu(np.ones((RB, RB), np.float32), 1), dtype=BF16)
    grp = lambda i: (jnp.where(i >= GROUP0_BLOCKS // SEL_BLOCKS, 1, 0), 0)
    return pl.pallas_call(
        _select_kernel,
        grid=(N_RB // SEL_BLOCKS,),
        in_specs=[
            pl.BlockSpec((N_EXPERTS, SEL_BLOCKS * RB), lambda i: (0, i)),
            pl.BlockSpec((N_EXPERTS, LANES), grp),
            pl.BlockSpec((N_EXPERTS, LANES), grp),
            pl.BlockSpec((RB, RB), lambda i: (0, 0)),
        ],
        out_specs=[
            pl.BlockSpec((N_EXPERTS, SEL_BLOCKS * RB), lambda i: (0, i)),
            pl.BlockSpec((N_EXPERTS, SEL_BLOCKS * RB), lambda i: (0, i)),
            pl.BlockSpec((SEL_BLOCKS, N_EXPERTS, LANES), lambda i: (i, 0, 0)),
        ],
        out_shape=[
            jax.ShapeDtypeStruct((N_EXPERTS, N_TOK), F32),
            jax.ShapeDtypeStruct((N_EXPERTS, N_TOK), F32),
            jax.ShapeDtypeStruct((N_RB, N_EXPERTS, LANES), F32),
        ],
        scratch_shapes=[pltpu.VMEM((N_EXPERTS, 1), F32)],
        compiler_params=_cparams("arbitrary"),
    )(aff_t, thr, need, utri)


def _fill_chunk_list(list_ref, which, rb, seg_start_ref, seg_rows_ref, filler):
    cnt = jnp.int32(0)
    for e in range(N_EXPERTS):
        base = e * SLOT_CHUNKS + seg_start_ref[rb, e] // CHUNK
        n = seg_rows_ref[rb, e] // CHUNK

        def body(k, carry, base=base, cnt=cnt):
            for u in range(LIST_UNROLL):
                list_ref[which, cnt + LIST_UNROLL * k + u] = base + LIST_UNROLL * k + u
            return carry

        lax.fori_loop(0, (n + LIST_UNROLL - 1) // LIST_UNROLL, body, 0)
        cnt = cnt + n
    for j in range(CHUNKS_PER_TILE):
        list_ref[which, cnt + j] = filler(cnt + j)
    return cnt


WIN_ROWS = 128
SPAN_TILES = 8
SPAN_ROWS = SPAN_TILES * TILE_ROWS


def _mark_span(scr, rb, s0, n_rows, tgt, values, seg_off_ref, seg_rows_ref):
    win_rows = lax.broadcasted_iota(jnp.int32, (WIN_ROWS, RB), 0).astype(F32)
    n_here = jnp.minimum(n_rows - s0, SPAN_ROWS)

    def zero(i, carry):
        scr[pl.ds(pl.multiple_of(i * TILE_ROWS, TILE_ROWS), TILE_ROWS), :] = jnp.zeros((TILE_ROWS, RB), F32)
        return carry

    lax.fori_loop(0, (n_here + TILE_ROWS - 1) // TILE_ROWS, zero, 0)
    for e in range(N_EXPERTS):
        lo = seg_off_ref[rb, e]
        first = jnp.maximum(lo, s0)
        last = jnp.minimum(lo + seg_rows_ref[rb, e], s0 + SPAN_ROWS)

        def mark(w, carry, e=e, first=first):
            start = first + w * WIN_ROWS
            win = pl.ds(pl.multiple_of(start - s0, CHUNK), WIN_ROWS)
            scr[win, :] = jnp.where(tgt[e:e + 1, :] == win_rows + start.astype(F32), values[e], scr[win, :])
            return carry

        lax.fori_loop(0, jnp.maximum(last - first + WIN_ROWS - 1, 0) // WIN_ROWS, mark, 0)


def _expert_column(ref, rb):
    e_iota = lax.broadcasted_iota(jnp.int32, (N_EXPERTS, 1), 0)
    col = jnp.zeros((N_EXPERTS, 1), F32)
    for e in range(N_EXPERTS):
        col = jnp.where(e_iota == e, ref[rb, e].astype(F32), col)
    return col


def _gather_kernel(in_start_ref, in_rows_ref, in_off_ref, tot_ref, in_rank_ref, seg_start_ref, seg_len_ref,
                   x_ref, selw_ref, rank_ref, xe_hbm,
                   p_scr, obuf, hbuf, carry, zbuf, dst_list, pending, sem, hsem, zsem):
    rb = pl.program_id(0)
    n_rows = tot_ref[rb]
    n_tiles = (n_rows + TILE_ROWS - 1) // TILE_ROWS
    head_slot = TILE_SLOTS

    @pl.when(rb == 0)
    def _():
        for slot in range(TILE_SLOTS + 1):
            pending[slot] = 0
        p_scr[...] = jnp.zeros_like(p_scr)
        carry[...] = jnp.zeros_like(carry)

    _fill_chunk_list(dst_list, 0, rb, in_start_ref, in_rows_ref,
                     lambda c: TRASH_CHUNK + c % (TILE_SLOTS * CHUNKS_PER_TILE))

    def tile_copy(slot):
        return pltpu.make_async_copy(obuf.at[slot], xe_hbm.at[pl.ds(0, CHUNKS_PER_TILE)], sem.at[slot])

    def head_copy():
        return pltpu.make_async_copy(hbuf, xe_hbm.at[pl.ds(0, N_EXPERTS)], hsem)

    x = x_ref[...]
    rank = rank_ref[...]
    sel = selw_ref[...] > 0.0

    j8 = lax.broadcasted_iota(jnp.int32, (CHUNK, RB), 0).astype(F32)
    heads, tails, flags = [], [], []
    for e in range(N_EXPERTS):
        start = seg_start_ref[rb, e]
        end = start + seg_len_ref[rb, e]
        nonempty = seg_len_ref[rb, e] > 0
        has_head = nonempty & (start % CHUNK != 0)
        has_tail = nonempty & (end % CHUNK != 0)
        single = has_head & has_tail & (start // CHUNK == end // CHUNK)
        head_shift = (start % CHUNK).astype(F32)
        tail_rank0 = ((end // CHUNK) * CHUNK - start).astype(F32)
        r_e = rank[e:e + 1, :]
        s_e = sel[e:e + 1, :]
        heads.append(jnp.where(s_e & (r_e + head_shift == j8), has_head.astype(F32), 0.0))
        tails.append(jnp.where(s_e & (r_e - tail_rank0 == j8), (has_tail & ~single).astype(F32), 0.0))
        flags.append((has_head, has_tail, single, start // CHUNK))
    p_edge = jnp.concatenate(heads + tails, axis=0).astype(BF16)
    edge = jnp.dot(p_edge, x, preferred_element_type=F32)

    @pl.when(pending[head_slot] == 1)
    def _():
        head_copy().wait()

    head_rows = []
    for e, (has_head, has_tail, single, head_chunk) in enumerate(flags):
        rows = slice(e * CHUNK, (e + 1) * CHUNK)
        old = carry[rows, :]
        head = edge[rows, :] + old
        tail = edge[N_EXPERTS * CHUNK + e * CHUNK:N_EXPERTS * CHUNK + (e + 1) * CHUNK, :]
        head_rows.append(head)
        keep = ~single & ~has_tail & ~has_head
        carry[rows, :] = (single.astype(F32) * head + (has_tail & ~single).astype(F32) * tail
                          + keep.astype(F32) * old)
    hbuf[...] = _pack_chunks(jnp.concatenate(head_rows, axis=0))
    for e, (has_head, has_tail, single, head_chunk) in enumerate(flags):
        dst = jnp.where(has_head & ~single, e * SLOT_CHUNKS + head_chunk, HEAD_TRASH_CHUNK + e)
        pltpu.make_async_copy(hbuf.at[e], xe_hbm.at[dst], hsem).start()
    pending[head_slot] = 1

    off = _expert_column(in_off_ref, rb)
    rank0 = _expert_column(in_rank_ref, rb)
    n_in = _expert_column(in_rows_ref, rb)
    tgt = jnp.where(sel & (rank >= rank0) & (rank < rank0 + n_in), rank - rank0 + off, -1.0)

    def tile_body(t, carry_):
        slot = t % TILE_SLOTS

        @pl.when(pending[slot] == 1)
        def _():
            tile_copy(slot).wait()

        @pl.when(t % SPAN_TILES == 0)
        def _():
            _mark_span(p_scr, rb, t * TILE_ROWS, n_rows, tgt, [1.0] * N_EXPERTS, in_off_ref, in_rows_ref)

        p = p_scr[pl.ds(pl.multiple_of((t % SPAN_TILES) * TILE_ROWS, TILE_ROWS), TILE_ROWS), :]
        obuf[slot] = _pack_chunks(jnp.dot(p.astype(BF16), x, preferred_element_type=F32))

        for i in range(CHUNKS_PER_TILE):
            pltpu.make_async_copy(obuf.at[slot, i], xe_hbm.at[dst_list[0, t * CHUNKS_PER_TILE + i]],
                                  sem.at[slot]).start()
        pending[slot] = 1
        return carry_

    lax.fori_loop(0, n_tiles, tile_body, 0)

    @pl.when(rb == N_RB - 1)
    def _():
        for slot in range(TILE_SLOTS):
            @pl.when(pending[slot] == 1)
            def _(slot=slot):
                tile_copy(slot).wait()
        head_copy().wait()
        zbuf[...] = jnp.zeros_like(zbuf)

        def zcopy(k):
            return pltpu.make_async_copy(zbuf, xe_hbm.at[TRASH_CHUNK + k], zsem)

        def zstart(k, c):
            zcopy(k).start()
            return c

        def zwait(k, c):
            zcopy(k).wait()
            return c

        lax.fori_loop(0, TRASH_CHUNKS, zstart, 0)
        lax.fori_loop(0, TRASH_CHUNKS, zwait, 0)


def _route_gather(tables, xn, selw_t, rank_t):
    gs = pltpu.PrefetchScalarGridSpec(
        num_scalar_prefetch=len(tables),
        grid=(N_RB,),
        in_specs=[
            pl.BlockSpec((RB, D_MODEL), lambda rb, *_: (rb, 0)),
            pl.BlockSpec((N_EXPERTS, RB), lambda rb, *_: (0, rb)),
            pl.BlockSpec((N_EXPERTS, RB), lambda rb, *_: (0, rb)),
        ],
        out_specs=pl.BlockSpec(memory_space=pl.ANY),
        scratch_shapes=[
            pltpu.VMEM((SPAN_ROWS + WIN_ROWS, RB), F32),
            pltpu.VMEM((TILE_SLOTS, CHUNKS_PER_TILE) + CHUNK_SHAPE, BF16),
            pltpu.VMEM((N_EXPERTS,) + CHUNK_SHAPE, BF16),
            pltpu.VMEM((N_EXPERTS * CHUNK, D_MODEL), F32),
            pltpu.VMEM(CHUNK_SHAPE, BF16),
            pltpu.SMEM((1, LIST_LEN), jnp.int32),
            pltpu.SMEM((TILE_SLOTS + 1,), jnp.int32),
            pltpu.SemaphoreType.DMA((TILE_SLOTS,)),
            pltpu.SemaphoreType.DMA,
            pltpu.SemaphoreType.DMA,
        ],
    )
    return pl.pallas_call(
        _gather_kernel,
        grid_spec=gs,
        out_shape=jax.ShapeDtypeStruct((TRASH_CHUNK + TRASH_CHUNKS,) + CHUNK_SHAPE, BF16),
        compiler_params=_cparams("arbitrary"),
    )(*tables, xn, selw_t, rank_t)


W_PIECES_PER_MATRIX = 4
W_PIECES = 3 * W_PIECES_PER_MATRIX
FFN_STEPS = max(SLOT_TILES, W_PIECES + 1)
WG_PIECE_ROWS = D_MODEL // W_PIECES_PER_MATRIX
WD_PIECE_ROWS = EXPERT_D_FF // W_PIECES_PER_MATRIX
FFN_VMEM_LIMIT = 56 * 1024 * 1024


def _ffn_kernel(x_ref, wg_hbm, wu_hbm, wd_hbm, o_ref,
                wg_bf, wu_bf, wd_bf, stage_up, stage_dn, sem):
    e = pl.program_id(0)
    i = pl.program_id(1)
    slot = e % 2

    matrices = ((wg_hbm, stage_up, wg_bf, WG_PIECE_ROWS),
                (wu_hbm, stage_up, wu_bf, WG_PIECE_ROWS),
                (wd_hbm, stage_dn, wd_bf, WD_PIECE_ROWS))

    def piece(expert, k, buf, finish):
        def of_kind(kind):
            w_hbm, stage, w_bf, rows = matrices[kind]
            r0 = pl.multiple_of((k - kind * W_PIECES_PER_MATRIX) * rows, rows)
            cp = pltpu.make_async_copy(w_hbm.at[expert, pl.ds(r0, rows), :], stage.at[buf], sem.at[buf])
            if finish:
                cp.wait()
                w_bf[expert % 2, pl.ds(r0, rows), :] = stage[buf].astype(BF16)
            else:
                cp.start()

        if isinstance(k, int):
            of_kind(k // W_PIECES_PER_MATRIX)
        else:
            for kind in range(len(matrices)):
                pl.when(k // W_PIECES_PER_MATRIX == kind)(functools.partial(of_kind, kind))

    @pl.when((e == 0) & (i == 0))
    def _():
        for k in range(W_PIECES):
            piece(0, k, k % 2, finish=False)
            piece(0, k, k % 2, finish=True)

    @pl.when(e + 1 < N_EXPERTS)
    def _():
        @pl.when((i >= 1) & (i <= W_PIECES))
        def _():
            piece(e + 1, i - 1, (i - 1) % 2, finish=True)

        @pl.when(i < W_PIECES)
        def _():
            piece(e + 1, i, i % 2, finish=False)

    @pl.when(i < SLOT_TILES)
    def _():
        x = _unpack_chunks(x_ref[...])
        acc = jnp.zeros((FF_TM, D_MODEL), F32)
        for c in range(EXPERT_D_FF // FF_FC):
            gt = jnp.dot(x, wg_bf[slot, :, c * FF_FC:(c + 1) * FF_FC], preferred_element_type=F32)
            up = jnp.dot(x, wu_bf[slot, :, c * FF_FC:(c + 1) * FF_FC], preferred_element_type=F32)
            h = (gt * (1.0 / (1.0 + jnp.exp(-gt))) * up).astype(BF16)
            acc = acc + jnp.dot(h, wd_bf[slot, c * FF_FC:(c + 1) * FF_FC, :], preferred_element_type=F32)
        o_ref[...] = _pack_chunks(acc)


def _ffn(xe, w_gate, w_up, w_down):
    def rows(e, i):
        return (e * SLOT_TILES + jnp.minimum(i, SLOT_TILES - 1), 0, 0)

    tile_chunks = (FF_TM // CHUNK,) + CHUNK_SHAPE
    gs = pltpu.PrefetchScalarGridSpec(
        num_scalar_prefetch=0,
        grid=(N_EXPERTS, FFN_STEPS),
        in_specs=[
            pl.BlockSpec(tile_chunks, rows),
            pl.BlockSpec(memory_space=pl.ANY),
            pl.BlockSpec(memory_space=pl.ANY),
            pl.BlockSpec(memory_space=pl.ANY),
        ],
        out_specs=pl.BlockSpec(tile_chunks, rows),
        scratch_shapes=[
            pltpu.VMEM((2, D_MODEL, EXPERT_D_FF), BF16),
            pltpu.VMEM((2, D_MODEL, EXPERT_D_FF), BF16),
            pltpu.VMEM((2, EXPERT_D_FF, D_MODEL), BF16),
            pltpu.VMEM((2, WG_PIECE_ROWS, EXPERT_D_FF), F32),
            pltpu.VMEM((2, WD_PIECE_ROWS, D_MODEL), F32),
            pltpu.SemaphoreType.DMA((2,)),
        ],
    )
    return pl.pallas_call(
        _ffn_kernel,
        grid_spec=gs,
        out_shape=jax.ShapeDtypeStruct((N_EXPERTS * SLOT_CHUNKS,) + CHUNK_SHAPE, BF16),
        compiler_params=pltpu.CompilerParams(dimension_semantics=("arbitrary", "arbitrary"),
                                             vmem_limit_bytes=FFN_VMEM_LIMIT),
    )(xe, w_gate, w_up, w_down)


def _combine_kernel(seg_start_ref, seg_rows_ref, seg_off_ref, tot_ref, seg_shift_ref,
                    x1_ref, selw_ref, rank_ref, g_ref, ye_hbm, op_ref, os_ref,
                    w_scr, ybuf, acc, src_list, state, sem):
    rb = pl.program_id(0)
    cur = rb % 2
    nxt_rb = jnp.minimum(rb + 1, N_RB - 1)

    def tiles_of(b):
        return (tot_ref[b] + TILE_ROWS - 1) // TILE_ROWS

    n_tiles = tiles_of(rb)
    n_tiles_next = jnp.where(rb + 1 < N_RB, tiles_of(nxt_rb), 0)

    def fetch(which, t, slot):
        for i in range(CHUNKS_PER_TILE):
            pltpu.make_async_copy(ye_hbm.at[src_list[which, t * CHUNKS_PER_TILE + i]], ybuf.at[slot, i],
                                  sem.at[slot]).start()

    def tile_wait(slot):
        pltpu.make_async_copy(ye_hbm.at[pl.ds(0, CHUNKS_PER_TILE)], ybuf.at[slot], sem.at[slot]).wait()

    @pl.when(rb == 0)
    def _():
        w_scr[...] = jnp.zeros_like(w_scr)
        _fill_chunk_list(src_list, 0, 0, seg_start_ref, seg_rows_ref, lambda c: 0)
        state[0] = 0
        state[1] = 0

    @pl.when(rb + 1 < N_RB)
    def _():
        _fill_chunk_list(src_list, 1 - cur, nxt_rb, seg_start_ref, seg_rows_ref, lambda c: 0)

    slot0 = state[0]
    started = state[1]
    for j in range(LOOKAHEAD):
        @pl.when((j >= started) & (j < n_tiles))
        def _(j=j):
            fetch(cur, j, (slot0 + j) % TILE_SLOTS)

    off = _expert_column(seg_off_ref, rb) + _expert_column(seg_shift_ref, rb)
    selw = selw_ref[...]
    gates = [selw[e:e + 1, :] for e in range(N_EXPERTS)]
    tgt = jnp.where(selw > 0.0, rank_ref[...] + off, -1.0)
    acc[...] = x1_ref[...]

    def tile_body(t, next_started):
        slot = (slot0 + t) % TILE_SLOTS
        ahead = t + LOOKAHEAD

        @pl.when(ahead < n_tiles)
        def _():
            fetch(cur, ahead, (slot0 + ahead) % TILE_SLOTS)

        start_next = (ahead >= n_tiles) & (next_started < jnp.minimum(LOOKAHEAD, n_tiles_next))

        @pl.when(start_next)
        def _():
            fetch(1 - cur, next_started, (slot0 + n_tiles + next_started) % TILE_SLOTS)

        @pl.when(t % SPAN_TILES == 0)
        def _():
            _mark_span(w_scr, rb, t * TILE_ROWS, tot_ref[rb], tgt, gates, seg_off_ref, seg_rows_ref)

        tile_wait(slot)
        w = w_scr[pl.ds(pl.multiple_of((t % SPAN_TILES) * TILE_ROWS, TILE_ROWS), TILE_ROWS), :]
        acc[...] += lax.dot_general(w.astype(BF16), _unpack_chunks(ybuf[slot]),
                                    (((0,), (0,)), ((), ())), preferred_element_type=F32)
        return next_started + start_next.astype(jnp.int32)

    state[1] = lax.fori_loop(0, n_tiles, tile_body, jnp.int32(0))
    state[0] = (slot0 + n_tiles) % TILE_SLOTS
    x2 = acc[...]
    ms = jnp.mean(x2 * x2, axis=-1, keepdims=True)
    y = x2 * lax.rsqrt(ms + EPS) * g_ref[...]

    @pl.when(rb < GROUP0_BLOCKS)
    def _():
        op_ref[...] = y

    @pl.when(rb >= GROUP0_BLOCKS)
    def _():
        os_ref[...] = y


def _combine(tables, x1, selw_t, rank_t, g, ye):
    gs = pltpu.PrefetchScalarGridSpec(
        num_scalar_prefetch=len(tables),
        grid=(N_RB,),
        in_specs=[
            pl.BlockSpec((RB, D_MODEL), lambda rb, *_: (rb, 0)),
            pl.BlockSpec((N_EXPERTS, RB), lambda rb, *_: (0, rb)),
            pl.BlockSpec((N_EXPERTS, RB), lambda rb, *_: (0, rb)),
            pl.BlockSpec((1, D_MODEL), lambda rb, *_: (0, 0)),
            pl.BlockSpec(memory_space=pl.ANY),
        ],
        out_specs=[
            pl.BlockSpec((RB, D_MODEL), lambda rb, *_: (jnp.minimum(rb, GROUP0_BLOCKS - 1), 0)),
            pl.BlockSpec((RB, D_MODEL), lambda rb, *_: (jnp.maximum(rb - GROUP0_BLOCKS, 0), 0)),
        ],
        scratch_shapes=[
            pltpu.VMEM((SPAN_ROWS + WIN_ROWS, RB), F32),
            pltpu.VMEM((TILE_SLOTS, CHUNKS_PER_TILE) + CHUNK_SHAPE, BF16),
            pltpu.VMEM((RB, D_MODEL), F32),
            pltpu.SMEM((2, LIST_LEN), jnp.int32),
            pltpu.SMEM((2,), jnp.int32),
            pltpu.SemaphoreType.DMA((TILE_SLOTS,)),
        ],
    )
    return pl.pallas_call(
        _combine_kernel,
        grid_spec=gs,
        out_shape=[jax.ShapeDtypeStruct((GROUP_TOKENS[0], D_MODEL), F32),
                   jax.ShapeDtypeStruct((GROUP_TOKENS[1], D_MODEL), F32)],
        compiler_params=_cparams("arbitrary"),
    )(*tables, x1, selw_t, rank_t, g, ye)


def kernel(x_prompt, x_sample, norm_mix_g, w_in, w_fourier, sink, w_out, norm_ffn_g, w_router,
           w_gate, w_up, w_down, norm_final_g):
    assert x_prompt.shape == (1, SEQ, D_MODEL) and x_sample.shape == (N_SEQ - 1, SEQ, D_MODEL)
    assert w_in.shape == (1, D_MODEL, IN_WIDTH) and w_out.shape == (1, MIX_WIDTH, D_MODEL)
    assert w_gate.shape == (1, N_EXPERTS, D_MODEL, EXPERT_D_FF) and w_down.shape == (1, N_EXPERTS, EXPERT_D_FF, D_MODEL)
    xp = x_prompt.reshape(-1, D_MODEL)
    xs = x_sample.reshape(-1, D_MODEL)
    cos_t, sin_t = _rope_tables()

    mix_cols = np.concatenate([
        np.arange(FOURIER_WIDTH),
        FOURIER_WIDTH + (np.asarray(Q_ORDER)[:, None] * HEAD_DIM + np.arange(HEAD_DIM)[None, :]).reshape(-1)])
    in_cols = np.concatenate([mix_cols, np.arange(MIX_WIDTH, IN_WIDTH)])
    w_in_r = w_in[0][:, in_cols].astype(BF16)
    w_out_r = w_out[0][mix_cols, :].astype(BF16)

    u, q, k, v = _inproj(xp, xs, norm_mix_g[0][None, :], w_in_r, cos_t, sin_t)
    yf = _fourier(u, w_fourier[0].astype(BF16))
    ya = _attention(sink[0], q, k, v)
    x1, xn, aff_t = _outproj(xp, xs, yf, ya, w_out_r, norm_ffn_g[0][None, :],
                             w_router[0].T.astype(BF16))

    thr0, need0 = _thresholds(aff_t[:, :GROUP_TOKENS[0]])
    thr1, need1 = _thresholds(aff_t[:, GROUP_TOKENS[0]:])
    selw_t, rank_t, m = _select(aff_t, jnp.concatenate([thr0, thr1]), jnp.concatenate([need0, need1]))

    seg_len = m[:, :, 0].astype(jnp.int32)
    seg_start = jnp.cumsum(seg_len, axis=0) - seg_len
    seg_end = seg_start + seg_len

    def offsets(rows):
        return jnp.cumsum(rows, axis=1) - rows, jnp.sum(rows, axis=1)

    in_start = -(-seg_start // CHUNK) * CHUNK
    in_rows = jnp.maximum(seg_end // CHUNK * CHUNK - in_start, 0)
    in_off, in_tot = offsets(in_rows)
    gather_tables = (in_start, in_rows, in_off, in_tot, in_start - seg_start, seg_start, seg_len)
    out_start = seg_start // CHUNK * CHUNK
    out_rows = jnp.where(seg_len > 0, -(-seg_end // CHUNK) * CHUNK - out_start, 0)
    out_off, out_tot = offsets(out_rows)
    combine_tables = (out_start, out_rows, out_off, out_tot, seg_start - out_start)

    xe = _route_gather(gather_tables, xn, selw_t, rank_t)
    ye = _ffn(xe, w_gate[0], w_up[0], w_down[0])
    yp, ys = _combine(combine_tables, x1, selw_t, rank_t, norm_final_g[None, :], ye)
    return (yp.reshape(x_prompt.shape), ys.reshape(x_sample.shape))
```

```python
import functools

import numpy as np
import jax
import jax.numpy as jnp
from jax import lax
from jax.experimental import pallas as pl
from jax.experimental.pallas import tpu as pltpu

F32 = jnp.float32
BF16 = jnp.bfloat16

D_MODEL = 1024
SEQ = 16384
N_SEQ = 3
N_TOK = N_SEQ * SEQ
HEAD_DIM = 64
N_HEADS = 12
N_KV_HEADS = 4
GQA_GROUP = N_HEADS // N_KV_HEADS
ATTN_WIDTH = N_HEADS * HEAD_DIM
KV_WIDTH = N_KV_HEADS * HEAD_DIM
FOURIER_WIDTH = 256
FOURIER_GROUP_DIM = 64
MIX_WIDTH = FOURIER_WIDTH + ATTN_WIDTH
IN_WIDTH = MIX_WIDTH + 2 * KV_WIDTH
WINDOW = 128
ROPE_THETA = 10000.0
N_EXPERTS = 16
EC_CAPACITY_FACTOR = 2
EXPERT_D_FF = 2048
EPS = 1e-6
NEG = -1e30

LANES = 128
FFT_R = 128
VMEM_LIMIT = 48 * 1024 * 1024

TM = 1024
TQ = 1024
QB = 128
FF_TM = 512
FF_FC = 512


def _cparams(*sem):
    return pltpu.CompilerParams(dimension_semantics=sem, vmem_limit_bytes=VMEM_LIMIT)


def _x_specs():
    n_p = SEQ // TM
    return [pl.BlockSpec((TM, D_MODEL), lambda i, *_: (jnp.minimum(i, n_p - 1), 0)),
            pl.BlockSpec((TM, D_MODEL), lambda i, *_: (jnp.maximum(i - n_p, 0), 0))]


def _x_tile(xp_ref, xs_ref):
    return jnp.where(pl.program_id(0) < SEQ // TM, xp_ref[...], xs_ref[...])


def _inproj_kernel(xp_ref, xs_ref, g_ref, w_ref, cos_ref, sin_ref, u_ref, q_ref, k_ref, v_ref):
    x = _x_tile(xp_ref, xs_ref)
    ms = jnp.mean(x * x, axis=-1, keepdims=True)
    hn = (x * lax.rsqrt(ms + EPS) * g_ref[...]).astype(BF16)
    proj = jnp.dot(hn, w_ref[...], preferred_element_type=F32)
    u_ref[...] = proj[:, :FOURIER_WIDTH].astype(BF16)
    cos = cos_ref[...]
    sin = sin_ref[...]
    lane = lax.broadcasted_iota(jnp.int32, cos.shape, 1)
    first_half = (lane % HEAD_DIM) < (HEAD_DIM // 2)

    def rope(t):
        partner = jnp.where(first_half,
                            pltpu.roll(t, LANES - HEAD_DIM // 2, 1),
                            pltpu.roll(t, HEAD_DIM // 2, 1))
        return t * cos + partner * sin

    scale = HEAD_DIM ** -0.5 * LOG2E
    for c in range(ATTN_WIDTH // LANES):
        lo = FOURIER_WIDTH + c * LANES
        q_ref[:, c * LANES:(c + 1) * LANES] = (rope(proj[:, lo:lo + LANES]) * scale).astype(BF16)
    low_half = lane < HEAD_DIM
    for c in range(KV_WIDTH // LANES):
        lo = FOURIER_WIDTH + ATTN_WIDTH + c * LANES
        kc = rope(proj[:, lo:lo + LANES])
        vc = proj[:, lo + KV_WIDTH:lo + KV_WIDTH + LANES]
        for half, keep in enumerate((low_half, ~low_half)):
            g = 2 * c + half
            k_ref[:, g * LANES:(g + 1) * LANES] = jnp.where(keep, kc, 0.0).astype(BF16)
            v_ref[:, g * LANES:(g + 1) * LANES] = jnp.where(keep, vc, 0.0).astype(BF16)


def _inproj(xp, xs, g, w_in, cos_t, sin_t):
    n_pos_blocks = SEQ // TM
    return pl.pallas_call(
        _inproj_kernel,
        grid=(N_TOK // TM,),
        in_specs=_x_specs() + [
            pl.BlockSpec((1, D_MODEL), lambda i: (0, 0)),
            pl.BlockSpec((D_MODEL, IN_WIDTH), lambda i: (0, 0)),
            pl.BlockSpec((TM, LANES), lambda i: (i % n_pos_blocks, 0)),
            pl.BlockSpec((TM, LANES), lambda i: (i % n_pos_blocks, 0)),
        ],
        out_specs=[
            pl.BlockSpec((TM, FOURIER_WIDTH), lambda i: (i, 0)),
            pl.BlockSpec((TM, ATTN_WIDTH), lambda i: (i, 0)),
            pl.BlockSpec((TM, KV4), lambda i: (i, 0)),
            pl.BlockSpec((TM, KV4), lambda i: (i, 0)),
        ],
        out_shape=[
            jax.ShapeDtypeStruct((N_TOK, FOURIER_WIDTH), BF16),
            jax.ShapeDtypeStruct((N_TOK, ATTN_WIDTH), BF16),
            jax.ShapeDtypeStruct((N_TOK, KV4), BF16),
            jax.ShapeDtypeStruct((N_TOK, KV4), BF16),
        ],
        compiler_params=_cparams("parallel"),
    )(xp, xs, g, w_in, cos_t, sin_t)


def _rope_tables():
    half = HEAD_DIM // 2
    inv_freq = 1.0 / (ROPE_THETA ** (jnp.arange(half, dtype=F32) / half))
    ang = jnp.arange(SEQ, dtype=F32)[:, None] * inv_freq[None, :]
    cos = jnp.cos(ang)
    sin = jnp.sin(ang)
    reps = LANES // HEAD_DIM
    cos_t = jnp.tile(jnp.concatenate([cos, cos], axis=-1), (1, reps))
    sin_t = jnp.tile(jnp.concatenate([-sin, sin], axis=-1), (1, reps))
    return cos_t, sin_t


def _dft_tables():
    r = FFT_R
    c = np.arange(FOURIER_GROUP_DIM)
    ang_c = 2.0 * np.pi * np.outer(c, c) / FOURIER_GROUP_DIM
    n_groups = FOURIER_WIDTH // FOURIER_GROUP_DIM
    eye = np.eye(n_groups)
    ch_scale = FOURIER_GROUP_DIM ** -0.5
    c_blk = np.kron(eye, np.cos(ang_c)) * ch_scale
    s_blk = np.kron(eye, np.sin(ang_c)) * ch_scale
    cs = np.concatenate([c_blk, -s_blk], axis=1)
    k = np.arange(r)
    ang_r = 2.0 * np.pi * np.outer(k, k) / r
    st_scale = r ** -0.5
    wr = np.cos(ang_r) * st_scale
    wi = -np.sin(ang_r) * st_scale
    wbig = np.block([[wr, -wi], [wi, wr]])
    ang_t = 2.0 * np.pi * np.outer(k, k) / (r * r)
    tr = np.cos(ang_t)
    ti = -np.sin(ang_t)
    f = lambda a: jnp.asarray(a, dtype=F32)
    return f(cs).astype(BF16), f(wbig).astype(BF16), f(wr), f(wi), f(tr), f(ti)


FS_SB = 16
FS_KB = 16


def _fft_stage1_kernel(u_ref, cs_ref, wbig_ref, y_ref):
    cs = cs_ref[...]
    wbig = wbig_ref[...]
    fw = FOURIER_WIDTH
    for j in range(FS_SB):
        u = u_ref[:, j * fw:(j + 1) * fw]
        g = jnp.dot(u, cs, preferred_element_type=F32)
        gst = jnp.concatenate([g[:, :fw], g[:, fw:]], axis=0).astype(BF16)
        y = jnp.dot(wbig, gst, preferred_element_type=F32)
        y_ref[:, j * 2 * fw:j * 2 * fw + fw] = y[:FFT_R].astype(BF16)
        y_ref[:, j * 2 * fw + fw:(j + 1) * 2 * fw] = y[FFT_R:].astype(BF16)


def _fft_stage2_kernel(y_ref, wr_ref, wi_ref, tr_ref, ti_ref, wf_ref, o_ref):
    kb = pl.program_id(1)
    wr = wr_ref[...]
    wi = wi_ref[...]
    wf = wf_ref[...]
    fw = FOURIER_WIDTH
    for j in range(FS_KB):
        k1 = kb * FS_KB + j
        tr = tr_ref[pl.ds(k1, 1), :]
        ti = ti_ref[pl.ds(k1, 1), :]
        mr = wr * tr - wi * ti
        mi = wr * ti + wi * tr
        m2 = jnp.concatenate([mr, -mi], axis=1).astype(BF16)
        yb = y_ref[j * FFT_R:(j + 1) * FFT_R, :]
        yst = jnp.concatenate([yb[:, :fw], yb[:, fw:]], axis=0)
        z = jnp.dot(m2, yst, preferred_element_type=F32)
        yf = jnp.dot(z.astype(BF16), wf, preferred_element_type=F32)
        o_ref[:, j * fw:(j + 1) * fw] = yf.astype(BF16)


def _fourier(u, w_fourier):
    cs, wbig, wr, wi, tr, ti = _dft_tables()
    r, fw = FFT_R, FOURIER_WIDTH
    u2 = u.reshape(N_SEQ * r, r * fw)
    y = pl.pallas_call(
        _fft_stage1_kernel,
        grid=(N_SEQ, r // FS_SB),
        in_specs=[
            pl.BlockSpec((r, FS_SB * fw), lambda s, j: (s, j)),
            pl.BlockSpec((fw, 2 * fw), lambda s, j: (0, 0)),
            pl.BlockSpec((2 * r, 2 * r), lambda s, j: (0, 0)),
        ],
        out_specs=pl.BlockSpec((r, FS_SB * 2 * fw), lambda s, j: (s, j)),
        out_shape=jax.ShapeDtypeStruct((N_SEQ * r, r * 2 * fw), BF16),
        compiler_params=_cparams("parallel", "parallel"),
    )(u2, cs, wbig)
    y2 = y.reshape(N_SEQ * r * r, 2 * fw)
    full = lambda s, j: (0, 0)
    yf = pl.pallas_call(
        _fft_stage2_kernel,
        grid=(N_SEQ, r // FS_KB),
        in_specs=[
            pl.BlockSpec((FS_KB * r, 2 * fw), lambda s, j: (s * (r // FS_KB) + j, 0)),
            pl.BlockSpec((r, r), full),
            pl.BlockSpec((r, r), full),
            pl.BlockSpec((r, r), full),
            pl.BlockSpec((r, r), full),
            pl.BlockSpec((fw, fw), full),
        ],
        out_specs=pl.BlockSpec((r, FS_KB * fw), lambda s, j: (s, j)),
        out_shape=jax.ShapeDtypeStruct((N_SEQ * r, r * fw), BF16),
        compiler_params=_cparams("parallel", "parallel"),
    )(y2, wr, wi, tr, ti, w_fourier)
    return yf.reshape(N_TOK, fw)


LO_HEADS = (0, 1, 2, 6, 7, 8)
HI_HEADS = (3, 4, 5, 9, 10, 11)
Q_ORDER = tuple(h for pair in zip(LO_HEADS, HI_HEADS) for h in pair)
assert all(h // GQA_GROUP % 2 == 0 for h in LO_HEADS) and all(h // GQA_GROUP % 2 == 1 for h in HI_HEADS)
KV4 = 2 * KV_WIDTH
LOG2E = 1.4426950408889634


def _attn_kernel(sink_ref, q_ref, kp_ref, kc_ref, kn_ref, vp_ref, vc_ref, vn_ref, o_ref,
                 kext, vext):
    i = pl.program_id(1)
    n_i = pl.num_programs(1)
    kext[0:QB, :] = kp_ref[...]
    kext[QB:QB + TQ, :] = kc_ref[...]
    kext[QB + TQ:, :] = kn_ref[...]
    vext[0:QB, :] = vp_ref[...]
    vext[QB:QB + TQ, :] = vc_ref[...]
    vext[QB + TQ:, :] = vn_ref[...]

    rows = GQA_GROUP * QB
    a = lax.broadcasted_iota(jnp.int32, (rows, QB), 0) % QB
    j = lax.broadcasted_iota(jnp.int32, (rows, QB), 1)
    band_prev = jnp.where(j >= a, 0.0, NEG)
    band_next = jnp.where(j <= a, 0.0, NEG)
    row_id = lax.broadcasted_iota(jnp.int32, (rows, 1), 0)
    low_half = lax.broadcasted_iota(jnp.int32, (rows, LANES), 1) < HEAD_DIM

    def softmax_half(s, sinks, bias_prev, bias_next):
        s_p = s[:, :QB] + bias_prev
        s_c = s[:, QB:2 * QB]
        s_n = s[:, 2 * QB:] + bias_next
        sink = jnp.where(row_id < QB, sinks[0], jnp.where(row_id < 2 * QB, sinks[1], sinks[2])) * LOG2E
        m = jnp.max(jnp.maximum(jnp.maximum(s_p, s_c), s_n), axis=-1, keepdims=True)
        m = jnp.maximum(m, sink)
        p_p = jnp.exp2(s_p - m)
        p_c = jnp.exp2(s_c - m)
        p_n = jnp.exp2(s_n - m)
        denom = jnp.sum(p_p + p_c + p_n, axis=-1, keepdims=True) + jnp.exp2(sink - m)
        return [p_p.astype(BF16), p_c.astype(BF16), p_n.astype(BF16)], 1.0 / denom

    for sb in range(TQ // QB):
        blk = i * (TQ // QB) + sb
        bias_prev = band_prev + jnp.where(blk == 0, NEG, 0.0)
        bias_next = band_next + jnp.where(blk == n_i * (TQ // QB) - 1, NEG, 0.0)
        for pr in range(N_KV_HEADS // 2):
            groups = [GQA_GROUP * pr + m for m in range(GQA_GROUP)]
            q3 = jnp.concatenate([q_ref[sb * QB:(sb + 1) * QB, g * LANES:(g + 1) * LANES] for g in groups],
                                 axis=0)
            kp = kext[sb * QB:sb * QB + 3 * QB, pr * 2 * LANES:(pr + 1) * 2 * LANES]
            vp = vext[sb * QB:sb * QB + 3 * QB, pr * 2 * LANES:(pr + 1) * 2 * LANES]
            kcat = jnp.concatenate([kp[:, :LANES], kp[:, LANES:]], axis=0)
            vcat = jnp.concatenate([vp[:, :LANES], vp[:, LANES:]], axis=0)
            s = lax.dot_general(q3, kcat, (((1,), (1,)), ((), ())), preferred_element_type=F32)
            p_lo, r_lo = softmax_half(s[:, :3 * QB], [sink_ref[LO_HEADS[g]] for g in groups],
                                      bias_prev, bias_next)
            p_hi, r_hi = softmax_half(s[:, 3 * QB:], [sink_ref[HI_HEADS[g]] for g in groups],
                                      bias_prev, bias_next)
            p = jnp.concatenate(p_lo + p_hi, axis=1)
            o = jnp.dot(p, vcat, preferred_element_type=F32) * jnp.where(low_half, r_lo, r_hi)
            for m, g in enumerate(groups):
                o_ref[sb * QB:(sb + 1) * QB, g * LANES:(g + 1) * LANES] = o[m * QB:(m + 1) * QB].astype(BF16)


def _attention(sink, q, k, v):
    nqb = SEQ // QB
    per = TQ // QB
    cur = lambda s, i: (s * (SEQ // TQ) + i, 0)
    prev = lambda s, i: (s * nqb + jnp.maximum(i * per - 1, 0), 0)
    nxt = lambda s, i: (s * nqb + jnp.minimum(i * per + per, nqb - 1), 0)
    return pl.pallas_call(
        _attn_kernel,
        grid=(N_SEQ, SEQ // TQ),
        in_specs=[
            pl.BlockSpec(memory_space=pltpu.SMEM),
            pl.BlockSpec((TQ, ATTN_WIDTH), cur),
            pl.BlockSpec((QB, KV4), prev),
            pl.BlockSpec((TQ, KV4), cur),
            pl.BlockSpec((QB, KV4), nxt),
            pl.BlockSpec((QB, KV4), prev),
            pl.BlockSpec((TQ, KV4), cur),
            pl.BlockSpec((QB, KV4), nxt),
        ],
        out_specs=pl.BlockSpec((TQ, ATTN_WIDTH), cur),
        out_shape=jax.ShapeDtypeStruct((N_TOK, ATTN_WIDTH), BF16),
        scratch_shapes=[pltpu.VMEM((TQ + 2 * QB, KV4), BF16),
                        pltpu.VMEM((TQ + 2 * QB, KV4), BF16)],
        compiler_params=_cparams("parallel", "parallel"),
    )(sink, q, k, k, k, v, v, v)


def _outproj_kernel(xp_ref, xs_ref, yf_ref, ya_ref, wo_ref, g_ref, wr_ref, x1_ref, xn_ref, aff_ref):
    mix = jnp.dot(yf_ref[...], wo_ref[:FOURIER_WIDTH, :], preferred_element_type=F32)
    mix = mix + jnp.dot(ya_ref[...], wo_ref[FOURIER_WIDTH:, :], preferred_element_type=F32)
    x1 = _x_tile(xp_ref, xs_ref) + mix
    x1_ref[...] = x1
    ms = jnp.mean(x1 * x1, axis=-1, keepdims=True)
    xn = (x1 * lax.rsqrt(ms + EPS) * g_ref[...]).astype(BF16)
    xn_ref[...] = xn
    logits = lax.dot_general(wr_ref[...], xn, (((1,), (1,)), ((), ())), preferred_element_type=F32)
    mx = jnp.max(logits, axis=0, keepdims=True)
    ex = jnp.exp(logits - mx)
    aff_ref[...] = ex / jnp.sum(ex, axis=0, keepdims=True)


def _outproj(xp, xs, yf, ya, w_out, g, w_router_t):
    return pl.pallas_call(
        _outproj_kernel,
        grid=(N_TOK // TM,),
        in_specs=_x_specs() + [
            pl.BlockSpec((TM, FOURIER_WIDTH), lambda i: (i, 0)),
            pl.BlockSpec((TM, ATTN_WIDTH), lambda i: (i, 0)),
            pl.BlockSpec((D_MODEL, D_MODEL), lambda i: (0, 0)),
            pl.BlockSpec((1, D_MODEL), lambda i: (0, 0)),
            pl.BlockSpec((N_EXPERTS, D_MODEL), lambda i: (0, 0)),
        ],
        out_specs=[
            pl.BlockSpec((TM, D_MODEL), lambda i: (i, 0)),
            pl.BlockSpec((TM, D_MODEL), lambda i: (i, 0)),
            pl.BlockSpec((N_EXPERTS, TM), lambda i: (0, i)),
        ],
        out_shape=[
            jax.ShapeDtypeStruct((N_TOK, D_MODEL), F32),
            jax.ShapeDtypeStruct((N_TOK, D_MODEL), BF16),
            jax.ShapeDtypeStruct((N_EXPERTS, N_TOK), F32),
        ],
        compiler_params=_cparams("parallel"),
    )(xp, xs, yf, ya, w_out, g, w_router_t)


RB = 512
N_RB = N_TOK // RB
GROUP_TOKENS = (SEQ, 2 * SEQ)
GROUP0_BLOCKS = GROUP_TOKENS[0] // RB
SEL_BLOCKS = 4
assert GROUP0_BLOCKS % SEL_BLOCKS == 0 and N_RB % SEL_BLOCKS == 0
CHUNK = 8
TILE_ROWS = 256
CHUNKS_PER_TILE = TILE_ROWS // CHUNK
SLOT_ROWS = EC_CAPACITY_FACTOR * N_TOK // N_EXPERTS
assert all((EC_CAPACITY_FACTOR * n // N_EXPERTS) % CHUNK == 0 for n in GROUP_TOKENS)
SLOT_TILES = SLOT_ROWS // FF_TM
assert SLOT_TILES * FF_TM == SLOT_ROWS
SLOT_CHUNKS = SLOT_ROWS // CHUNK
TILE_SLOTS = 4
LOOKAHEAD = TILE_SLOTS - 1
TRASH_CHUNK = N_EXPERTS * SLOT_CHUNKS
HEAD_TRASH_CHUNK = TRASH_CHUNK + TILE_SLOTS * CHUNKS_PER_TILE
TRASH_CHUNKS = TILE_SLOTS * CHUNKS_PER_TILE + FF_TM // CHUNK
assert N_EXPERTS <= FF_TM // CHUNK and TRASH_CHUNKS % (FF_TM // CHUNK) == 0
MAX_CHUNKS = (N_EXPERTS * (RB + 2 * CHUNK)) // CHUNK
LIST_UNROLL = 4
LIST_LEN = (MAX_CHUNKS // CHUNKS_PER_TILE + 2) * CHUNKS_PER_TILE
CHUNK_SHAPE = (2 * CHUNK, D_MODEL // 2)


def _pack_chunks(rows_f32):
    n = rows_f32.shape[0] // CHUNK
    r3 = rows_f32.reshape(n, CHUNK, D_MODEL)
    half = D_MODEL // 2
    return jnp.concatenate([r3[:, :, :half], r3[:, :, half:]], axis=1).astype(BF16)


def _unpack_chunks(chunks_bf16):
    n = chunks_bf16.shape[0]
    c3 = chunks_bf16.astype(F32)
    left = c3[:, :CHUNK, :].reshape(n * CHUNK, D_MODEL // 2)
    right = c3[:, CHUNK:, :].reshape(n * CHUNK, D_MODEL // 2)
    return jnp.concatenate([left, right], axis=1).astype(BF16)


def _thresh_kernel(aff_ref, thr_ref, need_ref, *, cap):
    aff = aff_ref[...]

    def body(i, cur):
        cand = cur | jnp.left_shift(jnp.int32(1), 30 - i)
        cnt = jnp.sum(jnp.where(aff >= pltpu.bitcast(cand, F32), 1.0, 0.0), axis=1, keepdims=True)
        return jnp.where(cnt >= cap, cand, cur)

    thr = pltpu.bitcast(lax.fori_loop(0, 31, body, jnp.zeros((N_EXPERTS, 1), jnp.int32)), F32)
    n_gt = jnp.sum(jnp.where(aff > thr, 1.0, 0.0), axis=1, keepdims=True)
    thr_ref[...] = jnp.broadcast_to(thr, thr_ref.shape)
    need_ref[...] = jnp.broadcast_to(cap - n_gt, need_ref.shape)


def _thresholds(aff_group):
    n = aff_group.shape[1]
    cap = EC_CAPACITY_FACTOR * n // N_EXPERTS
    return pl.pallas_call(
        functools.partial(_thresh_kernel, cap=float(cap)),
        out_shape=[jax.ShapeDtypeStruct((N_EXPERTS, LANES), F32),
                   jax.ShapeDtypeStruct((N_EXPERTS, LANES), F32)],
        compiler_params=pltpu.CompilerParams(vmem_limit_bytes=VMEM_LIMIT),
    )(aff_group)


def _select_kernel(aff_ref, thr_ref, need_ref, utri_ref, selw_ref, rank_ref, m_ref, eq_seen):
    step = pl.program_id(0)

    @pl.when((step == 0) | (step == GROUP0_BLOCKS // SEL_BLOCKS))
    def _():
        eq_seen[...] = jnp.zeros_like(eq_seen)

    thr = thr_ref[:, 0:1]
    need = need_ref[:, 0:1]
    utri = utri_ref[...]
    for b in range(SEL_BLOCKS):
        cols = slice(b * RB, (b + 1) * RB)
        aff = aff_ref[:, cols]
        eq = jnp.where(aff == thr, 1.0, 0.0)
        eq_before = jnp.dot(eq.astype(BF16), utri, preferred_element_type=F32) + eq_seen[...]
        sel = (aff > thr) | ((aff == thr) & (eq_before < need))
        eq_seen[...] += jnp.sum(eq, axis=1, keepdims=True)
        self = jnp.where(sel, 1.0, 0.0)
        rank_ref[:, cols] = jnp.dot(self.astype(BF16), utri, preferred_element_type=F32)
        selw_ref[:, cols] = jnp.where(sel, aff, 0.0)
        m_ref[b] = jnp.broadcast_to(jnp.sum(self, axis=1, keepdims=True), (N_EXPERTS, LANES))


def _select(aff_t, thr, need):
    utri = jnp.asarray(np.triu(np.ones((RB, RB), np.float32), 1), dtype=BF16)
    grp = lambda i: (jnp.where(i >= GROUP0_BLOCKS // SEL_BLOCKS, 1, 0), 0)
    return pl.pallas_call(
        _select_kernel,
        grid=(N_RB // SEL_BLOCKS,),
        in_specs=[
            pl.BlockSpec((N_EXPERTS, SEL_BLOCKS * RB), lambda i: (0, i)),
            pl.BlockSpec((N_EXPERTS, LANES), grp),
            pl.BlockSpec((N_EXPERTS, LANES), grp),
            pl.BlockSpec((RB, RB), lambda i: (0, 0)),
        ],
        out_specs=[
            pl.BlockSpec((N_EXPERTS, SEL_BLOCKS * RB), lambda i: (0, i)),
            pl.BlockSpec((N_EXPERTS, SEL_BLOCKS * RB), lambda i: (0, i)),
            pl.BlockSpec((SEL_BLOCKS, N_EXPERTS, LANES), lambda i: (i, 0, 0)),
        ],
        out_shape=[
            jax.ShapeDtypeStruct((N_EXPERTS, N_TOK), F32),
            jax.ShapeDtypeStruct((N_EXPERTS, N_TOK), F32),
            jax.ShapeDtypeStruct((N_RB, N_EXPERTS, LANES), F32),
        ],
        scratch_shapes=[pltpu.VMEM((N_EXPERTS, 1), F32)],
        compiler_params=_cparams("arbitrary"),
    )(aff_t, thr, need, utri)


def _fill_chunk_list(list_ref, which, rb, seg_start_ref, seg_rows_ref, filler):
    cnt = jnp.int32(0)
    for e in range(N_EXPERTS):
        base = e * SLOT_CHUNKS + seg_start_ref[rb, e] // CHUNK
        n = seg_rows_ref[rb, e] // CHUNK

        def body(k, carry, base=base, cnt=cnt):
            for u in range(LIST_UNROLL):
                list_ref[which, cnt + LIST_UNROLL * k + u] = base + LIST_UNROLL * k + u
            return carry

        lax.fori_loop(0, (n + LIST_UNROLL - 1) // LIST_UNROLL, body, 0)
        cnt = cnt + n
    for j in range(CHUNKS_PER_TILE):
        list_ref[which, cnt + j] = filler(cnt + j)
    return cnt


WIN_ROWS = 128
SPAN_TILES = 8
SPAN_ROWS = SPAN_TILES * TILE_ROWS


def _mark_span(scr, rb, s0, n_rows, tgt, values, seg_off_ref, seg_rows_ref):
    win_rows = lax.broadcasted_iota(jnp.int32, (WIN_ROWS, RB), 0).astype(F32)
    n_here = jnp.minimum(n_rows - s0, SPAN_ROWS)

    def zero(i, carry):
        scr[pl.ds(pl.multiple_of(i * TILE_ROWS, TILE_ROWS), TILE_ROWS), :] = jnp.zeros((TILE_ROWS, RB), F32)
        return carry

    lax.fori_loop(0, (n_here + TILE_ROWS - 1) // TILE_ROWS, zero, 0)
    for e in range(N_EXPERTS):
        lo = seg_off_ref[rb, e]
        first = jnp.maximum(lo, s0)
        last = jnp.minimum(lo + seg_rows_ref[rb, e], s0 + SPAN_ROWS)

        def mark(w, carry, e=e, first=first):
            start = first + w * WIN_ROWS
            win = pl.ds(pl.multiple_of(start - s0, CHUNK), WIN_ROWS)
            scr[win, :] = jnp.where(tgt[e:e + 1, :] == win_rows + start.astype(F32), values[e], scr[win, :])
            return carry

        lax.fori_loop(0, jnp.maximum(last - first + WIN_ROWS - 1, 0) // WIN_ROWS, mark, 0)


def _expert_column(ref, rb):
    e_iota = lax.broadcasted_iota(jnp.int32, (N_EXPERTS, 1), 0)
    col = jnp.zeros((N_EXPERTS, 1), F32)
    for e in range(N_EXPERTS):
        col = jnp.where(e_iota == e, ref[rb, e].astype(F32), col)
    return col


def _gather_kernel(in_start_ref, in_rows_ref, in_off_ref, tot_ref, in_rank_ref, seg_start_ref, seg_len_ref,
                   x_ref, selw_ref, rank_ref, xe_hbm,
                   p_scr, obuf, hbuf, carry, zbuf, dst_list, pending, sem, hsem, zsem):
    rb = pl.program_id(0)
    n_rows = tot_ref[rb]
    n_tiles = (n_rows + TILE_ROWS - 1) // TILE_ROWS
    head_slot = TILE_SLOTS

    @pl.when(rb == 0)
    def _():
        for slot in range(TILE_SLOTS + 1):
            pending[slot] = 0
        p_scr[...] = jnp.zeros_like(p_scr)
        carry[...] = jnp.zeros_like(carry)

    _fill_chunk_list(dst_list, 0, rb, in_start_ref, in_rows_ref,
                     lambda c: TRASH_CHUNK + c % (TILE_SLOTS * CHUNKS_PER_TILE))

    def tile_copy(slot):
        return pltpu.make_async_copy(obuf.at[slot], xe_hbm.at[pl.ds(0, CHUNKS_PER_TILE)], sem.at[slot])

    def head_copy():
        return pltpu.make_async_copy(hbuf, xe_hbm.at[pl.ds(0, N_EXPERTS)], hsem)

    x = x_ref[...]
    rank = rank_ref[...]
    sel = selw_ref[...] > 0.0

    j8 = lax.broadcasted_iota(jnp.int32, (CHUNK, RB), 0).astype(F32)
    heads, tails, flags = [], [], []
    for e in range(N_EXPERTS):
        start = seg_start_ref[rb, e]
        end = start + seg_len_ref[rb, e]
        nonempty = seg_len_ref[rb, e] > 0
        has_head = nonempty & (start % CHUNK != 0)
        has_tail = nonempty & (end % CHUNK != 0)
        single = has_head & has_tail & (start // CHUNK == end // CHUNK)
        head_shift = (start % CHUNK).astype(F32)
        tail_rank0 = ((end // CHUNK) * CHUNK - start).astype(F32)
        r_e = rank[e:e + 1, :]
        s_e = sel[e:e + 1, :]
        heads.append(jnp.where(s_e & (r_e + head_shift == j8), has_head.astype(F32), 0.0))
        tails.append(jnp.where(s_e & (r_e - tail_rank0 == j8), (has_tail & ~single).astype(F32), 0.0))
        flags.append((has_head, has_tail, single, start // CHUNK))
    p_edge = jnp.concatenate(heads + tails, axis=0).astype(BF16)
    edge = jnp.dot(p_edge, x, preferred_element_type=F32)

    @pl.when(pending[head_slot] == 1)
    def _():
        head_copy().wait()

    head_rows = []
    for e, (has_head, has_tail, single, head_chunk) in enumerate(flags):
        rows = slice(e * CHUNK, (e + 1) * CHUNK)
        old = carry[rows, :]
        head = edge[rows, :] + old
        tail = edge[N_EXPERTS * CHUNK + e * CHUNK:N_EXPERTS * CHUNK + (e + 1) * CHUNK, :]
        head_rows.append(head)
        keep = ~single & ~has_tail & ~has_head
        carry[rows, :] = (single.astype(F32) * head + (has_tail & ~single).astype(F32) * tail
                          + keep.astype(F32) * old)
    hbuf[...] = _pack_chunks(jnp.concatenate(head_rows, axis=0))
    for e, (has_head, has_tail, single, head_chunk) in enumerate(flags):
        dst = jnp.where(has_head & ~single, e * SLOT_CHUNKS + head_chunk, HEAD_TRASH_CHUNK + e)
        pltpu.make_async_copy(hbuf.at[e], xe_hbm.at[dst], hsem).start()
    pending[head_slot] = 1

    off = _expert_column(in_off_ref, rb)
    rank0 = _expert_column(in_rank_ref, rb)
    n_in = _expert_column(in_rows_ref, rb)
    tgt = jnp.where(sel & (rank >= rank0) & (rank < rank0 + n_in), rank - rank0 + off, -1.0)

    def tile_body(t, carry_):
        slot = t % TILE_SLOTS

        @pl.when(pending[slot] == 1)
        def _():
            tile_copy(slot).wait()

        @pl.when(t % SPAN_TILES == 0)
        def _():
            _mark_span(p_scr, rb, t * TILE_ROWS, n_rows, tgt, [1.0] * N_EXPERTS, in_off_ref, in_rows_ref)

        p = p_scr[pl.ds(pl.multiple_of((t % SPAN_TILES) * TILE_ROWS, TILE_ROWS), TILE_ROWS), :]
        obuf[slot] = _pack_chunks(jnp.dot(p.astype(BF16), x, preferred_element_type=F32))

        for i in range(CHUNKS_PER_TILE):
            pltpu.make_async_copy(obuf.at[slot, i], xe_hbm.at[dst_list[0, t * CHUNKS_PER_TILE + i]],
                                  sem.at[slot]).start()
        pending[slot] = 1
        return carry_

    lax.fori_loop(0, n_tiles, tile_body, 0)

    @pl.when(rb == N_RB - 1)
    def _():
        for slot in range(TILE_SLOTS):
            @pl.when(pending[slot] == 1)
            def _(slot=slot):
                tile_copy(slot).wait()
        head_copy().wait()
        zbuf[...] = jnp.zeros_like(zbuf)

        def zcopy(k):
            return pltpu.make_async_copy(zbuf, xe_hbm.at[TRASH_CHUNK + k], zsem)

        def zstart(k, c):
            zcopy(k).start()
            return c

        def zwait(k, c):
            zcopy(k).wait()
            return c

        lax.fori_loop(0, TRASH_CHUNKS, zstart, 0)
        lax.fori_loop(0, TRASH_CHUNKS, zwait, 0)


def _route_gather(tables, xn, selw_t, rank_t):
    gs = pltpu.PrefetchScalarGridSpec(
        num_scalar_prefetch=len(tables),
        grid=(N_RB,),
        in_specs=[
            pl.BlockSpec((RB, D_MODEL), lambda rb, *_: (rb, 0)),
            pl.BlockSpec((N_EXPERTS, RB), lambda rb, *_: (0, rb)),
            pl.BlockSpec((N_EXPERTS, RB), lambda rb, *_: (0, rb)),
        ],
        out_specs=pl.BlockSpec(memory_space=pl.ANY),
        scratch_shapes=[
            pltpu.VMEM((SPAN_ROWS + WIN_ROWS, RB), F32),
            pltpu.VMEM((TILE_SLOTS, CHUNKS_PER_TILE) + CHUNK_SHAPE, BF16),
            pltpu.VMEM((N_EXPERTS,) + CHUNK_SHAPE, BF16),
            pltpu.VMEM((N_EXPERTS * CHUNK, D_MODEL), F32),
            pltpu.VMEM(CHUNK_SHAPE, BF16),
            pltpu.SMEM((1, LIST_LEN), jnp.int32),
            pltpu.SMEM((TILE_SLOTS + 1,), jnp.int32),
            pltpu.SemaphoreType.DMA((TILE_SLOTS,)),
            pltpu.SemaphoreType.DMA,
            pltpu.SemaphoreType.DMA,
        ],
    )
    return pl.pallas_call(
        _gather_kernel,
        grid_spec=gs,
        out_shape=jax.ShapeDtypeStruct((TRASH_CHUNK + TRASH_CHUNKS,) + CHUNK_SHAPE, BF16),
        compiler_params=_cparams("arbitrary"),
    )(*tables, xn, selw_t, rank_t)


W_PIECES_PER_MATRIX = 4
W_PIECES = 3 * W_PIECES_PER_MATRIX
FFN_STEPS = max(SLOT_TILES, W_PIECES + 1)
WG_PIECE_ROWS = D_MODEL // W_PIECES_PER_MATRIX
WD_PIECE_ROWS = EXPERT_D_FF // W_PIECES_PER_MATRIX
FFN_VMEM_LIMIT = 56 * 1024 * 1024


def _ffn_kernel(x_ref, wg_hbm, wu_hbm, wd_hbm, o_ref,
                wg_bf, wu_bf, wd_bf, stage_up, stage_dn, sem):
    e = pl.program_id(0)
    i = pl.program_id(1)
    slot = e % 2

    matrices = ((wg_hbm, stage_up, wg_bf, WG_PIECE_ROWS),
                (wu_hbm, stage_up, wu_bf, WG_PIECE_ROWS),
                (wd_hbm, stage_dn, wd_bf, WD_PIECE_ROWS))

    def piece(expert, k, buf, finish):
        def of_kind(kind):
            w_hbm, stage, w_bf, rows = matrices[kind]
            r0 = pl.multiple_of((k - kind * W_PIECES_PER_MATRIX) * rows, rows)
            cp = pltpu.make_async_copy(w_hbm.at[expert, pl.ds(r0, rows), :], stage.at[buf], sem.at[buf])
            if finish:
                cp.wait()
                w_bf[expert % 2, pl.ds(r0, rows), :] = stage[buf].astype(BF16)
            else:
                cp.start()

        if isinstance(k, int):
            of_kind(k // W_PIECES_PER_MATRIX)
        else:
            for kind in range(len(matrices)):
                pl.when(k // W_PIECES_PER_MATRIX == kind)(functools.partial(of_kind, kind))

    @pl.when((e == 0) & (i == 0))
    def _():
        for k in range(W_PIECES):
            piece(0, k, k % 2, finish=False)
            piece(0, k, k % 2, finish=True)

    @pl.when(e + 1 < N_EXPERTS)
    def _():
        @pl.when((i >= 1) & (i <= W_PIECES))
        def _():
            piece(e + 1, i - 1, (i - 1) % 2, finish=True)

        @pl.when(i < W_PIECES)
        def _():
            piece(e + 1, i, i % 2, finish=False)

    @pl.when(i < SLOT_TILES)
    def _():
        x = _unpack_chunks(x_ref[...])
        acc = None
        for c in range(EXPERT_D_FF // FF_FC):
            gt = jnp.dot(x, wg_bf[slot, :, c * FF_FC:(c + 1) * FF_FC], preferred_element_type=F32)
            up = jnp.dot(x, wu_bf[slot, :, c * FF_FC:(c + 1) * FF_FC], preferred_element_type=F32)
            h = (gt * (1.0 / (1.0 + jnp.exp(-gt))) * up).astype(BF16)
            down = jnp.dot(h, wd_bf[slot, c * FF_FC:(c + 1) * FF_FC, :], preferred_element_type=F32)
            acc = down if acc is None else acc + down
        o_ref[...] = _pack_chunks(acc)


def _ffn(xe, w_gate, w_up, w_down):
    def rows(e, i):
        return (e * SLOT_TILES + jnp.minimum(i, SLOT_TILES - 1), 0, 0)

    tile_chunks = (FF_TM // CHUNK,) + CHUNK_SHAPE
    gs = pltpu.PrefetchScalarGridSpec(
        num_scalar_prefetch=0,
        grid=(N_EXPERTS, FFN_STEPS),
        in_specs=[
            pl.BlockSpec(tile_chunks, rows),
            pl.BlockSpec(memory_space=pl.ANY),
            pl.BlockSpec(memory_space=pl.ANY),
            pl.BlockSpec(memory_space=pl.ANY),
        ],
        out_specs=pl.BlockSpec(tile_chunks, rows),
        scratch_shapes=[
            pltpu.VMEM((2, D_MODEL, EXPERT_D_FF), BF16),
            pltpu.VMEM((2, D_MODEL, EXPERT_D_FF), BF16),
            pltpu.VMEM((2, EXPERT_D_FF, D_MODEL), BF16),
            pltpu.VMEM((2, WG_PIECE_ROWS, EXPERT_D_FF), F32),
            pltpu.VMEM((2, WD_PIECE_ROWS, D_MODEL), F32),
            pltpu.SemaphoreType.DMA((2,)),
        ],
    )
    return pl.pallas_call(
        _ffn_kernel,
        grid_spec=gs,
        out_shape=jax.ShapeDtypeStruct((N_EXPERTS * SLOT_CHUNKS,) + CHUNK_SHAPE, BF16),
        compiler_params=pltpu.CompilerParams(dimension_semantics=("arbitrary", "arbitrary"),
                                             vmem_limit_bytes=FFN_VMEM_LIMIT),
    )(xe, w_gate, w_up, w_down)


def _combine_kernel(seg_start_ref, seg_rows_ref, seg_off_ref, tot_ref, seg_shift_ref,
                    x1_ref, selw_ref, rank_ref, g_ref, ye_hbm, op_ref, os_ref,
                    w_scr, ybuf, acc, src_list, state, sem):
    rb = pl.program_id(0)
    cur = rb % 2
    nxt_rb = jnp.minimum(rb + 1, N_RB - 1)

    def tiles_of(b):
        return (tot_ref[b] + TILE_ROWS - 1) // TILE_ROWS

    n_tiles = tiles_of(rb)
    n_tiles_next = jnp.where(rb + 1 < N_RB, tiles_of(nxt_rb), 0)

    def fetch(which, t, slot):
        for i in range(CHUNKS_PER_TILE):
            pltpu.make_async_copy(ye_hbm.at[src_list[which, t * CHUNKS_PER_TILE + i]], ybuf.at[slot, i],
                                  sem.at[slot]).start()

    def tile_wait(slot):
        pltpu.make_async_copy(ye_hbm.at[pl.ds(0, CHUNKS_PER_TILE)], ybuf.at[slot], sem.at[slot]).wait()

    @pl.when(rb == 0)
    def _():
        w_scr[...] = jnp.zeros_like(w_scr)
        _fill_chunk_list(src_list, 0, 0, seg_start_ref, seg_rows_ref, lambda c: 0)
        state[0] = 0
        state[1] = 0

    @pl.when(rb + 1 < N_RB)
    def _():
        _fill_chunk_list(src_list, 1 - cur, nxt_rb, seg_start_ref, seg_rows_ref, lambda c: 0)

    slot0 = state[0]
    started = state[1]
    for j in range(LOOKAHEAD):
        @pl.when((j >= started) & (j < n_tiles))
        def _(j=j):
            fetch(cur, j, (slot0 + j) % TILE_SLOTS)

    off = _expert_column(seg_off_ref, rb) + _expert_column(seg_shift_ref, rb)
    selw = selw_ref[...]
    gates = [selw[e:e + 1, :] for e in range(N_EXPERTS)]
    tgt = jnp.where(selw > 0.0, rank_ref[...] + off, -1.0)
    acc[...] = x1_ref[...]

    def tile_body(t, next_started):
        slot = (slot0 + t) % TILE_SLOTS
        ahead = t + LOOKAHEAD

        @pl.when(ahead < n_tiles)
        def _():
            fetch(cur, ahead, (slot0 + ahead) % TILE_SLOTS)

        start_next = (ahead >= n_tiles) & (next_started < jnp.minimum(LOOKAHEAD, n_tiles_next))

        @pl.when(start_next)
        def _():
            fetch(1 - cur, next_started, (slot0 + n_tiles + next_started) % TILE_SLOTS)

        @pl.when(t % SPAN_TILES == 0)
        def _():
            _mark_span(w_scr, rb, t * TILE_ROWS, tot_ref[rb], tgt, gates, seg_off_ref, seg_rows_ref)

        tile_wait(slot)
        w = w_scr[pl.ds(pl.multiple_of((t % SPAN_TILES) * TILE_ROWS, TILE_ROWS), TILE_ROWS), :]
        acc[...] += lax.dot_general(w.astype(BF16), _unpack_chunks(ybuf[slot]),
                                    (((0,), (0,)), ((), ())), preferred_element_type=F32)
        return next_started + start_next.astype(jnp.int32)

    state[1] = lax.fori_loop(0, n_tiles, tile_body, jnp.int32(0))
    state[0] = (slot0 + n_tiles) % TILE_SLOTS
    x2 = acc[...]
    ms = jnp.mean(x2 * x2, axis=-1, keepdims=True)
    y = x2 * lax.rsqrt(ms + EPS) * g_ref[...]

    @pl.when(rb < GROUP0_BLOCKS)
    def _():
        op_ref[...] = y

    @pl.when(rb >= GROUP0_BLOCKS)
    def _():
        os_ref[...] = y


def _combine(tables, x1, selw_t, rank_t, g, ye):
    gs = pltpu.PrefetchScalarGridSpec(
        num_scalar_prefetch=len(tables),
        grid=(N_RB,),
        in_specs=[
            pl.BlockSpec((RB, D_MODEL), lambda rb, *_: (rb, 0)),
            pl.BlockSpec((N_EXPERTS, RB), lambda rb, *_: (0, rb)),
            pl.BlockSpec((N_EXPERTS, RB), lambda rb, *_: (0, rb)),
            pl.BlockSpec((1, D_MODEL), lambda rb, *_: (0, 0)),
            pl.BlockSpec(memory_space=pl.ANY),
        ],
        out_specs=[
            pl.BlockSpec((RB, D_MODEL), lambda rb, *_: (jnp.minimum(rb, GROUP0_BLOCKS - 1), 0)),
            pl.BlockSpec((RB, D_MODEL), lambda rb, *_: (jnp.maximum(rb - GROUP0_BLOCKS, 0), 0)),
        ],
        scratch_shapes=[
            pltpu.VMEM((SPAN_ROWS + WIN_ROWS, RB), F32),
            pltpu.VMEM((TILE_SLOTS, CHUNKS_PER_TILE) + CHUNK_SHAPE, BF16),
            pltpu.VMEM((RB, D_MODEL), F32),
            pltpu.SMEM((2, LIST_LEN), jnp.int32),
            pltpu.SMEM((2,), jnp.int32),
            pltpu.SemaphoreType.DMA((TILE_SLOTS,)),
        ],
    )
    return pl.pallas_call(
        _combine_kernel,
        grid_spec=gs,
        out_shape=[jax.ShapeDtypeStruct((GROUP_TOKENS[0], D_MODEL), F32),
                   jax.ShapeDtypeStruct((GROUP_TOKENS[1], D_MODEL), F32)],
        compiler_params=_cparams("arbitrary"),
    )(*tables, x1, selw_t, rank_t, g, ye)


def kernel(x_prompt, x_sample, norm_mix_g, w_in, w_fourier, sink, w_out, norm_ffn_g, w_router,
           w_gate, w_up, w_down, norm_final_g):
    assert x_prompt.shape == (1, SEQ, D_MODEL) and x_sample.shape == (N_SEQ - 1, SEQ, D_MODEL)
    assert w_in.shape == (1, D_MODEL, IN_WIDTH) and w_out.shape == (1, MIX_WIDTH, D_MODEL)
    assert w_gate.shape == (1, N_EXPERTS, D_MODEL, EXPERT_D_FF) and w_down.shape == (1, N_EXPERTS, EXPERT_D_FF, D_MODEL)
    xp = x_prompt.reshape(-1, D_MODEL)
    xs = x_sample.reshape(-1, D_MODEL)
    cos_t, sin_t = _rope_tables()

    mix_cols = np.concatenate([
        np.arange(FOURIER_WIDTH),
        FOURIER_WIDTH + (np.asarray(Q_ORDER)[:, None] * HEAD_DIM + np.arange(HEAD_DIM)[None, :]).reshape(-1)])
    in_cols = np.concatenate([mix_cols, np.arange(MIX_WIDTH, IN_WIDTH)])
    w_in_r = w_in[0][:, in_cols].astype(BF16)
    w_out_r = w_out[0][mix_cols, :].astype(BF16)

    u, q, k, v = _inproj(xp, xs, norm_mix_g[0][None, :], w_in_r, cos_t, sin_t)
    yf = _fourier(u, w_fourier[0].astype(BF16))
    ya = _attention(sink[0], q, k, v)
    x1, xn, aff_t = _outproj(xp, xs, yf, ya, w_out_r, norm_ffn_g[0][None, :],
                             w_router[0].T.astype(BF16))

    thr0, need0 = _thresholds(aff_t[:, :GROUP_TOKENS[0]])
    thr1, need1 = _thresholds(aff_t[:, GROUP_TOKENS[0]:])
    selw_t, rank_t, m = _select(aff_t, jnp.concatenate([thr0, thr1]), jnp.concatenate([need0, need1]))

    seg_len = m[:, :, 0].astype(jnp.int32)
    seg_start = jnp.cumsum(seg_len, axis=0) - seg_len
    seg_end = seg_start + seg_len

    def offsets(rows):
        return jnp.cumsum(rows, axis=1) - rows, jnp.sum(rows, axis=1)

    in_start = -(-seg_start // CHUNK) * CHUNK
    in_rows = jnp.maximum(seg_end // CHUNK * CHUNK - in_start, 0)
    in_off, in_tot = offsets(in_rows)
    gather_tables = (in_start, in_rows, in_off, in_tot, in_start - seg_start, seg_start, seg_len)
    out_start = seg_start // CHUNK * CHUNK
    out_rows = jnp.where(seg_len > 0, -(-seg_end // CHUNK) * CHUNK - out_start, 0)
    out_off, out_tot = offsets(out_rows)
    combine_tables = (out_start, out_rows, out_off, out_tot, seg_start - out_start)

    xe = _route_gather(gather_tables, xn, selw_t, rank_t)
    ye = _ffn(xe, w_gate[0], w_up[0], w_down[0])
    yp, ys = _combine(combine_tables, x1, selw_t, rank_t, norm_final_g[None, :], ye)
    return (yp.reshape(x_prompt.shape), ys.reshape(x_sample.shape))
```

```python
import functools

import numpy as np
import jax
import jax.numpy as jnp
from jax import lax
from jax.experimental import pallas as pl
from jax.experimental.pallas import tpu as pltpu

F32 = jnp.float32
BF16 = jnp.bfloat16

D_MODEL = 1024
SEQ = 16384
N_SEQ = 3
N_TOK = N_SEQ * SEQ
HEAD_DIM = 64
N_HEADS = 12
N_KV_HEADS = 4
GQA_GROUP = N_HEADS // N_KV_HEADS
ATTN_WIDTH = N_HEADS * HEAD_DIM
KV_WIDTH = N_KV_HEADS * HEAD_DIM
FOURIER_WIDTH = 256
FOURIER_GROUP_DIM = 64
MIX_WIDTH = FOURIER_WIDTH + ATTN_WIDTH
IN_WIDTH = MIX_WIDTH + 2 * KV_WIDTH
WINDOW = 128
ROPE_THETA = 10000.0
N_EXPERTS = 16
EC_CAPACITY_FACTOR = 2
EXPERT_D_FF = 2048
EPS = 1e-6
NEG = -1e30

LANES = 128
FFT_R = 128
VMEM_LIMIT = 48 * 1024 * 1024

TM = 1024
TQ = 1024
QB = 128
FF_TM = 512
FF_FC = 512


def _cparams(*sem):
    return pltpu.CompilerParams(dimension_semantics=sem, vmem_limit_bytes=VMEM_LIMIT)


def _x_specs():
    n_p = SEQ // TM
    return [pl.BlockSpec((TM, D_MODEL), lambda i, *_: (jnp.minimum(i, n_p - 1), 0)),
            pl.BlockSpec((TM, D_MODEL), lambda i, *_: (jnp.maximum(i - n_p, 0), 0))]


def _x_tile(xp_ref, xs_ref):
    return jnp.where(pl.program_id(0) < SEQ // TM, xp_ref[...], xs_ref[...])


def _inproj_kernel(xp_ref, xs_ref, g_ref, w_ref, cos_ref, sin_ref, u_ref, q_ref, k_ref, v_ref):
    x = _x_tile(xp_ref, xs_ref)
    ms = jnp.mean(x * x, axis=-1, keepdims=True)
    hn = (x * lax.rsqrt(ms + EPS) * g_ref[...]).astype(BF16)
    proj = jnp.dot(hn, w_ref[...], preferred_element_type=F32)
    u_ref[...] = proj[:, :FOURIER_WIDTH].astype(BF16)
    cos = cos_ref[...]
    sin = sin_ref[...]
    lane = lax.broadcasted_iota(jnp.int32, cos.shape, 1)
    first_half = (lane % HEAD_DIM) < (HEAD_DIM // 2)

    def rope(t):
        partner = jnp.where(first_half,
                            pltpu.roll(t, LANES - HEAD_DIM // 2, 1),
                            pltpu.roll(t, HEAD_DIM // 2, 1))
        return t * cos + partner * sin

    scale = HEAD_DIM ** -0.5 * LOG2E
    for c in range(ATTN_WIDTH // LANES):
        lo = FOURIER_WIDTH + c * LANES
        q_ref[:, c * LANES:(c + 1) * LANES] = (rope(proj[:, lo:lo + LANES]) * scale).astype(BF16)
    low_half = lane < HEAD_DIM
    for c in range(KV_WIDTH // LANES):
        lo = FOURIER_WIDTH + ATTN_WIDTH + c * LANES
        kc = rope(proj[:, lo:lo + LANES])
        vc = proj[:, lo + KV_WIDTH:lo + KV_WIDTH + LANES]
        for half, keep in enumerate((low_half, ~low_half)):
            g = 2 * c + half
            k_ref[:, g * LANES:(g + 1) * LANES] = jnp.where(keep, kc, 0.0).astype(BF16)
            v_ref[:, g * LANES:(g + 1) * LANES] = jnp.where(keep, vc, 0.0).astype(BF16)


def _inproj(xp, xs, g, w_in, cos_t, sin_t):
    n_pos_blocks = SEQ // TM
    return pl.pallas_call(
        _inproj_kernel,
        grid=(N_TOK // TM,),
        in_specs=_x_specs() + [
            pl.BlockSpec((1, D_MODEL), lambda i: (0, 0)),
            pl.BlockSpec((D_MODEL, IN_WIDTH), lambda i: (0, 0)),
            pl.BlockSpec((TM, LANES), lambda i: (i % n_pos_blocks, 0)),
            pl.BlockSpec((TM, LANES), lambda i: (i % n_pos_blocks, 0)),
        ],
        out_specs=[
            pl.BlockSpec((TM, FOURIER_WIDTH), lambda i: (i, 0)),
            pl.BlockSpec((TM, ATTN_WIDTH), lambda i: (i, 0)),
            pl.BlockSpec((TM, KV4), lambda i: (i, 0)),
            pl.BlockSpec((TM, KV4), lambda i: (i, 0)),
        ],
        out_shape=[
            jax.ShapeDtypeStruct((N_TOK, FOURIER_WIDTH), BF16),
            jax.ShapeDtypeStruct((N_TOK, ATTN_WIDTH), BF16),
            jax.ShapeDtypeStruct((N_TOK, KV4), BF16),
            jax.ShapeDtypeStruct((N_TOK, KV4), BF16),
        ],
        compiler_params=_cparams("parallel"),
    )(xp, xs, g, w_in, cos_t, sin_t)


def _rope_tables():
    half = HEAD_DIM // 2
    inv_freq = 1.0 / (ROPE_THETA ** (jnp.arange(half, dtype=F32) / half))
    ang = jnp.arange(SEQ, dtype=F32)[:, None] * inv_freq[None, :]
    cos = jnp.cos(ang)
    sin = jnp.sin(ang)
    reps = LANES // HEAD_DIM
    cos_t = jnp.tile(jnp.concatenate([cos, cos], axis=-1), (1, reps))
    sin_t = jnp.tile(jnp.concatenate([-sin, sin], axis=-1), (1, reps))
    return cos_t, sin_t


def _dft_tables():
    r = FFT_R
    c = np.arange(FOURIER_GROUP_DIM)
    ang_c = 2.0 * np.pi * np.outer(c, c) / FOURIER_GROUP_DIM
    n_groups = FOURIER_WIDTH // FOURIER_GROUP_DIM
    eye = np.eye(n_groups)
    ch_scale = FOURIER_GROUP_DIM ** -0.5
    c_blk = np.kron(eye, np.cos(ang_c)) * ch_scale
    s_blk = np.kron(eye, np.sin(ang_c)) * ch_scale
    cs = np.concatenate([c_blk, -s_blk], axis=1)
    k = np.arange(r)
    ang_r = 2.0 * np.pi * np.outer(k, k) / r
    st_scale = r ** -0.5
    wr = np.cos(ang_r) * st_scale
    wi = -np.sin(ang_r) * st_scale
    wbig = np.block([[wr, -wi], [wi, wr]])
    ang_t = 2.0 * np.pi * np.outer(k, k) / (r * r)
    tr = np.cos(ang_t)
    ti = -np.sin(ang_t)
    f = lambda a: jnp.asarray(a, dtype=F32)
    return f(cs).astype(BF16), f(wbig).astype(BF16), f(wr), f(wi), f(tr), f(ti)


FS_SB = 16
FS_KB = 16


def _fft_stage1_kernel(u_ref, cs_ref, wbig_ref, y_ref):
    cs = cs_ref[...]
    wbig = wbig_ref[...]
    fw = FOURIER_WIDTH
    for j in range(FS_SB):
        u = u_ref[:, j * fw:(j + 1) * fw]
        g = jnp.dot(u, cs, preferred_element_type=F32)
        gst = jnp.concatenate([g[:, :fw], g[:, fw:]], axis=0).astype(BF16)
        y = jnp.dot(wbig, gst, preferred_element_type=F32)
        y_ref[:, j * 2 * fw:j * 2 * fw + fw] = y[:FFT_R].astype(BF16)
        y_ref[:, j * 2 * fw + fw:(j + 1) * 2 * fw] = y[FFT_R:].astype(BF16)


def _fft_stage2_kernel(y_ref, wr_ref, wi_ref, tr_ref, ti_ref, wf_ref, o_ref):
    kb = pl.program_id(1)
    wr = wr_ref[...]
    wi = wi_ref[...]
    wf = wf_ref[...]
    fw = FOURIER_WIDTH
    for j in range(FS_KB):
        k1 = kb * FS_KB + j
        tr = tr_ref[pl.ds(k1, 1), :]
        ti = ti_ref[pl.ds(k1, 1), :]
        mr = wr * tr - wi * ti
        mi = wr * ti + wi * tr
        m2 = jnp.concatenate([mr, -mi], axis=1).astype(BF16)
        yb = y_ref[j * FFT_R:(j + 1) * FFT_R, :]
        yst = jnp.concatenate([yb[:, :fw], yb[:, fw:]], axis=0)
        z = jnp.dot(m2, yst, preferred_element_type=F32)
        yf = jnp.dot(z.astype(BF16), wf, preferred_element_type=F32)
        o_ref[:, j * fw:(j + 1) * fw] = yf.astype(BF16)


def _fourier(u, w_fourier):
    cs, wbig, wr, wi, tr, ti = _dft_tables()
    r, fw = FFT_R, FOURIER_WIDTH
    u2 = u.reshape(N_SEQ * r, r * fw)
    y = pl.pallas_call(
        _fft_stage1_kernel,
        grid=(N_SEQ, r // FS_SB),
        in_specs=[
            pl.BlockSpec((r, FS_SB * fw), lambda s, j: (s, j)),
            pl.BlockSpec((fw, 2 * fw), lambda s, j: (0, 0)),
            pl.BlockSpec((2 * r, 2 * r), lambda s, j: (0, 0)),
        ],
        out_specs=pl.BlockSpec((r, FS_SB * 2 * fw), lambda s, j: (s, j)),
        out_shape=jax.ShapeDtypeStruct((N_SEQ * r, r * 2 * fw), BF16),
        compiler_params=_cparams("parallel", "parallel"),
    )(u2, cs, wbig)
    y2 = y.reshape(N_SEQ * r * r, 2 * fw)
    full = lambda s, j: (0, 0)
    yf = pl.pallas_call(
        _fft_stage2_kernel,
        grid=(N_SEQ, r // FS_KB),
        in_specs=[
            pl.BlockSpec((FS_KB * r, 2 * fw), lambda s, j: (s * (r // FS_KB) + j, 0)),
            pl.BlockSpec((r, r), full),
            pl.BlockSpec((r, r), full),
            pl.BlockSpec((r, r), full),
            pl.BlockSpec((r, r), full),
            pl.BlockSpec((fw, fw), full),
        ],
        out_specs=pl.BlockSpec((r, FS_KB * fw), lambda s, j: (s, j)),
        out_shape=jax.ShapeDtypeStruct((N_SEQ * r, r * fw), BF16),
        compiler_params=_cparams("parallel", "parallel"),
    )(y2, wr, wi, tr, ti, w_fourier)
    return yf.reshape(N_TOK, fw)


LO_HEADS = (0, 1, 2, 6, 7, 8)
HI_HEADS = (3, 4, 5, 9, 10, 11)
Q_ORDER = tuple(h for pair in zip(LO_HEADS, HI_HEADS) for h in pair)
assert all(h // GQA_GROUP % 2 == 0 for h in LO_HEADS) and all(h // GQA_GROUP % 2 == 1 for h in HI_HEADS)
KV4 = 2 * KV_WIDTH
LOG2E = 1.4426950408889634


def _attn_kernel(sink_ref, q_ref, kp_ref, kc_ref, kn_ref, vp_ref, vc_ref, vn_ref, o_ref,
                 kext, vext):
    i = pl.program_id(1)
    n_i = pl.num_programs(1)
    kext[0:QB, :] = kp_ref[...]
    kext[QB:QB + TQ, :] = kc_ref[...]
    kext[QB + TQ:, :] = kn_ref[...]
    vext[0:QB, :] = vp_ref[...]
    vext[QB:QB + TQ, :] = vc_ref[...]
    vext[QB + TQ:, :] = vn_ref[...]

    rows = GQA_GROUP * QB
    a = lax.broadcasted_iota(jnp.int32, (rows, QB), 0) % QB
    j = lax.broadcasted_iota(jnp.int32, (rows, QB), 1)
    band_prev = jnp.where(j >= a, 0.0, NEG)
    band_next = jnp.where(j <= a, 0.0, NEG)
    row_id = lax.broadcasted_iota(jnp.int32, (rows, 1), 0)
    low_half = lax.broadcasted_iota(jnp.int32, (rows, LANES), 1) < HEAD_DIM

    def softmax_half(s, sinks, bias_prev, bias_next):
        s_p = s[:, :QB] + bias_prev
        s_c = s[:, QB:2 * QB]
        s_n = s[:, 2 * QB:] + bias_next
        sink = jnp.where(row_id < QB, sinks[0], jnp.where(row_id < 2 * QB, sinks[1], sinks[2])) * LOG2E
        m = jnp.max(jnp.maximum(jnp.maximum(s_p, s_c), s_n), axis=-1, keepdims=True)
        m = jnp.maximum(m, sink)
        p_p = jnp.exp2(s_p - m)
        p_c = jnp.exp2(s_c - m)
        p_n = jnp.exp2(s_n - m)
        denom = jnp.sum(p_p + p_c + p_n, axis=-1, keepdims=True) + jnp.exp2(sink - m)
        return [p_p.astype(BF16), p_c.astype(BF16), p_n.astype(BF16)], 1.0 / denom

    for sb in range(TQ // QB):
        blk = i * (TQ // QB) + sb
        bias_prev = band_prev + jnp.where(blk == 0, NEG, 0.0)
        bias_next = band_next + jnp.where(blk == n_i * (TQ // QB) - 1, NEG, 0.0)
        for pr in range(N_KV_HEADS // 2):
            groups = [GQA_GROUP * pr + m for m in range(GQA_GROUP)]
            q3 = jnp.concatenate([q_ref[sb * QB:(sb + 1) * QB, g * LANES:(g + 1) * LANES] for g in groups],
                                 axis=0)
            kp = kext[sb * QB:sb * QB + 3 * QB, pr * 2 * LANES:(pr + 1) * 2 * LANES]
            vp = vext[sb * QB:sb * QB + 3 * QB, pr * 2 * LANES:(pr + 1) * 2 * LANES]
            kcat = jnp.concatenate([kp[:, :LANES], kp[:, LANES:]], axis=0)
            vcat = jnp.concatenate([vp[:, :LANES], vp[:, LANES:]], axis=0)
            s = lax.dot_general(q3, kcat, (((1,), (1,)), ((), ())), preferred_element_type=F32)
            p_lo, r_lo = softmax_half(s[:, :3 * QB], [sink_ref[LO_HEADS[g]] for g in groups],
                                      bias_prev, bias_next)
            p_hi, r_hi = softmax_half(s[:, 3 * QB:], [sink_ref[HI_HEADS[g]] for g in groups],
                                      bias_prev, bias_next)
            p = jnp.concatenate(p_lo + p_hi, axis=1)
            o = jnp.dot(p, vcat, preferred_element_type=F32) * jnp.where(low_half, r_lo, r_hi)
            for m, g in enumerate(groups):
                o_ref[sb * QB:(sb + 1) * QB, g * LANES:(g + 1) * LANES] = o[m * QB:(m + 1) * QB].astype(BF16)


def _attention(sink, q, k, v):
    nqb = SEQ // QB
    per = TQ // QB
    cur = lambda s, i: (s * (SEQ // TQ) + i, 0)
    prev = lambda s, i: (s * nqb + jnp.maximum(i * per - 1, 0), 0)
    nxt = lambda s, i: (s * nqb + jnp.minimum(i * per + per, nqb - 1), 0)
    return pl.pallas_call(
        _attn_kernel,
        grid=(N_SEQ, SEQ // TQ),
        in_specs=[
            pl.BlockSpec(memory_space=pltpu.SMEM),
            pl.BlockSpec((TQ, ATTN_WIDTH), cur),
            pl.BlockSpec((QB, KV4), prev),
            pl.BlockSpec((TQ, KV4), cur),
            pl.BlockSpec((QB, KV4), nxt),
            pl.BlockSpec((QB, KV4), prev),
            pl.BlockSpec((TQ, KV4), cur),
            pl.BlockSpec((QB, KV4), nxt),
        ],
        out_specs=pl.BlockSpec((TQ, ATTN_WIDTH), cur),
        out_shape=jax.ShapeDtypeStruct((N_TOK, ATTN_WIDTH), BF16),
        scratch_shapes=[pltpu.VMEM((TQ + 2 * QB, KV4), BF16),
                        pltpu.VMEM((TQ + 2 * QB, KV4), BF16)],
        compiler_params=_cparams("parallel", "parallel"),
    )(sink, q, k, k, k, v, v, v)


def _outproj_kernel(xp_ref, xs_ref, yf_ref, ya_ref, wo_ref, g_ref, wr_ref, x1_ref, xn_ref, aff_ref):
    mix = jnp.dot(yf_ref[...], wo_ref[:FOURIER_WIDTH, :], preferred_element_type=F32)
    mix = mix + jnp.dot(ya_ref[...], wo_ref[FOURIER_WIDTH:, :], preferred_element_type=F32)
    x1 = _x_tile(xp_ref, xs_ref) + mix
    x1_ref[...] = x1
    ms = jnp.mean(x1 * x1, axis=-1, keepdims=True)
    xn = (x1 * lax.rsqrt(ms + EPS) * g_ref[...]).astype(BF16)
    xn_ref[...] = xn
    logits = lax.dot_general(wr_ref[...], xn, (((1,), (1,)), ((), ())), preferred_element_type=F32)
    mx = jnp.max(logits, axis=0, keepdims=True)
    ex = jnp.exp(logits - mx)
    aff_ref[...] = ex / jnp.sum(ex, axis=0, keepdims=True)


def _outproj(xp, xs, yf, ya, w_out, g, w_router_t):
    return pl.pallas_call(
        _outproj_kernel,
        grid=(N_TOK // TM,),
        in_specs=_x_specs() + [
            pl.BlockSpec((TM, FOURIER_WIDTH), lambda i: (i, 0)),
            pl.BlockSpec((TM, ATTN_WIDTH), lambda i: (i, 0)),
            pl.BlockSpec((D_MODEL, D_MODEL), lambda i: (0, 0)),
            pl.BlockSpec((1, D_MODEL), lambda i: (0, 0)),
            pl.BlockSpec((N_EXPERTS, D_MODEL), lambda i: (0, 0)),
        ],
        out_specs=[
            pl.BlockSpec((TM, D_MODEL), lambda i: (i, 0)),
            pl.BlockSpec((TM, D_MODEL), lambda i: (i, 0)),
            pl.BlockSpec((N_EXPERTS, TM), lambda i: (0, i)),
        ],
        out_shape=[
            jax.ShapeDtypeStruct((N_TOK, D_MODEL), F32),
            jax.ShapeDtypeStruct((N_TOK, D_MODEL), BF16),
            jax.ShapeDtypeStruct((N_EXPERTS, N_TOK), F32),
        ],
        compiler_params=_cparams("parallel"),
    )(xp, xs, yf, ya, w_out, g, w_router_t)


RB = 512
N_RB = N_TOK // RB
GROUP_TOKENS = (SEQ, 2 * SEQ)
GROUP0_BLOCKS = GROUP_TOKENS[0] // RB
SEL_BLOCKS = 4
assert GROUP0_BLOCKS % SEL_BLOCKS == 0 and N_RB % SEL_BLOCKS == 0
CHUNK = 8
TILE_ROWS = 256
CHUNKS_PER_TILE = TILE_ROWS // CHUNK
SLOT_ROWS = EC_CAPACITY_FACTOR * N_TOK // N_EXPERTS
assert all((EC_CAPACITY_FACTOR * n // N_EXPERTS) % CHUNK == 0 for n in GROUP_TOKENS)
SLOT_TILES = SLOT_ROWS // FF_TM
assert SLOT_TILES * FF_TM == SLOT_ROWS
SLOT_CHUNKS = SLOT_ROWS // CHUNK
TILE_SLOTS = 4
LOOKAHEAD = TILE_SLOTS - 1
TRASH_CHUNK = N_EXPERTS * SLOT_CHUNKS
HEAD_TRASH_CHUNK = TRASH_CHUNK + TILE_SLOTS * CHUNKS_PER_TILE
TRASH_CHUNKS = TILE_SLOTS * CHUNKS_PER_TILE + FF_TM // CHUNK
assert N_EXPERTS <= FF_TM // CHUNK and TRASH_CHUNKS % (FF_TM // CHUNK) == 0
MAX_CHUNKS = (N_EXPERTS * (RB + 2 * CHUNK)) // CHUNK
LIST_UNROLL = 4
LIST_LEN = (MAX_CHUNKS // CHUNKS_PER_TILE + 2) * CHUNKS_PER_TILE
CHUNK_SHAPE = (2 * CHUNK, D_MODEL // 2)


def _pack_chunks(rows_f32):
    n = rows_f32.shape[0] // CHUNK
    r3 = rows_f32.reshape(n, CHUNK, D_MODEL)
    half = D_MODEL // 2
    return jnp.concatenate([r3[:, :, :half], r3[:, :, half:]], axis=1).astype(BF16)


def _unpack_chunks(chunks_bf16):
    n = chunks_bf16.shape[0]
    c3 = chunks_bf16.astype(F32)
    left = c3[:, :CHUNK, :].reshape(n * CHUNK, D_MODEL // 2)
    right = c3[:, CHUNK:, :].reshape(n * CHUNK, D_MODEL // 2)
    return jnp.concatenate([left, right], axis=1).astype(BF16)


def _thresh_kernel(aff_ref, thr_ref, need_ref):
    first = 0
    for g, n in enumerate(GROUP_TOKENS):
        cap = float(EC_CAPACITY_FACTOR * n // N_EXPERTS)
        aff = aff_ref[:, first:first + n]
        first += n

        def body(i, cur, aff=aff, cap=cap):
            cand = cur | jnp.left_shift(jnp.int32(1), 30 - i)
            cnt = jnp.sum(jnp.where(aff >= pltpu.bitcast(cand, F32), 1.0, 0.0), axis=1, keepdims=True)
            return jnp.where(cnt >= cap, cand, cur)

        thr = pltpu.bitcast(lax.fori_loop(0, 31, body, jnp.zeros((N_EXPERTS, 1), jnp.int32)), F32)
        n_gt = jnp.sum(jnp.where(aff > thr, 1.0, 0.0), axis=1, keepdims=True)
        rows = slice(g * N_EXPERTS, (g + 1) * N_EXPERTS)
        thr_ref[rows, :] = jnp.broadcast_to(thr, (N_EXPERTS, LANES))
        need_ref[rows, :] = jnp.broadcast_to(cap - n_gt, (N_EXPERTS, LANES))


def _thresholds(aff_t):
    n_groups = len(GROUP_TOKENS)
    return pl.pallas_call(
        _thresh_kernel,
        out_shape=[jax.ShapeDtypeStruct((n_groups * N_EXPERTS, LANES), F32),
                   jax.ShapeDtypeStruct((n_groups * N_EXPERTS, LANES), F32)],
        compiler_params=pltpu.CompilerParams(vmem_limit_bytes=VMEM_LIMIT),
    )(aff_t)


def _select_kernel(aff_ref, thr_ref, need_ref, utri_ref, selw_ref, rank_ref, m_ref, eq_seen):
    step = pl.program_id(0)

    @pl.when((step == 0) | (step == GROUP0_BLOCKS // SEL_BLOCKS))
    def _():
        eq_seen[...] = jnp.zeros_like(eq_seen)

    thr = thr_ref[:, 0:1]
    need = need_ref[:, 0:1]
    utri = utri_ref[...]
    for b in range(SEL_BLOCKS):
        cols = slice(b * RB, (b + 1) * RB)
        aff = aff_ref[:, cols]
        eq = jnp.where(aff == thr, 1.0, 0.0)
        eq_before = jnp.dot(eq.astype(BF16), utri, preferred_element_type=F32) + eq_seen[...]
        sel = (aff > thr) | ((aff == thr) & (eq_before < need))
        eq_seen[...] += jnp.sum(eq, axis=1, keepdims=True)
        self = jnp.where(sel, 1.0, 0.0)
        rank_ref[:, cols] = jnp.dot(self.astype(BF16), utri, preferred_element_type=F32)
        selw_ref[:, cols] = jnp.where(sel, aff, 0.0)
        m_ref[b] = jnp.broadcast_to(jnp.sum(self, axis=1, keepdims=True), (N_EXPERTS, LANES))


def _select(aff_t, thr, need):
    utri = jnp.asarray(np.triu(np.ones((RB, RB), np.float32), 1), dtype=BF16)
    grp = lambda i: (jnp.where(i >= GROUP0_BLOCKS // SEL_BLOCKS, 1, 0), 0)
    return pl.pallas_call(
        _select_kernel,
        grid=(N_RB // SEL_BLOCKS,),
        in_specs=[
            pl.BlockSpec((N_EXPERTS, SEL_BLOCKS * RB), lambda i: (0, i)),
            pl.BlockSpec((N_EXPERTS, LANES), grp),
            pl.BlockSpec((N_EXPERTS, LANES), grp),
            pl.BlockSpec((RB, RB), lambda i: (0, 0)),
        ],
        out_specs=[
            pl.BlockSpec((N_EXPERTS, SEL_BLOCKS * RB), lambda i: (0, i)),
            pl.BlockSpec((N_EXPERTS, SEL_BLOCKS * RB), lambda i: (0, i)),
            pl.BlockSpec((SEL_BLOCKS, N_EXPERTS, LANES), lambda i: (i, 0, 0)),
        ],
        out_shape=[
            jax.ShapeDtypeStruct((N_EXPERTS, N_TOK), F32),
            jax.ShapeDtypeStruct((N_EXPERTS, N_TOK), F32),
            jax.ShapeDtypeStruct((N_RB, N_EXPERTS, LANES), F32),
        ],
        scratch_shapes=[pltpu.VMEM((N_EXPERTS, 1), F32)],
        compiler_params=_cparams("arbitrary"),
    )(aff_t, thr, need, utri)


def _fill_chunk_list(list_ref, which, rb, seg_start_ref, seg_rows_ref, filler):
    cnt = jnp.int32(0)
    for e in range(N_EXPERTS):
        base = e * SLOT_CHUNKS + seg_start_ref[rb, e] // CHUNK
        n = seg_rows_ref[rb, e] // CHUNK

        def body(k, carry, base=base, cnt=cnt):
            for u in range(LIST_UNROLL):
                list_ref[which, cnt + LIST_UNROLL * k + u] = base + LIST_UNROLL * k + u
            return carry

        lax.fori_loop(0, (n + LIST_UNROLL - 1) // LIST_UNROLL, body, 0)
        cnt = cnt + n
    for j in range(CHUNKS_PER_TILE):
        list_ref[which, cnt + j] = filler(cnt + j)
    return cnt


WIN_ROWS = 128
SPAN_TILES = 8
SPAN_ROWS = SPAN_TILES * TILE_ROWS


def _mark_span(scr, rb, s0, n_rows, tgt, values, seg_off_ref, seg_rows_ref):
    win_rows = lax.broadcasted_iota(jnp.int32, (WIN_ROWS, RB), 0).astype(F32)
    n_here = jnp.minimum(n_rows - s0, SPAN_ROWS)

    def zero(i, carry):
        scr[pl.ds(pl.multiple_of(i * TILE_ROWS, TILE_ROWS), TILE_ROWS), :] = jnp.zeros((TILE_ROWS, RB), F32)
        return carry

    lax.fori_loop(0, (n_here + TILE_ROWS - 1) // TILE_ROWS, zero, 0)
    for e in range(N_EXPERTS):
        lo = seg_off_ref[rb, e]
        first = jnp.maximum(lo, s0)
        last = jnp.minimum(lo + seg_rows_ref[rb, e], s0 + SPAN_ROWS)

        def mark(w, carry, e=e, first=first):
            start = first + w * WIN_ROWS
            win = pl.ds(pl.multiple_of(start - s0, CHUNK), WIN_ROWS)
            scr[win, :] = jnp.where(tgt[e:e + 1, :] == win_rows + start.astype(F32), values[e], scr[win, :])
            return carry

        lax.fori_loop(0, jnp.maximum(last - first + WIN_ROWS - 1, 0) // WIN_ROWS, mark, 0)


def _expert_column(ref, rb):
    e_iota = lax.broadcasted_iota(jnp.int32, (N_EXPERTS, 1), 0)
    col = jnp.zeros((N_EXPERTS, 1), F32)
    for e in range(N_EXPERTS):
        col = jnp.where(e_iota == e, ref[rb, e].astype(F32), col)
    return col


def _gather_kernel(in_start_ref, in_rows_ref, in_off_ref, tot_ref, in_rank_ref, seg_start_ref, seg_len_ref,
                   x_ref, selw_ref, rank_ref, xe_hbm,
                   p_scr, obuf, hbuf, carry, zbuf, dst_list, pending, sem, hsem, zsem):
    rb = pl.program_id(0)
    n_rows = tot_ref[rb]
    n_tiles = (n_rows + TILE_ROWS - 1) // TILE_ROWS
    head_slot = TILE_SLOTS

    @pl.when(rb == 0)
    def _():
        for slot in range(TILE_SLOTS + 1):
            pending[slot] = 0
        p_scr[...] = jnp.zeros_like(p_scr)
        carry[...] = jnp.zeros_like(carry)

    _fill_chunk_list(dst_list, 0, rb, in_start_ref, in_rows_ref,
                     lambda c: TRASH_CHUNK + c % (TILE_SLOTS * CHUNKS_PER_TILE))

    def tile_copy(slot):
        return pltpu.make_async_copy(obuf.at[slot], xe_hbm.at[pl.ds(0, CHUNKS_PER_TILE)], sem.at[slot])

    def head_copy():
        return pltpu.make_async_copy(hbuf, xe_hbm.at[pl.ds(0, N_EXPERTS)], hsem)

    x = x_ref[...]
    rank = rank_ref[...]
    sel = selw_ref[...] > 0.0

    j8 = lax.broadcasted_iota(jnp.int32, (CHUNK, RB), 0).astype(F32)
    heads, tails, flags = [], [], []
    for e in range(N_EXPERTS):
        start = seg_start_ref[rb, e]
        end = start + seg_len_ref[rb, e]
        nonempty = seg_len_ref[rb, e] > 0
        has_head = nonempty & (start % CHUNK != 0)
        has_tail = nonempty & (end % CHUNK != 0)
        single = has_head & has_tail & (start // CHUNK == end // CHUNK)
        head_shift = (start % CHUNK).astype(F32)
        tail_rank0 = ((end // CHUNK) * CHUNK - start).astype(F32)
        r_e = rank[e:e + 1, :]
        s_e = sel[e:e + 1, :]
        heads.append(jnp.where(s_e & (r_e + head_shift == j8), has_head.astype(F32), 0.0))
        tails.append(jnp.where(s_e & (r_e - tail_rank0 == j8), (has_tail & ~single).astype(F32), 0.0))
        flags.append((has_head, has_tail, single, start // CHUNK))
    p_edge = jnp.concatenate(heads + tails, axis=0).astype(BF16)
    edge = jnp.dot(p_edge, x, preferred_element_type=F32)

    @pl.when(pending[head_slot] == 1)
    def _():
        head_copy().wait()

    head_rows = []
    for e, (has_head, has_tail, single, head_chunk) in enumerate(flags):
        rows = slice(e * CHUNK, (e + 1) * CHUNK)
        old = carry[rows, :]
        head = edge[rows, :] + old
        tail = edge[N_EXPERTS * CHUNK + e * CHUNK:N_EXPERTS * CHUNK + (e + 1) * CHUNK, :]
        head_rows.append(head)
        keep = ~single & ~has_tail & ~has_head
        carry[rows, :] = (single.astype(F32) * head + (has_tail & ~single).astype(F32) * tail
                          + keep.astype(F32) * old)
    hbuf[...] = _pack_chunks(jnp.concatenate(head_rows, axis=0))
    for e, (has_head, has_tail, single, head_chunk) in enumerate(flags):
        dst = jnp.where(has_head & ~single, e * SLOT_CHUNKS + head_chunk, HEAD_TRASH_CHUNK + e)
        pltpu.make_async_copy(hbuf.at[e], xe_hbm.at[dst], hsem).start()
    pending[head_slot] = 1

    off = _expert_column(in_off_ref, rb)
    rank0 = _expert_column(in_rank_ref, rb)
    n_in = _expert_column(in_rows_ref, rb)
    tgt = jnp.where(sel & (rank >= rank0) & (rank < rank0 + n_in), rank - rank0 + off, -1.0)

    def tile_body(t, carry_):
        slot = t % TILE_SLOTS

        @pl.when(pending[slot] == 1)
        def _():
            tile_copy(slot).wait()

        @pl.when(t % SPAN_TILES == 0)
        def _():
            _mark_span(p_scr, rb, t * TILE_ROWS, n_rows, tgt, [1.0] * N_EXPERTS, in_off_ref, in_rows_ref)

        p = p_scr[pl.ds(pl.multiple_of((t % SPAN_TILES) * TILE_ROWS, TILE_ROWS), TILE_ROWS), :]
        obuf[slot] = _pack_chunks(jnp.dot(p.astype(BF16), x, preferred_element_type=F32))

        for i in range(CHUNKS_PER_TILE):
            pltpu.make_async_copy(obuf.at[slot, i], xe_hbm.at[dst_list[0, t * CHUNKS_PER_TILE + i]],
                                  sem.at[slot]).start()
        pending[slot] = 1
        return carry_

    lax.fori_loop(0, n_tiles, tile_body, 0)

    @pl.when(rb == N_RB - 1)
    def _():
        for slot in range(TILE_SLOTS):
            @pl.when(pending[slot] == 1)
            def _(slot=slot):
                tile_copy(slot).wait()
        head_copy().wait()
        zbuf[...] = jnp.zeros_like(zbuf)

        def zcopy(k):
            return pltpu.make_async_copy(zbuf, xe_hbm.at[TRASH_CHUNK + k], zsem)

        def zstart(k, c):
            zcopy(k).start()
            return c

        def zwait(k, c):
            zcopy(k).wait()
            return c

        lax.fori_loop(0, TRASH_CHUNKS, zstart, 0)
        lax.fori_loop(0, TRASH_CHUNKS, zwait, 0)


def _route_gather(tables, xn, selw_t, rank_t):
    gs = pltpu.PrefetchScalarGridSpec(
        num_scalar_prefetch=len(tables),
        grid=(N_RB,),
        in_specs=[
            pl.BlockSpec((RB, D_MODEL), lambda rb, *_: (rb, 0)),
            pl.BlockSpec((N_EXPERTS, RB), lambda rb, *_: (0, rb)),
            pl.BlockSpec((N_EXPERTS, RB), lambda rb, *_: (0, rb)),
        ],
        out_specs=pl.BlockSpec(memory_space=pl.ANY),
        scratch_shapes=[
            pltpu.VMEM((SPAN_ROWS + WIN_ROWS, RB), F32),
            pltpu.VMEM((TILE_SLOTS, CHUNKS_PER_TILE) + CHUNK_SHAPE, BF16),
            pltpu.VMEM((N_EXPERTS,) + CHUNK_SHAPE, BF16),
            pltpu.VMEM((N_EXPERTS * CHUNK, D_MODEL), F32),
            pltpu.VMEM(CHUNK_SHAPE, BF16),
            pltpu.SMEM((1, LIST_LEN), jnp.int32),
            pltpu.SMEM((TILE_SLOTS + 1,), jnp.int32),
            pltpu.SemaphoreType.DMA((TILE_SLOTS,)),
            pltpu.SemaphoreType.DMA,
            pltpu.SemaphoreType.DMA,
        ],
    )
    return pl.pallas_call(
        _gather_kernel,
        grid_spec=gs,
        out_shape=jax.ShapeDtypeStruct((TRASH_CHUNK + TRASH_CHUNKS,) + CHUNK_SHAPE, BF16),
        compiler_params=_cparams("arbitrary"),
    )(*tables, xn, selw_t, rank_t)


W_PIECES_PER_MATRIX = 4
W_PIECES = 3 * W_PIECES_PER_MATRIX
FFN_STEPS = max(SLOT_TILES, W_PIECES + 1)
WG_PIECE_ROWS = D_MODEL // W_PIECES_PER_MATRIX
WD_PIECE_ROWS = EXPERT_D_FF // W_PIECES_PER_MATRIX
FFN_VMEM_LIMIT = 56 * 1024 * 1024


def _ffn_kernel(x_ref, wg_hbm, wu_hbm, wd_hbm, o_ref,
                wg_bf, wu_bf, wd_bf, stage_up, stage_dn, sem):
    e = pl.program_id(0)
    i = pl.program_id(1)
    slot = e % 2

    matrices = ((wg_hbm, stage_up, wg_bf, WG_PIECE_ROWS),
                (wu_hbm, stage_up, wu_bf, WG_PIECE_ROWS),
                (wd_hbm, stage_dn, wd_bf, WD_PIECE_ROWS))

    def piece(expert, k, buf, finish):
        def of_kind(kind):
            w_hbm, stage, w_bf, rows = matrices[kind]
            r0 = pl.multiple_of((k - kind * W_PIECES_PER_MATRIX) * rows, rows)
            cp = pltpu.make_async_copy(w_hbm.at[expert, pl.ds(r0, rows), :], stage.at[buf], sem.at[buf])
            if finish:
                cp.wait()
                w_bf[expert % 2, pl.ds(r0, rows), :] = stage[buf].astype(BF16)
            else:
                cp.start()

        if isinstance(k, int):
            of_kind(k // W_PIECES_PER_MATRIX)
        else:
            for kind in range(len(matrices)):
                pl.when(k // W_PIECES_PER_MATRIX == kind)(functools.partial(of_kind, kind))

    @pl.when((e == 0) & (i == 0))
    def _():
        for k in range(W_PIECES):
            piece(0, k, k % 2, finish=False)
            piece(0, k, k % 2, finish=True)

    @pl.when(e + 1 < N_EXPERTS)
    def _():
        @pl.when((i >= 1) & (i <= W_PIECES))
        def _():
            piece(e + 1, i - 1, (i - 1) % 2, finish=True)

        @pl.when(i < W_PIECES)
        def _():
            piece(e + 1, i, i % 2, finish=False)

    @pl.when(i < SLOT_TILES)
    def _():
        x = _unpack_chunks(x_ref[...])
        acc = None
        for c in range(EXPERT_D_FF // FF_FC):
            gt = jnp.dot(x, wg_bf[slot, :, c * FF_FC:(c + 1) * FF_FC], preferred_element_type=F32)
            up = jnp.dot(x, wu_bf[slot, :, c * FF_FC:(c + 1) * FF_FC], preferred_element_type=F32)
            h = (gt * (1.0 / (1.0 + jnp.exp(-gt))) * up).astype(BF16)
            down = jnp.dot(h, wd_bf[slot, c * FF_FC:(c + 1) * FF_FC, :], preferred_element_type=F32)
            acc = down if acc is None else acc + down
        o_ref[...] = _pack_chunks(acc)


def _ffn(xe, w_gate, w_up, w_down):
    def rows(e, i):
        return (e * SLOT_TILES + jnp.minimum(i, SLOT_TILES - 1), 0, 0)

    tile_chunks = (FF_TM // CHUNK,) + CHUNK_SHAPE
    gs = pltpu.PrefetchScalarGridSpec(
        num_scalar_prefetch=0,
        grid=(N_EXPERTS, FFN_STEPS),
        in_specs=[
            pl.BlockSpec(tile_chunks, rows),
            pl.BlockSpec(memory_space=pl.ANY),
            pl.BlockSpec(memory_space=pl.ANY),
            pl.BlockSpec(memory_space=pl.ANY),
        ],
        out_specs=pl.BlockSpec(tile_chunks, rows),
        scratch_shapes=[
            pltpu.VMEM((2, D_MODEL, EXPERT_D_FF), BF16),
            pltpu.VMEM((2, D_MODEL, EXPERT_D_FF), BF16),
            pltpu.VMEM((2, EXPERT_D_FF, D_MODEL), BF16),
            pltpu.VMEM((2, WG_PIECE_ROWS, EXPERT_D_FF), F32),
            pltpu.VMEM((2, WD_PIECE_ROWS, D_MODEL), F32),
            pltpu.SemaphoreType.DMA((2,)),
        ],
    )
    return pl.pallas_call(
        _ffn_kernel,
        grid_spec=gs,
        out_shape=jax.ShapeDtypeStruct((N_EXPERTS * SLOT_CHUNKS,) + CHUNK_SHAPE, BF16),
        compiler_params=pltpu.CompilerParams(dimension_semantics=("arbitrary", "arbitrary"),
                                             vmem_limit_bytes=FFN_VMEM_LIMIT),
    )(xe, w_gate, w_up, w_down)


def _combine_kernel(seg_start_ref, seg_rows_ref, seg_off_ref, tot_ref, seg_shift_ref,
                    x1_ref, selw_ref, rank_ref, g_ref, ye_hbm, op_ref, os_ref,
                    w_scr, ybuf, acc, src_list, state, sem):
    rb = pl.program_id(0)
    cur = rb % 2
    nxt_rb = jnp.minimum(rb + 1, N_RB - 1)

    def tiles_of(b):
        return (tot_ref[b] + TILE_ROWS - 1) // TILE_ROWS

    n_tiles = tiles_of(rb)
    n_tiles_next = jnp.where(rb + 1 < N_RB, tiles_of(nxt_rb), 0)

    def fetch(which, t, slot):
        for i in range(CHUNKS_PER_TILE):
            pltpu.make_async_copy(ye_hbm.at[src_list[which, t * CHUNKS_PER_TILE + i]], ybuf.at[slot, i],
                                  sem.at[slot]).start()

    def tile_wait(slot):
        pltpu.make_async_copy(ye_hbm.at[pl.ds(0, CHUNKS_PER_TILE)], ybuf.at[slot], sem.at[slot]).wait()

    @pl.when(rb == 0)
    def _():
        w_scr[...] = jnp.zeros_like(w_scr)
        _fill_chunk_list(src_list, 0, 0, seg_start_ref, seg_rows_ref, lambda c: 0)
        state[0] = 0
        state[1] = 0

    @pl.when(rb + 1 < N_RB)
    def _():
        _fill_chunk_list(src_list, 1 - cur, nxt_rb, seg_start_ref, seg_rows_ref, lambda c: 0)

    slot0 = state[0]
    started = state[1]
    for j in range(LOOKAHEAD):
        @pl.when((j >= started) & (j < n_tiles))
        def _(j=j):
            fetch(cur, j, (slot0 + j) % TILE_SLOTS)

    off = _expert_column(seg_off_ref, rb) + _expert_column(seg_shift_ref, rb)
    selw = selw_ref[...]
    gates = [selw[e:e + 1, :] for e in range(N_EXPERTS)]
    tgt = jnp.where(selw > 0.0, rank_ref[...] + off, -1.0)
    acc[...] = x1_ref[...]

    def tile_body(t, next_started):
        slot = (slot0 + t) % TILE_SLOTS
        ahead = t + LOOKAHEAD

        @pl.when(ahead < n_tiles)
        def _():
            fetch(cur, ahead, (slot0 + ahead) % TILE_SLOTS)

        start_next = (ahead >= n_tiles) & (next_started < jnp.minimum(LOOKAHEAD, n_tiles_next))

        @pl.when(start_next)
        def _():
            fetch(1 - cur, next_started, (slot0 + n_tiles + next_started) % TILE_SLOTS)

        @pl.when(t % SPAN_TILES == 0)
        def _():
            _mark_span(w_scr, rb, t * TILE_ROWS, tot_ref[rb], tgt, gates, seg_off_ref, seg_rows_ref)

        tile_wait(slot)
        w = w_scr[pl.ds(pl.multiple_of((t % SPAN_TILES) * TILE_ROWS, TILE_ROWS), TILE_ROWS), :]
        acc[...] += lax.dot_general(w.astype(BF16), _unpack_chunks(ybuf[slot]),
                                    (((0,), (0,)), ((), ())), preferred_element_type=F32)
        return next_started + start_next.astype(jnp.int32)

    state[1] = lax.fori_loop(0, n_tiles, tile_body, jnp.int32(0))
    state[0] = (slot0 + n_tiles) % TILE_SLOTS
    x2 = acc[...]
    ms = jnp.mean(x2 * x2, axis=-1, keepdims=True)
    y = x2 * lax.rsqrt(ms + EPS) * g_ref[...]

    @pl.when(rb < GROUP0_BLOCKS)
    def _():
        op_ref[...] = y

    @pl.when(rb >= GROUP0_BLOCKS)
    def _():
        os_ref[...] = y


def _combine(tables, x1, selw_t, rank_t, g, ye):
    gs = pltpu.PrefetchScalarGridSpec(
        num_scalar_prefetch=len(tables),
        grid=(N_RB,),
        in_specs=[
            pl.BlockSpec((RB, D_MODEL), lambda rb, *_: (rb, 0)),
            pl.BlockSpec((N_EXPERTS, RB), lambda rb, *_: (0, rb)),
            pl.BlockSpec((N_EXPERTS, RB), lambda rb, *_: (0, rb)),
            pl.BlockSpec((1, D_MODEL), lambda rb, *_: (0, 0)),
            pl.BlockSpec(memory_space=pl.ANY),
        ],
        out_specs=[
            pl.BlockSpec((RB, D_MODEL), lambda rb, *_: (jnp.minimum(rb, GROUP0_BLOCKS - 1), 0)),
            pl.BlockSpec((RB, D_MODEL), lambda rb, *_: (jnp.maximum(rb - GROUP0_BLOCKS, 0), 0)),
        ],
        scratch_shapes=[
            pltpu.VMEM((SPAN_ROWS + WIN_ROWS, RB), F32),
            pltpu.VMEM((TILE_SLOTS, CHUNKS_PER_TILE) + CHUNK_SHAPE, BF16),
            pltpu.VMEM((RB, D_MODEL), F32),
            pltpu.SMEM((2, LIST_LEN), jnp.int32),
            pltpu.SMEM((2,), jnp.int32),
            pltpu.SemaphoreType.DMA((TILE_SLOTS,)),
        ],
    )
    return pl.pallas_call(
        _combine_kernel,
        grid_spec=gs,
        out_shape=[jax.ShapeDtypeStruct((GROUP_TOKENS[0], D_MODEL), F32),
                   jax.ShapeDtypeStruct((GROUP_TOKENS[1], D_MODEL), F32)],
        compiler_params=_cparams("arbitrary"),
    )(*tables, x1, selw_t, rank_t, g, ye)


def kernel(x_prompt, x_sample, norm_mix_g, w_in, w_fourier, sink, w_out, norm_ffn_g, w_router,
           w_gate, w_up, w_down, norm_final_g):
    assert x_prompt.shape == (1, SEQ, D_MODEL) and x_sample.shape == (N_SEQ - 1, SEQ, D_MODEL)
    assert w_in.shape == (1, D_MODEL, IN_WIDTH) and w_out.shape == (1, MIX_WIDTH, D_MODEL)
    assert w_gate.shape == (1, N_EXPERTS, D_MODEL, EXPERT_D_FF) and w_down.shape == (1, N_EXPERTS, EXPERT_D_FF, D_MODEL)
    xp = x_prompt.reshape(-1, D_MODEL)
    xs = x_sample.reshape(-1, D_MODEL)
    cos_t, sin_t = _rope_tables()

    mix_cols = np.concatenate([
        np.arange(FOURIER_WIDTH),
        FOURIER_WIDTH + (np.asarray(Q_ORDER)[:, None] * HEAD_DIM + np.arange(HEAD_DIM)[None, :]).reshape(-1)])
    in_cols = np.concatenate([mix_cols, np.arange(MIX_WIDTH, IN_WIDTH)])
    w_in_r = w_in[0][:, in_cols].astype(BF16)
    w_out_r = w_out[0][mix_cols, :].astype(BF16)

    u, q, k, v = _inproj(xp, xs, norm_mix_g[0][None, :], w_in_r, cos_t, sin_t)
    yf = _fourier(u, w_fourier[0].astype(BF16))
    ya = _attention(sink[0], q, k, v)
    x1, xn, aff_t = _outproj(xp, xs, yf, ya, w_out_r, norm_ffn_g[0][None, :],
                             w_router[0].T.astype(BF16))

    thr, need = _thresholds(aff_t)
    selw_t, rank_t, m = _select(aff_t, thr, need)

    seg_len = m[:, :, 0].astype(jnp.int32)
    seg_start = jnp.cumsum(seg_len, axis=0) - seg_len
    seg_end = seg_start + seg_len

    def offsets(rows):
        return jnp.cumsum(rows, axis=1) - rows, jnp.sum(rows, axis=1)

    in_start = -(-seg_start // CHUNK) * CHUNK
    in_rows = jnp.maximum(seg_end // CHUNK * CHUNK - in_start, 0)
    in_off, in_tot = offsets(in_rows)
    gather_tables = (in_start, in_rows, in_off, in_tot, in_start - seg_start, seg_start, seg_len)
    out_start = seg_start // CHUNK * CHUNK
    out_rows = jnp.where(seg_len > 0, -(-seg_end // CHUNK) * CHUNK - out_start, 0)
    out_off, out_tot = offsets(out_rows)
    combine_tables = (out_start, out_rows, out_off, out_tot, seg_start - out_start)

    xe = _route_gather(gather_tables, xn, selw_t, rank_t)
    ye = _ffn(xe, w_gate[0], w_up[0], w_down[0])
    yp, ys = _combine(combine_tables, x1, selw_t, rank_t, norm_final_g[None, :], ye)
    return (yp.reshape(x_prompt.shape), ys.reshape(x_sample.shape))
```

```python
import functools

import numpy as np
import jax
import jax.numpy as jnp
from jax import lax
from jax.experimental import pallas as pl
from jax.experimental.pallas import tpu as pltpu

F32 = jnp.float32
BF16 = jnp.bfloat16

D_MODEL = 1024
SEQ = 16384
N_SEQ = 3
N_TOK = N_SEQ * SEQ
HEAD_DIM = 64
N_HEADS = 12
N_KV_HEADS = 4
GQA_GROUP = N_HEADS // N_KV_HEADS
ATTN_WIDTH = N_HEADS * HEAD_DIM
KV_WIDTH = N_KV_HEADS * HEAD_DIM
FOURIER_WIDTH = 256
FOURIER_GROUP_DIM = 64
MIX_WIDTH = FOURIER_WIDTH + ATTN_WIDTH
IN_WIDTH = MIX_WIDTH + 2 * KV_WIDTH
WINDOW = 128
ROPE_THETA = 10000.0
N_EXPERTS = 16
EC_CAPACITY_FACTOR = 2
EXPERT_D_FF = 2048
EPS = 1e-6
NEG = -1e30

LANES = 128
FFT_R = 128
VMEM_LIMIT = 48 * 1024 * 1024

TM = 1024
TQ = 1024
QB = 128
FF_TM = 512
FF_FC = 512


def _cparams(*sem):
    return pltpu.CompilerParams(dimension_semantics=sem, vmem_limit_bytes=VMEM_LIMIT)


def _x_specs():
    n_p = SEQ // TM
    return [pl.BlockSpec((TM, D_MODEL), lambda i, *_: (jnp.minimum(i, n_p - 1), 0)),
            pl.BlockSpec((TM, D_MODEL), lambda i, *_: (jnp.maximum(i - n_p, 0), 0))]


def _x_tile(xp_ref, xs_ref):
    return jnp.where(pl.program_id(0) < SEQ // TM, xp_ref[...], xs_ref[...])


def _inproj_kernel(xp_ref, xs_ref, g_ref, w_ref, cos_ref, sin_ref, u_ref, q_ref, k_ref, v_ref):
    x = _x_tile(xp_ref, xs_ref)
    ms = jnp.mean(x * x, axis=-1, keepdims=True)
    hn = (x * lax.rsqrt(ms + EPS) * g_ref[...]).astype(BF16)
    proj = jnp.dot(hn, w_ref[...], preferred_element_type=F32)
    u_ref[...] = proj[:, :FOURIER_WIDTH].astype(BF16)
    cos = cos_ref[...]
    sin = sin_ref[...]
    lane = lax.broadcasted_iota(jnp.int32, cos.shape, 1)
    first_half = (lane % HEAD_DIM) < (HEAD_DIM // 2)

    def rope(t):
        partner = jnp.where(first_half,
                            pltpu.roll(t, LANES - HEAD_DIM // 2, 1),
                            pltpu.roll(t, HEAD_DIM // 2, 1))
        return t * cos + partner * sin

    scale = HEAD_DIM ** -0.5 * LOG2E
    for c in range(ATTN_WIDTH // LANES):
        lo = FOURIER_WIDTH + c * LANES
        q_ref[:, c * LANES:(c + 1) * LANES] = (rope(proj[:, lo:lo + LANES]) * scale).astype(BF16)
    low_half = lane < HEAD_DIM
    for c in range(KV_WIDTH // LANES):
        lo = FOURIER_WIDTH + ATTN_WIDTH + c * LANES
        kc = rope(proj[:, lo:lo + LANES])
        vc = proj[:, lo + KV_WIDTH:lo + KV_WIDTH + LANES]
        for half, keep in enumerate((low_half, ~low_half)):
            g = 2 * c + half
            k_ref[:, g * LANES:(g + 1) * LANES] = jnp.where(keep, kc, 0.0).astype(BF16)
            v_ref[:, g * LANES:(g + 1) * LANES] = jnp.where(keep, vc, 0.0).astype(BF16)


def _inproj(xp, xs, g, w_in, cos_t, sin_t):
    n_pos_blocks = SEQ // TM
    return pl.pallas_call(
        _inproj_kernel,
        grid=(N_TOK // TM,),
        in_specs=_x_specs() + [
            pl.BlockSpec((1, D_MODEL), lambda i: (0, 0)),
            pl.BlockSpec((D_MODEL, IN_WIDTH), lambda i: (0, 0)),
            pl.BlockSpec((TM, LANES), lambda i: (i % n_pos_blocks, 0)),
            pl.BlockSpec((TM, LANES), lambda i: (i % n_pos_blocks, 0)),
        ],
        out_specs=[
            pl.BlockSpec((TM, FOURIER_WIDTH), lambda i: (i, 0)),
            pl.BlockSpec((TM, ATTN_WIDTH), lambda i: (i, 0)),
            pl.BlockSpec((TM, KV4), lambda i: (i, 0)),
            pl.BlockSpec((TM, KV4), lambda i: (i, 0)),
        ],
        out_shape=[
            jax.ShapeDtypeStruct((N_TOK, FOURIER_WIDTH), BF16),
            jax.ShapeDtypeStruct((N_TOK, ATTN_WIDTH), BF16),
            jax.ShapeDtypeStruct((N_TOK, KV4), BF16),
            jax.ShapeDtypeStruct((N_TOK, KV4), BF16),
        ],
        compiler_params=_cparams("parallel"),
    )(xp, xs, g, w_in, cos_t, sin_t)


def _rope_tables():
    half = HEAD_DIM // 2
    inv_freq = 1.0 / (ROPE_THETA ** (jnp.arange(half, dtype=F32) / half))
    ang = jnp.arange(SEQ, dtype=F32)[:, None] * inv_freq[None, :]
    cos = jnp.cos(ang)
    sin = jnp.sin(ang)
    reps = LANES // HEAD_DIM
    cos_t = jnp.tile(jnp.concatenate([cos, cos], axis=-1), (1, reps))
    sin_t = jnp.tile(jnp.concatenate([-sin, sin], axis=-1), (1, reps))
    return cos_t, sin_t


def _dft_tables():
    r = FFT_R
    c = np.arange(FOURIER_GROUP_DIM)
    ang_c = 2.0 * np.pi * np.outer(c, c) / FOURIER_GROUP_DIM
    n_groups = FOURIER_WIDTH // FOURIER_GROUP_DIM
    eye = np.eye(n_groups)
    ch_scale = FOURIER_GROUP_DIM ** -0.5
    c_blk = np.kron(eye, np.cos(ang_c)) * ch_scale
    s_blk = np.kron(eye, np.sin(ang_c)) * ch_scale
    cs = np.concatenate([c_blk, -s_blk], axis=1)
    k = np.arange(r)
    ang_r = 2.0 * np.pi * np.outer(k, k) / r
    st_scale = r ** -0.5
    wr = np.cos(ang_r) * st_scale
    wi = -np.sin(ang_r) * st_scale
    wbig = np.block([[wr, -wi], [wi, wr]])
    ang_t = 2.0 * np.pi * np.outer(k, k) / (r * r)
    tr = np.cos(ang_t)
    ti = -np.sin(ang_t)
    f = lambda a: jnp.asarray(a, dtype=F32)
    return f(cs).astype(BF16), f(wbig).astype(BF16), f(wr), f(wi), f(tr), f(ti)


FS_SB = 16
FS_KB = 16


def _fft_stage1_kernel(u_ref, cs_ref, wbig_ref, y_ref):
    cs = cs_ref[...]
    wbig = wbig_ref[...]
    fw = FOURIER_WIDTH
    for j in range(FS_SB):
        u = u_ref[:, j * fw:(j + 1) * fw]
        g = jnp.dot(u, cs, preferred_element_type=F32)
        gst = jnp.concatenate([g[:, :fw], g[:, fw:]], axis=0).astype(BF16)
        y = jnp.dot(wbig, gst, preferred_element_type=F32)
        y_ref[:, j * 2 * fw:j * 2 * fw + fw] = y[:FFT_R].astype(BF16)
        y_ref[:, j * 2 * fw + fw:(j + 1) * 2 * fw] = y[FFT_R:].astype(BF16)


def _fft_stage2_kernel(y_ref, wr_ref, wi_ref, tr_ref, ti_ref, wf_ref, o_ref):
    kb = pl.program_id(1)
    wr = wr_ref[...]
    wi = wi_ref[...]
    wf = wf_ref[...]
    fw = FOURIER_WIDTH
    for j in range(FS_KB):
        k1 = kb * FS_KB + j
        tr = tr_ref[pl.ds(k1, 1), :]
        ti = ti_ref[pl.ds(k1, 1), :]
        mr = wr * tr - wi * ti
        mi = wr * ti + wi * tr
        m2 = jnp.concatenate([mr, -mi], axis=1).astype(BF16)
        yb = y_ref[j * FFT_R:(j + 1) * FFT_R, :]
        yst = jnp.concatenate([yb[:, :fw], yb[:, fw:]], axis=0)
        z = jnp.dot(m2, yst, preferred_element_type=F32)
        yf = jnp.dot(z.astype(BF16), wf, preferred_element_type=F32)
        o_ref[:, j * fw:(j + 1) * fw] = yf.astype(BF16)


def _fourier(u, w_fourier):
    cs, wbig, wr, wi, tr, ti = _dft_tables()
    r, fw = FFT_R, FOURIER_WIDTH
    u2 = u.reshape(N_SEQ * r, r * fw)
    y = pl.pallas_call(
        _fft_stage1_kernel,
        grid=(N_SEQ, r // FS_SB),
        in_specs=[
            pl.BlockSpec((r, FS_SB * fw), lambda s, j: (s, j)),
            pl.BlockSpec((fw, 2 * fw), lambda s, j: (0, 0)),
            pl.BlockSpec((2 * r, 2 * r), lambda s, j: (0, 0)),
        ],
        out_specs=pl.BlockSpec((r, FS_SB * 2 * fw), lambda s, j: (s, j)),
        out_shape=jax.ShapeDtypeStruct((N_SEQ * r, r * 2 * fw), BF16),
        compiler_params=_cparams("parallel", "parallel"),
    )(u2, cs, wbig)
    y2 = y.reshape(N_SEQ * r * r, 2 * fw)
    full = lambda s, j: (0, 0)
    yf = pl.pallas_call(
        _fft_stage2_kernel,
        grid=(N_SEQ, r // FS_KB),
        in_specs=[
            pl.BlockSpec((FS_KB * r, 2 * fw), lambda s, j: (s * (r // FS_KB) + j, 0)),
            pl.BlockSpec((r, r), full),
            pl.BlockSpec((r, r), full),
            pl.BlockSpec((r, r), full),
            pl.BlockSpec((r, r), full),
            pl.BlockSpec((fw, fw), full),
        ],
        out_specs=pl.BlockSpec((r, FS_KB * fw), lambda s, j: (s, j)),
        out_shape=jax.ShapeDtypeStruct((N_SEQ * r, r * fw), BF16),
        compiler_params=_cparams("parallel", "parallel"),
    )(y2, wr, wi, tr, ti, w_fourier)
    return yf.reshape(N_TOK, fw)


LO_HEADS = (0, 1, 2, 6, 7, 8)
HI_HEADS = (3, 4, 5, 9, 10, 11)
Q_ORDER = tuple(h for pair in zip(LO_HEADS, HI_HEADS) for h in pair)
assert all(h // GQA_GROUP % 2 == 0 for h in LO_HEADS) and all(h // GQA_GROUP % 2 == 1 for h in HI_HEADS)
KV4 = 2 * KV_WIDTH
LOG2E = 1.4426950408889634


def _attn_kernel(sink_ref, q_ref, kp_ref, kc_ref, kn_ref, vp_ref, vc_ref, vn_ref, o_ref,
                 kext, vext):
    i = pl.program_id(1)
    n_i = pl.num_programs(1)
    kext[0:QB, :] = kp_ref[...]
    kext[QB:QB + TQ, :] = kc_ref[...]
    kext[QB + TQ:, :] = kn_ref[...]
    vext[0:QB, :] = vp_ref[...]
    vext[QB:QB + TQ, :] = vc_ref[...]
    vext[QB + TQ:, :] = vn_ref[...]

    rows = GQA_GROUP * QB
    a = lax.broadcasted_iota(jnp.int32, (rows, QB), 0) % QB
    j = lax.broadcasted_iota(jnp.int32, (rows, QB), 1)
    band_prev = jnp.where(j >= a, 0.0, NEG)
    band_next = jnp.where(j <= a, 0.0, NEG)
    row_id = lax.broadcasted_iota(jnp.int32, (rows, 1), 0)
    low_half = lax.broadcasted_iota(jnp.int32, (rows, LANES), 1) < HEAD_DIM

    def softmax_half(s, sinks, bias_prev, bias_next):
        s_p = s[:, :QB] + bias_prev
        s_c = s[:, QB:2 * QB]
        s_n = s[:, 2 * QB:] + bias_next
        sink = jnp.where(row_id < QB, sinks[0], jnp.where(row_id < 2 * QB, sinks[1], sinks[2])) * LOG2E
        m = jnp.max(jnp.maximum(jnp.maximum(s_p, s_c), s_n), axis=-1, keepdims=True)
        m = jnp.maximum(m, sink)
        p_p = jnp.exp2(s_p - m)
        p_c = jnp.exp2(s_c - m)
        p_n = jnp.exp2(s_n - m)
        denom = jnp.sum(p_p + p_c + p_n, axis=-1, keepdims=True) + jnp.exp2(sink - m)
        return [p_p.astype(BF16), p_c.astype(BF16), p_n.astype(BF16)], 1.0 / denom

    for sb in range(TQ // QB):
        blk = i * (TQ // QB) + sb
        bias_prev = band_prev + jnp.where(blk == 0, NEG, 0.0)
        bias_next = band_next + jnp.where(blk == n_i * (TQ // QB) - 1, NEG, 0.0)
        for pr in range(N_KV_HEADS // 2):
            groups = [GQA_GROUP * pr + m for m in range(GQA_GROUP)]
            q3 = jnp.concatenate([q_ref[sb * QB:(sb + 1) * QB, g * LANES:(g + 1) * LANES] for g in groups],
                                 axis=0)
            kp = kext[sb * QB:sb * QB + 3 * QB, pr * 2 * LANES:(pr + 1) * 2 * LANES]
            vp = vext[sb * QB:sb * QB + 3 * QB, pr * 2 * LANES:(pr + 1) * 2 * LANES]
            kcat = jnp.concatenate([kp[:, :LANES], kp[:, LANES:]], axis=0)
            vcat = jnp.concatenate([vp[:, :LANES], vp[:, LANES:]], axis=0)
            s = lax.dot_general(q3, kcat, (((1,), (1,)), ((), ())), preferred_element_type=F32)
            p_lo, r_lo = softmax_half(s[:, :3 * QB], [sink_ref[LO_HEADS[g]] for g in groups],
                                      bias_prev, bias_next)
            p_hi, r_hi = softmax_half(s[:, 3 * QB:], [sink_ref[HI_HEADS[g]] for g in groups],
                                      bias_prev, bias_next)
            p = jnp.concatenate(p_lo + p_hi, axis=1)
            o = jnp.dot(p, vcat, preferred_element_type=F32) * jnp.where(low_half, r_lo, r_hi)
            for m, g in enumerate(groups):
                o_ref[sb * QB:(sb + 1) * QB, g * LANES:(g + 1) * LANES] = o[m * QB:(m + 1) * QB].astype(BF16)


def _attention(sink, q, k, v):
    nqb = SEQ // QB
    per = TQ // QB
    cur = lambda s, i: (s * (SEQ // TQ) + i, 0)
    prev = lambda s, i: (s * nqb + jnp.maximum(i * per - 1, 0), 0)
    nxt = lambda s, i: (s * nqb + jnp.minimum(i * per + per, nqb - 1), 0)
    return pl.pallas_call(
        _attn_kernel,
        grid=(N_SEQ, SEQ // TQ),
        in_specs=[
            pl.BlockSpec(memory_space=pltpu.SMEM),
            pl.BlockSpec((TQ, ATTN_WIDTH), cur),
            pl.BlockSpec((QB, KV4), prev),
            pl.BlockSpec((TQ, KV4), cur),
            pl.BlockSpec((QB, KV4), nxt),
            pl.BlockSpec((QB, KV4), prev),
            pl.BlockSpec((TQ, KV4), cur),
            pl.BlockSpec((QB, KV4), nxt),
        ],
        out_specs=pl.BlockSpec((TQ, ATTN_WIDTH), cur),
        out_shape=jax.ShapeDtypeStruct((N_TOK, ATTN_WIDTH), BF16),
        scratch_shapes=[pltpu.VMEM((TQ + 2 * QB, KV4), BF16),
                        pltpu.VMEM((TQ + 2 * QB, KV4), BF16)],
        compiler_params=_cparams("parallel", "parallel"),
    )(sink, q, k, k, k, v, v, v)


def _outproj_kernel(xp_ref, xs_ref, yf_ref, ya_ref, wo_ref, g_ref, wr_ref, x1_ref, xn_ref, aff_ref):
    mix = jnp.dot(yf_ref[...], wo_ref[:FOURIER_WIDTH, :], preferred_element_type=F32)
    mix = mix + jnp.dot(ya_ref[...], wo_ref[FOURIER_WIDTH:, :], preferred_element_type=F32)
    x1 = _x_tile(xp_ref, xs_ref) + mix
    x1_ref[...] = x1
    ms = jnp.mean(x1 * x1, axis=-1, keepdims=True)
    xn = (x1 * lax.rsqrt(ms + EPS) * g_ref[...]).astype(BF16)
    xn_ref[...] = xn
    logits = lax.dot_general(wr_ref[...], xn, (((1,), (1,)), ((), ())), preferred_element_type=F32)
    mx = jnp.max(logits, axis=0, keepdims=True)
    ex = jnp.exp(logits - mx)
    aff_ref[...] = ex / jnp.sum(ex, axis=0, keepdims=True)


def _outproj(xp, xs, yf, ya, w_out, g, w_router_t):
    return pl.pallas_call(
        _outproj_kernel,
        grid=(N_TOK // TM,),
        in_specs=_x_specs() + [
            pl.BlockSpec((TM, FOURIER_WIDTH), lambda i: (i, 0)),
            pl.BlockSpec((TM, ATTN_WIDTH), lambda i: (i, 0)),
            pl.BlockSpec((D_MODEL, D_MODEL), lambda i: (0, 0)),
            pl.BlockSpec((1, D_MODEL), lambda i: (0, 0)),
            pl.BlockSpec((N_EXPERTS, D_MODEL), lambda i: (0, 0)),
        ],
        out_specs=[
            pl.BlockSpec((TM, D_MODEL), lambda i: (i, 0)),
            pl.BlockSpec((TM, D_MODEL), lambda i: (i, 0)),
            pl.BlockSpec((N_EXPERTS, TM), lambda i: (0, i)),
        ],
        out_shape=[
            jax.ShapeDtypeStruct((N_TOK, D_MODEL), F32),
            jax.ShapeDtypeStruct((N_TOK, D_MODEL), BF16),
            jax.ShapeDtypeStruct((N_EXPERTS, N_TOK), F32),
        ],
        compiler_params=_cparams("parallel"),
    )(xp, xs, yf, ya, w_out, g, w_router_t)


RB = 512
N_RB = N_TOK // RB
GROUP_TOKENS = (SEQ, 2 * SEQ)
GROUP0_BLOCKS = GROUP_TOKENS[0] // RB
SEL_BLOCKS = 4
assert GROUP0_BLOCKS % SEL_BLOCKS == 0 and N_RB % SEL_BLOCKS == 0
CHUNK = 8
TILE_ROWS = 256
CHUNKS_PER_TILE = TILE_ROWS // CHUNK
SLOT_ROWS = EC_CAPACITY_FACTOR * N_TOK // N_EXPERTS
assert all((EC_CAPACITY_FACTOR * n // N_EXPERTS) % CHUNK == 0 for n in GROUP_TOKENS)
SLOT_TILES = SLOT_ROWS // FF_TM
assert SLOT_TILES * FF_TM == SLOT_ROWS
SLOT_CHUNKS = SLOT_ROWS // CHUNK
TILE_SLOTS = 6
LOOKAHEAD = TILE_SLOTS - 2
TRASH_CHUNK = N_EXPERTS * SLOT_CHUNKS
HEAD_TRASH_CHUNK = TRASH_CHUNK + TILE_SLOTS * CHUNKS_PER_TILE
TRASH_CHUNKS = TILE_SLOTS * CHUNKS_PER_TILE + FF_TM // CHUNK
assert N_EXPERTS <= FF_TM // CHUNK and TRASH_CHUNKS % (FF_TM // CHUNK) == 0
MAX_CHUNKS = (N_EXPERTS * (RB + 2 * CHUNK)) // CHUNK
LIST_UNROLL = 4
LIST_LEN = (MAX_CHUNKS // CHUNKS_PER_TILE + 2) * CHUNKS_PER_TILE
CHUNK_SHAPE = (2 * CHUNK, D_MODEL // 2)


def _pack_chunks(rows_f32):
    n = rows_f32.shape[0] // CHUNK
    r3 = rows_f32.reshape(n, CHUNK, D_MODEL)
    half = D_MODEL // 2
    return jnp.concatenate([r3[:, :, :half], r3[:, :, half:]], axis=1).astype(BF16)


def _unpack_chunks(chunks_bf16):
    n = chunks_bf16.shape[0]
    c3 = chunks_bf16.astype(F32)
    left = c3[:, :CHUNK, :].reshape(n * CHUNK, D_MODEL // 2)
    right = c3[:, CHUNK:, :].reshape(n * CHUNK, D_MODEL // 2)
    return jnp.concatenate([left, right], axis=1).astype(BF16)


def _thresh_kernel(aff_ref, thr_ref, need_ref):
    first = 0
    for g, n in enumerate(GROUP_TOKENS):
        cap = float(EC_CAPACITY_FACTOR * n // N_EXPERTS)
        aff = aff_ref[:, first:first + n]
        first += n

        def body(i, cur, aff=aff, cap=cap):
            cand = cur | jnp.left_shift(jnp.int32(1), 30 - i)
            cnt = jnp.sum(jnp.where(aff >= pltpu.bitcast(cand, F32), 1.0, 0.0), axis=1, keepdims=True)
            return jnp.where(cnt >= cap, cand, cur)

        thr = pltpu.bitcast(lax.fori_loop(0, 31, body, jnp.zeros((N_EXPERTS, 1), jnp.int32)), F32)
        n_gt = jnp.sum(jnp.where(aff > thr, 1.0, 0.0), axis=1, keepdims=True)
        rows = slice(g * N_EXPERTS, (g + 1) * N_EXPERTS)
        thr_ref[rows, :] = jnp.broadcast_to(thr, (N_EXPERTS, LANES))
        need_ref[rows, :] = jnp.broadcast_to(cap - n_gt, (N_EXPERTS, LANES))


def _thresholds(aff_t):
    n_groups = len(GROUP_TOKENS)
    return pl.pallas_call(
        _thresh_kernel,
        out_shape=[jax.ShapeDtypeStruct((n_groups * N_EXPERTS, LANES), F32),
                   jax.ShapeDtypeStruct((n_groups * N_EXPERTS, LANES), F32)],
        compiler_params=pltpu.CompilerParams(vmem_limit_bytes=VMEM_LIMIT),
    )(aff_t)


def _select_kernel(aff_ref, thr_ref, need_ref, utri_ref, selw_ref, rank_ref, m_ref, eq_seen):
    step = pl.program_id(0)

    @pl.when((step == 0) | (step == GROUP0_BLOCKS // SEL_BLOCKS))
    def _():
        eq_seen[...] = jnp.zeros_like(eq_seen)

    thr = thr_ref[:, 0:1]
    need = need_ref[:, 0:1]
    utri = utri_ref[...]
    for b in range(SEL_BLOCKS):
        cols = slice(b * RB, (b + 1) * RB)
        aff = aff_ref[:, cols]
        eq = jnp.where(aff == thr, 1.0, 0.0)
        eq_before = jnp.dot(eq.astype(BF16), utri, preferred_element_type=F32) + eq_seen[...]
        sel = (aff > thr) | ((aff == thr) & (eq_before < need))
        eq_seen[...] += jnp.sum(eq, axis=1, keepdims=True)
        self = jnp.where(sel, 1.0, 0.0)
        rank_ref[:, cols] = jnp.dot(self.astype(BF16), utri, preferred_element_type=F32)
        selw_ref[:, cols] = jnp.where(sel, aff, 0.0)
        m_ref[b] = jnp.broadcast_to(jnp.sum(self, axis=1, keepdims=True), (N_EXPERTS, LANES))


def _select(aff_t, thr, need):
    utri = jnp.asarray(np.triu(np.ones((RB, RB), np.float32), 1), dtype=BF16)
    grp = lambda i: (jnp.where(i >= GROUP0_BLOCKS // SEL_BLOCKS, 1, 0), 0)
    return pl.pallas_call(
        _select_kernel,
        grid=(N_RB // SEL_BLOCKS,),
        in_specs=[
            pl.BlockSpec((N_EXPERTS, SEL_BLOCKS * RB), lambda i: (0, i)),
            pl.BlockSpec((N_EXPERTS, LANES), grp),
            pl.BlockSpec((N_EXPERTS, LANES), grp),
            pl.BlockSpec((RB, RB), lambda i: (0, 0)),
        ],
        out_specs=[
            pl.BlockSpec((N_EXPERTS, SEL_BLOCKS * RB), lambda i: (0, i)),
            pl.BlockSpec((N_EXPERTS, SEL_BLOCKS * RB), lambda i: (0, i)),
            pl.BlockSpec((SEL_BLOCKS, N_EXPERTS, LANES), lambda i: (i, 0, 0)),
        ],
        out_shape=[
            jax.ShapeDtypeStruct((N_EXPERTS, N_TOK), F32),
            jax.ShapeDtypeStruct((N_EXPERTS, N_TOK), F32),
            jax.ShapeDtypeStruct((N_RB, N_EXPERTS, LANES), F32),
        ],
        scratch_shapes=[pltpu.VMEM((N_EXPERTS, 1), F32)],
        compiler_params=_cparams("arbitrary"),
    )(aff_t, thr, need, utri)


def _fill_chunk_list(list_ref, which, rb, seg_start_ref, seg_rows_ref, filler):
    cnt = jnp.int32(0)
    for e in range(N_EXPERTS):
        base = e * SLOT_CHUNKS + seg_start_ref[rb, e] // CHUNK
        n = seg_rows_ref[rb, e] // CHUNK

        def body(k, carry, base=base, cnt=cnt):
            for u in range(LIST_UNROLL):
                list_ref[which, cnt + LIST_UNROLL * k + u] = base + LIST_UNROLL * k + u
            return carry

        lax.fori_loop(0, (n + LIST_UNROLL - 1) // LIST_UNROLL, body, 0)
        cnt = cnt + n
    for j in range(CHUNKS_PER_TILE):
        list_ref[which, cnt + j] = filler(cnt + j)
    return cnt


WIN_ROWS = 128
SPAN_TILES = 8
SPAN_ROWS = SPAN_TILES * TILE_ROWS


def _mark_span(scr, rb, s0, n_rows, tgt, values, seg_off_ref, seg_rows_ref):
    win_rows = lax.broadcasted_iota(jnp.int32, (WIN_ROWS, RB), 0).astype(F32)
    n_here = jnp.minimum(n_rows - s0, SPAN_ROWS)

    def zero(i, carry):
        scr[pl.ds(pl.multiple_of(i * TILE_ROWS, TILE_ROWS), TILE_ROWS), :] = jnp.zeros((TILE_ROWS, RB), F32)
        return carry

    lax.fori_loop(0, (n_here + TILE_ROWS - 1) // TILE_ROWS, zero, 0)
    for e in range(N_EXPERTS):
        lo = seg_off_ref[rb, e]
        first = jnp.maximum(lo, s0)
        last = jnp.minimum(lo + seg_rows_ref[rb, e], s0 + SPAN_ROWS)

        def mark(w, carry, e=e, first=first):
            start = first + w * WIN_ROWS
            win = pl.ds(pl.multiple_of(start - s0, CHUNK), WIN_ROWS)
            scr[win, :] = jnp.where(tgt[e:e + 1, :] == win_rows + start.astype(F32), values[e], scr[win, :])
            return carry

        lax.fori_loop(0, jnp.maximum(last - first + WIN_ROWS - 1, 0) // WIN_ROWS, mark, 0)


def _expert_column(ref, rb):
    e_iota = lax.broadcasted_iota(jnp.int32, (N_EXPERTS, 1), 0)
    col = jnp.zeros((N_EXPERTS, 1), F32)
    for e in range(N_EXPERTS):
        col = jnp.where(e_iota == e, ref[rb, e].astype(F32), col)
    return col


def _gather_kernel(in_start_ref, in_rows_ref, in_off_ref, tot_ref, in_rank_ref, seg_start_ref, seg_len_ref,
                   x_ref, selw_ref, rank_ref, xe_hbm,
                   p_scr, obuf, hbuf, carry, zbuf, dst_list, pending, sem, hsem, zsem):
    rb = pl.program_id(0)
    n_rows = tot_ref[rb]
    n_tiles = (n_rows + TILE_ROWS - 1) // TILE_ROWS
    head_slot = TILE_SLOTS

    @pl.when(rb == 0)
    def _():
        for slot in range(TILE_SLOTS + 1):
            pending[slot] = 0
        p_scr[...] = jnp.zeros_like(p_scr)
        carry[...] = jnp.zeros_like(carry)

    _fill_chunk_list(dst_list, 0, rb, in_start_ref, in_rows_ref,
                     lambda c: TRASH_CHUNK + c % (TILE_SLOTS * CHUNKS_PER_TILE))

    def tile_copy(slot):
        return pltpu.make_async_copy(obuf.at[slot], xe_hbm.at[pl.ds(0, CHUNKS_PER_TILE)], sem.at[slot])

    def head_copy():
        return pltpu.make_async_copy(hbuf, xe_hbm.at[pl.ds(0, N_EXPERTS)], hsem)

    x = x_ref[...]
    rank = rank_ref[...]
    sel = selw_ref[...] > 0.0

    j8 = lax.broadcasted_iota(jnp.int32, (CHUNK, RB), 0).astype(F32)
    heads, tails, flags = [], [], []
    for e in range(N_EXPERTS):
        start = seg_start_ref[rb, e]
        end = start + seg_len_ref[rb, e]
        nonempty = seg_len_ref[rb, e] > 0
        has_head = nonempty & (start % CHUNK != 0)
        has_tail = nonempty & (end % CHUNK != 0)
        single = has_head & has_tail & (start // CHUNK == end // CHUNK)
        head_shift = (start % CHUNK).astype(F32)
        tail_rank0 = ((end // CHUNK) * CHUNK - start).astype(F32)
        r_e = rank[e:e + 1, :]
        s_e = sel[e:e + 1, :]
        heads.append(jnp.where(s_e & (r_e + head_shift == j8), has_head.astype(F32), 0.0))
        tails.append(jnp.where(s_e & (r_e - tail_rank0 == j8), (has_tail & ~single).astype(F32), 0.0))
        flags.append((has_head, has_tail, single, start // CHUNK))
    p_edge = jnp.concatenate(heads + tails, axis=0).astype(BF16)
    edge = jnp.dot(p_edge, x, preferred_element_type=F32)

    @pl.when(pending[head_slot] == 1)
    def _():
        head_copy().wait()

    head_rows = []
    for e, (has_head, has_tail, single, head_chunk) in enumerate(flags):
        rows = slice(e * CHUNK, (e + 1) * CHUNK)
        old = carry[rows, :]
        head = edge[rows, :] + old
        tail = edge[N_EXPERTS * CHUNK + e * CHUNK:N_EXPERTS * CHUNK + (e + 1) * CHUNK, :]
        head_rows.append(head)
        keep = ~single & ~has_tail & ~has_head
        carry[rows, :] = (single.astype(F32) * head + (has_tail & ~single).astype(F32) * tail
                          + keep.astype(F32) * old)
    hbuf[...] = _pack_chunks(jnp.concatenate(head_rows, axis=0))
    for e, (has_head, has_tail, single, head_chunk) in enumerate(flags):
        dst = jnp.where(has_head & ~single, e * SLOT_CHUNKS + head_chunk, HEAD_TRASH_CHUNK + e)
        pltpu.make_async_copy(hbuf.at[e], xe_hbm.at[dst], hsem).start()
    pending[head_slot] = 1

    off = _expert_column(in_off_ref, rb)
    rank0 = _expert_column(in_rank_ref, rb)
    n_in = _expert_column(in_rows_ref, rb)
    tgt = jnp.where(sel & (rank >= rank0) & (rank < rank0 + n_in), rank - rank0 + off, -1.0)

    def tile_body(t, carry_):
        slot = t % TILE_SLOTS

        @pl.when(pending[slot] == 1)
        def _():
            tile_copy(slot).wait()

        @pl.when(t % SPAN_TILES == 0)
        def _():
            _mark_span(p_scr, rb, t * TILE_ROWS, n_rows, tgt, [1.0] * N_EXPERTS, in_off_ref, in_rows_ref)

        p = p_scr[pl.ds(pl.multiple_of((t % SPAN_TILES) * TILE_ROWS, TILE_ROWS), TILE_ROWS), :]
        obuf[slot] = _pack_chunks(jnp.dot(p.astype(BF16), x, preferred_element_type=F32))

        for i in range(CHUNKS_PER_TILE):
            pltpu.make_async_copy(obuf.at[slot, i], xe_hbm.at[dst_list[0, t * CHUNKS_PER_TILE + i]],
                                  sem.at[slot]).start()
        pending[slot] = 1
        return carry_

    lax.fori_loop(0, n_tiles, tile_body, 0)

    @pl.when(rb == N_RB - 1)
    def _():
        for slot in range(TILE_SLOTS):
            @pl.when(pending[slot] == 1)
            def _(slot=slot):
                tile_copy(slot).wait()
        head_copy().wait()
        zbuf[...] = jnp.zeros_like(zbuf)

        def zcopy(k):
            return pltpu.make_async_copy(zbuf, xe_hbm.at[TRASH_CHUNK + k], zsem)

        def zstart(k, c):
            zcopy(k).start()
            return c

        def zwait(k, c):
            zcopy(k).wait()
            return c

        lax.fori_loop(0, TRASH_CHUNKS, zstart, 0)
        lax.fori_loop(0, TRASH_CHUNKS, zwait, 0)


def _route_gather(tables, xn, selw_t, rank_t):
    gs = pltpu.PrefetchScalarGridSpec(
        num_scalar_prefetch=len(tables),
        grid=(N_RB,),
        in_specs=[
            pl.BlockSpec((RB, D_MODEL), lambda rb, *_: (rb, 0)),
            pl.BlockSpec((N_EXPERTS, RB), lambda rb, *_: (0, rb)),
            pl.BlockSpec((N_EXPERTS, RB), lambda rb, *_: (0, rb)),
        ],
        out_specs=pl.BlockSpec(memory_space=pl.ANY),
        scratch_shapes=[
            pltpu.VMEM((SPAN_ROWS + WIN_ROWS, RB), F32),
            pltpu.VMEM((TILE_SLOTS, CHUNKS_PER_TILE) + CHUNK_SHAPE, BF16),
            pltpu.VMEM((N_EXPERTS,) + CHUNK_SHAPE, BF16),
            pltpu.VMEM((N_EXPERTS * CHUNK, D_MODEL), F32),
            pltpu.VMEM(CHUNK_SHAPE, BF16),
            pltpu.SMEM((1, LIST_LEN), jnp.int32),
            pltpu.SMEM((TILE_SLOTS + 1,), jnp.int32),
            pltpu.SemaphoreType.DMA((TILE_SLOTS,)),
            pltpu.SemaphoreType.DMA,
            pltpu.SemaphoreType.DMA,
        ],
    )
    return pl.pallas_call(
        _gather_kernel,
        grid_spec=gs,
        out_shape=jax.ShapeDtypeStruct((TRASH_CHUNK + TRASH_CHUNKS,) + CHUNK_SHAPE, BF16),
        compiler_params=_cparams("arbitrary"),
    )(*tables, xn, selw_t, rank_t)


W_PIECES_PER_MATRIX = 4
W_PIECES = 3 * W_PIECES_PER_MATRIX
FFN_STEPS = max(SLOT_TILES, W_PIECES + 1)
WG_PIECE_ROWS = D_MODEL // W_PIECES_PER_MATRIX
WD_PIECE_ROWS = EXPERT_D_FF // W_PIECES_PER_MATRIX
FFN_VMEM_LIMIT = 56 * 1024 * 1024


def _ffn_kernel(x_ref, wg_hbm, wu_hbm, wd_hbm, o_ref,
                wg_bf, wu_bf, wd_bf, stage_up, stage_dn, sem):
    e = pl.program_id(0)
    i = pl.program_id(1)
    slot = e % 2

    matrices = ((wg_hbm, stage_up, wg_bf, WG_PIECE_ROWS),
                (wu_hbm, stage_up, wu_bf, WG_PIECE_ROWS),
                (wd_hbm, stage_dn, wd_bf, WD_PIECE_ROWS))

    def piece(expert, k, buf, finish):
        def of_kind(kind):
            w_hbm, stage, w_bf, rows = matrices[kind]
            r0 = pl.multiple_of((k - kind * W_PIECES_PER_MATRIX) * rows, rows)
            cp = pltpu.make_async_copy(w_hbm.at[expert, pl.ds(r0, rows), :], stage.at[buf], sem.at[buf])
            if finish:
                cp.wait()
                w_bf[expert % 2, pl.ds(r0, rows), :] = stage[buf].astype(BF16)
            else:
                cp.start()

        if isinstance(k, int):
            of_kind(k // W_PIECES_PER_MATRIX)
        else:
            for kind in range(len(matrices)):
                pl.when(k // W_PIECES_PER_MATRIX == kind)(functools.partial(of_kind, kind))

    @pl.when((e == 0) & (i == 0))
    def _():
        for k in range(W_PIECES):
            piece(0, k, k % 2, finish=False)
            piece(0, k, k % 2, finish=True)

    @pl.when(e + 1 < N_EXPERTS)
    def _():
        @pl.when((i >= 1) & (i <= W_PIECES))
        def _():
            piece(e + 1, i - 1, (i - 1) % 2, finish=True)

        @pl.when(i < W_PIECES)
        def _():
            piece(e + 1, i, i % 2, finish=False)

    @pl.when(i < SLOT_TILES)
    def _():
        x = _unpack_chunks(x_ref[...])
        acc = None
        for c in range(EXPERT_D_FF // FF_FC):
            gt = jnp.dot(x, wg_bf[slot, :, c * FF_FC:(c + 1) * FF_FC], preferred_element_type=F32)
            up = jnp.dot(x, wu_bf[slot, :, c * FF_FC:(c + 1) * FF_FC], preferred_element_type=F32)
            h = (gt * (1.0 / (1.0 + jnp.exp(-gt))) * up).astype(BF16)
            down = jnp.dot(h, wd_bf[slot, c * FF_FC:(c + 1) * FF_FC, :], preferred_element_type=F32)
            acc = down if acc is None else acc + down
        o_ref[...] = _pack_chunks(acc)


def _ffn(xe, w_gate, w_up, w_down):
    def rows(e, i):
        return (e * SLOT_TILES + jnp.minimum(i, SLOT_TILES - 1), 0, 0)

    tile_chunks = (FF_TM // CHUNK,) + CHUNK_SHAPE
    gs = pltpu.PrefetchScalarGridSpec(
        num_scalar_prefetch=0,
        grid=(N_EXPERTS, FFN_STEPS),
        in_specs=[
            pl.BlockSpec(tile_chunks, rows),
            pl.BlockSpec(memory_space=pl.ANY),
            pl.BlockSpec(memory_space=pl.ANY),
            pl.BlockSpec(memory_space=pl.ANY),
        ],
        out_specs=pl.BlockSpec(tile_chunks, rows),
        scratch_shapes=[
            pltpu.VMEM((2, D_MODEL, EXPERT_D_FF), BF16),
            pltpu.VMEM((2, D_MODEL, EXPERT_D_FF), BF16),
            pltpu.VMEM((2, EXPERT_D_FF, D_MODEL), BF16),
            pltpu.VMEM((2, WG_PIECE_ROWS, EXPERT_D_FF), F32),
            pltpu.VMEM((2, WD_PIECE_ROWS, D_MODEL), F32),
            pltpu.SemaphoreType.DMA((2,)),
        ],
    )
    return pl.pallas_call(
        _ffn_kernel,
        grid_spec=gs,
        out_shape=jax.ShapeDtypeStruct((N_EXPERTS * SLOT_CHUNKS,) + CHUNK_SHAPE, BF16),
        compiler_params=pltpu.CompilerParams(dimension_semantics=("arbitrary", "arbitrary"),
                                             vmem_limit_bytes=FFN_VMEM_LIMIT),
    )(xe, w_gate, w_up, w_down)


def _combine_kernel(seg_start_ref, seg_rows_ref, seg_off_ref, tot_ref, seg_shift_ref,
                    x1_ref, selw_ref, rank_ref, g_ref, ye_hbm, op_ref, os_ref,
                    w_scr, ybuf, acc, src_list, state, sem):
    rb = pl.program_id(0)
    cur = rb % 2
    nxt_rb = jnp.minimum(rb + 1, N_RB - 1)

    def tiles_of(b):
        return (tot_ref[b] + TILE_ROWS - 1) // TILE_ROWS

    n_tiles = tiles_of(rb)
    n_tiles_next = jnp.where(rb + 1 < N_RB, tiles_of(nxt_rb), 0)

    def fetch(which, t, slot):
        for i in range(CHUNKS_PER_TILE):
            pltpu.make_async_copy(ye_hbm.at[src_list[which, t * CHUNKS_PER_TILE + i]], ybuf.at[slot, i],
                                  sem.at[slot]).start()

    def tile_wait(slot):
        pltpu.make_async_copy(ye_hbm.at[pl.ds(0, CHUNKS_PER_TILE)], ybuf.at[slot], sem.at[slot]).wait()

    @pl.when(rb == 0)
    def _():
        w_scr[...] = jnp.zeros_like(w_scr)
        _fill_chunk_list(src_list, 0, 0, seg_start_ref, seg_rows_ref, lambda c: 0)
        state[0] = 0
        state[1] = 0

    @pl.when(rb + 1 < N_RB)
    def _():
        _fill_chunk_list(src_list, 1 - cur, nxt_rb, seg_start_ref, seg_rows_ref, lambda c: 0)

    slot0 = state[0]
    started = state[1]
    for j in range(LOOKAHEAD):
        @pl.when((j >= started) & (j < n_tiles))
        def _(j=j):
            fetch(cur, j, (slot0 + j) % TILE_SLOTS)

    off = _expert_column(seg_off_ref, rb) + _expert_column(seg_shift_ref, rb)
    selw = selw_ref[...]
    gates = [selw[e:e + 1, :] for e in range(N_EXPERTS)]
    tgt = jnp.where(selw > 0.0, rank_ref[...] + off, -1.0)
    acc[...] = x1_ref[...]

    def prefetch(t, next_started):
        ahead = t + LOOKAHEAD

        @pl.when(ahead < n_tiles)
        def _():
            fetch(cur, ahead, (slot0 + ahead) % TILE_SLOTS)

        start_next = (ahead >= n_tiles) & (next_started < jnp.minimum(LOOKAHEAD, n_tiles_next))

        @pl.when(start_next)
        def _():
            fetch(1 - cur, next_started, (slot0 + n_tiles + next_started) % TILE_SLOTS)

        return next_started + start_next.astype(jnp.int32)

    def add_tiles(t, n):
        w = w_scr[pl.ds(pl.multiple_of((t % SPAN_TILES) * TILE_ROWS, TILE_ROWS), n * TILE_ROWS), :]
        ys = [_unpack_chunks(ybuf[(slot0 + t + k) % TILE_SLOTS]) for k in range(n)]
        y = ys[0] if n == 1 else jnp.concatenate(ys, axis=0)
        acc[...] += lax.dot_general(w.astype(BF16), y, (((0,), (0,)), ((), ())),
                                    preferred_element_type=F32)

    def pair_body(p, next_started):
        t = 2 * p
        has_second = t + 1 < n_tiles
        next_started = prefetch(t + 1, prefetch(t, next_started))

        @pl.when(t % SPAN_TILES == 0)
        def _():
            _mark_span(w_scr, rb, t * TILE_ROWS, tot_ref[rb], tgt, gates, seg_off_ref, seg_rows_ref)

        tile_wait((slot0 + t) % TILE_SLOTS)

        @pl.when(has_second)
        def _():
            tile_wait((slot0 + t + 1) % TILE_SLOTS)
            add_tiles(t, 2)

        @pl.when(~has_second)
        def _():
            add_tiles(t, 1)

        return next_started

    state[1] = lax.fori_loop(0, (n_tiles + 1) // 2, pair_body, jnp.int32(0))
    state[0] = (slot0 + n_tiles) % TILE_SLOTS
    x2 = acc[...]
    ms = jnp.mean(x2 * x2, axis=-1, keepdims=True)
    y = x2 * lax.rsqrt(ms + EPS) * g_ref[...]

    @pl.when(rb < GROUP0_BLOCKS)
    def _():
        op_ref[...] = y

    @pl.when(rb >= GROUP0_BLOCKS)
    def _():
        os_ref[...] = y


def _combine(tables, x1, selw_t, rank_t, g, ye):
    gs = pltpu.PrefetchScalarGridSpec(
        num_scalar_prefetch=len(tables),
        grid=(N_RB,),
        in_specs=[
            pl.BlockSpec((RB, D_MODEL), lambda rb, *_: (rb, 0)),
            pl.BlockSpec((N_EXPERTS, RB), lambda rb, *_: (0, rb)),
            pl.BlockSpec((N_EXPERTS, RB), lambda rb, *_: (0, rb)),
            pl.BlockSpec((1, D_MODEL), lambda rb, *_: (0, 0)),
            pl.BlockSpec(memory_space=pl.ANY),
        ],
        out_specs=[
            pl.BlockSpec((RB, D_MODEL), lambda rb, *_: (jnp.minimum(rb, GROUP0_BLOCKS - 1), 0)),
            pl.BlockSpec((RB, D_MODEL), lambda rb, *_: (jnp.maximum(rb - GROUP0_BLOCKS, 0), 0)),
        ],
        scratch_shapes=[
            pltpu.VMEM((SPAN_ROWS + WIN_ROWS, RB), F32),
            pltpu.VMEM((TILE_SLOTS, CHUNKS_PER_TILE) + CHUNK_SHAPE, BF16),
            pltpu.VMEM((RB, D_MODEL), F32),
            pltpu.SMEM((2, LIST_LEN), jnp.int32),
            pltpu.SMEM((2,), jnp.int32),
            pltpu.SemaphoreType.DMA((TILE_SLOTS,)),
        ],
    )
    return pl.pallas_call(
        _combine_kernel,
        grid_spec=gs,
        out_shape=[jax.ShapeDtypeStruct((GROUP_TOKENS[0], D_MODEL), F32),
                   jax.ShapeDtypeStruct((GROUP_TOKENS[1], D_MODEL), F32)],
        compiler_params=_cparams("arbitrary"),
    )(*tables, x1, selw_t, rank_t, g, ye)


def kernel(x_prompt, x_sample, norm_mix_g, w_in, w_fourier, sink, w_out, norm_ffn_g, w_router,
           w_gate, w_up, w_down, norm_final_g):
    assert x_prompt.shape == (1, SEQ, D_MODEL) and x_sample.shape == (N_SEQ - 1, SEQ, D_MODEL)
    assert w_in.shape == (1, D_MODEL, IN_WIDTH) and w_out.shape == (1, MIX_WIDTH, D_MODEL)
    assert w_gate.shape == (1, N_EXPERTS, D_MODEL, EXPERT_D_FF) and w_down.shape == (1, N_EXPERTS, EXPERT_D_FF, D_MODEL)
    xp = x_prompt.reshape(-1, D_MODEL)
    xs = x_sample.reshape(-1, D_MODEL)
    cos_t, sin_t = _rope_tables()

    mix_cols = np.concatenate([
        np.arange(FOURIER_WIDTH),
        FOURIER_WIDTH + (np.asarray(Q_ORDER)[:, None] * HEAD_DIM + np.arange(HEAD_DIM)[None, :]).reshape(-1)])
    in_cols = np.concatenate([mix_cols, np.arange(MIX_WIDTH, IN_WIDTH)])
    w_in_r = w_in[0][:, in_cols].astype(BF16)
    w_out_r = w_out[0][mix_cols, :].astype(BF16)

    u, q, k, v = _inproj(xp, xs, norm_mix_g[0][None, :], w_in_r, cos_t, sin_t)
    yf = _fourier(u, w_fourier[0].astype(BF16))
    ya = _attention(sink[0], q, k, v)
    x1, xn, aff_t = _outproj(xp, xs, yf, ya, w_out_r, norm_ffn_g[0][None, :],
                             w_router[0].T.astype(BF16))

    thr, need = _thresholds(aff_t)
    selw_t, rank_t, m = _select(aff_t, thr, need)

    seg_len = m[:, :, 0].astype(jnp.int32)
    seg_start = jnp.cumsum(seg_len, axis=0) - seg_len
    seg_end = seg_start + seg_len

    def offsets(rows):
        return jnp.cumsum(rows, axis=1) - rows, jnp.sum(rows, axis=1)

    in_start = -(-seg_start // CHUNK) * CHUNK
    in_rows = jnp.maximum(seg_end // CHUNK * CHUNK - in_start, 0)
    in_off, in_tot = offsets(in_rows)
    gather_tables = (in_start, in_rows, in_off, in_tot, in_start - seg_start, seg_start, seg_len)
    out_start = seg_start // CHUNK * CHUNK
    out_rows = jnp.where(seg_len > 0, -(-seg_end // CHUNK) * CHUNK - out_start, 0)
    out_off, out_tot = offsets(out_rows)
    combine_tables = (out_start, out_rows, out_off, out_tot, seg_start - out_start)

    xe = _route_gather(gather_tables, xn, selw_t, rank_t)
    ye = _ffn(xe, w_gate[0], w_up[0], w_down[0])
    yp, ys = _combine(combine_tables, x1, selw_t, rank_t, norm_final_g[None, :], ye)
    return (yp.reshape(x_prompt.shape), ys.reshape(x_sample.shape))
```

```python
import functools

import numpy as np
import jax
import jax.numpy as jnp
from jax import lax
from jax.experimental import pallas as pl
from jax.experimental.pallas import tpu as pltpu

F32 = jnp.float32
BF16 = jnp.bfloat16

D_MODEL = 1024
SEQ = 16384
N_SEQ = 3
N_TOK = N_SEQ * SEQ
HEAD_DIM = 64
N_HEADS = 12
N_KV_HEADS = 4
GQA_GROUP = N_HEADS // N_KV_HEADS
ATTN_WIDTH = N_HEADS * HEAD_DIM
KV_WIDTH = N_KV_HEADS * HEAD_DIM
FOURIER_WIDTH = 256
FOURIER_GROUP_DIM = 64
MIX_WIDTH = FOURIER_WIDTH + ATTN_WIDTH
IN_WIDTH = MIX_WIDTH + 2 * KV_WIDTH
WINDOW = 128
ROPE_THETA = 10000.0
N_EXPERTS = 16
EC_CAPACITY_FACTOR = 2
EXPERT_D_FF = 2048
EPS = 1e-6
NEG = -1e30

LANES = 128
FFT_R = 128
VMEM_LIMIT = 48 * 1024 * 1024

TM = 1024
TQ = 1024
QB = 128
FF_TM = 512
FF_FC = 512


def _cparams(*sem):
    return pltpu.CompilerParams(dimension_semantics=sem, vmem_limit_bytes=VMEM_LIMIT)


def _x_specs():
    n_p = SEQ // TM
    return [pl.BlockSpec((TM, D_MODEL), lambda i, *_: (jnp.minimum(i, n_p - 1), 0)),
            pl.BlockSpec((TM, D_MODEL), lambda i, *_: (jnp.maximum(i - n_p, 0), 0))]


def _x_tile(xp_ref, xs_ref):
    return jnp.where(pl.program_id(0) < SEQ // TM, xp_ref[...], xs_ref[...])


def _inproj_kernel(xp_ref, xs_ref, g_ref, w_ref, cos_ref, sin_ref, u_ref, q_ref, k_ref, v_ref):
    x = _x_tile(xp_ref, xs_ref)
    ms = jnp.mean(x * x, axis=-1, keepdims=True)
    hn = (x * lax.rsqrt(ms + EPS) * g_ref[...]).astype(BF16)
    proj = jnp.dot(hn, w_ref[...], preferred_element_type=F32)
    u_ref[...] = proj[:, :FOURIER_WIDTH].astype(BF16)
    cos = cos_ref[...]
    sin = sin_ref[...]
    lane = lax.broadcasted_iota(jnp.int32, cos.shape, 1)
    first_half = (lane % HEAD_DIM) < (HEAD_DIM // 2)

    def rope(t):
        partner = jnp.where(first_half,
                            pltpu.roll(t, LANES - HEAD_DIM // 2, 1),
                            pltpu.roll(t, HEAD_DIM // 2, 1))
        return t * cos + partner * sin

    scale = HEAD_DIM ** -0.5 * LOG2E
    for c in range(ATTN_WIDTH // LANES):
        lo = FOURIER_WIDTH + c * LANES
        q_ref[:, c * LANES:(c + 1) * LANES] = (rope(proj[:, lo:lo + LANES]) * scale).astype(BF16)
    low_half = lane < HEAD_DIM
    for c in range(KV_WIDTH // LANES):
        lo = FOURIER_WIDTH + ATTN_WIDTH + c * LANES
        kc = rope(proj[:, lo:lo + LANES])
        vc = proj[:, lo + KV_WIDTH:lo + KV_WIDTH + LANES]
        for half, keep in enumerate((low_half, ~low_half)):
            g = 2 * c + half
            k_ref[:, g * LANES:(g + 1) * LANES] = jnp.where(keep, kc, 0.0).astype(BF16)
            v_ref[:, g * LANES:(g + 1) * LANES] = jnp.where(keep, vc, 0.0).astype(BF16)


def _inproj(xp, xs, g, w_in, cos_t, sin_t):
    n_pos_blocks = SEQ // TM
    return pl.pallas_call(
        _inproj_kernel,
        grid=(N_TOK // TM,),
        in_specs=_x_specs() + [
            pl.BlockSpec((1, D_MODEL), lambda i: (0, 0)),
            pl.BlockSpec((D_MODEL, IN_WIDTH), lambda i: (0, 0)),
            pl.BlockSpec((TM, LANES), lambda i: (i % n_pos_blocks, 0)),
            pl.BlockSpec((TM, LANES), lambda i: (i % n_pos_blocks, 0)),
        ],
        out_specs=[
            pl.BlockSpec((TM, FOURIER_WIDTH), lambda i: (i, 0)),
            pl.BlockSpec((TM, ATTN_WIDTH), lambda i: (i, 0)),
            pl.BlockSpec((TM, KV4), lambda i: (i, 0)),
            pl.BlockSpec((TM, KV4), lambda i: (i, 0)),
        ],
        out_shape=[
            jax.ShapeDtypeStruct((N_TOK, FOURIER_WIDTH), BF16),
            jax.ShapeDtypeStruct((N_TOK, ATTN_WIDTH), BF16),
            jax.ShapeDtypeStruct((N_TOK, KV4), BF16),
            jax.ShapeDtypeStruct((N_TOK, KV4), BF16),
        ],
        compiler_params=_cparams("parallel"),
    )(xp, xs, g, w_in, cos_t, sin_t)


def _rope_tables():
    half = HEAD_DIM // 2
    inv_freq = 1.0 / (ROPE_THETA ** (jnp.arange(half, dtype=F32) / half))
    ang = jnp.arange(SEQ, dtype=F32)[:, None] * inv_freq[None, :]
    cos = jnp.cos(ang)
    sin = jnp.sin(ang)
    reps = LANES // HEAD_DIM
    cos_t = jnp.tile(jnp.concatenate([cos, cos], axis=-1), (1, reps))
    sin_t = jnp.tile(jnp.concatenate([-sin, sin], axis=-1), (1, reps))
    return cos_t, sin_t


def _dft_tables():
    r = FFT_R
    c = np.arange(FOURIER_GROUP_DIM)
    ang_c = 2.0 * np.pi * np.outer(c, c) / FOURIER_GROUP_DIM
    n_groups = FOURIER_WIDTH // FOURIER_GROUP_DIM
    eye = np.eye(n_groups)
    ch_scale = FOURIER_GROUP_DIM ** -0.5
    c_blk = np.kron(eye, np.cos(ang_c)) * ch_scale
    s_blk = np.kron(eye, np.sin(ang_c)) * ch_scale
    cs = np.concatenate([c_blk, -s_blk], axis=1)
    k = np.arange(r)
    ang_r = 2.0 * np.pi * np.outer(k, k) / r
    st_scale = r ** -0.5
    wr = np.cos(ang_r) * st_scale
    wi = -np.sin(ang_r) * st_scale
    wbig = np.block([[wr, -wi], [wi, wr]])
    ang_t = 2.0 * np.pi * np.outer(k, k) / (r * r)
    tr = np.cos(ang_t)
    ti = -np.sin(ang_t)
    f = lambda a: jnp.asarray(a, dtype=F32)
    return f(cs).astype(BF16), f(wbig).astype(BF16), f(wr), f(wi), f(tr), f(ti)


FS_SB = 16
FS_KB = 16


def _fft_stage1_kernel(u_ref, cs_ref, wbig_ref, y_ref):
    cs = cs_ref[...]
    wbig = wbig_ref[...]
    fw = FOURIER_WIDTH
    for j in range(FS_SB):
        u = u_ref[:, j * fw:(j + 1) * fw]
        g = jnp.dot(u, cs, preferred_element_type=F32)
        gst = jnp.concatenate([g[:, :fw], g[:, fw:]], axis=0).astype(BF16)
        y = jnp.dot(wbig, gst, preferred_element_type=F32)
        y_ref[:, j * 2 * fw:j * 2 * fw + fw] = y[:FFT_R].astype(BF16)
        y_ref[:, j * 2 * fw + fw:(j + 1) * 2 * fw] = y[FFT_R:].astype(BF16)


def _fft_stage2_kernel(y_ref, wr_ref, wi_ref, tr_ref, ti_ref, wf_ref, o_ref):
    kb = pl.program_id(1)
    wr = wr_ref[...]
    wi = wi_ref[...]
    wf = wf_ref[...]
    fw = FOURIER_WIDTH
    for j in range(FS_KB):
        k1 = kb * FS_KB + j
        tr = tr_ref[pl.ds(k1, 1), :]
        ti = ti_ref[pl.ds(k1, 1), :]
        mr = wr * tr - wi * ti
        mi = wr * ti + wi * tr
        m2 = jnp.concatenate([mr, -mi], axis=1).astype(BF16)
        yb = y_ref[j * FFT_R:(j + 1) * FFT_R, :]
        yst = jnp.concatenate([yb[:, :fw], yb[:, fw:]], axis=0)
        z = jnp.dot(m2, yst, preferred_element_type=F32)
        yf = jnp.dot(z.astype(BF16), wf, preferred_element_type=F32)
        o_ref[:, j * fw:(j + 1) * fw] = yf.astype(BF16)


def _fourier(u, w_fourier):
    cs, wbig, wr, wi, tr, ti = _dft_tables()
    r, fw = FFT_R, FOURIER_WIDTH
    u2 = u.reshape(N_SEQ * r, r * fw)
    y = pl.pallas_call(
        _fft_stage1_kernel,
        grid=(N_SEQ, r // FS_SB),
        in_specs=[
            pl.BlockSpec((r, FS_SB * fw), lambda s, j: (s, j)),
            pl.BlockSpec((fw, 2 * fw), lambda s, j: (0, 0)),
            pl.BlockSpec((2 * r, 2 * r), lambda s, j: (0, 0)),
        ],
        out_specs=pl.BlockSpec((r, FS_SB * 2 * fw), lambda s, j: (s, j)),
        out_shape=jax.ShapeDtypeStruct((N_SEQ * r, r * 2 * fw), BF16),
        compiler_params=_cparams("parallel", "parallel"),
    )(u2, cs, wbig)
    y2 = y.reshape(N_SEQ * r * r, 2 * fw)
    full = lambda s, j: (0, 0)
    yf = pl.pallas_call(
        _fft_stage2_kernel,
        grid=(N_SEQ, r // FS_KB),
        in_specs=[
            pl.BlockSpec((FS_KB * r, 2 * fw), lambda s, j: (s * (r // FS_KB) + j, 0)),
            pl.BlockSpec((r, r), full),
            pl.BlockSpec((r, r), full),
            pl.BlockSpec((r, r), full),
            pl.BlockSpec((r, r), full),
            pl.BlockSpec((fw, fw), full),
        ],
        out_specs=pl.BlockSpec((r, FS_KB * fw), lambda s, j: (s, j)),
        out_shape=jax.ShapeDtypeStruct((N_SEQ * r, r * fw), BF16),
        compiler_params=_cparams("parallel", "parallel"),
    )(y2, wr, wi, tr, ti, w_fourier)
    return yf.reshape(N_TOK, fw)


LO_HEADS = (0, 1, 2, 6, 7, 8)
HI_HEADS = (3, 4, 5, 9, 10, 11)
Q_ORDER = tuple(h for pair in zip(LO_HEADS, HI_HEADS) for h in pair)
assert all(h // GQA_GROUP % 2 == 0 for h in LO_HEADS) and all(h // GQA_GROUP % 2 == 1 for h in HI_HEADS)
KV4 = 2 * KV_WIDTH
LOG2E = 1.4426950408889634


def _attn_kernel(sink_ref, q_ref, kp_ref, kc_ref, kn_ref, vp_ref, vc_ref, vn_ref, o_ref,
                 kext, vext):
    i = pl.program_id(1)
    n_i = pl.num_programs(1)
    kext[0:QB, :] = kp_ref[...]
    kext[QB:QB + TQ, :] = kc_ref[...]
    kext[QB + TQ:, :] = kn_ref[...]
    vext[0:QB, :] = vp_ref[...]
    vext[QB:QB + TQ, :] = vc_ref[...]
    vext[QB + TQ:, :] = vn_ref[...]

    rows = GQA_GROUP * QB
    a = lax.broadcasted_iota(jnp.int32, (rows, QB), 0) % QB
    j = lax.broadcasted_iota(jnp.int32, (rows, QB), 1)
    band_prev = jnp.where(j >= a, 0.0, NEG)
    band_next = jnp.where(j <= a, 0.0, NEG)
    row_id = lax.broadcasted_iota(jnp.int32, (rows, 1), 0)
    low_half = lax.broadcasted_iota(jnp.int32, (rows, LANES), 1) < HEAD_DIM

    def softmax_half(s, sinks, bias_prev, bias_next):
        s_p = s[:, :QB] + bias_prev
        s_c = s[:, QB:2 * QB]
        s_n = s[:, 2 * QB:] + bias_next
        sink = jnp.where(row_id < QB, sinks[0], jnp.where(row_id < 2 * QB, sinks[1], sinks[2])) * LOG2E
        m = jnp.max(jnp.maximum(jnp.maximum(s_p, s_c), s_n), axis=-1, keepdims=True)
        m = jnp.maximum(m, sink)
        p_p = jnp.exp2(s_p - m)
        p_c = jnp.exp2(s_c - m)
        p_n = jnp.exp2(s_n - m)
        denom = jnp.sum(p_p + p_c + p_n, axis=-1, keepdims=True) + jnp.exp2(sink - m)
        return [p_p.astype(BF16), p_c.astype(BF16), p_n.astype(BF16)], 1.0 / denom

    for sb in range(TQ // QB):
        blk = i * (TQ // QB) + sb
        bias_prev = band_prev + jnp.where(blk == 0, NEG, 0.0)
        bias_next = band_next + jnp.where(blk == n_i * (TQ // QB) - 1, NEG, 0.0)
        for pr in range(N_KV_HEADS // 2):
            groups = [GQA_GROUP * pr + m for m in range(GQA_GROUP)]
            q3 = jnp.concatenate([q_ref[sb * QB:(sb + 1) * QB, g * LANES:(g + 1) * LANES] for g in groups],
                                 axis=0)
            kp = kext[sb * QB:sb * QB + 3 * QB, pr * 2 * LANES:(pr + 1) * 2 * LANES]
            vp = vext[sb * QB:sb * QB + 3 * QB, pr * 2 * LANES:(pr + 1) * 2 * LANES]
            kcat = jnp.concatenate([kp[:, :LANES], kp[:, LANES:]], axis=0)
            vcat = jnp.concatenate([vp[:, :LANES], vp[:, LANES:]], axis=0)
            s = lax.dot_general(q3, kcat, (((1,), (1,)), ((), ())), preferred_element_type=F32)
            p_lo, r_lo = softmax_half(s[:, :3 * QB], [sink_ref[LO_HEADS[g]] for g in groups],
                                      bias_prev, bias_next)
            p_hi, r_hi = softmax_half(s[:, 3 * QB:], [sink_ref[HI_HEADS[g]] for g in groups],
                                      bias_prev, bias_next)
            p = jnp.concatenate(p_lo + p_hi, axis=1)
            o = jnp.dot(p, vcat, preferred_element_type=F32) * jnp.where(low_half, r_lo, r_hi)
            for m, g in enumerate(groups):
                o_ref[sb * QB:(sb + 1) * QB, g * LANES:(g + 1) * LANES] = o[m * QB:(m + 1) * QB].astype(BF16)


def _attention(sink, q, k, v):
    nqb = SEQ // QB
    per = TQ // QB
    cur = lambda s, i: (s * (SEQ // TQ) + i, 0)
    prev = lambda s, i: (s * nqb + jnp.maximum(i * per - 1, 0), 0)
    nxt = lambda s, i: (s * nqb + jnp.minimum(i * per + per, nqb - 1), 0)
    return pl.pallas_call(
        _attn_kernel,
        grid=(N_SEQ, SEQ // TQ),
        in_specs=[
            pl.BlockSpec(memory_space=pltpu.SMEM),
            pl.BlockSpec((TQ, ATTN_WIDTH), cur),
            pl.BlockSpec((QB, KV4), prev),
            pl.BlockSpec((TQ, KV4), cur),
            pl.BlockSpec((QB, KV4), nxt),
            pl.BlockSpec((QB, KV4), prev),
            pl.BlockSpec((TQ, KV4), cur),
            pl.BlockSpec((QB, KV4), nxt),
        ],
        out_specs=pl.BlockSpec((TQ, ATTN_WIDTH), cur),
        out_shape=jax.ShapeDtypeStruct((N_TOK, ATTN_WIDTH), BF16),
        scratch_shapes=[pltpu.VMEM((TQ + 2 * QB, KV4), BF16),
                        pltpu.VMEM((TQ + 2 * QB, KV4), BF16)],
        compiler_params=_cparams("parallel", "parallel"),
    )(sink, q, k, k, k, v, v, v)


def _outproj_kernel(xp_ref, xs_ref, yf_ref, ya_ref, wo_ref, g_ref, wr_ref, x1_ref, xn_ref, aff_ref):
    mix = jnp.dot(yf_ref[...], wo_ref[:FOURIER_WIDTH, :], preferred_element_type=F32)
    mix = mix + jnp.dot(ya_ref[...], wo_ref[FOURIER_WIDTH:, :], preferred_element_type=F32)
    x1 = _x_tile(xp_ref, xs_ref) + mix
    x1_ref[...] = x1
    ms = jnp.mean(x1 * x1, axis=-1, keepdims=True)
    xn = (x1 * lax.rsqrt(ms + EPS) * g_ref[...]).astype(BF16)
    xn_ref[...] = xn
    logits = lax.dot_general(wr_ref[...], xn, (((1,), (1,)), ((), ())), preferred_element_type=F32)
    mx = jnp.max(logits, axis=0, keepdims=True)
    ex = jnp.exp(logits - mx)
    aff_ref[...] = ex / jnp.sum(ex, axis=0, keepdims=True)


def _outproj(xp, xs, yf, ya, w_out, g, w_router_t):
    return pl.pallas_call(
        _outproj_kernel,
        grid=(N_TOK // TM,),
        in_specs=_x_specs() + [
            pl.BlockSpec((TM, FOURIER_WIDTH), lambda i: (i, 0)),
            pl.BlockSpec((TM, ATTN_WIDTH), lambda i: (i, 0)),
            pl.BlockSpec((D_MODEL, D_MODEL), lambda i: (0, 0)),
            pl.BlockSpec((1, D_MODEL), lambda i: (0, 0)),
            pl.BlockSpec((N_EXPERTS, D_MODEL), lambda i: (0, 0)),
        ],
        out_specs=[
            pl.BlockSpec((TM, D_MODEL), lambda i: (i, 0)),
            pl.BlockSpec((TM, D_MODEL), lambda i: (i, 0)),
            pl.BlockSpec((N_EXPERTS, TM), lambda i: (0, i)),
        ],
        out_shape=[
            jax.ShapeDtypeStruct((N_TOK, D_MODEL), F32),
            jax.ShapeDtypeStruct((N_TOK, D_MODEL), BF16),
            jax.ShapeDtypeStruct((N_EXPERTS, N_TOK), F32),
        ],
        compiler_params=_cparams("parallel"),
    )(xp, xs, yf, ya, w_out, g, w_router_t)


RB = 512
N_RB = N_TOK // RB
GROUP_TOKENS = (SEQ, 2 * SEQ)
GROUP0_BLOCKS = GROUP_TOKENS[0] // RB
SEL_BLOCKS = 4
assert GROUP0_BLOCKS % SEL_BLOCKS == 0 and N_RB % SEL_BLOCKS == 0
CHUNK = 8
TILE_ROWS = 256
CHUNKS_PER_TILE = TILE_ROWS // CHUNK
SLOT_ROWS = EC_CAPACITY_FACTOR * N_TOK // N_EXPERTS
assert all((EC_CAPACITY_FACTOR * n // N_EXPERTS) % CHUNK == 0 for n in GROUP_TOKENS)
SLOT_TILES = SLOT_ROWS // FF_TM
assert SLOT_TILES * FF_TM == SLOT_ROWS
SLOT_CHUNKS = SLOT_ROWS // CHUNK
TILE_SLOTS = 6
LOOKAHEAD = TILE_SLOTS - 2
TRASH_CHUNK = N_EXPERTS * SLOT_CHUNKS
HEAD_TRASH_CHUNK = TRASH_CHUNK + TILE_SLOTS * CHUNKS_PER_TILE
TRASH_CHUNKS = TILE_SLOTS * CHUNKS_PER_TILE + FF_TM // CHUNK
assert N_EXPERTS <= FF_TM // CHUNK and TRASH_CHUNKS % (FF_TM // CHUNK) == 0
MAX_CHUNKS = (N_EXPERTS * (RB + 2 * CHUNK)) // CHUNK
LIST_UNROLL = 4
LIST_LEN = (MAX_CHUNKS // CHUNKS_PER_TILE + 2) * CHUNKS_PER_TILE
CHUNK_SHAPE = (2 * CHUNK, D_MODEL // 2)


def _pack_chunks(rows_f32):
    n = rows_f32.shape[0] // CHUNK
    r3 = rows_f32.reshape(n, CHUNK, D_MODEL)
    half = D_MODEL // 2
    return jnp.concatenate([r3[:, :, :half], r3[:, :, half:]], axis=1).astype(BF16)


def _unpack_chunks(chunks_bf16):
    n = chunks_bf16.shape[0]
    c3 = chunks_bf16.astype(F32)
    left = c3[:, :CHUNK, :].reshape(n * CHUNK, D_MODEL // 2)
    right = c3[:, CHUNK:, :].reshape(n * CHUNK, D_MODEL // 2)
    return jnp.concatenate([left, right], axis=1).astype(BF16)


def _thresh_kernel(aff_ref, thr_ref, need_ref):
    first = 0
    for g, n in enumerate(GROUP_TOKENS):
        cap = float(EC_CAPACITY_FACTOR * n // N_EXPERTS)
        aff = aff_ref[:, first:first + n]
        first += n

        def body(i, cur, aff=aff, cap=cap):
            cand = cur | jnp.left_shift(jnp.int32(1), 30 - i)
            cnt = jnp.sum(jnp.where(aff >= pltpu.bitcast(cand, F32), 1.0, 0.0), axis=1, keepdims=True)
            return jnp.where(cnt >= cap, cand, cur)

        thr = pltpu.bitcast(lax.fori_loop(0, 31, body, jnp.zeros((N_EXPERTS, 1), jnp.int32)), F32)
        n_gt = jnp.sum(jnp.where(aff > thr, 1.0, 0.0), axis=1, keepdims=True)
        rows = slice(g * N_EXPERTS, (g + 1) * N_EXPERTS)
        thr_ref[rows, :] = jnp.broadcast_to(thr, (N_EXPERTS, LANES))
        need_ref[rows, :] = jnp.broadcast_to(cap - n_gt, (N_EXPERTS, LANES))


def _thresholds(aff_t):
    n_groups = len(GROUP_TOKENS)
    return pl.pallas_call(
        _thresh_kernel,
        out_shape=[jax.ShapeDtypeStruct((n_groups * N_EXPERTS, LANES), F32),
                   jax.ShapeDtypeStruct((n_groups * N_EXPERTS, LANES), F32)],
        compiler_params=pltpu.CompilerParams(vmem_limit_bytes=VMEM_LIMIT),
    )(aff_t)


def _select_kernel(aff_ref, thr_ref, need_ref, utri_ref, selw_ref, rank_ref, m_ref, eq_seen):
    step = pl.program_id(0)

    @pl.when((step == 0) | (step == GROUP0_BLOCKS // SEL_BLOCKS))
    def _():
        eq_seen[...] = jnp.zeros_like(eq_seen)

    thr = thr_ref[:, 0:1]
    need = need_ref[:, 0:1]
    utri = utri_ref[...]
    for b in range(SEL_BLOCKS):
        cols = slice(b * RB, (b + 1) * RB)
        aff = aff_ref[:, cols]
        eq = jnp.where(aff == thr, 1.0, 0.0)
        eq_before = jnp.dot(eq.astype(BF16), utri, preferred_element_type=F32) + eq_seen[...]
        sel = (aff > thr) | ((aff == thr) & (eq_before < need))
        eq_seen[...] += jnp.sum(eq, axis=1, keepdims=True)
        self = jnp.where(sel, 1.0, 0.0)
        rank_ref[:, cols] = jnp.dot(self.astype(BF16), utri, preferred_element_type=F32)
        selw_ref[:, cols] = jnp.where(sel, aff, 0.0)
        m_ref[b] = jnp.broadcast_to(jnp.sum(self, axis=1, keepdims=True), (N_EXPERTS, LANES))


def _select(aff_t, thr, need):
    utri = jnp.asarray(np.triu(np.ones((RB, RB), np.float32), 1), dtype=BF16)
    grp = lambda i: (jnp.where(i >= GROUP0_BLOCKS // SEL_BLOCKS, 1, 0), 0)
    return pl.pallas_call(
        _select_kernel,
        grid=(N_RB // SEL_BLOCKS,),
        in_specs=[
            pl.BlockSpec((N_EXPERTS, SEL_BLOCKS * RB), lambda i: (0, i)),
            pl.BlockSpec((N_EXPERTS, LANES), grp),
            pl.BlockSpec((N_EXPERTS, LANES), grp),
            pl.BlockSpec((RB, RB), lambda i: (0, 0)),
        ],
        out_specs=[
            pl.BlockSpec((N_EXPERTS, SEL_BLOCKS * RB), lambda i: (0, i)),
            pl.BlockSpec((N_EXPERTS, SEL_BLOCKS * RB), lambda i: (0, i)),
            pl.BlockSpec((SEL_BLOCKS, N_EXPERTS, LANES), lambda i: (i, 0, 0)),
        ],
        out_shape=[
            jax.ShapeDtypeStruct((N_EXPERTS, N_TOK), F32),
            jax.ShapeDtypeStruct((N_EXPERTS, N_TOK), F32),
            jax.ShapeDtypeStruct((N_RB, N_EXPERTS, LANES), F32),
        ],
        scratch_shapes=[pltpu.VMEM((N_EXPERTS, 1), F32)],
        compiler_params=_cparams("arbitrary"),
    )(aff_t, thr, need, utri)


def _fill_chunk_list(list_ref, which, rb, seg_start_ref, seg_rows_ref, filler):
    cnt = jnp.int32(0)
    for e in range(N_EXPERTS):
        base = e * SLOT_CHUNKS + seg_start_ref[rb, e] // CHUNK
        n = seg_rows_ref[rb, e] // CHUNK

        def body(k, carry, base=base, cnt=cnt):
            for u in range(LIST_UNROLL):
                list_ref[which, cnt + LIST_UNROLL * k + u] = base + LIST_UNROLL * k + u
            return carry

        lax.fori_loop(0, (n + LIST_UNROLL - 1) // LIST_UNROLL, body, 0)
        cnt = cnt + n
    for j in range(CHUNKS_PER_TILE):
        list_ref[which, cnt + j] = filler(cnt + j)
    return cnt


WIN_ROWS = 128
SPAN_TILES = 8
SPAN_ROWS = SPAN_TILES * TILE_ROWS


def _mark_span(scr, rb, s0, n_rows, tgt, values, seg_off_ref, seg_rows_ref):
    win_rows = lax.broadcasted_iota(jnp.int32, (WIN_ROWS, RB), 0).astype(F32)
    n_here = jnp.minimum(n_rows - s0, SPAN_ROWS)

    def zero(i, carry):
        scr[pl.ds(pl.multiple_of(i * TILE_ROWS, TILE_ROWS), TILE_ROWS), :] = jnp.zeros((TILE_ROWS, RB), F32)
        return carry

    lax.fori_loop(0, (n_here + TILE_ROWS - 1) // TILE_ROWS, zero, 0)
    for e in range(N_EXPERTS):
        lo = seg_off_ref[rb, e]
        first = jnp.maximum(lo, s0)
        last = jnp.minimum(lo + seg_rows_ref[rb, e], s0 + SPAN_ROWS)

        def mark(w, carry, e=e, first=first):
            start = first + w * WIN_ROWS
            win = pl.ds(pl.multiple_of(start - s0, CHUNK), WIN_ROWS)
            scr[win, :] = jnp.where(tgt[e:e + 1, :] == win_rows + start.astype(F32), values[e], scr[win, :])
            return carry

        lax.fori_loop(0, jnp.maximum(last - first + WIN_ROWS - 1, 0) // WIN_ROWS, mark, 0)


def _expert_column(ref, rb):
    e_iota = lax.broadcasted_iota(jnp.int32, (N_EXPERTS, 1), 0)
    col = jnp.zeros((N_EXPERTS, 1), F32)
    for e in range(N_EXPERTS):
        col = jnp.where(e_iota == e, ref[rb, e].astype(F32), col)
    return col


def _gather_kernel(in_start_ref, in_rows_ref, in_off_ref, tot_ref, in_rank_ref, seg_start_ref, seg_len_ref,
                   x_ref, selw_ref, rank_ref, xe_hbm,
                   p_scr, obuf, hbuf, carry, zbuf, dst_list, pending, sem, hsem, zsem):
    rb = pl.program_id(0)
    n_rows = tot_ref[rb]
    n_tiles = (n_rows + TILE_ROWS - 1) // TILE_ROWS
    head_slot = TILE_SLOTS

    @pl.when(rb == 0)
    def _():
        for slot in range(TILE_SLOTS + 1):
            pending[slot] = 0
        p_scr[...] = jnp.zeros_like(p_scr)
        carry[...] = jnp.zeros_like(carry)

    _fill_chunk_list(dst_list, 0, rb, in_start_ref, in_rows_ref,
                     lambda c: TRASH_CHUNK + c % (TILE_SLOTS * CHUNKS_PER_TILE))

    def tile_copy(slot):
        return pltpu.make_async_copy(obuf.at[slot], xe_hbm.at[pl.ds(0, CHUNKS_PER_TILE)], sem.at[slot])

    def head_copy():
        return pltpu.make_async_copy(hbuf, xe_hbm.at[pl.ds(0, N_EXPERTS)], hsem)

    x = x_ref[...]
    rank = rank_ref[...]
    sel = selw_ref[...] > 0.0

    j8 = lax.broadcasted_iota(jnp.int32, (CHUNK, RB), 0).astype(F32)
    heads, tails, flags = [], [], []
    for e in range(N_EXPERTS):
        start = seg_start_ref[rb, e]
        end = start + seg_len_ref[rb, e]
        nonempty = seg_len_ref[rb, e] > 0
        has_head = nonempty & (start % CHUNK != 0)
        has_tail = nonempty & (end % CHUNK != 0)
        single = has_head & has_tail & (start // CHUNK == end // CHUNK)
        head_shift = (start % CHUNK).astype(F32)
        tail_rank0 = ((end // CHUNK) * CHUNK - start).astype(F32)
        r_e = rank[e:e + 1, :]
        s_e = sel[e:e + 1, :]
        heads.append(jnp.where(s_e & (r_e + head_shift == j8), has_head.astype(F32), 0.0))
        tails.append(jnp.where(s_e & (r_e - tail_rank0 == j8), (has_tail & ~single).astype(F32), 0.0))
        flags.append((has_head, has_tail, single, start // CHUNK))
    p_edge = jnp.concatenate(heads + tails, axis=0).astype(BF16)
    edge = jnp.dot(p_edge, x, preferred_element_type=F32)

    @pl.when(pending[head_slot] == 1)
    def _():
        head_copy().wait()

    head_rows = []
    for e, (has_head, has_tail, single, head_chunk) in enumerate(flags):
        rows = slice(e * CHUNK, (e + 1) * CHUNK)
        old = carry[rows, :]
        head = edge[rows, :] + old
        tail = edge[N_EXPERTS * CHUNK + e * CHUNK:N_EXPERTS * CHUNK + (e + 1) * CHUNK, :]
        head_rows.append(head)
        keep = ~single & ~has_tail & ~has_head
        carry[rows, :] = (single.astype(F32) * head + (has_tail & ~single).astype(F32) * tail
                          + keep.astype(F32) * old)
    hbuf[...] = _pack_chunks(jnp.concatenate(head_rows, axis=0))
    for e, (has_head, has_tail, single, head_chunk) in enumerate(flags):
        dst = jnp.where(has_head & ~single, e * SLOT_CHUNKS + head_chunk, HEAD_TRASH_CHUNK + e)
        pltpu.make_async_copy(hbuf.at[e], xe_hbm.at[dst], hsem).start()
    pending[head_slot] = 1

    off = _expert_column(in_off_ref, rb)
    rank0 = _expert_column(in_rank_ref, rb)
    n_in = _expert_column(in_rows_ref, rb)
    tgt = jnp.where(sel & (rank >= rank0) & (rank < rank0 + n_in), rank - rank0 + off, -1.0)

    def free_slot(slot):
        @pl.when(pending[slot] == 1)
        def _():
            tile_copy(slot).wait()

    def emit(t, slot, rows_f32):
        obuf[slot] = _pack_chunks(rows_f32)
        for i in range(CHUNKS_PER_TILE):
            pltpu.make_async_copy(obuf.at[slot, i], xe_hbm.at[dst_list[0, t * CHUNKS_PER_TILE + i]],
                                  sem.at[slot]).start()
        pending[slot] = 1

    def pair_body(pair, carry_):
        t = 2 * pair
        has_second = t + 1 < n_tiles
        slot_a = t % TILE_SLOTS
        slot_b = (t + 1) % TILE_SLOTS
        free_slot(slot_a)

        @pl.when(t % SPAN_TILES == 0)
        def _():
            _mark_span(p_scr, rb, t * TILE_ROWS, n_rows, tgt, [1.0] * N_EXPERTS, in_off_ref, in_rows_ref)

        first_row = pl.multiple_of((t % SPAN_TILES) * TILE_ROWS, TILE_ROWS)

        @pl.when(has_second)
        def _():
            free_slot(slot_b)
            p = p_scr[pl.ds(first_row, 2 * TILE_ROWS), :]
            rows = jnp.dot(p.astype(BF16), x, preferred_element_type=F32)
            emit(t, slot_a, rows[:TILE_ROWS])
            emit(t + 1, slot_b, rows[TILE_ROWS:])

        @pl.when(~has_second)
        def _():
            p = p_scr[pl.ds(first_row, TILE_ROWS), :]
            emit(t, slot_a, jnp.dot(p.astype(BF16), x, preferred_element_type=F32))

        return carry_

    lax.fori_loop(0, (n_tiles + 1) // 2, pair_body, 0)

    @pl.when(rb == N_RB - 1)
    def _():
        for slot in range(TILE_SLOTS):
            @pl.when(pending[slot] == 1)
            def _(slot=slot):
                tile_copy(slot).wait()
        head_copy().wait()
        zbuf[...] = jnp.zeros_like(zbuf)

        def zcopy(k):
            return pltpu.make_async_copy(zbuf, xe_hbm.at[TRASH_CHUNK + k], zsem)

        def zstart(k, c):
            zcopy(k).start()
            return c

        def zwait(k, c):
            zcopy(k).wait()
            return c

        lax.fori_loop(0, TRASH_CHUNKS, zstart, 0)
        lax.fori_loop(0, TRASH_CHUNKS, zwait, 0)


def _route_gather(tables, xn, selw_t, rank_t):
    gs = pltpu.PrefetchScalarGridSpec(
        num_scalar_prefetch=len(tables),
        grid=(N_RB,),
        in_specs=[
            pl.BlockSpec((RB, D_MODEL), lambda rb, *_: (rb, 0)),
            pl.BlockSpec((N_EXPERTS, RB), lambda rb, *_: (0, rb)),
            pl.BlockSpec((N_EXPERTS, RB), lambda rb, *_: (0, rb)),
        ],
        out_specs=pl.BlockSpec(memory_space=pl.ANY),
        scratch_shapes=[
            pltpu.VMEM((SPAN_ROWS + WIN_ROWS, RB), F32),
            pltpu.VMEM((TILE_SLOTS, CHUNKS_PER_TILE) + CHUNK_SHAPE, BF16),
            pltpu.VMEM((N_EXPERTS,) + CHUNK_SHAPE, BF16),
            pltpu.VMEM((N_EXPERTS * CHUNK, D_MODEL), F32),
            pltpu.VMEM(CHUNK_SHAPE, BF16),
            pltpu.SMEM((1, LIST_LEN), jnp.int32),
            pltpu.SMEM((TILE_SLOTS + 1,), jnp.int32),
            pltpu.SemaphoreType.DMA((TILE_SLOTS,)),
            pltpu.SemaphoreType.DMA,
            pltpu.SemaphoreType.DMA,
        ],
    )
    return pl.pallas_call(
        _gather_kernel,
        grid_spec=gs,
        out_shape=jax.ShapeDtypeStruct((TRASH_CHUNK + TRASH_CHUNKS,) + CHUNK_SHAPE, BF16),
        compiler_params=_cparams("arbitrary"),
    )(*tables, xn, selw_t, rank_t)


W_PIECES_PER_MATRIX = 4
W_PIECES = 3 * W_PIECES_PER_MATRIX
FFN_STEPS = max(SLOT_TILES, W_PIECES + 1)
WG_PIECE_ROWS = D_MODEL // W_PIECES_PER_MATRIX
WD_PIECE_ROWS = EXPERT_D_FF // W_PIECES_PER_MATRIX
FFN_VMEM_LIMIT = 56 * 1024 * 1024


def _ffn_kernel(x_ref, wg_hbm, wu_hbm, wd_hbm, o_ref,
                wg_bf, wu_bf, wd_bf, stage_up, stage_dn, sem):
    e = pl.program_id(0)
    i = pl.program_id(1)
    slot = e % 2

    matrices = ((wg_hbm, stage_up, wg_bf, WG_PIECE_ROWS),
                (wu_hbm, stage_up, wu_bf, WG_PIECE_ROWS),
                (wd_hbm, stage_dn, wd_bf, WD_PIECE_ROWS))

    def piece(expert, k, buf, finish):
        def of_kind(kind):
            w_hbm, stage, w_bf, rows = matrices[kind]
            r0 = pl.multiple_of((k - kind * W_PIECES_PER_MATRIX) * rows, rows)
            cp = pltpu.make_async_copy(w_hbm.at[expert, pl.ds(r0, rows), :], stage.at[buf], sem.at[buf])
            if finish:
                cp.wait()
                w_bf[expert % 2, pl.ds(r0, rows), :] = stage[buf].astype(BF16)
            else:
                cp.start()

        if isinstance(k, int):
            of_kind(k // W_PIECES_PER_MATRIX)
        else:
            for kind in range(len(matrices)):
                pl.when(k // W_PIECES_PER_MATRIX == kind)(functools.partial(of_kind, kind))

    @pl.when((e == 0) & (i == 0))
    def _():
        for k in range(W_PIECES):
            piece(0, k, k % 2, finish=False)
            piece(0, k, k % 2, finish=True)

    @pl.when(e + 1 < N_EXPERTS)
    def _():
        @pl.when((i >= 1) & (i <= W_PIECES))
        def _():
            piece(e + 1, i - 1, (i - 1) % 2, finish=True)

        @pl.when(i < W_PIECES)
        def _():
            piece(e + 1, i, i % 2, finish=False)

    @pl.when(i < SLOT_TILES)
    def _():
        x = _unpack_chunks(x_ref[...])
        acc = None
        for c in range(EXPERT_D_FF // FF_FC):
            gt = jnp.dot(x, wg_bf[slot, :, c * FF_FC:(c + 1) * FF_FC], preferred_element_type=F32)
            up = jnp.dot(x, wu_bf[slot, :, c * FF_FC:(c + 1) * FF_FC], preferred_element_type=F32)
            h = (gt * (1.0 / (1.0 + jnp.exp(-gt))) * up).astype(BF16)
            down = jnp.dot(h, wd_bf[slot, c * FF_FC:(c + 1) * FF_FC, :], preferred_element_type=F32)
            acc = down if acc is None else acc + down
        o_ref[...] = _pack_chunks(acc)


def _ffn(xe, w_gate, w_up, w_down):
    def rows(e, i):
        return (e * SLOT_TILES + jnp.minimum(i, SLOT_TILES - 1), 0, 0)

    tile_chunks = (FF_TM // CHUNK,) + CHUNK_SHAPE
    gs = pltpu.PrefetchScalarGridSpec(
        num_scalar_prefetch=0,
        grid=(N_EXPERTS, FFN_STEPS),
        in_specs=[
            pl.BlockSpec(tile_chunks, rows),
            pl.BlockSpec(memory_space=pl.ANY),
            pl.BlockSpec(memory_space=pl.ANY),
            pl.BlockSpec(memory_space=pl.ANY),
        ],
        out_specs=pl.BlockSpec(tile_chunks, rows),
        scratch_shapes=[
            pltpu.VMEM((2, D_MODEL, EXPERT_D_FF), BF16),
            pltpu.VMEM((2, D_MODEL, EXPERT_D_FF), BF16),
            pltpu.VMEM((2, EXPERT_D_FF, D_MODEL), BF16),
            pltpu.VMEM((2, WG_PIECE_ROWS, EXPERT_D_FF), F32),
            pltpu.VMEM((2, WD_PIECE_ROWS, D_MODEL), F32),
            pltpu.SemaphoreType.DMA((2,)),
        ],
    )
    return pl.pallas_call(
        _ffn_kernel,
        grid_spec=gs,
        out_shape=jax.ShapeDtypeStruct((N_EXPERTS * SLOT_CHUNKS,) + CHUNK_SHAPE, BF16),
        compiler_params=pltpu.CompilerParams(dimension_semantics=("arbitrary", "arbitrary"),
                                             vmem_limit_bytes=FFN_VMEM_LIMIT),
    )(xe, w_gate, w_up, w_down)


def _combine_kernel(seg_start_ref, seg_rows_ref, seg_off_ref, tot_ref, seg_shift_ref,
                    x1_ref, selw_ref, rank_ref, g_ref, ye_hbm, op_ref, os_ref,
                    w_scr, ybuf, acc, src_list, state, sem):
    rb = pl.program_id(0)
    cur = rb % 2
    nxt_rb = jnp.minimum(rb + 1, N_RB - 1)

    def tiles_of(b):
        return (tot_ref[b] + TILE_ROWS - 1) // TILE_ROWS

    n_tiles = tiles_of(rb)
    n_tiles_next = jnp.where(rb + 1 < N_RB, tiles_of(nxt_rb), 0)

    def fetch(which, t, slot):
        for i in range(CHUNKS_PER_TILE):
            pltpu.make_async_copy(ye_hbm.at[src_list[which, t * CHUNKS_PER_TILE + i]], ybuf.at[slot, i],
                                  sem.at[slot]).start()

    def tile_wait(slot):
        pltpu.make_async_copy(ye_hbm.at[pl.ds(0, CHUNKS_PER_TILE)], ybuf.at[slot], sem.at[slot]).wait()

    @pl.when(rb == 0)
    def _():
        w_scr[...] = jnp.zeros_like(w_scr)
        _fill_chunk_list(src_list, 0, 0, seg_start_ref, seg_rows_ref, lambda c: 0)
        state[0] = 0
        state[1] = 0

    @pl.when(rb + 1 < N_RB)
    def _():
        _fill_chunk_list(src_list, 1 - cur, nxt_rb, seg_start_ref, seg_rows_ref, lambda c: 0)

    slot0 = state[0]
    started = state[1]
    for j in range(LOOKAHEAD):
        @pl.when((j >= started) & (j < n_tiles))
        def _(j=j):
            fetch(cur, j, (slot0 + j) % TILE_SLOTS)

    off = _expert_column(seg_off_ref, rb) + _expert_column(seg_shift_ref, rb)
    selw = selw_ref[...]
    gates = [selw[e:e + 1, :] for e in range(N_EXPERTS)]
    tgt = jnp.where(selw > 0.0, rank_ref[...] + off, -1.0)
    acc[...] = x1_ref[...]

    def prefetch(t, next_started):
        ahead = t + LOOKAHEAD

        @pl.when(ahead < n_tiles)
        def _():
            fetch(cur, ahead, (slot0 + ahead) % TILE_SLOTS)

        start_next = (ahead >= n_tiles) & (next_started < jnp.minimum(LOOKAHEAD, n_tiles_next))

        @pl.when(start_next)
        def _():
            fetch(1 - cur, next_started, (slot0 + n_tiles + next_started) % TILE_SLOTS)

        return next_started + start_next.astype(jnp.int32)

    def add_tiles(t, n):
        w = w_scr[pl.ds(pl.multiple_of((t % SPAN_TILES) * TILE_ROWS, TILE_ROWS), n * TILE_ROWS), :]
        ys = [_unpack_chunks(ybuf[(slot0 + t + k) % TILE_SLOTS]) for k in range(n)]
        y = ys[0] if n == 1 else jnp.concatenate(ys, axis=0)
        acc[...] += lax.dot_general(w.astype(BF16), y, (((0,), (0,)), ((), ())),
                                    preferred_element_type=F32)

    def pair_body(p, next_started):
        t = 2 * p
        has_second = t + 1 < n_tiles
        next_started = prefetch(t + 1, prefetch(t, next_started))

        @pl.when(t % SPAN_TILES == 0)
        def _():
            _mark_span(w_scr, rb, t * TILE_ROWS, tot_ref[rb], tgt, gates, seg_off_ref, seg_rows_ref)

        tile_wait((slot0 + t) % TILE_SLOTS)

        @pl.when(has_second)
        def _():
            tile_wait((slot0 + t + 1) % TILE_SLOTS)
            add_tiles(t, 2)

        @pl.when(~has_second)
        def _():
            add_tiles(t, 1)

        return next_started

    state[1] = lax.fori_loop(0, (n_tiles + 1) // 2, pair_body, jnp.int32(0))
    state[0] = (slot0 + n_tiles) % TILE_SLOTS
    x2 = acc[...]
    ms = jnp.mean(x2 * x2, axis=-1, keepdims=True)
    y = x2 * lax.rsqrt(ms + EPS) * g_ref[...]

    @pl.when(rb < GROUP0_BLOCKS)
    def _():
        op_ref[...] = y

    @pl.when(rb >= GROUP0_BLOCKS)
    def _():
        os_ref[...] = y


def _combine(tables, x1, selw_t, rank_t, g, ye):
    gs = pltpu.PrefetchScalarGridSpec(
        num_scalar_prefetch=len(tables),
        grid=(N_RB,),
        in_specs=[
            pl.BlockSpec((RB, D_MODEL), lambda rb, *_: (rb, 0)),
            pl.BlockSpec((N_EXPERTS, RB), lambda rb, *_: (0, rb)),
            pl.BlockSpec((N_EXPERTS, RB), lambda rb, *_: (0, rb)),
            pl.BlockSpec((1, D_MODEL), lambda rb, *_: (0, 0)),
            pl.BlockSpec(memory_space=pl.ANY),
        ],
        out_specs=[
            pl.BlockSpec((RB, D_MODEL), lambda rb, *_: (jnp.minimum(rb, GROUP0_BLOCKS - 1), 0)),
            pl.BlockSpec((RB, D_MODEL), lambda rb, *_: (jnp.maximum(rb - GROUP0_BLOCKS, 0), 0)),
        ],
        scratch_shapes=[
            pltpu.VMEM((SPAN_ROWS + WIN_ROWS, RB), F32),
            pltpu.VMEM((TILE_SLOTS, CHUNKS_PER_TILE) + CHUNK_SHAPE, BF16),
            pltpu.VMEM((RB, D_MODEL), F32),
            pltpu.SMEM((2, LIST_LEN), jnp.int32),
            pltpu.SMEM((2,), jnp.int32),
            pltpu.SemaphoreType.DMA((TILE_SLOTS,)),
        ],
    )
    return pl.pallas_call(
        _combine_kernel,
        grid_spec=gs,
        out_shape=[jax.ShapeDtypeStruct((GROUP_TOKENS[0], D_MODEL), F32),
                   jax.ShapeDtypeStruct((GROUP_TOKENS[1], D_MODEL), F32)],
        compiler_params=_cparams("arbitrary"),
    )(*tables, x1, selw_t, rank_t, g, ye)


def kernel(x_prompt, x_sample, norm_mix_g, w_in, w_fourier, sink, w_out, norm_ffn_g, w_router,
           w_gate, w_up, w_down, norm_final_g):
    assert x_prompt.shape == (1, SEQ, D_MODEL) and x_sample.shape == (N_SEQ - 1, SEQ, D_MODEL)
    assert w_in.shape == (1, D_MODEL, IN_WIDTH) and w_out.shape == (1, MIX_WIDTH, D_MODEL)
    assert w_gate.shape == (1, N_EXPERTS, D_MODEL, EXPERT_D_FF) and w_down.shape == (1, N_EXPERTS, EXPERT_D_FF, D_MODEL)
    xp = x_prompt.reshape(-1, D_MODEL)
    xs = x_sample.reshape(-1, D_MODEL)
    cos_t, sin_t = _rope_tables()

    mix_cols = np.concatenate([
        np.arange(FOURIER_WIDTH),
        FOURIER_WIDTH + (np.asarray(Q_ORDER)[:, None] * HEAD_DIM + np.arange(HEAD_DIM)[None, :]).reshape(-1)])
    in_cols = np.concatenate([mix_cols, np.arange(MIX_WIDTH, IN_WIDTH)])
    w_in_r = w_in[0][:, in_cols].astype(BF16)
    w_out_r = w_out[0][mix_cols, :].astype(BF16)

    u, q, k, v = _inproj(xp, xs, norm_mix_g[0][None, :], w_in_r, cos_t, sin_t)
    yf = _fourier(u, w_fourier[0].astype(BF16))
    ya = _attention(sink[0], q, k, v)
    x1, xn, aff_t = _outproj(xp, xs, yf, ya, w_out_r, norm_ffn_g[0][None, :],
                             w_router[0].T.astype(BF16))

    thr, need = _thresholds(aff_t)
    selw_t, rank_t, m = _select(aff_t, thr, need)

    seg_len = m[:, :, 0].astype(jnp.int32)
    seg_start = jnp.cumsum(seg_len, axis=0) - seg_len
    seg_end = seg_start + seg_len

    def offsets(rows):
        return jnp.cumsum(rows, axis=1) - rows, jnp.sum(rows, axis=1)

    in_start = -(-seg_start // CHUNK) * CHUNK
    in_rows = jnp.maximum(seg_end // CHUNK * CHUNK - in_start, 0)
    in_off, in_tot = offsets(in_rows)
    gather_tables = (in_start, in_rows, in_off, in_tot, in_start - seg_start, seg_start, seg_len)
    out_start = seg_start // CHUNK * CHUNK
    out_rows = jnp.where(seg_len > 0, -(-seg_end // CHUNK) * CHUNK - out_start, 0)
    out_off, out_tot = offsets(out_rows)
    combine_tables = (out_start, out_rows, out_off, out_tot, seg_start - out_start)

    xe = _route_gather(gather_tables, xn, selw_t, rank_t)
    ye = _ffn(xe, w_gate[0], w_up[0], w_down[0])
    yp, ys = _combine(combine_tables, x1, selw_t, rank_t, norm_final_g[None, :], ye)
    return (yp.reshape(x_prompt.shape), ys.reshape(x_sample.shape))
```

```python
import functools

import numpy as np
import jax
import jax.numpy as jnp
from jax import lax
from jax.experimental import pallas as pl
from jax.experimental.pallas import tpu as pltpu

F32 = jnp.float32
BF16 = jnp.bfloat16

D_MODEL = 1024
SEQ = 16384
N_SEQ = 3
N_TOK = N_SEQ * SEQ
HEAD_DIM = 64
N_HEADS = 12
N_KV_HEADS = 4
GQA_GROUP = N_HEADS // N_KV_HEADS
ATTN_WIDTH = N_HEADS * HEAD_DIM
KV_WIDTH = N_KV_HEADS * HEAD_DIM
FOURIER_WIDTH = 256
FOURIER_GROUP_DIM = 64
MIX_WIDTH = FOURIER_WIDTH + ATTN_WIDTH
IN_WIDTH = MIX_WIDTH + 2 * KV_WIDTH
WINDOW = 128
ROPE_THETA = 10000.0
N_EXPERTS = 16
EC_CAPACITY_FACTOR = 2
EXPERT_D_FF = 2048
EPS = 1e-6
NEG = -1e30

LANES = 128
FFT_R = 128
VMEM_LIMIT = 48 * 1024 * 1024

TM = 1024
TQ = 1024
QB = 128
FF_TM = 512
FF_FC = 512


def _cparams(*sem):
    return pltpu.CompilerParams(dimension_semantics=sem, vmem_limit_bytes=VMEM_LIMIT)


def _x_specs():
    n_p = SEQ // TM
    return [pl.BlockSpec((TM, D_MODEL), lambda i, *_: (jnp.minimum(i, n_p - 1), 0)),
            pl.BlockSpec((TM, D_MODEL), lambda i, *_: (jnp.maximum(i - n_p, 0), 0))]


def _x_tile(xp_ref, xs_ref):
    return jnp.where(pl.program_id(0) < SEQ // TM, xp_ref[...], xs_ref[...])


def _inproj_kernel(xp_ref, xs_ref, g_ref, w_ref, cos_ref, sin_ref, u_ref, q_ref, k_ref, v_ref):
    x = _x_tile(xp_ref, xs_ref)
    ms = jnp.mean(x * x, axis=-1, keepdims=True)
    hn = (x * lax.rsqrt(ms + EPS) * g_ref[...]).astype(BF16)
    proj = jnp.dot(hn, w_ref[...], preferred_element_type=F32)
    u_ref[...] = proj[:, :FOURIER_WIDTH].astype(BF16)
    cos = cos_ref[...]
    sin = sin_ref[...]
    lane = lax.broadcasted_iota(jnp.int32, cos.shape, 1)
    first_half = (lane % HEAD_DIM) < (HEAD_DIM // 2)

    def rope(t):
        partner = jnp.where(first_half,
                            pltpu.roll(t, LANES - HEAD_DIM // 2, 1),
                            pltpu.roll(t, HEAD_DIM // 2, 1))
        return t * cos + partner * sin

    scale = HEAD_DIM ** -0.5 * LOG2E
    for c in range(ATTN_WIDTH // LANES):
        lo = FOURIER_WIDTH + c * LANES
        q_ref[:, c * LANES:(c + 1) * LANES] = (rope(proj[:, lo:lo + LANES]) * scale).astype(BF16)
    low_half = lane < HEAD_DIM
    for c in range(KV_WIDTH // LANES):
        lo = FOURIER_WIDTH + ATTN_WIDTH + c * LANES
        kc = rope(proj[:, lo:lo + LANES])
        vc = proj[:, lo + KV_WIDTH:lo + KV_WIDTH + LANES]
        for half, keep in enumerate((low_half, ~low_half)):
            g = 2 * c + half
            k_ref[:, g * LANES:(g + 1) * LANES] = jnp.where(keep, kc, 0.0).astype(BF16)
            v_ref[:, g * LANES:(g + 1) * LANES] = jnp.where(keep, vc, 0.0).astype(BF16)


def _inproj(xp, xs, g, w_in, cos_t, sin_t):
    n_pos_blocks = SEQ // TM
    return pl.pallas_call(
        _inproj_kernel,
        grid=(N_TOK // TM,),
        in_specs=_x_specs() + [
            pl.BlockSpec((1, D_MODEL), lambda i: (0, 0)),
            pl.BlockSpec((D_MODEL, IN_WIDTH), lambda i: (0, 0)),
            pl.BlockSpec((TM, LANES), lambda i: (i % n_pos_blocks, 0)),
            pl.BlockSpec((TM, LANES), lambda i: (i % n_pos_blocks, 0)),
        ],
        out_specs=[
            pl.BlockSpec((TM, FOURIER_WIDTH), lambda i: (i, 0)),
            pl.BlockSpec((TM, ATTN_WIDTH), lambda i: (i, 0)),
            pl.BlockSpec((TM, KV4), lambda i: (i, 0)),
            pl.BlockSpec((TM, KV4), lambda i: (i, 0)),
        ],
        out_shape=[
            jax.ShapeDtypeStruct((N_TOK, FOURIER_WIDTH), BF16),
            jax.ShapeDtypeStruct((N_TOK, ATTN_WIDTH), BF16),
            jax.ShapeDtypeStruct((N_TOK, KV4), BF16),
            jax.ShapeDtypeStruct((N_TOK, KV4), BF16),
        ],
        compiler_params=_cparams("parallel"),
    )(xp, xs, g, w_in, cos_t, sin_t)


def _rope_tables():
    half = HEAD_DIM // 2
    inv_freq = 1.0 / (ROPE_THETA ** (jnp.arange(half, dtype=F32) / half))
    ang = jnp.arange(SEQ, dtype=F32)[:, None] * inv_freq[None, :]
    cos = jnp.cos(ang)
    sin = jnp.sin(ang)
    reps = LANES // HEAD_DIM
    cos_t = jnp.tile(jnp.concatenate([cos, cos], axis=-1), (1, reps))
    sin_t = jnp.tile(jnp.concatenate([-sin, sin], axis=-1), (1, reps))
    return cos_t, sin_t


def _dft_tables():
    r = FFT_R
    c = np.arange(FOURIER_GROUP_DIM)
    ang_c = 2.0 * np.pi * np.outer(c, c) / FOURIER_GROUP_DIM
    n_groups = FOURIER_WIDTH // FOURIER_GROUP_DIM
    eye = np.eye(n_groups)
    ch_scale = FOURIER_GROUP_DIM ** -0.5
    c_blk = np.kron(eye, np.cos(ang_c)) * ch_scale
    s_blk = np.kron(eye, np.sin(ang_c)) * ch_scale
    cs = np.concatenate([c_blk, -s_blk], axis=1)
    k = np.arange(r)
    ang_r = 2.0 * np.pi * np.outer(k, k) / r
    st_scale = r ** -0.5
    wr = np.cos(ang_r) * st_scale
    wi = -np.sin(ang_r) * st_scale
    wbig = np.block([[wr, -wi], [wi, wr]])
    ang_t = 2.0 * np.pi * np.outer(k, k) / (r * r)
    tr = np.cos(ang_t)
    ti = -np.sin(ang_t)
    f = lambda a: jnp.asarray(a, dtype=F32)
    return f(cs).astype(BF16), f(wbig).astype(BF16), f(wr), f(wi), f(tr), f(ti)


FS_SB = 16
FS_KB = 16


def _fft_stage1_kernel(u_ref, cs_ref, wbig_ref, y_ref):
    cs = cs_ref[...]
    wbig = wbig_ref[...]
    fw = FOURIER_WIDTH
    for j in range(FS_SB):
        u = u_ref[:, j * fw:(j + 1) * fw]
        g = jnp.dot(u, cs, preferred_element_type=F32)
        gst = jnp.concatenate([g[:, :fw], g[:, fw:]], axis=0).astype(BF16)
        y = jnp.dot(wbig, gst, preferred_element_type=F32)
        y_ref[:, j * 2 * fw:j * 2 * fw + fw] = y[:FFT_R].astype(BF16)
        y_ref[:, j * 2 * fw + fw:(j + 1) * 2 * fw] = y[FFT_R:].astype(BF16)


def _fft_stage2_kernel(y_ref, wr_ref, wi_ref, tr_ref, ti_ref, wf_ref, o_ref):
    kb = pl.program_id(1)
    wr = wr_ref[...]
    wi = wi_ref[...]
    wf = wf_ref[...]
    fw = FOURIER_WIDTH
    for j in range(FS_KB):
        k1 = kb * FS_KB + j
        tr = tr_ref[pl.ds(k1, 1), :]
        ti = ti_ref[pl.ds(k1, 1), :]
        mr = wr * tr - wi * ti
        mi = wr * ti + wi * tr
        m2 = jnp.concatenate([mr, -mi], axis=1).astype(BF16)
        yb = y_ref[j * FFT_R:(j + 1) * FFT_R, :]
        yst = jnp.concatenate([yb[:, :fw], yb[:, fw:]], axis=0)
        z = jnp.dot(m2, yst, preferred_element_type=F32)
        yf = jnp.dot(z.astype(BF16), wf, preferred_element_type=F32)
        o_ref[:, j * fw:(j + 1) * fw] = yf.astype(BF16)


def _fourier(u, w_fourier):
    cs, wbig, wr, wi, tr, ti = _dft_tables()
    r, fw = FFT_R, FOURIER_WIDTH
    u2 = u.reshape(N_SEQ * r, r * fw)
    y = pl.pallas_call(
        _fft_stage1_kernel,
        grid=(N_SEQ, r // FS_SB),
        in_specs=[
            pl.BlockSpec((r, FS_SB * fw), lambda s, j: (s, j)),
            pl.BlockSpec((fw, 2 * fw), lambda s, j: (0, 0)),
            pl.BlockSpec((2 * r, 2 * r), lambda s, j: (0, 0)),
        ],
        out_specs=pl.BlockSpec((r, FS_SB * 2 * fw), lambda s, j: (s, j)),
        out_shape=jax.ShapeDtypeStruct((N_SEQ * r, r * 2 * fw), BF16),
        compiler_params=_cparams("parallel", "parallel"),
    )(u2, cs, wbig)
    y2 = y.reshape(N_SEQ * r * r, 2 * fw)
    full = lambda s, j: (0, 0)
    yf = pl.pallas_call(
        _fft_stage2_kernel,
        grid=(N_SEQ, r // FS_KB),
        in_specs=[
            pl.BlockSpec((FS_KB * r, 2 * fw), lambda s, j: (s * (r // FS_KB) + j, 0)),
            pl.BlockSpec((r, r), full),
            pl.BlockSpec((r, r), full),
            pl.BlockSpec((r, r), full),
            pl.BlockSpec((r, r), full),
            pl.BlockSpec((fw, fw), full),
        ],
        out_specs=pl.BlockSpec((r, FS_KB * fw), lambda s, j: (s, j)),
        out_shape=jax.ShapeDtypeStruct((N_SEQ * r, r * fw), BF16),
        compiler_params=_cparams("parallel", "parallel"),
    )(y2, wr, wi, tr, ti, w_fourier)
    return yf.reshape(N_TOK, fw)


LO_HEADS = (0, 1, 2, 6, 7, 8)
HI_HEADS = (3, 4, 5, 9, 10, 11)
Q_ORDER = tuple(h for pair in zip(LO_HEADS, HI_HEADS) for h in pair)
assert all(h // GQA_GROUP % 2 == 0 for h in LO_HEADS) and all(h // GQA_GROUP % 2 == 1 for h in HI_HEADS)
KV4 = 2 * KV_WIDTH
LOG2E = 1.4426950408889634


def _attn_kernel(sink_ref, q_ref, kp_ref, kc_ref, kn_ref, vp_ref, vc_ref, vn_ref, o_ref,
                 kext, vext):
    i = pl.program_id(1)
    n_i = pl.num_programs(1)
    kext[0:QB, :] = kp_ref[...]
    kext[QB:QB + TQ, :] = kc_ref[...]
    kext[QB + TQ:, :] = kn_ref[...]
    vext[0:QB, :] = vp_ref[...]
    vext[QB:QB + TQ, :] = vc_ref[...]
    vext[QB + TQ:, :] = vn_ref[...]

    rows = GQA_GROUP * QB
    a = lax.broadcasted_iota(jnp.int32, (rows, QB), 0) % QB
    j = lax.broadcasted_iota(jnp.int32, (rows, QB), 1)
    band_prev = jnp.where(j >= a, 0.0, NEG)
    band_next = jnp.where(j <= a, 0.0, NEG)
    row_id = lax.broadcasted_iota(jnp.int32, (rows, 1), 0)
    low_half = lax.broadcasted_iota(jnp.int32, (rows, LANES), 1) < HEAD_DIM

    def softmax_half(s, sinks, bias_prev, bias_next):
        s_p = s[:, :QB] + bias_prev
        s_c = s[:, QB:2 * QB]
        s_n = s[:, 2 * QB:] + bias_next
        sink = jnp.where(row_id < QB, sinks[0], jnp.where(row_id < 2 * QB, sinks[1], sinks[2])) * LOG2E
        m = jnp.max(jnp.maximum(jnp.maximum(s_p, s_c), s_n), axis=-1, keepdims=True)
        m = jnp.maximum(m, sink)
        p_p = jnp.exp2(s_p - m)
        p_c = jnp.exp2(s_c - m)
        p_n = jnp.exp2(s_n - m)
        denom = jnp.sum(p_p + p_c + p_n, axis=-1, keepdims=True) + jnp.exp2(sink - m)
        return [p_p.astype(BF16), p_c.astype(BF16), p_n.astype(BF16)], 1.0 / denom

    for sb in range(TQ // QB):
        blk = i * (TQ // QB) + sb
        bias_prev = band_prev + jnp.where(blk == 0, NEG, 0.0)
        bias_next = band_next + jnp.where(blk == n_i * (TQ // QB) - 1, NEG, 0.0)
        for pr in range(N_KV_HEADS // 2):
            groups = [GQA_GROUP * pr + m for m in range(GQA_GROUP)]
            q3 = jnp.concatenate([q_ref[sb * QB:(sb + 1) * QB, g * LANES:(g + 1) * LANES] for g in groups],
                                 axis=0)
            kp = kext[sb * QB:sb * QB + 3 * QB, pr * 2 * LANES:(pr + 1) * 2 * LANES]
            vp = vext[sb * QB:sb * QB + 3 * QB, pr * 2 * LANES:(pr + 1) * 2 * LANES]
            kcat = jnp.concatenate([kp[:, :LANES], kp[:, LANES:]], axis=0)
            vcat = jnp.concatenate([vp[:, :LANES], vp[:, LANES:]], axis=0)
            s = lax.dot_general(q3, kcat, (((1,), (1,)), ((), ())), preferred_element_type=F32)
            p_lo, r_lo = softmax_half(s[:, :3 * QB], [sink_ref[LO_HEADS[g]] for g in groups],
                                      bias_prev, bias_next)
            p_hi, r_hi = softmax_half(s[:, 3 * QB:], [sink_ref[HI_HEADS[g]] for g in groups],
                                      bias_prev, bias_next)
            p = jnp.concatenate(p_lo + p_hi, axis=1)
            o = jnp.dot(p, vcat, preferred_element_type=F32) * jnp.where(low_half, r_lo, r_hi)
            for m, g in enumerate(groups):
                o_ref[sb * QB:(sb + 1) * QB, g * LANES:(g + 1) * LANES] = o[m * QB:(m + 1) * QB].astype(BF16)


def _attention(sink, q, k, v):
    nqb = SEQ // QB
    per = TQ // QB
    cur = lambda s, i: (s * (SEQ // TQ) + i, 0)
    prev = lambda s, i: (s * nqb + jnp.maximum(i * per - 1, 0), 0)
    nxt = lambda s, i: (s * nqb + jnp.minimum(i * per + per, nqb - 1), 0)
    return pl.pallas_call(
        _attn_kernel,
        grid=(N_SEQ, SEQ // TQ),
        in_specs=[
            pl.BlockSpec(memory_space=pltpu.SMEM),
            pl.BlockSpec((TQ, ATTN_WIDTH), cur),
            pl.BlockSpec((QB, KV4), prev),
            pl.BlockSpec((TQ, KV4), cur),
            pl.BlockSpec((QB, KV4), nxt),
            pl.BlockSpec((QB, KV4), prev),
            pl.BlockSpec((TQ, KV4), cur),
            pl.BlockSpec((QB, KV4), nxt),
        ],
        out_specs=pl.BlockSpec((TQ, ATTN_WIDTH), cur),
        out_shape=jax.ShapeDtypeStruct((N_TOK, ATTN_WIDTH), BF16),
        scratch_shapes=[pltpu.VMEM((TQ + 2 * QB, KV4), BF16),
                        pltpu.VMEM((TQ + 2 * QB, KV4), BF16)],
        compiler_params=_cparams("parallel", "parallel"),
    )(sink, q, k, k, k, v, v, v)


def _outproj_kernel(xp_ref, xs_ref, yf_ref, ya_ref, wo_ref, g_ref, wr_ref, x1_ref, xn_ref, aff_ref):
    mix = jnp.dot(yf_ref[...], wo_ref[:FOURIER_WIDTH, :], preferred_element_type=F32)
    mix = mix + jnp.dot(ya_ref[...], wo_ref[FOURIER_WIDTH:, :], preferred_element_type=F32)
    x1 = _x_tile(xp_ref, xs_ref) + mix
    x1_ref[...] = x1
    ms = jnp.mean(x1 * x1, axis=-1, keepdims=True)
    xn = (x1 * lax.rsqrt(ms + EPS) * g_ref[...]).astype(BF16)
    xn_ref[...] = xn
    logits = lax.dot_general(wr_ref[...], xn, (((1,), (1,)), ((), ())), preferred_element_type=F32)
    mx = jnp.max(logits, axis=0, keepdims=True)
    ex = jnp.exp(logits - mx)
    aff_ref[...] = ex / jnp.sum(ex, axis=0, keepdims=True)


def _outproj(xp, xs, yf, ya, w_out, g, w_router_t):
    return pl.pallas_call(
        _outproj_kernel,
        grid=(N_TOK // TM,),
        in_specs=_x_specs() + [
            pl.BlockSpec((TM, FOURIER_WIDTH), lambda i: (i, 0)),
            pl.BlockSpec((TM, ATTN_WIDTH), lambda i: (i, 0)),
            pl.BlockSpec((D_MODEL, D_MODEL), lambda i: (0, 0)),
            pl.BlockSpec((1, D_MODEL), lambda i: (0, 0)),
            pl.BlockSpec((N_EXPERTS, D_MODEL), lambda i: (0, 0)),
        ],
        out_specs=[
            pl.BlockSpec((TM, D_MODEL), lambda i: (i, 0)),
            pl.BlockSpec((TM, D_MODEL), lambda i: (i, 0)),
            pl.BlockSpec((N_EXPERTS, TM), lambda i: (0, i)),
        ],
        out_shape=[
            jax.ShapeDtypeStruct((N_TOK, D_MODEL), F32),
            jax.ShapeDtypeStruct((N_TOK, D_MODEL), BF16),
            jax.ShapeDtypeStruct((N_EXPERTS, N_TOK), F32),
        ],
        compiler_params=_cparams("parallel"),
    )(xp, xs, yf, ya, w_out, g, w_router_t)


RB = 512
N_RB = N_TOK // RB
GROUP_TOKENS = (SEQ, 2 * SEQ)
GROUP0_BLOCKS = GROUP_TOKENS[0] // RB
SEL_BLOCKS = 4
assert GROUP0_BLOCKS % SEL_BLOCKS == 0 and N_RB % SEL_BLOCKS == 0
CHUNK = 8
TILE_ROWS = 256
CHUNKS_PER_TILE = TILE_ROWS // CHUNK
SLOT_ROWS = EC_CAPACITY_FACTOR * N_TOK // N_EXPERTS
assert all((EC_CAPACITY_FACTOR * n // N_EXPERTS) % CHUNK == 0 for n in GROUP_TOKENS)
SLOT_TILES = SLOT_ROWS // FF_TM
assert SLOT_TILES * FF_TM == SLOT_ROWS
SLOT_CHUNKS = SLOT_ROWS // CHUNK
GROUP_TILES = 4
TILE_SLOTS = 8
LOOKAHEAD = TILE_SLOTS - GROUP_TILES
TRASH_CHUNK = N_EXPERTS * SLOT_CHUNKS
HEAD_TRASH_CHUNK = TRASH_CHUNK + TILE_SLOTS * CHUNKS_PER_TILE
TRASH_CHUNKS = TILE_SLOTS * CHUNKS_PER_TILE + FF_TM // CHUNK
assert N_EXPERTS <= FF_TM // CHUNK and TRASH_CHUNKS % (FF_TM // CHUNK) == 0
MAX_CHUNKS = (N_EXPERTS * (RB + 2 * CHUNK)) // CHUNK
LIST_UNROLL = 4
LIST_LEN = (MAX_CHUNKS // CHUNKS_PER_TILE + 2) * CHUNKS_PER_TILE
CHUNK_SHAPE = (2 * CHUNK, D_MODEL // 2)


def _pack_chunks(rows_f32):
    n = rows_f32.shape[0] // CHUNK
    r3 = rows_f32.reshape(n, CHUNK, D_MODEL)
    half = D_MODEL // 2
    return jnp.concatenate([r3[:, :, :half], r3[:, :, half:]], axis=1).astype(BF16)


def _unpack_chunks(chunks_bf16):
    n = chunks_bf16.shape[0]
    c3 = chunks_bf16.astype(F32)
    left = c3[:, :CHUNK, :].reshape(n * CHUNK, D_MODEL // 2)
    right = c3[:, CHUNK:, :].reshape(n * CHUNK, D_MODEL // 2)
    return jnp.concatenate([left, right], axis=1).astype(BF16)


def _thresh_kernel(aff_ref, thr_ref, need_ref):
    first = 0
    for g, n in enumerate(GROUP_TOKENS):
        cap = float(EC_CAPACITY_FACTOR * n // N_EXPERTS)
        aff = aff_ref[:, first:first + n]
        first += n

        def body(i, cur, aff=aff, cap=cap):
            cand = cur | jnp.left_shift(jnp.int32(1), 30 - i)
            cnt = jnp.sum(jnp.where(aff >= pltpu.bitcast(cand, F32), 1.0, 0.0), axis=1, keepdims=True)
            return jnp.where(cnt >= cap, cand, cur)

        thr = pltpu.bitcast(lax.fori_loop(0, 31, body, jnp.zeros((N_EXPERTS, 1), jnp.int32)), F32)
        n_gt = jnp.sum(jnp.where(aff > thr, 1.0, 0.0), axis=1, keepdims=True)
        rows = slice(g * N_EXPERTS, (g + 1) * N_EXPERTS)
        thr_ref[rows, :] = jnp.broadcast_to(thr, (N_EXPERTS, LANES))
        need_ref[rows, :] = jnp.broadcast_to(cap - n_gt, (N_EXPERTS, LANES))


def _thresholds(aff_t):
    n_groups = len(GROUP_TOKENS)
    return pl.pallas_call(
        _thresh_kernel,
        out_shape=[jax.ShapeDtypeStruct((n_groups * N_EXPERTS, LANES), F32),
                   jax.ShapeDtypeStruct((n_groups * N_EXPERTS, LANES), F32)],
        compiler_params=pltpu.CompilerParams(vmem_limit_bytes=VMEM_LIMIT),
    )(aff_t)


def _select_kernel(aff_ref, thr_ref, need_ref, utri_ref, selw_ref, rank_ref, m_ref, eq_seen):
    step = pl.program_id(0)

    @pl.when((step == 0) | (step == GROUP0_BLOCKS // SEL_BLOCKS))
    def _():
        eq_seen[...] = jnp.zeros_like(eq_seen)

    thr = thr_ref[:, 0:1]
    need = need_ref[:, 0:1]
    utri = utri_ref[...]
    for b in range(SEL_BLOCKS):
        cols = slice(b * RB, (b + 1) * RB)
        aff = aff_ref[:, cols]
        eq = jnp.where(aff == thr, 1.0, 0.0)
        eq_before = jnp.dot(eq.astype(BF16), utri, preferred_element_type=F32) + eq_seen[...]
        sel = (aff > thr) | ((aff == thr) & (eq_before < need))
        eq_seen[...] += jnp.sum(eq, axis=1, keepdims=True)
        self = jnp.where(sel, 1.0, 0.0)
        rank_ref[:, cols] = jnp.dot(self.astype(BF16), utri, preferred_element_type=F32)
        selw_ref[:, cols] = jnp.where(sel, aff, 0.0)
        m_ref[b] = jnp.broadcast_to(jnp.sum(self, axis=1, keepdims=True), (N_EXPERTS, LANES))


def _select(aff_t, thr, need):
    utri = jnp.asarray(np.triu(np.ones((RB, RB), np.float32), 1), dtype=BF16)
    grp = lambda i: (jnp.where(i >= GROUP0_BLOCKS // SEL_BLOCKS, 1, 0), 0)
    return pl.pallas_call(
        _select_kernel,
        grid=(N_RB // SEL_BLOCKS,),
        in_specs=[
            pl.BlockSpec((N_EXPERTS, SEL_BLOCKS * RB), lambda i: (0, i)),
            pl.BlockSpec((N_EXPERTS, LANES), grp),
            pl.BlockSpec((N_EXPERTS, LANES), grp),
            pl.BlockSpec((RB, RB), lambda i: (0, 0)),
        ],
        out_specs=[
            pl.BlockSpec((N_EXPERTS, SEL_BLOCKS * RB), lambda i: (0, i)),
            pl.BlockSpec((N_EXPERTS, SEL_BLOCKS * RB), lambda i: (0, i)),
            pl.BlockSpec((SEL_BLOCKS, N_EXPERTS, LANES), lambda i: (i, 0, 0)),
        ],
        out_shape=[
            jax.ShapeDtypeStruct((N_EXPERTS, N_TOK), F32),
            jax.ShapeDtypeStruct((N_EXPERTS, N_TOK), F32),
            jax.ShapeDtypeStruct((N_RB, N_EXPERTS, LANES), F32),
        ],
        scratch_shapes=[pltpu.VMEM((N_EXPERTS, 1), F32)],
        compiler_params=_cparams("arbitrary"),
    )(aff_t, thr, need, utri)


def _fill_chunk_list(list_ref, which, rb, seg_start_ref, seg_rows_ref, filler):
    cnt = jnp.int32(0)
    for e in range(N_EXPERTS):
        base = e * SLOT_CHUNKS + seg_start_ref[rb, e] // CHUNK
        n = seg_rows_ref[rb, e] // CHUNK

        def body(k, carry, base=base, cnt=cnt):
            for u in range(LIST_UNROLL):
                list_ref[which, cnt + LIST_UNROLL * k + u] = base + LIST_UNROLL * k + u
            return carry

        lax.fori_loop(0, (n + LIST_UNROLL - 1) // LIST_UNROLL, body, 0)
        cnt = cnt + n
    for j in range(CHUNKS_PER_TILE):
        list_ref[which, cnt + j] = filler(cnt + j)
    return cnt


WIN_ROWS = 128
SPAN_TILES = 8
SPAN_ROWS = SPAN_TILES * TILE_ROWS
assert SPAN_TILES % GROUP_TILES == 0


def _mark_span(scr, rb, s0, n_rows, tgt, values, seg_off_ref, seg_rows_ref):
    win_rows = lax.broadcasted_iota(jnp.int32, (WIN_ROWS, RB), 0).astype(F32)
    n_here = jnp.minimum(n_rows - s0, SPAN_ROWS)

    def zero(i, carry):
        scr[pl.ds(pl.multiple_of(i * TILE_ROWS, TILE_ROWS), TILE_ROWS), :] = jnp.zeros((TILE_ROWS, RB), F32)
        return carry

    lax.fori_loop(0, (n_here + TILE_ROWS - 1) // TILE_ROWS, zero, 0)
    for e in range(N_EXPERTS):
        lo = seg_off_ref[rb, e]
        first = jnp.maximum(lo, s0)
        last = jnp.minimum(lo + seg_rows_ref[rb, e], s0 + SPAN_ROWS)

        def mark(w, carry, e=e, first=first):
            start = first + w * WIN_ROWS
            win = pl.ds(pl.multiple_of(start - s0, CHUNK), WIN_ROWS)
            scr[win, :] = jnp.where(tgt[e:e + 1, :] == win_rows + start.astype(F32), values[e], scr[win, :])
            return carry

        lax.fori_loop(0, jnp.maximum(last - first + WIN_ROWS - 1, 0) // WIN_ROWS, mark, 0)


def _expert_column(ref, rb):
    e_iota = lax.broadcasted_iota(jnp.int32, (N_EXPERTS, 1), 0)
    col = jnp.zeros((N_EXPERTS, 1), F32)
    for e in range(N_EXPERTS):
        col = jnp.where(e_iota == e, ref[rb, e].astype(F32), col)
    return col


def _gather_kernel(in_start_ref, in_rows_ref, in_off_ref, tot_ref, in_rank_ref, seg_start_ref, seg_len_ref,
                   x_ref, selw_ref, rank_ref, xe_hbm,
                   p_scr, obuf, hbuf, carry, zbuf, dst_list, pending, sem, hsem, zsem):
    rb = pl.program_id(0)
    n_rows = tot_ref[rb]
    n_tiles = (n_rows + TILE_ROWS - 1) // TILE_ROWS
    head_slot = TILE_SLOTS

    @pl.when(rb == 0)
    def _():
        for slot in range(TILE_SLOTS + 1):
            pending[slot] = 0
        p_scr[...] = jnp.zeros_like(p_scr)
        carry[...] = jnp.zeros_like(carry)

    _fill_chunk_list(dst_list, 0, rb, in_start_ref, in_rows_ref,
                     lambda c: TRASH_CHUNK + c % (TILE_SLOTS * CHUNKS_PER_TILE))

    def tile_copy(slot):
        return pltpu.make_async_copy(obuf.at[slot], xe_hbm.at[pl.ds(0, CHUNKS_PER_TILE)], sem.at[slot])

    def head_copy():
        return pltpu.make_async_copy(hbuf, xe_hbm.at[pl.ds(0, N_EXPERTS)], hsem)

    x = x_ref[...]
    rank = rank_ref[...]
    sel = selw_ref[...] > 0.0

    j8 = lax.broadcasted_iota(jnp.int32, (CHUNK, RB), 0).astype(F32)
    heads, tails, flags = [], [], []
    for e in range(N_EXPERTS):
        start = seg_start_ref[rb, e]
        end = start + seg_len_ref[rb, e]
        nonempty = seg_len_ref[rb, e] > 0
        has_head = nonempty & (start % CHUNK != 0)
        has_tail = nonempty & (end % CHUNK != 0)
        single = has_head & has_tail & (start // CHUNK == end // CHUNK)
        head_shift = (start % CHUNK).astype(F32)
        tail_rank0 = ((end // CHUNK) * CHUNK - start).astype(F32)
        r_e = rank[e:e + 1, :]
        s_e = sel[e:e + 1, :]
        heads.append(jnp.where(s_e & (r_e + head_shift == j8), has_head.astype(F32), 0.0))
        tails.append(jnp.where(s_e & (r_e - tail_rank0 == j8), (has_tail & ~single).astype(F32), 0.0))
        flags.append((has_head, has_tail, single, start // CHUNK))
    p_edge = jnp.concatenate(heads + tails, axis=0).astype(BF16)
    edge = jnp.dot(p_edge, x, preferred_element_type=F32)

    @pl.when(pending[head_slot] == 1)
    def _():
        head_copy().wait()

    head_rows = []
    for e, (has_head, has_tail, single, head_chunk) in enumerate(flags):
        rows = slice(e * CHUNK, (e + 1) * CHUNK)
        old = carry[rows, :]
        head = edge[rows, :] + old
        tail = edge[N_EXPERTS * CHUNK + e * CHUNK:N_EXPERTS * CHUNK + (e + 1) * CHUNK, :]
        head_rows.append(head)
        keep = ~single & ~has_tail & ~has_head
        carry[rows, :] = (single.astype(F32) * head + (has_tail & ~single).astype(F32) * tail
                          + keep.astype(F32) * old)
    hbuf[...] = _pack_chunks(jnp.concatenate(head_rows, axis=0))
    for e, (has_head, has_tail, single, head_chunk) in enumerate(flags):
        dst = jnp.where(has_head & ~single, e * SLOT_CHUNKS + head_chunk, HEAD_TRASH_CHUNK + e)
        pltpu.make_async_copy(hbuf.at[e], xe_hbm.at[dst], hsem).start()
    pending[head_slot] = 1

    off = _expert_column(in_off_ref, rb)
    rank0 = _expert_column(in_rank_ref, rb)
    n_in = _expert_column(in_rows_ref, rb)
    tgt = jnp.where(sel & (rank >= rank0) & (rank < rank0 + n_in), rank - rank0 + off, -1.0)

    def free_slot(slot):
        @pl.when(pending[slot] == 1)
        def _():
            tile_copy(slot).wait()

    def emit(t, slot, rows_f32):
        obuf[slot] = _pack_chunks(rows_f32)
        for i in range(CHUNKS_PER_TILE):
            pltpu.make_async_copy(obuf.at[slot, i], xe_hbm.at[dst_list[0, t * CHUNKS_PER_TILE + i]],
                                  sem.at[slot]).start()
        pending[slot] = 1

    def group_body(g, carry_):
        t = GROUP_TILES * g
        cnt = jnp.minimum(n_tiles - t, GROUP_TILES)
        for k in range(GROUP_TILES):
            @pl.when(k < cnt)
            def _(k=k):
                free_slot((t + k) % TILE_SLOTS)

        @pl.when(t % SPAN_TILES == 0)
        def _():
            _mark_span(p_scr, rb, t * TILE_ROWS, n_rows, tgt, [1.0] * N_EXPERTS, in_off_ref, in_rows_ref)

        first_row = pl.multiple_of((t % SPAN_TILES) * TILE_ROWS, TILE_ROWS)
        for n in range(1, GROUP_TILES + 1):
            @pl.when(cnt == n)
            def _(n=n):
                p = p_scr[pl.ds(first_row, n * TILE_ROWS), :]
                rows = jnp.dot(p.astype(BF16), x, preferred_element_type=F32)
                for k in range(n):
                    emit(t + k, (t + k) % TILE_SLOTS, rows[k * TILE_ROWS:(k + 1) * TILE_ROWS])

        return carry_

    lax.fori_loop(0, (n_tiles + GROUP_TILES - 1) // GROUP_TILES, group_body, 0)

    @pl.when(rb == N_RB - 1)
    def _():
        for slot in range(TILE_SLOTS):
            @pl.when(pending[slot] == 1)
            def _(slot=slot):
                tile_copy(slot).wait()
        head_copy().wait()
        zbuf[...] = jnp.zeros_like(zbuf)

        def zcopy(k):
            return pltpu.make_async_copy(zbuf, xe_hbm.at[TRASH_CHUNK + k], zsem)

        def zstart(k, c):
            zcopy(k).start()
            return c

        def zwait(k, c):
            zcopy(k).wait()
            return c

        lax.fori_loop(0, TRASH_CHUNKS, zstart, 0)
        lax.fori_loop(0, TRASH_CHUNKS, zwait, 0)


def _route_gather(tables, xn, selw_t, rank_t):
    gs = pltpu.PrefetchScalarGridSpec(
        num_scalar_prefetch=len(tables),
        grid=(N_RB,),
        in_specs=[
            pl.BlockSpec((RB, D_MODEL), lambda rb, *_: (rb, 0)),
            pl.BlockSpec((N_EXPERTS, RB), lambda rb, *_: (0, rb)),
            pl.BlockSpec((N_EXPERTS, RB), lambda rb, *_: (0, rb)),
        ],
        out_specs=pl.BlockSpec(memory_space=pl.ANY),
        scratch_shapes=[
            pltpu.VMEM((SPAN_ROWS + WIN_ROWS, RB), F32),
            pltpu.VMEM((TILE_SLOTS, CHUNKS_PER_TILE) + CHUNK_SHAPE, BF16),
            pltpu.VMEM((N_EXPERTS,) + CHUNK_SHAPE, BF16),
            pltpu.VMEM((N_EXPERTS * CHUNK, D_MODEL), F32),
            pltpu.VMEM(CHUNK_SHAPE, BF16),
            pltpu.SMEM((1, LIST_LEN), jnp.int32),
            pltpu.SMEM((TILE_SLOTS + 1,), jnp.int32),
            pltpu.SemaphoreType.DMA((TILE_SLOTS,)),
            pltpu.SemaphoreType.DMA,
            pltpu.SemaphoreType.DMA,
        ],
    )
    return pl.pallas_call(
        _gather_kernel,
        grid_spec=gs,
        out_shape=jax.ShapeDtypeStruct((TRASH_CHUNK + TRASH_CHUNKS,) + CHUNK_SHAPE, BF16),
        compiler_params=_cparams("arbitrary"),
    )(*tables, xn, selw_t, rank_t)


W_PIECES_PER_MATRIX = 4
W_PIECES = 3 * W_PIECES_PER_MATRIX
FFN_STEPS = max(SLOT_TILES, W_PIECES + 1)
WG_PIECE_ROWS = D_MODEL // W_PIECES_PER_MATRIX
WD_PIECE_ROWS = EXPERT_D_FF // W_PIECES_PER_MATRIX
FFN_VMEM_LIMIT = 56 * 1024 * 1024


def _ffn_kernel(x_ref, wg_hbm, wu_hbm, wd_hbm, o_ref,
                wg_bf, wu_bf, wd_bf, stage_up, stage_dn, sem):
    e = pl.program_id(0)
    i = pl.program_id(1)
    slot = e % 2

    matrices = ((wg_hbm, stage_up, wg_bf, WG_PIECE_ROWS),
                (wu_hbm, stage_up, wu_bf, WG_PIECE_ROWS),
                (wd_hbm, stage_dn, wd_bf, WD_PIECE_ROWS))

    def piece(expert, k, buf, finish):
        def of_kind(kind):
            w_hbm, stage, w_bf, rows = matrices[kind]
            r0 = pl.multiple_of((k - kind * W_PIECES_PER_MATRIX) * rows, rows)
            cp = pltpu.make_async_copy(w_hbm.at[expert, pl.ds(r0, rows), :], stage.at[buf], sem.at[buf])
            if finish:
                cp.wait()
                w_bf[expert % 2, pl.ds(r0, rows), :] = stage[buf].astype(BF16)
            else:
                cp.start()

        if isinstance(k, int):
            of_kind(k // W_PIECES_PER_MATRIX)
        else:
            for kind in range(len(matrices)):
                pl.when(k // W_PIECES_PER_MATRIX == kind)(functools.partial(of_kind, kind))

    @pl.when((e == 0) & (i == 0))
    def _():
        for k in range(W_PIECES):
            piece(0, k, k % 2, finish=False)
            piece(0, k, k % 2, finish=True)

    @pl.when(e + 1 < N_EXPERTS)
    def _():
        @pl.when((i >= 1) & (i <= W_PIECES))
        def _():
            piece(e + 1, i - 1, (i - 1) % 2, finish=True)

        @pl.when(i < W_PIECES)
        def _():
            piece(e + 1, i, i % 2, finish=False)

    @pl.when(i < SLOT_TILES)
    def _():
        x = _unpack_chunks(x_ref[...])
        acc = None
        for c in range(EXPERT_D_FF // FF_FC):
            gt = jnp.dot(x, wg_bf[slot, :, c * FF_FC:(c + 1) * FF_FC], preferred_element_type=F32)
            up = jnp.dot(x, wu_bf[slot, :, c * FF_FC:(c + 1) * FF_FC], preferred_element_type=F32)
            h = (gt * (1.0 / (1.0 + jnp.exp(-gt))) * up).astype(BF16)
            down = jnp.dot(h, wd_bf[slot, c * FF_FC:(c + 1) * FF_FC, :], preferred_element_type=F32)
            acc = down if acc is None else acc + down
        o_ref[...] = _pack_chunks(acc)


def _ffn(xe, w_gate, w_up, w_down):
    def rows(e, i):
        return (e * SLOT_TILES + jnp.minimum(i, SLOT_TILES - 1), 0, 0)

    tile_chunks = (FF_TM // CHUNK,) + CHUNK_SHAPE
    gs = pltpu.PrefetchScalarGridSpec(
        num_scalar_prefetch=0,
        grid=(N_EXPERTS, FFN_STEPS),
        in_specs=[
            pl.BlockSpec(tile_chunks, rows),
            pl.BlockSpec(memory_space=pl.ANY),
            pl.BlockSpec(memory_space=pl.ANY),
            pl.BlockSpec(memory_space=pl.ANY),
        ],
        out_specs=pl.BlockSpec(tile_chunks, rows),
        scratch_shapes=[
            pltpu.VMEM((2, D_MODEL, EXPERT_D_FF), BF16),
            pltpu.VMEM((2, D_MODEL, EXPERT_D_FF), BF16),
            pltpu.VMEM((2, EXPERT_D_FF, D_MODEL), BF16),
            pltpu.VMEM((2, WG_PIECE_ROWS, EXPERT_D_FF), F32),
            pltpu.VMEM((2, WD_PIECE_ROWS, D_MODEL), F32),
            pltpu.SemaphoreType.DMA((2,)),
        ],
    )
    return pl.pallas_call(
        _ffn_kernel,
        grid_spec=gs,
        out_shape=jax.ShapeDtypeStruct((N_EXPERTS * SLOT_CHUNKS,) + CHUNK_SHAPE, BF16),
        compiler_params=pltpu.CompilerParams(dimension_semantics=("arbitrary", "arbitrary"),
                                             vmem_limit_bytes=FFN_VMEM_LIMIT),
    )(xe, w_gate, w_up, w_down)


def _combine_kernel(seg_start_ref, seg_rows_ref, seg_off_ref, tot_ref, seg_shift_ref,
                    x1_ref, selw_ref, rank_ref, g_ref, ye_hbm, op_ref, os_ref,
                    w_scr, ybuf, acc, src_list, state, sem):
    rb = pl.program_id(0)
    cur = rb % 2
    nxt_rb = jnp.minimum(rb + 1, N_RB - 1)

    def tiles_of(b):
        return (tot_ref[b] + TILE_ROWS - 1) // TILE_ROWS

    n_tiles = tiles_of(rb)
    n_tiles_next = jnp.where(rb + 1 < N_RB, tiles_of(nxt_rb), 0)

    def fetch(which, t, slot):
        for i in range(CHUNKS_PER_TILE):
            pltpu.make_async_copy(ye_hbm.at[src_list[which, t * CHUNKS_PER_TILE + i]], ybuf.at[slot, i],
                                  sem.at[slot]).start()

    def tile_wait(slot):
        pltpu.make_async_copy(ye_hbm.at[pl.ds(0, CHUNKS_PER_TILE)], ybuf.at[slot], sem.at[slot]).wait()

    @pl.when(rb == 0)
    def _():
        w_scr[...] = jnp.zeros_like(w_scr)
        _fill_chunk_list(src_list, 0, 0, seg_start_ref, seg_rows_ref, lambda c: 0)
        state[0] = 0
        state[1] = 0

    @pl.when(rb + 1 < N_RB)
    def _():
        _fill_chunk_list(src_list, 1 - cur, nxt_rb, seg_start_ref, seg_rows_ref, lambda c: 0)

    slot0 = state[0]
    started = state[1]
    for j in range(LOOKAHEAD):
        @pl.when((j >= started) & (j < n_tiles))
        def _(j=j):
            fetch(cur, j, (slot0 + j) % TILE_SLOTS)

    off = _expert_column(seg_off_ref, rb) + _expert_column(seg_shift_ref, rb)
    selw = selw_ref[...]
    gates = [selw[e:e + 1, :] for e in range(N_EXPERTS)]
    tgt = jnp.where(selw > 0.0, rank_ref[...] + off, -1.0)
    acc[...] = x1_ref[...]

    def prefetch(t, next_started):
        ahead = t + LOOKAHEAD

        @pl.when(ahead < n_tiles)
        def _():
            fetch(cur, ahead, (slot0 + ahead) % TILE_SLOTS)

        start_next = (ahead >= n_tiles) & (next_started < jnp.minimum(LOOKAHEAD, n_tiles_next))

        @pl.when(start_next)
        def _():
            fetch(1 - cur, next_started, (slot0 + n_tiles + next_started) % TILE_SLOTS)

        return next_started + start_next.astype(jnp.int32)

    def add_tiles(t, n):
        w = w_scr[pl.ds(pl.multiple_of((t % SPAN_TILES) * TILE_ROWS, TILE_ROWS), n * TILE_ROWS), :]
        ys = [_unpack_chunks(ybuf[(slot0 + t + k) % TILE_SLOTS]) for k in range(n)]
        y = ys[0] if n == 1 else jnp.concatenate(ys, axis=0)
        acc[...] += lax.dot_general(w.astype(BF16), y, (((0,), (0,)), ((), ())),
                                    preferred_element_type=F32)

    def group_body(g, next_started):
        t = GROUP_TILES * g
        cnt = jnp.minimum(n_tiles - t, GROUP_TILES)
        for k in range(GROUP_TILES):
            next_started = prefetch(t + k, next_started)

        @pl.when(t % SPAN_TILES == 0)
        def _():
            _mark_span(w_scr, rb, t * TILE_ROWS, tot_ref[rb], tgt, gates, seg_off_ref, seg_rows_ref)

        for k in range(GROUP_TILES):
            @pl.when(k < cnt)
            def _(k=k):
                tile_wait((slot0 + t + k) % TILE_SLOTS)

        for n in range(1, GROUP_TILES + 1):
            @pl.when(cnt == n)
            def _(n=n):
                add_tiles(t, n)

        return next_started

    state[1] = lax.fori_loop(0, (n_tiles + GROUP_TILES - 1) // GROUP_TILES, group_body, jnp.int32(0))
    state[0] = (slot0 + n_tiles) % TILE_SLOTS
    x2 = acc[...]
    ms = jnp.mean(x2 * x2, axis=-1, keepdims=True)
    y = x2 * lax.rsqrt(ms + EPS) * g_ref[...]

    @pl.when(rb < GROUP0_BLOCKS)
    def _():
        op_ref[...] = y

    @pl.when(rb >= GROUP0_BLOCKS)
    def _():
        os_ref[...] = y


def _combine(tables, x1, selw_t, rank_t, g, ye):
    gs = pltpu.PrefetchScalarGridSpec(
        num_scalar_prefetch=len(tables),
        grid=(N_RB,),
        in_specs=[
            pl.BlockSpec((RB, D_MODEL), lambda rb, *_: (rb, 0)),
            pl.BlockSpec((N_EXPERTS, RB), lambda rb, *_: (0, rb)),
            pl.BlockSpec((N_EXPERTS, RB), lambda rb, *_: (0, rb)),
            pl.BlockSpec((1, D_MODEL), lambda rb, *_: (0, 0)),
            pl.BlockSpec(memory_space=pl.ANY),
        ],
        out_specs=[
            pl.BlockSpec((RB, D_MODEL), lambda rb, *_: (jnp.minimum(rb, GROUP0_BLOCKS - 1), 0)),
            pl.BlockSpec((RB, D_MODEL), lambda rb, *_: (jnp.maximum(rb - GROUP0_BLOCKS, 0), 0)),
        ],
        scratch_shapes=[
            pltpu.VMEM((SPAN_ROWS + WIN_ROWS, RB), F32),
            pltpu.VMEM((TILE_SLOTS, CHUNKS_PER_TILE) + CHUNK_SHAPE, BF16),
            pltpu.VMEM((RB, D_MODEL), F32),
            pltpu.SMEM((2, LIST_LEN), jnp.int32),
            pltpu.SMEM((2,), jnp.int32),
            pltpu.SemaphoreType.DMA((TILE_SLOTS,)),
        ],
    )
    return pl.pallas_call(
        _combine_kernel,
        grid_spec=gs,
        out_shape=[jax.ShapeDtypeStruct((GROUP_TOKENS[0], D_MODEL), F32),
                   jax.ShapeDtypeStruct((GROUP_TOKENS[1], D_MODEL), F32)],
        compiler_params=_cparams("arbitrary"),
    )(*tables, x1, selw_t, rank_t, g, ye)


def kernel(x_prompt, x_sample, norm_mix_g, w_in, w_fourier, sink, w_out, norm_ffn_g, w_router,
           w_gate, w_up, w_down, norm_final_g):
    assert x_prompt.shape == (1, SEQ, D_MODEL) and x_sample.shape == (N_SEQ - 1, SEQ, D_MODEL)
    assert w_in.shape == (1, D_MODEL, IN_WIDTH) and w_out.shape == (1, MIX_WIDTH, D_MODEL)
    assert w_gate.shape == (1, N_EXPERTS, D_MODEL, EXPERT_D_FF) and w_down.shape == (1, N_EXPERTS, EXPERT_D_FF, D_MODEL)
    xp = x_prompt.reshape(-1, D_MODEL)
    xs = x_sample.reshape(-1, D_MODEL)
    cos_t, sin_t = _rope_tables()

    mix_cols = np.concatenate([
        np.arange(FOURIER_WIDTH),
        FOURIER_WIDTH + (np.asarray(Q_ORDER)[:, None] * HEAD_DIM + np.arange(HEAD_DIM)[None, :]).reshape(-1)])
    in_cols = np.concatenate([mix_cols, np.arange(MIX_WIDTH, IN_WIDTH)])
    w_in_r = w_in[0][:, in_cols].astype(BF16)
    w_out_r = w_out[0][mix_cols, :].astype(BF16)

    u, q, k, v = _inproj(xp, xs, norm_mix_g[0][None, :], w_in_r, cos_t, sin_t)
    yf = _fourier(u, w_fourier[0].astype(BF16))
    ya = _attention(sink[0], q, k, v)
    x1, xn, aff_t = _outproj(xp, xs, yf, ya, w_out_r, norm_ffn_g[0][None, :],
                             w_router[0].T.astype(BF16))

    thr, need = _thresholds(aff_t)
    selw_t, rank_t, m = _select(aff_t, thr, need)

    seg_len = m[:, :, 0].astype(jnp.int32)
    seg_start = jnp.cumsum(seg_len, axis=0) - seg_len
    seg_end = seg_start + seg_len

    def offsets(rows):
        return jnp.cumsum(rows, axis=1) - rows, jnp.sum(rows, axis=1)

    in_start = -(-seg_start // CHUNK) * CHUNK
    in_rows = jnp.maximum(seg_end // CHUNK * CHUNK - in_start, 0)
    in_off, in_tot = offsets(in_rows)
    gather_tables = (in_start, in_rows, in_off, in_tot, in_start - seg_start, seg_start, seg_len)
    out_start = seg_start // CHUNK * CHUNK
    out_rows = jnp.where(seg_len > 0, -(-seg_end // CHUNK) * CHUNK - out_start, 0)
    out_off, out_tot = offsets(out_rows)
    combine_tables = (out_start, out_rows, out_off, out_tot, seg_start - out_start)

    xe = _route_gather(gather_tables, xn, selw_t, rank_t)
    ye = _ffn(xe, w_gate[0], w_up[0], w_down[0])
    yp, ys = _combine(combine_tables, x1, selw_t, rank_t, norm_final_g[None, :], ye)
    return (yp.reshape(x_prompt.shape), ys.reshape(x_sample.shape))
```

```python
import functools

import numpy as np
import jax
import jax.numpy as jnp
from jax import lax
from jax.experimental import pallas as pl
from jax.experimental.pallas import tpu as pltpu

F32 = jnp.float32
BF16 = jnp.bfloat16

D_MODEL = 1024
SEQ = 16384
N_SEQ = 3
N_TOK = N_SEQ * SEQ
HEAD_DIM = 64
N_HEADS = 12
N_KV_HEADS = 4
GQA_GROUP = N_HEADS // N_KV_HEADS
ATTN_WIDTH = N_HEADS * HEAD_DIM
KV_WIDTH = N_KV_HEADS * HEAD_DIM
FOURIER_WIDTH = 256
FOURIER_GROUP_DIM = 64
MIX_WIDTH = FOURIER_WIDTH + ATTN_WIDTH
IN_WIDTH = MIX_WIDTH + 2 * KV_WIDTH
WINDOW = 128
ROPE_THETA = 10000.0
N_EXPERTS = 16
EC_CAPACITY_FACTOR = 2
EXPERT_D_FF = 2048
EPS = 1e-6
NEG = -1e30

LANES = 128
FFT_R = 128
VMEM_LIMIT = 48 * 1024 * 1024

TM = 1024
TQ = 1024
QB = 128
FF_TM = 512
FF_FC = 512


def _cparams(*sem):
    return pltpu.CompilerParams(dimension_semantics=sem, vmem_limit_bytes=VMEM_LIMIT)


def _x_specs():
    n_p = SEQ // TM
    return [pl.BlockSpec((TM, D_MODEL), lambda i, *_: (jnp.minimum(i, n_p - 1), 0)),
            pl.BlockSpec((TM, D_MODEL), lambda i, *_: (jnp.maximum(i - n_p, 0), 0))]


def _on_x_tile(body, xp_ref, xs_ref, *refs):
    is_prompt = pl.program_id(0) < SEQ // TM
    pl.when(is_prompt)(lambda: body(xp_ref, *refs))
    pl.when(jnp.logical_not(is_prompt))(lambda: body(xs_ref, *refs))


def _inproj_kernel(*refs):
    _on_x_tile(_inproj_body, *refs)


def _inproj_body(x_ref, g_ref, w_ref, cos_ref, sin_ref, u_ref, q_ref, k_ref, v_ref):
    x = x_ref[...]
    ms = jnp.mean(x * x, axis=-1, keepdims=True)
    hn = (x * lax.rsqrt(ms + EPS) * g_ref[...]).astype(BF16)
    proj = jnp.dot(hn, w_ref[...], preferred_element_type=F32)
    u_ref[...] = proj[:, :FOURIER_WIDTH].astype(BF16)
    cos = cos_ref[...]
    sin = sin_ref[...]
    lane = lax.broadcasted_iota(jnp.int32, cos.shape, 1)
    first_half = (lane % HEAD_DIM) < (HEAD_DIM // 2)

    def rope(t):
        partner = jnp.where(first_half,
                            pltpu.roll(t, LANES - HEAD_DIM // 2, 1),
                            pltpu.roll(t, HEAD_DIM // 2, 1))
        return t * cos + partner * sin

    scale = HEAD_DIM ** -0.5 * LOG2E
    for c in range(ATTN_WIDTH // LANES):
        lo = FOURIER_WIDTH + c * LANES
        q_ref[:, c * LANES:(c + 1) * LANES] = (rope(proj[:, lo:lo + LANES]) * scale).astype(BF16)
    low_half = lane < HEAD_DIM
    for c in range(KV_WIDTH // LANES):
        lo = FOURIER_WIDTH + ATTN_WIDTH + c * LANES
        kc = rope(proj[:, lo:lo + LANES])
        vc = proj[:, lo + KV_WIDTH:lo + KV_WIDTH + LANES]
        for half, keep in enumerate((low_half, ~low_half)):
            g = 2 * c + half
            k_ref[:, g * LANES:(g + 1) * LANES] = jnp.where(keep, kc, 0.0).astype(BF16)
            v_ref[:, g * LANES:(g + 1) * LANES] = jnp.where(keep, vc, 0.0).astype(BF16)


def _inproj(xp, xs, g, w_in, cos_t, sin_t):
    n_pos_blocks = SEQ // TM
    return pl.pallas_call(
        _inproj_kernel,
        grid=(N_TOK // TM,),
        in_specs=_x_specs() + [
            pl.BlockSpec((1, D_MODEL), lambda i: (0, 0)),
            pl.BlockSpec((D_MODEL, IN_WIDTH), lambda i: (0, 0)),
            pl.BlockSpec((TM, LANES), lambda i: (i % n_pos_blocks, 0)),
            pl.BlockSpec((TM, LANES), lambda i: (i % n_pos_blocks, 0)),
        ],
        out_specs=[
            pl.BlockSpec((TM, FOURIER_WIDTH), lambda i: (i, 0)),
            pl.BlockSpec((TM, ATTN_WIDTH), lambda i: (i, 0)),
            pl.BlockSpec((TM, KV4), lambda i: (i, 0)),
            pl.BlockSpec((TM, KV4), lambda i: (i, 0)),
        ],
        out_shape=[
            jax.ShapeDtypeStruct((N_TOK, FOURIER_WIDTH), BF16),
            jax.ShapeDtypeStruct((N_TOK, ATTN_WIDTH), BF16),
            jax.ShapeDtypeStruct((N_TOK, KV4), BF16),
            jax.ShapeDtypeStruct((N_TOK, KV4), BF16),
        ],
        compiler_params=_cparams("parallel"),
    )(xp, xs, g, w_in, cos_t, sin_t)


def _rope_tables():
    half = HEAD_DIM // 2
    inv_freq = 1.0 / (ROPE_THETA ** (jnp.arange(half, dtype=F32) / half))
    ang = jnp.arange(SEQ, dtype=F32)[:, None] * inv_freq[None, :]
    cos = jnp.cos(ang)
    sin = jnp.sin(ang)
    reps = LANES // HEAD_DIM
    cos_t = jnp.tile(jnp.concatenate([cos, cos], axis=-1), (1, reps))
    sin_t = jnp.tile(jnp.concatenate([-sin, sin], axis=-1), (1, reps))
    return cos_t, sin_t


def _dft_tables():
    r = FFT_R
    c = np.arange(FOURIER_GROUP_DIM)
    ang_c = 2.0 * np.pi * np.outer(c, c) / FOURIER_GROUP_DIM
    n_groups = FOURIER_WIDTH // FOURIER_GROUP_DIM
    eye = np.eye(n_groups)
    ch_scale = FOURIER_GROUP_DIM ** -0.5
    c_blk = np.kron(eye, np.cos(ang_c)) * ch_scale
    s_blk = np.kron(eye, np.sin(ang_c)) * ch_scale
    cs = np.concatenate([c_blk, -s_blk], axis=1)
    k = np.arange(r)
    ang_r = 2.0 * np.pi * np.outer(k, k) / r
    st_scale = r ** -0.5
    wr = np.cos(ang_r) * st_scale
    wi = -np.sin(ang_r) * st_scale
    wbig = np.block([[wr, -wi], [wi, wr]])
    ang_t = 2.0 * np.pi * np.outer(k, k) / (r * r)
    tr = np.cos(ang_t)
    ti = -np.sin(ang_t)
    f = lambda a: jnp.asarray(a, dtype=F32)
    return f(cs).astype(BF16), f(wbig).astype(BF16), f(wr), f(wi), f(tr), f(ti)


FS_SB = 16
FS_KB = 16


def _fft_stage1_kernel(u_ref, cs_ref, wbig_ref, y_ref):
    cs = cs_ref[...]
    wbig = wbig_ref[...]
    fw = FOURIER_WIDTH
    for j in range(FS_SB):
        u = u_ref[:, j * fw:(j + 1) * fw]
        g = jnp.dot(u, cs, preferred_element_type=F32)
        gst = jnp.concatenate([g[:, :fw], g[:, fw:]], axis=0).astype(BF16)
        y = jnp.dot(wbig, gst, preferred_element_type=F32)
        y_ref[:, j * 2 * fw:j * 2 * fw + fw] = y[:FFT_R].astype(BF16)
        y_ref[:, j * 2 * fw + fw:(j + 1) * 2 * fw] = y[FFT_R:].astype(BF16)


def _fft_stage2_kernel(y_ref, wr_ref, wi_ref, tr_ref, ti_ref, wf_ref, o_ref):
    kb = pl.program_id(1)
    wr = wr_ref[...]
    wi = wi_ref[...]
    wf = wf_ref[...]
    fw = FOURIER_WIDTH
    for j in range(FS_KB):
        k1 = kb * FS_KB + j
        tr = tr_ref[pl.ds(k1, 1), :]
        ti = ti_ref[pl.ds(k1, 1), :]
        mr = wr * tr - wi * ti
        mi = wr * ti + wi * tr
        m2 = jnp.concatenate([mr, -mi], axis=1).astype(BF16)
        yb = y_ref[j * FFT_R:(j + 1) * FFT_R, :]
        yst = jnp.concatenate([yb[:, :fw], yb[:, fw:]], axis=0)
        z = jnp.dot(m2, yst, preferred_element_type=F32)
        yf = jnp.dot(z.astype(BF16), wf, preferred_element_type=F32)
        o_ref[:, j * fw:(j + 1) * fw] = yf.astype(BF16)


def _fourier(u, w_fourier):
    cs, wbig, wr, wi, tr, ti = _dft_tables()
    r, fw = FFT_R, FOURIER_WIDTH
    u2 = u.reshape(N_SEQ * r, r * fw)
    y = pl.pallas_call(
        _fft_stage1_kernel,
        grid=(N_SEQ, r // FS_SB),
        in_specs=[
            pl.BlockSpec((r, FS_SB * fw), lambda s, j: (s, j)),
            pl.BlockSpec((fw, 2 * fw), lambda s, j: (0, 0)),
            pl.BlockSpec((2 * r, 2 * r), lambda s, j: (0, 0)),
        ],
        out_specs=pl.BlockSpec((r, FS_SB * 2 * fw), lambda s, j: (s, j)),
        out_shape=jax.ShapeDtypeStruct((N_SEQ * r, r * 2 * fw), BF16),
        compiler_params=_cparams("parallel", "parallel"),
    )(u2, cs, wbig)
    y2 = y.reshape(N_SEQ * r * r, 2 * fw)
    full = lambda s, j: (0, 0)
    yf = pl.pallas_call(
        _fft_stage2_kernel,
        grid=(N_SEQ, r // FS_KB),
        in_specs=[
            pl.BlockSpec((FS_KB * r, 2 * fw), lambda s, j: (s * (r // FS_KB) + j, 0)),
            pl.BlockSpec((r, r), full),
            pl.BlockSpec((r, r), full),
            pl.BlockSpec((r, r), full),
            pl.BlockSpec((r, r), full),
            pl.BlockSpec((fw, fw), full),
        ],
        out_specs=pl.BlockSpec((r, FS_KB * fw), lambda s, j: (s, j)),
        out_shape=jax.ShapeDtypeStruct((N_SEQ * r, r * fw), BF16),
        compiler_params=_cparams("parallel", "parallel"),
    )(y2, wr, wi, tr, ti, w_fourier)
    return yf.reshape(N_TOK, fw)


LO_HEADS = (0, 1, 2, 6, 7, 8)
HI_HEADS = (3, 4, 5, 9, 10, 11)
Q_ORDER = tuple(h for pair in zip(LO_HEADS, HI_HEADS) for h in pair)
assert all(h // GQA_GROUP % 2 == 0 for h in LO_HEADS) and all(h // GQA_GROUP % 2 == 1 for h in HI_HEADS)
KV4 = 2 * KV_WIDTH
LOG2E = 1.4426950408889634


def _attn_kernel(sink_ref, q_ref, kp_ref, kc_ref, kn_ref, vp_ref, vc_ref, vn_ref, o_ref,
                 kext, vext):
    i = pl.program_id(1)
    n_i = pl.num_programs(1)
    kext[0:QB, :] = kp_ref[...]
    kext[QB:QB + TQ, :] = kc_ref[...]
    kext[QB + TQ:, :] = kn_ref[...]
    vext[0:QB, :] = vp_ref[...]
    vext[QB:QB + TQ, :] = vc_ref[...]
    vext[QB + TQ:, :] = vn_ref[...]

    rows = GQA_GROUP * QB
    a = lax.broadcasted_iota(jnp.int32, (rows, QB), 0) % QB
    j = lax.broadcasted_iota(jnp.int32, (rows, QB), 1)
    band_prev = jnp.where(j >= a, 0.0, NEG)
    band_next = jnp.where(j <= a, 0.0, NEG)
    row_id = lax.broadcasted_iota(jnp.int32, (rows, 1), 0)
    low_half = lax.broadcasted_iota(jnp.int32, (rows, LANES), 1) < HEAD_DIM

    def softmax_half(s, sinks, bias_prev, bias_next):
        s_p = s[:, :QB] + bias_prev
        s_c = s[:, QB:2 * QB]
        s_n = s[:, 2 * QB:] + bias_next
        sink = jnp.where(row_id < QB, sinks[0], jnp.where(row_id < 2 * QB, sinks[1], sinks[2])) * LOG2E
        m = jnp.max(jnp.maximum(jnp.maximum(s_p, s_c), s_n), axis=-1, keepdims=True)
        m = jnp.maximum(m, sink)
        p_p = jnp.exp2(s_p - m)
        p_c = jnp.exp2(s_c - m)
        p_n = jnp.exp2(s_n - m)
        denom = jnp.sum(p_p + p_c + p_n, axis=-1, keepdims=True) + jnp.exp2(sink - m)
        return [p_p.astype(BF16), p_c.astype(BF16), p_n.astype(BF16)], 1.0 / denom

    for sb in range(TQ // QB):
        blk = i * (TQ // QB) + sb
        bias_prev = band_prev + jnp.where(blk == 0, NEG, 0.0)
        bias_next = band_next + jnp.where(blk == n_i * (TQ // QB) - 1, NEG, 0.0)
        for pr in range(N_KV_HEADS // 2):
            groups = [GQA_GROUP * pr + m for m in range(GQA_GROUP)]
            q3 = jnp.concatenate([q_ref[sb * QB:(sb + 1) * QB, g * LANES:(g + 1) * LANES] for g in groups],
                                 axis=0)
            kp = kext[sb * QB:sb * QB + 3 * QB, pr * 2 * LANES:(pr + 1) * 2 * LANES]
            vp = vext[sb * QB:sb * QB + 3 * QB, pr * 2 * LANES:(pr + 1) * 2 * LANES]
            kcat = jnp.concatenate([kp[:, :LANES], kp[:, LANES:]], axis=0)
            vcat = jnp.concatenate([vp[:, :LANES], vp[:, LANES:]], axis=0)
            s = lax.dot_general(q3, kcat, (((1,), (1,)), ((), ())), preferred_element_type=F32)
            p_lo, r_lo = softmax_half(s[:, :3 * QB], [sink_ref[LO_HEADS[g]] for g in groups],
                                      bias_prev, bias_next)
            p_hi, r_hi = softmax_half(s[:, 3 * QB:], [sink_ref[HI_HEADS[g]] for g in groups],
                                      bias_prev, bias_next)
            p = jnp.concatenate(p_lo + p_hi, axis=1)
            o = jnp.dot(p, vcat, preferred_element_type=F32) * jnp.where(low_half, r_lo, r_hi)
            for m, g in enumerate(groups):
                o_ref[sb * QB:(sb + 1) * QB, g * LANES:(g + 1) * LANES] = o[m * QB:(m + 1) * QB].astype(BF16)


def _attention(sink, q, k, v):
    nqb = SEQ // QB
    per = TQ // QB
    cur = lambda s, i: (s * (SEQ // TQ) + i, 0)
    prev = lambda s, i: (s * nqb + jnp.maximum(i * per - 1, 0), 0)
    nxt = lambda s, i: (s * nqb + jnp.minimum(i * per + per, nqb - 1), 0)
    return pl.pallas_call(
        _attn_kernel,
        grid=(N_SEQ, SEQ // TQ),
        in_specs=[
            pl.BlockSpec(memory_space=pltpu.SMEM),
            pl.BlockSpec((TQ, ATTN_WIDTH), cur),
            pl.BlockSpec((QB, KV4), prev),
            pl.BlockSpec((TQ, KV4), cur),
            pl.BlockSpec((QB, KV4), nxt),
            pl.BlockSpec((QB, KV4), prev),
            pl.BlockSpec((TQ, KV4), cur),
            pl.BlockSpec((QB, KV4), nxt),
        ],
        out_specs=pl.BlockSpec((TQ, ATTN_WIDTH), cur),
        out_shape=jax.ShapeDtypeStruct((N_TOK, ATTN_WIDTH), BF16),
        scratch_shapes=[pltpu.VMEM((TQ + 2 * QB, KV4), BF16),
                        pltpu.VMEM((TQ + 2 * QB, KV4), BF16)],
        compiler_params=_cparams("parallel", "parallel"),
    )(sink, q, k, k, k, v, v, v)


def _outproj_kernel(*refs):
    _on_x_tile(_outproj_body, *refs)


def _outproj_body(x_ref, yf_ref, ya_ref, wo_ref, g_ref, wr_ref, x1_ref, xn_ref, aff_ref):
    mix = jnp.dot(yf_ref[...], wo_ref[:FOURIER_WIDTH, :], preferred_element_type=F32)
    mix = mix + jnp.dot(ya_ref[...], wo_ref[FOURIER_WIDTH:, :], preferred_element_type=F32)
    x1 = x_ref[...] + mix
    x1_ref[...] = x1
    ms = jnp.mean(x1 * x1, axis=-1, keepdims=True)
    xn = (x1 * lax.rsqrt(ms + EPS) * g_ref[...]).astype(BF16)
    xn_ref[...] = xn
    logits = lax.dot_general(wr_ref[...], xn, (((1,), (1,)), ((), ())), preferred_element_type=F32)
    mx = jnp.max(logits, axis=0, keepdims=True)
    ex = jnp.exp(logits - mx)
    aff_ref[...] = ex / jnp.sum(ex, axis=0, keepdims=True)


def _outproj(xp, xs, yf, ya, w_out, g, w_router_t):
    return pl.pallas_call(
        _outproj_kernel,
        grid=(N_TOK // TM,),
        in_specs=_x_specs() + [
            pl.BlockSpec((TM, FOURIER_WIDTH), lambda i: (i, 0)),
            pl.BlockSpec((TM, ATTN_WIDTH), lambda i: (i, 0)),
            pl.BlockSpec((D_MODEL, D_MODEL), lambda i: (0, 0)),
            pl.BlockSpec((1, D_MODEL), lambda i: (0, 0)),
            pl.BlockSpec((N_EXPERTS, D_MODEL), lambda i: (0, 0)),
        ],
        out_specs=[
            pl.BlockSpec((TM, D_MODEL), lambda i: (i, 0)),
            pl.BlockSpec((TM, D_MODEL), lambda i: (i, 0)),
            pl.BlockSpec((N_EXPERTS, TM), lambda i: (0, i)),
        ],
        out_shape=[
            jax.ShapeDtypeStruct((N_TOK, D_MODEL), F32),
            jax.ShapeDtypeStruct((N_TOK, D_MODEL), BF16),
            jax.ShapeDtypeStruct((N_EXPERTS, N_TOK), F32),
        ],
        compiler_params=_cparams("parallel"),
    )(xp, xs, yf, ya, w_out, g, w_router_t)


RB = 512
N_RB = N_TOK // RB
GROUP_TOKENS = (SEQ, 2 * SEQ)
GROUP0_BLOCKS = GROUP_TOKENS[0] // RB
SEL_BLOCKS = 4
assert GROUP0_BLOCKS % SEL_BLOCKS == 0 and N_RB % SEL_BLOCKS == 0
CHUNK = 8
TILE_ROWS = 256
CHUNKS_PER_TILE = TILE_ROWS // CHUNK
SLOT_ROWS = EC_CAPACITY_FACTOR * N_TOK // N_EXPERTS
assert all((EC_CAPACITY_FACTOR * n // N_EXPERTS) % CHUNK == 0 for n in GROUP_TOKENS)
SLOT_TILES = SLOT_ROWS // FF_TM
assert SLOT_TILES * FF_TM == SLOT_ROWS
SLOT_CHUNKS = SLOT_ROWS // CHUNK
GROUP_TILES = 4
TILE_SLOTS = 8
LOOKAHEAD = TILE_SLOTS - GROUP_TILES
TRASH_CHUNK = N_EXPERTS * SLOT_CHUNKS
HEAD_TRASH_CHUNK = TRASH_CHUNK + TILE_SLOTS * CHUNKS_PER_TILE
TRASH_CHUNKS = TILE_SLOTS * CHUNKS_PER_TILE + FF_TM // CHUNK
assert N_EXPERTS <= FF_TM // CHUNK and TRASH_CHUNKS % (FF_TM // CHUNK) == 0
MAX_CHUNKS = (N_EXPERTS * (RB + 2 * CHUNK)) // CHUNK
LIST_UNROLL = 4
LIST_LEN = (MAX_CHUNKS // CHUNKS_PER_TILE + 2) * CHUNKS_PER_TILE
CHUNK_SHAPE = (2 * CHUNK, D_MODEL // 2)


def _pack_chunks(rows_f32):
    n = rows_f32.shape[0] // CHUNK
    r3 = rows_f32.reshape(n, CHUNK, D_MODEL)
    half = D_MODEL // 2
    return jnp.concatenate([r3[:, :, :half], r3[:, :, half:]], axis=1).astype(BF16)


def _unpack_chunks(chunks_bf16):
    n = chunks_bf16.shape[0]
    c3 = chunks_bf16.astype(F32)
    left = c3[:, :CHUNK, :].reshape(n * CHUNK, D_MODEL // 2)
    right = c3[:, CHUNK:, :].reshape(n * CHUNK, D_MODEL // 2)
    return jnp.concatenate([left, right], axis=1).astype(BF16)


def _thresh_kernel(aff_ref, thr_ref, need_ref):
    first = 0
    for g, n in enumerate(GROUP_TOKENS):
        cap = float(EC_CAPACITY_FACTOR * n // N_EXPERTS)
        aff = aff_ref[:, first:first + n]
        first += n

        def body(i, cur, aff=aff, cap=cap):
            cand = cur | jnp.left_shift(jnp.int32(1), 30 - i)
            cnt = jnp.sum(jnp.where(aff >= pltpu.bitcast(cand, F32), 1.0, 0.0), axis=1, keepdims=True)
            return jnp.where(cnt >= cap, cand, cur)

        thr = pltpu.bitcast(lax.fori_loop(0, 31, body, jnp.zeros((N_EXPERTS, 1), jnp.int32)), F32)
        n_gt = jnp.sum(jnp.where(aff > thr, 1.0, 0.0), axis=1, keepdims=True)
        rows = slice(g * N_EXPERTS, (g + 1) * N_EXPERTS)
        thr_ref[rows, :] = jnp.broadcast_to(thr, (N_EXPERTS, LANES))
        need_ref[rows, :] = jnp.broadcast_to(cap - n_gt, (N_EXPERTS, LANES))


def _thresholds(aff_t):
    n_groups = len(GROUP_TOKENS)
    return pl.pallas_call(
        _thresh_kernel,
        out_shape=[jax.ShapeDtypeStruct((n_groups * N_EXPERTS, LANES), F32),
                   jax.ShapeDtypeStruct((n_groups * N_EXPERTS, LANES), F32)],
        compiler_params=pltpu.CompilerParams(vmem_limit_bytes=VMEM_LIMIT),
    )(aff_t)


def _select_kernel(aff_ref, thr_ref, need_ref, utri_ref, selw_ref, rank_ref, m_ref, eq_seen):
    step = pl.program_id(0)

    @pl.when((step == 0) | (step == GROUP0_BLOCKS // SEL_BLOCKS))
    def _():
        eq_seen[...] = jnp.zeros_like(eq_seen)

    thr = thr_ref[:, 0:1]
    need = need_ref[:, 0:1]
    utri = utri_ref[...]
    for b in range(SEL_BLOCKS):
        cols = slice(b * RB, (b + 1) * RB)
        aff = aff_ref[:, cols]
        eq = jnp.where(aff == thr, 1.0, 0.0)
        eq_before = jnp.dot(eq.astype(BF16), utri, preferred_element_type=F32) + eq_seen[...]
        sel = (aff > thr) | ((aff == thr) & (eq_before < need))
        eq_seen[...] += jnp.sum(eq, axis=1, keepdims=True)
        self = jnp.where(sel, 1.0, 0.0)
        rank_ref[:, cols] = jnp.dot(self.astype(BF16), utri, preferred_element_type=F32)
        selw_ref[:, cols] = jnp.where(sel, aff, 0.0)
        m_ref[b] = jnp.broadcast_to(jnp.sum(self, axis=1, keepdims=True), (N_EXPERTS, LANES))


def _select(aff_t, thr, need):
    utri = jnp.asarray(np.triu(np.ones((RB, RB), np.float32), 1), dtype=BF16)
    grp = lambda i: (jnp.where(i >= GROUP0_BLOCKS // SEL_BLOCKS, 1, 0), 0)
    return pl.pallas_call(
        _select_kernel,
        grid=(N_RB // SEL_BLOCKS,),
        in_specs=[
            pl.BlockSpec((N_EXPERTS, SEL_BLOCKS * RB), lambda i: (0, i)),
            pl.BlockSpec((N_EXPERTS, LANES), grp),
            pl.BlockSpec((N_EXPERTS, LANES), grp),
            pl.BlockSpec((RB, RB), lambda i: (0, 0)),
        ],
        out_specs=[
            pl.BlockSpec((N_EXPERTS, SEL_BLOCKS * RB), lambda i: (0, i)),
            pl.BlockSpec((N_EXPERTS, SEL_BLOCKS * RB), lambda i: (0, i)),
            pl.BlockSpec((SEL_BLOCKS, N_EXPERTS, LANES), lambda i: (i, 0, 0)),
        ],
        out_shape=[
            jax.ShapeDtypeStruct((N_EXPERTS, N_TOK), F32),
            jax.ShapeDtypeStruct((N_EXPERTS, N_TOK), F32),
            jax.ShapeDtypeStruct((N_RB, N_EXPERTS, LANES), F32),
        ],
        scratch_shapes=[pltpu.VMEM((N_EXPERTS, 1), F32)],
        compiler_params=_cparams("arbitrary"),
    )(aff_t, thr, need, utri)


def _fill_chunk_list(list_ref, which, rb, seg_start_ref, seg_rows_ref, filler):
    cnt = jnp.int32(0)
    for e in range(N_EXPERTS):
        base = e * SLOT_CHUNKS + seg_start_ref[rb, e] // CHUNK
        n = seg_rows_ref[rb, e] // CHUNK

        def body(k, carry, base=base, cnt=cnt):
            for u in range(LIST_UNROLL):
                list_ref[which, cnt + LIST_UNROLL * k + u] = base + LIST_UNROLL * k + u
            return carry

        lax.fori_loop(0, (n + LIST_UNROLL - 1) // LIST_UNROLL, body, 0)
        cnt = cnt + n
    for j in range(CHUNKS_PER_TILE):
        list_ref[which, cnt + j] = filler(cnt + j)
    return cnt


WIN_ROWS = 128
SPAN_TILES = 8
SPAN_ROWS = SPAN_TILES * TILE_ROWS
assert SPAN_TILES % GROUP_TILES == 0


def _mark_span(scr, rb, s0, n_rows, tgt, values, seg_off_ref, seg_rows_ref):
    win_rows = lax.broadcasted_iota(jnp.int32, (WIN_ROWS, RB), 0).astype(F32)
    n_here = jnp.minimum(n_rows - s0, SPAN_ROWS)

    def zero(i, carry):
        scr[pl.ds(pl.multiple_of(i * TILE_ROWS, TILE_ROWS), TILE_ROWS), :] = jnp.zeros((TILE_ROWS, RB), F32)
        return carry

    lax.fori_loop(0, (n_here + TILE_ROWS - 1) // TILE_ROWS, zero, 0)
    for e in range(N_EXPERTS):
        lo = seg_off_ref[rb, e]
        first = jnp.maximum(lo, s0)
        last = jnp.minimum(lo + seg_rows_ref[rb, e], s0 + SPAN_ROWS)

        def mark(w, carry, e=e, first=first):
            start = first + w * WIN_ROWS
            win = pl.ds(pl.multiple_of(start - s0, CHUNK), WIN_ROWS)
            scr[win, :] = jnp.where(tgt[e:e + 1, :] == win_rows + start.astype(F32), values[e], scr[win, :])
            return carry

        lax.fori_loop(0, jnp.maximum(last - first + WIN_ROWS - 1, 0) // WIN_ROWS, mark, 0)


def _expert_column(ref, rb):
    e_iota = lax.broadcasted_iota(jnp.int32, (N_EXPERTS, 1), 0)
    col = jnp.zeros((N_EXPERTS, 1), F32)
    for e in range(N_EXPERTS):
        col = jnp.where(e_iota == e, ref[rb, e].astype(F32), col)
    return col


def _gather_kernel(in_start_ref, in_rows_ref, in_off_ref, tot_ref, in_rank_ref, seg_start_ref, seg_len_ref,
                   x_ref, selw_ref, rank_ref, xe_hbm,
                   p_scr, obuf, hbuf, carry, zbuf, dst_list, pending, sem, hsem, zsem):
    rb = pl.program_id(0)
    n_rows = tot_ref[rb]
    n_tiles = (n_rows + TILE_ROWS - 1) // TILE_ROWS
    head_slot = TILE_SLOTS

    @pl.when(rb == 0)
    def _():
        for slot in range(TILE_SLOTS + 1):
            pending[slot] = 0
        p_scr[...] = jnp.zeros_like(p_scr)
        carry[...] = jnp.zeros_like(carry)

    _fill_chunk_list(dst_list, 0, rb, in_start_ref, in_rows_ref,
                     lambda c: TRASH_CHUNK + c % (TILE_SLOTS * CHUNKS_PER_TILE))

    def tile_copy(slot):
        return pltpu.make_async_copy(obuf.at[slot], xe_hbm.at[pl.ds(0, CHUNKS_PER_TILE)], sem.at[slot])

    def head_copy():
        return pltpu.make_async_copy(hbuf, xe_hbm.at[pl.ds(0, N_EXPERTS)], hsem)

    x = x_ref[...]
    rank = rank_ref[...]
    sel = selw_ref[...] > 0.0

    j8 = lax.broadcasted_iota(jnp.int32, (CHUNK, RB), 0).astype(F32)
    heads, tails, flags = [], [], []
    for e in range(N_EXPERTS):
        start = seg_start_ref[rb, e]
        end = start + seg_len_ref[rb, e]
        nonempty = seg_len_ref[rb, e] > 0
        has_head = nonempty & (start % CHUNK != 0)
        has_tail = nonempty & (end % CHUNK != 0)
        single = has_head & has_tail & (start // CHUNK == end // CHUNK)
        head_shift = (start % CHUNK).astype(F32)
        tail_rank0 = ((end // CHUNK) * CHUNK - start).astype(F32)
        r_e = rank[e:e + 1, :]
        s_e = sel[e:e + 1, :]
        heads.append(jnp.where(s_e & (r_e + head_shift == j8), has_head.astype(F32), 0.0))
        tails.append(jnp.where(s_e & (r_e - tail_rank0 == j8), (has_tail & ~single).astype(F32), 0.0))
        flags.append((has_head, has_tail, single, start // CHUNK))
    p_edge = jnp.concatenate(heads + tails, axis=0).astype(BF16)
    edge = jnp.dot(p_edge, x, preferred_element_type=F32)

    @pl.when(pending[head_slot] == 1)
    def _():
        head_copy().wait()

    head_rows = []
    for e, (has_head, has_tail, single, head_chunk) in enumerate(flags):
        rows = slice(e * CHUNK, (e + 1) * CHUNK)
        old = carry[rows, :]
        head = edge[rows, :] + old
        tail = edge[N_EXPERTS * CHUNK + e * CHUNK:N_EXPERTS * CHUNK + (e + 1) * CHUNK, :]
        head_rows.append(head)
        keep = ~single & ~has_tail & ~has_head
        carry[rows, :] = (single.astype(F32) * head + (has_tail & ~single).astype(F32) * tail
                          + keep.astype(F32) * old)
    hbuf[...] = _pack_chunks(jnp.concatenate(head_rows, axis=0))
    for e, (has_head, has_tail, single, head_chunk) in enumerate(flags):
        dst = jnp.where(has_head & ~single, e * SLOT_CHUNKS + head_chunk, HEAD_TRASH_CHUNK + e)
        pltpu.make_async_copy(hbuf.at[e], xe_hbm.at[dst], hsem).start()
    pending[head_slot] = 1

    off = _expert_column(in_off_ref, rb)
    rank0 = _expert_column(in_rank_ref, rb)
    n_in = _expert_column(in_rows_ref, rb)
    tgt = jnp.where(sel & (rank >= rank0) & (rank < rank0 + n_in), rank - rank0 + off, -1.0)

    def free_slot(slot):
        @pl.when(pending[slot] == 1)
        def _():
            tile_copy(slot).wait()

    def emit(t, slot, rows_f32):
        obuf[slot] = _pack_chunks(rows_f32)
        for i in range(CHUNKS_PER_TILE):
            pltpu.make_async_copy(obuf.at[slot, i], xe_hbm.at[dst_list[0, t * CHUNKS_PER_TILE + i]],
                                  sem.at[slot]).start()
        pending[slot] = 1

    def group_body(g, carry_):
        t = GROUP_TILES * g
        cnt = jnp.minimum(n_tiles - t, GROUP_TILES)
        for k in range(GROUP_TILES):
            @pl.when(k < cnt)
            def _(k=k):
                free_slot((t + k) % TILE_SLOTS)

        @pl.when(t % SPAN_TILES == 0)
        def _():
            _mark_span(p_scr, rb, t * TILE_ROWS, n_rows, tgt, [1.0] * N_EXPERTS, in_off_ref, in_rows_ref)

        first_row = pl.multiple_of((t % SPAN_TILES) * TILE_ROWS, TILE_ROWS)
        for n in range(1, GROUP_TILES + 1):
            @pl.when(cnt == n)
            def _(n=n):
                p = p_scr[pl.ds(first_row, n * TILE_ROWS), :]
                rows = jnp.dot(p.astype(BF16), x, preferred_element_type=F32)
                for k in range(n):
                    emit(t + k, (t + k) % TILE_SLOTS, rows[k * TILE_ROWS:(k + 1) * TILE_ROWS])

        return carry_

    lax.fori_loop(0, (n_tiles + GROUP_TILES - 1) // GROUP_TILES, group_body, 0)

    @pl.when(rb == N_RB - 1)
    def _():
        for slot in range(TILE_SLOTS):
            @pl.when(pending[slot] == 1)
            def _(slot=slot):
                tile_copy(slot).wait()
        head_copy().wait()
        zbuf[...] = jnp.zeros_like(zbuf)

        def zcopy(k):
            return pltpu.make_async_copy(zbuf, xe_hbm.at[TRASH_CHUNK + k], zsem)

        def zstart(k, c):
            zcopy(k).start()
            return c

        def zwait(k, c):
            zcopy(k).wait()
            return c

        lax.fori_loop(0, TRASH_CHUNKS, zstart, 0)
        lax.fori_loop(0, TRASH_CHUNKS, zwait, 0)


def _route_gather(tables, xn, selw_t, rank_t):
    gs = pltpu.PrefetchScalarGridSpec(
        num_scalar_prefetch=len(tables),
        grid=(N_RB,),
        in_specs=[
            pl.BlockSpec((RB, D_MODEL), lambda rb, *_: (rb, 0)),
            pl.BlockSpec((N_EXPERTS, RB), lambda rb, *_: (0, rb)),
            pl.BlockSpec((N_EXPERTS, RB), lambda rb, *_: (0, rb)),
        ],
        out_specs=pl.BlockSpec(memory_space=pl.ANY),
        scratch_shapes=[
            pltpu.VMEM((SPAN_ROWS + WIN_ROWS, RB), F32),
            pltpu.VMEM((TILE_SLOTS, CHUNKS_PER_TILE) + CHUNK_SHAPE, BF16),
            pltpu.VMEM((N_EXPERTS,) + CHUNK_SHAPE, BF16),
            pltpu.VMEM((N_EXPERTS * CHUNK, D_MODEL), F32),
            pltpu.VMEM(CHUNK_SHAPE, BF16),
            pltpu.SMEM((1, LIST_LEN), jnp.int32),
            pltpu.SMEM((TILE_SLOTS + 1,), jnp.int32),
            pltpu.SemaphoreType.DMA((TILE_SLOTS,)),
            pltpu.SemaphoreType.DMA,
            pltpu.SemaphoreType.DMA,
        ],
    )
    return pl.pallas_call(
        _gather_kernel,
        grid_spec=gs,
        out_shape=jax.ShapeDtypeStruct((TRASH_CHUNK + TRASH_CHUNKS,) + CHUNK_SHAPE, BF16),
        compiler_params=_cparams("arbitrary"),
    )(*tables, xn, selw_t, rank_t)


W_PIECES_PER_MATRIX = 4
W_PIECES = 3 * W_PIECES_PER_MATRIX
FFN_STEPS = max(SLOT_TILES, W_PIECES + 1)
WG_PIECE_ROWS = D_MODEL // W_PIECES_PER_MATRIX
WD_PIECE_ROWS = EXPERT_D_FF // W_PIECES_PER_MATRIX
FFN_VMEM_LIMIT = 56 * 1024 * 1024


def _ffn_kernel(x_ref, wg_hbm, wu_hbm, wd_hbm, o_ref,
                wg_bf, wu_bf, wd_bf, stage_up, stage_dn, sem):
    e = pl.program_id(0)
    i = pl.program_id(1)
    slot = e % 2

    matrices = ((wg_hbm, stage_up, wg_bf, WG_PIECE_ROWS),
                (wu_hbm, stage_up, wu_bf, WG_PIECE_ROWS),
                (wd_hbm, stage_dn, wd_bf, WD_PIECE_ROWS))

    def piece(expert, k, buf, finish):
        def of_kind(kind):
            w_hbm, stage, w_bf, rows = matrices[kind]
            r0 = pl.multiple_of((k - kind * W_PIECES_PER_MATRIX) * rows, rows)
            cp = pltpu.make_async_copy(w_hbm.at[expert, pl.ds(r0, rows), :], stage.at[buf], sem.at[buf])
            if finish:
                cp.wait()
                w_bf[expert % 2, pl.ds(r0, rows), :] = stage[buf].astype(BF16)
            else:
                cp.start()

        if isinstance(k, int):
            of_kind(k // W_PIECES_PER_MATRIX)
        else:
            for kind in range(len(matrices)):
                pl.when(k // W_PIECES_PER_MATRIX == kind)(functools.partial(of_kind, kind))

    @pl.when((e == 0) & (i == 0))
    def _():
        for k in range(W_PIECES):
            piece(0, k, k % 2, finish=False)
            piece(0, k, k % 2, finish=True)

    @pl.when(e + 1 < N_EXPERTS)
    def _():
        @pl.when((i >= 1) & (i <= W_PIECES))
        def _():
            piece(e + 1, i - 1, (i - 1) % 2, finish=True)

        @pl.when(i < W_PIECES)
        def _():
            piece(e + 1, i, i % 2, finish=False)

    @pl.when(i < SLOT_TILES)
    def _():
        x = _unpack_chunks(x_ref[...])
        acc = None
        for c in range(EXPERT_D_FF // FF_FC):
            gt = jnp.dot(x, wg_bf[slot, :, c * FF_FC:(c + 1) * FF_FC], preferred_element_type=F32)
            up = jnp.dot(x, wu_bf[slot, :, c * FF_FC:(c + 1) * FF_FC], preferred_element_type=F32)
            h = (gt * (1.0 / (1.0 + jnp.exp(-gt))) * up).astype(BF16)
            down = jnp.dot(h, wd_bf[slot, c * FF_FC:(c + 1) * FF_FC, :], preferred_element_type=F32)
            acc = down if acc is None else acc + down
        o_ref[...] = _pack_chunks(acc)


def _ffn(xe, w_gate, w_up, w_down):
    def rows(e, i):
        return (e * SLOT_TILES + jnp.minimum(i, SLOT_TILES - 1), 0, 0)

    tile_chunks = (FF_TM // CHUNK,) + CHUNK_SHAPE
    gs = pltpu.PrefetchScalarGridSpec(
        num_scalar_prefetch=0,
        grid=(N_EXPERTS, FFN_STEPS),
        in_specs=[
            pl.BlockSpec(tile_chunks, rows),
            pl.BlockSpec(memory_space=pl.ANY),
            pl.BlockSpec(memory_space=pl.ANY),
            pl.BlockSpec(memory_space=pl.ANY),
        ],
        out_specs=pl.BlockSpec(tile_chunks, rows),
        scratch_shapes=[
            pltpu.VMEM((2, D_MODEL, EXPERT_D_FF), BF16),
            pltpu.VMEM((2, D_MODEL, EXPERT_D_FF), BF16),
            pltpu.VMEM((2, EXPERT_D_FF, D_MODEL), BF16),
            pltpu.VMEM((2, WG_PIECE_ROWS, EXPERT_D_FF), F32),
            pltpu.VMEM((2, WD_PIECE_ROWS, D_MODEL), F32),
            pltpu.SemaphoreType.DMA((2,)),
        ],
    )
    return pl.pallas_call(
        _ffn_kernel,
        grid_spec=gs,
        out_shape=jax.ShapeDtypeStruct((N_EXPERTS * SLOT_CHUNKS,) + CHUNK_SHAPE, BF16),
        compiler_params=pltpu.CompilerParams(dimension_semantics=("arbitrary", "arbitrary"),
                                             vmem_limit_bytes=FFN_VMEM_LIMIT),
    )(xe, w_gate, w_up, w_down)


def _combine_kernel(seg_start_ref, seg_rows_ref, seg_off_ref, tot_ref, seg_shift_ref,
                    x1_ref, selw_ref, rank_ref, g_ref, ye_hbm, op_ref, os_ref,
                    w_scr, ybuf, acc, src_list, state, sem):
    rb = pl.program_id(0)
    cur = rb % 2
    nxt_rb = jnp.minimum(rb + 1, N_RB - 1)

    def tiles_of(b):
        return (tot_ref[b] + TILE_ROWS - 1) // TILE_ROWS

    n_tiles = tiles_of(rb)
    n_tiles_next = jnp.where(rb + 1 < N_RB, tiles_of(nxt_rb), 0)

    def fetch(which, t, slot):
        for i in range(CHUNKS_PER_TILE):
            pltpu.make_async_copy(ye_hbm.at[src_list[which, t * CHUNKS_PER_TILE + i]], ybuf.at[slot, i],
                                  sem.at[slot]).start()

    def tile_wait(slot):
        pltpu.make_async_copy(ye_hbm.at[pl.ds(0, CHUNKS_PER_TILE)], ybuf.at[slot], sem.at[slot]).wait()

    @pl.when(rb == 0)
    def _():
        w_scr[...] = jnp.zeros_like(w_scr)
        _fill_chunk_list(src_list, 0, 0, seg_start_ref, seg_rows_ref, lambda c: 0)
        state[0] = 0
        state[1] = 0

    @pl.when(rb + 1 < N_RB)
    def _():
        _fill_chunk_list(src_list, 1 - cur, nxt_rb, seg_start_ref, seg_rows_ref, lambda c: 0)

    slot0 = state[0]
    started = state[1]
    for j in range(LOOKAHEAD):
        @pl.when((j >= started) & (j < n_tiles))
        def _(j=j):
            fetch(cur, j, (slot0 + j) % TILE_SLOTS)

    off = _expert_column(seg_off_ref, rb) + _expert_column(seg_shift_ref, rb)
    selw = selw_ref[...]
    gates = [selw[e:e + 1, :] for e in range(N_EXPERTS)]
    tgt = jnp.where(selw > 0.0, rank_ref[...] + off, -1.0)
    acc[...] = x1_ref[...]

    def prefetch(t, next_started):
        ahead = t + LOOKAHEAD

        @pl.when(ahead < n_tiles)
        def _():
            fetch(cur, ahead, (slot0 + ahead) % TILE_SLOTS)

        start_next = (ahead >= n_tiles) & (next_started < jnp.minimum(LOOKAHEAD, n_tiles_next))

        @pl.when(start_next)
        def _():
            fetch(1 - cur, next_started, (slot0 + n_tiles + next_started) % TILE_SLOTS)

        return next_started + start_next.astype(jnp.int32)

    def add_tiles(t, n):
        w = w_scr[pl.ds(pl.multiple_of((t % SPAN_TILES) * TILE_ROWS, TILE_ROWS), n * TILE_ROWS), :]
        ys = [_unpack_chunks(ybuf[(slot0 + t + k) % TILE_SLOTS]) for k in range(n)]
        y = ys[0] if n == 1 else jnp.concatenate(ys, axis=0)
        acc[...] += lax.dot_general(w.astype(BF16), y, (((0,), (0,)), ((), ())),
                                    preferred_element_type=F32)

    def group_body(g, next_started):
        t = GROUP_TILES * g
        cnt = jnp.minimum(n_tiles - t, GROUP_TILES)
        for k in range(GROUP_TILES):
            next_started = prefetch(t + k, next_started)

        @pl.when(t % SPAN_TILES == 0)
        def _():
            _mark_span(w_scr, rb, t * TILE_ROWS, tot_ref[rb], tgt, gates, seg_off_ref, seg_rows_ref)

        for k in range(GROUP_TILES):
            @pl.when(k < cnt)
            def _(k=k):
                tile_wait((slot0 + t + k) % TILE_SLOTS)

        for n in range(1, GROUP_TILES + 1):
            @pl.when(cnt == n)
            def _(n=n):
                add_tiles(t, n)

        return next_started

    state[1] = lax.fori_loop(0, (n_tiles + GROUP_TILES - 1) // GROUP_TILES, group_body, jnp.int32(0))
    state[0] = (slot0 + n_tiles) % TILE_SLOTS
    x2 = acc[...]
    ms = jnp.mean(x2 * x2, axis=-1, keepdims=True)
    y = x2 * lax.rsqrt(ms + EPS) * g_ref[...]

    @pl.when(rb < GROUP0_BLOCKS)
    def _():
        op_ref[...] = y

    @pl.when(rb >= GROUP0_BLOCKS)
    def _():
        os_ref[...] = y


def _combine(tables, x1, selw_t, rank_t, g, ye):
    gs = pltpu.PrefetchScalarGridSpec(
        num_scalar_prefetch=len(tables),
        grid=(N_RB,),
        in_specs=[
            pl.BlockSpec((RB, D_MODEL), lambda rb, *_: (rb, 0)),
            pl.BlockSpec((N_EXPERTS, RB), lambda rb, *_: (0, rb)),
            pl.BlockSpec((N_EXPERTS, RB), lambda rb, *_: (0, rb)),
            pl.BlockSpec((1, D_MODEL), lambda rb, *_: (0, 0)),
            pl.BlockSpec(memory_space=pl.ANY),
        ],
        out_specs=[
            pl.BlockSpec((RB, D_MODEL), lambda rb, *_: (jnp.minimum(rb, GROUP0_BLOCKS - 1), 0)),
            pl.BlockSpec((RB, D_MODEL), lambda rb, *_: (jnp.maximum(rb - GROUP0_BLOCKS, 0), 0)),
        ],
        scratch_shapes=[
            pltpu.VMEM((SPAN_ROWS + WIN_ROWS, RB), F32),
            pltpu.VMEM((TILE_SLOTS, CHUNKS_PER_TILE) + CHUNK_SHAPE, BF16),
            pltpu.VMEM((RB, D_MODEL), F32),
            pltpu.SMEM((2, LIST_LEN), jnp.int32),
            pltpu.SMEM((2,), jnp.int32),
            pltpu.SemaphoreType.DMA((TILE_SLOTS,)),
        ],
    )
    return pl.pallas_call(
        _combine_kernel,
        grid_spec=gs,
        out_shape=[jax.ShapeDtypeStruct((GROUP_TOKENS[0], D_MODEL), F32),
                   jax.ShapeDtypeStruct((GROUP_TOKENS[1], D_MODEL), F32)],
        compiler_params=_cparams("arbitrary"),
    )(*tables, x1, selw_t, rank_t, g, ye)


def kernel(x_prompt, x_sample, norm_mix_g, w_in, w_fourier, sink, w_out, norm_ffn_g, w_router,
           w_gate, w_up, w_down, norm_final_g):
    assert x_prompt.shape == (1, SEQ, D_MODEL) and x_sample.shape == (N_SEQ - 1, SEQ, D_MODEL)
    assert w_in.shape == (1, D_MODEL, IN_WIDTH) and w_out.shape == (1, MIX_WIDTH, D_MODEL)
    assert w_gate.shape == (1, N_EXPERTS, D_MODEL, EXPERT_D_FF) and w_down.shape == (1, N_EXPERTS, EXPERT_D_FF, D_MODEL)
    xp = x_prompt.reshape(-1, D_MODEL)
    xs = x_sample.reshape(-1, D_MODEL)
    cos_t, sin_t = _rope_tables()

    mix_cols = np.concatenate([
        np.arange(FOURIER_WIDTH),
        FOURIER_WIDTH + (np.asarray(Q_ORDER)[:, None] * HEAD_DIM + np.arange(HEAD_DIM)[None, :]).reshape(-1)])
    in_cols = np.concatenate([mix_cols, np.arange(MIX_WIDTH, IN_WIDTH)])
    w_in_r = w_in[0][:, in_cols].astype(BF16)
    w_out_r = w_out[0][mix_cols, :].astype(BF16)

    u, q, k, v = _inproj(xp, xs, norm_mix_g[0][None, :], w_in_r, cos_t, sin_t)
    yf = _fourier(u, w_fourier[0].astype(BF16))
    ya = _attention(sink[0], q, k, v)
    x1, xn, aff_t = _outproj(xp, xs, yf, ya, w_out_r, norm_ffn_g[0][None, :],
                             w_router[0].T.astype(BF16))

    thr, need = _thresholds(aff_t)
    selw_t, rank_t, m = _select(aff_t, thr, need)

    seg_len = m[:, :, 0].astype(jnp.int32)
    seg_start = jnp.cumsum(seg_len, axis=0) - seg_len
    seg_end = seg_start + seg_len

    def offsets(rows):
        return jnp.cumsum(rows, axis=1) - rows, jnp.sum(rows, axis=1)

    in_start = -(-seg_start // CHUNK) * CHUNK
    in_rows = jnp.maximum(seg_end // CHUNK * CHUNK - in_start, 0)
    in_off, in_tot = offsets(in_rows)
    gather_tables = (in_start, in_rows, in_off, in_tot, in_start - seg_start, seg_start, seg_len)
    out_start = seg_start // CHUNK * CHUNK
    out_rows = jnp.where(seg_len > 0, -(-seg_end // CHUNK) * CHUNK - out_start, 0)
    out_off, out_tot = offsets(out_rows)
    combine_tables = (out_start, out_rows, out_off, out_tot, seg_start - out_start)

    xe = _route_gather(gather_tables, xn, selw_t, rank_t)
    ye = _ffn(xe, w_gate[0], w_up[0], w_down[0])
    yp, ys = _combine(combine_tables, x1, selw_t, rank_t, norm_final_g[None, :], ye)
    return (yp.reshape(x_prompt.shape), ys.reshape(x_sample.shape))
```

```python
import functools

import numpy as np
import jax
import jax.numpy as jnp
from jax import lax
from jax.experimental import pallas as pl
from jax.experimental.pallas import tpu as pltpu

F32 = jnp.float32
BF16 = jnp.bfloat16

D_MODEL = 1024
SEQ = 16384
N_SEQ = 3
N_TOK = N_SEQ * SEQ
HEAD_DIM = 64
N_HEADS = 12
N_KV_HEADS = 4
GQA_GROUP = N_HEADS // N_KV_HEADS
ATTN_WIDTH = N_HEADS * HEAD_DIM
KV_WIDTH = N_KV_HEADS * HEAD_DIM
FOURIER_WIDTH = 256
FOURIER_GROUP_DIM = 64
MIX_WIDTH = FOURIER_WIDTH + ATTN_WIDTH
IN_WIDTH = MIX_WIDTH + 2 * KV_WIDTH
WINDOW = 128
ROPE_THETA = 10000.0
N_EXPERTS = 16
EC_CAPACITY_FACTOR = 2
EXPERT_D_FF = 2048
EPS = 1e-6
NEG = -1e30

LANES = 128
FFT_R = 128
VMEM_LIMIT = 48 * 1024 * 1024

TM = 1024
TQ = 1024
QB = 128
FF_TM = 512
FF_FC = 512


def _cparams(*sem):
    return pltpu.CompilerParams(dimension_semantics=sem, vmem_limit_bytes=VMEM_LIMIT)


def _x_specs():
    n_p = SEQ // TM
    return [pl.BlockSpec((TM, D_MODEL), lambda i, *_: (jnp.minimum(i, n_p - 1), 0)),
            pl.BlockSpec((TM, D_MODEL), lambda i, *_: (jnp.maximum(i - n_p, 0), 0))]


def _on_x_tile(body, xp_ref, xs_ref, *refs):
    is_prompt = pl.program_id(0) < SEQ // TM
    pl.when(is_prompt)(lambda: body(xp_ref, *refs))
    pl.when(jnp.logical_not(is_prompt))(lambda: body(xs_ref, *refs))


def _inproj_kernel(*refs):
    _on_x_tile(_inproj_body, *refs)


def _inproj_body(x_ref, g_ref, w_ref, cos_ref, sin_ref, u_ref, q_ref, k_ref, v_ref):
    x = x_ref[...]
    ms = jnp.mean(x * x, axis=-1, keepdims=True)
    hn = (x * lax.rsqrt(ms + EPS) * g_ref[...]).astype(BF16)
    proj = jnp.dot(hn, w_ref[...], preferred_element_type=F32)
    u_ref[...] = proj[:, :FOURIER_WIDTH].astype(BF16)
    cos = cos_ref[...]
    sin = sin_ref[...]
    lane = lax.broadcasted_iota(jnp.int32, cos.shape, 1)
    first_half = (lane % HEAD_DIM) < (HEAD_DIM // 2)

    def rope(t):
        partner = jnp.where(first_half,
                            pltpu.roll(t, LANES - HEAD_DIM // 2, 1),
                            pltpu.roll(t, HEAD_DIM // 2, 1))
        return t * cos + partner * sin

    scale = HEAD_DIM ** -0.5 * LOG2E
    for c in range(ATTN_WIDTH // LANES):
        lo = FOURIER_WIDTH + c * LANES
        q_ref[:, c * LANES:(c + 1) * LANES] = (rope(proj[:, lo:lo + LANES]) * scale).astype(BF16)
    low_half = lane < HEAD_DIM
    for c in range(KV_WIDTH // LANES):
        lo = FOURIER_WIDTH + ATTN_WIDTH + c * LANES
        kc = rope(proj[:, lo:lo + LANES])
        vc = proj[:, lo + KV_WIDTH:lo + KV_WIDTH + LANES]
        for half, keep in enumerate((low_half, ~low_half)):
            g = 2 * c + half
            k_ref[:, g * LANES:(g + 1) * LANES] = jnp.where(keep, kc, 0.0).astype(BF16)
            v_ref[:, g * LANES:(g + 1) * LANES] = jnp.where(keep, vc, 0.0).astype(BF16)


def _inproj(xp, xs, g, w_in, cos_t, sin_t):
    n_pos_blocks = SEQ // TM
    return pl.pallas_call(
        _inproj_kernel,
        grid=(N_TOK // TM,),
        in_specs=_x_specs() + [
            pl.BlockSpec((1, D_MODEL), lambda i: (0, 0)),
            pl.BlockSpec((D_MODEL, IN_WIDTH), lambda i: (0, 0)),
            pl.BlockSpec((TM, LANES), lambda i: (i % n_pos_blocks, 0)),
            pl.BlockSpec((TM, LANES), lambda i: (i % n_pos_blocks, 0)),
        ],
        out_specs=[
            pl.BlockSpec((TM, FOURIER_WIDTH), lambda i: (i, 0)),
            pl.BlockSpec((TM, ATTN_WIDTH), lambda i: (i, 0)),
            pl.BlockSpec((TM, KV4), lambda i: (i, 0)),
            pl.BlockSpec((TM, KV4), lambda i: (i, 0)),
        ],
        out_shape=[
            jax.ShapeDtypeStruct((N_TOK, FOURIER_WIDTH), BF16),
            jax.ShapeDtypeStruct((N_TOK, ATTN_WIDTH), BF16),
            jax.ShapeDtypeStruct((N_TOK, KV4), BF16),
            jax.ShapeDtypeStruct((N_TOK, KV4), BF16),
        ],
        compiler_params=_cparams("parallel"),
    )(xp, xs, g, w_in, cos_t, sin_t)


def _rope_tables():
    half = HEAD_DIM // 2
    inv_freq = 1.0 / (ROPE_THETA ** (jnp.arange(half, dtype=F32) / half))
    ang = jnp.arange(SEQ, dtype=F32)[:, None] * inv_freq[None, :]
    cos = jnp.cos(ang)
    sin = jnp.sin(ang)
    reps = LANES // HEAD_DIM
    cos_t = jnp.tile(jnp.concatenate([cos, cos], axis=-1), (1, reps))
    sin_t = jnp.tile(jnp.concatenate([-sin, sin], axis=-1), (1, reps))
    return cos_t, sin_t


def _dft_tables():
    r = FFT_R
    c = np.arange(FOURIER_GROUP_DIM)
    ang_c = 2.0 * np.pi * np.outer(c, c) / FOURIER_GROUP_DIM
    n_groups = FOURIER_WIDTH // FOURIER_GROUP_DIM
    eye = np.eye(n_groups)
    ch_scale = FOURIER_GROUP_DIM ** -0.5
    c_blk = np.kron(eye, np.cos(ang_c)) * ch_scale
    s_blk = np.kron(eye, np.sin(ang_c)) * ch_scale
    cs = np.concatenate([c_blk, -s_blk], axis=1)
    k = np.arange(r)
    ang_r = 2.0 * np.pi * np.outer(k, k) / r
    st_scale = r ** -0.5
    wr = np.cos(ang_r) * st_scale
    wi = -np.sin(ang_r) * st_scale
    wbig = np.block([[wr, -wi], [wi, wr]])
    ang_t = 2.0 * np.pi * np.outer(k, k) / (r * r)
    tr = np.cos(ang_t)
    ti = -np.sin(ang_t)
    f = lambda a: jnp.asarray(a, dtype=F32)
    return f(cs).astype(BF16), f(wbig).astype(BF16), f(wr), f(wi), f(tr), f(ti)


FS_SB = 32
FS_KB = 32


def _fft_stage1_kernel(u_ref, cs_ref, wbig_ref, y_ref):
    cs = cs_ref[...]
    wbig = wbig_ref[...]
    fw = FOURIER_WIDTH
    for j in range(FS_SB):
        u = u_ref[:, j * fw:(j + 1) * fw]
        g = jnp.dot(u, cs, preferred_element_type=F32)
        gst = jnp.concatenate([g[:, :fw], g[:, fw:]], axis=0).astype(BF16)
        y = jnp.dot(wbig, gst, preferred_element_type=F32)
        y_ref[:, j * 2 * fw:j * 2 * fw + fw] = y[:FFT_R].astype(BF16)
        y_ref[:, j * 2 * fw + fw:(j + 1) * 2 * fw] = y[FFT_R:].astype(BF16)


def _fft_stage2_kernel(y_ref, wr_ref, wi_ref, tr_ref, ti_ref, wf_ref, o_ref):
    kb = pl.program_id(1)
    wr = wr_ref[...]
    wi = wi_ref[...]
    wf = wf_ref[...]
    fw = FOURIER_WIDTH
    for j in range(FS_KB):
        k1 = kb * FS_KB + j
        tr = tr_ref[pl.ds(k1, 1), :]
        ti = ti_ref[pl.ds(k1, 1), :]
        mr = wr * tr - wi * ti
        mi = wr * ti + wi * tr
        m2 = jnp.concatenate([mr, -mi], axis=1).astype(BF16)
        yb = y_ref[j * FFT_R:(j + 1) * FFT_R, :]
        yst = jnp.concatenate([yb[:, :fw], yb[:, fw:]], axis=0)
        z = jnp.dot(m2, yst, preferred_element_type=F32)
        yf = jnp.dot(z.astype(BF16), wf, preferred_element_type=F32)
        o_ref[:, j * fw:(j + 1) * fw] = yf.astype(BF16)


def _fourier(u, w_fourier):
    cs, wbig, wr, wi, tr, ti = _dft_tables()
    r, fw = FFT_R, FOURIER_WIDTH
    u2 = u.reshape(N_SEQ * r, r * fw)
    y = pl.pallas_call(
        _fft_stage1_kernel,
        grid=(N_SEQ, r // FS_SB),
        in_specs=[
            pl.BlockSpec((r, FS_SB * fw), lambda s, j: (s, j)),
            pl.BlockSpec((fw, 2 * fw), lambda s, j: (0, 0)),
            pl.BlockSpec((2 * r, 2 * r), lambda s, j: (0, 0)),
        ],
        out_specs=pl.BlockSpec((r, FS_SB * 2 * fw), lambda s, j: (s, j)),
        out_shape=jax.ShapeDtypeStruct((N_SEQ * r, r * 2 * fw), BF16),
        compiler_params=_cparams("parallel", "parallel"),
    )(u2, cs, wbig)
    y2 = y.reshape(N_SEQ * r * r, 2 * fw)
    full = lambda s, j: (0, 0)
    yf = pl.pallas_call(
        _fft_stage2_kernel,
        grid=(N_SEQ, r // FS_KB),
        in_specs=[
            pl.BlockSpec((FS_KB * r, 2 * fw), lambda s, j: (s * (r // FS_KB) + j, 0)),
            pl.BlockSpec((r, r), full),
            pl.BlockSpec((r, r), full),
            pl.BlockSpec((r, r), full),
            pl.BlockSpec((r, r), full),
            pl.BlockSpec((fw, fw), full),
        ],
        out_specs=pl.BlockSpec((r, FS_KB * fw), lambda s, j: (s, j)),
        out_shape=jax.ShapeDtypeStruct((N_SEQ * r, r * fw), BF16),
        compiler_params=_cparams("parallel", "parallel"),
    )(y2, wr, wi, tr, ti, w_fourier)
    return yf.reshape(N_TOK, fw)


LO_HEADS = (0, 1, 2, 6, 7, 8)
HI_HEADS = (3, 4, 5, 9, 10, 11)
Q_ORDER = tuple(h for pair in zip(LO_HEADS, HI_HEADS) for h in pair)
assert all(h // GQA_GROUP % 2 == 0 for h in LO_HEADS) and all(h // GQA_GROUP % 2 == 1 for h in HI_HEADS)
KV4 = 2 * KV_WIDTH
LOG2E = 1.4426950408889634


def _attn_kernel(sink_ref, q_ref, kp_ref, kc_ref, kn_ref, vp_ref, vc_ref, vn_ref, o_ref,
                 kext, vext):
    i = pl.program_id(1)
    n_i = pl.num_programs(1)
    kext[0:QB, :] = kp_ref[...]
    kext[QB:QB + TQ, :] = kc_ref[...]
    kext[QB + TQ:, :] = kn_ref[...]
    vext[0:QB, :] = vp_ref[...]
    vext[QB:QB + TQ, :] = vc_ref[...]
    vext[QB + TQ:, :] = vn_ref[...]

    rows = GQA_GROUP * QB
    a = lax.broadcasted_iota(jnp.int32, (rows, QB), 0) % QB
    j = lax.broadcasted_iota(jnp.int32, (rows, QB), 1)
    band_prev = jnp.where(j >= a, 0.0, NEG)
    band_next = jnp.where(j <= a, 0.0, NEG)
    row_id = lax.broadcasted_iota(jnp.int32, (rows, 1), 0)
    low_half = lax.broadcasted_iota(jnp.int32, (rows, LANES), 1) < HEAD_DIM

    def softmax_half(s, sinks, bias_prev, bias_next):
        s_p = s[:, :QB] + bias_prev
        s_c = s[:, QB:2 * QB]
        s_n = s[:, 2 * QB:] + bias_next
        sink = jnp.where(row_id < QB, sinks[0], jnp.where(row_id < 2 * QB, sinks[1], sinks[2])) * LOG2E
        m = jnp.max(jnp.maximum(jnp.maximum(s_p, s_c), s_n), axis=-1, keepdims=True)
        m = jnp.maximum(m, sink)
        p_p = jnp.exp2(s_p - m)
        p_c = jnp.exp2(s_c - m)
        p_n = jnp.exp2(s_n - m)
        denom = jnp.sum(p_p + p_c + p_n, axis=-1, keepdims=True) + jnp.exp2(sink - m)
        return [p_p.astype(BF16), p_c.astype(BF16), p_n.astype(BF16)], 1.0 / denom

    for sb in range(TQ // QB):
        blk = i * (TQ // QB) + sb
        bias_prev = band_prev + jnp.where(blk == 0, NEG, 0.0)
        bias_next = band_next + jnp.where(blk == n_i * (TQ // QB) - 1, NEG, 0.0)
        for pr in range(N_KV_HEADS // 2):
            groups = [GQA_GROUP * pr + m for m in range(GQA_GROUP)]
            q3 = jnp.concatenate([q_ref[sb * QB:(sb + 1) * QB, g * LANES:(g + 1) * LANES] for g in groups],
                                 axis=0)
            kp = kext[sb * QB:sb * QB + 3 * QB, pr * 2 * LANES:(pr + 1) * 2 * LANES]
            vp = vext[sb * QB:sb * QB + 3 * QB, pr * 2 * LANES:(pr + 1) * 2 * LANES]
            kcat = jnp.concatenate([kp[:, :LANES], kp[:, LANES:]], axis=0)
            vcat = jnp.concatenate([vp[:, :LANES], vp[:, LANES:]], axis=0)
            s = lax.dot_general(q3, kcat, (((1,), (1,)), ((), ())), preferred_element_type=F32)
            p_lo, r_lo = softmax_half(s[:, :3 * QB], [sink_ref[LO_HEADS[g]] for g in groups],
                                      bias_prev, bias_next)
            p_hi, r_hi = softmax_half(s[:, 3 * QB:], [sink_ref[HI_HEADS[g]] for g in groups],
                                      bias_prev, bias_next)
            p = jnp.concatenate(p_lo + p_hi, axis=1)
            o = jnp.dot(p, vcat, preferred_element_type=F32) * jnp.where(low_half, r_lo, r_hi)
            for m, g in enumerate(groups):
                o_ref[sb * QB:(sb + 1) * QB, g * LANES:(g + 1) * LANES] = o[m * QB:(m + 1) * QB].astype(BF16)


def _attention(sink, q, k, v):
    nqb = SEQ // QB
    per = TQ // QB
    cur = lambda s, i: (s * (SEQ // TQ) + i, 0)
    prev = lambda s, i: (s * nqb + jnp.maximum(i * per - 1, 0), 0)
    nxt = lambda s, i: (s * nqb + jnp.minimum(i * per + per, nqb - 1), 0)
    return pl.pallas_call(
        _attn_kernel,
        grid=(N_SEQ, SEQ // TQ),
        in_specs=[
            pl.BlockSpec(memory_space=pltpu.SMEM),
            pl.BlockSpec((TQ, ATTN_WIDTH), cur),
            pl.BlockSpec((QB, KV4), prev),
            pl.BlockSpec((TQ, KV4), cur),
            pl.BlockSpec((QB, KV4), nxt),
            pl.BlockSpec((QB, KV4), prev),
            pl.BlockSpec((TQ, KV4), cur),
            pl.BlockSpec((QB, KV4), nxt),
        ],
        out_specs=pl.BlockSpec((TQ, ATTN_WIDTH), cur),
        out_shape=jax.ShapeDtypeStruct((N_TOK, ATTN_WIDTH), BF16),
        scratch_shapes=[pltpu.VMEM((TQ + 2 * QB, KV4), BF16),
                        pltpu.VMEM((TQ + 2 * QB, KV4), BF16)],
        compiler_params=_cparams("parallel", "parallel"),
    )(sink, q, k, k, k, v, v, v)


def _outproj_kernel(*refs):
    _on_x_tile(_outproj_body, *refs)


def _outproj_body(x_ref, yf_ref, ya_ref, wo_ref, g_ref, wr_ref, x1_ref, xn_ref, aff_ref):
    mix = jnp.dot(yf_ref[...], wo_ref[:FOURIER_WIDTH, :], preferred_element_type=F32)
    mix = mix + jnp.dot(ya_ref[...], wo_ref[FOURIER_WIDTH:, :], preferred_element_type=F32)
    x1 = x_ref[...] + mix
    x1_ref[...] = x1
    ms = jnp.mean(x1 * x1, axis=-1, keepdims=True)
    xn = (x1 * lax.rsqrt(ms + EPS) * g_ref[...]).astype(BF16)
    xn_ref[...] = xn
    logits = lax.dot_general(wr_ref[...], xn, (((1,), (1,)), ((), ())), preferred_element_type=F32)
    mx = jnp.max(logits, axis=0, keepdims=True)
    ex = jnp.exp(logits - mx)
    aff_ref[...] = ex / jnp.sum(ex, axis=0, keepdims=True)


def _outproj(xp, xs, yf, ya, w_out, g, w_router_t):
    return pl.pallas_call(
        _outproj_kernel,
        grid=(N_TOK // TM,),
        in_specs=_x_specs() + [
            pl.BlockSpec((TM, FOURIER_WIDTH), lambda i: (i, 0)),
            pl.BlockSpec((TM, ATTN_WIDTH), lambda i: (i, 0)),
            pl.BlockSpec((D_MODEL, D_MODEL), lambda i: (0, 0)),
            pl.BlockSpec((1, D_MODEL), lambda i: (0, 0)),
            pl.BlockSpec((N_EXPERTS, D_MODEL), lambda i: (0, 0)),
        ],
        out_specs=[
            pl.BlockSpec((TM, D_MODEL), lambda i: (i, 0)),
            pl.BlockSpec((TM, D_MODEL), lambda i: (i, 0)),
            pl.BlockSpec((N_EXPERTS, TM), lambda i: (0, i)),
        ],
        out_shape=[
            jax.ShapeDtypeStruct((N_TOK, D_MODEL), F32),
            jax.ShapeDtypeStruct((N_TOK, D_MODEL), BF16),
            jax.ShapeDtypeStruct((N_EXPERTS, N_TOK), F32),
        ],
        compiler_params=_cparams("parallel"),
    )(xp, xs, yf, ya, w_out, g, w_router_t)


RB = 512
N_RB = N_TOK // RB
GROUP_TOKENS = (SEQ, 2 * SEQ)
GROUP0_BLOCKS = GROUP_TOKENS[0] // RB
SEL_BLOCKS = 8
assert GROUP0_BLOCKS % SEL_BLOCKS == 0 and N_RB % SEL_BLOCKS == 0
CHUNK = 8
TILE_ROWS = 256
CHUNKS_PER_TILE = TILE_ROWS // CHUNK
SLOT_ROWS = EC_CAPACITY_FACTOR * N_TOK // N_EXPERTS
assert all((EC_CAPACITY_FACTOR * n // N_EXPERTS) % CHUNK == 0 for n in GROUP_TOKENS)
SLOT_TILES = SLOT_ROWS // FF_TM
assert SLOT_TILES * FF_TM == SLOT_ROWS
SLOT_CHUNKS = SLOT_ROWS // CHUNK
GROUP_TILES = 4
TILE_SLOTS = 8
LOOKAHEAD = TILE_SLOTS - GROUP_TILES
TRASH_CHUNK = N_EXPERTS * SLOT_CHUNKS
HEAD_TRASH_CHUNK = TRASH_CHUNK + TILE_SLOTS * CHUNKS_PER_TILE
TRASH_CHUNKS = TILE_SLOTS * CHUNKS_PER_TILE + FF_TM // CHUNK
assert N_EXPERTS <= FF_TM // CHUNK and TRASH_CHUNKS % (FF_TM // CHUNK) == 0
MAX_CHUNKS = (N_EXPERTS * (RB + 2 * CHUNK)) // CHUNK
LIST_UNROLL = 4
LIST_LEN = (MAX_CHUNKS // CHUNKS_PER_TILE + 2) * CHUNKS_PER_TILE
CHUNK_SHAPE = (2 * CHUNK, D_MODEL // 2)


def _pack_chunks(rows_f32):
    n = rows_f32.shape[0] // CHUNK
    r3 = rows_f32.reshape(n, CHUNK, D_MODEL)
    half = D_MODEL // 2
    return jnp.concatenate([r3[:, :, :half], r3[:, :, half:]], axis=1).astype(BF16)


def _unpack_chunks(chunks_bf16):
    n = chunks_bf16.shape[0]
    c3 = chunks_bf16.astype(F32)
    left = c3[:, :CHUNK, :].reshape(n * CHUNK, D_MODEL // 2)
    right = c3[:, CHUNK:, :].reshape(n * CHUNK, D_MODEL // 2)
    return jnp.concatenate([left, right], axis=1).astype(BF16)


def _thresh_kernel(aff_ref, thr_ref, need_ref):
    first = 0
    for g, n in enumerate(GROUP_TOKENS):
        cap = float(EC_CAPACITY_FACTOR * n // N_EXPERTS)
        aff = aff_ref[:, first:first + n]
        first += n

        def body(i, cur, aff=aff, cap=cap):
            cand = cur | jnp.left_shift(jnp.int32(1), 30 - i)
            cnt = jnp.sum(jnp.where(aff >= pltpu.bitcast(cand, F32), 1.0, 0.0), axis=1, keepdims=True)
            return jnp.where(cnt >= cap, cand, cur)

        thr = pltpu.bitcast(lax.fori_loop(0, 31, body, jnp.zeros((N_EXPERTS, 1), jnp.int32)), F32)
        n_gt = jnp.sum(jnp.where(aff > thr, 1.0, 0.0), axis=1, keepdims=True)
        rows = slice(g * N_EXPERTS, (g + 1) * N_EXPERTS)
        thr_ref[rows, :] = jnp.broadcast_to(thr, (N_EXPERTS, LANES))
        need_ref[rows, :] = jnp.broadcast_to(cap - n_gt, (N_EXPERTS, LANES))


def _thresholds(aff_t):
    n_groups = len(GROUP_TOKENS)
    return pl.pallas_call(
        _thresh_kernel,
        out_shape=[jax.ShapeDtypeStruct((n_groups * N_EXPERTS, LANES), F32),
                   jax.ShapeDtypeStruct((n_groups * N_EXPERTS, LANES), F32)],
        compiler_params=pltpu.CompilerParams(vmem_limit_bytes=VMEM_LIMIT),
    )(aff_t)


def _select_kernel(aff_ref, thr_ref, need_ref, utri_ref, selw_ref, rank_ref, m_ref, eq_seen):
    step = pl.program_id(0)

    @pl.when((step == 0) | (step == GROUP0_BLOCKS // SEL_BLOCKS))
    def _():
        eq_seen[...] = jnp.zeros_like(eq_seen)

    thr = thr_ref[:, 0:1]
    need = need_ref[:, 0:1]
    utri = utri_ref[...]
    for b in range(SEL_BLOCKS):
        cols = slice(b * RB, (b + 1) * RB)
        aff = aff_ref[:, cols]
        eq = jnp.where(aff == thr, 1.0, 0.0)
        eq_before = jnp.dot(eq.astype(BF16), utri, preferred_element_type=F32) + eq_seen[...]
        sel = (aff > thr) | ((aff == thr) & (eq_before < need))
        eq_seen[...] += jnp.sum(eq, axis=1, keepdims=True)
        self = jnp.where(sel, 1.0, 0.0)
        rank_ref[:, cols] = jnp.dot(self.astype(BF16), utri, preferred_element_type=F32)
        selw_ref[:, cols] = jnp.where(sel, aff, 0.0)
        m_ref[b] = jnp.broadcast_to(jnp.sum(self, axis=1, keepdims=True), (N_EXPERTS, LANES))


def _select(aff_t, thr, need):
    utri = jnp.asarray(np.triu(np.ones((RB, RB), np.float32), 1), dtype=BF16)
    grp = lambda i: (jnp.where(i >= GROUP0_BLOCKS // SEL_BLOCKS, 1, 0), 0)
    return pl.pallas_call(
        _select_kernel,
        grid=(N_RB // SEL_BLOCKS,),
        in_specs=[
            pl.BlockSpec((N_EXPERTS, SEL_BLOCKS * RB), lambda i: (0, i)),
            pl.BlockSpec((N_EXPERTS, LANES), grp),
            pl.BlockSpec((N_EXPERTS, LANES), grp),
            pl.BlockSpec((RB, RB), lambda i: (0, 0)),
        ],
        out_specs=[
            pl.BlockSpec((N_EXPERTS, SEL_BLOCKS * RB), lambda i: (0, i)),
            pl.BlockSpec((N_EXPERTS, SEL_BLOCKS * RB), lambda i: (0, i)),
            pl.BlockSpec((SEL_BLOCKS, N_EXPERTS, LANES), lambda i: (i, 0, 0)),
        ],
        out_shape=[
            jax.ShapeDtypeStruct((N_EXPERTS, N_TOK), F32),
            jax.ShapeDtypeStruct((N_EXPERTS, N_TOK), F32),
            jax.ShapeDtypeStruct((N_RB, N_EXPERTS, LANES), F32),
        ],
        scratch_shapes=[pltpu.VMEM((N_EXPERTS, 1), F32)],
        compiler_params=_cparams("arbitrary"),
    )(aff_t, thr, need, utri)


def _fill_chunk_list(list_ref, which, rb, seg_start_ref, seg_rows_ref, filler):
    cnt = jnp.int32(0)
    for e in range(N_EXPERTS):
        base = e * SLOT_CHUNKS + seg_start_ref[rb, e] // CHUNK
        n = seg_rows_ref[rb, e] // CHUNK

        def body(k, carry, base=base, cnt=cnt):
            for u in range(LIST_UNROLL):
                list_ref[which, cnt + LIST_UNROLL * k + u] = base + LIST_UNROLL * k + u
            return carry

        lax.fori_loop(0, (n + LIST_UNROLL - 1) // LIST_UNROLL, body, 0)
        cnt = cnt + n
    for j in range(CHUNKS_PER_TILE):
        list_ref[which, cnt + j] = filler(cnt + j)
    return cnt


WIN_ROWS = 128
SPAN_TILES = 8
SPAN_ROWS = SPAN_TILES * TILE_ROWS
assert SPAN_TILES % GROUP_TILES == 0


def _mark_span(scr, rb, s0, n_rows, tgt, values, seg_off_ref, seg_rows_ref):
    win_rows = lax.broadcasted_iota(jnp.int32, (WIN_ROWS, RB), 0).astype(F32)
    n_here = jnp.minimum(n_rows - s0, SPAN_ROWS)

    def zero(i, carry):
        scr[pl.ds(pl.multiple_of(i * TILE_ROWS, TILE_ROWS), TILE_ROWS), :] = jnp.zeros((TILE_ROWS, RB), F32)
        return carry

    lax.fori_loop(0, (n_here + TILE_ROWS - 1) // TILE_ROWS, zero, 0)
    for e in range(N_EXPERTS):
        lo = seg_off_ref[rb, e]
        first = jnp.maximum(lo, s0)
        last = jnp.minimum(lo + seg_rows_ref[rb, e], s0 + SPAN_ROWS)

        def mark(w, carry, e=e, first=first):
            start = first + w * WIN_ROWS
            win = pl.ds(pl.multiple_of(start - s0, CHUNK), WIN_ROWS)
            scr[win, :] = jnp.where(tgt[e:e + 1, :] == win_rows + start.astype(F32), values[e], scr[win, :])
            return carry

        lax.fori_loop(0, jnp.maximum(last - first + WIN_ROWS - 1, 0) // WIN_ROWS, mark, 0)


def _expert_column(ref, rb):
    e_iota = lax.broadcasted_iota(jnp.int32, (N_EXPERTS, 1), 0)
    col = jnp.zeros((N_EXPERTS, 1), F32)
    for e in range(N_EXPERTS):
        col = jnp.where(e_iota == e, ref[rb, e].astype(F32), col)
    return col


def _gather_kernel(in_start_ref, in_rows_ref, in_off_ref, tot_ref, in_rank_ref, seg_start_ref, seg_len_ref,
                   x_ref, selw_ref, rank_ref, xe_hbm,
                   p_scr, obuf, hbuf, carry, zbuf, dst_list, pending, sem, hsem, zsem):
    rb = pl.program_id(0)
    n_rows = tot_ref[rb]
    n_tiles = (n_rows + TILE_ROWS - 1) // TILE_ROWS
    head_slot = TILE_SLOTS

    @pl.when(rb == 0)
    def _():
        for slot in range(TILE_SLOTS + 1):
            pending[slot] = 0
        p_scr[...] = jnp.zeros_like(p_scr)
        carry[...] = jnp.zeros_like(carry)

    _fill_chunk_list(dst_list, 0, rb, in_start_ref, in_rows_ref,
                     lambda c: TRASH_CHUNK + c % (TILE_SLOTS * CHUNKS_PER_TILE))

    def tile_copy(slot):
        return pltpu.make_async_copy(obuf.at[slot], xe_hbm.at[pl.ds(0, CHUNKS_PER_TILE)], sem.at[slot])

    def head_copy():
        return pltpu.make_async_copy(hbuf, xe_hbm.at[pl.ds(0, N_EXPERTS)], hsem)

    x = x_ref[...]
    rank = rank_ref[...]
    sel = selw_ref[...] > 0.0

    j8 = lax.broadcasted_iota(jnp.int32, (CHUNK, RB), 0).astype(F32)
    heads, tails, flags = [], [], []
    for e in range(N_EXPERTS):
        start = seg_start_ref[rb, e]
        end = start + seg_len_ref[rb, e]
        nonempty = seg_len_ref[rb, e] > 0
        has_head = nonempty & (start % CHUNK != 0)
        has_tail = nonempty & (end % CHUNK != 0)
        single = has_head & has_tail & (start // CHUNK == end // CHUNK)
        head_shift = (start % CHUNK).astype(F32)
        tail_rank0 = ((end // CHUNK) * CHUNK - start).astype(F32)
        r_e = rank[e:e + 1, :]
        s_e = sel[e:e + 1, :]
        heads.append(jnp.where(s_e & (r_e + head_shift == j8), has_head.astype(F32), 0.0))
        tails.append(jnp.where(s_e & (r_e - tail_rank0 == j8), (has_tail & ~single).astype(F32), 0.0))
        flags.append((has_head, has_tail, single, start // CHUNK))
    p_edge = jnp.concatenate(heads + tails, axis=0).astype(BF16)
    edge = jnp.dot(p_edge, x, preferred_element_type=F32)

    @pl.when(pending[head_slot] == 1)
    def _():
        head_copy().wait()

    head_rows = []
    for e, (has_head, has_tail, single, head_chunk) in enumerate(flags):
        rows = slice(e * CHUNK, (e + 1) * CHUNK)
        old = carry[rows, :]
        head = edge[rows, :] + old
        tail = edge[N_EXPERTS * CHUNK + e * CHUNK:N_EXPERTS * CHUNK + (e + 1) * CHUNK, :]
        head_rows.append(head)
        keep = ~single & ~has_tail & ~has_head
        carry[rows, :] = (single.astype(F32) * head + (has_tail & ~single).astype(F32) * tail
                          + keep.astype(F32) * old)
    hbuf[...] = _pack_chunks(jnp.concatenate(head_rows, axis=0))
    for e, (has_head, has_tail, single, head_chunk) in enumerate(flags):
        dst = jnp.where(has_head & ~single, e * SLOT_CHUNKS + head_chunk, HEAD_TRASH_CHUNK + e)
        pltpu.make_async_copy(hbuf.at[e], xe_hbm.at[dst], hsem).start()
    pending[head_slot] = 1

    off = _expert_column(in_off_ref, rb)
    rank0 = _expert_column(in_rank_ref, rb)
    n_in = _expert_column(in_rows_ref, rb)
    tgt = jnp.where(sel & (rank >= rank0) & (rank < rank0 + n_in), rank - rank0 + off, -1.0)

    def free_slot(slot):
        @pl.when(pending[slot] == 1)
        def _():
            tile_copy(slot).wait()

    def emit(t, slot, rows_f32):
        obuf[slot] = _pack_chunks(rows_f32)
        for i in range(CHUNKS_PER_TILE):
            pltpu.make_async_copy(obuf.at[slot, i], xe_hbm.at[dst_list[0, t * CHUNKS_PER_TILE + i]],
                                  sem.at[slot]).start()
        pending[slot] = 1

    def group_body(g, carry_):
        t = GROUP_TILES * g
        cnt = jnp.minimum(n_tiles - t, GROUP_TILES)
        for k in range(GROUP_TILES):
            @pl.when(k < cnt)
            def _(k=k):
                free_slot((t + k) % TILE_SLOTS)

        @pl.when(t % SPAN_TILES == 0)
        def _():
            _mark_span(p_scr, rb, t * TILE_ROWS, n_rows, tgt, [1.0] * N_EXPERTS, in_off_ref, in_rows_ref)

        first_row = pl.multiple_of((t % SPAN_TILES) * TILE_ROWS, TILE_ROWS)
        for n in range(1, GROUP_TILES + 1):
            @pl.when(cnt == n)
            def _(n=n):
                p = p_scr[pl.ds(first_row, n * TILE_ROWS), :]
                rows = jnp.dot(p.astype(BF16), x, preferred_element_type=F32)
                for k in range(n):
                    emit(t + k, (t + k) % TILE_SLOTS, rows[k * TILE_ROWS:(k + 1) * TILE_ROWS])

        return carry_

    lax.fori_loop(0, (n_tiles + GROUP_TILES - 1) // GROUP_TILES, group_body, 0)

    @pl.when(rb == N_RB - 1)
    def _():
        for slot in range(TILE_SLOTS):
            @pl.when(pending[slot] == 1)
            def _(slot=slot):
                tile_copy(slot).wait()
        head_copy().wait()
        zbuf[...] = jnp.zeros_like(zbuf)

        def zcopy(k):
            return pltpu.make_async_copy(zbuf, xe_hbm.at[TRASH_CHUNK + k], zsem)

        def zstart(k, c):
            zcopy(k).start()
            return c

        def zwait(k, c):
            zcopy(k).wait()
            return c

        lax.fori_loop(0, TRASH_CHUNKS, zstart, 0)
        lax.fori_loop(0, TRASH_CHUNKS, zwait, 0)


def _route_gather(tables, xn, selw_t, rank_t):
    gs = pltpu.PrefetchScalarGridSpec(
        num_scalar_prefetch=len(tables),
        grid=(N_RB,),
        in_specs=[
            pl.BlockSpec((RB, D_MODEL), lambda rb, *_: (rb, 0)),
            pl.BlockSpec((N_EXPERTS, RB), lambda rb, *_: (0, rb)),
            pl.BlockSpec((N_EXPERTS, RB), lambda rb, *_: (0, rb)),
        ],
        out_specs=pl.BlockSpec(memory_space=pl.ANY),
        scratch_shapes=[
            pltpu.VMEM((SPAN_ROWS + WIN_ROWS, RB), F32),
            pltpu.VMEM((TILE_SLOTS, CHUNKS_PER_TILE) + CHUNK_SHAPE, BF16),
            pltpu.VMEM((N_EXPERTS,) + CHUNK_SHAPE, BF16),
            pltpu.VMEM((N_EXPERTS * CHUNK, D_MODEL), F32),
            pltpu.VMEM(CHUNK_SHAPE, BF16),
            pltpu.SMEM((1, LIST_LEN), jnp.int32),
            pltpu.SMEM((TILE_SLOTS + 1,), jnp.int32),
            pltpu.SemaphoreType.DMA((TILE_SLOTS,)),
            pltpu.SemaphoreType.DMA,
            pltpu.SemaphoreType.DMA,
        ],
    )
    return pl.pallas_call(
        _gather_kernel,
        grid_spec=gs,
        out_shape=jax.ShapeDtypeStruct((TRASH_CHUNK + TRASH_CHUNKS,) + CHUNK_SHAPE, BF16),
        compiler_params=_cparams("arbitrary"),
    )(*tables, xn, selw_t, rank_t)


W_PIECES_PER_MATRIX = 4
W_PIECES = 3 * W_PIECES_PER_MATRIX
FFN_STEPS = max(SLOT_TILES, W_PIECES + 1)
WG_PIECE_ROWS = D_MODEL // W_PIECES_PER_MATRIX
WD_PIECE_ROWS = EXPERT_D_FF // W_PIECES_PER_MATRIX
FFN_VMEM_LIMIT = 56 * 1024 * 1024


def _ffn_kernel(x_ref, wg_hbm, wu_hbm, wd_hbm, o_ref,
                wg_bf, wu_bf, wd_bf, stage_up, stage_dn, sem):
    e = pl.program_id(0)
    i = pl.program_id(1)
    slot = e % 2

    matrices = ((wg_hbm, stage_up, wg_bf, WG_PIECE_ROWS),
                (wu_hbm, stage_up, wu_bf, WG_PIECE_ROWS),
                (wd_hbm, stage_dn, wd_bf, WD_PIECE_ROWS))

    def piece(expert, k, buf, finish):
        def of_kind(kind):
            w_hbm, stage, w_bf, rows = matrices[kind]
            r0 = pl.multiple_of((k - kind * W_PIECES_PER_MATRIX) * rows, rows)
            cp = pltpu.make_async_copy(w_hbm.at[expert, pl.ds(r0, rows), :], stage.at[buf], sem.at[buf])
            if finish:
                cp.wait()
                w_bf[expert % 2, pl.ds(r0, rows), :] = stage[buf].astype(BF16)
            else:
                cp.start()

        if isinstance(k, int):
            of_kind(k // W_PIECES_PER_MATRIX)
        else:
            for kind in range(len(matrices)):
                pl.when(k // W_PIECES_PER_MATRIX == kind)(functools.partial(of_kind, kind))

    @pl.when((e == 0) & (i == 0))
    def _():
        for k in range(W_PIECES):
            piece(0, k, k % 2, finish=False)
            piece(0, k, k % 2, finish=True)

    @pl.when(e + 1 < N_EXPERTS)
    def _():
        @pl.when((i >= 1) & (i <= W_PIECES))
        def _():
            piece(e + 1, i - 1, (i - 1) % 2, finish=True)

        @pl.when(i < W_PIECES)
        def _():
            piece(e + 1, i, i % 2, finish=False)

    @pl.when(i < SLOT_TILES)
    def _():
        x = _unpack_chunks(x_ref[...])
        acc = None
        for c in range(EXPERT_D_FF // FF_FC):
            gt = jnp.dot(x, wg_bf[slot, :, c * FF_FC:(c + 1) * FF_FC], preferred_element_type=F32)
            up = jnp.dot(x, wu_bf[slot, :, c * FF_FC:(c + 1) * FF_FC], preferred_element_type=F32)
            h = (gt * (1.0 / (1.0 + jnp.exp(-gt))) * up).astype(BF16)
            down = jnp.dot(h, wd_bf[slot, c * FF_FC:(c + 1) * FF_FC, :], preferred_element_type=F32)
            acc = down if acc is None else acc + down
        o_ref[...] = _pack_chunks(acc)


def _ffn(xe, w_gate, w_up, w_down):
    def rows(e, i):
        return (e * SLOT_TILES + jnp.minimum(i, SLOT_TILES - 1), 0, 0)

    tile_chunks = (FF_TM // CHUNK,) + CHUNK_SHAPE
    gs = pltpu.PrefetchScalarGridSpec(
        num_scalar_prefetch=0,
        grid=(N_EXPERTS, FFN_STEPS),
        in_specs=[
            pl.BlockSpec(tile_chunks, rows),
            pl.BlockSpec(memory_space=pl.ANY),
            pl.BlockSpec(memory_space=pl.ANY),
            pl.BlockSpec(memory_space=pl.ANY),
        ],
        out_specs=pl.BlockSpec(tile_chunks, rows),
        scratch_shapes=[
            pltpu.VMEM((2, D_MODEL, EXPERT_D_FF), BF16),
            pltpu.VMEM((2, D_MODEL, EXPERT_D_FF), BF16),
            pltpu.VMEM((2, EXPERT_D_FF, D_MODEL), BF16),
            pltpu.VMEM((2, WG_PIECE_ROWS, EXPERT_D_FF), F32),
            pltpu.VMEM((2, WD_PIECE_ROWS, D_MODEL), F32),
            pltpu.SemaphoreType.DMA((2,)),
        ],
    )
    return pl.pallas_call(
        _ffn_kernel,
        grid_spec=gs,
        out_shape=jax.ShapeDtypeStruct((N_EXPERTS * SLOT_CHUNKS,) + CHUNK_SHAPE, BF16),
        compiler_params=pltpu.CompilerParams(dimension_semantics=("arbitrary", "arbitrary"),
                                             vmem_limit_bytes=FFN_VMEM_LIMIT),
    )(xe, w_gate, w_up, w_down)


def _combine_kernel(seg_start_ref, seg_rows_ref, seg_off_ref, tot_ref, seg_shift_ref,
                    x1_ref, selw_ref, rank_ref, g_ref, ye_hbm, op_ref, os_ref,
                    w_scr, ybuf, acc, src_list, state, sem):
    rb = pl.program_id(0)
    cur = rb % 2
    nxt_rb = jnp.minimum(rb + 1, N_RB - 1)

    def tiles_of(b):
        return (tot_ref[b] + TILE_ROWS - 1) // TILE_ROWS

    n_tiles = tiles_of(rb)
    n_tiles_next = jnp.where(rb + 1 < N_RB, tiles_of(nxt_rb), 0)

    def fetch(which, t, slot):
        for i in range(CHUNKS_PER_TILE):
            pltpu.make_async_copy(ye_hbm.at[src_list[which, t * CHUNKS_PER_TILE + i]], ybuf.at[slot, i],
                                  sem.at[slot]).start()

    def tile_wait(slot):
        pltpu.make_async_copy(ye_hbm.at[pl.ds(0, CHUNKS_PER_TILE)], ybuf.at[slot], sem.at[slot]).wait()

    @pl.when(rb == 0)
    def _():
        w_scr[...] = jnp.zeros_like(w_scr)
        _fill_chunk_list(src_list, 0, 0, seg_start_ref, seg_rows_ref, lambda c: 0)
        state[0] = 0
        state[1] = 0

    @pl.when(rb + 1 < N_RB)
    def _():
        _fill_chunk_list(src_list, 1 - cur, nxt_rb, seg_start_ref, seg_rows_ref, lambda c: 0)

    slot0 = state[0]
    started = state[1]
    for j in range(LOOKAHEAD):
        @pl.when((j >= started) & (j < n_tiles))
        def _(j=j):
            fetch(cur, j, (slot0 + j) % TILE_SLOTS)

    off = _expert_column(seg_off_ref, rb) + _expert_column(seg_shift_ref, rb)
    selw = selw_ref[...]
    gates = [selw[e:e + 1, :] for e in range(N_EXPERTS)]
    tgt = jnp.where(selw > 0.0, rank_ref[...] + off, -1.0)
    acc[...] = x1_ref[...]

    def prefetch(t, next_started):
        ahead = t + LOOKAHEAD

        @pl.when(ahead < n_tiles)
        def _():
            fetch(cur, ahead, (slot0 + ahead) % TILE_SLOTS)

        start_next = (ahead >= n_tiles) & (next_started < jnp.minimum(LOOKAHEAD, n_tiles_next))

        @pl.when(start_next)
        def _():
            fetch(1 - cur, next_started, (slot0 + n_tiles + next_started) % TILE_SLOTS)

        return next_started + start_next.astype(jnp.int32)

    def add_tiles(t, n):
        w = w_scr[pl.ds(pl.multiple_of((t % SPAN_TILES) * TILE_ROWS, TILE_ROWS), n * TILE_ROWS), :]
        ys = [_unpack_chunks(ybuf[(slot0 + t + k) % TILE_SLOTS]) for k in range(n)]
        y = ys[0] if n == 1 else jnp.concatenate(ys, axis=0)
        acc[...] += lax.dot_general(w.astype(BF16), y, (((0,), (0,)), ((), ())),
                                    preferred_element_type=F32)

    def group_body(g, next_started):
        t = GROUP_TILES * g
        cnt = jnp.minimum(n_tiles - t, GROUP_TILES)
        for k in range(GROUP_TILES):
            next_started = prefetch(t + k, next_started)

        @pl.when(t % SPAN_TILES == 0)
        def _():
            _mark_span(w_scr, rb, t * TILE_ROWS, tot_ref[rb], tgt, gates, seg_off_ref, seg_rows_ref)

        for k in range(GROUP_TILES):
            @pl.when(k < cnt)
            def _(k=k):
                tile_wait((slot0 + t + k) % TILE_SLOTS)

        for n in range(1, GROUP_TILES + 1):
            @pl.when(cnt == n)
            def _(n=n):
                add_tiles(t, n)

        return next_started

    state[1] = lax.fori_loop(0, (n_tiles + GROUP_TILES - 1) // GROUP_TILES, group_body, jnp.int32(0))
    state[0] = (slot0 + n_tiles) % TILE_SLOTS
    x2 = acc[...]
    ms = jnp.mean(x2 * x2, axis=-1, keepdims=True)
    y = x2 * lax.rsqrt(ms + EPS) * g_ref[...]

    @pl.when(rb < GROUP0_BLOCKS)
    def _():
        op_ref[...] = y

    @pl.when(rb >= GROUP0_BLOCKS)
    def _():
        os_ref[...] = y


def _combine(tables, x1, selw_t, rank_t, g, ye):
    gs = pltpu.PrefetchScalarGridSpec(
        num_scalar_prefetch=len(tables),
        grid=(N_RB,),
        in_specs=[
            pl.BlockSpec((RB, D_MODEL), lambda rb, *_: (rb, 0)),
            pl.BlockSpec((N_EXPERTS, RB), lambda rb, *_: (0, rb)),
            pl.BlockSpec((N_EXPERTS, RB), lambda rb, *_: (0, rb)),
            pl.BlockSpec((1, D_MODEL), lambda rb, *_: (0, 0)),
            pl.BlockSpec(memory_space=pl.ANY),
        ],
        out_specs=[
            pl.BlockSpec((RB, D_MODEL), lambda rb, *_: (jnp.minimum(rb, GROUP0_BLOCKS - 1), 0)),
            pl.BlockSpec((RB, D_MODEL), lambda rb, *_: (jnp.maximum(rb - GROUP0_BLOCKS, 0), 0)),
        ],
        scratch_shapes=[
            pltpu.VMEM((SPAN_ROWS + WIN_ROWS, RB), F32),
            pltpu.VMEM((TILE_SLOTS, CHUNKS_PER_TILE) + CHUNK_SHAPE, BF16),
            pltpu.VMEM((RB, D_MODEL), F32),
            pltpu.SMEM((2, LIST_LEN), jnp.int32),
            pltpu.SMEM((2,), jnp.int32),
            pltpu.SemaphoreType.DMA((TILE_SLOTS,)),
        ],
    )
    return pl.pallas_call(
        _combine_kernel,
        grid_spec=gs,
        out_shape=[jax.ShapeDtypeStruct((GROUP_TOKENS[0], D_MODEL), F32),
                   jax.ShapeDtypeStruct((GROUP_TOKENS[1], D_MODEL), F32)],
        compiler_params=_cparams("arbitrary"),
    )(*tables, x1, selw_t, rank_t, g, ye)


def kernel(x_prompt, x_sample, norm_mix_g, w_in, w_fourier, sink, w_out, norm_ffn_g, w_router,
           w_gate, w_up, w_down, norm_final_g):
    assert x_prompt.shape == (1, SEQ, D_MODEL) and x_sample.shape == (N_SEQ - 1, SEQ, D_MODEL)
    assert w_in.shape == (1, D_MODEL, IN_WIDTH) and w_out.shape == (1, MIX_WIDTH, D_MODEL)
    assert w_gate.shape == (1, N_EXPERTS, D_MODEL, EXPERT_D_FF) and w_down.shape == (1, N_EXPERTS, EXPERT_D_FF, D_MODEL)
    xp = x_prompt.reshape(-1, D_MODEL)
    xs = x_sample.reshape(-1, D_MODEL)
    cos_t, sin_t = _rope_tables()

    mix_cols = np.concatenate([
        np.arange(FOURIER_WIDTH),
        FOURIER_WIDTH + (np.asarray(Q_ORDER)[:, None] * HEAD_DIM + np.arange(HEAD_DIM)[None, :]).reshape(-1)])
    in_cols = np.concatenate([mix_cols, np.arange(MIX_WIDTH, IN_WIDTH)])
    w_in_r = w_in[0][:, in_cols].astype(BF16)
    w_out_r = w_out[0][mix_cols, :].astype(BF16)

    u, q, k, v = _inproj(xp, xs, norm_mix_g[0][None, :], w_in_r, cos_t, sin_t)
    yf = _fourier(u, w_fourier[0].astype(BF16))
    ya = _attention(sink[0], q, k, v)
    x1, xn, aff_t = _outproj(xp, xs, yf, ya, w_out_r, norm_ffn_g[0][None, :],
                             w_router[0].T.astype(BF16))

    thr, need = _thresholds(aff_t)
    selw_t, rank_t, m = _select(aff_t, thr, need)

    seg_len = m[:, :, 0].astype(jnp.int32)
    seg_start = jnp.cumsum(seg_len, axis=0) - seg_len
    seg_end = seg_start + seg_len

    def offsets(rows):
        return jnp.cumsum(rows, axis=1) - rows, jnp.sum(rows, axis=1)

    in_start = -(-seg_start // CHUNK) * CHUNK
    in_rows = jnp.maximum(seg_end // CHUNK * CHUNK - in_start, 0)
    in_off, in_tot = offsets(in_rows)
    gather_tables = (in_start, in_rows, in_off, in_tot, in_start - seg_start, seg_start, seg_len)
    out_start = seg_start // CHUNK * CHUNK
    out_rows = jnp.where(seg_len > 0, -(-seg_end // CHUNK) * CHUNK - out_start, 0)
    out_off, out_tot = offsets(out_rows)
    combine_tables = (out_start, out_rows, out_off, out_tot, seg_start - out_start)

    xe = _route_gather(gather_tables, xn, selw_t, rank_t)
    ye = _ffn(xe, w_gate[0], w_up[0], w_down[0])
    yp, ys = _combine(combine_tables, x1, selw_t, rank_t, norm_final_g[None, :], ye)
    return (yp.reshape(x_prompt.shape), ys.reshape(x_sample.shape))
```
